```python
import jax, jax.numpy as jnp
from jax import lax
import numpy as np

D_MODEL = 1024
BATCH = 8
SEQ = 8192
DEPTH = 1

HEAD_DIM = 64
ATTN_Q_HEADS = 8
ATTN_KV_HEADS = 4
ATTN_GROUP = ATTN_Q_HEADS // ATTN_KV_HEADS
DILATED_PATTERNS = ((128, 1), (512, 4), (2048, 16))
ATTN_BLOCK = 128
ATTN_DIM = ATTN_Q_HEADS * HEAD_DIM
KV_DIM = ATTN_KV_HEADS * HEAD_DIM
SSM_HEADS = 16
SSM_HEAD_DIM = 64
SSM_INNER = SSM_HEADS * SSM_HEAD_DIM
SSM_GROUPS = 2
SSM_STATE = 128
SSM_CONV = 4
SSM_CHUNK = 128
BC_DIM = SSM_GROUPS * SSM_STATE
MIX_DIM = ATTN_DIM + SSM_INNER
IN_PROJ_DIM = ATTN_DIM + 2 * KV_DIM + 2 * SSM_INNER + 2 * BC_DIM + SSM_HEADS
D_FF = 2816
FFN_CONV = 3
PLE_DIM = 256
EPS = 1e-6

kernel_name = 'hybrid_dilated_attn_ssd_convffn_ple'


def rms_norm(x, g):
    xf = x.astype(jnp.float32)
    y = xf * lax.rsqrt(jnp.mean(xf * xf, axis=-1, keepdims=True) + EPS)
    return (y * g.astype(jnp.float32)).astype(x.dtype)


def causal_depthwise_conv(u, w):
    k_width, chans = w.shape
    return lax.conv_general_dilated(
        u, w[:, None, :].astype(u.dtype), window_strides=(1,), padding=[(k_width - 1, 0)],
        dimension_numbers=('NWC', 'WIO', 'NWC'), feature_group_count=chans)


def _dilated_pattern(qh, kh, vh, window, dilation):
    bsz, seq = qh.shape[:2]
    steps = window // dilation
    span = dilation * ATTN_BLOCK
    s_pad = -(-seq // span) * span
    nblk = s_pad // span

    def to_sub(t):
        t = jnp.pad(t, [(0, 0), (0, s_pad - seq)] + [(0, 0)] * (t.ndim - 2))
        t = t.reshape(bsz, s_pad // dilation, dilation, *t.shape[2:])
        t = jnp.moveaxis(t, 2, 1)
        return t.reshape(bsz, dilation, nblk, ATTN_BLOCK, *t.shape[3:])

    def from_sub(t):
        t = t.reshape(bsz, dilation, nblk * ATTN_BLOCK, *t.shape[4:])
        t = jnp.moveaxis(t, 1, 2).reshape(bsz, s_pad, *t.shape[3:])
        return t[:, :seq]

    qs, ks, vs = to_sub(qh), to_sub(kh), to_sub(vh)
    pad_blk = [(0, 0), (0, 0), (1, 0), (0, 0), (0, 0), (0, 0)]
    kk = jnp.concatenate([jnp.pad(ks, pad_blk)[:, :, :-1], ks], axis=3)
    vv = jnp.concatenate([jnp.pad(vs, pad_blk)[:, :, :-1], vs], axis=3)
    s = jnp.einsum('brnikge,brnjke->brnkgij', qs, kk)
    qi = jnp.arange(ATTN_BLOCK)[:, None]
    kj = jnp.arange(2 * ATTN_BLOCK)[None, :]
    delta = ATTN_BLOCK + qi - kj
    band = (delta >= 0) & (delta <= steps)
    valid = (jnp.arange(nblk)[:, None, None] > 0) | (kj[None] >= ATTN_BLOCK)
    mask = band[None] & valid
    s = jnp.where(mask[:, None, None], s, -jnp.inf)
    m = jnp.max(s, axis=-1)
    pexp = jnp.exp(s - m[..., None])
    l = jnp.sum(pexp, axis=-1)
    o = jnp.einsum('brnkgij,brnjke->brnikge', pexp, vv)
    m = jnp.moveaxis(m, -1, 3)
    l = jnp.moveaxis(l, -1, 3)
    return from_sub(o), from_sub(m), from_sub(l)


def dilated_attention(q, k, v, q_g, k_g):
    bsz, seq, _ = q.shape
    qh = rms_norm(q.reshape(bsz, seq, ATTN_KV_HEADS, ATTN_GROUP, HEAD_DIM), q_g).astype(jnp.float32)
    qh = qh * (HEAD_DIM ** -0.5)
    kh = rms_norm(k.reshape(bsz, seq, ATTN_KV_HEADS, HEAD_DIM), k_g).astype(jnp.float32)
    vh = v.reshape(bsz, seq, ATTN_KV_HEADS, HEAD_DIM).astype(jnp.float32)
    res = [_dilated_pattern(qh, kh, vh, w, d) for (w, d) in DILATED_PATTERNS]
    m_max = jnp.max(jnp.stack([r[1] for r in res]), axis=0)
    wts = [jnp.exp(r[1] - m_max) for r in res]
    num = sum(wi[..., None] * r[0] for wi, r in zip(wts, res))
    den = sum(wi * r[2] for wi, r in zip(wts, res))
    out = num / den[..., None]
    return out.reshape(bsz, seq, ATTN_DIM).astype(q.dtype)


def ssd_chunked(xdt, a, bm, cm):
    bsz, seq, nh, hp = xdt.shape
    nc, ln, ng, ns = seq // SSM_CHUNK, SSM_CHUNK, SSM_GROUPS, SSM_STATE
    ne = nh // ng
    xc = xdt.reshape(bsz, nc, ln, ng, ne, hp)
    ac = a.reshape(bsz, nc, ln, ng, ne)
    bc = bm.reshape(bsz, nc, ln, ng, ns)
    cc = cm.reshape(bsz, nc, ln, ng, ns)
    acum = jnp.cumsum(ac, axis=2)
    causal = jnp.tril(jnp.ones((ln, ln), dtype=bool))
    seg = acum[:, :, :, None] - acum[:, :, None]
    decay = jnp.exp(jnp.where(causal[:, :, None, None], seg, -jnp.inf))
    cb = jnp.einsum('bclgn,bcsgn->bclsg', cc, bc)
    y_diag = jnp.einsum('bclsge,bcsgep->bclgep', cb[..., None] * decay, xc)
    decay_to_end = jnp.exp(acum[:, :, -1:] - acum)
    chunk_states = jnp.einsum('bclgn,bclgep->bcgepn', bc, xc * decay_to_end[..., None])
    chunk_decay = jnp.exp(acum[:, :, -1])

    def step(h, inp):
        st, dec = inp
        return h * dec[..., None, None] + st, h

    h0 = jnp.zeros((bsz, ng, ne, hp, ns), jnp.float32)
    _, h_in = lax.scan(step, h0, (jnp.moveaxis(chunk_states, 1, 0), jnp.moveaxis(chunk_decay, 1, 0)))
    h_in = jnp.moveaxis(h_in, 0, 1)
    y_off = jnp.einsum('bclgn,bcgepn->bclgep', cc, h_in) * jnp.exp(acum)[..., None]
    return (y_diag + y_off).reshape(bsz, seq, nh, hp)


def ssd_mixer(z, xbc, dt_raw, conv_w, conv_b, dt_bias, a_log, d_skip, norm_g):
    bsz, seq, _ = z.shape
    xbc = jax.nn.silu(causal_depthwise_conv(xbc, conv_w) + conv_b)
    xs, bm, cm = jnp.split(xbc, [SSM_INNER, SSM_INNER + BC_DIM], axis=-1)
    xs = xs.reshape(bsz, seq, SSM_HEADS, SSM_HEAD_DIM).astype(jnp.float32)
    bm = bm.reshape(bsz, seq, SSM_GROUPS, SSM_STATE).astype(jnp.float32)
    cm = cm.reshape(bsz, seq, SSM_GROUPS, SSM_STATE).astype(jnp.float32)
    dt = jax.nn.softplus(dt_raw.astype(jnp.float32) + dt_bias.astype(jnp.float32))
    a = -jnp.exp(a_log.astype(jnp.float32))
    y = ssd_chunked(xs * dt[..., None], dt * a, bm, cm) + d_skip.astype(jnp.float32)[:, None] * xs
    y = y.reshape(bsz, seq, SSM_INNER) * jax.nn.silu(z.astype(jnp.float32))
    return rms_norm(y, norm_g).astype(z.dtype)


def conv_ffn(x, norm_g, w_up, conv_w, conv_b, w_down):
    u = rms_norm(x, norm_g) @ w_up
    u = causal_depthwise_conv(u, conv_w) + conv_b
    gate, val = jnp.split(u, 2, axis=-1)
    return (jax.nn.silu(gate) * val) @ w_down


def _fwd_setup_inputs(seed: int = 0) -> dict:
    key = jax.random.key(seed)
    ks = jax.random.split(key, 24)
    f32 = jnp.float32

    def nrm(k, shape, fan_in):
        return jax.random.normal(k, shape, f32) * (fan_in ** -0.5)

    def gain(k, shape):
        return 1.0 + 0.02 * jax.random.normal(k, shape, f32)

    dt0 = jnp.exp(jax.random.uniform(ks[8], (DEPTH, SSM_HEADS), f32) * (np.log(0.1) - np.log(0.001)) + np.log(0.001))
    return {
        'x': jax.random.normal(ks[0], (BATCH, SEQ, D_MODEL), f32),
        'p': jax.random.normal(ks[1], (DEPTH, BATCH, SEQ, PLE_DIM), f32),
        'attn_norm_g': gain(ks[2], (DEPTH, D_MODEL)),
        'w_in': nrm(ks[3], (DEPTH, D_MODEL, IN_PROJ_DIM), D_MODEL),
        'q_norm_g': gain(ks[4], (DEPTH, HEAD_DIM)),
        'k_norm_g': gain(ks[5], (DEPTH, HEAD_DIM)),
        'ssm_conv_w': nrm(ks[6], (DEPTH, SSM_CONV, SSM_INNER + 2 * BC_DIM), SSM_CONV),
        'ssm_conv_b': 0.02 * jax.random.normal(ks[7], (DEPTH, SSM_INNER + 2 * BC_DIM), f32),
        'dt_bias': dt0 + jnp.log(-jnp.expm1(-dt0)),
        'a_log': jnp.log(jax.random.uniform(ks[9], (DEPTH, SSM_HEADS), f32, 1.0, 16.0)),
        'd_skip': 1.0 + 0.1 * jax.random.normal(ks[10], (DEPTH, SSM_HEADS), f32),
        'ssm_norm_g': gain(ks[11], (DEPTH, SSM_INNER)),
        'w_out': nrm(ks[12], (DEPTH, MIX_DIM, D_MODEL), MIX_DIM),
        'ffn_norm_g': gain(ks[13], (DEPTH, D_MODEL)),
        'w_up': nrm(ks[14], (DEPTH, D_MODEL, 2 * D_FF), D_MODEL),
        'ffn_conv_w': nrm(ks[15], (DEPTH, FFN_CONV, 2 * D_FF), FFN_CONV),
        'ffn_conv_b': 0.02 * jax.random.normal(ks[16], (DEPTH, 2 * D_FF), f32),
        'w_down': nrm(ks[17], (DEPTH, D_FF, D_MODEL), D_FF),
        'ple_norm_g': gain(ks[18], (DEPTH, D_MODEL)),
        'w_ple_gate': nrm(ks[19], (DEPTH, D_MODEL, D_MODEL), D_MODEL),
        'w_ple_proj': nrm(ks[20], (DEPTH, PLE_DIM, D_MODEL), PLE_DIM),
    }


def _fwd_reference(x, p, attn_norm_g, w_in, q_norm_g, k_norm_g, ssm_conv_w, ssm_conv_b, dt_bias, a_log,
              d_skip, ssm_norm_g, w_out, ffn_norm_g, w_up, ffn_conv_w, ffn_conv_b, w_down,
              ple_norm_g, w_ple_gate, w_ple_proj):
    splits = [ATTN_DIM, ATTN_DIM + KV_DIM, ATTN_DIM + 2 * KV_DIM, ATTN_DIM + 2 * KV_DIM + SSM_INNER,
              ATTN_DIM + 2 * KV_DIM + 2 * SSM_INNER + 2 * BC_DIM]
    for i in range(DEPTH):
        h = rms_norm(x, attn_norm_g[i])
        proj = h @ w_in[i]
        q, k, v, z, xbc, dt_raw = jnp.split(proj, splits, axis=-1)
        attn_out = dilated_attention(q, k, v, q_norm_g[i], k_norm_g[i])
        ssm_out = ssd_mixer(z, xbc, dt_raw, ssm_conv_w[i], ssm_conv_b[i], dt_bias[i],
                            a_log[i], d_skip[i], ssm_norm_g[i])
        x = x + jnp.concatenate([attn_out, ssm_out], axis=-1) @ w_out[i]
        x = x + conv_ffn(x, ffn_norm_g[i], w_up[i], ffn_conv_w[i], ffn_conv_b[i], w_down[i])
        gate = jax.nn.sigmoid(rms_norm(x, ple_norm_g[i]) @ w_ple_gate[i])
        x = x + gate * (p[i] @ w_ple_proj[i])
    return x


import jax as _jax
import jax.numpy as _jnp

TWIN_FORMAT = 'train_step'
FWD_PARAMS = ['x', 'p', 'attn_norm_g', 'w_in', 'q_norm_g', 'k_norm_g', 'ssm_conv_w', 'ssm_conv_b', 'dt_bias', 'a_log', 'd_skip', 'ssm_norm_g', 'w_out', 'ffn_norm_g', 'w_up', 'ffn_conv_w', 'ffn_conv_b', 'w_down', 'ple_norm_g', 'w_ple_gate', 'w_ple_proj']
TWIN_WEIGHTS = ['attn_norm_g', 'w_in', 'q_norm_g', 'k_norm_g', 'ssm_conv_w', 'ssm_conv_b', 'dt_bias', 'a_log', 'd_skip', 'ssm_norm_g', 'w_out', 'ffn_norm_g', 'w_up', 'ffn_conv_w', 'ffn_conv_b', 'w_down', 'ple_norm_g', 'w_ple_gate', 'w_ple_proj']
TWIN_DIFF_INPUT = 'x'
TWIN_INPUTS = ['x', 'p', 'attn_norm_g', 'w_in', 'q_norm_g', 'k_norm_g', 'ssm_conv_w', 'ssm_conv_b', 'dt_bias', 'a_log', 'd_skip', 'ssm_norm_g', 'w_out', 'ffn_norm_g', 'w_up', 'ffn_conv_w', 'ffn_conv_b', 'w_down', 'ple_norm_g', 'w_ple_gate', 'w_ple_proj', 'loss_target', 'm_attn_norm_g', 'm_w_in', 'm_q_norm_g', 'm_k_norm_g', 'm_ssm_conv_w', 'm_ssm_conv_b', 'm_dt_bias', 'm_a_log', 'm_d_skip', 'm_ssm_norm_g', 'm_w_out', 'm_ffn_norm_g', 'm_w_up', 'm_ffn_conv_w', 'm_ffn_conv_b', 'm_w_down', 'm_ple_norm_g', 'm_w_ple_gate', 'm_w_ple_proj', 'v_attn_norm_g', 'v_w_in', 'v_q_norm_g', 'v_k_norm_g', 'v_ssm_conv_w', 'v_ssm_conv_b', 'v_dt_bias', 'v_a_log', 'v_d_skip', 'v_ssm_norm_g', 'v_w_out', 'v_ffn_norm_g', 'v_w_up', 'v_ffn_conv_w', 'v_ffn_conv_b', 'v_w_down', 'v_ple_norm_g', 'v_w_ple_gate', 'v_w_ple_proj']
TWIN_OUTPUTS = ['loss', 'grad_x', 'grad_attn_norm_g', 'grad_w_in', 'grad_q_norm_g', 'grad_k_norm_g', 'grad_ssm_conv_w', 'grad_ssm_conv_b', 'grad_dt_bias', 'grad_a_log', 'grad_d_skip', 'grad_ssm_norm_g', 'grad_w_out', 'grad_ffn_norm_g', 'grad_w_up', 'grad_ffn_conv_w', 'grad_ffn_conv_b', 'grad_w_down', 'grad_ple_norm_g', 'grad_w_ple_gate', 'grad_w_ple_proj', 'delta_attn_norm_g', 'delta_w_in', 'delta_q_norm_g', 'delta_k_norm_g', 'delta_ssm_conv_w', 'delta_ssm_conv_b', 'delta_dt_bias', 'delta_a_log', 'delta_d_skip', 'delta_ssm_norm_g', 'delta_w_out', 'delta_ffn_norm_g', 'delta_w_up', 'delta_ffn_conv_w', 'delta_ffn_conv_b', 'delta_w_down', 'delta_ple_norm_g', 'delta_w_ple_gate', 'delta_w_ple_proj', 'new_m_attn_norm_g', 'new_m_w_in', 'new_m_q_norm_g', 'new_m_k_norm_g', 'new_m_ssm_conv_w', 'new_m_ssm_conv_b', 'new_m_dt_bias', 'new_m_a_log', 'new_m_d_skip', 'new_m_ssm_norm_g', 'new_m_w_out', 'new_m_ffn_norm_g', 'new_m_w_up', 'new_m_ffn_conv_w', 'new_m_ffn_conv_b', 'new_m_w_down', 'new_m_ple_norm_g', 'new_m_w_ple_gate', 'new_m_w_ple_proj', 'new_v_attn_norm_g', 'new_v_w_in', 'new_v_q_norm_g', 'new_v_k_norm_g', 'new_v_ssm_conv_w', 'new_v_ssm_conv_b', 'new_v_dt_bias', 'new_v_a_log', 'new_v_d_skip', 'new_v_ssm_norm_g', 'new_v_w_out', 'new_v_ffn_norm_g', 'new_v_w_up', 'new_v_ffn_conv_w', 'new_v_ffn_conv_b', 'new_v_w_down', 'new_v_ple_norm_g', 'new_v_w_ple_gate', 'new_v_w_ple_proj']
TWIN_LEAF_KINDS = {'loss': 'loss', 'grad_x': 'grad_x', 'grad_attn_norm_g': 'grad_w', 'grad_w_in': 'grad_w', 'grad_q_norm_g': 'grad_w', 'grad_k_norm_g': 'grad_w', 'grad_ssm_conv_w': 'grad_w', 'grad_ssm_conv_b': 'grad_w', 'grad_dt_bias': 'grad_w', 'grad_a_log': 'grad_w', 'grad_d_skip': 'grad_w', 'grad_ssm_norm_g': 'grad_w', 'grad_w_out': 'grad_w', 'grad_ffn_norm_g': 'grad_w', 'grad_w_up': 'grad_w', 'grad_ffn_conv_w': 'grad_w', 'grad_ffn_conv_b': 'grad_w', 'grad_w_down': 'grad_w', 'grad_ple_norm_g': 'grad_w', 'grad_w_ple_gate': 'grad_w', 'grad_w_ple_proj': 'grad_w', 'delta_attn_norm_g': 'delta_w', 'delta_w_in': 'delta_w', 'delta_q_norm_g': 'delta_w', 'delta_k_norm_g': 'delta_w', 'delta_ssm_conv_w': 'delta_w', 'delta_ssm_conv_b': 'delta_w', 'delta_dt_bias': 'delta_w', 'delta_a_log': 'delta_w', 'delta_d_skip': 'delta_w', 'delta_ssm_norm_g': 'delta_w', 'delta_w_out': 'delta_w', 'delta_ffn_norm_g': 'delta_w', 'delta_w_up': 'delta_w', 'delta_ffn_conv_w': 'delta_w', 'delta_ffn_conv_b': 'delta_w', 'delta_w_down': 'delta_w', 'delta_ple_norm_g': 'delta_w', 'delta_w_ple_gate': 'delta_w', 'delta_w_ple_proj': 'delta_w', 'new_m_attn_norm_g': 'new_m', 'new_m_w_in': 'new_m', 'new_m_q_norm_g': 'new_m', 'new_m_k_norm_g': 'new_m', 'new_m_ssm_conv_w': 'new_m', 'new_m_ssm_conv_b': 'new_m', 'new_m_dt_bias': 'new_m', 'new_m_a_log': 'new_m', 'new_m_d_skip': 'new_m', 'new_m_ssm_norm_g': 'new_m', 'new_m_w_out': 'new_m', 'new_m_ffn_norm_g': 'new_m', 'new_m_w_up': 'new_m', 'new_m_ffn_conv_w': 'new_m', 'new_m_ffn_conv_b': 'new_m', 'new_m_w_down': 'new_m', 'new_m_ple_norm_g': 'new_m', 'new_m_w_ple_gate': 'new_m', 'new_m_w_ple_proj': 'new_m', 'new_v_attn_norm_g': 'new_v', 'new_v_w_in': 'new_v', 'new_v_q_norm_g': 'new_v', 'new_v_k_norm_g': 'new_v', 'new_v_ssm_conv_w': 'new_v', 'new_v_ssm_conv_b': 'new_v', 'new_v_dt_bias': 'new_v', 'new_v_a_log': 'new_v', 'new_v_d_skip': 'new_v', 'new_v_ssm_norm_g': 'new_v', 'new_v_w_out': 'new_v', 'new_v_ffn_norm_g': 'new_v', 'new_v_w_up': 'new_v', 'new_v_ffn_conv_w': 'new_v', 'new_v_ffn_conv_b': 'new_v', 'new_v_w_down': 'new_v', 'new_v_ple_norm_g': 'new_v', 'new_v_w_ple_gate': 'new_v', 'new_v_w_ple_proj': 'new_v'}


def _forward(args):
    return _fwd_reference(*[args[k] for k in FWD_PARAMS])


def _output_shape():
    def fwd():
        inp = _fwd_setup_inputs(0)
        return _fwd_reference(*[inp[k] for k in FWD_PARAMS])
    out = _jax.eval_shape(fwd)
    return out.shape, out.dtype

N_MICROBATCH = 1
ADAM_LR = 0.001
ADAM_B1 = 0.9
ADAM_B2 = 0.999
ADAM_EPS = 1e-08
ADAM_WD = 0.01
ADAM_STEP = 10
PER_EXAMPLE_BATCH_AXIS = {'x': 0, 'p': 1, 'loss_target': 0}
SHARED_INPUTS = []
_WEIGHT_DTYPES = {'attn_norm_g': _jnp.float32, 'w_in': _jnp.float32, 'q_norm_g': _jnp.float32, 'k_norm_g': _jnp.float32, 'ssm_conv_w': _jnp.float32, 'ssm_conv_b': _jnp.float32, 'dt_bias': _jnp.float32, 'a_log': _jnp.float32, 'd_skip': _jnp.float32, 'ssm_norm_g': _jnp.float32, 'w_out': _jnp.float32, 'ffn_norm_g': _jnp.float32, 'w_up': _jnp.float32, 'ffn_conv_w': _jnp.float32, 'ffn_conv_b': _jnp.float32, 'w_down': _jnp.float32, 'ple_norm_g': _jnp.float32, 'w_ple_gate': _jnp.float32, 'w_ple_proj': _jnp.float32}
MOMENT_SCALE = {'attn_norm_g': 1.118310e+00, 'w_in': 5.099481e-01, 'q_norm_g': 2.344894e+00, 'k_norm_g': 2.332499e+00, 'ssm_conv_w': 1.908591e+00, 'ssm_conv_b': 6.570828e+00, 'dt_bias': 3.479151e+00, 'a_log': 1.154163e+01, 'd_skip': 2.232605e+01, 'ssm_norm_g': 6.300172e+01, 'w_out': 3.760238e+00, 'ffn_norm_g': 5.374446e+01, 'w_up': 1.183496e+00, 'ffn_conv_w': 7.806241e+00, 'ffn_conv_b': 7.041670e+00, 'w_down': 7.730726e-01, 'ple_norm_g': 2.087715e+00, 'w_ple_gate': 6.733966e-01, 'w_ple_proj': 8.781108e-01}


def _to_microbatches(a, axis):
    t = _jnp.moveaxis(a, axis, 0)
    t = t.reshape((N_MICROBATCH, t.shape[0] // N_MICROBATCH) + t.shape[1:])
    return _jnp.moveaxis(t, 1, axis + 1)


def setup_inputs(seed: int = 0) -> dict:
    inp = _fwd_setup_inputs(seed)
    key = _jax.random.fold_in(_jax.random.key(seed), 7919)
    shape, _ = _output_shape()
    out = dict(inp)
    out["loss_target"] = _jax.random.normal(_jax.random.fold_in(key, 0), shape, _jnp.float32)
    for i, name in enumerate(TWIN_WEIGHTS):
        w = inp[name].astype(_jnp.float32)
        if MOMENT_SCALE is None:
            s = _jnp.sqrt(_jnp.mean(_jnp.square(w)) + 1e-30)
        else:
            s = MOMENT_SCALE[name]
        km, kv = _jax.random.split(_jax.random.fold_in(key, i + 1))
        out[name] = w
        out["m_" + name] = s * _jax.random.normal(km, w.shape, _jnp.float32)
        out["v_" + name] = (s * s) * _jax.random.uniform(kv, w.shape, _jnp.float32, 0.5, 1.5)
    if N_MICROBATCH > 1:
        for name, axis in PER_EXAMPLE_BATCH_AXIS.items():
            out[name] = _to_microbatches(out[name], axis)
    return {'x': out['x'], 'p': out['p'], 'attn_norm_g': out['attn_norm_g'], 'w_in': out['w_in'], 'q_norm_g': out['q_norm_g'], 'k_norm_g': out['k_norm_g'], 'ssm_conv_w': out['ssm_conv_w'], 'ssm_conv_b': out['ssm_conv_b'], 'dt_bias': out['dt_bias'], 'a_log': out['a_log'], 'd_skip': out['d_skip'], 'ssm_norm_g': out['ssm_norm_g'], 'w_out': out['w_out'], 'ffn_norm_g': out['ffn_norm_g'], 'w_up': out['w_up'], 'ffn_conv_w': out['ffn_conv_w'], 'ffn_conv_b': out['ffn_conv_b'], 'w_down': out['w_down'], 'ple_norm_g': out['ple_norm_g'], 'w_ple_gate': out['w_ple_gate'], 'w_ple_proj': out['w_ple_proj'], 'loss_target': out['loss_target'], 'm_attn_norm_g': out['m_attn_norm_g'], 'm_w_in': out['m_w_in'], 'm_q_norm_g': out['m_q_norm_g'], 'm_k_norm_g': out['m_k_norm_g'], 'm_ssm_conv_w': out['m_ssm_conv_w'], 'm_ssm_conv_b': out['m_ssm_conv_b'], 'm_dt_bias': out['m_dt_bias'], 'm_a_log': out['m_a_log'], 'm_d_skip': out['m_d_skip'], 'm_ssm_norm_g': out['m_ssm_norm_g'], 'm_w_out': out['m_w_out'], 'm_ffn_norm_g': out['m_ffn_norm_g'], 'm_w_up': out['m_w_up'], 'm_ffn_conv_w': out['m_ffn_conv_w'], 'm_ffn_conv_b': out['m_ffn_conv_b'], 'm_w_down': out['m_w_down'], 'm_ple_norm_g': out['m_ple_norm_g'], 'm_w_ple_gate': out['m_w_ple_gate'], 'm_w_ple_proj': out['m_w_ple_proj'], 'v_attn_norm_g': out['v_attn_norm_g'], 'v_w_in': out['v_w_in'], 'v_q_norm_g': out['v_q_norm_g'], 'v_k_norm_g': out['v_k_norm_g'], 'v_ssm_conv_w': out['v_ssm_conv_w'], 'v_ssm_conv_b': out['v_ssm_conv_b'], 'v_dt_bias': out['v_dt_bias'], 'v_a_log': out['v_a_log'], 'v_d_skip': out['v_d_skip'], 'v_ssm_norm_g': out['v_ssm_norm_g'], 'v_w_out': out['v_w_out'], 'v_ffn_norm_g': out['v_ffn_norm_g'], 'v_w_up': out['v_w_up'], 'v_ffn_conv_w': out['v_ffn_conv_w'], 'v_ffn_conv_b': out['v_ffn_conv_b'], 'v_w_down': out['v_w_down'], 'v_ple_norm_g': out['v_ple_norm_g'], 'v_w_ple_gate': out['v_w_ple_gate'], 'v_w_ple_proj': out['v_w_ple_proj']}


def _loss(weights, diff, rest, loss_target):
    with _jax.named_scope("forward"):
        args = {**rest, TWIN_DIFF_INPUT: diff, **{k: w.astype(_WEIGHT_DTYPES[k]) for k, w in weights.items()}}
        y = _forward(args)
    with _jax.named_scope("loss_head"):
        err = _jnp.square(y.astype(_jnp.float32) - loss_target)
        return 0.5 * _jnp.sum(_jnp.mean(err, axis=-1)) if err.ndim else 0.5 * err


def _adamw(w, g, m, v):
    m = ADAM_B1 * m + (1.0 - ADAM_B1) * g
    v = ADAM_B2 * v + (1.0 - ADAM_B2) * _jnp.square(g)
    m_hat = m / (1.0 - ADAM_B1 ** ADAM_STEP)
    v_hat = v / (1.0 - ADAM_B2 ** ADAM_STEP)
    delta = -ADAM_LR * (m_hat / (_jnp.sqrt(v_hat) + ADAM_EPS) + ADAM_WD * w)
    return delta, m, v


def reference(x, p, attn_norm_g, w_in, q_norm_g, k_norm_g, ssm_conv_w, ssm_conv_b, dt_bias, a_log, d_skip, ssm_norm_g, w_out, ffn_norm_g, w_up, ffn_conv_w, ffn_conv_b, w_down, ple_norm_g, w_ple_gate, w_ple_proj, loss_target, m_attn_norm_g, m_w_in, m_q_norm_g, m_k_norm_g, m_ssm_conv_w, m_ssm_conv_b, m_dt_bias, m_a_log, m_d_skip, m_ssm_norm_g, m_w_out, m_ffn_norm_g, m_w_up, m_ffn_conv_w, m_ffn_conv_b, m_w_down, m_ple_norm_g, m_w_ple_gate, m_w_ple_proj, v_attn_norm_g, v_w_in, v_q_norm_g, v_k_norm_g, v_ssm_conv_w, v_ssm_conv_b, v_dt_bias, v_a_log, v_d_skip, v_ssm_norm_g, v_w_out, v_ffn_norm_g, v_w_up, v_ffn_conv_w, v_ffn_conv_b, v_w_down, v_ple_norm_g, v_w_ple_gate, v_w_ple_proj):
    given = dict(x=x, p=p, attn_norm_g=attn_norm_g, w_in=w_in, q_norm_g=q_norm_g, k_norm_g=k_norm_g, ssm_conv_w=ssm_conv_w, ssm_conv_b=ssm_conv_b, dt_bias=dt_bias, a_log=a_log, d_skip=d_skip, ssm_norm_g=ssm_norm_g, w_out=w_out, ffn_norm_g=ffn_norm_g, w_up=w_up, ffn_conv_w=ffn_conv_w, ffn_conv_b=ffn_conv_b, w_down=w_down, ple_norm_g=ple_norm_g, w_ple_gate=w_ple_gate, w_ple_proj=w_ple_proj, loss_target=loss_target, m_attn_norm_g=m_attn_norm_g, m_w_in=m_w_in, m_q_norm_g=m_q_norm_g, m_k_norm_g=m_k_norm_g, m_ssm_conv_w=m_ssm_conv_w, m_ssm_conv_b=m_ssm_conv_b, m_dt_bias=m_dt_bias, m_a_log=m_a_log, m_d_skip=m_d_skip, m_ssm_norm_g=m_ssm_norm_g, m_w_out=m_w_out, m_ffn_norm_g=m_ffn_norm_g, m_w_up=m_w_up, m_ffn_conv_w=m_ffn_conv_w, m_ffn_conv_b=m_ffn_conv_b, m_w_down=m_w_down, m_ple_norm_g=m_ple_norm_g, m_w_ple_gate=m_w_ple_gate, m_w_ple_proj=m_w_ple_proj, v_attn_norm_g=v_attn_norm_g, v_w_in=v_w_in, v_q_norm_g=v_q_norm_g, v_k_norm_g=v_k_norm_g, v_ssm_conv_w=v_ssm_conv_w, v_ssm_conv_b=v_ssm_conv_b, v_dt_bias=v_dt_bias, v_a_log=v_a_log, v_d_skip=v_d_skip, v_ssm_norm_g=v_ssm_norm_g, v_w_out=v_w_out, v_ffn_norm_g=v_ffn_norm_g, v_w_up=v_w_up, v_ffn_conv_w=v_ffn_conv_w, v_ffn_conv_b=v_ffn_conv_b, v_w_down=v_w_down, v_ple_norm_g=v_ple_norm_g, v_w_ple_gate=v_w_ple_gate, v_w_ple_proj=v_w_ple_proj)
    weights = {n: given[n] for n in TWIN_WEIGHTS}
    shared = {n: given[n] for n in SHARED_INPUTS}
    per_example = {n: given[n] for n in ['x', 'p']}
    grad_fn = _jax.value_and_grad(_loss, argnums=(0, 1))

    def one_microbatch(ex, loss_target):
        ex = dict(ex)
        diff = ex.pop(TWIN_DIFF_INPUT)
        return grad_fn(weights, diff, {**shared, **ex}, loss_target)

    if N_MICROBATCH == 1:
        loss, (grad_w, grad_x) = one_microbatch(per_example, given["loss_target"])
    else:
        def body(carry, xs):
            loss_sum, grad_sum = carry
            l_k, (gw_k, gx_k) = one_microbatch(xs[0], xs[1])
            with _jax.named_scope("update"):
                return (loss_sum + l_k, _jax.tree.map(_jnp.add, grad_sum, gw_k)), gx_k

        init = (_jnp.zeros((), _jnp.float32), _jax.tree.map(_jnp.zeros_like, weights))
        (loss, grad_w), grad_x = _jax.lax.scan(body, init, (per_example, given["loss_target"]))
    with _jax.named_scope("update"):
        delta_w, new_m, new_v = {}, {}, {}
        for n in TWIN_WEIGHTS:
            delta_w[n], new_m[n], new_v[n] = _adamw(weights[n], grad_w[n], given["m_" + n], given["v_" + n])
    return (loss, grad_x, *[grad_w[n] for n in TWIN_WEIGHTS], *[delta_w[n] for n in TWIN_WEIGHTS],
            *[new_m[n] for n in TWIN_WEIGHTS], *[new_v[n] for n in TWIN_WEIGHTS])
```

```python
import functools

import jax
import jax.numpy as jnp
from jax import lax
from jax.experimental import pallas as pl
from jax.experimental.pallas import tpu as pltpu

F32 = jnp.float32
BF16 = jnp.bfloat16

D_MODEL = 1024
HEAD_DIM = 64
ATTN_DIM = 512
KV_DIM = 256
N_KV = 4
SSM_INNER = 1024
SSM_HEADS = 16
SSM_STATE = 128
BC_DIM = 256
XBC_DIM = SSM_INNER + 2 * BC_DIM
MIX_DIM = ATTN_DIM + SSM_INNER
IN_PROJ = 3600
D_FF = 2816
PLE_DIM = 256
CHUNK = 128
DILATIONS = (1, 4, 16)
EPS = 1e-6
ADAM_LR, ADAM_B1, ADAM_B2, ADAM_EPS, ADAM_WD, ADAM_STEP = 0.001, 0.9, 0.999, 1e-08, 0.01, 10

PROJ_P = 3712
OFF_XBC, OFF_Q, OFF_Z, OFF_K, OFF_V, OFF_DT = 0, 1536, 2048, 3072, 3328, 3584
LANE = 128
VMEM_LIMIT = 48 * 1024 * 1024
NEG = -1e30


def _cparams(sem):
    return pltpu.CompilerParams(dimension_semantics=sem, vmem_limit_bytes=VMEM_LIMIT)


def _sigmoid(x):
    return 1.0 / (1.0 + jnp.exp(-x))


def _dot(a, b):
    return jnp.dot(a, b, preferred_element_type=F32)


def _dot_nt(a, b):
    return lax.dot_general(a, b, (((1,), (1,)), ((), ())), preferred_element_type=F32)


def _dot_tn(a, b):
    return lax.dot_general(a, b, (((0,), (0,)), ((), ())), preferred_element_type=F32)


def _dot_split(x, m):
    hi = x.astype(BF16)
    lo = (x - hi.astype(F32)).astype(BF16)
    return _dot(hi, m) + _dot(lo, m)


def _rows(name, body, ins, outs, accs=(), tm=512):
    t_rows = next(s[1].shape[0] for s in ins if s[0] in ("t", "tc"))
    tm = min(tm, t_rows)
    in_specs, args = [], []
    for s in ins:
        if s[0] == "t":
            in_specs.append(pl.BlockSpec((tm, s[1].shape[1]), lambda i: (i, 0)))
        elif s[0] == "tc":
            in_specs.append(pl.BlockSpec((tm, s[2]), functools.partial(lambda i, c: (i, c), c=s[3])))
        else:
            in_specs.append(pl.BlockSpec(s[1].shape, lambda i: (0, 0)))
        args.append(s[1])
    out_shape = [jax.ShapeDtypeStruct((t_rows, w), dt) for w, dt in outs]
    out_specs = [pl.BlockSpec((tm, w), lambda i: (i, 0)) for w, _ in outs]
    out_shape += [jax.ShapeDtypeStruct(a, F32) for a in accs]
    out_specs += [pl.BlockSpec(a, lambda i: (0, 0)) for a in accs]
    n_acc = len(accs)

    def kern(*refs):
        if n_acc:
            @pl.when(pl.program_id(0) == 0)
            def _():
                for r in refs[len(refs) - n_acc:]:
                    r[...] = jnp.zeros(r.shape, F32)
        body(*refs)

    return pl.pallas_call(
        kern, name=name, grid=(t_rows // tm,), in_specs=in_specs, out_specs=out_specs, out_shape=out_shape,
        compiler_params=_cparams(("arbitrary",) if n_acc else ("parallel",)))(*args)


NCHUNK = 512


def _col_chunks(n):
    return [(c, min(NCHUNK, n - c)) for c in range(0, n, NCHUNK)]


def _mm_nn(name, pairs, out_dtype, res=None, tm=512, tn=None):
    m, n = pairs[0][0].shape[0], pairs[0][1].shape[1]
    tn = n if tn is None else tn
    tm = min(tm, m)
    np_ = len(pairs)
    in_specs, args = [], []
    for a, w in pairs:
        in_specs += [pl.BlockSpec((tm, a.shape[1]), lambda j, i: (i, 0)), pl.BlockSpec((w.shape[0], tn), lambda j, i: (0, j))]
        args += [a, w]
    if res is not None:
        in_specs.append(pl.BlockSpec((tm, tn), lambda j, i: (i, j)))
        args.append(res)

    def kern(*refs):
        o_ref = refs[-1]
        for c0, cw in _col_chunks(tn):
            acc = None
            for q in range(np_):
                d = _dot(refs[2 * q][...], refs[2 * q + 1][:, c0:c0 + cw])
                acc = d if acc is None else acc + d
            if res is not None:
                acc = acc + refs[2 * np_][:, c0:c0 + cw]
            o_ref[:, c0:c0 + cw] = acc.astype(o_ref.dtype)

    return pl.pallas_call(
        kern, name=name, grid=(n // tn, m // tm), in_specs=in_specs,
        out_specs=pl.BlockSpec((tm, tn), lambda j, i: (i, j)),
        out_shape=jax.ShapeDtypeStruct((m, n), out_dtype), compiler_params=_cparams(("parallel", "parallel")))(*args)


def _mm_nt(name, pairs, out_dtype, tm=512, tn=None):
    m, n = pairs[0][0].shape[0], pairs[0][1].shape[0]
    tn = n if tn is None else tn
    tm = min(tm, m)
    np_ = len(pairs)
    in_specs, args = [], []
    for a, w in pairs:
        in_specs += [pl.BlockSpec((tm, a.shape[1]), lambda j, i: (i, 0)), pl.BlockSpec((tn, w.shape[1]), lambda j, i: (j, 0))]
        args += [a, w]

    def kern(*refs):
        o_ref = refs[-1]
        for c0, cw in _col_chunks(tn):
            acc = None
            for q in range(np_):
                d = _dot_nt(refs[2 * q][...], refs[2 * q + 1][c0:c0 + cw, :])
                acc = d if acc is None else acc + d
            o_ref[:, c0:c0 + cw] = acc.astype(o_ref.dtype)

    return pl.pallas_call(
        kern, name=name, grid=(n // tn, m // tm), in_specs=in_specs,
        out_specs=pl.BlockSpec((tm, tn), lambda j, i: (i, j)),
        out_shape=jax.ShapeDtypeStruct((m, n), out_dtype), compiler_params=_cparams(("parallel", "parallel")))(*args)


def _mm_tn(name, a, b, tm=None, tn=None, tk=1024):
    t, m = a.shape
    n = b.shape[1]
    tm = m if tm is None else tm
    tn = n if tn is None else tn
    tk = min(tk, t)

    def kern(a_ref, b_ref, o_ref):
        @pl.when(pl.program_id(2) == 0)
        def _():
            o_ref[...] = jnp.zeros(o_ref.shape, F32)
        for c0, cw in _col_chunks(tn):
            o_ref[:, c0:c0 + cw] += _dot_tn(a_ref[...], b_ref[:, c0:c0 + cw])

    return pl.pallas_call(
        kern, name=name, grid=(m // tm, n // tn, t // tk),
        in_specs=[pl.BlockSpec((tk, tm), lambda i, j, k: (k, i)), pl.BlockSpec((tk, tn), lambda i, j, k: (k, j))],
        out_specs=pl.BlockSpec((tm, tn), lambda i, j, k: (i, j)),
        out_shape=jax.ShapeDtypeStruct((m, n), F32),
        compiler_params=_cparams(("parallel", "parallel", "arbitrary")))(a, b)


def _rms_fwd(name, x, g):
    def body(x_ref, g_ref, h_ref):
        xv = x_ref[...]
        r = lax.rsqrt(jnp.mean(xv * xv, axis=-1, keepdims=True) + EPS)
        h_ref[...] = (xv * r * g_ref[...]).astype(BF16)
    return _rows(name, body, [("t", x), ("p", g)], [(x.shape[1], BF16)])[0]


def _rms_bwd(name, dh, x, g, dres):
    d = x.shape[1]

    def body(dh_ref, x_ref, g_ref, dres_ref, dx_ref, dxb_ref, dg_ref):
        xv, dhv = x_ref[...], dh_ref[...]
        r = lax.rsqrt(jnp.mean(xv * xv, axis=-1, keepdims=True) + EPS)
        gd = dhv * g_ref[...]
        dx = dres_ref[...] + r * gd - xv * (r * r * r * jnp.mean(xv * gd, axis=-1, keepdims=True))
        dx_ref[...] = dx
        dxb_ref[...] = dx.astype(BF16)
        dg_ref[...] += jnp.sum(dhv * xv * r, axis=0, keepdims=True)
    return _rows(name, body, [("t", dh), ("t", x), ("p", g), ("t", dres)], [(d, F32), (d, BF16)], accs=[(1, d)])


def _head_mean_matrix(width):
    i = jnp.arange(width) // HEAD_DIM
    return jnp.where(i[:, None] == i[None, :], 1.0 / HEAD_DIM, 0.0).astype(BF16)


def _qknorm_fwd(proj, gq_t, gk_t):
    bq, bk = _head_mean_matrix(ATTN_DIM), _head_mean_matrix(KV_DIM)
    scale = HEAD_DIM ** -0.5

    def body(q_ref, k_ref, v_ref, gq_ref, gk_ref, bq_ref, bk_ref, qn_ref, kn_ref, vb_ref):
        q, k = q_ref[...], k_ref[...]
        rq = lax.rsqrt(_dot_split(q * q, bq_ref[...]) + EPS)
        rk = lax.rsqrt(_dot_split(k * k, bk_ref[...]) + EPS)
        qn_ref[...] = ((q * rq * gq_ref[...]) * scale).astype(BF16)
        kn_ref[...] = (k * rk * gk_ref[...]).astype(BF16)
        vb_ref[...] = v_ref[...].astype(BF16)

    return _rows("qknorm_fwd", body,
                 [("tc", proj, ATTN_DIM, OFF_Q // ATTN_DIM), ("tc", proj, KV_DIM, OFF_K // KV_DIM),
                  ("tc", proj, KV_DIM, OFF_V // KV_DIM), ("p", gq_t), ("p", gk_t), ("p", bq), ("p", bk)],
                 [(ATTN_DIM, BF16), (KV_DIM, BF16), (KV_DIM, BF16)])


def _qknorm_bwd(proj, gq_t, gk_t, dqs, dks, dvs):
    bq, bk = _head_mean_matrix(ATTN_DIM), _head_mean_matrix(KV_DIM)
    scale = HEAD_DIM ** -0.5

    def body(q_ref, k_ref, gq_ref, gk_ref, bq_ref, bk_ref, dq1, dq2, dq3, dk1, dk2, dk3, dv1, dv2, dv3,
             dq_ref, dk_ref, dv_ref, dgq_ref, dgk_ref):
        q, k = q_ref[...], k_ref[...]
        dqn = (dq1[...] + dq2[...] + dq3[...]) * scale
        dkn = dk1[...] + dk2[...] + dk3[...]
        rq = lax.rsqrt(_dot_split(q * q, bq_ref[...]) + EPS)
        rk = lax.rsqrt(_dot_split(k * k, bk_ref[...]) + EPS)
        gdq, gdk = dqn * gq_ref[...], dkn * gk_ref[...]
        dq_ref[...] = (rq * gdq - q * (rq * rq * rq * _dot_split(q * gdq, bq_ref[...]))).astype(BF16)
        dk_ref[...] = (rk * gdk - k * (rk * rk * rk * _dot_split(k * gdk, bk_ref[...]))).astype(BF16)
        dv_ref[...] = (dv1[...] + dv2[...] + dv3[...]).astype(BF16)
        dgq_ref[...] += jnp.sum(dqn * q * rq, axis=0, keepdims=True)
        dgk_ref[...] += jnp.sum(dkn * k * rk, axis=0, keepdims=True)

    ins = [("tc", proj, ATTN_DIM, OFF_Q // ATTN_DIM), ("tc", proj, KV_DIM, OFF_K // KV_DIM),
           ("p", gq_t), ("p", gk_t), ("p", bq), ("p", bk)]
    ins += [("t", a) for a in dqs] + [("t", a) for a in dks] + [("t", a) for a in dvs]
    return _rows("qknorm_bwd", body, ins, [(ATTN_DIM, BF16), (KV_DIM, BF16), (KV_DIM, BF16)],
                 accs=[(1, ATTN_DIM), (1, KV_DIM)], tm=256)


def _head_cols(x, kh, width=HEAD_DIM):
    return x[:, kh * width:(kh + 1) * width]


def _stack_q(x, kh):
    return jnp.concatenate([x[:, 2 * kh * HEAD_DIM:(2 * kh + 1) * HEAD_DIM],
                            x[:, (2 * kh + 1) * HEAD_DIM:(2 * kh + 2) * HEAD_DIM]], axis=0)


def _stat_cols(stat, kh, rows):
    return jnp.concatenate([stat[:, 2 * kh:2 * kh + 1], stat[:, 2 * kh + 1:2 * kh + 2]], axis=0)


def _sub_view(a, dil):
    t, c = a.shape
    return a.reshape(t // (CHUNK * dil), CHUNK, dil * c)


def _attn_fwd(qn, kn, vb, dil):
    t = qn.shape[0]
    nblk = t // (CHUNK * dil)

    def kern(q_ref, kp_ref, kc_ref, vp_ref, vc_ref, o_ref, lse_ref):
        n = pl.program_id(1)
        q, kp, kc, vp, vc = q_ref[...], kp_ref[...], kc_ref[...], vp_ref[...], vc_ref[...]
        ri = lax.broadcasted_iota(jnp.int32, (2 * CHUNK, 2 * CHUNK), 0) % CHUNK
        cj = lax.broadcasted_iota(jnp.int32, (2 * CHUNK, 2 * CHUNK), 1)
        mask = (cj - ri >= 0) & (cj - ri <= CHUNK) & ((n > 0) | (cj >= CHUNK))
        lane = lax.broadcasted_iota(jnp.int32, (CHUNK, LANE), 1)
        outs, lse_tile = [], jnp.zeros((CHUNK, LANE), F32)
        for kh in range(N_KV):
            q2 = _stack_q(q, kh)
            k2 = jnp.concatenate([_head_cols(kp, kh), _head_cols(kc, kh)], axis=0)
            v2 = jnp.concatenate([_head_cols(vp, kh), _head_cols(vc, kh)], axis=0)
            s = jnp.where(mask, _dot_nt(q2, k2), NEG)
            m = jnp.max(s, axis=1, keepdims=True)
            p = jnp.exp(s - m)
            l = jnp.sum(p, axis=1, keepdims=True)
            o = _dot(p.astype(BF16), v2) / l
            lse = m + jnp.log(l)
            outs += [o[:CHUNK], o[CHUNK:]]
            lse_tile = jnp.where(lane == 2 * kh, lse[:CHUNK], lse_tile)
            lse_tile = jnp.where(lane == 2 * kh + 1, lse[CHUNK:], lse_tile)
        o_ref[...] = jnp.concatenate(outs, axis=1)
        lse_ref[...] = lse_tile

    cur = lambda r, n: (n, 0, r)
    prev = lambda r, n: (jnp.maximum(n - 1, 0), 0, r)
    o, lse = pl.pallas_call(
        kern, name=f"attn_fwd_d{dil}", grid=(dil, nblk),
        in_specs=[pl.BlockSpec((None, CHUNK, ATTN_DIM), cur),
                  pl.BlockSpec((None, CHUNK, KV_DIM), prev), pl.BlockSpec((None, CHUNK, KV_DIM), cur),
                  pl.BlockSpec((None, CHUNK, KV_DIM), prev), pl.BlockSpec((None, CHUNK, KV_DIM), cur)],
        out_specs=[pl.BlockSpec((None, CHUNK, ATTN_DIM), cur), pl.BlockSpec((None, CHUNK, LANE), cur)],
        out_shape=[jax.ShapeDtypeStruct((nblk, CHUNK, dil * ATTN_DIM), F32),
                   jax.ShapeDtypeStruct((nblk, CHUNK, dil * LANE), F32)],
        compiler_params=_cparams(("parallel", "parallel")),
    )(_sub_view(qn, dil), _sub_view(kn, dil), _sub_view(kn, dil), _sub_view(vb, dil), _sub_view(vb, dil))
    return o.reshape(t, ATTN_DIM), lse.reshape(t, LANE)


def _head_expand_matrix():
    return (jnp.arange(LANE)[:, None] == (jnp.arange(ATTN_DIM)[None, :] // HEAD_DIM)).astype(BF16)


def _attn_merge(os_, lses):
    def body(o1, o2, o3, l1, l2, l3, e_ref, out_ref, lse_ref):
        a, b, c = l1[...], l2[...], l3[...]
        m = jnp.maximum(jnp.maximum(a, b), c)
        tot = m + jnp.log(jnp.exp(a - m) + jnp.exp(b - m) + jnp.exp(c - m))
        e = e_ref[...]
        out = (_dot_split(jnp.exp(a - tot), e) * o1[...] + _dot_split(jnp.exp(b - tot), e) * o2[...]
               + _dot_split(jnp.exp(c - tot), e) * o3[...])
        out_ref[...] = out.astype(BF16)
        lse_ref[...] = tot
    ins = [("t", o) for o in os_] + [("t", l) for l in lses] + [("p", _head_expand_matrix())]
    return _rows("attn_merge", body, ins, [(ATTN_DIM, BF16), (LANE, F32)], tm=256)


def _attn_bwd_prep(dmix, attn_out):
    et = _head_expand_matrix().T

    def body(do_ref, o_ref, et_ref, dob_ref, d_ref):
        do = do_ref[...]
        dob_ref[...] = do.astype(BF16)
        d_ref[...] = _dot_split(do * o_ref[...].astype(F32), et_ref[...])
    return _rows("attn_bwd_prep", body, [("tc", dmix, ATTN_DIM, SSM_INNER // ATTN_DIM), ("t", attn_out), ("p", et)],
                 [(ATTN_DIM, BF16), (LANE, F32)])


def _attn_dq(qn, kn, vb, dob, lse, dsum, dil):
    t = qn.shape[0]
    nblk = t // (CHUNK * dil)

    def kern(q_ref, kp_ref, kc_ref, vp_ref, vc_ref, do_ref, lse_ref, d_ref, dq_ref):
        n = pl.program_id(1)
        q, kp, kc, vp, vc, do = q_ref[...], kp_ref[...], kc_ref[...], vp_ref[...], vc_ref[...], do_ref[...]
        lse_t, d_t = lse_ref[...], d_ref[...]
        ri = lax.broadcasted_iota(jnp.int32, (2 * CHUNK, 2 * CHUNK), 0) % CHUNK
        cj = lax.broadcasted_iota(jnp.int32, (2 * CHUNK, 2 * CHUNK), 1)
        mask = (cj - ri >= 0) & (cj - ri <= CHUNK) & ((n > 0) | (cj >= CHUNK))
        outs = []
        for kh in range(N_KV):
            q2, do2 = _stack_q(q, kh), _stack_q(do, kh)
            k2 = jnp.concatenate([_head_cols(kp, kh), _head_cols(kc, kh)], axis=0)
            v2 = jnp.concatenate([_head_cols(vp, kh), _head_cols(vc, kh)], axis=0)
            p = jnp.where(mask, jnp.exp(jnp.where(mask, _dot_nt(q2, k2), NEG) - _stat_cols(lse_t, kh, CHUNK)), 0.0)
            ds = p * (_dot_nt(do2, v2) - _stat_cols(d_t, kh, CHUNK))
            dq2 = _dot(ds.astype(BF16), k2)
            outs += [dq2[:CHUNK], dq2[CHUNK:]]
        dq_ref[...] = jnp.concatenate(outs, axis=1)

    cur = lambda r, n: (n, 0, r)
    prev = lambda r, n: (jnp.maximum(n - 1, 0), 0, r)
    dq = pl.pallas_call(
        kern, name=f"attn_dq_d{dil}", grid=(dil, nblk),
        in_specs=[pl.BlockSpec((None, CHUNK, ATTN_DIM), cur),
                  pl.BlockSpec((None, CHUNK, KV_DIM), prev), pl.BlockSpec((None, CHUNK, KV_DIM), cur),
                  pl.BlockSpec((None, CHUNK, KV_DIM), prev), pl.BlockSpec((None, CHUNK, KV_DIM), cur),
                  pl.BlockSpec((None, CHUNK, ATTN_DIM), cur),
                  pl.BlockSpec((None, CHUNK, LANE), cur), pl.BlockSpec((None, CHUNK, LANE), cur)],
        out_specs=pl.BlockSpec((None, CHUNK, ATTN_DIM), cur),
        out_shape=jax.ShapeDtypeStruct((nblk, CHUNK, dil * ATTN_DIM), F32),
        compiler_params=_cparams(("parallel", "parallel")),
    )(_sub_view(qn, dil), _sub_view(kn, dil), _sub_view(kn, dil), _sub_view(vb, dil), _sub_view(vb, dil),
      _sub_view(dob, dil), _sub_view(lse, dil), _sub_view(dsum, dil))
    return dq.reshape(t, ATTN_DIM)


def _attn_dkv(qn, kn, vb, dob, lse, dsum, dil):
    t = qn.shape[0]
    nblk = t // (CHUNK * dil)

    def kern(k_ref, v_ref, qc_ref, qn_ref, doc_ref, don_ref, lc_ref, ln_ref, dc_ref, dn_ref, dk_ref, dv_ref):
        n = pl.program_id(1)
        k, v = k_ref[...], v_ref[...]
        qc, qx, doc, dox = qc_ref[...], qn_ref[...], doc_ref[...], don_ref[...]
        lc, lx, dc, dx = lc_ref[...], ln_ref[...], dc_ref[...], dn_ref[...]
        ri = lax.broadcasted_iota(jnp.int32, (4 * CHUNK, CHUNK), 0) % (2 * CHUNK)
        cj = lax.broadcasted_iota(jnp.int32, (4 * CHUNK, CHUNK), 1)
        mask = (ri - cj >= 0) & (ri - cj <= CHUNK) & ((n < nblk - 1) | (ri < CHUNK))
        dks, dvs = [], []
        for kh in range(N_KV):
            def rows4(cur, nxt, w=HEAD_DIM):
                return jnp.concatenate([cur[:, 2 * kh * w:(2 * kh + 1) * w], nxt[:, 2 * kh * w:(2 * kh + 1) * w],
                                        cur[:, (2 * kh + 1) * w:(2 * kh + 2) * w], nxt[:, (2 * kh + 1) * w:(2 * kh + 2) * w]],
                                       axis=0)
            q4, do4 = rows4(qc, qx), rows4(doc, dox)
            lse4, d4 = rows4(lc, lx, 1), rows4(dc, dx, 1)
            kk, vv = _head_cols(k, kh), _head_cols(v, kh)
            p = jnp.where(mask, jnp.exp(jnp.where(mask, _dot_nt(q4, kk), NEG) - lse4), 0.0)
            dvs.append(_dot_tn(p.astype(BF16), do4))
            ds = p * (_dot_nt(do4, vv) - d4)
            dks.append(_dot_tn(ds.astype(BF16), q4))
        dk_ref[...] = jnp.concatenate(dks, axis=1)
        dv_ref[...] = jnp.concatenate(dvs, axis=1)

    cur = lambda r, n: (n, 0, r)
    nxt = lambda r, n: (jnp.minimum(n + 1, nblk - 1), 0, r)
    qv, dv_, lv, sv = _sub_view(qn, dil), _sub_view(dob, dil), _sub_view(lse, dil), _sub_view(dsum, dil)
    dk, dv = pl.pallas_call(
        kern, name=f"attn_dkv_d{dil}", grid=(dil, nblk),
        in_specs=[pl.BlockSpec((None, CHUNK, KV_DIM), cur), pl.BlockSpec((None, CHUNK, KV_DIM), cur),
                  pl.BlockSpec((None, CHUNK, ATTN_DIM), cur), pl.BlockSpec((None, CHUNK, ATTN_DIM), nxt),
                  pl.BlockSpec((None, CHUNK, ATTN_DIM), cur), pl.BlockSpec((None, CHUNK, ATTN_DIM), nxt),
                  pl.BlockSpec((None, CHUNK, LANE), cur), pl.BlockSpec((None, CHUNK, LANE), nxt),
                  pl.BlockSpec((None, CHUNK, LANE), cur), pl.BlockSpec((None, CHUNK, LANE), nxt)],
        out_specs=[pl.BlockSpec((None, CHUNK, KV_DIM), cur), pl.BlockSpec((None, CHUNK, KV_DIM), cur)],
        out_shape=[jax.ShapeDtypeStruct((nblk, CHUNK, dil * KV_DIM), F32)] * 2,
        compiler_params=_cparams(("parallel", "parallel")),
    )(_sub_view(kn, dil), _sub_view(vb, dil), qv, qv, dv_, dv_, lv, lv, sv, sv)
    return dk.reshape(t, KV_DIM), dv.reshape(t, KV_DIM)


HALO = 8
CONV_TM = 1024


def _fill_causal(buf, tail, x_ref, first):
    @pl.when(first)
    def _():
        tail[...] = jnp.zeros(tail.shape, F32)
    buf[0:HALO, :] = tail[...]
    buf[HALO:, :] = x_ref[...].astype(F32)
    tail[...] = buf[buf.shape[0] - HALO:, :]


def _causal_taps(buf, w, taps, tm):
    acc = None
    for k in range(taps):
        term = w[k:k + 1, :] * buf[pl.ds(HALO - (taps - 1) + k, tm), :]
        acc = term if acc is None else acc + term
    return acc


def _silu_grad(pre):
    sg = _sigmoid(pre)
    return sg * (1.0 + pre * (1.0 - sg))


def _ssm_conv_fwd(proj, w, b):
    t = proj.shape[0]
    tm = min(CONV_TM, t)
    taps = w.shape[0]

    def kern(x_ref, w_ref, b_ref, o_ref, buf, tail):
        _fill_causal(buf, tail, x_ref, pl.program_id(1) == 0)
        pre = _causal_taps(buf, w_ref[...], taps, tm) + b_ref[...]
        o_ref[...] = pre * _sigmoid(pre)

    return pl.pallas_call(
        kern, name="ssm_conv_fwd", grid=(XBC_DIM // LANE, t // tm),
        in_specs=[pl.BlockSpec((tm, LANE), lambda c, i: (i, c)), pl.BlockSpec((taps, LANE), lambda c, i: (0, c)),
                  pl.BlockSpec((1, LANE), lambda c, i: (0, c))],
        out_specs=pl.BlockSpec((tm, LANE), lambda c, i: (i, c)),
        out_shape=jax.ShapeDtypeStruct((t, XBC_DIM), F32),
        scratch_shapes=[pltpu.VMEM((tm + HALO, LANE), F32), pltpu.VMEM((HALO, LANE), F32)],
        compiler_params=_cparams(("parallel", "arbitrary")))(proj, w, b)


def _ssm_conv_bwd_pre(proj, w, b, dact):
    t = proj.shape[0]
    tm = min(CONV_TM, t)
    taps = w.shape[0]

    def kern(x_ref, w_ref, b_ref, d_ref, dpre_ref, gw_ref, gb_ref, buf, tail):
        first = pl.program_id(1) == 0
        _fill_causal(buf, tail, x_ref, first)

        @pl.when(first)
        def _():
            gw_ref[...] = jnp.zeros(gw_ref.shape, F32)
            gb_ref[...] = jnp.zeros(gb_ref.shape, F32)
        pre = _causal_taps(buf, w_ref[...], taps, tm) + b_ref[...]
        dpre = d_ref[...] * _silu_grad(pre)
        dpre_ref[...] = dpre
        gb_ref[...] += jnp.sum(dpre, axis=0, keepdims=True)
        for k in range(taps):
            gw_ref[k:k + 1, :] += jnp.sum(dpre * buf[pl.ds(HALO - (taps - 1) + k, tm), :], axis=0, keepdims=True)

    return pl.pallas_call(
        kern, name="ssm_conv_bwd_pre", grid=(XBC_DIM // LANE, t // tm),
        in_specs=[pl.BlockSpec((tm, LANE), lambda c, i: (i, c)), pl.BlockSpec((taps, LANE), lambda c, i: (0, c)),
                  pl.BlockSpec((1, LANE), lambda c, i: (0, c)), pl.BlockSpec((tm, LANE), lambda c, i: (i, c))],
        out_specs=[pl.BlockSpec((tm, LANE), lambda c, i: (i, c)), pl.BlockSpec((taps, LANE), lambda c, i: (0, c)),
                   pl.BlockSpec((1, LANE), lambda c, i: (0, c))],
        out_shape=[jax.ShapeDtypeStruct((t, XBC_DIM), F32), jax.ShapeDtypeStruct((taps, XBC_DIM), F32),
                   jax.ShapeDtypeStruct((1, XBC_DIM), F32)],
        scratch_shapes=[pltpu.VMEM((tm + HALO, LANE), F32), pltpu.VMEM((HALO, LANE), F32)],
        compiler_params=_cparams(("parallel", "arbitrary")))(proj, w, b, dact)


def _conv_bwd_input(name, parts, w, out_dtype):
    t = parts[0].shape[0]
    tm = min(CONV_TM, t)
    taps = w.shape[0]
    nt = t // tm
    widths = [p.shape[1] // LANE for p in parts]
    starts = [sum(widths[:i]) for i in range(len(parts))]
    ncol = sum(widths)

    def kern(*refs):
        d_refs, w_ref, o_ref, buf, head = refs[:len(parts)], refs[len(parts)], refs[len(parts) + 1], refs[-2], refs[-1]
        c = pl.program_id(0)

        @pl.when(pl.program_id(1) == 0)
        def _():
            head[...] = jnp.zeros(head.shape, F32)
        d = d_refs[0][...]
        for q in range(1, len(parts)):
            d = jnp.where(c >= starts[q], d_refs[q][...], d)
        buf[0:tm, :] = d
        buf[tm:, :] = head[...]
        head[...] = buf[0:HALO, :]
        wv = w_ref[...]
        acc = None
        for k in range(taps):
            term = wv[k:k + 1, :] * buf[pl.ds((taps - 1) - k, tm), :]
            acc = term if acc is None else acc + term
        o_ref[...] = acc.astype(out_dtype)

    def part_map(q):
        return lambda c, i: (nt - 1 - i, jnp.clip(c - starts[q], 0, widths[q] - 1))

    in_specs = [pl.BlockSpec((tm, LANE), part_map(q)) for q in range(len(parts))]
    in_specs.append(pl.BlockSpec((taps, LANE), lambda c, i: (0, c)))
    return pl.pallas_call(
        kern, name=name, grid=(ncol, nt), in_specs=in_specs,
        out_specs=pl.BlockSpec((tm, LANE), lambda c, i: (nt - 1 - i, c)),
        out_shape=jax.ShapeDtypeStruct((t, ncol * LANE), out_dtype),
        scratch_shapes=[pltpu.VMEM((tm + HALO, LANE), F32), pltpu.VMEM((HALO, LANE), F32)],
        compiler_params=_cparams(("parallel", "arbitrary")))(*parts, w)


def _ffn_act_fwd(u, w, b):
    t = u.shape[0]
    tm = min(CONV_TM, t)
    taps = w.shape[0]
    nc = D_FF // LANE

    def kern(ug_ref, uv_ref, wg_ref, wv_ref, bg_ref, bv_ref, a_ref, bufg, bufv, tailg, tailv):
        first = pl.program_id(1) == 0
        _fill_causal(bufg, tailg, ug_ref, first)
        _fill_causal(bufv, tailv, uv_ref, first)
        g = _causal_taps(bufg, wg_ref[...], taps, tm) + bg_ref[...]
        v = _causal_taps(bufv, wv_ref[...], taps, tm) + bv_ref[...]
        a_ref[...] = (g * _sigmoid(g) * v).astype(BF16)

    blk = lambda off: pl.BlockSpec((tm, LANE), functools.partial(lambda c, i, off: (i, c + off), off=off))
    par = lambda rows, off: pl.BlockSpec((rows, LANE), functools.partial(lambda c, i, off: (0, c + off), off=off))
    return pl.pallas_call(
        kern, name="ffn_act_fwd", grid=(nc, t // tm),
        in_specs=[blk(0), blk(nc), par(taps, 0), par(taps, nc), par(1, 0), par(1, nc)],
        out_specs=pl.BlockSpec((tm, LANE), lambda c, i: (i, c)),
        out_shape=jax.ShapeDtypeStruct((t, D_FF), BF16),
        scratch_shapes=[pltpu.VMEM((tm + HALO, LANE), F32)] * 2 + [pltpu.VMEM((HALO, LANE), F32)] * 2,
        compiler_params=_cparams(("parallel", "arbitrary")))(u, u, w, w, b, b)


def _ffn_act_bwd(u, w, b, da):
    t = u.shape[0]
    tm = min(CONV_TM, t)
    taps = w.shape[0]
    nc = D_FF // LANE

    def kern(ug_ref, uv_ref, wg_ref, wv_ref, bg_ref, bv_ref, da_ref, dg_ref, dv_ref, gwg_ref, gwv_ref, gbg_ref, gbv_ref,
             bufg, bufv, tailg, tailv):
        first = pl.program_id(1) == 0
        _fill_causal(bufg, tailg, ug_ref, first)
        _fill_causal(bufv, tailv, uv_ref, first)

        @pl.when(first)
        def _():
            for r in (gwg_ref, gwv_ref, gbg_ref, gbv_ref):
                r[...] = jnp.zeros(r.shape, F32)
        g = _causal_taps(bufg, wg_ref[...], taps, tm) + bg_ref[...]
        v = _causal_taps(bufv, wv_ref[...], taps, tm) + bv_ref[...]
        dav = da_ref[...]
        dg = dav * v * _silu_grad(g)
        dv = dav * (g * _sigmoid(g))
        dg_ref[...] = dg
        dv_ref[...] = dv
        gbg_ref[...] += jnp.sum(dg, axis=0, keepdims=True)
        gbv_ref[...] += jnp.sum(dv, axis=0, keepdims=True)
        for k in range(taps):
            sl = pl.ds(HALO - (taps - 1) + k, tm)
            gwg_ref[k:k + 1, :] += jnp.sum(dg * bufg[sl, :], axis=0, keepdims=True)
            gwv_ref[k:k + 1, :] += jnp.sum(dv * bufv[sl, :], axis=0, keepdims=True)

    blk = lambda off: pl.BlockSpec((tm, LANE), functools.partial(lambda c, i, off: (i, c + off), off=off))
    par = lambda rows, off: pl.BlockSpec((rows, LANE), functools.partial(lambda c, i, off: (0, c + off), off=off))
    return pl.pallas_call(
        kern, name="ffn_act_bwd", grid=(nc, t // tm),
        in_specs=[blk(0), blk(nc), par(taps, 0), par(taps, nc), par(1, 0), par(1, nc), blk(0)],
        out_specs=[blk(0), blk(0), par(taps, 0), par(taps, 0), par(1, 0), par(1, 0)],
        out_shape=[jax.ShapeDtypeStruct((t, D_FF), F32)] * 2 + [jax.ShapeDtypeStruct((taps, D_FF), F32)] * 2
        + [jax.ShapeDtypeStruct((1, D_FF), F32)] * 2,
        scratch_shapes=[pltpu.VMEM((tm + HALO, LANE), F32)] * 2 + [pltpu.VMEM((HALO, LANE), F32)] * 2,
        compiler_params=_cparams(("parallel", "arbitrary")))(u, u, w, w, b, b, da)


def _softplus(x):
    e = jnp.exp(-jnp.abs(x))
    return jnp.maximum(x, 0.0) + jnp.where(e < 1e-4, e - 0.5 * e * e, jnp.log(1.0 + e))


def _tri(lower):
    r = lax.broadcasted_iota(jnp.int32, (CHUNK, CHUNK), 0)
    c = lax.broadcasted_iota(jnp.int32, (CHUNK, CHUNK), 1)
    return (r >= c) if lower else (r <= c)


def _cum(mat_bool, x):
    return jnp.dot(mat_bool.astype(F32), x, precision=lax.Precision.HIGHEST, preferred_element_type=F32)


def _pair_sel(lane_lo, tile, h0):
    return jnp.where(lane_lo, tile[:, h0:h0 + 1], tile[:, h0 + 1:h0 + 2])


def _ssd_fwd(xbc_act, proj, dt_bias_p, a_log_p, dskip_t):
    t = xbc_act.shape[0]
    nch = t // CHUNK

    def kern(xa_ref, dtr_ref, bias_ref, alog_ref, dsk_ref, y_ref, dt_ref, hs_ref, hst):
        @pl.when(pl.program_id(0) == 0)
        def _():
            hst[...] = jnp.zeros(hst.shape, F32)
        dt = _softplus(dtr_ref[...] + bias_ref[...])
        dt_ref[...] = dt
        acum = _cum(_tri(True), dt * (-jnp.exp(alog_ref[...])))
        acum_t = acum.T
        ea = jnp.exp(acum)
        a_last = acum[CHUNK - 1:CHUNK, :]
        dend = jnp.exp(a_last - acum)
        ea_last = jnp.exp(a_last)
        causal = _tri(True)
        lane_lo = lax.broadcasted_iota(jnp.int32, (CHUNK, LANE), 1) < HEAD_DIM
        row_lo = lax.broadcasted_iota(jnp.int32, (CHUNK, LANE), 0) < HEAD_DIM
        for g in range(2):
            bg = xa_ref[:, SSM_INNER + g * SSM_STATE:SSM_INNER + (g + 1) * SSM_STATE].astype(BF16)
            cg = xa_ref[:, SSM_INNER + BC_DIM + g * SSM_STATE:SSM_INNER + BC_DIM + (g + 1) * SSM_STATE].astype(BF16)
            cb = _dot_nt(cg, bg)
            for j in range(4 * g, 4 * g + 4):
                h0 = 2 * j
                cols = slice(j * LANE, (j + 1) * LANE)
                xp = xa_ref[:, cols]
                xdt = xp * _pair_sel(lane_lo, dt, h0)
                ydiag = None
                for hh, sel in ((h0, lane_lo), (h0 + 1, ~lane_lo)):
                    seg = acum[:, hh:hh + 1] - acum_t[hh:hh + 1, :]
                    mm = (cb * jnp.where(causal, jnp.exp(jnp.minimum(seg, 0.0)), 0.0)).astype(BF16)
                    d = _dot(mm, jnp.where(sel, xdt, 0.0).astype(BF16))
                    ydiag = d if ydiag is None else ydiag + d
                hp = hst[cols, :]
                hs_ref[cols, :] = hp
                yoff = _dot_nt(cg, hp.astype(BF16)) * _pair_sel(lane_lo, ea, h0)
                y_ref[:, cols] = ydiag + yoff + dsk_ref[:, cols] * xp
                xw = (xdt * _pair_sel(lane_lo, dend, h0)).astype(BF16)
                rowf = jnp.where(row_lo, ea_last[:, h0:h0 + 1], ea_last[:, h0 + 1:h0 + 2])
                hst[cols, :] = hp * rowf + _dot_tn(xw, bg)

    return pl.pallas_call(
        kern, name="ssd_fwd", grid=(nch,),
        in_specs=[pl.BlockSpec((CHUNK, XBC_DIM), lambda c: (c, 0)), pl.BlockSpec((CHUNK, LANE), lambda c: (c, OFF_DT // LANE)),
                  pl.BlockSpec((1, LANE), lambda c: (0, 0)), pl.BlockSpec((1, LANE), lambda c: (0, 0)),
                  pl.BlockSpec((1, SSM_INNER), lambda c: (0, 0))],
        out_specs=[pl.BlockSpec((CHUNK, SSM_INNER), lambda c: (c, 0)), pl.BlockSpec((CHUNK, LANE), lambda c: (c, 0)),
                   pl.BlockSpec((None, SSM_INNER, SSM_STATE), lambda c: (c, 0, 0))],
        out_shape=[jax.ShapeDtypeStruct((t, SSM_INNER), F32), jax.ShapeDtypeStruct((t, LANE), F32),
                   jax.ShapeDtypeStruct((nch, SSM_INNER, SSM_STATE), F32)],
        scratch_shapes=[pltpu.VMEM((SSM_INNER, SSM_STATE), F32)],
        compiler_params=_cparams(("arbitrary",)))(xbc_act, proj, dt_bias_p, a_log_p, dskip_t)


def _ssd_bwd(xbc_act, proj, dt_sp, hstates, dy, dt_bias_p, a_log_p, dskip_t):
    t = xbc_act.shape[0]
    nch = t // CHUNK

    def kern(xa_ref, dtr_ref, dt_ref, hs_ref, dy_ref, bias_ref, alog_ref, dsk_ref,
             dact_ref, ddtr_ref, da_ref, dbias_ref, ddsk_ref, dh):
        @pl.when(pl.program_id(0) == 0)
        def _():
            dh[...] = jnp.zeros(dh.shape, F32)
            for r in (da_ref, dbias_ref, ddsk_ref):
                r[...] = jnp.zeros(r.shape, F32)
        dt = dt_ref[...]
        a_neg = -jnp.exp(alog_ref[...])
        acum = _cum(_tri(True), dt * a_neg)
        acum_t = acum.T
        ea = jnp.exp(acum)
        a_last = acum[CHUNK - 1:CHUNK, :]
        dend = jnp.exp(a_last - acum)
        ea_last = jnp.exp(a_last)
        causal = _tri(True)
        lane = lax.broadcasted_iota(jnp.int32, (CHUNK, LANE), 1)
        rowi = lax.broadcasted_iota(jnp.int32, (CHUNK, LANE), 0)
        lane_lo, row_lo, last_row = lane < HEAD_DIM, rowi < HEAD_DIM, rowi == CHUNK - 1
        d_dt = jnp.zeros((CHUNK, LANE), F32)
        d_acum = jnp.zeros((CHUNK, LANE), F32)
        d_acum_t = jnp.zeros((CHUNK, LANE), F32)

        def half_sums(v):
            lo = jnp.sum(jnp.where(lane_lo, v, 0.0), axis=1, keepdims=True)
            return lo, jnp.sum(v, axis=1, keepdims=True) - lo

        for g in range(2):
            bcols = slice(SSM_INNER + g * SSM_STATE, SSM_INNER + (g + 1) * SSM_STATE)
            ccols = slice(SSM_INNER + BC_DIM + g * SSM_STATE, SSM_INNER + BC_DIM + (g + 1) * SSM_STATE)
            bg, cg = xa_ref[:, bcols].astype(BF16), xa_ref[:, ccols].astype(BF16)
            cb = _dot_nt(cg, bg)
            dg_sum = jnp.zeros((CHUNK, CHUNK), F32)
            dcg = jnp.zeros((CHUNK, SSM_STATE), F32)
            dbg = jnp.zeros((CHUNK, SSM_STATE), F32)
            for j in range(4 * g, 4 * g + 4):
                h0 = 2 * j
                cols = slice(j * LANE, (j + 1) * LANE)
                xp, dyp = xa_ref[:, cols], dy_ref[:, cols]
                dtsel = _pair_sel(lane_lo, dt, h0)
                xdt = xp * dtsel
                xdt_b = xdt.astype(BF16)
                hp, dhp = hs_ref[cols, :], dh[cols, :]
                hp_b, dhp_b = hp.astype(BF16), dhp.astype(BF16)
                easel, dendsel = _pair_sel(lane_lo, ea, h0), _pair_sel(lane_lo, dend, h0)
                dx = None
                for hh, sel in ((h0, lane_lo), (h0 + 1, ~lane_lo)):
                    dyh = jnp.where(sel, dyp, 0.0).astype(BF16)
                    seg = acum[:, hh:hh + 1] - acum_t[hh:hh + 1, :]
                    dec = jnp.where(causal, jnp.exp(jnp.minimum(seg, 0.0)), 0.0)
                    mm = cb * dec
                    e = _dot_nt(dyh, xdt_b)
                    d = _dot_tn(mm.astype(BF16), dyh)
                    dx = d if dx is None else dx + d
                    dg_sum = dg_sum + dec * e
                    qm = mm * e
                    d_acum = d_acum + jnp.where(lane == hh, jnp.sum(qm, axis=1, keepdims=True), 0.0)
                    d_acum_t = d_acum_t - jnp.where(rowi == hh, jnp.sum(qm, axis=0, keepdims=True), 0.0)
                g2 = _dot_nt(bg, dhp_b)
                dx = dx + g2 * dendsel
                yoff = _dot_nt(cg, hp_b) * easel
                t_lo, t_hi = half_sums(xdt * g2 * dendsel)
                y_lo, y_hi = half_sums(dyp * yoff)
                hh_prod = dhp * hp
                s_lo = jnp.sum(jnp.where(row_lo, hh_prod, 0.0), keepdims=True).reshape(1, 1)
                s_hi = jnp.sum(hh_prod, keepdims=True).reshape(1, 1) - s_lo
                end_lo = ea_last[:, h0:h0 + 1] * s_lo + jnp.sum(t_lo, axis=0, keepdims=True)
                end_hi = ea_last[:, h0 + 1:h0 + 2] * s_hi + jnp.sum(t_hi, axis=0, keepdims=True)
                d_acum = d_acum + jnp.where(lane == h0, y_lo - t_lo + jnp.where(last_row, end_lo, 0.0), 0.0)
                d_acum = d_acum + jnp.where(lane == h0 + 1, y_hi - t_hi + jnp.where(last_row, end_hi, 0.0), 0.0)
                dye = (dyp * easel).astype(BF16)
                dcg = dcg + _dot(dye, hp_b)
                dbg = dbg + _dot((xdt * dendsel).astype(BF16), dhp_b)
                rowf = jnp.where(row_lo, ea_last[:, h0:h0 + 1], ea_last[:, h0 + 1:h0 + 2])
                dh[cols, :] = dhp * rowf + _dot_tn(dye, cg)
                dact_ref[:, cols] = dx * dtsel + dsk_ref[:, cols] * dyp
                x_lo, x_hi = half_sums(dx * xp)
                d_dt = d_dt + jnp.where(lane == h0, x_lo, 0.0) + jnp.where(lane == h0 + 1, x_hi, 0.0)
                ddsk_ref[:, cols] += jnp.sum(dyp * xp, axis=0, keepdims=True)
            dg_b = dg_sum.astype(BF16)
            dact_ref[:, ccols] = dcg + _dot(dg_b, bg)
            dact_ref[:, bcols] = dbg + _dot_tn(dg_b, cg)
        d_adt = _cum(_tri(False), d_acum + d_acum_t.T)
        d_dt = d_dt + d_adt * a_neg
        da_ref[...] += jnp.sum(d_adt * dt, axis=0, keepdims=True)
        d_raw = jnp.where(lane < SSM_HEADS, d_dt * _sigmoid(dtr_ref[...] + bias_ref[...]), 0.0)
        ddtr_ref[...] = d_raw.astype(BF16)
        dbias_ref[...] += jnp.sum(d_raw, axis=0, keepdims=True)

    rev = lambda c: (nch - 1 - c, 0)
    return pl.pallas_call(
        kern, name="ssd_bwd", grid=(nch,),
        in_specs=[pl.BlockSpec((CHUNK, XBC_DIM), rev), pl.BlockSpec((CHUNK, LANE), lambda c: (nch - 1 - c, OFF_DT // LANE)),
                  pl.BlockSpec((CHUNK, LANE), rev), pl.BlockSpec((None, SSM_INNER, SSM_STATE), lambda c: (nch - 1 - c, 0, 0)),
                  pl.BlockSpec((CHUNK, SSM_INNER), rev),
                  pl.BlockSpec((1, LANE), lambda c: (0, 0)), pl.BlockSpec((1, LANE), lambda c: (0, 0)),
                  pl.BlockSpec((1, SSM_INNER), lambda c: (0, 0))],
        out_specs=[pl.BlockSpec((CHUNK, XBC_DIM), rev), pl.BlockSpec((CHUNK, LANE), rev),
                   pl.BlockSpec((1, LANE), lambda c: (0, 0)), pl.BlockSpec((1, LANE), lambda c: (0, 0)),
                   pl.BlockSpec((1, SSM_INNER), lambda c: (0, 0))],
        out_shape=[jax.ShapeDtypeStruct((t, XBC_DIM), F32), jax.ShapeDtypeStruct((t, LANE), BF16),
                   jax.ShapeDtypeStruct((1, LANE), F32), jax.ShapeDtypeStruct((1, LANE), F32),
                   jax.ShapeDtypeStruct((1, SSM_INNER), F32)],
        scratch_shapes=[pltpu.VMEM((SSM_INNER, SSM_STATE), F32)],
        compiler_params=_cparams(("arbitrary",)))(xbc_act, proj, dt_sp, hstates, dy, dt_bias_p, a_log_p, dskip_t)


def _ssm_post_fwd(y, proj, g):
    def body(y_ref, z_ref, g_ref, o_ref):
        z = z_ref[...]
        yz = y_ref[...] * (z * _sigmoid(z))
        r = lax.rsqrt(jnp.mean(yz * yz, axis=-1, keepdims=True) + EPS)
        o_ref[...] = (yz * r * g_ref[...]).astype(BF16)
    return _rows("ssm_post_fwd", body, [("t", y), ("tc", proj, SSM_INNER, OFF_Z // SSM_INNER), ("p", g)],
                 [(SSM_INNER, BF16)])[0]


def _ssm_post_bwd(dmix, y, proj, g):
    def body(do_ref, y_ref, z_ref, g_ref, dy_ref, dz_ref, dg_ref):
        z, yv, dout = z_ref[...], y_ref[...], do_ref[...]
        sg = _sigmoid(z)
        gz = z * sg
        yz = yv * gz
        r = lax.rsqrt(jnp.mean(yz * yz, axis=-1, keepdims=True) + EPS)
        gd = dout * g_ref[...]
        dyz = r * gd - yz * (r * r * r * jnp.mean(yz * gd, axis=-1, keepdims=True))
        dy_ref[...] = dyz * gz
        dz_ref[...] = (dyz * yv * (sg * (1.0 + z * (1.0 - sg)))).astype(BF16)
        dg_ref[...] += jnp.sum(dout * yz * r, axis=0, keepdims=True)
    return _rows("ssm_post_bwd", body,
                 [("tc", dmix, SSM_INNER, 0), ("t", y), ("tc", proj, SSM_INNER, OFF_Z // SSM_INNER), ("p", g)],
                 [(SSM_INNER, F32), (SSM_INNER, BF16)], accs=[(1, SSM_INNER)])


def _ple_loss(gl, pp, x2, tgt):
    d = x2.shape[1]

    def body(gl_ref, pp_ref, x_ref, t_ref, dy_ref, dgl_ref, dpp_ref, sq_ref):
        s = _sigmoid(gl_ref[...])
        ppv = pp_ref[...]
        diff = x_ref[...] + s * ppv - t_ref[...]
        dy = diff * (1.0 / d)
        dy_ref[...] = dy
        dgl_ref[...] = (dy * ppv * s * (1.0 - s)).astype(BF16)
        dpp_ref[...] = (dy * s).astype(BF16)
        sq_ref[...] += jnp.sum(diff * diff, axis=0, keepdims=True)
    return _rows("ple_loss", body, [("t", gl), ("t", pp), ("t", x2), ("t", tgt)], [(d, F32), (d, BF16), (d, BF16)],
                 accs=[(1, d)])


def _pad_lanes(v, width=LANE):
    return jnp.pad(v, ((0, 0), (0, width - v.shape[1])))


def _local_step(x, p, tgt, wts):
    g_attn, g_ssm, g_ffn, g_ple = wts["attn_norm_g"], wts["ssm_norm_g"], wts["ffn_norm_g"], wts["ple_norm_g"]
    w_in_p, w_out_s, w_out_a = wts["w_in_p"], wts["w_out_ssm"], wts["w_out_attn"]
    w_up, w_down, w_gate, w_proj = wts["w_up"], wts["w_down"], wts["w_ple_gate"], wts["w_ple_proj"]
    gq_t = jnp.tile(wts["q_norm_g"], (1, ATTN_DIM // HEAD_DIM))
    gk_t = jnp.tile(wts["k_norm_g"], (1, KV_DIM // HEAD_DIM))
    dt_bias_p, a_log_p = _pad_lanes(wts["dt_bias"]), _pad_lanes(wts["a_log"])
    dskip_t = jnp.repeat(wts["d_skip"], HEAD_DIM, axis=1)

    h1 = _rms_fwd("rms_attn", x, g_attn)
    proj = _mm_nn("in_proj", [(h1, w_in_p)], F32)
    qn, kn, vb = _qknorm_fwd(proj, gq_t, gk_t)
    pats = [_attn_fwd(qn, kn, vb, d) for d in DILATIONS]
    attn_out, lse = _attn_merge([o for o, _ in pats], [l for _, l in pats])
    xbc_act = _ssm_conv_fwd(proj, wts["ssm_conv_w"], wts["ssm_conv_b"])
    y_ssd, dt_sp, hstates = _ssd_fwd(xbc_act, proj, dt_bias_p, a_log_p, dskip_t)
    ssm_out = _ssm_post_fwd(y_ssd, proj, g_ssm)
    x1 = _mm_nn("out_proj", [(ssm_out, w_out_s), (attn_out, w_out_a)], F32, res=x)
    h2 = _rms_fwd("rms_ffn", x1, g_ffn)
    u = _mm_nn("ffn_up", [(h2, w_up)], F32, tn=1408)
    a = _ffn_act_fwd(u, wts["ffn_conv_w"], wts["ffn_conv_b"])
    x2 = _mm_nn("ffn_down", [(a, w_down)], F32, res=x1)
    h3 = _rms_fwd("rms_ple", x2, g_ple)
    gl = _mm_nn("ple_gate", [(h3, w_gate)], F32)
    pb = p.astype(BF16)
    pp = _mm_nn("ple_proj", [(pb, w_proj)], F32)
    dy, dgl, dpp, sq = _ple_loss(gl, pp, x2, tgt)

    grads = {}
    grads["w_ple_proj"] = _mm_tn("g_ple_proj", pb, dpp)
    grads["w_ple_gate"] = _mm_tn("g_ple_gate", h3, dgl)
    dh3 = _mm_nt("d_h3", [(dgl, w_gate)], F32)
    dx2, dx2b, grads["ple_norm_g"] = _rms_bwd("rms_ple_bwd", dh3, x2, g_ple, dy)
    da = _mm_nt("d_ffn_act", [(dx2b, w_down)], F32, tn=1408)
    grads["w_down"] = _mm_tn("g_ffn_down", a, dx2b, tm=1408)
    dgate, dval, gwg, gwv, gbg, gbv = _ffn_act_bwd(u, wts["ffn_conv_w"], wts["ffn_conv_b"], da)
    grads["ffn_conv_w"] = jnp.concatenate([gwg, gwv], axis=1)
    grads["ffn_conv_b"] = jnp.concatenate([gbg, gbv], axis=1)
    du = _conv_bwd_input("ffn_conv_bwd_in", [dgate, dval], wts["ffn_conv_w"], BF16)
    grads["w_up"] = _mm_tn("g_ffn_up", h2, du, tn=1408)
    dh2 = _mm_nt("d_h2", [(du, w_up)], F32)
    dx1, dx1b, grads["ffn_norm_g"] = _rms_bwd("rms_ffn_bwd", dh2, x1, g_ffn, dx2)
    dmix = _mm_nt("d_mix", [(dx1b, jnp.concatenate([w_out_s, w_out_a], axis=0))], F32)
    grads["w_out"] = jnp.concatenate([_mm_tn("g_out_attn", attn_out, dx1b), _mm_tn("g_out_ssm", ssm_out, dx1b)], axis=0)
    dy_ssd, dz, grads["ssm_norm_g"] = _ssm_post_bwd(dmix, y_ssd, proj, g_ssm)
    dact, ddtr, d_a, d_bias, d_dsk = _ssd_bwd(xbc_act, proj, dt_sp, hstates, dy_ssd, dt_bias_p, a_log_p, dskip_t)
    grads["dt_bias"] = d_bias[:, :SSM_HEADS]
    grads["a_log"] = d_a[:, :SSM_HEADS] * (-jnp.exp(wts["a_log"]))
    grads["d_skip"] = jnp.sum(d_dsk.reshape(SSM_HEADS, HEAD_DIM), axis=1)[None, :]
    dpre, grads["ssm_conv_w"], grads["ssm_conv_b"] = _ssm_conv_bwd_pre(proj, wts["ssm_conv_w"], wts["ssm_conv_b"], dact)
    dxbc = _conv_bwd_input("ssm_conv_bwd_in", [dpre], wts["ssm_conv_w"], BF16)
    dob, dsum = _attn_bwd_prep(dmix, attn_out)
    dqs = [_attn_dq(qn, kn, vb, dob, lse, dsum, d) for d in DILATIONS]
    dkvs = [_attn_dkv(qn, kn, vb, dob, lse, dsum, d) for d in DILATIONS]
    dq, dk, dv, dgq, dgk = _qknorm_bwd(proj, gq_t, gk_t, dqs, [a_ for a_, _ in dkvs], [b_ for _, b_ in dkvs])
    grads["q_norm_g"] = jnp.sum(dgq.reshape(ATTN_DIM // HEAD_DIM, HEAD_DIM), axis=0)[None, :]
    grads["k_norm_g"] = jnp.sum(dgk.reshape(KV_DIM // HEAD_DIM, HEAD_DIM), axis=0)[None, :]
    dproj = jnp.concatenate([dxbc, dq, dz, dk, dv, ddtr], axis=1)
    grads["w_in_p"] = _mm_tn("g_in_proj", h1, dproj, tn=PROJ_P)
    dh1 = _mm_nt("d_h1", [(dproj, w_in_p)], F32)
    grad_x, _, grads["attn_norm_g"] = _rms_bwd("rms_attn_bwd", dh1, x, g_attn, dx1)
    return sq, grad_x, grads


MESH_IDS = pl.DeviceIdType.MESH
N_CHIPS = 4
ANY_SPEC = pl.BlockSpec(memory_space=pl.ANY)
PACK_ROWS = 3840
HALF_ROWS = PACK_ROWS // 2
SMALL_ROWS = 96


def _place():
    x, y, c = lax.axis_index("x"), lax.axis_index("y"), lax.axis_index("c")
    return x, y, c, [(1 - x, y), (x, 1 - y), (1 - x, 1 - y)]


def _gather_over_chips(arrs):
    n = len(arrs)

    def body(*refs):
        ins, outs = refs[:n], refs[n:2 * n]
        send, recv, loc = refs[2 * n], refs[2 * n + 1], refs[2 * n + 2]
        x, y, c, chips = _place()
        mine = 2 * x + y
        local = []
        for a in range(n):
            cp = pltpu.make_async_copy(ins[a], outs[a].at[mine], loc.at[a])
            cp.start()
            local.append(cp)
            for k, (px, py) in enumerate(chips):
                pltpu.make_async_remote_copy(src_ref=ins[a], dst_ref=outs[a].at[mine], send_sem=send.at[3 * a + k],
                                             recv_sem=recv.at[3 * a + k], device_id=(px, py, c), device_id_type=MESH_IDS).start()
        for a in range(n):
            for k, (px, py) in enumerate(chips):
                pltpu.make_async_remote_copy(src_ref=ins[a], dst_ref=outs[a].at[2 * px + py], send_sem=send.at[3 * a + k],
                                             recv_sem=recv.at[3 * a + k], device_id=(px, py, c), device_id_type=MESH_IDS).wait()
        for cp in local:
            cp.wait()

    return pl.pallas_call(
        body, name="gather_weights", in_specs=[ANY_SPEC] * n, out_specs=[ANY_SPEC] * n,
        out_shape=[jax.ShapeDtypeStruct((N_CHIPS,) + a.shape, a.dtype) for a in arrs],
        scratch_shapes=[pltpu.SemaphoreType.DMA((3 * n,)), pltpu.SemaphoreType.DMA((3 * n,)), pltpu.SemaphoreType.DMA((n,))],
    )(*arrs)


def _swap_halves(g):
    def body(g_ref, o_ref, send, recv):
        x, y, c, _ = _place()
        cps = [pltpu.make_async_remote_copy(src_ref=g_ref.at[q, 1 - c], dst_ref=o_ref.at[q], send_sem=send.at[q], recv_sem=recv.at[q],
                                            device_id=(x, y, 1 - c), device_id_type=MESH_IDS) for q in range(N_CHIPS)]
        for cp in cps:
            cp.start()
        for cp in cps:
            cp.wait()

    return pl.pallas_call(
        body, name="grad_swap_halves", in_specs=[ANY_SPEC], out_specs=ANY_SPEC,
        out_shape=jax.ShapeDtypeStruct((N_CHIPS,) + g.shape[2:], g.dtype),
        scratch_shapes=[pltpu.SemaphoreType.DMA((N_CHIPS,)), pltpu.SemaphoreType.DMA((N_CHIPS,))])(g)


def _add_halves(g, got, c_idx, tm=384):
    rows = g.shape[2]

    def kern(c_ref, g_ref, r_ref, o_ref):
        o_ref[...] = (g_ref[...] + r_ref[...]).astype(BF16)

    return pl.pallas_call(
        kern, name="grad_add_halves",
        grid_spec=pltpu.PrefetchScalarGridSpec(
            num_scalar_prefetch=1, grid=(N_CHIPS, rows // tm),
            in_specs=[pl.BlockSpec((None, None, tm, D_MODEL), lambda q, i, c_ref: (q, c_ref[0], i, 0)),
                      pl.BlockSpec((None, tm, D_MODEL), lambda q, i, c_ref: (q, i, 0))],
            out_specs=pl.BlockSpec((None, tm, D_MODEL), lambda q, i, c_ref: (q, i, 0))),
        out_shape=jax.ShapeDtypeStruct((N_CHIPS, rows, D_MODEL), BF16),
        compiler_params=_cparams(("parallel", "parallel")))(c_idx, g, got)


def _scatter_over_chips(s):
    def body(s_ref, o_ref, send, recv, loc):
        x, y, c, chips = _place()
        mine = 2 * x + y
        own = pltpu.make_async_copy(s_ref.at[mine], o_ref.at[mine], loc)
        own.start()
        for k, (px, py) in enumerate(chips):
            pltpu.make_async_remote_copy(src_ref=s_ref.at[2 * px + py], dst_ref=o_ref.at[mine], send_sem=send.at[k], recv_sem=recv.at[k],
                                         device_id=(px, py, c), device_id_type=MESH_IDS).start()
        for k, (px, py) in enumerate(chips):
            pltpu.make_async_remote_copy(src_ref=s_ref.at[2 * px + py], dst_ref=o_ref.at[2 * px + py], send_sem=send.at[k],
                                         recv_sem=recv.at[k], device_id=(px, py, c), device_id_type=MESH_IDS).wait()
        own.wait()

    return pl.pallas_call(
        body, name="grad_scatter_chips", in_specs=[ANY_SPEC], out_specs=ANY_SPEC,
        out_shape=jax.ShapeDtypeStruct(s.shape, s.dtype),
        scratch_shapes=[pltpu.SemaphoreType.DMA((3,)), pltpu.SemaphoreType.DMA((3,)), pltpu.SemaphoreType.DMA])(s)


def _sum_chips(parts, tm=384):
    rows = parts.shape[1]

    def kern(p_ref, o_ref):
        acc = p_ref[0].astype(F32)
        for q in range(1, N_CHIPS):
            acc = acc + p_ref[q].astype(F32)
        o_ref[...] = acc

    return pl.pallas_call(
        kern, name="grad_sum_chips", grid=(rows // tm,),
        in_specs=[pl.BlockSpec((N_CHIPS, tm, D_MODEL), lambda i: (0, i, 0))],
        out_specs=pl.BlockSpec((tm, D_MODEL), lambda i: (i, 0)),
        out_shape=jax.ShapeDtypeStruct((rows, D_MODEL), F32), compiler_params=_cparams(("parallel",)))(parts)


def _share_with_sibling(s):
    def body(s_ref, o_ref, send, recv, loc):
        x, y, c, _ = _place()
        own = pltpu.make_async_copy(s_ref, o_ref.at[c], loc)
        own.start()
        cp = pltpu.make_async_remote_copy(src_ref=s_ref, dst_ref=o_ref.at[c], send_sem=send, recv_sem=recv,
                                          device_id=(x, y, 1 - c), device_id_type=MESH_IDS)
        cp.start()
        cp.wait()
        own.wait()

    return pl.pallas_call(
        body, name="grad_share_sibling", in_specs=[ANY_SPEC], out_specs=ANY_SPEC,
        out_shape=jax.ShapeDtypeStruct((2,) + s.shape, s.dtype),
        scratch_shapes=[pltpu.SemaphoreType.DMA, pltpu.SemaphoreType.DMA, pltpu.SemaphoreType.DMA])(s)


def _allreduce_small(v):
    def body(v_ref, o_ref, land, send, recv):
        x, y, c, _ = _place()
        me = 4 * x + 2 * y + c
        land[me] = v_ref[...]
        cps = []
        for rel in range(1, 8):
            bx, by, bc = (rel >> 2) & 1, (rel >> 1) & 1, rel & 1
            peer = (1 - x if bx else x, 1 - y if by else y, 1 - c if bc else c)
            cps.append(pltpu.make_async_remote_copy(src_ref=v_ref, dst_ref=land.at[me], send_sem=send.at[rel - 1],
                                                    recv_sem=recv.at[rel - 1], device_id=peer, device_id_type=MESH_IDS))
        for cp in cps:
            cp.start()
        for cp in cps:
            cp.wait()
        acc = land[0]
        for d in range(1, 8):
            acc = acc + land[d]
        o_ref[...] = acc

    vm = pl.BlockSpec(memory_space=pltpu.VMEM)
    return pl.pallas_call(
        body, name="allreduce_small", in_specs=[vm], out_specs=vm, out_shape=jax.ShapeDtypeStruct(v.shape, F32),
        scratch_shapes=[pltpu.VMEM((8,) + v.shape, F32), pltpu.SemaphoreType.DMA((7,)), pltpu.SemaphoreType.DMA((7,))])(v)


def _adamw(name, w, g, m, v):
    rows, cols = w.shape
    tm = rows
    if rows * cols > 128 * 1024:
        tm = max(d for d in range(8, 257, 8) if rows % d == 0)
    c1 = 1.0 / (1.0 - ADAM_B1 ** ADAM_STEP)
    c2 = 1.0 / (1.0 - ADAM_B2 ** ADAM_STEP)

    def kern(w_ref, g_ref, m_ref, v_ref, d_ref, mo_ref, vo_ref):
        gv = g_ref[...]
        mn = ADAM_B1 * m_ref[...] + (1.0 - ADAM_B1) * gv
        vn = ADAM_B2 * v_ref[...] + (1.0 - ADAM_B2) * (gv * gv)
        d_ref[...] = -ADAM_LR * ((mn * c1) / (jnp.sqrt(vn * c2) + ADAM_EPS) + ADAM_WD * w_ref[...])
        mo_ref[...] = mn
        vo_ref[...] = vn

    spec = pl.BlockSpec((tm, cols), lambda i: (i, 0))
    return pl.pallas_call(
        kern, name=name, grid=(rows // tm,), in_specs=[spec] * 4, out_specs=[spec] * 3,
        out_shape=[jax.ShapeDtypeStruct(w.shape, F32)] * 3, compiler_params=_cparams(("parallel",)))(w, g, m, v)


SHARDED = (("w_in", 1), ("w_out", 0), ("w_up", 1), ("w_down", 0), ("w_ple_gate", 0), ("w_ple_proj", 1),
           ("ssm_conv_w", 1), ("ffn_conv_w", 1))
MATRICES = ("w_in", "w_out", "w_up", "w_down", "w_ple_gate", "w_ple_proj")
REPLICATED = ("attn_norm_g", "q_norm_g", "k_norm_g", "ssm_conv_b", "dt_bias", "a_log", "d_skip", "ssm_norm_g",
              "ffn_norm_g", "ffn_conv_b", "ple_norm_g")
WEIGHT_ORDER = ("attn_norm_g", "w_in", "q_norm_g", "k_norm_g", "ssm_conv_w", "ssm_conv_b", "dt_bias", "a_log", "d_skip",
                "ssm_norm_g", "w_out", "ffn_norm_g", "w_up", "ffn_conv_w", "ffn_conv_b", "w_down", "ple_norm_g",
                "w_ple_gate", "w_ple_proj")


def _join_chips(g, axis):
    if axis == 0:
        return g.reshape(g.shape[0] * g.shape[1], g.shape[2])
    return jnp.transpose(g, (1, 0, 2)).reshape(g.shape[1], g.shape[0] * g.shape[2])


def _split_chips(g, axis):
    if axis == 0:
        return g.reshape(N_CHIPS, -1)
    r, c = g.shape
    return jnp.transpose(g.reshape(r, N_CHIPS, c // N_CHIPS), (1, 0, 2)).reshape(N_CHIPS, -1)


def _pack_small(vals):
    flat = jnp.concatenate([v.reshape(-1) for v in vals])
    return jnp.pad(flat, (0, SMALL_ROWS * LANE - flat.shape[0])).reshape(SMALL_ROWS, LANE)


def _unpack_small(packed, like):
    flat, out, off = packed.reshape(-1), [], 0
    for v in like:
        out.append(flat[off:off + v.size].reshape(v.shape))
        off += v.size
    return out


def kernel(x, p, attn_norm_g, w_in, q_norm_g, k_norm_g, ssm_conv_w, ssm_conv_b, dt_bias, a_log, d_skip, ssm_norm_g, w_out, ffn_norm_g, w_up, ffn_conv_w, ffn_conv_b, w_down, ple_norm_g, w_ple_gate, w_ple_proj, loss_target, m_attn_norm_g, m_w_in, m_q_norm_g, m_k_norm_g, m_ssm_conv_w, m_ssm_conv_b, m_dt_bias, m_a_log, m_d_skip, m_ssm_norm_g, m_w_out, m_ffn_norm_g, m_w_up, m_ffn_conv_w, m_ffn_conv_b, m_w_down, m_ple_norm_g, m_w_ple_gate, m_w_ple_proj, v_attn_norm_g, v_w_in, v_q_norm_g, v_k_norm_g, v_ssm_conv_w, v_ssm_conv_b, v_dt_bias, v_a_log, v_d_skip, v_ssm_norm_g, v_w_out, v_ffn_norm_g, v_w_up, v_ffn_conv_w, v_ffn_conv_b, v_w_down, v_ple_norm_g, v_w_ple_gate, v_w_ple_proj):
    given = dict(locals())
    w2 = {n: given[n].reshape(given[n].shape[-2:]) if given[n].ndim == 3 else given[n] for n in WEIGHT_ORDER}
    m2 = {n: given["m_" + n].reshape(w2[n].shape) for n in WEIGHT_ORDER}
    v2 = {n: given["v_" + n].reshape(w2[n].shape) for n in WEIGHT_ORDER}

    shards = [w2[n].astype(BF16) if n in MATRICES else w2[n] for n, _ in SHARDED]
    full = {n: _join_chips(g, ax) for (n, ax), g in zip(SHARDED, _gather_over_chips(shards))}
    win = full["w_in"]
    w_in_p = jnp.concatenate([win[:, 2048:3584], win[:, 0:512], win[:, 1024:2048], win[:, 512:768], win[:, 768:1024],
                              win[:, 3584:3600], jnp.zeros((D_MODEL, PROJ_P - IN_PROJ), BF16)], axis=1)
    wts = {n: w2[n] for n in REPLICATED}
    wts.update(w_in_p=w_in_p, w_out_attn=full["w_out"][:ATTN_DIM], w_out_ssm=full["w_out"][ATTN_DIM:], w_up=full["w_up"],
               w_down=full["w_down"], w_ple_gate=full["w_ple_gate"], w_ple_proj=full["w_ple_proj"],
               ssm_conv_w=full["ssm_conv_w"], ffn_conv_w=full["ffn_conv_w"])

    sq, grad_x, grads = _local_step(x[0], p[0, 0], loss_target[0], wts)
    gi = grads.pop("w_in_p")
    grads["w_in"] = jnp.concatenate([gi[:, OFF_Q:OFF_Q + ATTN_DIM], gi[:, OFF_K:OFF_K + KV_DIM], gi[:, OFF_V:OFF_V + KV_DIM],
                                     gi[:, OFF_Z:OFF_Z + SSM_INNER], gi[:, OFF_XBC:OFF_XBC + XBC_DIM], gi[:, OFF_DT:OFF_DT + SSM_HEADS]],
                                    axis=1)

    packed = jnp.concatenate([_split_chips(grads[n], ax) for n, ax in SHARDED], axis=1)
    packed = jnp.pad(packed, ((0, 0), (0, PACK_ROWS * D_MODEL - packed.shape[1]))).reshape(N_CHIPS, 2, HALF_ROWS, D_MODEL)
    c_idx = lax.axis_index("c").astype(jnp.int32).reshape(1)
    chip_sums = _add_halves(packed, _swap_halves(packed), c_idx)
    reduced = _share_with_sibling(_sum_chips(_scatter_over_chips(chip_sums))).reshape(-1)
    g_shard, off = {}, 0
    for n, _ in SHARDED:
        g_shard[n] = reduced[off:off + w2[n].size].reshape(w2[n].shape)
        off += w2[n].size

    small = _allreduce_small(_pack_small([grads[n] for n in REPLICATED] + [jnp.sum(sq).reshape(1)]))
    small_vals = _unpack_small(small, [w2[n] for n in REPLICATED] + [jnp.zeros((1,), F32)])
    for n, g in zip(REPLICATED, small_vals):
        g_shard[n] = g
    loss = (0.5 / D_MODEL) * small_vals[-1][0]

    delta, new_m, new_v = {}, {}, {}
    for n, _ in SHARDED:
        delta[n], new_m[n], new_v[n] = _adamw("adamw_" + n, w2[n], g_shard[n], m2[n], v2[n])
    sm = _adamw("adamw_small", _pack_small([w2[n] for n in REPLICATED]), _pack_small([g_shard[n] for n in REPLICATED]),
                _pack_small([m2[n] for n in REPLICATED]), _pack_small([v2[n] for n in REPLICATED]))
    for dst, packed_out in zip((delta, new_m, new_v), sm):
        for n, val in zip(REPLICATED, _unpack_small(packed_out, [w2[n] for n in REPLICATED])):
            dst[n] = val

    def shaped(d):
        return [d[n].reshape(given[n].shape) for n in WEIGHT_ORDER]
    return (loss, grad_x[None], *shaped(g_shard), *shaped(delta), *shaped(new_m), *shaped(new_v))
```

```python
import functools

import jax
import jax.numpy as jnp
from jax import lax
from jax.experimental import pallas as pl
from jax.experimental.pallas import tpu as pltpu

F32 = jnp.float32
BF16 = jnp.bfloat16

D_MODEL = 1024
HEAD_DIM = 64
ATTN_DIM = 512
KV_DIM = 256
N_KV = 4
SSM_INNER = 1024
SSM_HEADS = 16
SSM_STATE = 128
BC_DIM = 256
XBC_DIM = SSM_INNER + 2 * BC_DIM
MIX_DIM = ATTN_DIM + SSM_INNER
IN_PROJ = 3600
D_FF = 2816
PLE_DIM = 256
CHUNK = 128
DILATIONS = (1, 4, 16)
EPS = 1e-6
ADAM_LR, ADAM_B1, ADAM_B2, ADAM_EPS, ADAM_WD, ADAM_STEP = 0.001, 0.9, 0.999, 1e-08, 0.01, 10

PROJ_P = 3712
OFF_XBC, OFF_Q, OFF_Z, OFF_K, OFF_V, OFF_DT = 0, 1536, 2048, 3072, 3328, 3584
LANE = 128
VMEM_LIMIT = 48 * 1024 * 1024
NEG = -1e30


def _cparams(sem):
    return pltpu.CompilerParams(dimension_semantics=sem, vmem_limit_bytes=VMEM_LIMIT)


def _sigmoid(x):
    return 1.0 / (1.0 + jnp.exp(-x))


def _dot(a, b):
    return jnp.dot(a, b, preferred_element_type=F32)


def _dot_nt(a, b):
    return lax.dot_general(a, b, (((1,), (1,)), ((), ())), preferred_element_type=F32)


def _dot_tn(a, b):
    return lax.dot_general(a, b, (((0,), (0,)), ((), ())), preferred_element_type=F32)


def _dot_split(x, m):
    hi = x.astype(BF16)
    lo = (x - hi.astype(F32)).astype(BF16)
    return _dot(hi, m) + _dot(lo, m)


def _rows(name, body, ins, outs, accs=(), tm=512):
    t_rows = next(s[1].shape[0] for s in ins if s[0] in ("t", "tc"))
    tm = min(tm, t_rows)
    in_specs, args = [], []
    for s in ins:
        if s[0] == "t":
            in_specs.append(pl.BlockSpec((tm, s[1].shape[1]), lambda i: (i, 0)))
        elif s[0] == "tc":
            in_specs.append(pl.BlockSpec((tm, s[2]), functools.partial(lambda i, c: (i, c), c=s[3])))
        else:
            in_specs.append(pl.BlockSpec(s[1].shape, lambda i: (0, 0)))
        args.append(s[1])
    out_shape = [jax.ShapeDtypeStruct((t_rows, w), dt) for w, dt in outs]
    out_specs = [pl.BlockSpec((tm, w), lambda i: (i, 0)) for w, _ in outs]
    out_shape += [jax.ShapeDtypeStruct(a, F32) for a in accs]
    out_specs += [pl.BlockSpec(a, lambda i: (0, 0)) for a in accs]
    n_acc = len(accs)

    def kern(*refs):
        if n_acc:
            @pl.when(pl.program_id(0) == 0)
            def _():
                for r in refs[len(refs) - n_acc:]:
                    r[...] = jnp.zeros(r.shape, F32)
        body(*refs)

    return pl.pallas_call(
        kern, name=name, grid=(t_rows // tm,), in_specs=in_specs, out_specs=out_specs, out_shape=out_shape,
        compiler_params=_cparams(("arbitrary",) if n_acc else ("parallel",)))(*args)


NCHUNK = 512


def _col_chunks(n):
    return [(c, min(NCHUNK, n - c)) for c in range(0, n, NCHUNK)]


def _mm_nn(name, pairs, out_dtype, res=None, tm=512, tn=None, halves=False):
    m, n = pairs[0][0].shape[0], pairs[0][1].shape[1]
    tn = n if tn is None else tn
    tm = min(tm, m)
    np_ = len(pairs)
    if halves:
        per = n // 2 // tn
        out_spec = pl.BlockSpec((None, tm, tn), lambda j, i: (j // per, i, j % per))
        out_shape = jax.ShapeDtypeStruct((2, m, n // 2), out_dtype)
    else:
        out_spec = pl.BlockSpec((tm, tn), lambda j, i: (i, j))
        out_shape = jax.ShapeDtypeStruct((m, n), out_dtype)
    in_specs, args = [], []
    for a, w in pairs:
        in_specs += [pl.BlockSpec((tm, a.shape[1]), lambda j, i: (i, 0)), pl.BlockSpec((w.shape[0], tn), lambda j, i: (0, j))]
        args += [a, w]
    if res is not None:
        in_specs.append(pl.BlockSpec((tm, tn), lambda j, i: (i, j)))
        args.append(res)

    def kern(*refs):
        o_ref = refs[-1]
        for c0, cw in _col_chunks(tn):
            acc = None
            for q in range(np_):
                d = _dot(refs[2 * q][...], refs[2 * q + 1][:, c0:c0 + cw])
                acc = d if acc is None else acc + d
            if res is not None:
                acc = acc + refs[2 * np_][:, c0:c0 + cw]
            o_ref[:, c0:c0 + cw] = acc.astype(o_ref.dtype)

    return pl.pallas_call(
        kern, name=name, grid=(n // tn, m // tm), in_specs=in_specs, out_specs=out_spec, out_shape=out_shape,
        compiler_params=_cparams(("parallel", "parallel")))(*args)


def _mm_nt(name, pairs, out_dtype, tm=512, tn=None):
    m, n = pairs[0][0].shape[0], pairs[0][1].shape[0]
    tn = n if tn is None else tn
    tm = min(tm, m)
    np_ = len(pairs)
    in_specs, args = [], []
    for a, w, kb in pairs:
        in_specs += [pl.BlockSpec((tm, a.shape[1]), lambda j, i: (i, 0)),
                     pl.BlockSpec((tn, a.shape[1]), functools.partial(lambda j, i, kb: (j, kb), kb=kb))]
        args += [a, w]

    def kern(*refs):
        o_ref = refs[-1]
        for c0, cw in _col_chunks(tn):
            acc = None
            for q in range(np_):
                d = _dot_nt(refs[2 * q][...], refs[2 * q + 1][c0:c0 + cw, :])
                acc = d if acc is None else acc + d
            o_ref[:, c0:c0 + cw] = acc.astype(o_ref.dtype)

    return pl.pallas_call(
        kern, name=name, grid=(n // tn, m // tm), in_specs=in_specs,
        out_specs=pl.BlockSpec((tm, tn), lambda j, i: (i, j)),
        out_shape=jax.ShapeDtypeStruct((m, n), out_dtype), compiler_params=_cparams(("parallel", "parallel")))(*args)


def _mm_tn(name, a, b, tm=None, tn=None, tk=1024):
    t, m = a.shape
    n = b.shape[1]
    tm = m if tm is None else tm
    tn = n if tn is None else tn
    tk = min(tk, t)

    def kern(a_ref, b_ref, o_ref):
        @pl.when(pl.program_id(2) == 0)
        def _():
            o_ref[...] = jnp.zeros(o_ref.shape, F32)
        for c0, cw in _col_chunks(tn):
            o_ref[:, c0:c0 + cw] += _dot_tn(a_ref[...], b_ref[:, c0:c0 + cw])

    return pl.pallas_call(
        kern, name=name, grid=(m // tm, n // tn, t // tk),
        in_specs=[pl.BlockSpec((tk, tm), lambda i, j, k: (k, i)), pl.BlockSpec((tk, tn), lambda i, j, k: (k, j))],
        out_specs=pl.BlockSpec((tm, tn), lambda i, j, k: (i, j)),
        out_shape=jax.ShapeDtypeStruct((m, n), F32),
        compiler_params=_cparams(("parallel", "parallel", "arbitrary")))(a, b)


def _rms_fwd(name, x, g):
    def body(x_ref, g_ref, h_ref):
        xv = x_ref[...]
        r = lax.rsqrt(jnp.mean(xv * xv, axis=-1, keepdims=True) + EPS)
        h_ref[...] = (xv * r * g_ref[...]).astype(BF16)
    return _rows(name, body, [("t", x), ("p", g)], [(x.shape[1], BF16)])[0]


def _rms_bwd(name, dh, x, g, dres):
    d = x.shape[1]

    def body(dh_ref, x_ref, g_ref, dres_ref, dx_ref, dxb_ref, dg_ref):
        xv, dhv = x_ref[...], dh_ref[...]
        r = lax.rsqrt(jnp.mean(xv * xv, axis=-1, keepdims=True) + EPS)
        gd = dhv * g_ref[...]
        dx = dres_ref[...] + r * gd - xv * (r * r * r * jnp.mean(xv * gd, axis=-1, keepdims=True))
        dx_ref[...] = dx
        dxb_ref[...] = dx.astype(BF16)
        dg_ref[...] += jnp.sum(dhv * xv * r, axis=0, keepdims=True)
    return _rows(name, body, [("t", dh), ("t", x), ("p", g), ("t", dres)], [(d, F32), (d, BF16)], accs=[(1, d)])


def _head_mean_matrix(width):
    i = jnp.arange(width) // HEAD_DIM
    return jnp.where(i[:, None] == i[None, :], 1.0 / HEAD_DIM, 0.0).astype(BF16)


def _qknorm_fwd(proj, gq_t, gk_t):
    bq, bk = _head_mean_matrix(ATTN_DIM), _head_mean_matrix(KV_DIM)
    scale = HEAD_DIM ** -0.5

    def body(q_ref, k_ref, v_ref, gq_ref, gk_ref, bq_ref, bk_ref, qn_ref, kn_ref, vb_ref):
        q, k = q_ref[...], k_ref[...]
        rq = lax.rsqrt(_dot_split(q * q, bq_ref[...]) + EPS)
        rk = lax.rsqrt(_dot_split(k * k, bk_ref[...]) + EPS)
        qn_ref[...] = ((q * rq * gq_ref[...]) * scale).astype(BF16)
        kn_ref[...] = (k * rk * gk_ref[...]).astype(BF16)
        vb_ref[...] = v_ref[...].astype(BF16)

    return _rows("qknorm_fwd", body,
                 [("tc", proj, ATTN_DIM, OFF_Q // ATTN_DIM), ("tc", proj, KV_DIM, OFF_K // KV_DIM),
                  ("tc", proj, KV_DIM, OFF_V // KV_DIM), ("p", gq_t), ("p", gk_t), ("p", bq), ("p", bk)],
                 [(ATTN_DIM, BF16), (KV_DIM, BF16), (KV_DIM, BF16)])


def _qknorm_bwd(proj, gq_t, gk_t, dqs, dks, dvs):
    bq, bk = _head_mean_matrix(ATTN_DIM), _head_mean_matrix(KV_DIM)
    scale = HEAD_DIM ** -0.5

    def body(q_ref, k_ref, gq_ref, gk_ref, bq_ref, bk_ref, dq1, dq2, dq3, dk1, dk2, dk3, dv1, dv2, dv3,
             dq_ref, dk_ref, dv_ref, dgq_ref, dgk_ref):
        q, k = q_ref[...], k_ref[...]
        dqn = (dq1[...] + dq2[...] + dq3[...]) * scale
        dkn = dk1[...] + dk2[...] + dk3[...]
        rq = lax.rsqrt(_dot_split(q * q, bq_ref[...]) + EPS)
        rk = lax.rsqrt(_dot_split(k * k, bk_ref[...]) + EPS)
        gdq, gdk = dqn * gq_ref[...], dkn * gk_ref[...]
        dq_ref[...] = (rq * gdq - q * (rq * rq * rq * _dot_split(q * gdq, bq_ref[...]))).astype(BF16)
        dk_ref[...] = (rk * gdk - k * (rk * rk * rk * _dot_split(k * gdk, bk_ref[...]))).astype(BF16)
        dv_ref[...] = (dv1[...] + dv2[...] + dv3[...]).astype(BF16)
        dgq_ref[...] += jnp.sum(dqn * q * rq, axis=0, keepdims=True)
        dgk_ref[...] += jnp.sum(dkn * k * rk, axis=0, keepdims=True)

    ins = [("tc", proj, ATTN_DIM, OFF_Q // ATTN_DIM), ("tc", proj, KV_DIM, OFF_K // KV_DIM),
           ("p", gq_t), ("p", gk_t), ("p", bq), ("p", bk)]
    ins += [("t", a) for a in dqs] + [("t", a) for a in dks] + [("t", a) for a in dvs]
    return _rows("qknorm_bwd", body, ins, [(ATTN_DIM, BF16), (KV_DIM, BF16), (KV_DIM, BF16)],
                 accs=[(1, ATTN_DIM), (1, KV_DIM)], tm=256)


def _head_cols(x, kh, width=HEAD_DIM):
    return x[:, kh * width:(kh + 1) * width]


def _stack_q(x, kh):
    return jnp.concatenate([x[:, 2 * kh * HEAD_DIM:(2 * kh + 1) * HEAD_DIM],
                            x[:, (2 * kh + 1) * HEAD_DIM:(2 * kh + 2) * HEAD_DIM]], axis=0)


def _stat_cols(stat, kh, rows):
    return jnp.concatenate([stat[:, 2 * kh:2 * kh + 1], stat[:, 2 * kh + 1:2 * kh + 2]], axis=0)


def _sub_view(a, dil):
    t, c = a.shape
    return a.reshape(t // (CHUNK * dil), CHUNK, dil * c)


def _attn_fwd(qn, kn, vb, dil):
    t = qn.shape[0]
    nblk = t // (CHUNK * dil)

    def kern(q_ref, kp_ref, kc_ref, vp_ref, vc_ref, o_ref, lse_ref):
        n = pl.program_id(1)
        q, kp, kc, vp, vc = q_ref[...], kp_ref[...], kc_ref[...], vp_ref[...], vc_ref[...]
        ri = lax.broadcasted_iota(jnp.int32, (2 * CHUNK, 2 * CHUNK), 0) % CHUNK
        cj = lax.broadcasted_iota(jnp.int32, (2 * CHUNK, 2 * CHUNK), 1)
        mask = (cj - ri >= 0) & (cj - ri <= CHUNK) & ((n > 0) | (cj >= CHUNK))
        lane = lax.broadcasted_iota(jnp.int32, (CHUNK, LANE), 1)
        outs, lse_tile = [], jnp.zeros((CHUNK, LANE), F32)
        for kh in range(N_KV):
            q2 = _stack_q(q, kh)
            k2 = jnp.concatenate([_head_cols(kp, kh), _head_cols(kc, kh)], axis=0)
            v2 = jnp.concatenate([_head_cols(vp, kh), _head_cols(vc, kh)], axis=0)
            s = jnp.where(mask, _dot_nt(q2, k2), NEG)
            m = jnp.max(s, axis=1, keepdims=True)
            p = jnp.exp(s - m)
            l = jnp.sum(p, axis=1, keepdims=True)
            o = _dot(p.astype(BF16), v2) / l
            lse = m + jnp.log(l)
            outs += [o[:CHUNK], o[CHUNK:]]
            lse_tile = jnp.where(lane == 2 * kh, lse[:CHUNK], lse_tile)
            lse_tile = jnp.where(lane == 2 * kh + 1, lse[CHUNK:], lse_tile)
        o_ref[...] = jnp.concatenate(outs, axis=1)
        lse_ref[...] = lse_tile

    cur = lambda r, n: (n, 0, r)
    prev = lambda r, n: (jnp.maximum(n - 1, 0), 0, r)
    o, lse = pl.pallas_call(
        kern, name=f"attn_fwd_d{dil}", grid=(dil, nblk),
        in_specs=[pl.BlockSpec((None, CHUNK, ATTN_DIM), cur),
                  pl.BlockSpec((None, CHUNK, KV_DIM), prev), pl.BlockSpec((None, CHUNK, KV_DIM), cur),
                  pl.BlockSpec((None, CHUNK, KV_DIM), prev), pl.BlockSpec((None, CHUNK, KV_DIM), cur)],
        out_specs=[pl.BlockSpec((None, CHUNK, ATTN_DIM), cur), pl.BlockSpec((None, CHUNK, LANE), cur)],
        out_shape=[jax.ShapeDtypeStruct((nblk, CHUNK, dil * ATTN_DIM), F32),
                   jax.ShapeDtypeStruct((nblk, CHUNK, dil * LANE), F32)],
        compiler_params=_cparams(("parallel", "parallel")),
    )(_sub_view(qn, dil), _sub_view(kn, dil), _sub_view(kn, dil), _sub_view(vb, dil), _sub_view(vb, dil))
    return o.reshape(t, ATTN_DIM), lse.reshape(t, LANE)


def _head_expand_matrix():
    return (jnp.arange(LANE)[:, None] == (jnp.arange(ATTN_DIM)[None, :] // HEAD_DIM)).astype(BF16)


def _attn_merge(os_, lses):
    def body(o1, o2, o3, l1, l2, l3, e_ref, out_ref, lse_ref):
        a, b, c = l1[...], l2[...], l3[...]
        m = jnp.maximum(jnp.maximum(a, b), c)
        tot = m + jnp.log(jnp.exp(a - m) + jnp.exp(b - m) + jnp.exp(c - m))
        e = e_ref[...]
        out = (_dot_split(jnp.exp(a - tot), e) * o1[...] + _dot_split(jnp.exp(b - tot), e) * o2[...]
               + _dot_split(jnp.exp(c - tot), e) * o3[...])
        out_ref[...] = out.astype(BF16)
        lse_ref[...] = tot
    ins = [("t", o) for o in os_] + [("t", l) for l in lses] + [("p", _head_expand_matrix())]
    return _rows("attn_merge", body, ins, [(ATTN_DIM, BF16), (LANE, F32)], tm=256)


def _attn_bwd_prep(dmix, attn_out):
    et = _head_expand_matrix().T

    def body(do_ref, o_ref, et_ref, dob_ref, d_ref):
        do = do_ref[...]
        dob_ref[...] = do.astype(BF16)
        d_ref[...] = _dot_split(do * o_ref[...].astype(F32), et_ref[...])
    return _rows("attn_bwd_prep", body, [("tc", dmix, ATTN_DIM, SSM_INNER // ATTN_DIM), ("t", attn_out), ("p", et)],
                 [(ATTN_DIM, BF16), (LANE, F32)])


def _attn_dq(qn, kn, vb, dob, lse, dsum, dil):
    t = qn.shape[0]
    nblk = t // (CHUNK * dil)

    def kern(q_ref, kp_ref, kc_ref, vp_ref, vc_ref, do_ref, lse_ref, d_ref, dq_ref):
        n = pl.program_id(1)
        q, kp, kc, vp, vc, do = q_ref[...], kp_ref[...], kc_ref[...], vp_ref[...], vc_ref[...], do_ref[...]
        lse_t, d_t = lse_ref[...], d_ref[...]
        ri = lax.broadcasted_iota(jnp.int32, (2 * CHUNK, 2 * CHUNK), 0) % CHUNK
        cj = lax.broadcasted_iota(jnp.int32, (2 * CHUNK, 2 * CHUNK), 1)
        mask = (cj - ri >= 0) & (cj - ri <= CHUNK) & ((n > 0) | (cj >= CHUNK))
        outs = []
        for kh in range(N_KV):
            q2, do2 = _stack_q(q, kh), _stack_q(do, kh)
            k2 = jnp.concatenate([_head_cols(kp, kh), _head_cols(kc, kh)], axis=0)
            v2 = jnp.concatenate([_head_cols(vp, kh), _head_cols(vc, kh)], axis=0)
            p = jnp.where(mask, jnp.exp(jnp.where(mask, _dot_nt(q2, k2), NEG) - _stat_cols(lse_t, kh, CHUNK)), 0.0)
            ds = p * (_dot_nt(do2, v2) - _stat_cols(d_t, kh, CHUNK))
            dq2 = _dot(ds.astype(BF16), k2)
            outs += [dq2[:CHUNK], dq2[CHUNK:]]
        dq_ref[...] = jnp.concatenate(outs, axis=1)

    cur = lambda r, n: (n, 0, r)
    prev = lambda r, n: (jnp.maximum(n - 1, 0), 0, r)
    dq = pl.pallas_call(
        kern, name=f"attn_dq_d{dil}", grid=(dil, nblk),
        in_specs=[pl.BlockSpec((None, CHUNK, ATTN_DIM), cur),
                  pl.BlockSpec((None, CHUNK, KV_DIM), prev), pl.BlockSpec((None, CHUNK, KV_DIM), cur),
                  pl.BlockSpec((None, CHUNK, KV_DIM), prev), pl.BlockSpec((None, CHUNK, KV_DIM), cur),
                  pl.BlockSpec((None, CHUNK, ATTN_DIM), cur),
                  pl.BlockSpec((None, CHUNK, LANE), cur), pl.BlockSpec((None, CHUNK, LANE), cur)],
        out_specs=pl.BlockSpec((None, CHUNK, ATTN_DIM), cur),
        out_shape=jax.ShapeDtypeStruct((nblk, CHUNK, dil * ATTN_DIM), F32),
        compiler_params=_cparams(("parallel", "parallel")),
    )(_sub_view(qn, dil), _sub_view(kn, dil), _sub_view(kn, dil), _sub_view(vb, dil), _sub_view(vb, dil),
      _sub_view(dob, dil), _sub_view(lse, dil), _sub_view(dsum, dil))
    return dq.reshape(t, ATTN_DIM)


def _attn_dkv(qn, kn, vb, dob, lse, dsum, dil):
    t = qn.shape[0]
    nblk = t // (CHUNK * dil)

    def kern(k_ref, v_ref, qc_ref, qn_ref, doc_ref, don_ref, lc_ref, ln_ref, dc_ref, dn_ref, dk_ref, dv_ref):
        n = pl.program_id(1)
        k, v = k_ref[...], v_ref[...]
        qc, qx, doc, dox = qc_ref[...], qn_ref[...], doc_ref[...], don_ref[...]
        lc, lx, dc, dx = lc_ref[...], ln_ref[...], dc_ref[...], dn_ref[...]
        ri = lax.broadcasted_iota(jnp.int32, (4 * CHUNK, CHUNK), 0) % (2 * CHUNK)
        cj = lax.broadcasted_iota(jnp.int32, (4 * CHUNK, CHUNK), 1)
        mask = (ri - cj >= 0) & (ri - cj <= CHUNK) & ((n < nblk - 1) | (ri < CHUNK))
        dks, dvs = [], []
        for kh in range(N_KV):
            def rows4(cur, nxt, w=HEAD_DIM):
                return jnp.concatenate([cur[:, 2 * kh * w:(2 * kh + 1) * w], nxt[:, 2 * kh * w:(2 * kh + 1) * w],
                                        cur[:, (2 * kh + 1) * w:(2 * kh + 2) * w], nxt[:, (2 * kh + 1) * w:(2 * kh + 2) * w]],
                                       axis=0)
            q4, do4 = rows4(qc, qx), rows4(doc, dox)
            lse4, d4 = rows4(lc, lx, 1), rows4(dc, dx, 1)
            kk, vv = _head_cols(k, kh), _head_cols(v, kh)
            p = jnp.where(mask, jnp.exp(jnp.where(mask, _dot_nt(q4, kk), NEG) - lse4), 0.0)
            dvs.append(_dot_tn(p.astype(BF16), do4))
            ds = p * (_dot_nt(do4, vv) - d4)
            dks.append(_dot_tn(ds.astype(BF16), q4))
        dk_ref[...] = jnp.concatenate(dks, axis=1)
        dv_ref[...] = jnp.concatenate(dvs, axis=1)

    cur = lambda r, n: (n, 0, r)
    nxt = lambda r, n: (jnp.minimum(n + 1, nblk - 1), 0, r)
    qv, dv_, lv, sv = _sub_view(qn, dil), _sub_view(dob, dil), _sub_view(lse, dil), _sub_view(dsum, dil)
    dk, dv = pl.pallas_call(
        kern, name=f"attn_dkv_d{dil}", grid=(dil, nblk),
        in_specs=[pl.BlockSpec((None, CHUNK, KV_DIM), cur), pl.BlockSpec((None, CHUNK, KV_DIM), cur),
                  pl.BlockSpec((None, CHUNK, ATTN_DIM), cur), pl.BlockSpec((None, CHUNK, ATTN_DIM), nxt),
                  pl.BlockSpec((None, CHUNK, ATTN_DIM), cur), pl.BlockSpec((None, CHUNK, ATTN_DIM), nxt),
                  pl.BlockSpec((None, CHUNK, LANE), cur), pl.BlockSpec((None, CHUNK, LANE), nxt),
                  pl.BlockSpec((None, CHUNK, LANE), cur), pl.BlockSpec((None, CHUNK, LANE), nxt)],
        out_specs=[pl.BlockSpec((None, CHUNK, KV_DIM), cur), pl.BlockSpec((None, CHUNK, KV_DIM), cur)],
        out_shape=[jax.ShapeDtypeStruct((nblk, CHUNK, dil * KV_DIM), F32)] * 2,
        compiler_params=_cparams(("parallel", "parallel")),
    )(_sub_view(kn, dil), _sub_view(vb, dil), qv, qv, dv_, dv_, lv, lv, sv, sv)
    return dk.reshape(t, KV_DIM), dv.reshape(t, KV_DIM)


HALO = 8
SSM_CONV_TM, SSM_CONV_W = 512, 512
FFN_CONV_TM, FFN_CONV_W = 256, 1408


def _halo_specs(tm, width, t_rows, col_off=0, lead=None):
    per, last = tm // HALO, t_rows // HALO - 1
    row_maps = (lambda i: i, lambda i: jnp.maximum(i * per - 1, 0), lambda i: jnp.minimum((i + 1) * per, last))
    specs = []
    for rows, rm in zip((tm, HALO, HALO), row_maps):
        if lead is None:
            specs.append(pl.BlockSpec((rows, width), functools.partial(lambda c, i, rm: (rm(i), c + col_off), rm=rm)))
        else:
            specs.append(pl.BlockSpec((None, rows, width), functools.partial(lambda c, i, rm: (lead, rm(i), c + col_off), rm=rm)))
    return specs


def _fill_ext(buf, tile_ref, before_ref, after_ref, i, nt):
    tm = tile_ref.shape[0]
    buf[0:HALO, :] = jnp.where(i > 0, before_ref[...].astype(F32), 0.0)
    buf[HALO:HALO + tm, :] = tile_ref[...].astype(F32)
    if after_ref is not None:
        buf[HALO + tm:, :] = jnp.where(i < nt - 1, after_ref[...].astype(F32), 0.0)


def _taps_fwd(buf, w, taps, rows):
    acc = None
    for k in range(taps):
        term = w[k:k + 1, :] * buf[pl.ds(HALO - (taps - 1) + k, rows), :]
        acc = term if acc is None else acc + term
    return acc


def _taps_bwd(bufd, w, taps, rows):
    acc = None
    for k in range(taps):
        term = w[k:k + 1, :] * bufd[pl.ds((taps - 1) - k, rows), :]
        acc = term if acc is None else acc + term
    return acc


def _silu_grad(pre):
    sg = _sigmoid(pre)
    return sg * (1.0 + pre * (1.0 - sg))


def _ssm_conv_fwd(proj, w, b):
    t = proj.shape[0]
    tm, wd = min(SSM_CONV_TM, t), SSM_CONV_W
    nt, taps = t // tm, w.shape[0]

    def kern(x_ref, xb_ref, w_ref, b_ref, o_ref, buf):
        _fill_ext(buf, x_ref, xb_ref, None, pl.program_id(1), nt)
        pre = _taps_fwd(buf, w_ref[...], taps, tm) + b_ref[...]
        o_ref[...] = pre * _sigmoid(pre)

    tile, before, _ = _halo_specs(tm, wd, t)
    par = lambda rows: pl.BlockSpec((rows, wd), lambda c, i: (0, c))
    return pl.pallas_call(
        kern, name="ssm_conv_fwd", grid=(XBC_DIM // wd, nt), in_specs=[tile, before, par(taps), par(1)],
        out_specs=pl.BlockSpec((tm, wd), lambda c, i: (i, c)), out_shape=jax.ShapeDtypeStruct((t, XBC_DIM), F32),
        scratch_shapes=[pltpu.VMEM((tm + HALO, wd), F32)],
        compiler_params=_cparams(("parallel", "parallel")))(proj, proj, w, b)


def _ssm_conv_bwd(proj, w, b, dact):
    t = proj.shape[0]
    tm, wd = min(SSM_CONV_TM, t), SSM_CONV_W
    nt, taps = t // tm, w.shape[0]

    def kern(x_ref, xb_ref, xa_ref, d_ref, dn_ref, w_ref, b_ref, dx_ref, gw_ref, gb_ref, buf, bufd):
        i = pl.program_id(1)
        _fill_ext(buf, x_ref, xb_ref, xa_ref, i, nt)

        @pl.when(i == 0)
        def _():
            gw_ref[...] = jnp.zeros(gw_ref.shape, F32)
            gb_ref[...] = jnp.zeros(gb_ref.shape, F32)
        wv = w_ref[...]
        pre = _taps_fwd(buf, wv, taps, tm + HALO) + b_ref[...]
        bufd[0:tm, :] = d_ref[...]
        bufd[tm:, :] = jnp.where(i < nt - 1, dn_ref[...], 0.0)
        dpre = bufd[...] * _silu_grad(pre)
        bufd[...] = dpre
        dx_ref[...] = _taps_bwd(bufd, wv, taps, tm).astype(BF16)
        dp = dpre[:tm]
        gb_ref[...] += jnp.sum(dp, axis=0, keepdims=True)
        for k in range(taps):
            gw_ref[k:k + 1, :] += jnp.sum(dp * buf[pl.ds(HALO - (taps - 1) + k, tm), :], axis=0, keepdims=True)

    xt, xb, xa = _halo_specs(tm, wd, t)
    dt_, _, dn = _halo_specs(tm, wd, t)
    par = lambda rows: pl.BlockSpec((rows, wd), lambda c, i: (0, c))
    return pl.pallas_call(
        kern, name="ssm_conv_bwd", grid=(XBC_DIM // wd, nt), in_specs=[xt, xb, xa, dt_, dn, par(taps), par(1)],
        out_specs=[pl.BlockSpec((tm, wd), lambda c, i: (i, c)), par(taps), par(1)],
        out_shape=[jax.ShapeDtypeStruct((t, XBC_DIM), BF16), jax.ShapeDtypeStruct((taps, XBC_DIM), F32),
                   jax.ShapeDtypeStruct((1, XBC_DIM), F32)],
        scratch_shapes=[pltpu.VMEM((tm + 2 * HALO, wd), F32), pltpu.VMEM((tm + HALO, wd), F32)],
        compiler_params=_cparams(("parallel", "arbitrary")))(proj, proj, proj, dact, dact, w, b)


def _ffn_act_fwd(u, w, b):
    t = u.shape[1]
    tm, wd = min(FFN_CONV_TM, t), FFN_CONV_W
    nt, taps, nc = t // tm, w.shape[0], D_FF // FFN_CONV_W

    def kern(g_ref, gb_ref, v_ref, vb_ref, wg_ref, wv_ref, bg_ref, bv_ref, a_ref, bufg, bufv):
        i = pl.program_id(1)
        _fill_ext(bufg, g_ref, gb_ref, None, i, nt)
        _fill_ext(bufv, v_ref, vb_ref, None, i, nt)
        g = _taps_fwd(bufg, wg_ref[...], taps, tm) + bg_ref[...]
        v = _taps_fwd(bufv, wv_ref[...], taps, tm) + bv_ref[...]
        a_ref[...] = (g * _sigmoid(g) * v).astype(BF16)

    gt, gbf, _ = _halo_specs(tm, wd, t, lead=0)
    vt, vbf, _ = _halo_specs(tm, wd, t, lead=1)
    par = lambda rows, off: pl.BlockSpec((rows, wd), functools.partial(lambda c, i, off: (0, c + off), off=off))
    return pl.pallas_call(
        kern, name="ffn_act_fwd", grid=(nc, nt),
        in_specs=[gt, gbf, vt, vbf, par(taps, 0), par(taps, nc), par(1, 0), par(1, nc)],
        out_specs=pl.BlockSpec((tm, wd), lambda c, i: (i, c)), out_shape=jax.ShapeDtypeStruct((t, D_FF), BF16),
        scratch_shapes=[pltpu.VMEM((tm + HALO, wd), F32)] * 2,
        compiler_params=_cparams(("parallel", "parallel")))(u, u, u, u, w, w, b, b)


def _ffn_act_bwd(u, w, b, da):
    t = u.shape[1]
    tm, wd = min(FFN_CONV_TM, t), FFN_CONV_W
    nt, taps, nc = t // tm, w.shape[0], D_FF // FFN_CONV_W

    def kern(g_ref, gb_ref, ga_ref, v_ref, vb_ref, va_ref, d_ref, dn_ref, wg_ref, wv_ref, bg_ref, bv_ref,
             du_ref, gwg_ref, gwv_ref, gbg_ref, gbv_ref, bufg, bufv, bufdg, bufdv):
        i = pl.program_id(1)
        _fill_ext(bufg, g_ref, gb_ref, ga_ref, i, nt)
        _fill_ext(bufv, v_ref, vb_ref, va_ref, i, nt)

        @pl.when(i == 0)
        def _():
            for r in (gwg_ref, gwv_ref, gbg_ref, gbv_ref):
                r[...] = jnp.zeros(r.shape, F32)
        wg, wv = wg_ref[...], wv_ref[...]
        g = _taps_fwd(bufg, wg, taps, tm + HALO) + bg_ref[...]
        v = _taps_fwd(bufv, wv, taps, tm + HALO) + bv_ref[...]
        bufdg[0:tm, :] = d_ref[...]
        bufdg[tm:, :] = jnp.where(i < nt - 1, dn_ref[...], 0.0)
        dav = bufdg[...]
        sg = _sigmoid(g)
        dg = dav * v * (sg * (1.0 + g * (1.0 - sg)))
        dv = dav * (g * sg)
        bufdg[...] = dg
        bufdv[...] = dv
        du_ref[0] = _taps_bwd(bufdg, wg, taps, tm).astype(BF16)
        du_ref[1] = _taps_bwd(bufdv, wv, taps, tm).astype(BF16)
        dg, dv = dg[:tm], dv[:tm]
        gbg_ref[...] += jnp.sum(dg, axis=0, keepdims=True)
        gbv_ref[...] += jnp.sum(dv, axis=0, keepdims=True)
        for k in range(taps):
            sl = pl.ds(HALO - (taps - 1) + k, tm)
            gwg_ref[k:k + 1, :] += jnp.sum(dg * bufg[sl, :], axis=0, keepdims=True)
            gwv_ref[k:k + 1, :] += jnp.sum(dv * bufv[sl, :], axis=0, keepdims=True)

    gt, gbf, gaf = _halo_specs(tm, wd, t, lead=0)
    vt, vbf, vaf = _halo_specs(tm, wd, t, lead=1)
    dt_, _, dn = _halo_specs(tm, wd, t)
    par = lambda rows, off: pl.BlockSpec((rows, wd), functools.partial(lambda c, i, off: (0, c + off), off=off))
    return pl.pallas_call(
        kern, name="ffn_act_bwd", grid=(nc, nt),
        in_specs=[gt, gbf, gaf, vt, vbf, vaf, dt_, dn, par(taps, 0), par(taps, nc), par(1, 0), par(1, nc)],
        out_specs=[pl.BlockSpec((2, tm, wd), lambda c, i: (0, i, c)), par(taps, 0), par(taps, 0), par(1, 0), par(1, 0)],
        out_shape=[jax.ShapeDtypeStruct((2, t, D_FF), BF16)] + [jax.ShapeDtypeStruct((taps, D_FF), F32)] * 2
        + [jax.ShapeDtypeStruct((1, D_FF), F32)] * 2,
        scratch_shapes=[pltpu.VMEM((tm + 2 * HALO, wd), F32)] * 2 + [pltpu.VMEM((tm + HALO, wd), F32)] * 2,
        compiler_params=_cparams(("parallel", "arbitrary")))(u, u, u, u, u, u, da, da, w, w, b, b)


def _softplus(x):
    e = jnp.exp(-jnp.abs(x))
    return jnp.maximum(x, 0.0) + jnp.where(e < 1e-4, e - 0.5 * e * e, jnp.log(1.0 + e))


def _tri(lower):
    r = lax.broadcasted_iota(jnp.int32, (CHUNK, CHUNK), 0)
    c = lax.broadcasted_iota(jnp.int32, (CHUNK, CHUNK), 1)
    return (r >= c) if lower else (r <= c)


def _cum(mat_bool, x):
    return jnp.dot(mat_bool.astype(F32), x, precision=lax.Precision.HIGHEST, preferred_element_type=F32)


def _pair_sel(lane_lo, tile, h0):
    return jnp.where(lane_lo, tile[:, h0:h0 + 1], tile[:, h0 + 1:h0 + 2])


def _ssd_fwd(xbc_act, proj, dt_bias_p, a_log_p, dskip_t):
    t = xbc_act.shape[0]
    nch = t // CHUNK

    def kern(xa_ref, dtr_ref, bias_ref, alog_ref, dsk_ref, y_ref, dt_ref, hs_ref, hst):
        @pl.when(pl.program_id(0) == 0)
        def _():
            hst[...] = jnp.zeros(hst.shape, F32)
        dt = _softplus(dtr_ref[...] + bias_ref[...])
        dt_ref[...] = dt
        acum = _cum(_tri(True), dt * (-jnp.exp(alog_ref[...])))
        acum_t = acum.T
        ea = jnp.exp(acum)
        a_last = acum[CHUNK - 1:CHUNK, :]
        dend = jnp.exp(a_last - acum)
        ea_last = jnp.exp(a_last)
        causal = _tri(True)
        lane_lo = lax.broadcasted_iota(jnp.int32, (CHUNK, LANE), 1) < HEAD_DIM
        row_lo = lax.broadcasted_iota(jnp.int32, (CHUNK, LANE), 0) < HEAD_DIM
        for g in range(2):
            bg = xa_ref[:, SSM_INNER + g * SSM_STATE:SSM_INNER + (g + 1) * SSM_STATE].astype(BF16)
            cg = xa_ref[:, SSM_INNER + BC_DIM + g * SSM_STATE:SSM_INNER + BC_DIM + (g + 1) * SSM_STATE].astype(BF16)
            cb = _dot_nt(cg, bg)
            for j in range(4 * g, 4 * g + 4):
                h0 = 2 * j
                cols = slice(j * LANE, (j + 1) * LANE)
                xp = xa_ref[:, cols]
                xdt = xp * _pair_sel(lane_lo, dt, h0)
                ydiag = None
                for hh, sel in ((h0, lane_lo), (h0 + 1, ~lane_lo)):
                    seg = acum[:, hh:hh + 1] - acum_t[hh:hh + 1, :]
                    mm = (cb * jnp.where(causal, jnp.exp(jnp.minimum(seg, 0.0)), 0.0)).astype(BF16)
                    d = _dot(mm, jnp.where(sel, xdt, 0.0).astype(BF16))
                    ydiag = d if ydiag is None else ydiag + d
                hp = hst[cols, :]
                hs_ref[cols, :] = hp
                yoff = _dot_nt(cg, hp.astype(BF16)) * _pair_sel(lane_lo, ea, h0)
                y_ref[:, cols] = ydiag + yoff + dsk_ref[:, cols] * xp
                xw = (xdt * _pair_sel(lane_lo, dend, h0)).astype(BF16)
                rowf = jnp.where(row_lo, ea_last[:, h0:h0 + 1], ea_last[:, h0 + 1:h0 + 2])
                hst[cols, :] = hp * rowf + _dot_tn(xw, bg)

    return pl.pallas_call(
        kern, name="ssd_fwd", grid=(nch,),
        in_specs=[pl.BlockSpec((CHUNK, XBC_DIM), lambda c: (c, 0)), pl.BlockSpec((CHUNK, LANE), lambda c: (c, OFF_DT // LANE)),
                  pl.BlockSpec((1, LANE), lambda c: (0, 0)), pl.BlockSpec((1, LANE), lambda c: (0, 0)),
                  pl.BlockSpec((1, SSM_INNER), lambda c: (0, 0))],
        out_specs=[pl.BlockSpec((CHUNK, SSM_INNER), lambda c: (c, 0)), pl.BlockSpec((CHUNK, LANE), lambda c: (c, 0)),
                   pl.BlockSpec((None, SSM_INNER, SSM_STATE), lambda c: (c, 0, 0))],
        out_shape=[jax.ShapeDtypeStruct((t, SSM_INNER), F32), jax.ShapeDtypeStruct((t, LANE), F32),
                   jax.ShapeDtypeStruct((nch, SSM_INNER, SSM_STATE), F32)],
        scratch_shapes=[pltpu.VMEM((SSM_INNER, SSM_STATE), F32)],
        compiler_params=_cparams(("arbitrary",)))(xbc_act, proj, dt_bias_p, a_log_p, dskip_t)


def _ssd_bwd(xbc_act, proj, dt_sp, hstates, dy, dt_bias_p, a_log_p, dskip_t):
    t = xbc_act.shape[0]
    nch = t // CHUNK

    def kern(xa_ref, dtr_ref, dt_ref, hs_ref, dy_ref, bias_ref, alog_ref, dsk_ref,
             dact_ref, ddtr_ref, da_ref, dbias_ref, ddsk_ref, dh):
        @pl.when(pl.program_id(0) == 0)
        def _():
            dh[...] = jnp.zeros(dh.shape, F32)
            for r in (da_ref, dbias_ref, ddsk_ref):
                r[...] = jnp.zeros(r.shape, F32)
        dt = dt_ref[...]
        a_neg = -jnp.exp(alog_ref[...])
        acum = _cum(_tri(True), dt * a_neg)
        acum_t = acum.T
        ea = jnp.exp(acum)
        a_last = acum[CHUNK - 1:CHUNK, :]
        dend = jnp.exp(a_last - acum)
        ea_last = jnp.exp(a_last)
        causal = _tri(True)
        lane = lax.broadcasted_iota(jnp.int32, (CHUNK, LANE), 1)
        rowi = lax.broadcasted_iota(jnp.int32, (CHUNK, LANE), 0)
        lane_lo, row_lo, last_row = lane < HEAD_DIM, rowi < HEAD_DIM, rowi == CHUNK - 1
        d_dt = jnp.zeros((CHUNK, LANE), F32)
        d_acum = jnp.zeros((CHUNK, LANE), F32)
        d_acum_t = jnp.zeros((CHUNK, LANE), F32)

        def half_sums(v):
            lo = jnp.sum(jnp.where(lane_lo, v, 0.0), axis=1, keepdims=True)
            return lo, jnp.sum(v, axis=1, keepdims=True) - lo

        for g in range(2):
            bcols = slice(SSM_INNER + g * SSM_STATE, SSM_INNER + (g + 1) * SSM_STATE)
            ccols = slice(SSM_INNER + BC_DIM + g * SSM_STATE, SSM_INNER + BC_DIM + (g + 1) * SSM_STATE)
            bg, cg = xa_ref[:, bcols].astype(BF16), xa_ref[:, ccols].astype(BF16)
            cb = _dot_nt(cg, bg)
            dg_sum = jnp.zeros((CHUNK, CHUNK), F32)
            dcg = jnp.zeros((CHUNK, SSM_STATE), F32)
            dbg = jnp.zeros((CHUNK, SSM_STATE), F32)
            for j in range(4 * g, 4 * g + 4):
                h0 = 2 * j
                cols = slice(j * LANE, (j + 1) * LANE)
                xp, dyp = xa_ref[:, cols], dy_ref[:, cols]
                dtsel = _pair_sel(lane_lo, dt, h0)
                xdt = xp * dtsel
                xdt_b = xdt.astype(BF16)
                hp, dhp = hs_ref[cols, :], dh[cols, :]
                hp_b, dhp_b = hp.astype(BF16), dhp.astype(BF16)
                easel, dendsel = _pair_sel(lane_lo, ea, h0), _pair_sel(lane_lo, dend, h0)
                dx = None
                for hh, sel in ((h0, lane_lo), (h0 + 1, ~lane_lo)):
                    dyh = jnp.where(sel, dyp, 0.0).astype(BF16)
                    seg = acum[:, hh:hh + 1] - acum_t[hh:hh + 1, :]
                    dec = jnp.where(causal, jnp.exp(jnp.minimum(seg, 0.0)), 0.0)
                    mm = cb * dec
                    e = _dot_nt(dyh, xdt_b)
                    d = _dot_tn(mm.astype(BF16), dyh)
                    dx = d if dx is None else dx + d
                    dg_sum = dg_sum + dec * e
                    qm = mm * e
                    d_acum = d_acum + jnp.where(lane == hh, jnp.sum(qm, axis=1, keepdims=True), 0.0)
                    d_acum_t = d_acum_t - jnp.where(rowi == hh, jnp.sum(qm, axis=0, keepdims=True), 0.0)
                g2 = _dot_nt(bg, dhp_b)
                dx = dx + g2 * dendsel
                yoff = _dot_nt(cg, hp_b) * easel
                t_lo, t_hi = half_sums(xdt * g2 * dendsel)
                y_lo, y_hi = half_sums(dyp * yoff)
                hh_prod = dhp * hp
                s_lo = jnp.sum(jnp.where(row_lo, hh_prod, 0.0), keepdims=True).reshape(1, 1)
                s_hi = jnp.sum(hh_prod, keepdims=True).reshape(1, 1) - s_lo
                end_lo = ea_last[:, h0:h0 + 1] * s_lo + jnp.sum(t_lo, axis=0, keepdims=True)
                end_hi = ea_last[:, h0 + 1:h0 + 2] * s_hi + jnp.sum(t_hi, axis=0, keepdims=True)
                d_acum = d_acum + jnp.where(lane == h0, y_lo - t_lo + jnp.where(last_row, end_lo, 0.0), 0.0)
                d_acum = d_acum + jnp.where(lane == h0 + 1, y_hi - t_hi + jnp.where(last_row, end_hi, 0.0), 0.0)
                dye = (dyp * easel).astype(BF16)
                dcg = dcg + _dot(dye, hp_b)
                dbg = dbg + _dot((xdt * dendsel).astype(BF16), dhp_b)
                rowf = jnp.where(row_lo, ea_last[:, h0:h0 + 1], ea_last[:, h0 + 1:h0 + 2])
                dh[cols, :] = dhp * rowf + _dot_tn(dye, cg)
                dact_ref[:, cols] = dx * dtsel + dsk_ref[:, cols] * dyp
                x_lo, x_hi = half_sums(dx * xp)
                d_dt = d_dt + jnp.where(lane == h0, x_lo, 0.0) + jnp.where(lane == h0 + 1, x_hi, 0.0)
                ddsk_ref[:, cols] += jnp.sum(dyp * xp, axis=0, keepdims=True)
            dg_b = dg_sum.astype(BF16)
            dact_ref[:, ccols] = dcg + _dot(dg_b, bg)
            dact_ref[:, bcols] = dbg + _dot_tn(dg_b, cg)
        d_adt = _cum(_tri(False), d_acum + d_acum_t.T)
        d_dt = d_dt + d_adt * a_neg
        da_ref[...] += jnp.sum(d_adt * dt, axis=0, keepdims=True)
        d_raw = jnp.where(lane < SSM_HEADS, d_dt * _sigmoid(dtr_ref[...] + bias_ref[...]), 0.0)
        ddtr_ref[...] = d_raw.astype(BF16)
        dbias_ref[...] += jnp.sum(d_raw, axis=0, keepdims=True)

    rev = lambda c: (nch - 1 - c, 0)
    return pl.pallas_call(
        kern, name="ssd_bwd", grid=(nch,),
        in_specs=[pl.BlockSpec((CHUNK, XBC_DIM), rev), pl.BlockSpec((CHUNK, LANE), lambda c: (nch - 1 - c, OFF_DT // LANE)),
                  pl.BlockSpec((CHUNK, LANE), rev), pl.BlockSpec((None, SSM_INNER, SSM_STATE), lambda c: (nch - 1 - c, 0, 0)),
                  pl.BlockSpec((CHUNK, SSM_INNER), rev),
                  pl.BlockSpec((1, LANE), lambda c: (0, 0)), pl.BlockSpec((1, LANE), lambda c: (0, 0)),
                  pl.BlockSpec((1, SSM_INNER), lambda c: (0, 0))],
        out_specs=[pl.BlockSpec((CHUNK, XBC_DIM), rev), pl.BlockSpec((CHUNK, LANE), rev),
                   pl.BlockSpec((1, LANE), lambda c: (0, 0)), pl.BlockSpec((1, LANE), lambda c: (0, 0)),
                   pl.BlockSpec((1, SSM_INNER), lambda c: (0, 0))],
        out_shape=[jax.ShapeDtypeStruct((t, XBC_DIM), F32), jax.ShapeDtypeStruct((t, LANE), BF16),
                   jax.ShapeDtypeStruct((1, LANE), F32), jax.ShapeDtypeStruct((1, LANE), F32),
                   jax.ShapeDtypeStruct((1, SSM_INNER), F32)],
        scratch_shapes=[pltpu.VMEM((SSM_INNER, SSM_STATE), F32)],
        compiler_params=_cparams(("arbitrary",)))(xbc_act, proj, dt_sp, hstates, dy, dt_bias_p, a_log_p, dskip_t)


def _ssm_post_fwd(y, proj, g):
    def body(y_ref, z_ref, g_ref, o_ref):
        z = z_ref[...]
        yz = y_ref[...] * (z * _sigmoid(z))
        r = lax.rsqrt(jnp.mean(yz * yz, axis=-1, keepdims=True) + EPS)
        o_ref[...] = (yz * r * g_ref[...]).astype(BF16)
    return _rows("ssm_post_fwd", body, [("t", y), ("tc", proj, SSM_INNER, OFF_Z // SSM_INNER), ("p", g)],
                 [(SSM_INNER, BF16)])[0]


def _ssm_post_bwd(dmix, y, proj, g):
    def body(do_ref, y_ref, z_ref, g_ref, dy_ref, dz_ref, dg_ref):
        z, yv, dout = z_ref[...], y_ref[...], do_ref[...]
        sg = _sigmoid(z)
        gz = z * sg
        yz = yv * gz
        r = lax.rsqrt(jnp.mean(yz * yz, axis=-1, keepdims=True) + EPS)
        gd = dout * g_ref[...]
        dyz = r * gd - yz * (r * r * r * jnp.mean(yz * gd, axis=-1, keepdims=True))
        dy_ref[...] = dyz * gz
        dz_ref[...] = (dyz * yv * (sg * (1.0 + z * (1.0 - sg)))).astype(BF16)
        dg_ref[...] += jnp.sum(dout * yz * r, axis=0, keepdims=True)
    return _rows("ssm_post_bwd", body,
                 [("tc", dmix, SSM_INNER, 0), ("t", y), ("tc", proj, SSM_INNER, OFF_Z // SSM_INNER), ("p", g)],
                 [(SSM_INNER, F32), (SSM_INNER, BF16)], accs=[(1, SSM_INNER)])


def _ple_loss(gl, pp, x2, tgt):
    d = x2.shape[1]

    def body(gl_ref, pp_ref, x_ref, t_ref, dy_ref, dgl_ref, dpp_ref, sq_ref):
        s = _sigmoid(gl_ref[...])
        ppv = pp_ref[...]
        diff = x_ref[...] + s * ppv - t_ref[...]
        dy = diff * (1.0 / d)
        dy_ref[...] = dy
        dgl_ref[...] = (dy * ppv * s * (1.0 - s)).astype(BF16)
        dpp_ref[...] = (dy * s).astype(BF16)
        sq_ref[...] += jnp.sum(diff * diff, axis=0, keepdims=True)
    return _rows("ple_loss", body, [("t", gl), ("t", pp), ("t", x2), ("t", tgt)], [(d, F32), (d, BF16), (d, BF16)],
                 accs=[(1, d)])


def _pad_lanes(v, width=LANE):
    return jnp.pad(v, ((0, 0), (0, width - v.shape[1])))


def _local_step(x, p, tgt, wts):
    g_attn, g_ssm, g_ffn, g_ple = wts["attn_norm_g"], wts["ssm_norm_g"], wts["ffn_norm_g"], wts["ple_norm_g"]
    w_in_p, w_out_s, w_out_a = wts["w_in_p"], wts["w_out_ssm"], wts["w_out_attn"]
    w_up, w_down, w_gate, w_proj = wts["w_up"], wts["w_down"], wts["w_ple_gate"], wts["w_ple_proj"]
    gq_t = jnp.tile(wts["q_norm_g"], (1, ATTN_DIM // HEAD_DIM))
    gk_t = jnp.tile(wts["k_norm_g"], (1, KV_DIM // HEAD_DIM))
    dt_bias_p, a_log_p = _pad_lanes(wts["dt_bias"]), _pad_lanes(wts["a_log"])
    dskip_t = jnp.repeat(wts["d_skip"], HEAD_DIM, axis=1)

    h1 = _rms_fwd("rms_attn", x, g_attn)
    proj = _mm_nn("in_proj", [(h1, w_in_p)], F32)
    qn, kn, vb = _qknorm_fwd(proj, gq_t, gk_t)
    pats = [_attn_fwd(qn, kn, vb, d) for d in DILATIONS]
    attn_out, lse = _attn_merge([o for o, _ in pats], [l for _, l in pats])
    xbc_act = _ssm_conv_fwd(proj, wts["ssm_conv_w"], wts["ssm_conv_b"])
    y_ssd, dt_sp, hstates = _ssd_fwd(xbc_act, proj, dt_bias_p, a_log_p, dskip_t)
    ssm_out = _ssm_post_fwd(y_ssd, proj, g_ssm)
    x1 = _mm_nn("out_proj", [(ssm_out, w_out_s), (attn_out, w_out_a)], F32, res=x)
    h2 = _rms_fwd("rms_ffn", x1, g_ffn)
    u = _mm_nn("ffn_up", [(h2, w_up)], F32, tn=1408, halves=True)
    a = _ffn_act_fwd(u, wts["ffn_conv_w"], wts["ffn_conv_b"])
    x2 = _mm_nn("ffn_down", [(a, w_down)], F32, res=x1)
    h3 = _rms_fwd("rms_ple", x2, g_ple)
    gl = _mm_nn("ple_gate", [(h3, w_gate)], F32)
    pb = p.astype(BF16)
    pp = _mm_nn("ple_proj", [(pb, w_proj)], F32)
    dy, dgl, dpp, sq = _ple_loss(gl, pp, x2, tgt)

    grads = {}
    grads["w_ple_proj"] = _mm_tn("g_ple_proj", pb, dpp)
    grads["w_ple_gate"] = _mm_tn("g_ple_gate", h3, dgl)
    dh3 = _mm_nt("d_h3", [(dgl, w_gate, 0)], F32)
    dx2, dx2b, grads["ple_norm_g"] = _rms_bwd("rms_ple_bwd", dh3, x2, g_ple, dy)
    da = _mm_nt("d_ffn_act", [(dx2b, w_down, 0)], F32, tn=1408)
    grads["w_down"] = _mm_tn("g_ffn_down", a, dx2b, tm=1408)
    du, gwg, gwv, gbg, gbv = _ffn_act_bwd(u, wts["ffn_conv_w"], wts["ffn_conv_b"], da)
    grads["ffn_conv_w"] = jnp.concatenate([gwg, gwv], axis=1)
    grads["ffn_conv_b"] = jnp.concatenate([gbg, gbv], axis=1)
    grads["w_up"] = jnp.concatenate([_mm_tn("g_ffn_up_gate", h2, du[0], tn=1408), _mm_tn("g_ffn_up_val", h2, du[1], tn=1408)],
                                    axis=1)
    dh2 = _mm_nt("d_h2", [(du[0], w_up, 0), (du[1], w_up, 1)], F32)
    dx1, dx1b, grads["ffn_norm_g"] = _rms_bwd("rms_ffn_bwd", dh2, x1, g_ffn, dx2)
    dmix = _mm_nt("d_mix", [(dx1b, jnp.concatenate([w_out_s, w_out_a], axis=0), 0)], F32)
    grads["w_out"] = jnp.concatenate([_mm_tn("g_out_attn", attn_out, dx1b), _mm_tn("g_out_ssm", ssm_out, dx1b)], axis=0)
    dy_ssd, dz, grads["ssm_norm_g"] = _ssm_post_bwd(dmix, y_ssd, proj, g_ssm)
    dact, ddtr, d_a, d_bias, d_dsk = _ssd_bwd(xbc_act, proj, dt_sp, hstates, dy_ssd, dt_bias_p, a_log_p, dskip_t)
    grads["dt_bias"] = d_bias[:, :SSM_HEADS]
    grads["a_log"] = d_a[:, :SSM_HEADS] * (-jnp.exp(wts["a_log"]))
    grads["d_skip"] = jnp.sum(d_dsk.reshape(SSM_HEADS, HEAD_DIM), axis=1)[None, :]
    dxbc, grads["ssm_conv_w"], grads["ssm_conv_b"] = _ssm_conv_bwd(proj, wts["ssm_conv_w"], wts["ssm_conv_b"], dact)
    dob, dsum = _attn_bwd_prep(dmix, attn_out)
    dqs = [_attn_dq(qn, kn, vb, dob, lse, dsum, d) for d in DILATIONS]
    dkvs = [_attn_dkv(qn, kn, vb, dob, lse, dsum, d) for d in DILATIONS]
    dq, dk, dv, dgq, dgk = _qknorm_bwd(proj, gq_t, gk_t, dqs, [a_ for a_, _ in dkvs], [b_ for _, b_ in dkvs])
    grads["q_norm_g"] = jnp.sum(dgq.reshape(ATTN_DIM // HEAD_DIM, HEAD_DIM), axis=0)[None, :]
    grads["k_norm_g"] = jnp.sum(dgk.reshape(KV_DIM // HEAD_DIM, HEAD_DIM), axis=0)[None, :]
    dproj = jnp.concatenate([dxbc, dq, dz, dk, dv, ddtr], axis=1)
    grads["w_in_p"] = _mm_tn("g_in_proj", h1, dproj, tm=512)
    dh1 = _mm_nt("d_h1", [(dproj, w_in_p, 0)], F32)
    grad_x, _, grads["attn_norm_g"] = _rms_bwd("rms_attn_bwd", dh1, x, g_attn, dx1)
    return sq, grad_x, grads


MESH_IDS = pl.DeviceIdType.MESH
N_CHIPS = 4
ANY_SPEC = pl.BlockSpec(memory_space=pl.ANY)
PACK_ROWS = 3840
HALF_ROWS = PACK_ROWS // 2
SMALL_ROWS = 96


def _place():
    x, y, c = lax.axis_index("x"), lax.axis_index("y"), lax.axis_index("c")
    return x, y, c, [(1 - x, y), (x, 1 - y), (1 - x, 1 - y)]


def _gather_over_chips(arrs):
    n = len(arrs)
    split = [a.shape[0] % 64 == 0 for a in arrs]

    def body(*refs):
        ins, outs = refs[:n], refs[n:2 * n]
        ici_send, ici_recv, d2d_send, d2d_recv = refs[2 * n:2 * n + 4]
        x, y, c, chips = _place()
        mine = 2 * x + y

        def part(ref, a, core):
            if not split[a]:
                return ref
            half = arrs[a].shape[0] // 2
            return ref.at[pl.ds(core * half, half)]

        def ici(a, k, src_chip_slot, core):
            px, py = chips[k]
            return pltpu.make_async_remote_copy(
                src_ref=part(ins[a], a, core), dst_ref=part(outs[a].at[src_chip_slot], a, core), send_sem=ici_send.at[3 * a + k],
                recv_sem=ici_recv.at[3 * a + k], device_id=(px, py, c), device_id_type=MESH_IDS)

        def d2d(a, k, core):
            px, py = chips[k]
            piece = part(outs[a].at[2 * px + py], a, core)
            return pltpu.make_async_remote_copy(src_ref=piece, dst_ref=piece, send_sem=d2d_send.at[3 * a + k],
                                                recv_sem=d2d_recv.at[3 * a + k], device_id=(x, y, 1 - c), device_id_type=MESH_IDS)

        for a in range(n):
            for k in range(3):
                ici(a, k, mine, c).start()
        passed = []
        for a in range(n):
            for k, (px, py) in enumerate(chips):
                ici(a, k, 2 * px + py, c).wait_recv()
                if split[a]:
                    fwd = d2d(a, k, c)
                    fwd.start()
                    passed.append(fwd)
        for a in range(n):
            for k in range(3):
                if split[a]:
                    d2d(a, k, 1 - c).wait_recv()
                ici(a, k, mine, c).wait_send()
        for fwd in passed:
            fwd.wait_send()

    sems = [pltpu.SemaphoreType.DMA((3 * n,))] * 4
    return pl.pallas_call(
        body, name="gather_weights", in_specs=[ANY_SPEC] * n, out_specs=[ANY_SPEC] * n,
        out_shape=[jax.ShapeDtypeStruct((N_CHIPS,) + a.shape, a.dtype) for a in arrs], scratch_shapes=sems)(*arrs)


def _swap_halves(g):
    def body(g_ref, o_ref, send, recv):
        x, y, c, _ = _place()
        cps = [pltpu.make_async_remote_copy(src_ref=g_ref.at[q, 1 - c], dst_ref=o_ref.at[q], send_sem=send.at[q], recv_sem=recv.at[q],
                                            device_id=(x, y, 1 - c), device_id_type=MESH_IDS) for q in range(N_CHIPS)]
        for cp in cps:
            cp.start()
        for cp in cps:
            cp.wait()

    return pl.pallas_call(
        body, name="grad_swap_halves", in_specs=[ANY_SPEC], out_specs=ANY_SPEC,
        out_shape=jax.ShapeDtypeStruct((N_CHIPS,) + g.shape[2:], g.dtype),
        scratch_shapes=[pltpu.SemaphoreType.DMA((N_CHIPS,)), pltpu.SemaphoreType.DMA((N_CHIPS,))])(g)


def _add_halves(g, got, c_idx, tm=384):
    rows = g.shape[2]

    def kern(c_ref, g_ref, r_ref, o_ref):
        o_ref[...] = (g_ref[...] + r_ref[...]).astype(BF16)

    return pl.pallas_call(
        kern, name="grad_add_halves",
        grid_spec=pltpu.PrefetchScalarGridSpec(
            num_scalar_prefetch=1, grid=(N_CHIPS, rows // tm),
            in_specs=[pl.BlockSpec((None, None, tm, D_MODEL), lambda q, i, c_ref: (q, c_ref[0], i, 0)),
                      pl.BlockSpec((None, tm, D_MODEL), lambda q, i, c_ref: (q, i, 0))],
            out_specs=pl.BlockSpec((None, tm, D_MODEL), lambda q, i, c_ref: (q, i, 0))),
        out_shape=jax.ShapeDtypeStruct((N_CHIPS, rows, D_MODEL), BF16),
        compiler_params=_cparams(("parallel", "parallel")))(c_idx, g, got)


def _scatter_over_chips(s):
    def body(s_ref, o_ref, send, recv):
        x, y, c, chips = _place()
        mine = 2 * x + y
        for k, (px, py) in enumerate(chips):
            pltpu.make_async_remote_copy(src_ref=s_ref.at[2 * px + py], dst_ref=o_ref.at[mine], send_sem=send.at[k], recv_sem=recv.at[k],
                                         device_id=(px, py, c), device_id_type=MESH_IDS).start()
        for k, (px, py) in enumerate(chips):
            pltpu.make_async_remote_copy(src_ref=s_ref.at[2 * px + py], dst_ref=o_ref.at[2 * px + py], send_sem=send.at[k],
                                         recv_sem=recv.at[k], device_id=(px, py, c), device_id_type=MESH_IDS).wait()

    return pl.pallas_call(
        body, name="grad_scatter_chips", in_specs=[ANY_SPEC], out_specs=ANY_SPEC,
        out_shape=jax.ShapeDtypeStruct(s.shape, s.dtype),
        scratch_shapes=[pltpu.SemaphoreType.DMA((3,)), pltpu.SemaphoreType.DMA((3,))])(s)


def _sum_chips(own, parts, order, tm=384):
    rows = parts.shape[1]

    def kern(o_idx, a_ref, b_ref, c_ref, d_ref, o_ref):
        o_ref[...] = ((a_ref[...].astype(F32) + b_ref[...].astype(F32)) + c_ref[...].astype(F32)) + d_ref[...].astype(F32)

    def spec(k):
        return pl.BlockSpec((None, tm, D_MODEL), functools.partial(lambda i, o_idx, k: (o_idx[k], i, 0), k=k))

    return pl.pallas_call(
        kern, name="grad_sum_chips",
        grid_spec=pltpu.PrefetchScalarGridSpec(
            num_scalar_prefetch=1, grid=(rows // tm,), in_specs=[spec(0), spec(1), spec(2), spec(3)],
            out_specs=pl.BlockSpec((tm, D_MODEL), lambda i, o_idx: (i, 0))),
        out_shape=jax.ShapeDtypeStruct((rows, D_MODEL), F32), compiler_params=_cparams(("parallel",)))(order, own, parts, parts, parts)


def _share_with_sibling(s):
    def body(s_ref, o_ref, send, recv):
        x, y, c, _ = _place()
        cp = pltpu.make_async_remote_copy(src_ref=s_ref, dst_ref=o_ref, send_sem=send, recv_sem=recv,
                                          device_id=(x, y, 1 - c), device_id_type=MESH_IDS)
        cp.start()
        cp.wait()

    return pl.pallas_call(
        body, name="grad_share_sibling", in_specs=[ANY_SPEC], out_specs=ANY_SPEC,
        out_shape=jax.ShapeDtypeStruct(s.shape, s.dtype),
        scratch_shapes=[pltpu.SemaphoreType.DMA, pltpu.SemaphoreType.DMA])(s)


def _allreduce_small(v):
    def body(v_ref, o_ref, land, send, recv):
        x, y, c, _ = _place()
        me = 4 * x + 2 * y + c
        land[me] = v_ref[...]
        cps = []
        for rel in range(1, 8):
            bx, by, bc = (rel >> 2) & 1, (rel >> 1) & 1, rel & 1
            peer = (1 - x if bx else x, 1 - y if by else y, 1 - c if bc else c)
            cps.append(pltpu.make_async_remote_copy(src_ref=v_ref, dst_ref=land.at[me], send_sem=send.at[rel - 1],
                                                    recv_sem=recv.at[rel - 1], device_id=peer, device_id_type=MESH_IDS))
        for cp in cps:
            cp.start()
        for cp in cps:
            cp.wait()
        acc = land[0]
        for d in range(1, 8):
            acc = acc + land[d]
        o_ref[...] = acc

    vm = pl.BlockSpec(memory_space=pltpu.VMEM)
    return pl.pallas_call(
        body, name="allreduce_small", in_specs=[vm], out_specs=vm, out_shape=jax.ShapeDtypeStruct(v.shape, F32),
        scratch_shapes=[pltpu.VMEM((8,) + v.shape, F32), pltpu.SemaphoreType.DMA((7,)), pltpu.SemaphoreType.DMA((7,))])(v)


def _adamw(name, w, g, m, v):
    rows, cols = w.shape
    tm = rows
    if rows * cols > 128 * 1024:
        tm = max(d for d in range(8, 257, 8) if rows % d == 0)
    c1 = 1.0 / (1.0 - ADAM_B1 ** ADAM_STEP)
    c2 = 1.0 / (1.0 - ADAM_B2 ** ADAM_STEP)

    def kern(w_ref, g_ref, m_ref, v_ref, d_ref, mo_ref, vo_ref):
        gv = g_ref[...]
        mn = ADAM_B1 * m_ref[...] + (1.0 - ADAM_B1) * gv
        vn = ADAM_B2 * v_ref[...] + (1.0 - ADAM_B2) * (gv * gv)
        d_ref[...] = -ADAM_LR * ((mn * c1) / (jnp.sqrt(vn * c2) + ADAM_EPS) + ADAM_WD * w_ref[...])
        mo_ref[...] = mn
        vo_ref[...] = vn

    spec = pl.BlockSpec((tm, cols), lambda i: (i, 0))
    return pl.pallas_call(
        kern, name=name, grid=(rows // tm,), in_specs=[spec] * 4, out_specs=[spec] * 3,
        out_shape=[jax.ShapeDtypeStruct(w.shape, F32)] * 3, compiler_params=_cparams(("parallel",)))(w, g, m, v)


SHARDED = (("w_in", 1), ("w_out", 0), ("w_up", 1), ("w_down", 0), ("w_ple_gate", 0), ("w_ple_proj", 1),
           ("ssm_conv_w", 1), ("ffn_conv_w", 1))
MATRICES = ("w_in", "w_out", "w_up", "w_down", "w_ple_gate", "w_ple_proj")
REPLICATED = ("attn_norm_g", "q_norm_g", "k_norm_g", "ssm_conv_b", "dt_bias", "a_log", "d_skip", "ssm_norm_g",
              "ffn_norm_g", "ffn_conv_b", "ple_norm_g")
WEIGHT_ORDER = ("attn_norm_g", "w_in", "q_norm_g", "k_norm_g", "ssm_conv_w", "ssm_conv_b", "dt_bias", "a_log", "d_skip",
                "ssm_norm_g", "w_out", "ffn_norm_g", "w_up", "ffn_conv_w", "ffn_conv_b", "w_down", "ple_norm_g",
                "w_ple_gate", "w_ple_proj")


def _join_chips(g, axis):
    if axis == 0:
        return g.reshape(g.shape[0] * g.shape[1], g.shape[2])
    return jnp.transpose(g, (1, 0, 2)).reshape(g.shape[1], g.shape[0] * g.shape[2])


def _split_chips(g, axis):
    if axis == 0:
        return g.reshape(N_CHIPS, -1)
    r, c = g.shape
    return jnp.transpose(g.reshape(r, N_CHIPS, c // N_CHIPS), (1, 0, 2)).reshape(N_CHIPS, -1)


def _pack_small(vals):
    flat = jnp.concatenate([v.reshape(-1) for v in vals])
    return jnp.pad(flat, (0, SMALL_ROWS * LANE - flat.shape[0])).reshape(SMALL_ROWS, LANE)


def _unpack_small(packed, like):
    flat, out, off = packed.reshape(-1), [], 0
    for v in like:
        out.append(flat[off:off + v.size].reshape(v.shape))
        off += v.size
    return out


def kernel(x, p, attn_norm_g, w_in, q_norm_g, k_norm_g, ssm_conv_w, ssm_conv_b, dt_bias, a_log, d_skip, ssm_norm_g, w_out, ffn_norm_g, w_up, ffn_conv_w, ffn_conv_b, w_down, ple_norm_g, w_ple_gate, w_ple_proj, loss_target, m_attn_norm_g, m_w_in, m_q_norm_g, m_k_norm_g, m_ssm_conv_w, m_ssm_conv_b, m_dt_bias, m_a_log, m_d_skip, m_ssm_norm_g, m_w_out, m_ffn_norm_g, m_w_up, m_ffn_conv_w, m_ffn_conv_b, m_w_down, m_ple_norm_g, m_w_ple_gate, m_w_ple_proj, v_attn_norm_g, v_w_in, v_q_norm_g, v_k_norm_g, v_ssm_conv_w, v_ssm_conv_b, v_dt_bias, v_a_log, v_d_skip, v_ssm_norm_g, v_w_out, v_ffn_norm_g, v_w_up, v_ffn_conv_w, v_ffn_conv_b, v_w_down, v_ple_norm_g, v_w_ple_gate, v_w_ple_proj):
    given = dict(locals())
    w2 = {n: given[n].reshape(given[n].shape[-2:]) if given[n].ndim == 3 else given[n] for n in WEIGHT_ORDER}
    m2 = {n: given["m_" + n].reshape(w2[n].shape) for n in WEIGHT_ORDER}
    v2 = {n: given["v_" + n].reshape(w2[n].shape) for n in WEIGHT_ORDER}

    cx, cy, cc = lax.axis_index("x"), lax.axis_index("y"), lax.axis_index("c")
    chip = 2 * cx + cy
    shards = [w2[n].astype(BF16) if n in MATRICES else w2[n] for n, _ in SHARDED]
    full = {n: _join_chips(lax.dynamic_update_index_in_dim(g, s, chip, 0), ax)
            for (n, ax), g, s in zip(SHARDED, _gather_over_chips(shards), shards)}
    win = full["w_in"]
    w_in_p = jnp.concatenate([win[:, 2048:3584], win[:, 0:512], win[:, 1024:2048], win[:, 512:768], win[:, 768:1024],
                              win[:, 3584:3600], jnp.zeros((D_MODEL, PROJ_P - IN_PROJ), BF16)], axis=1)
    wts = {n: w2[n] for n in REPLICATED}
    wts.update(w_in_p=w_in_p, w_out_attn=full["w_out"][:ATTN_DIM], w_out_ssm=full["w_out"][ATTN_DIM:], w_up=full["w_up"],
               w_down=full["w_down"], w_ple_gate=full["w_ple_gate"], w_ple_proj=full["w_ple_proj"],
               ssm_conv_w=full["ssm_conv_w"], ffn_conv_w=full["ffn_conv_w"])

    sq, grad_x, grads = _local_step(x[0], p[0, 0], loss_target[0], wts)
    gi = grads.pop("w_in_p")
    grads["w_in"] = jnp.concatenate([gi[:, OFF_Q:OFF_Q + ATTN_DIM], gi[:, OFF_K:OFF_K + KV_DIM], gi[:, OFF_V:OFF_V + KV_DIM],
                                     gi[:, OFF_Z:OFF_Z + SSM_INNER], gi[:, OFF_XBC:OFF_XBC + XBC_DIM], gi[:, OFF_DT:OFF_DT + SSM_HEADS]],
                                    axis=1)

    packed = jnp.concatenate([_split_chips(grads[n], ax) for n, ax in SHARDED], axis=1)
    packed = jnp.pad(packed, ((0, 0), (0, PACK_ROWS * D_MODEL - packed.shape[1]))).reshape(N_CHIPS, 2, HALF_ROWS, D_MODEL)
    chip_sums = _add_halves(packed, _swap_halves(packed), cc.astype(jnp.int32).reshape(1))
    order = jnp.stack([chip, 2 * (1 - cx) + cy, 2 * cx + (1 - cy), 2 * (1 - cx) + (1 - cy)]).astype(jnp.int32)
    mine_half = _sum_chips(chip_sums, _scatter_over_chips(chip_sums), order)
    other_half = _share_with_sibling(mine_half)
    reduced = jnp.where(cc == 0, jnp.stack([mine_half, other_half]), jnp.stack([other_half, mine_half])).reshape(-1)
    g_shard, off = {}, 0
    for n, _ in SHARDED:
        g_shard[n] = reduced[off:off + w2[n].size].reshape(w2[n].shape)
        off += w2[n].size

    small = _allreduce_small(_pack_small([grads[n] for n in REPLICATED] + [jnp.sum(sq).reshape(1)]))
    small_vals = _unpack_small(small, [w2[n] for n in REPLICATED] + [jnp.zeros((1,), F32)])
    for n, g in zip(REPLICATED, small_vals):
        g_shard[n] = g
    loss = (0.5 / D_MODEL) * small_vals[-1][0]

    delta, new_m, new_v = {}, {}, {}
    for n, _ in SHARDED:
        delta[n], new_m[n], new_v[n] = _adamw("adamw_" + n, w2[n], g_shard[n], m2[n], v2[n])
    sm = _adamw("adamw_small", _pack_small([w2[n] for n in REPLICATED]), _pack_small([g_shard[n] for n in REPLICATED]),
                _pack_small([m2[n] for n in REPLICATED]), _pack_small([v2[n] for n in REPLICATED]))
    for dst, packed_out in zip((delta, new_m, new_v), sm):
        for n, val in zip(REPLICATED, _unpack_small(packed_out, [w2[n] for n in REPLICATED])):
            dst[n] = val

    def shaped(d):
        return [d[n].reshape(given[n].shape) for n in WEIGHT_ORDER]
    return (loss, grad_x[None], *shaped(g_shard), *shaped(delta), *shaped(new_m), *shaped(new_v))
```

```python
import functools

import jax
import jax.numpy as jnp
from jax import lax
from jax.experimental import pallas as pl
from jax.experimental.pallas import tpu as pltpu

F32 = jnp.float32
BF16 = jnp.bfloat16

D_MODEL = 1024
HEAD_DIM = 64
ATTN_DIM = 512
KV_DIM = 256
N_KV = 4
SSM_INNER = 1024
SSM_HEADS = 16
SSM_STATE = 128
BC_DIM = 256
XBC_DIM = SSM_INNER + 2 * BC_DIM
MIX_DIM = ATTN_DIM + SSM_INNER
IN_PROJ = 3600
D_FF = 2816
PLE_DIM = 256
CHUNK = 128
DILATIONS = (1, 4, 16)
EPS = 1e-6
ADAM_LR, ADAM_B1, ADAM_B2, ADAM_EPS, ADAM_WD, ADAM_STEP = 0.001, 0.9, 0.999, 1e-08, 0.01, 10

PROJ_P = 3712
OFF_XBC, OFF_Q, OFF_Z, OFF_K, OFF_V, OFF_DT = 0, 1536, 2048, 3072, 3328, 3584
LANE = 128
VMEM_LIMIT = 48 * 1024 * 1024
NEG = -1e30


def _cparams(sem):
    return pltpu.CompilerParams(dimension_semantics=sem, vmem_limit_bytes=VMEM_LIMIT)


def _sigmoid(x):
    return 1.0 / (1.0 + jnp.exp(-x))


def _dot(a, b):
    return jnp.dot(a, b, preferred_element_type=F32)


def _dot_nt(a, b):
    return lax.dot_general(a, b, (((1,), (1,)), ((), ())), preferred_element_type=F32)


def _dot_tn(a, b):
    return lax.dot_general(a, b, (((0,), (0,)), ((), ())), preferred_element_type=F32)


def _dot_split(x, m):
    hi = x.astype(BF16)
    lo = (x - hi.astype(F32)).astype(BF16)
    return _dot(hi, m) + _dot(lo, m)


def _rows(name, body, ins, outs, accs=(), tm=512):
    t_rows = next(s[1].shape[0] for s in ins if s[0] in ("t", "tc"))
    tm = min(tm, t_rows)
    in_specs, args = [], []
    for s in ins:
        if s[0] == "t":
            in_specs.append(pl.BlockSpec((tm, s[1].shape[1]), lambda i: (i, 0)))
        elif s[0] == "tc":
            in_specs.append(pl.BlockSpec((tm, s[2]), functools.partial(lambda i, c: (i, c), c=s[3])))
        else:
            in_specs.append(pl.BlockSpec(s[1].shape, lambda i: (0, 0)))
        args.append(s[1])
    out_shape = [jax.ShapeDtypeStruct((t_rows, w), dt) for w, dt in outs]
    out_specs = [pl.BlockSpec((tm, w), lambda i: (i, 0)) for w, _ in outs]
    out_shape += [jax.ShapeDtypeStruct(a, F32) for a in accs]
    out_specs += [pl.BlockSpec(a, lambda i: (0, 0)) for a in accs]
    n_acc = len(accs)

    def kern(*refs):
        if n_acc:
            @pl.when(pl.program_id(0) == 0)
            def _():
                for r in refs[len(refs) - n_acc:]:
                    r[...] = jnp.zeros(r.shape, F32)
        body(*refs)

    return pl.pallas_call(
        kern, name=name, grid=(t_rows // tm,), in_specs=in_specs, out_specs=out_specs, out_shape=out_shape,
        compiler_params=_cparams(("arbitrary",) if n_acc else ("parallel",)))(*args)


NCHUNK = 512


def _col_chunks(n):
    return [(c, min(NCHUNK, n - c)) for c in range(0, n, NCHUNK)]


def _mm_nn(name, pairs, out_dtype, res=None, tm=512, tn=None, halves=False):
    m, n = pairs[0][0].shape[0], pairs[0][1].shape[1]
    tn = n if tn is None else tn
    tm = min(tm, m)
    np_ = len(pairs)
    if halves:
        per = n // 2 // tn
        out_spec = pl.BlockSpec((None, tm, tn), lambda j, i: (j // per, i, j % per))
        out_shape = jax.ShapeDtypeStruct((2, m, n // 2), out_dtype)
    else:
        out_spec = pl.BlockSpec((tm, tn), lambda j, i: (i, j))
        out_shape = jax.ShapeDtypeStruct((m, n), out_dtype)
    in_specs, args = [], []
    for a, w in pairs:
        in_specs += [pl.BlockSpec((tm, a.shape[1]), lambda j, i: (i, 0)), pl.BlockSpec((w.shape[0], tn), lambda j, i: (0, j))]
        args += [a, w]
    if res is not None:
        in_specs.append(pl.BlockSpec((tm, tn), lambda j, i: (i, j)))
        args.append(res)

    def kern(*refs):
        o_ref = refs[-1]
        for c0, cw in _col_chunks(tn):
            acc = None
            for q in range(np_):
                d = _dot(refs[2 * q][...], refs[2 * q + 1][:, c0:c0 + cw])
                acc = d if acc is None else acc + d
            if res is not None:
                acc = acc + refs[2 * np_][:, c0:c0 + cw]
            o_ref[:, c0:c0 + cw] = acc.astype(o_ref.dtype)

    return pl.pallas_call(
        kern, name=name, grid=(n // tn, m // tm), in_specs=in_specs, out_specs=out_spec, out_shape=out_shape,
        compiler_params=_cparams(("parallel", "parallel")))(*args)


def _mm_nt(name, pairs, out_dtype, tm=512, tn=None):
    m, n = pairs[0][0].shape[0], pairs[0][1].shape[0]
    tn = n if tn is None else tn
    tm = min(tm, m)
    np_ = len(pairs)
    in_specs, args = [], []
    for a, w, kb in pairs:
        in_specs += [pl.BlockSpec((tm, a.shape[1]), lambda j, i: (i, 0)),
                     pl.BlockSpec((tn, a.shape[1]), functools.partial(lambda j, i, kb: (j, kb), kb=kb))]
        args += [a, w]

    def kern(*refs):
        o_ref = refs[-1]
        for c0, cw in _col_chunks(tn):
            acc = None
            for q in range(np_):
                d = _dot_nt(refs[2 * q][...], refs[2 * q + 1][c0:c0 + cw, :])
                acc = d if acc is None else acc + d
            o_ref[:, c0:c0 + cw] = acc.astype(o_ref.dtype)

    return pl.pallas_call(
        kern, name=name, grid=(n // tn, m // tm), in_specs=in_specs,
        out_specs=pl.BlockSpec((tm, tn), lambda j, i: (i, j)),
        out_shape=jax.ShapeDtypeStruct((m, n), out_dtype), compiler_params=_cparams(("parallel", "parallel")))(*args)


def _mm_tn(name, a, b, tm=None, tn=None, tk=1024):
    t, m = a.shape
    n = b.shape[1]
    tm = m if tm is None else tm
    tn = n if tn is None else tn
    tk = min(tk, t)

    def kern(a_ref, b_ref, o_ref):
        @pl.when(pl.program_id(2) == 0)
        def _():
            o_ref[...] = jnp.zeros(o_ref.shape, F32)
        for c0, cw in _col_chunks(tn):
            o_ref[:, c0:c0 + cw] += _dot_tn(a_ref[...], b_ref[:, c0:c0 + cw])

    return pl.pallas_call(
        kern, name=name, grid=(m // tm, n // tn, t // tk),
        in_specs=[pl.BlockSpec((tk, tm), lambda i, j, k: (k, i)), pl.BlockSpec((tk, tn), lambda i, j, k: (k, j))],
        out_specs=pl.BlockSpec((tm, tn), lambda i, j, k: (i, j)),
        out_shape=jax.ShapeDtypeStruct((m, n), F32),
        compiler_params=_cparams(("parallel", "parallel", "arbitrary")))(a, b)


def _rms_fwd(name, x, g):
    def body(x_ref, g_ref, h_ref):
        xv = x_ref[...]
        r = lax.rsqrt(jnp.mean(xv * xv, axis=-1, keepdims=True) + EPS)
        h_ref[...] = (xv * r * g_ref[...]).astype(BF16)
    return _rows(name, body, [("t", x), ("p", g)], [(x.shape[1], BF16)])[0]


def _rms_bwd(name, dh, x, g, dres):
    d = x.shape[1]

    def body(dh_ref, x_ref, g_ref, dres_ref, dx_ref, dxb_ref, dg_ref):
        xv, dhv = x_ref[...], dh_ref[...]
        r = lax.rsqrt(jnp.mean(xv * xv, axis=-1, keepdims=True) + EPS)
        gd = dhv * g_ref[...]
        dx = dres_ref[...] + r * gd - xv * (r * r * r * jnp.mean(xv * gd, axis=-1, keepdims=True))
        dx_ref[...] = dx
        dxb_ref[...] = dx.astype(BF16)
        dg_ref[...] += jnp.sum(dhv * xv * r, axis=0, keepdims=True)
    return _rows(name, body, [("t", dh), ("t", x), ("p", g), ("t", dres)], [(d, F32), (d, BF16)], accs=[(1, d)])


def _head_mean_matrix(width):
    i = jnp.arange(width) // HEAD_DIM
    return jnp.where(i[:, None] == i[None, :], 1.0 / HEAD_DIM, 0.0).astype(BF16)


def _qknorm_fwd(proj, gq_t, gk_t):
    bq, bk = _head_mean_matrix(ATTN_DIM), _head_mean_matrix(KV_DIM)
    scale = HEAD_DIM ** -0.5

    def body(q_ref, k_ref, v_ref, gq_ref, gk_ref, bq_ref, bk_ref, qn_ref, kn_ref, vb_ref):
        q, k = q_ref[...], k_ref[...]
        rq = lax.rsqrt(_dot_split(q * q, bq_ref[...]) + EPS)
        rk = lax.rsqrt(_dot_split(k * k, bk_ref[...]) + EPS)
        qn_ref[...] = ((q * rq * gq_ref[...]) * scale).astype(BF16)
        kn_ref[...] = (k * rk * gk_ref[...]).astype(BF16)
        vb_ref[...] = v_ref[...].astype(BF16)

    return _rows("qknorm_fwd", body,
                 [("tc", proj, ATTN_DIM, OFF_Q // ATTN_DIM), ("tc", proj, KV_DIM, OFF_K // KV_DIM),
                  ("tc", proj, KV_DIM, OFF_V // KV_DIM), ("p", gq_t), ("p", gk_t), ("p", bq), ("p", bk)],
                 [(ATTN_DIM, BF16), (KV_DIM, BF16), (KV_DIM, BF16)])


def _qknorm_bwd(proj, gq_t, gk_t, dqs, dks, dvs):
    bq, bk = _head_mean_matrix(ATTN_DIM), _head_mean_matrix(KV_DIM)
    scale = HEAD_DIM ** -0.5

    def body(q_ref, k_ref, gq_ref, gk_ref, bq_ref, bk_ref, dq1, dq2, dq3, dk1, dk2, dk3, dv1, dv2, dv3,
             dq_ref, dk_ref, dv_ref, dgq_ref, dgk_ref):
        q, k = q_ref[...], k_ref[...]
        dqn = (dq1[...] + dq2[...] + dq3[...]) * scale
        dkn = dk1[...] + dk2[...] + dk3[...]
        rq = lax.rsqrt(_dot_split(q * q, bq_ref[...]) + EPS)
        rk = lax.rsqrt(_dot_split(k * k, bk_ref[...]) + EPS)
        gdq, gdk = dqn * gq_ref[...], dkn * gk_ref[...]
        dq_ref[...] = (rq * gdq - q * (rq * rq * rq * _dot_split(q * gdq, bq_ref[...]))).astype(BF16)
        dk_ref[...] = (rk * gdk - k * (rk * rk * rk * _dot_split(k * gdk, bk_ref[...]))).astype(BF16)
        dv_ref[...] = (dv1[...] + dv2[...] + dv3[...]).astype(BF16)
        dgq_ref[...] += jnp.sum(dqn * q * rq, axis=0, keepdims=True)
        dgk_ref[...] += jnp.sum(dkn * k * rk, axis=0, keepdims=True)

    ins = [("tc", proj, ATTN_DIM, OFF_Q // ATTN_DIM), ("tc", proj, KV_DIM, OFF_K // KV_DIM),
           ("p", gq_t), ("p", gk_t), ("p", bq), ("p", bk)]
    ins += [("t", a) for a in dqs] + [("t", a) for a in dks] + [("t", a) for a in dvs]
    return _rows("qknorm_bwd", body, ins, [(ATTN_DIM, BF16), (KV_DIM, BF16), (KV_DIM, BF16)],
                 accs=[(1, ATTN_DIM), (1, KV_DIM)], tm=256)


def _head_cols(x, kh, width=HEAD_DIM):
    return x[:, kh * width:(kh + 1) * width]


def _stack_q(x, kh):
    return jnp.concatenate([x[:, 2 * kh * HEAD_DIM:(2 * kh + 1) * HEAD_DIM],
                            x[:, (2 * kh + 1) * HEAD_DIM:(2 * kh + 2) * HEAD_DIM]], axis=0)


def _stat_cols(stat, kh, rows):
    return jnp.concatenate([stat[:, 2 * kh:2 * kh + 1], stat[:, 2 * kh + 1:2 * kh + 2]], axis=0)


def _sub_view(a, dil):
    t, c = a.shape
    return a.reshape(t // (CHUNK * dil), CHUNK, dil * c)


def _attn_fwd(qn, kn, vb, dil):
    t = qn.shape[0]
    nblk = t // (CHUNK * dil)

    def kern(q_ref, kp_ref, kc_ref, vp_ref, vc_ref, o_ref, lse_ref):
        n = pl.program_id(1)
        q, kp, kc, vp, vc = q_ref[...], kp_ref[...], kc_ref[...], vp_ref[...], vc_ref[...]
        ri = lax.broadcasted_iota(jnp.int32, (2 * CHUNK, 2 * CHUNK), 0) % CHUNK
        cj = lax.broadcasted_iota(jnp.int32, (2 * CHUNK, 2 * CHUNK), 1)
        mask = (cj - ri >= 0) & (cj - ri <= CHUNK) & ((n > 0) | (cj >= CHUNK))
        lane = lax.broadcasted_iota(jnp.int32, (CHUNK, LANE), 1)
        outs, lse_tile = [], jnp.zeros((CHUNK, LANE), F32)
        for kh in range(N_KV):
            q2 = _stack_q(q, kh)
            k2 = jnp.concatenate([_head_cols(kp, kh), _head_cols(kc, kh)], axis=0)
            v2 = jnp.concatenate([_head_cols(vp, kh), _head_cols(vc, kh)], axis=0)
            s = jnp.where(mask, _dot_nt(q2, k2), NEG)
            m = jnp.max(s, axis=1, keepdims=True)
            p = jnp.exp(s - m)
            l = jnp.sum(p, axis=1, keepdims=True)
            o = _dot(p.astype(BF16), v2) / l
            lse = m + jnp.log(l)
            outs += [o[:CHUNK], o[CHUNK:]]
            lse_tile = jnp.where(lane == 2 * kh, lse[:CHUNK], lse_tile)
            lse_tile = jnp.where(lane == 2 * kh + 1, lse[CHUNK:], lse_tile)
        o_ref[...] = jnp.concatenate(outs, axis=1)
        lse_ref[...] = lse_tile

    cur = lambda r, n: (n, 0, r)
    prev = lambda r, n: (jnp.maximum(n - 1, 0), 0, r)
    o, lse = pl.pallas_call(
        kern, name=f"attn_fwd_d{dil}", grid=(dil, nblk),
        in_specs=[pl.BlockSpec((None, CHUNK, ATTN_DIM), cur),
                  pl.BlockSpec((None, CHUNK, KV_DIM), prev), pl.BlockSpec((None, CHUNK, KV_DIM), cur),
                  pl.BlockSpec((None, CHUNK, KV_DIM), prev), pl.BlockSpec((None, CHUNK, KV_DIM), cur)],
        out_specs=[pl.BlockSpec((None, CHUNK, ATTN_DIM), cur), pl.BlockSpec((None, CHUNK, LANE), cur)],
        out_shape=[jax.ShapeDtypeStruct((nblk, CHUNK, dil * ATTN_DIM), F32),
                   jax.ShapeDtypeStruct((nblk, CHUNK, dil * LANE), F32)],
        compiler_params=_cparams(("parallel", "parallel")),
    )(_sub_view(qn, dil), _sub_view(kn, dil), _sub_view(kn, dil), _sub_view(vb, dil), _sub_view(vb, dil))
    return o.reshape(t, ATTN_DIM), lse.reshape(t, LANE)


def _head_expand_matrix():
    return (jnp.arange(LANE)[:, None] == (jnp.arange(ATTN_DIM)[None, :] // HEAD_DIM)).astype(BF16)


def _attn_merge(os_, lses):
    def body(o1, o2, o3, l1, l2, l3, e_ref, out_ref, lse_ref):
        a, b, c = l1[...], l2[...], l3[...]
        m = jnp.maximum(jnp.maximum(a, b), c)
        tot = m + jnp.log(jnp.exp(a - m) + jnp.exp(b - m) + jnp.exp(c - m))
        e = e_ref[...]
        out = (_dot_split(jnp.exp(a - tot), e) * o1[...] + _dot_split(jnp.exp(b - tot), e) * o2[...]
               + _dot_split(jnp.exp(c - tot), e) * o3[...])
        out_ref[...] = out.astype(BF16)
        lse_ref[...] = tot
    ins = [("t", o) for o in os_] + [("t", l) for l in lses] + [("p", _head_expand_matrix())]
    return _rows("attn_merge", body, ins, [(ATTN_DIM, BF16), (LANE, F32)], tm=256)


def _attn_bwd_prep(dmix, attn_out):
    et = _head_expand_matrix().T

    def body(do_ref, o_ref, et_ref, dob_ref, d_ref):
        do = do_ref[...]
        dob_ref[...] = do.astype(BF16)
        d_ref[...] = _dot_split(do * o_ref[...].astype(F32), et_ref[...])
    return _rows("attn_bwd_prep", body, [("tc", dmix, ATTN_DIM, SSM_INNER // ATTN_DIM), ("t", attn_out), ("p", et)],
                 [(ATTN_DIM, BF16), (LANE, F32)])


def _attn_dq(qn, kn, vb, dob, lse, dsum, dil):
    t = qn.shape[0]
    nblk = t // (CHUNK * dil)

    def kern(q_ref, kp_ref, kc_ref, vp_ref, vc_ref, do_ref, lse_ref, d_ref, dq_ref):
        n = pl.program_id(1)
        q, kp, kc, vp, vc, do = q_ref[...], kp_ref[...], kc_ref[...], vp_ref[...], vc_ref[...], do_ref[...]
        lse_t, d_t = lse_ref[...], d_ref[...]
        ri = lax.broadcasted_iota(jnp.int32, (2 * CHUNK, 2 * CHUNK), 0) % CHUNK
        cj = lax.broadcasted_iota(jnp.int32, (2 * CHUNK, 2 * CHUNK), 1)
        mask = (cj - ri >= 0) & (cj - ri <= CHUNK) & ((n > 0) | (cj >= CHUNK))
        outs = []
        for kh in range(N_KV):
            q2, do2 = _stack_q(q, kh), _stack_q(do, kh)
            k2 = jnp.concatenate([_head_cols(kp, kh), _head_cols(kc, kh)], axis=0)
            v2 = jnp.concatenate([_head_cols(vp, kh), _head_cols(vc, kh)], axis=0)
            p = jnp.where(mask, jnp.exp(jnp.where(mask, _dot_nt(q2, k2), NEG) - _stat_cols(lse_t, kh, CHUNK)), 0.0)
            ds = p * (_dot_nt(do2, v2) - _stat_cols(d_t, kh, CHUNK))
            dq2 = _dot(ds.astype(BF16), k2)
            outs += [dq2[:CHUNK], dq2[CHUNK:]]
        dq_ref[...] = jnp.concatenate(outs, axis=1)

    cur = lambda r, n: (n, 0, r)
    prev = lambda r, n: (jnp.maximum(n - 1, 0), 0, r)
    dq = pl.pallas_call(
        kern, name=f"attn_dq_d{dil}", grid=(dil, nblk),
        in_specs=[pl.BlockSpec((None, CHUNK, ATTN_DIM), cur),
                  pl.BlockSpec((None, CHUNK, KV_DIM), prev), pl.BlockSpec((None, CHUNK, KV_DIM), cur),
                  pl.BlockSpec((None, CHUNK, KV_DIM), prev), pl.BlockSpec((None, CHUNK, KV_DIM), cur),
                  pl.BlockSpec((None, CHUNK, ATTN_DIM), cur),
                  pl.BlockSpec((None, CHUNK, LANE), cur), pl.BlockSpec((None, CHUNK, LANE), cur)],
        out_specs=pl.BlockSpec((None, CHUNK, ATTN_DIM), cur),
        out_shape=jax.ShapeDtypeStruct((nblk, CHUNK, dil * ATTN_DIM), F32),
        compiler_params=_cparams(("parallel", "parallel")),
    )(_sub_view(qn, dil), _sub_view(kn, dil), _sub_view(kn, dil), _sub_view(vb, dil), _sub_view(vb, dil),
      _sub_view(dob, dil), _sub_view(lse, dil), _sub_view(dsum, dil))
    return dq.reshape(t, ATTN_DIM)


def _attn_dkv(qn, kn, vb, dob, lse, dsum, dil):
    t = qn.shape[0]
    nblk = t // (CHUNK * dil)

    def kern(k_ref, v_ref, qc_ref, qn_ref, doc_ref, don_ref, lc_ref, ln_ref, dc_ref, dn_ref, dk_ref, dv_ref):
        n = pl.program_id(1)
        k, v = k_ref[...], v_ref[...]
        qc, qx, doc, dox = qc_ref[...], qn_ref[...], doc_ref[...], don_ref[...]
        lc, lx, dc, dx = lc_ref[...], ln_ref[...], dc_ref[...], dn_ref[...]
        ri = lax.broadcasted_iota(jnp.int32, (4 * CHUNK, CHUNK), 0) % (2 * CHUNK)
        cj = lax.broadcasted_iota(jnp.int32, (4 * CHUNK, CHUNK), 1)
        mask = (ri - cj >= 0) & (ri - cj <= CHUNK) & ((n < nblk - 1) | (ri < CHUNK))
        dks, dvs = [], []
        for kh in range(N_KV):
            def rows4(cur, nxt, w=HEAD_DIM):
                return jnp.concatenate([cur[:, 2 * kh * w:(2 * kh + 1) * w], nxt[:, 2 * kh * w:(2 * kh + 1) * w],
                                        cur[:, (2 * kh + 1) * w:(2 * kh + 2) * w], nxt[:, (2 * kh + 1) * w:(2 * kh + 2) * w]],
                                       axis=0)
            q4, do4 = rows4(qc, qx), rows4(doc, dox)
            lse4, d4 = rows4(lc, lx, 1), rows4(dc, dx, 1)
            kk, vv = _head_cols(k, kh), _head_cols(v, kh)
            p = jnp.where(mask, jnp.exp(jnp.where(mask, _dot_nt(q4, kk), NEG) - lse4), 0.0)
            dvs.append(_dot_tn(p.astype(BF16), do4))
            ds = p * (_dot_nt(do4, vv) - d4)
            dks.append(_dot_tn(ds.astype(BF16), q4))
        dk_ref[...] = jnp.concatenate(dks, axis=1)
        dv_ref[...] = jnp.concatenate(dvs, axis=1)

    cur = lambda r, n: (n, 0, r)
    nxt = lambda r, n: (jnp.minimum(n + 1, nblk - 1), 0, r)
    qv, dv_, lv, sv = _sub_view(qn, dil), _sub_view(dob, dil), _sub_view(lse, dil), _sub_view(dsum, dil)
    dk, dv = pl.pallas_call(
        kern, name=f"attn_dkv_d{dil}", grid=(dil, nblk),
        in_specs=[pl.BlockSpec((None, CHUNK, KV_DIM), cur), pl.BlockSpec((None, CHUNK, KV_DIM), cur),
                  pl.BlockSpec((None, CHUNK, ATTN_DIM), cur), pl.BlockSpec((None, CHUNK, ATTN_DIM), nxt),
                  pl.BlockSpec((None, CHUNK, ATTN_DIM), cur), pl.BlockSpec((None, CHUNK, ATTN_DIM), nxt),
                  pl.BlockSpec((None, CHUNK, LANE), cur), pl.BlockSpec((None, CHUNK, LANE), nxt),
                  pl.BlockSpec((None, CHUNK, LANE), cur), pl.BlockSpec((None, CHUNK, LANE), nxt)],
        out_specs=[pl.BlockSpec((None, CHUNK, KV_DIM), cur), pl.BlockSpec((None, CHUNK, KV_DIM), cur)],
        out_shape=[jax.ShapeDtypeStruct((nblk, CHUNK, dil * KV_DIM), F32)] * 2,
        compiler_params=_cparams(("parallel", "parallel")),
    )(_sub_view(kn, dil), _sub_view(vb, dil), qv, qv, dv_, dv_, lv, lv, sv, sv)
    return dk.reshape(t, KV_DIM), dv.reshape(t, KV_DIM)


ATT_SPAN = 2048
N_QH = 8


def _lane_lo(rows):
    return lax.broadcasted_iota(jnp.int32, (rows, LANE), 1) < HEAD_DIM


def _swap_halves_lanes(x):
    return pltpu.roll(x, HEAD_DIM, axis=1)


def _qknorm_fwd2(proj, gq_t, gk_t, tm=256):
    t = proj.shape[0]
    bq, bk = _head_mean_matrix(ATTN_DIM), _head_mean_matrix(KV_DIM)
    scale = HEAD_DIM ** -0.5

    def kern(q_ref, k_ref, v_ref, gq_ref, gk_ref, bq_ref, bk_ref, qo_ref, kvo_ref):
        q, k, v = q_ref[...], k_ref[...], v_ref[...]
        qn = (q * lax.rsqrt(_dot_split(q * q, bq_ref[...]) + EPS) * gq_ref[...]) * scale
        kn = k * lax.rsqrt(_dot_split(k * k, bk_ref[...]) + EPS) * gk_ref[...]
        lo = _lane_lo(tm)
        for j in range(N_KV):
            blk = qn[:, j * LANE:(j + 1) * LANE]
            qo_ref[2 * j] = jnp.where(lo, blk, 0.0)
            qo_ref[2 * j + 1] = jnp.where(lo, _swap_halves_lanes(blk), 0.0)
        for j in range(2):
            kb, vb = kn[:, j * LANE:(j + 1) * LANE], v[:, j * LANE:(j + 1) * LANE]
            kvo_ref[2 * j] = jnp.where(lo, kb, _swap_halves_lanes(vb))
            kvo_ref[2 * j + 1] = jnp.where(lo, _swap_halves_lanes(kb), vb)

    col = lambda w, idx: pl.BlockSpec((tm, w), functools.partial(lambda i, idx: (i, idx), idx=idx))
    par = lambda a: pl.BlockSpec(a.shape, lambda i: (0, 0))
    return pl.pallas_call(
        kern, name="qknorm_fwd", grid=(t // tm,),
        in_specs=[col(ATTN_DIM, OFF_Q // ATTN_DIM), col(KV_DIM, OFF_K // KV_DIM), col(KV_DIM, OFF_V // KV_DIM),
                  par(gq_t), par(gk_t), par(bq), par(bk)],
        out_specs=[pl.BlockSpec((N_QH, tm, LANE), lambda i: (0, i, 0)), pl.BlockSpec((N_KV, tm, LANE), lambda i: (0, i, 0))],
        out_shape=[jax.ShapeDtypeStruct((N_QH, t, LANE), F32), jax.ShapeDtypeStruct((N_KV, t, LANE), F32)],
        compiler_params=_cparams(("parallel",)))(proj, proj, proj, gq_t, gk_t, bq, bk)


def _qknorm_bwd2(proj, gq_t, gk_t, dqs, dkvs, tm=256):
    t = proj.shape[0]
    bq, bk = _head_mean_matrix(ATTN_DIM), _head_mean_matrix(KV_DIM)
    scale = HEAD_DIM ** -0.5

    def kern(q_ref, k_ref, gq_ref, gk_ref, bq_ref, bk_ref, a1, a2, a3, b1, b2, b3, dq_ref, dk_ref, dv_ref, dgq_ref, dgk_ref):
        @pl.when(pl.program_id(0) == 0)
        def _():
            dgq_ref[...] = jnp.zeros(dgq_ref.shape, F32)
            dgk_ref[...] = jnp.zeros(dgk_ref.shape, F32)
        lo = _lane_lo(tm)
        sq = [a1[h] + a2[h] + a3[h] for h in range(N_QH)]
        skv = [b1[h] + b2[h] + b3[h] for h in range(N_KV)]
        dqn = jnp.concatenate([jnp.where(lo, sq[2 * j], _swap_halves_lanes(sq[2 * j + 1])) for j in range(N_KV)], axis=1) * scale
        dkn = jnp.concatenate([jnp.where(lo, skv[2 * j], _swap_halves_lanes(skv[2 * j + 1])) for j in range(2)], axis=1)
        dv = jnp.concatenate([jnp.where(lo, _swap_halves_lanes(skv[2 * j]), skv[2 * j + 1]) for j in range(2)], axis=1)
        q, k = q_ref[...], k_ref[...]
        rq = lax.rsqrt(_dot_split(q * q, bq_ref[...]) + EPS)
        rk = lax.rsqrt(_dot_split(k * k, bk_ref[...]) + EPS)
        gdq, gdk = dqn * gq_ref[...], dkn * gk_ref[...]
        dq_ref[...] = (rq * gdq - q * (rq * rq * rq * _dot_split(q * gdq, bq_ref[...]))).astype(BF16)
        dk_ref[...] = (rk * gdk - k * (rk * rk * rk * _dot_split(k * gdk, bk_ref[...]))).astype(BF16)
        dv_ref[...] = dv.astype(BF16)
        dgq_ref[...] += jnp.sum(dqn * q * rq, axis=0, keepdims=True)
        dgk_ref[...] += jnp.sum(dkn * k * rk, axis=0, keepdims=True)

    col = lambda w, idx: pl.BlockSpec((tm, w), functools.partial(lambda i, idx: (i, idx), idx=idx))
    par = lambda a: pl.BlockSpec(a.shape, lambda i: (0, 0))
    blk = lambda n: pl.BlockSpec((n, tm, LANE), lambda i: (0, i, 0))
    row = lambda w: pl.BlockSpec((tm, w), lambda i: (i, 0))
    acc = lambda w: pl.BlockSpec((1, w), lambda i: (0, 0))
    return pl.pallas_call(
        kern, name="qknorm_bwd", grid=(t // tm,),
        in_specs=[col(ATTN_DIM, OFF_Q // ATTN_DIM), col(KV_DIM, OFF_K // KV_DIM), par(gq_t), par(gk_t), par(bq), par(bk)]
        + [blk(N_QH)] * 3 + [blk(N_KV)] * 3,
        out_specs=[row(ATTN_DIM), row(KV_DIM), row(KV_DIM), acc(ATTN_DIM), acc(KV_DIM)],
        out_shape=[jax.ShapeDtypeStruct((t, ATTN_DIM), BF16), jax.ShapeDtypeStruct((t, KV_DIM), BF16),
                   jax.ShapeDtypeStruct((t, KV_DIM), BF16), jax.ShapeDtypeStruct((1, ATTN_DIM), F32),
                   jax.ShapeDtypeStruct((1, KV_DIM), F32)],
        compiler_params=_cparams(("arbitrary",)))(proj, proj, gq_t, gk_t, bq, bk, *dqs, *dkvs)


def _att_rows(b, r, dil):
    if dil == 1:
        return pl.ds(b * CHUNK, CHUNK)
    return pl.ds(b * CHUNK * dil + r, CHUNK, stride=dil)


def _for_residues(dil, unit):
    for r in range(dil):
        unit(r, 0)


def _band_qk(first):
    ri = lax.broadcasted_iota(jnp.int32, (CHUNK, 2 * CHUNK), 0)
    cj = lax.broadcasted_iota(jnp.int32, (CHUNK, 2 * CHUNK), 1)
    band = (cj - ri >= 0) & (cj - ri <= CHUNK)
    return band if first is None else band & (jnp.logical_not(first) | (cj >= CHUNK))


def _band_kq(last):
    rj = lax.broadcasted_iota(jnp.int32, (CHUNK, 2 * CHUNK), 0)
    ci = lax.broadcasted_iota(jnp.int32, (CHUNK, 2 * CHUNK), 1)
    band = (ci - rj >= 0) & (ci - rj <= CHUNK)
    return band if last is None else band & (jnp.logical_not(last) | (ci < CHUNK))


def _att_specs(t, dil):
    sub = CHUNK * dil
    nb, last = ATT_SPAN // sub, t // sub - 1
    cur = lambda heads: pl.BlockSpec((heads, ATT_SPAN, LANE), lambda kh, n: (kh, n, 0))
    prev = lambda heads: pl.BlockSpec((heads, sub, LANE), lambda kh, n: (kh, jnp.maximum(n * nb - 1, 0), 0))
    nxt = lambda heads: pl.BlockSpec((heads, sub, LANE), lambda kh, n: (kh, jnp.minimum((n + 1) * nb, last), 0))
    return sub, nb, cur, prev, nxt


def _attn_fwd2(q, kv, dil):
    t = q.shape[1]
    sub, nb, cur, prev, _ = _att_specs(t, dil)

    def kern(q_ref, kvp_ref, kvc_ref, o_ref, lse_ref):
        n = pl.program_id(1)
        lane = lax.broadcasted_iota(jnp.int32, (CHUNK, LANE), 1)
        for b in range(nb):
            mask = _band_qk((n == 0) if b == 0 else None)

            def unit(r, carry, b=b, mask=mask):
                rows = _att_rows(b, r, dil)
                kvp = kvc_ref[_att_rows(b - 1, r, dil), :] if b > 0 else kvp_ref[_att_rows(0, r, dil), :]
                kvcat = jnp.concatenate([kvp, kvc_ref[rows, :]], axis=0).astype(BF16)
                lse_tile = jnp.zeros((CHUNK, LANE), F32)
                for g in range(2):
                    s = jnp.where(mask, _dot_nt(q_ref.at[g][rows, :].astype(BF16), kvcat), NEG)
                    m = jnp.max(s, axis=1, keepdims=True)
                    p = jnp.exp(s - m)
                    l = jnp.sum(p, axis=1, keepdims=True)
                    o_ref.at[g][rows, :] = _dot(p.astype(BF16), kvcat) * (1.0 / l)
                    lse_tile = jnp.where(lane == g, m + jnp.log(l), lse_tile)
                lse_ref[rows, :] = lse_tile
                return carry
            _for_residues(dil, unit)

    return pl.pallas_call(
        kern, name=f"attn_fwd_d{dil}", grid=(N_KV, t // ATT_SPAN), in_specs=[cur(2), prev(None), cur(None)],
        out_specs=[cur(2), cur(None)],
        out_shape=[jax.ShapeDtypeStruct((N_QH, t, LANE), F32), jax.ShapeDtypeStruct((N_KV, t, LANE), F32)],
        compiler_params=_cparams(("parallel", "parallel")))(q, kv, kv)


def _attn_merge2(os_, lses, tm=256):
    t = os_[0].shape[1]

    def kern(o1, o2, o3, l1, l2, l3, out_ref, lse_ref):
        pieces = []
        for kh in range(N_KV):
            a, b, c = l1[kh], l2[kh], l3[kh]
            m = jnp.maximum(jnp.maximum(a, b), c)
            tot = m + jnp.log(jnp.exp(a - m) + jnp.exp(b - m) + jnp.exp(c - m))
            lse_ref[kh] = tot
            wa, wb, wc = jnp.exp(a - tot), jnp.exp(b - tot), jnp.exp(c - tot)
            for g in range(2):
                h = 2 * kh + g
                acc = wa[:, g:g + 1] * o1[h] + wb[:, g:g + 1] * o2[h] + wc[:, g:g + 1] * o3[h]
                pieces.append(acc[:, HEAD_DIM:])
        out_ref[...] = jnp.concatenate(pieces, axis=1).astype(BF16)

    blk = lambda n: pl.BlockSpec((n, tm, LANE), lambda i: (0, i, 0))
    return pl.pallas_call(
        kern, name="attn_merge", grid=(t // tm,), in_specs=[blk(N_QH)] * 3 + [blk(N_KV)] * 3,
        out_specs=[pl.BlockSpec((tm, ATTN_DIM), lambda i: (i, 0)), blk(N_KV)],
        out_shape=[jax.ShapeDtypeStruct((t, ATTN_DIM), BF16), jax.ShapeDtypeStruct((N_KV, t, LANE), F32)],
        compiler_params=_cparams(("parallel",)))(*os_, *lses)


def _attn_bwd_prep2(dmix, attn_out, tm=256):
    t = attn_out.shape[0]

    def kern(do_ref, o_ref, dot_ref, d_ref):
        do = do_ref[...]
        prod = do * o_ref[...].astype(F32)
        lo = _lane_lo(tm)
        lane = lax.broadcasted_iota(jnp.int32, (tm, LANE), 1)
        for kh in range(N_KV):
            blk, pb = do[:, kh * LANE:(kh + 1) * LANE], prod[:, kh * LANE:(kh + 1) * LANE]
            dot_ref[2 * kh] = jnp.where(lo, 0.0, _swap_halves_lanes(blk))
            dot_ref[2 * kh + 1] = jnp.where(lo, 0.0, blk)
            s_lo = jnp.sum(jnp.where(lo, pb, 0.0), axis=1, keepdims=True)
            s_hi = jnp.sum(pb, axis=1, keepdims=True) - s_lo
            d_ref[kh] = jnp.where(lane == 0, s_lo, jnp.where(lane == 1, s_hi, 0.0))

    blk = lambda n: pl.BlockSpec((n, tm, LANE), lambda i: (0, i, 0))
    return pl.pallas_call(
        kern, name="attn_bwd_prep", grid=(t // tm,),
        in_specs=[pl.BlockSpec((tm, ATTN_DIM), lambda i: (i, SSM_INNER // ATTN_DIM)), pl.BlockSpec((tm, ATTN_DIM), lambda i: (i, 0))],
        out_specs=[blk(N_QH), blk(N_KV)],
        out_shape=[jax.ShapeDtypeStruct((N_QH, t, LANE), F32), jax.ShapeDtypeStruct((N_KV, t, LANE), F32)],
        compiler_params=_cparams(("parallel",)))(dmix, attn_out)


def _attn_dq2(q, kv, dot, lse, dsum, dil):
    t = q.shape[1]
    sub, nb, cur, prev, _ = _att_specs(t, dil)

    def kern(q_ref, kvp_ref, kvc_ref, do_ref, lse_ref, d_ref, dq_ref):
        n = pl.program_id(1)
        for b in range(nb):
            mask = _band_qk((n == 0) if b == 0 else None)

            def unit(r, carry, b=b, mask=mask):
                rows = _att_rows(b, r, dil)
                kvp = kvc_ref[_att_rows(b - 1, r, dil), :] if b > 0 else kvp_ref[_att_rows(0, r, dil), :]
                kvcat = jnp.concatenate([kvp, kvc_ref[rows, :]], axis=0).astype(BF16)
                lse_t, d_t = lse_ref[rows, :], d_ref[rows, :]
                for g in range(2):
                    s = jnp.where(mask, _dot_nt(q_ref.at[g][rows, :].astype(BF16), kvcat), NEG)
                    p = jnp.exp(s - lse_t[:, g:g + 1])
                    dp = _dot_nt(do_ref.at[g][rows, :].astype(BF16), kvcat)
                    ds = p * (dp - d_t[:, g:g + 1])
                    dq_ref.at[g][rows, :] = _dot(ds.astype(BF16), kvcat)
                return carry
            _for_residues(dil, unit)

    return pl.pallas_call(
        kern, name=f"attn_dq_d{dil}", grid=(N_KV, t // ATT_SPAN),
        in_specs=[cur(2), prev(None), cur(None), cur(2), cur(None), cur(None)], out_specs=cur(2),
        out_shape=jax.ShapeDtypeStruct((N_QH, t, LANE), F32),
        compiler_params=_cparams(("parallel", "parallel")))(q, kv, kv, dot, lse, dsum)


def _attn_dkv2(q, kv, dot, lse, dsum, dil):
    t = q.shape[1]
    sub, nb, cur, _, nxt = _att_specs(t, dil)
    nsteps = t // ATT_SPAN

    def kern(kv_ref, qc_ref, qn_ref, doc_ref, don_ref, lc_ref, ln_ref, dc_ref, dn_ref, dkv_ref):
        n = pl.program_id(1)
        for b in range(nb):
            inside = b < nb - 1
            mask = _band_kq(None if inside else (n == nsteps - 1))

            def unit(r, carry, b=b, inside=inside, mask=mask):
                rows = _att_rows(b, r, dil)
                nrows = _att_rows(b + 1, r, dil) if inside else _att_rows(0, r, dil)
                kvb = kv_ref[rows, :].astype(BF16)
                follow = lambda cref, nref: (cref if inside else nref)[nrows, :]
                lse_t = jnp.concatenate([lc_ref[rows, :].T, follow(lc_ref, ln_ref).T], axis=1)
                d_t = jnp.concatenate([dc_ref[rows, :].T, follow(dc_ref, dn_ref).T], axis=1)
                acc = jnp.zeros((CHUNK, LANE), F32)
                for g in range(2):
                    qdo = jnp.concatenate([qc_ref.at[g][rows, :], follow(qc_ref.at[g], qn_ref.at[g]),
                                           doc_ref.at[g][rows, :], follow(doc_ref.at[g], don_ref.at[g])], axis=0).astype(BF16)
                    both = _dot_nt(kvb, qdo)
                    pt = jnp.exp(jnp.where(mask, both[:, :2 * CHUNK], NEG) - lse_t[g:g + 1, :])
                    dst = pt * (both[:, 2 * CHUNK:] - d_t[g:g + 1, :])
                    acc = acc + _dot(jnp.concatenate([dst, pt], axis=1).astype(BF16), qdo)
                dkv_ref[rows, :] = acc
                return carry
            _for_residues(dil, unit)

    return pl.pallas_call(
        kern, name=f"attn_dkv_d{dil}", grid=(N_KV, nsteps),
        in_specs=[cur(None), cur(2), nxt(2), cur(2), nxt(2), cur(None), nxt(None), cur(None), nxt(None)], out_specs=cur(None),
        out_shape=jax.ShapeDtypeStruct((N_KV, t, LANE), F32),
        compiler_params=_cparams(("parallel", "parallel")))(kv, q, q, dot, dot, lse, lse, dsum, dsum)


HALO = 8
SSM_CONV_TM, SSM_CONV_W = 512, 512
FFN_CONV_TM, FFN_CONV_W = 256, 1408


def _halo_specs(tm, width, t_rows, col_off=0, lead=None):
    per, last = tm // HALO, t_rows // HALO - 1
    row_maps = (lambda i: i, lambda i: jnp.maximum(i * per - 1, 0), lambda i: jnp.minimum((i + 1) * per, last))
    specs = []
    for rows, rm in zip((tm, HALO, HALO), row_maps):
        if lead is None:
            specs.append(pl.BlockSpec((rows, width), functools.partial(lambda c, i, rm: (rm(i), c + col_off), rm=rm)))
        else:
            specs.append(pl.BlockSpec((None, rows, width), functools.partial(lambda c, i, rm: (lead, rm(i), c + col_off), rm=rm)))
    return specs


def _fill_ext(buf, tile_ref, before_ref, after_ref, i, nt):
    tm = tile_ref.shape[0]
    buf[0:HALO, :] = jnp.where(i > 0, before_ref[...].astype(F32), 0.0)
    buf[HALO:HALO + tm, :] = tile_ref[...].astype(F32)
    if after_ref is not None:
        buf[HALO + tm:, :] = jnp.where(i < nt - 1, after_ref[...].astype(F32), 0.0)


CONV_RB, CONV_CW = 16, 256


def _lane_chunks(width):
    return [slice(c0, min(c0 + CONV_CW, width)) for c0 in range(0, width, CONV_CW)]


def _shifted(buf, taps, r0, rows, cs):
    return [buf[pl.ds(HALO - (taps - 1) + k + r0, rows), cs] for k in range(taps)]


def _taps_fwd(xs, w, b):
    acc = b
    for k, xk in enumerate(xs):
        acc = acc + w[k:k + 1, :] * xk
    return acc


def _taps_bwd(bufd, w, taps, r0, rows, cs):
    acc = None
    for k in range(taps):
        term = w[k:k + 1, :] * bufd[pl.ds(r0 + (taps - 1) - k, rows), cs]
        acc = term if acc is None else acc + term
    return acc


def _fold8(z):
    return z[:HALO] + z[HALO:] if z.shape[0] == 2 * HALO else z


def _silu_grad(pre):
    sg = _sigmoid(pre)
    return sg * (1.0 + pre * (1.0 - sg))


def _ssm_conv_fwd(proj, w, b):
    t = proj.shape[0]
    tm, wd = min(SSM_CONV_TM, t), SSM_CONV_W
    nt, taps = t // tm, w.shape[0]

    def kern(x_ref, xb_ref, w_ref, b_ref, o_ref, buf):
        _fill_ext(buf, x_ref, xb_ref, None, pl.program_id(1), nt)
        for cs in _lane_chunks(wd):
            wv, bv = w_ref[:, cs], b_ref[:, cs]
            for r0 in range(0, tm, CONV_RB):
                pre = _taps_fwd(_shifted(buf, taps, r0, CONV_RB, cs), wv, bv)
                o_ref[r0:r0 + CONV_RB, cs] = pre * _sigmoid(pre)

    tile, before, _ = _halo_specs(tm, wd, t)
    par = lambda rows: pl.BlockSpec((rows, wd), lambda c, i: (0, c))
    return pl.pallas_call(
        kern, name="ssm_conv_fwd", grid=(XBC_DIM // wd, nt), in_specs=[tile, before, par(taps), par(1)],
        out_specs=pl.BlockSpec((tm, wd), lambda c, i: (i, c)), out_shape=jax.ShapeDtypeStruct((t, XBC_DIM), F32),
        scratch_shapes=[pltpu.VMEM((tm + HALO, wd), F32)],
        compiler_params=_cparams(("parallel", "parallel")))(proj, proj, w, b)


def _ssm_conv_bwd(proj, w, b, dact):
    t = proj.shape[0]
    tm, wd = min(SSM_CONV_TM, t), SSM_CONV_W
    nt, taps = t // tm, w.shape[0]

    def kern(x_ref, xb_ref, xa_ref, d_ref, dn_ref, w_ref, b_ref, dx_ref, gw_ref, gb_ref, buf, bufd):
        i = pl.program_id(1)
        _fill_ext(buf, x_ref, xb_ref, xa_ref, i, nt)

        @pl.when(i == 0)
        def _():
            gw_ref[...] = jnp.zeros(gw_ref.shape, F32)
            gb_ref[...] = jnp.zeros(gb_ref.shape, F32)
        for cs in _lane_chunks(wd):
            wv, bv = w_ref[:, cs], b_ref[:, cs]
            acc = [jnp.zeros((HALO, cs.stop - cs.start), F32) for _ in range(taps + 1)]
            for r0 in list(range(0, tm, CONV_RB)) + [tm]:
                inside = r0 < tm
                rows = CONV_RB if inside else HALO
                xs = _shifted(buf, taps, r0, rows, cs)
                d = d_ref[r0:r0 + rows, cs] if inside else jnp.where(i < nt - 1, dn_ref[:, cs], 0.0)
                dpre = d * _silu_grad(_taps_fwd(xs, wv, bv))
                bufd[r0:r0 + rows, cs] = dpre
                if inside:
                    acc[taps] = acc[taps] + _fold8(dpre)
                    for k in range(taps):
                        acc[k] = acc[k] + _fold8(dpre * xs[k])
            gb_ref[:, cs] += jnp.sum(acc[taps], axis=0, keepdims=True)
            for k in range(taps):
                gw_ref[k:k + 1, cs] += jnp.sum(acc[k], axis=0, keepdims=True)
            for r0 in range(0, tm, CONV_RB):
                dx_ref[r0:r0 + CONV_RB, cs] = _taps_bwd(bufd, wv, taps, r0, CONV_RB, cs).astype(BF16)

    xt, xb, xa = _halo_specs(tm, wd, t)
    dt_, _, dn = _halo_specs(tm, wd, t)
    par = lambda rows: pl.BlockSpec((rows, wd), lambda c, i: (0, c))
    return pl.pallas_call(
        kern, name="ssm_conv_bwd", grid=(XBC_DIM // wd, nt), in_specs=[xt, xb, xa, dt_, dn, par(taps), par(1)],
        out_specs=[pl.BlockSpec((tm, wd), lambda c, i: (i, c)), par(taps), par(1)],
        out_shape=[jax.ShapeDtypeStruct((t, XBC_DIM), BF16), jax.ShapeDtypeStruct((taps, XBC_DIM), F32),
                   jax.ShapeDtypeStruct((1, XBC_DIM), F32)],
        scratch_shapes=[pltpu.VMEM((tm + 2 * HALO, wd), F32), pltpu.VMEM((tm + HALO, wd), F32)],
        compiler_params=_cparams(("parallel", "arbitrary")))(proj, proj, proj, dact, dact, w, b)


def _ffn_act_fwd(u, w, b):
    t = u.shape[1]
    tm, wd = min(FFN_CONV_TM, t), FFN_CONV_W
    nt, taps, nc = t // tm, w.shape[0], D_FF // FFN_CONV_W

    def kern(g_ref, gb_ref, v_ref, vb_ref, wg_ref, wv_ref, bg_ref, bv_ref, a_ref, bufg, bufv):
        i = pl.program_id(1)
        _fill_ext(bufg, g_ref, gb_ref, None, i, nt)
        _fill_ext(bufv, v_ref, vb_ref, None, i, nt)
        for cs in _lane_chunks(wd):
            wg, wv, bg, bv = wg_ref[:, cs], wv_ref[:, cs], bg_ref[:, cs], bv_ref[:, cs]
            for r0 in range(0, tm, CONV_RB):
                g = _taps_fwd(_shifted(bufg, taps, r0, CONV_RB, cs), wg, bg)
                v = _taps_fwd(_shifted(bufv, taps, r0, CONV_RB, cs), wv, bv)
                a_ref[r0:r0 + CONV_RB, cs] = (g * _sigmoid(g) * v).astype(BF16)

    gt, gbf, _ = _halo_specs(tm, wd, t, lead=0)
    vt, vbf, _ = _halo_specs(tm, wd, t, lead=1)
    par = lambda rows, off: pl.BlockSpec((rows, wd), functools.partial(lambda c, i, off: (0, c + off), off=off))
    return pl.pallas_call(
        kern, name="ffn_act_fwd", grid=(nc, nt),
        in_specs=[gt, gbf, vt, vbf, par(taps, 0), par(taps, nc), par(1, 0), par(1, nc)],
        out_specs=pl.BlockSpec((tm, wd), lambda c, i: (i, c)), out_shape=jax.ShapeDtypeStruct((t, D_FF), BF16),
        scratch_shapes=[pltpu.VMEM((tm + HALO, wd), F32)] * 2,
        compiler_params=_cparams(("parallel", "parallel")))(u, u, u, u, w, w, b, b)


def _ffn_act_bwd(u, w, b, da):
    t = u.shape[1]
    tm, wd = min(FFN_CONV_TM, t), FFN_CONV_W
    nt, taps, nc = t // tm, w.shape[0], D_FF // FFN_CONV_W

    def kern(g_ref, gb_ref, ga_ref, v_ref, vb_ref, va_ref, d_ref, dn_ref, wg_ref, wv_ref, bg_ref, bv_ref,
             du_ref, gwg_ref, gwv_ref, gbg_ref, gbv_ref, bufg, bufv, bufdg, bufdv):
        i = pl.program_id(1)
        _fill_ext(bufg, g_ref, gb_ref, ga_ref, i, nt)
        _fill_ext(bufv, v_ref, vb_ref, va_ref, i, nt)

        @pl.when(i == 0)
        def _():
            for r in (gwg_ref, gwv_ref, gbg_ref, gbv_ref):
                r[...] = jnp.zeros(r.shape, F32)
        for cs in _lane_chunks(wd):
            wg, wv, bg, bv = wg_ref[:, cs], wv_ref[:, cs], bg_ref[:, cs], bv_ref[:, cs]
            zero = jnp.zeros((HALO, cs.stop - cs.start), F32)
            accg, accv = [zero] * (taps + 1), [zero] * (taps + 1)
            for r0 in list(range(0, tm, CONV_RB)) + [tm]:
                inside = r0 < tm
                rows = CONV_RB if inside else HALO
                xg, xv = _shifted(bufg, taps, r0, rows, cs), _shifted(bufv, taps, r0, rows, cs)
                g, v = _taps_fwd(xg, wg, bg), _taps_fwd(xv, wv, bv)
                dav = d_ref[r0:r0 + rows, cs] if inside else jnp.where(i < nt - 1, dn_ref[:, cs], 0.0)
                sg = _sigmoid(g)
                dg = dav * v * (sg * (1.0 + g * (1.0 - sg)))
                dv = dav * (g * sg)
                bufdg[r0:r0 + rows, cs] = dg
                bufdv[r0:r0 + rows, cs] = dv
                if inside:
                    accg[taps], accv[taps] = accg[taps] + _fold8(dg), accv[taps] + _fold8(dv)
                    for k in range(taps):
                        accg[k], accv[k] = accg[k] + _fold8(dg * xg[k]), accv[k] + _fold8(dv * xv[k])
            gbg_ref[:, cs] += jnp.sum(accg[taps], axis=0, keepdims=True)
            gbv_ref[:, cs] += jnp.sum(accv[taps], axis=0, keepdims=True)
            for k in range(taps):
                gwg_ref[k:k + 1, cs] += jnp.sum(accg[k], axis=0, keepdims=True)
                gwv_ref[k:k + 1, cs] += jnp.sum(accv[k], axis=0, keepdims=True)
            for r0 in range(0, tm, CONV_RB):
                du_ref[0, r0:r0 + CONV_RB, cs] = _taps_bwd(bufdg, wg, taps, r0, CONV_RB, cs).astype(BF16)
                du_ref[1, r0:r0 + CONV_RB, cs] = _taps_bwd(bufdv, wv, taps, r0, CONV_RB, cs).astype(BF16)

    gt, gbf, gaf = _halo_specs(tm, wd, t, lead=0)
    vt, vbf, vaf = _halo_specs(tm, wd, t, lead=1)
    dt_, _, dn = _halo_specs(tm, wd, t)
    par = lambda rows, off: pl.BlockSpec((rows, wd), functools.partial(lambda c, i, off: (0, c + off), off=off))
    return pl.pallas_call(
        kern, name="ffn_act_bwd", grid=(nc, nt),
        in_specs=[gt, gbf, gaf, vt, vbf, vaf, dt_, dn, par(taps, 0), par(taps, nc), par(1, 0), par(1, nc)],
        out_specs=[pl.BlockSpec((2, tm, wd), lambda c, i: (0, i, c)), par(taps, 0), par(taps, 0), par(1, 0), par(1, 0)],
        out_shape=[jax.ShapeDtypeStruct((2, t, D_FF), BF16)] + [jax.ShapeDtypeStruct((taps, D_FF), F32)] * 2
        + [jax.ShapeDtypeStruct((1, D_FF), F32)] * 2,
        scratch_shapes=[pltpu.VMEM((tm + 2 * HALO, wd), F32)] * 2 + [pltpu.VMEM((tm + HALO, wd), F32)] * 2,
        compiler_params=_cparams(("parallel", "arbitrary")))(u, u, u, u, u, u, da, da, w, w, b, b)


def _softplus(x):
    e = jnp.exp(-jnp.abs(x))
    return jnp.maximum(x, 0.0) + jnp.where(e < 1e-4, e - 0.5 * e * e, jnp.log(1.0 + e))


def _tri(lower):
    r = lax.broadcasted_iota(jnp.int32, (CHUNK, CHUNK), 0)
    c = lax.broadcasted_iota(jnp.int32, (CHUNK, CHUNK), 1)
    return (r >= c) if lower else (r <= c)


def _cum(mat_bool, x):
    return jnp.dot(mat_bool.astype(F32), x, precision=lax.Precision.HIGHEST, preferred_element_type=F32)


def _pair_sel(lane_lo, tile, h0):
    return jnp.where(lane_lo, tile[:, h0:h0 + 1], tile[:, h0 + 1:h0 + 2])


def _ssd_fwd(xbc_act, proj, dt_bias_p, a_log_p, dskip_t):
    t = xbc_act.shape[0]
    nch = t // CHUNK

    def kern(xa_ref, dtr_ref, bias_ref, alog_ref, dsk_ref, y_ref, dt_ref, hs_ref, hst):
        @pl.when(pl.program_id(0) == 0)
        def _():
            hst[...] = jnp.zeros(hst.shape, F32)
        dt = _softplus(dtr_ref[...] + bias_ref[...])
        dt_ref[...] = dt
        acum = _cum(_tri(True), dt * (-jnp.exp(alog_ref[...])))
        acum_t = acum.T
        ea = jnp.exp(acum)
        a_last = acum[CHUNK - 1:CHUNK, :]
        dend = jnp.exp(a_last - acum)
        ea_last = jnp.exp(a_last)
        causal = _tri(True)
        lane_lo = lax.broadcasted_iota(jnp.int32, (CHUNK, LANE), 1) < HEAD_DIM
        row_lo = lax.broadcasted_iota(jnp.int32, (CHUNK, LANE), 0) < HEAD_DIM
        for g in range(2):
            bg = xa_ref[:, SSM_INNER + g * SSM_STATE:SSM_INNER + (g + 1) * SSM_STATE].astype(BF16)
            cg = xa_ref[:, SSM_INNER + BC_DIM + g * SSM_STATE:SSM_INNER + BC_DIM + (g + 1) * SSM_STATE].astype(BF16)
            cb = _dot_nt(cg, bg)
            for j in range(4 * g, 4 * g + 4):
                h0 = 2 * j
                cols = slice(j * LANE, (j + 1) * LANE)
                xp = xa_ref[:, cols]
                xdt = xp * _pair_sel(lane_lo, dt, h0)
                ydiag = None
                for hh, sel in ((h0, lane_lo), (h0 + 1, ~lane_lo)):
                    seg = acum[:, hh:hh + 1] - acum_t[hh:hh + 1, :]
                    mm = (cb * jnp.where(causal, jnp.exp(jnp.minimum(seg, 0.0)), 0.0)).astype(BF16)
                    d = _dot(mm, jnp.where(sel, xdt, 0.0).astype(BF16))
                    ydiag = d if ydiag is None else ydiag + d
                hp = hst[cols, :]
                hs_ref[cols, :] = hp
                yoff = _dot_nt(cg, hp.astype(BF16)) * _pair_sel(lane_lo, ea, h0)
                y_ref[:, cols] = ydiag + yoff + dsk_ref[:, cols] * xp
                xw = (xdt * _pair_sel(lane_lo, dend, h0)).astype(BF16)
                rowf = jnp.where(row_lo, ea_last[:, h0:h0 + 1], ea_last[:, h0 + 1:h0 + 2])
                hst[cols, :] = hp * rowf + _dot_tn(xw, bg)

    return pl.pallas_call(
        kern, name="ssd_fwd", grid=(nch,),
        in_specs=[pl.BlockSpec((CHUNK, XBC_DIM), lambda c: (c, 0)), pl.BlockSpec((CHUNK, LANE), lambda c: (c, OFF_DT // LANE)),
                  pl.BlockSpec((1, LANE), lambda c: (0, 0)), pl.BlockSpec((1, LANE), lambda c: (0, 0)),
                  pl.BlockSpec((1, SSM_INNER), lambda c: (0, 0))],
        out_specs=[pl.BlockSpec((CHUNK, SSM_INNER), lambda c: (c, 0)), pl.BlockSpec((CHUNK, LANE), lambda c: (c, 0)),
                   pl.BlockSpec((None, SSM_INNER, SSM_STATE), lambda c: (c, 0, 0))],
        out_shape=[jax.ShapeDtypeStruct((t, SSM_INNER), F32), jax.ShapeDtypeStruct((t, LANE), F32),
                   jax.ShapeDtypeStruct((nch, SSM_INNER, SSM_STATE), F32)],
        scratch_shapes=[pltpu.VMEM((SSM_INNER, SSM_STATE), F32)],
        compiler_params=_cparams(("arbitrary",)))(xbc_act, proj, dt_bias_p, a_log_p, dskip_t)


def _ssd_bwd(xbc_act, proj, dt_sp, hstates, dy, dt_bias_p, a_log_p, dskip_t):
    t = xbc_act.shape[0]
    nch = t // CHUNK

    def kern(xa_ref, dtr_ref, dt_ref, hs_ref, dy_ref, bias_ref, alog_ref, dsk_ref,
             dact_ref, ddtr_ref, da_ref, dbias_ref, ddsk_ref, dh):
        @pl.when(pl.program_id(0) == 0)
        def _():
            dh[...] = jnp.zeros(dh.shape, F32)
            for r in (da_ref, dbias_ref, ddsk_ref):
                r[...] = jnp.zeros(r.shape, F32)
        dt = dt_ref[...]
        a_neg = -jnp.exp(alog_ref[...])
        acum = _cum(_tri(True), dt * a_neg)
        acum_t = acum.T
        ea = jnp.exp(acum)
        a_last = acum[CHUNK - 1:CHUNK, :]
        dend = jnp.exp(a_last - acum)
        ea_last = jnp.exp(a_last)
        causal = _tri(True)
        lane = lax.broadcasted_iota(jnp.int32, (CHUNK, LANE), 1)
        rowi = lax.broadcasted_iota(jnp.int32, (CHUNK, LANE), 0)
        lane_lo, row_lo, last_row = lane < HEAD_DIM, rowi < HEAD_DIM, rowi == CHUNK - 1
        d_dt = jnp.zeros((CHUNK, LANE), F32)
        d_acum = jnp.zeros((CHUNK, LANE), F32)
        d_acum_t = jnp.zeros((CHUNK, LANE), F32)

        def half_sums(v):
            lo = jnp.sum(jnp.where(lane_lo, v, 0.0), axis=1, keepdims=True)
            return lo, jnp.sum(v, axis=1, keepdims=True) - lo

        for g in range(2):
            bcols = slice(SSM_INNER + g * SSM_STATE, SSM_INNER + (g + 1) * SSM_STATE)
            ccols = slice(SSM_INNER + BC_DIM + g * SSM_STATE, SSM_INNER + BC_DIM + (g + 1) * SSM_STATE)
            bg, cg = xa_ref[:, bcols].astype(BF16), xa_ref[:, ccols].astype(BF16)
            cb = _dot_nt(cg, bg)
            dg_sum = jnp.zeros((CHUNK, CHUNK), F32)
            dcg = jnp.zeros((CHUNK, SSM_STATE), F32)
            dbg = jnp.zeros((CHUNK, SSM_STATE), F32)
            for j in range(4 * g, 4 * g + 4):
                h0 = 2 * j
                cols = slice(j * LANE, (j + 1) * LANE)
                xp, dyp = xa_ref[:, cols], dy_ref[:, cols]
                dtsel = _pair_sel(lane_lo, dt, h0)
                xdt = xp * dtsel
                xdt_b = xdt.astype(BF16)
                hp, dhp = hs_ref[cols, :], dh[cols, :]
                hp_b, dhp_b = hp.astype(BF16), dhp.astype(BF16)
                easel, dendsel = _pair_sel(lane_lo, ea, h0), _pair_sel(lane_lo, dend, h0)
                dx = None
                for hh, sel in ((h0, lane_lo), (h0 + 1, ~lane_lo)):
                    dyh = jnp.where(sel, dyp, 0.0).astype(BF16)
                    seg = acum[:, hh:hh + 1] - acum_t[hh:hh + 1, :]
                    dec = jnp.where(causal, jnp.exp(jnp.minimum(seg, 0.0)), 0.0)
                    mm = cb * dec
                    e = _dot_nt(dyh, xdt_b)
                    d = _dot_tn(mm.astype(BF16), dyh)
                    dx = d if dx is None else dx + d
                    dg_sum = dg_sum + dec * e
                    qm = mm * e
                    d_acum = d_acum + jnp.where(lane == hh, jnp.sum(qm, axis=1, keepdims=True), 0.0)
                    d_acum_t = d_acum_t - jnp.where(rowi == hh, jnp.sum(qm, axis=0, keepdims=True), 0.0)
                g2 = _dot_nt(bg, dhp_b)
                dx = dx + g2 * dendsel
                yoff = _dot_nt(cg, hp_b) * easel
                t_lo, t_hi = half_sums(xdt * g2 * dendsel)
                y_lo, y_hi = half_sums(dyp * yoff)
                hh_prod = dhp * hp
                s_lo = jnp.sum(jnp.where(row_lo, hh_prod, 0.0), keepdims=True).reshape(1, 1)
                s_hi = jnp.sum(hh_prod, keepdims=True).reshape(1, 1) - s_lo
                end_lo = ea_last[:, h0:h0 + 1] * s_lo + jnp.sum(t_lo, axis=0, keepdims=True)
                end_hi = ea_last[:, h0 + 1:h0 + 2] * s_hi + jnp.sum(t_hi, axis=0, keepdims=True)
                d_acum = d_acum + jnp.where(lane == h0, y_lo - t_lo + jnp.where(last_row, end_lo, 0.0), 0.0)
                d_acum = d_acum + jnp.where(lane == h0 + 1, y_hi - t_hi + jnp.where(last_row, end_hi, 0.0), 0.0)
                dye = (dyp * easel).astype(BF16)
                dcg = dcg + _dot(dye, hp_b)
                dbg = dbg + _dot((xdt * dendsel).astype(BF16), dhp_b)
                rowf = jnp.where(row_lo, ea_last[:, h0:h0 + 1], ea_last[:, h0 + 1:h0 + 2])
                dh[cols, :] = dhp * rowf + _dot_tn(dye, cg)
                dact_ref[:, cols] = dx * dtsel + dsk_ref[:, cols] * dyp
                x_lo, x_hi = half_sums(dx * xp)
                d_dt = d_dt + jnp.where(lane == h0, x_lo, 0.0) + jnp.where(lane == h0 + 1, x_hi, 0.0)
                ddsk_ref[:, cols] += jnp.sum(dyp * xp, axis=0, keepdims=True)
            dg_b = dg_sum.astype(BF16)
            dact_ref[:, ccols] = dcg + _dot(dg_b, bg)
            dact_ref[:, bcols] = dbg + _dot_tn(dg_b, cg)
        d_adt = _cum(_tri(False), d_acum + d_acum_t.T)
        d_dt = d_dt + d_adt * a_neg
        da_ref[...] += jnp.sum(d_adt * dt, axis=0, keepdims=True)
        d_raw = jnp.where(lane < SSM_HEADS, d_dt * _sigmoid(dtr_ref[...] + bias_ref[...]), 0.0)
        ddtr_ref[...] = d_raw.astype(BF16)
        dbias_ref[...] += jnp.sum(d_raw, axis=0, keepdims=True)

    rev = lambda c: (nch - 1 - c, 0)
    return pl.pallas_call(
        kern, name="ssd_bwd", grid=(nch,),
        in_specs=[pl.BlockSpec((CHUNK, XBC_DIM), rev), pl.BlockSpec((CHUNK, LANE), lambda c: (nch - 1 - c, OFF_DT // LANE)),
                  pl.BlockSpec((CHUNK, LANE), rev), pl.BlockSpec((None, SSM_INNER, SSM_STATE), lambda c: (nch - 1 - c, 0, 0)),
                  pl.BlockSpec((CHUNK, SSM_INNER), rev),
                  pl.BlockSpec((1, LANE), lambda c: (0, 0)), pl.BlockSpec((1, LANE), lambda c: (0, 0)),
                  pl.BlockSpec((1, SSM_INNER), lambda c: (0, 0))],
        out_specs=[pl.BlockSpec((CHUNK, XBC_DIM), rev), pl.BlockSpec((CHUNK, LANE), rev),
                   pl.BlockSpec((1, LANE), lambda c: (0, 0)), pl.BlockSpec((1, LANE), lambda c: (0, 0)),
                   pl.BlockSpec((1, SSM_INNER), lambda c: (0, 0))],
        out_shape=[jax.ShapeDtypeStruct((t, XBC_DIM), F32), jax.ShapeDtypeStruct((t, LANE), BF16),
                   jax.ShapeDtypeStruct((1, LANE), F32), jax.ShapeDtypeStruct((1, LANE), F32),
                   jax.ShapeDtypeStruct((1, SSM_INNER), F32)],
        scratch_shapes=[pltpu.VMEM((SSM_INNER, SSM_STATE), F32)],
        compiler_params=_cparams(("arbitrary",)))(xbc_act, proj, dt_sp, hstates, dy, dt_bias_p, a_log_p, dskip_t)


def _ssm_post_fwd(y, proj, g):
    def body(y_ref, z_ref, g_ref, o_ref):
        z = z_ref[...]
        yz = y_ref[...] * (z * _sigmoid(z))
        r = lax.rsqrt(jnp.mean(yz * yz, axis=-1, keepdims=True) + EPS)
        o_ref[...] = (yz * r * g_ref[...]).astype(BF16)
    return _rows("ssm_post_fwd", body, [("t", y), ("tc", proj, SSM_INNER, OFF_Z // SSM_INNER), ("p", g)],
                 [(SSM_INNER, BF16)])[0]


def _ssm_post_bwd(dmix, y, proj, g):
    def body(do_ref, y_ref, z_ref, g_ref, dy_ref, dz_ref, dg_ref):
        z, yv, dout = z_ref[...], y_ref[...], do_ref[...]
        sg = _sigmoid(z)
        gz = z * sg
        yz = yv * gz
        r = lax.rsqrt(jnp.mean(yz * yz, axis=-1, keepdims=True) + EPS)
        gd = dout * g_ref[...]
        dyz = r * gd - yz * (r * r * r * jnp.mean(yz * gd, axis=-1, keepdims=True))
        dy_ref[...] = dyz * gz
        dz_ref[...] = (dyz * yv * (sg * (1.0 + z * (1.0 - sg)))).astype(BF16)
        dg_ref[...] += jnp.sum(dout * yz * r, axis=0, keepdims=True)
    return _rows("ssm_post_bwd", body,
                 [("tc", dmix, SSM_INNER, 0), ("t", y), ("tc", proj, SSM_INNER, OFF_Z // SSM_INNER), ("p", g)],
                 [(SSM_INNER, F32), (SSM_INNER, BF16)], accs=[(1, SSM_INNER)])


def _ple_loss(gl, pp, x2, tgt):
    d = x2.shape[1]

    def body(gl_ref, pp_ref, x_ref, t_ref, dy_ref, dgl_ref, dpp_ref, sq_ref):
        s = _sigmoid(gl_ref[...])
        ppv = pp_ref[...]
        diff = x_ref[...] + s * ppv - t_ref[...]
        dy = diff * (1.0 / d)
        dy_ref[...] = dy
        dgl_ref[...] = (dy * ppv * s * (1.0 - s)).astype(BF16)
        dpp_ref[...] = (dy * s).astype(BF16)
        sq_ref[...] += jnp.sum(diff * diff, axis=0, keepdims=True)
    return _rows("ple_loss", body, [("t", gl), ("t", pp), ("t", x2), ("t", tgt)], [(d, F32), (d, BF16), (d, BF16)],
                 accs=[(1, d)])


def _pad_lanes(v, width=LANE):
    return jnp.pad(v, ((0, 0), (0, width - v.shape[1])))


def _local_step(x, p, tgt, wts):
    g_attn, g_ssm, g_ffn, g_ple = wts["attn_norm_g"], wts["ssm_norm_g"], wts["ffn_norm_g"], wts["ple_norm_g"]
    w_in_p, w_out_s, w_out_a = wts["w_in_p"], wts["w_out_ssm"], wts["w_out_attn"]
    w_up, w_down, w_gate, w_proj = wts["w_up"], wts["w_down"], wts["w_ple_gate"], wts["w_ple_proj"]
    gq_t = jnp.tile(wts["q_norm_g"], (1, ATTN_DIM // HEAD_DIM))
    gk_t = jnp.tile(wts["k_norm_g"], (1, KV_DIM // HEAD_DIM))
    dt_bias_p, a_log_p = _pad_lanes(wts["dt_bias"]), _pad_lanes(wts["a_log"])
    dskip_t = jnp.repeat(wts["d_skip"], HEAD_DIM, axis=1)

    h1 = _rms_fwd("rms_attn", x, g_attn)
    proj = _mm_nn("in_proj", [(h1, w_in_p)], F32)
    q_hm, kv_hm = _qknorm_fwd2(proj, gq_t, gk_t)
    pats = [_attn_fwd2(q_hm, kv_hm, d) for d in DILATIONS]
    attn_out, lse = _attn_merge2([o for o, _ in pats], [l for _, l in pats])
    xbc_act = _ssm_conv_fwd(proj, wts["ssm_conv_w"], wts["ssm_conv_b"])
    y_ssd, dt_sp, hstates = _ssd_fwd(xbc_act, proj, dt_bias_p, a_log_p, dskip_t)
    ssm_out = _ssm_post_fwd(y_ssd, proj, g_ssm)
    x1 = _mm_nn("out_proj", [(ssm_out, w_out_s), (attn_out, w_out_a)], F32, res=x)
    h2 = _rms_fwd("rms_ffn", x1, g_ffn)
    u = _mm_nn("ffn_up", [(h2, w_up)], F32, tn=1408, halves=True)
    a = _ffn_act_fwd(u, wts["ffn_conv_w"], wts["ffn_conv_b"])
    x2 = _mm_nn("ffn_down", [(a, w_down)], F32, res=x1)
    h3 = _rms_fwd("rms_ple", x2, g_ple)
    gl = _mm_nn("ple_gate", [(h3, w_gate)], F32)
    pb = p.astype(BF16)
    pp = _mm_nn("ple_proj", [(pb, w_proj)], F32)
    dy, dgl, dpp, sq = _ple_loss(gl, pp, x2, tgt)

    grads = {}
    grads["w_ple_proj"] = _mm_tn("g_ple_proj", pb, dpp)
    grads["w_ple_gate"] = _mm_tn("g_ple_gate", h3, dgl)
    dh3 = _mm_nt("d_h3", [(dgl, w_gate, 0)], F32)
    dx2, dx2b, grads["ple_norm_g"] = _rms_bwd("rms_ple_bwd", dh3, x2, g_ple, dy)
    da = _mm_nt("d_ffn_act", [(dx2b, w_down, 0)], F32, tn=1408)
    grads["w_down"] = _mm_tn("g_ffn_down", a, dx2b, tm=1408)
    du, gwg, gwv, gbg, gbv = _ffn_act_bwd(u, wts["ffn_conv_w"], wts["ffn_conv_b"], da)
    grads["ffn_conv_w"] = jnp.concatenate([gwg, gwv], axis=1)
    grads["ffn_conv_b"] = jnp.concatenate([gbg, gbv], axis=1)
    grads["w_up"] = jnp.concatenate([_mm_tn("g_ffn_up_gate", h2, du[0], tn=1408), _mm_tn("g_ffn_up_val", h2, du[1], tn=1408)],
                                    axis=1)
    dh2 = _mm_nt("d_h2", [(du[0], w_up, 0), (du[1], w_up, 1)], F32)
    dx1, dx1b, grads["ffn_norm_g"] = _rms_bwd("rms_ffn_bwd", dh2, x1, g_ffn, dx2)
    dmix = _mm_nt("d_mix", [(dx1b, jnp.concatenate([w_out_s, w_out_a], axis=0), 0)], F32)
    grads["w_out"] = jnp.concatenate([_mm_tn("g_out_attn", attn_out, dx1b), _mm_tn("g_out_ssm", ssm_out, dx1b)], axis=0)
    dy_ssd, dz, grads["ssm_norm_g"] = _ssm_post_bwd(dmix, y_ssd, proj, g_ssm)
    dact, ddtr, d_a, d_bias, d_dsk = _ssd_bwd(xbc_act, proj, dt_sp, hstates, dy_ssd, dt_bias_p, a_log_p, dskip_t)
    grads["dt_bias"] = d_bias[:, :SSM_HEADS]
    grads["a_log"] = d_a[:, :SSM_HEADS] * (-jnp.exp(wts["a_log"]))
    grads["d_skip"] = jnp.sum(d_dsk.reshape(SSM_HEADS, HEAD_DIM), axis=1)[None, :]
    dxbc, grads["ssm_conv_w"], grads["ssm_conv_b"] = _ssm_conv_bwd(proj, wts["ssm_conv_w"], wts["ssm_conv_b"], dact)
    do_hm, dsum = _attn_bwd_prep2(dmix, attn_out)
    dqs = [_attn_dq2(q_hm, kv_hm, do_hm, lse, dsum, d) for d in DILATIONS]
    dkvs = [_attn_dkv2(q_hm, kv_hm, do_hm, lse, dsum, d) for d in DILATIONS]
    dq, dk, dv, dgq, dgk = _qknorm_bwd2(proj, gq_t, gk_t, dqs, dkvs)
    grads["q_norm_g"] = jnp.sum(dgq.reshape(ATTN_DIM // HEAD_DIM, HEAD_DIM), axis=0)[None, :]
    grads["k_norm_g"] = jnp.sum(dgk.reshape(KV_DIM // HEAD_DIM, HEAD_DIM), axis=0)[None, :]
    dproj = jnp.concatenate([dxbc, dq, dz, dk, dv, ddtr], axis=1)
    grads["w_in_p"] = _mm_tn("g_in_proj", h1, dproj, tm=512)
    dh1 = _mm_nt("d_h1", [(dproj, w_in_p, 0)], F32)
    grad_x, _, grads["attn_norm_g"] = _rms_bwd("rms_attn_bwd", dh1, x, g_attn, dx1)
    return sq, grad_x, grads


MESH_IDS = pl.DeviceIdType.MESH
N_CHIPS = 4
ANY_SPEC = pl.BlockSpec(memory_space=pl.ANY)
PACK_ROWS = 3840
HALF_ROWS = PACK_ROWS // 2
SMALL_ROWS = 96


def _place():
    x, y, c = lax.axis_index("x"), lax.axis_index("y"), lax.axis_index("c")
    return x, y, c, [(1 - x, y), (x, 1 - y), (1 - x, 1 - y)]


def _gather_over_chips(arrs):
    n = len(arrs)
    split = [a.shape[0] % 64 == 0 for a in arrs]

    def body(*refs):
        ins, outs = refs[:n], refs[n:2 * n]
        ici_send, ici_recv, d2d_send, d2d_recv = refs[2 * n:2 * n + 4]
        x, y, c, chips = _place()
        mine = 2 * x + y

        def part(ref, a, core):
            if not split[a]:
                return ref
            half = arrs[a].shape[0] // 2
            return ref.at[pl.ds(core * half, half)]

        def ici(a, k, src_chip_slot, core):
            px, py = chips[k]
            return pltpu.make_async_remote_copy(
                src_ref=part(ins[a], a, core), dst_ref=part(outs[a].at[src_chip_slot], a, core), send_sem=ici_send.at[3 * a + k],
                recv_sem=ici_recv.at[3 * a + k], device_id=(px, py, c), device_id_type=MESH_IDS)

        def d2d(a, k, core):
            px, py = chips[k]
            piece = part(outs[a].at[2 * px + py], a, core)
            return pltpu.make_async_remote_copy(src_ref=piece, dst_ref=piece, send_sem=d2d_send.at[3 * a + k],
                                                recv_sem=d2d_recv.at[3 * a + k], device_id=(x, y, 1 - c), device_id_type=MESH_IDS)

        for a in range(n):
            for k in range(3):
                ici(a, k, mine, c).start()
        passed = []
        for a in range(n):
            for k, (px, py) in enumerate(chips):
                ici(a, k, 2 * px + py, c).wait_recv()
                if split[a]:
                    fwd = d2d(a, k, c)
                    fwd.start()
                    passed.append(fwd)
        for a in range(n):
            for k in range(3):
                if split[a]:
                    d2d(a, k, 1 - c).wait_recv()
                ici(a, k, mine, c).wait_send()
        for fwd in passed:
            fwd.wait_send()

    sems = [pltpu.SemaphoreType.DMA((3 * n,))] * 4
    return pl.pallas_call(
        body, name="gather_weights", in_specs=[ANY_SPEC] * n, out_specs=[ANY_SPEC] * n,
        out_shape=[jax.ShapeDtypeStruct((N_CHIPS,) + a.shape, a.dtype) for a in arrs], scratch_shapes=sems)(*arrs)


def _swap_halves(g):
    def body(g_ref, o_ref, send, recv):
        x, y, c, _ = _place()
        cps = [pltpu.make_async_remote_copy(src_ref=g_ref.at[q, 1 - c], dst_ref=o_ref.at[q], send_sem=send.at[q], recv_sem=recv.at[q],
                                            device_id=(x, y, 1 - c), device_id_type=MESH_IDS) for q in range(N_CHIPS)]
        for cp in cps:
            cp.start()
        for cp in cps:
            cp.wait()

    return pl.pallas_call(
        body, name="grad_swap_halves", in_specs=[ANY_SPEC], out_specs=ANY_SPEC,
        out_shape=jax.ShapeDtypeStruct((N_CHIPS,) + g.shape[2:], g.dtype),
        scratch_shapes=[pltpu.SemaphoreType.DMA((N_CHIPS,)), pltpu.SemaphoreType.DMA((N_CHIPS,))])(g)


def _add_halves(g, got, c_idx, tm=384):
    rows = g.shape[2]

    def kern(c_ref, g_ref, r_ref, o_ref):
        o_ref[...] = (g_ref[...] + r_ref[...]).astype(BF16)

    return pl.pallas_call(
        kern, name="grad_add_halves",
        grid_spec=pltpu.PrefetchScalarGridSpec(
            num_scalar_prefetch=1, grid=(N_CHIPS, rows // tm),
            in_specs=[pl.BlockSpec((None, None, tm, D_MODEL), lambda q, i, c_ref: (q, c_ref[0], i, 0)),
                      pl.BlockSpec((None, tm, D_MODEL), lambda q, i, c_ref: (q, i, 0))],
            out_specs=pl.BlockSpec((None, tm, D_MODEL), lambda q, i, c_ref: (q, i, 0))),
        out_shape=jax.ShapeDtypeStruct((N_CHIPS, rows, D_MODEL), BF16),
        compiler_params=_cparams(("parallel", "parallel")))(c_idx, g, got)


def _scatter_over_chips(s):
    def body(s_ref, o_ref, send, recv):
        x, y, c, chips = _place()
        mine = 2 * x + y
        for k, (px, py) in enumerate(chips):
            pltpu.make_async_remote_copy(src_ref=s_ref.at[2 * px + py], dst_ref=o_ref.at[mine], send_sem=send.at[k], recv_sem=recv.at[k],
                                         device_id=(px, py, c), device_id_type=MESH_IDS).start()
        for k, (px, py) in enumerate(chips):
            pltpu.make_async_remote_copy(src_ref=s_ref.at[2 * px + py], dst_ref=o_ref.at[2 * px + py], send_sem=send.at[k],
                                         recv_sem=recv.at[k], device_id=(px, py, c), device_id_type=MESH_IDS).wait()

    return pl.pallas_call(
        body, name="grad_scatter_chips", in_specs=[ANY_SPEC], out_specs=ANY_SPEC,
        out_shape=jax.ShapeDtypeStruct(s.shape, s.dtype),
        scratch_shapes=[pltpu.SemaphoreType.DMA((3,)), pltpu.SemaphoreType.DMA((3,))])(s)


def _sum_chips(own, parts, order, tm=384):
    rows = parts.shape[1]

    def kern(o_idx, a_ref, b_ref, c_ref, d_ref, o_ref):
        o_ref[...] = ((a_ref[...].astype(F32) + b_ref[...].astype(F32)) + c_ref[...].astype(F32)) + d_ref[...].astype(F32)

    def spec(k):
        return pl.BlockSpec((None, tm, D_MODEL), functools.partial(lambda i, o_idx, k: (o_idx[k], i, 0), k=k))

    return pl.pallas_call(
        kern, name="grad_sum_chips",
        grid_spec=pltpu.PrefetchScalarGridSpec(
            num_scalar_prefetch=1, grid=(rows // tm,), in_specs=[spec(0), spec(1), spec(2), spec(3)],
            out_specs=pl.BlockSpec((tm, D_MODEL), lambda i, o_idx: (i, 0))),
        out_shape=jax.ShapeDtypeStruct((rows, D_MODEL), F32), compiler_params=_cparams(("parallel",)))(order, own, parts, parts, parts)


def _share_with_sibling(s):
    def body(s_ref, o_ref, send, recv):
        x, y, c, _ = _place()
        cp = pltpu.make_async_remote_copy(src_ref=s_ref, dst_ref=o_ref, send_sem=send, recv_sem=recv,
                                          device_id=(x, y, 1 - c), device_id_type=MESH_IDS)
        cp.start()
        cp.wait()

    return pl.pallas_call(
        body, name="grad_share_sibling", in_specs=[ANY_SPEC], out_specs=ANY_SPEC,
        out_shape=jax.ShapeDtypeStruct(s.shape, s.dtype),
        scratch_shapes=[pltpu.SemaphoreType.DMA, pltpu.SemaphoreType.DMA])(s)


def _allreduce_small(v):
    def body(v_ref, o_ref, land, send, recv):
        x, y, c, _ = _place()
        me = 4 * x + 2 * y + c
        land[me] = v_ref[...]
        cps = []
        for rel in range(1, 8):
            bx, by, bc = (rel >> 2) & 1, (rel >> 1) & 1, rel & 1
            peer = (1 - x if bx else x, 1 - y if by else y, 1 - c if bc else c)
            cps.append(pltpu.make_async_remote_copy(src_ref=v_ref, dst_ref=land.at[me], send_sem=send.at[rel - 1],
                                                    recv_sem=recv.at[rel - 1], device_id=peer, device_id_type=MESH_IDS))
        for cp in cps:
            cp.start()
        for cp in cps:
            cp.wait()
        acc = land[0]
        for d in range(1, 8):
            acc = acc + land[d]
        o_ref[...] = acc

    vm = pl.BlockSpec(memory_space=pltpu.VMEM)
    return pl.pallas_call(
        body, name="allreduce_small", in_specs=[vm], out_specs=vm, out_shape=jax.ShapeDtypeStruct(v.shape, F32),
        scratch_shapes=[pltpu.VMEM((8,) + v.shape, F32), pltpu.SemaphoreType.DMA((7,)), pltpu.SemaphoreType.DMA((7,))])(v)


def _adamw(name, w, g, m, v):
    rows, cols = w.shape
    tm = rows
    if rows * cols > 128 * 1024:
        tm = max(d for d in range(8, 257, 8) if rows % d == 0)
    c1 = 1.0 / (1.0 - ADAM_B1 ** ADAM_STEP)
    c2 = 1.0 / (1.0 - ADAM_B2 ** ADAM_STEP)

    def kern(w_ref, g_ref, m_ref, v_ref, d_ref, mo_ref, vo_ref):
        gv = g_ref[...]
        mn = ADAM_B1 * m_ref[...] + (1.0 - ADAM_B1) * gv
        vn = ADAM_B2 * v_ref[...] + (1.0 - ADAM_B2) * (gv * gv)
        d_ref[...] = -ADAM_LR * ((mn * c1) / (jnp.sqrt(vn * c2) + ADAM_EPS) + ADAM_WD * w_ref[...])
        mo_ref[...] = mn
        vo_ref[...] = vn

    spec = pl.BlockSpec((tm, cols), lambda i: (i, 0))
    return pl.pallas_call(
        kern, name=name, grid=(rows // tm,), in_specs=[spec] * 4, out_specs=[spec] * 3,
        out_shape=[jax.ShapeDtypeStruct(w.shape, F32)] * 3, compiler_params=_cparams(("parallel",)))(w, g, m, v)


SHARDED = (("w_in", 1), ("w_out", 0), ("w_up", 1), ("w_down", 0), ("w_ple_gate", 0), ("w_ple_proj", 1),
           ("ssm_conv_w", 1), ("ffn_conv_w", 1))
MATRICES = ("w_in", "w_out", "w_up", "w_down", "w_ple_gate", "w_ple_proj")
REPLICATED = ("attn_norm_g", "q_norm_g", "k_norm_g", "ssm_conv_b", "dt_bias", "a_log", "d_skip", "ssm_norm_g",
              "ffn_norm_g", "ffn_conv_b", "ple_norm_g")
WEIGHT_ORDER = ("attn_norm_g", "w_in", "q_norm_g", "k_norm_g", "ssm_conv_w", "ssm_conv_b", "dt_bias", "a_log", "d_skip",
                "ssm_norm_g", "w_out", "ffn_norm_g", "w_up", "ffn_conv_w", "ffn_conv_b", "w_down", "ple_norm_g",
                "w_ple_gate", "w_ple_proj")


def _join_chips(g, axis):
    if axis == 0:
        return g.reshape(g.shape[0] * g.shape[1], g.shape[2])
    return jnp.transpose(g, (1, 0, 2)).reshape(g.shape[1], g.shape[0] * g.shape[2])


def _split_chips(g, axis):
    if axis == 0:
        return g.reshape(N_CHIPS, -1)
    r, c = g.shape
    return jnp.transpose(g.reshape(r, N_CHIPS, c // N_CHIPS), (1, 0, 2)).reshape(N_CHIPS, -1)


def _pack_small(vals):
    flat = jnp.concatenate([v.reshape(-1) for v in vals])
    return jnp.pad(flat, (0, SMALL_ROWS * LANE - flat.shape[0])).reshape(SMALL_ROWS, LANE)


def _unpack_small(packed, like):
    flat, out, off = packed.reshape(-1), [], 0
    for v in like:
        out.append(flat[off:off + v.size].reshape(v.shape))
        off += v.size
    return out


def kernel(x, p, attn_norm_g, w_in, q_norm_g, k_norm_g, ssm_conv_w, ssm_conv_b, dt_bias, a_log, d_skip, ssm_norm_g, w_out, ffn_norm_g, w_up, ffn_conv_w, ffn_conv_b, w_down, ple_norm_g, w_ple_gate, w_ple_proj, loss_target, m_attn_norm_g, m_w_in, m_q_norm_g, m_k_norm_g, m_ssm_conv_w, m_ssm_conv_b, m_dt_bias, m_a_log, m_d_skip, m_ssm_norm_g, m_w_out, m_ffn_norm_g, m_w_up, m_ffn_conv_w, m_ffn_conv_b, m_w_down, m_ple_norm_g, m_w_ple_gate, m_w_ple_proj, v_attn_norm_g, v_w_in, v_q_norm_g, v_k_norm_g, v_ssm_conv_w, v_ssm_conv_b, v_dt_bias, v_a_log, v_d_skip, v_ssm_norm_g, v_w_out, v_ffn_norm_g, v_w_up, v_ffn_conv_w, v_ffn_conv_b, v_w_down, v_ple_norm_g, v_w_ple_gate, v_w_ple_proj):
    given = dict(locals())
    w2 = {n: given[n].reshape(given[n].shape[-2:]) if given[n].ndim == 3 else given[n] for n in WEIGHT_ORDER}
    m2 = {n: given["m_" + n].reshape(w2[n].shape) for n in WEIGHT_ORDER}
    v2 = {n: given["v_" + n].reshape(w2[n].shape) for n in WEIGHT_ORDER}

    cx, cy, cc = lax.axis_index("x"), lax.axis_index("y"), lax.axis_index("c")
    chip = 2 * cx + cy
    shards = [w2[n].astype(BF16) if n in MATRICES else w2[n] for n, _ in SHARDED]
    full = {n: _join_chips(lax.dynamic_update_index_in_dim(g, s, chip, 0), ax)
            for (n, ax), g, s in zip(SHARDED, _gather_over_chips(shards), shards)}
    win = full["w_in"]
    w_in_p = jnp.concatenate([win[:, 2048:3584], win[:, 0:512], win[:, 1024:2048], win[:, 512:768], win[:, 768:1024],
                              win[:, 3584:3600], jnp.zeros((D_MODEL, PROJ_P - IN_PROJ), BF16)], axis=1)
    wts = {n: w2[n] for n in REPLICATED}
    wts.update(w_in_p=w_in_p, w_out_attn=full["w_out"][:ATTN_DIM], w_out_ssm=full["w_out"][ATTN_DIM:], w_up=full["w_up"],
               w_down=full["w_down"], w_ple_gate=full["w_ple_gate"], w_ple_proj=full["w_ple_proj"],
               ssm_conv_w=full["ssm_conv_w"], ffn_conv_w=full["ffn_conv_w"])

    sq, grad_x, grads = _local_step(x[0], p[0, 0], loss_target[0], wts)
    gi = grads.pop("w_in_p")
    grads["w_in"] = jnp.concatenate([gi[:, OFF_Q:OFF_Q + ATTN_DIM], gi[:, OFF_K:OFF_K + KV_DIM], gi[:, OFF_V:OFF_V + KV_DIM],
                                     gi[:, OFF_Z:OFF_Z + SSM_INNER], gi[:, OFF_XBC:OFF_XBC + XBC_DIM], gi[:, OFF_DT:OFF_DT + SSM_HEADS]],
                                    axis=1)

    packed = jnp.concatenate([_split_chips(grads[n], ax) for n, ax in SHARDED], axis=1)
    packed = jnp.pad(packed, ((0, 0), (0, PACK_ROWS * D_MODEL - packed.shape[1]))).reshape(N_CHIPS, 2, HALF_ROWS, D_MODEL)
    chip_sums = _add_halves(packed, _swap_halves(packed), cc.astype(jnp.int32).reshape(1))
    order = jnp.stack([chip, 2 * (1 - cx) + cy, 2 * cx + (1 - cy), 2 * (1 - cx) + (1 - cy)]).astype(jnp.int32)
    mine_half = _sum_chips(chip_sums, _scatter_over_chips(chip_sums), order)
    other_half = _share_with_sibling(mine_half)
    reduced = jnp.where(cc == 0, jnp.stack([mine_half, other_half]), jnp.stack([other_half, mine_half])).reshape(-1)
    g_shard, off = {}, 0
    for n, _ in SHARDED:
        g_shard[n] = reduced[off:off + w2[n].size].reshape(w2[n].shape)
        off += w2[n].size

    small = _allreduce_small(_pack_small([grads[n] for n in REPLICATED] + [jnp.sum(sq).reshape(1)]))
    small_vals = _unpack_small(small, [w2[n] for n in REPLICATED] + [jnp.zeros((1,), F32)])
    for n, g in zip(REPLICATED, small_vals):
        g_shard[n] = g
    loss = (0.5 / D_MODEL) * small_vals[-1][0]

    delta, new_m, new_v = {}, {}, {}
    for n, _ in SHARDED:
        delta[n], new_m[n], new_v[n] = _adamw("adamw_" + n, w2[n], g_shard[n], m2[n], v2[n])
    sm = _adamw("adamw_small", _pack_small([w2[n] for n in REPLICATED]), _pack_small([g_shard[n] for n in REPLICATED]),
                _pack_small([m2[n] for n in REPLICATED]), _pack_small([v2[n] for n in REPLICATED]))
    for dst, packed_out in zip((delta, new_m, new_v), sm):
        for n, val in zip(REPLICATED, _unpack_small(packed_out, [w2[n] for n in REPLICATED])):
            dst[n] = val

    def shaped(d):
        return [d[n].reshape(given[n].shape) for n in WEIGHT_ORDER]
    return (loss, grad_x[None], *shaped(g_shard), *shaped(delta), *shaped(new_m), *shaped(new_v))
```

```python
import functools

import jax
import jax.numpy as jnp
from jax import lax
from jax.experimental import pallas as pl
from jax.experimental.pallas import tpu as pltpu

F32 = jnp.float32
BF16 = jnp.bfloat16

D_MODEL = 1024
HEAD_DIM = 64
ATTN_DIM = 512
KV_DIM = 256
N_KV = 4
SSM_INNER = 1024
SSM_HEADS = 16
SSM_STATE = 128
BC_DIM = 256
XBC_DIM = SSM_INNER + 2 * BC_DIM
MIX_DIM = ATTN_DIM + SSM_INNER
IN_PROJ = 3600
D_FF = 2816
PLE_DIM = 256
CHUNK = 128
DILATIONS = (1, 4, 16)
EPS = 1e-6
ADAM_LR, ADAM_B1, ADAM_B2, ADAM_EPS, ADAM_WD, ADAM_STEP = 0.001, 0.9, 0.999, 1e-08, 0.01, 10

PROJ_P = 3712
OFF_XBC, OFF_Q, OFF_Z, OFF_K, OFF_V, OFF_DT = 0, 1536, 2048, 3072, 3328, 3584
LANE = 128
VMEM_LIMIT = 48 * 1024 * 1024
NEG = -1e30


def _cparams(sem):
    return pltpu.CompilerParams(dimension_semantics=sem, vmem_limit_bytes=VMEM_LIMIT)


def _sigmoid(x):
    return 1.0 / (1.0 + jnp.exp(-x))


def _dot(a, b):
    return jnp.dot(a, b, preferred_element_type=F32)


def _dot_nt(a, b):
    return lax.dot_general(a, b, (((1,), (1,)), ((), ())), preferred_element_type=F32)


def _dot_tn(a, b):
    return lax.dot_general(a, b, (((0,), (0,)), ((), ())), preferred_element_type=F32)


def _dot_split(x, m):
    hi = x.astype(BF16)
    lo = (x - hi.astype(F32)).astype(BF16)
    return _dot(hi, m) + _dot(lo, m)


def _rows(name, body, ins, outs, accs=(), tm=512):
    t_rows = next(s[1].shape[0] for s in ins if s[0] in ("t", "tc"))
    tm = min(tm, t_rows)
    in_specs, args = [], []
    for s in ins:
        if s[0] == "t":
            in_specs.append(pl.BlockSpec((tm, s[1].shape[1]), lambda i: (i, 0)))
        elif s[0] == "tc":
            in_specs.append(pl.BlockSpec((tm, s[2]), functools.partial(lambda i, c: (i, c), c=s[3])))
        else:
            in_specs.append(pl.BlockSpec(s[1].shape, lambda i: (0, 0)))
        args.append(s[1])
    out_shape = [jax.ShapeDtypeStruct((t_rows, w), dt) for w, dt in outs]
    out_specs = [pl.BlockSpec((tm, w), lambda i: (i, 0)) for w, _ in outs]
    out_shape += [jax.ShapeDtypeStruct(a, F32) for a in accs]
    out_specs += [pl.BlockSpec(a, lambda i: (0, 0)) for a in accs]
    n_acc = len(accs)

    def kern(*refs):
        if n_acc:
            @pl.when(pl.program_id(0) == 0)
            def _():
                for r in refs[len(refs) - n_acc:]:
                    r[...] = jnp.zeros(r.shape, F32)
        body(*refs)

    return pl.pallas_call(
        kern, name=name, grid=(t_rows // tm,), in_specs=in_specs, out_specs=out_specs, out_shape=out_shape,
        compiler_params=_cparams(("arbitrary",) if n_acc else ("parallel",)))(*args)


NCHUNK = 512


def _col_chunks(n):
    return [(c, min(NCHUNK, n - c)) for c in range(0, n, NCHUNK)]


def _mm_nn(name, pairs, out_dtype, res=None, tm=512, tn=None, halves=False):
    m, n = pairs[0][0].shape[0], pairs[0][1].shape[1]
    tn = n if tn is None else tn
    tm = min(tm, m)
    np_ = len(pairs)
    if halves:
        per = n // 2 // tn
        out_spec = pl.BlockSpec((None, tm, tn), lambda j, i: (j // per, i, j % per))
        out_shape = jax.ShapeDtypeStruct((2, m, n // 2), out_dtype)
    else:
        out_spec = pl.BlockSpec((tm, tn), lambda j, i: (i, j))
        out_shape = jax.ShapeDtypeStruct((m, n), out_dtype)
    in_specs, args = [], []
    for a, w in pairs:
        in_specs += [pl.BlockSpec((tm, a.shape[1]), lambda j, i: (i, 0)), pl.BlockSpec((w.shape[0], tn), lambda j, i: (0, j))]
        args += [a, w]
    if res is not None:
        in_specs.append(pl.BlockSpec((tm, tn), lambda j, i: (i, j)))
        args.append(res)

    def kern(*refs):
        o_ref = refs[-1]
        for c0, cw in _col_chunks(tn):
            acc = None
            for q in range(np_):
                d = _dot(refs[2 * q][...], refs[2 * q + 1][:, c0:c0 + cw])
                acc = d if acc is None else acc + d
            if res is not None:
                acc = acc + refs[2 * np_][:, c0:c0 + cw]
            o_ref[:, c0:c0 + cw] = acc.astype(o_ref.dtype)

    return pl.pallas_call(
        kern, name=name, grid=(n // tn, m // tm), in_specs=in_specs, out_specs=out_spec, out_shape=out_shape,
        compiler_params=_cparams(("parallel", "parallel")))(*args)


def _mm_nt(name, pairs, out_dtype, tm=512, tn=None):
    m, n = pairs[0][0].shape[-2], pairs[0][1].shape[0]
    tn = n if tn is None else tn
    tm = min(tm, m)
    np_ = len(pairs)
    in_specs, args = [], []
    for a, w, kb, *lead in pairs:
        if lead:
            in_specs.append(pl.BlockSpec((None, tm, a.shape[2]), functools.partial(lambda j, i, ld: (ld, i, 0), ld=lead[0])))
        else:
            in_specs.append(pl.BlockSpec((tm, a.shape[1]), lambda j, i: (i, 0)))
        in_specs.append(pl.BlockSpec((tn, a.shape[-1]), functools.partial(lambda j, i, kb: (j, kb), kb=kb)))
        args += [a, w]

    def kern(*refs):
        o_ref = refs[-1]
        for c0, cw in _col_chunks(tn):
            acc = None
            for q in range(np_):
                d = _dot_nt(refs[2 * q][...], refs[2 * q + 1][c0:c0 + cw, :])
                acc = d if acc is None else acc + d
            o_ref[:, c0:c0 + cw] = acc.astype(o_ref.dtype)

    return pl.pallas_call(
        kern, name=name, grid=(n // tn, m // tm), in_specs=in_specs,
        out_specs=pl.BlockSpec((tm, tn), lambda j, i: (i, j)),
        out_shape=jax.ShapeDtypeStruct((m, n), out_dtype), compiler_params=_cparams(("parallel", "parallel")))(*args)


def _mm_tn(name, a, b, tm=None, tn=None, tk=1024, chip_cols=False):
    t, m = a.shape
    n = b.shape[-1] * (2 if b.ndim == 3 else 1)
    tm = m if tm is None else tm
    tn = n if tn is None else tn
    tk = min(tk, t)
    if b.ndim == 3:
        per = n // 2 // tn
        b_spec = pl.BlockSpec((None, tk, tn), lambda i, j, k: (j // per, k, j % per))
    else:
        b_spec = pl.BlockSpec((tk, tn), lambda i, j, k: (k, j))
    if chip_cols:
        out_spec = pl.BlockSpec((None, tm, tn), lambda i, j, k: (j, i, 0))
        out_shape = jax.ShapeDtypeStruct((n // tn, m, tn), F32)
    else:
        out_spec = pl.BlockSpec((tm, tn), lambda i, j, k: (i, j))
        out_shape = jax.ShapeDtypeStruct((m, n), F32)

    def kern(a_ref, b_ref, o_ref):
        @pl.when(pl.program_id(2) == 0)
        def _():
            o_ref[...] = jnp.zeros(o_ref.shape, F32)
        for c0, cw in _col_chunks(tn):
            o_ref[:, c0:c0 + cw] += _dot_tn(a_ref[...], b_ref[:, c0:c0 + cw])

    return pl.pallas_call(
        kern, name=name, grid=(m // tm, n // tn, t // tk),
        in_specs=[pl.BlockSpec((tk, tm), lambda i, j, k: (k, i)), b_spec], out_specs=out_spec, out_shape=out_shape,
        compiler_params=_cparams(("parallel", "parallel", "arbitrary")))(a, b)


def _rms_fwd(name, x, g):
    def body(x_ref, g_ref, h_ref):
        xv = x_ref[...]
        r = lax.rsqrt(jnp.mean(xv * xv, axis=-1, keepdims=True) + EPS)
        h_ref[...] = (xv * r * g_ref[...]).astype(BF16)
    return _rows(name, body, [("t", x), ("p", g)], [(x.shape[1], BF16)])[0]


def _rms_bwd(name, dh, x, g, dres):
    d = x.shape[1]

    def body(dh_ref, x_ref, g_ref, dres_ref, dx_ref, dxb_ref, dg_ref):
        xv, dhv = x_ref[...], dh_ref[...]
        r = lax.rsqrt(jnp.mean(xv * xv, axis=-1, keepdims=True) + EPS)
        gd = dhv * g_ref[...]
        dx = dres_ref[...] + r * gd - xv * (r * r * r * jnp.mean(xv * gd, axis=-1, keepdims=True))
        dx_ref[...] = dx
        dxb_ref[...] = dx.astype(BF16)
        dg_ref[...] += jnp.sum(dhv * xv * r, axis=0, keepdims=True)
    return _rows(name, body, [("t", dh), ("t", x), ("p", g), ("t", dres)], [(d, F32), (d, BF16)], accs=[(1, d)])


def _head_mean_matrix(width):
    i = jnp.arange(width) // HEAD_DIM
    return jnp.where(i[:, None] == i[None, :], 1.0 / HEAD_DIM, 0.0).astype(BF16)


ATT_SPAN = 2048
N_QH = 8


def _lane_lo(rows):
    return lax.broadcasted_iota(jnp.int32, (rows, LANE), 1) < HEAD_DIM


def _swap_halves_lanes(x):
    return pltpu.roll(x, HEAD_DIM, axis=1)


def _qknorm_fwd2(proj, gq_t, gk_t, tm=256):
    t = proj.shape[0]
    bq, bk = _head_mean_matrix(ATTN_DIM), _head_mean_matrix(KV_DIM)
    scale = HEAD_DIM ** -0.5

    def kern(q_ref, k_ref, v_ref, gq_ref, gk_ref, bq_ref, bk_ref, qo_ref, kvo_ref):
        q, k, v = q_ref[...], k_ref[...], v_ref[...]
        qn = (q * lax.rsqrt(_dot_split(q * q, bq_ref[...]) + EPS) * gq_ref[...]) * scale
        kn = k * lax.rsqrt(_dot_split(k * k, bk_ref[...]) + EPS) * gk_ref[...]
        lo = _lane_lo(tm)
        for j in range(N_KV):
            blk = qn[:, j * LANE:(j + 1) * LANE]
            qo_ref[2 * j] = jnp.where(lo, blk, 0.0)
            qo_ref[2 * j + 1] = jnp.where(lo, _swap_halves_lanes(blk), 0.0)
        for j in range(2):
            kb, vb = kn[:, j * LANE:(j + 1) * LANE], v[:, j * LANE:(j + 1) * LANE]
            kvo_ref[2 * j] = jnp.where(lo, kb, _swap_halves_lanes(vb))
            kvo_ref[2 * j + 1] = jnp.where(lo, _swap_halves_lanes(kb), vb)

    col = lambda w, idx: pl.BlockSpec((tm, w), functools.partial(lambda i, idx: (i, idx), idx=idx))
    par = lambda a: pl.BlockSpec(a.shape, lambda i: (0, 0))
    return pl.pallas_call(
        kern, name="qknorm_fwd", grid=(t // tm,),
        in_specs=[col(ATTN_DIM, OFF_Q // ATTN_DIM), col(KV_DIM, OFF_K // KV_DIM), col(KV_DIM, OFF_V // KV_DIM),
                  par(gq_t), par(gk_t), par(bq), par(bk)],
        out_specs=[pl.BlockSpec((N_QH, tm, LANE), lambda i: (0, i, 0)), pl.BlockSpec((N_KV, tm, LANE), lambda i: (0, i, 0))],
        out_shape=[jax.ShapeDtypeStruct((N_QH, t, LANE), F32), jax.ShapeDtypeStruct((N_KV, t, LANE), F32)],
        compiler_params=_cparams(("parallel",)))(proj, proj, proj, gq_t, gk_t, bq, bk)


def _qknorm_bwd2(proj, gq_t, gk_t, dqs, dkvs, tm=256):
    t = proj.shape[0]
    bq, bk = _head_mean_matrix(ATTN_DIM), _head_mean_matrix(KV_DIM)
    scale = HEAD_DIM ** -0.5

    def kern(q_ref, k_ref, gq_ref, gk_ref, bq_ref, bk_ref, a1, a2, a3, b1, b2, b3, dq_ref, dk_ref, dv_ref, dgq_ref, dgk_ref):
        @pl.when(pl.program_id(0) == 0)
        def _():
            dgq_ref[...] = jnp.zeros(dgq_ref.shape, F32)
            dgk_ref[...] = jnp.zeros(dgk_ref.shape, F32)
        lo = _lane_lo(tm)
        sq = [a1[h] + a2[h] + a3[h] for h in range(N_QH)]
        skv = [b1[h] + b2[h] + b3[h] for h in range(N_KV)]
        dqn = jnp.concatenate([jnp.where(lo, sq[2 * j], _swap_halves_lanes(sq[2 * j + 1])) for j in range(N_KV)], axis=1) * scale
        dkn = jnp.concatenate([jnp.where(lo, skv[2 * j], _swap_halves_lanes(skv[2 * j + 1])) for j in range(2)], axis=1)
        dv = jnp.concatenate([jnp.where(lo, _swap_halves_lanes(skv[2 * j]), skv[2 * j + 1]) for j in range(2)], axis=1)
        q, k = q_ref[...], k_ref[...]
        rq = lax.rsqrt(_dot_split(q * q, bq_ref[...]) + EPS)
        rk = lax.rsqrt(_dot_split(k * k, bk_ref[...]) + EPS)
        gdq, gdk = dqn * gq_ref[...], dkn * gk_ref[...]
        dq_ref[...] = (rq * gdq - q * (rq * rq * rq * _dot_split(q * gdq, bq_ref[...]))).astype(BF16)
        dk_ref[...] = (rk * gdk - k * (rk * rk * rk * _dot_split(k * gdk, bk_ref[...]))).astype(BF16)
        dv_ref[...] = dv.astype(BF16)
        dgq_ref[...] += jnp.sum(dqn * q * rq, axis=0, keepdims=True)
        dgk_ref[...] += jnp.sum(dkn * k * rk, axis=0, keepdims=True)

    col = lambda w, idx: pl.BlockSpec((tm, w), functools.partial(lambda i, idx: (i, idx), idx=idx))
    par = lambda a: pl.BlockSpec(a.shape, lambda i: (0, 0))
    blk = lambda n: pl.BlockSpec((n, tm, LANE), lambda i: (0, i, 0))
    row = lambda w: pl.BlockSpec((tm, w), lambda i: (i, 0))
    acc = lambda w: pl.BlockSpec((1, w), lambda i: (0, 0))
    return pl.pallas_call(
        kern, name="qknorm_bwd", grid=(t // tm,),
        in_specs=[col(ATTN_DIM, OFF_Q // ATTN_DIM), col(KV_DIM, OFF_K // KV_DIM), par(gq_t), par(gk_t), par(bq), par(bk)]
        + [blk(N_QH)] * 3 + [blk(N_KV)] * 3,
        out_specs=[row(ATTN_DIM), row(KV_DIM), row(KV_DIM), acc(ATTN_DIM), acc(KV_DIM)],
        out_shape=[jax.ShapeDtypeStruct((t, ATTN_DIM), BF16), jax.ShapeDtypeStruct((t, KV_DIM), BF16),
                   jax.ShapeDtypeStruct((t, KV_DIM), BF16), jax.ShapeDtypeStruct((1, ATTN_DIM), F32),
                   jax.ShapeDtypeStruct((1, KV_DIM), F32)],
        compiler_params=_cparams(("arbitrary",)))(proj, proj, gq_t, gk_t, bq, bk, *dqs, *dkvs)


def _att_rows(b, r, dil):
    if dil == 1:
        return pl.ds(b * CHUNK, CHUNK)
    return pl.ds(b * CHUNK * dil + r, CHUNK, stride=dil)


def _for_residues(dil, unit):
    for r in range(dil):
        unit(r, 0)


def _band_qk(first):
    ri = lax.broadcasted_iota(jnp.int32, (CHUNK, 2 * CHUNK), 0)
    cj = lax.broadcasted_iota(jnp.int32, (CHUNK, 2 * CHUNK), 1)
    band = (cj - ri >= 0) & (cj - ri <= CHUNK)
    return band if first is None else band & (jnp.logical_not(first) | (cj >= CHUNK))


def _band_kq(last):
    rj = lax.broadcasted_iota(jnp.int32, (CHUNK, 2 * CHUNK), 0)
    ci = lax.broadcasted_iota(jnp.int32, (CHUNK, 2 * CHUNK), 1)
    band = (ci - rj >= 0) & (ci - rj <= CHUNK)
    return band if last is None else band & (jnp.logical_not(last) | (ci < CHUNK))


def _att_specs(t, dil):
    sub = CHUNK * dil
    nb, last = ATT_SPAN // sub, t // sub - 1
    cur = lambda heads: pl.BlockSpec((heads, ATT_SPAN, LANE), lambda kh, n: (kh, n, 0))
    prev = lambda heads: pl.BlockSpec((heads, sub, LANE), lambda kh, n: (kh, jnp.maximum(n * nb - 1, 0), 0))
    nxt = lambda heads: pl.BlockSpec((heads, sub, LANE), lambda kh, n: (kh, jnp.minimum((n + 1) * nb, last), 0))
    return sub, nb, cur, prev, nxt


def _attn_fwd2(q, kv, dil):
    t = q.shape[1]
    sub, nb, cur, prev, _ = _att_specs(t, dil)

    def kern(q_ref, kvp_ref, kvc_ref, o_ref, lse_ref):
        n = pl.program_id(1)
        lane = lax.broadcasted_iota(jnp.int32, (CHUNK, LANE), 1)
        for b in range(nb):
            mask = _band_qk((n == 0) if b == 0 else None)

            def unit(r, carry, b=b, mask=mask):
                rows = _att_rows(b, r, dil)
                kvp = kvc_ref[_att_rows(b - 1, r, dil), :] if b > 0 else kvp_ref[_att_rows(0, r, dil), :]
                kvcat = jnp.concatenate([kvp, kvc_ref[rows, :]], axis=0).astype(BF16)
                lse_tile = jnp.zeros((CHUNK, LANE), F32)
                for g in range(2):
                    s = jnp.where(mask, _dot_nt(q_ref.at[g][rows, :].astype(BF16), kvcat), NEG)
                    m = jnp.max(s, axis=1, keepdims=True)
                    p = jnp.exp(s - m)
                    l = jnp.sum(p, axis=1, keepdims=True)
                    o_ref.at[g][rows, :] = _dot(p.astype(BF16), kvcat) * (1.0 / l)
                    lse_tile = jnp.where(lane == g, m + jnp.log(l), lse_tile)
                lse_ref[rows, :] = lse_tile
                return carry
            _for_residues(dil, unit)

    return pl.pallas_call(
        kern, name=f"attn_fwd_d{dil}", grid=(N_KV, t // ATT_SPAN), in_specs=[cur(2), prev(None), cur(None)],
        out_specs=[cur(2), cur(None)],
        out_shape=[jax.ShapeDtypeStruct((N_QH, t, LANE), F32), jax.ShapeDtypeStruct((N_KV, t, LANE), F32)],
        compiler_params=_cparams(("parallel", "parallel")))(q, kv, kv)


def _attn_merge2(os_, lses, tm=256):
    t = os_[0].shape[1]

    def kern(o1, o2, o3, l1, l2, l3, out_ref, lse_ref):
        pieces = []
        for kh in range(N_KV):
            a, b, c = l1[kh], l2[kh], l3[kh]
            m = jnp.maximum(jnp.maximum(a, b), c)
            tot = m + jnp.log(jnp.exp(a - m) + jnp.exp(b - m) + jnp.exp(c - m))
            lse_ref[kh] = tot
            wa, wb, wc = jnp.exp(a - tot), jnp.exp(b - tot), jnp.exp(c - tot)
            for g in range(2):
                h = 2 * kh + g
                acc = wa[:, g:g + 1] * o1[h] + wb[:, g:g + 1] * o2[h] + wc[:, g:g + 1] * o3[h]
                pieces.append(acc[:, HEAD_DIM:])
        out_ref[...] = jnp.concatenate(pieces, axis=1).astype(BF16)

    blk = lambda n: pl.BlockSpec((n, tm, LANE), lambda i: (0, i, 0))
    return pl.pallas_call(
        kern, name="attn_merge", grid=(t // tm,), in_specs=[blk(N_QH)] * 3 + [blk(N_KV)] * 3,
        out_specs=[pl.BlockSpec((tm, ATTN_DIM), lambda i: (i, 0)), blk(N_KV)],
        out_shape=[jax.ShapeDtypeStruct((t, ATTN_DIM), BF16), jax.ShapeDtypeStruct((N_KV, t, LANE), F32)],
        compiler_params=_cparams(("parallel",)))(*os_, *lses)


def _attn_bwd_prep2(dmix, attn_out, tm=256):
    t = attn_out.shape[0]

    def kern(do_ref, o_ref, dot_ref, d_ref):
        do = do_ref[...]
        prod = do * o_ref[...].astype(F32)
        lo = _lane_lo(tm)
        lane = lax.broadcasted_iota(jnp.int32, (tm, LANE), 1)
        for kh in range(N_KV):
            blk, pb = do[:, kh * LANE:(kh + 1) * LANE], prod[:, kh * LANE:(kh + 1) * LANE]
            dot_ref[2 * kh] = jnp.where(lo, 0.0, _swap_halves_lanes(blk))
            dot_ref[2 * kh + 1] = jnp.where(lo, 0.0, blk)
            s_lo = jnp.sum(jnp.where(lo, pb, 0.0), axis=1, keepdims=True)
            s_hi = jnp.sum(pb, axis=1, keepdims=True) - s_lo
            d_ref[kh] = jnp.where(lane == 0, s_lo, jnp.where(lane == 1, s_hi, 0.0))

    blk = lambda n: pl.BlockSpec((n, tm, LANE), lambda i: (0, i, 0))
    return pl.pallas_call(
        kern, name="attn_bwd_prep", grid=(t // tm,),
        in_specs=[pl.BlockSpec((tm, ATTN_DIM), lambda i: (i, SSM_INNER // ATTN_DIM)), pl.BlockSpec((tm, ATTN_DIM), lambda i: (i, 0))],
        out_specs=[blk(N_QH), blk(N_KV)],
        out_shape=[jax.ShapeDtypeStruct((N_QH, t, LANE), F32), jax.ShapeDtypeStruct((N_KV, t, LANE), F32)],
        compiler_params=_cparams(("parallel",)))(dmix, attn_out)


def _attn_dq2(q, kv, dot, lse, dsum, dil):
    t = q.shape[1]
    sub, nb, cur, prev, _ = _att_specs(t, dil)

    def kern(q_ref, kvp_ref, kvc_ref, do_ref, lse_ref, d_ref, dq_ref):
        n = pl.program_id(1)
        for b in range(nb):
            mask = _band_qk((n == 0) if b == 0 else None)

            def unit(r, carry, b=b, mask=mask):
                rows = _att_rows(b, r, dil)
                kvp = kvc_ref[_att_rows(b - 1, r, dil), :] if b > 0 else kvp_ref[_att_rows(0, r, dil), :]
                kvcat = jnp.concatenate([kvp, kvc_ref[rows, :]], axis=0).astype(BF16)
                lse_t, d_t = lse_ref[rows, :], d_ref[rows, :]
                for g in range(2):
                    s = jnp.where(mask, _dot_nt(q_ref.at[g][rows, :].astype(BF16), kvcat), NEG)
                    p = jnp.exp(s - lse_t[:, g:g + 1])
                    dp = _dot_nt(do_ref.at[g][rows, :].astype(BF16), kvcat)
                    ds = p * (dp - d_t[:, g:g + 1])
                    dq_ref.at[g][rows, :] = _dot(ds.astype(BF16), kvcat)
                return carry
            _for_residues(dil, unit)

    return pl.pallas_call(
        kern, name=f"attn_dq_d{dil}", grid=(N_KV, t // ATT_SPAN),
        in_specs=[cur(2), prev(None), cur(None), cur(2), cur(None), cur(None)], out_specs=cur(2),
        out_shape=jax.ShapeDtypeStruct((N_QH, t, LANE), F32),
        compiler_params=_cparams(("parallel", "parallel")))(q, kv, kv, dot, lse, dsum)


def _attn_dkv2(q, kv, dot, lse, dsum, dil):
    t = q.shape[1]
    sub, nb, cur, _, nxt = _att_specs(t, dil)
    nsteps = t // ATT_SPAN

    def kern(kv_ref, qc_ref, qn_ref, doc_ref, don_ref, lc_ref, ln_ref, dc_ref, dn_ref, dkv_ref):
        n = pl.program_id(1)
        for b in range(nb):
            inside = b < nb - 1
            mask = _band_kq(None if inside else (n == nsteps - 1))

            def unit(r, carry, b=b, inside=inside, mask=mask):
                rows = _att_rows(b, r, dil)
                nrows = _att_rows(b + 1, r, dil) if inside else _att_rows(0, r, dil)
                kvb = kv_ref[rows, :].astype(BF16)
                follow = lambda cref, nref: (cref if inside else nref)[nrows, :]
                lse_t = jnp.concatenate([lc_ref[rows, :].T, follow(lc_ref, ln_ref).T], axis=1)
                d_t = jnp.concatenate([dc_ref[rows, :].T, follow(dc_ref, dn_ref).T], axis=1)
                acc = jnp.zeros((CHUNK, LANE), F32)
                for g in range(2):
                    qdo = jnp.concatenate([qc_ref.at[g][rows, :], follow(qc_ref.at[g], qn_ref.at[g]),
                                           doc_ref.at[g][rows, :], follow(doc_ref.at[g], don_ref.at[g])], axis=0).astype(BF16)
                    both = _dot_nt(kvb, qdo)
                    pt = jnp.exp(jnp.where(mask, both[:, :2 * CHUNK], NEG) - lse_t[g:g + 1, :])
                    dst = pt * (both[:, 2 * CHUNK:] - d_t[g:g + 1, :])
                    acc = acc + _dot(jnp.concatenate([dst, pt], axis=1).astype(BF16), qdo)
                dkv_ref[rows, :] = acc
                return carry
            _for_residues(dil, unit)

    return pl.pallas_call(
        kern, name=f"attn_dkv_d{dil}", grid=(N_KV, nsteps),
        in_specs=[cur(None), cur(2), nxt(2), cur(2), nxt(2), cur(None), nxt(None), cur(None), nxt(None)], out_specs=cur(None),
        out_shape=jax.ShapeDtypeStruct((N_KV, t, LANE), F32),
        compiler_params=_cparams(("parallel", "parallel")))(kv, q, q, dot, dot, lse, lse, dsum, dsum)


HALO = 8
SSM_CONV_TM, SSM_CONV_W = 512, 512
FFN_CONV_TM, FFN_CONV_W = 256, 1408


def _halo_specs(tm, width, t_rows, col_off=0, lead=None):
    per, last = tm // HALO, t_rows // HALO - 1
    row_maps = (lambda i: i, lambda i: jnp.maximum(i * per - 1, 0), lambda i: jnp.minimum((i + 1) * per, last))
    specs = []
    for rows, rm in zip((tm, HALO, HALO), row_maps):
        if lead is None:
            specs.append(pl.BlockSpec((rows, width), functools.partial(lambda c, i, rm: (rm(i), c + col_off), rm=rm)))
        else:
            specs.append(pl.BlockSpec((None, rows, width), functools.partial(lambda c, i, rm: (lead, rm(i), c + col_off), rm=rm)))
    return specs


def _fill_ext(buf, tile_ref, before_ref, after_ref, i, nt):
    tm = tile_ref.shape[0]
    buf[0:HALO, :] = jnp.where(i > 0, before_ref[...].astype(F32), 0.0)
    buf[HALO:HALO + tm, :] = tile_ref[...].astype(F32)
    if after_ref is not None:
        buf[HALO + tm:, :] = jnp.where(i < nt - 1, after_ref[...].astype(F32), 0.0)


CONV_RB, CONV_CW = 16, 256


def _lane_chunks(width):
    return [slice(c0, min(c0 + CONV_CW, width)) for c0 in range(0, width, CONV_CW)]


def _shifted(buf, taps, r0, rows, cs):
    return [buf[pl.ds(HALO - (taps - 1) + k + r0, rows), cs] for k in range(taps)]


def _taps_fwd(xs, w, b):
    acc = b
    for k, xk in enumerate(xs):
        acc = acc + w[k:k + 1, :] * xk
    return acc


def _taps_bwd(bufd, w, taps, r0, rows, cs):
    acc = None
    for k in range(taps):
        term = w[k:k + 1, :] * bufd[pl.ds(r0 + (taps - 1) - k, rows), cs]
        acc = term if acc is None else acc + term
    return acc


def _fold8(z):
    return z[:HALO] + z[HALO:] if z.shape[0] == 2 * HALO else z


def _silu_grad(pre):
    sg = _sigmoid(pre)
    return sg * (1.0 + pre * (1.0 - sg))


def _ssm_conv_fwd(proj, w, b):
    t = proj.shape[0]
    tm, wd = min(SSM_CONV_TM, t), SSM_CONV_W
    nt, taps = t // tm, w.shape[0]

    def kern(x_ref, xb_ref, w_ref, b_ref, o_ref, buf):
        _fill_ext(buf, x_ref, xb_ref, None, pl.program_id(1), nt)
        for cs in _lane_chunks(wd):
            wv, bv = w_ref[:, cs], b_ref[:, cs]
            for r0 in range(0, tm, CONV_RB):
                pre = _taps_fwd(_shifted(buf, taps, r0, CONV_RB, cs), wv, bv)
                o_ref[r0:r0 + CONV_RB, cs] = pre * _sigmoid(pre)

    tile, before, _ = _halo_specs(tm, wd, t)
    par = lambda rows: pl.BlockSpec((rows, wd), lambda c, i: (0, c))
    return pl.pallas_call(
        kern, name="ssm_conv_fwd", grid=(XBC_DIM // wd, nt), in_specs=[tile, before, par(taps), par(1)],
        out_specs=pl.BlockSpec((tm, wd), lambda c, i: (i, c)), out_shape=jax.ShapeDtypeStruct((t, XBC_DIM), F32),
        scratch_shapes=[pltpu.VMEM((tm + HALO, wd), F32)],
        compiler_params=_cparams(("parallel", "parallel")))(proj, proj, w, b)


def _ssm_conv_bwd(proj, w, b, dact):
    t = proj.shape[0]
    tm, wd = min(SSM_CONV_TM, t), SSM_CONV_W
    nt, taps = t // tm, w.shape[0]

    def kern(x_ref, xb_ref, xa_ref, d_ref, dn_ref, w_ref, b_ref, dx_ref, gw_ref, gb_ref, buf, bufd):
        i = pl.program_id(1)
        _fill_ext(buf, x_ref, xb_ref, xa_ref, i, nt)

        @pl.when(i == 0)
        def _():
            gw_ref[...] = jnp.zeros(gw_ref.shape, F32)
            gb_ref[...] = jnp.zeros(gb_ref.shape, F32)
        for cs in _lane_chunks(wd):
            wv, bv = w_ref[:, cs], b_ref[:, cs]
            acc = [jnp.zeros((HALO, cs.stop - cs.start), F32) for _ in range(taps + 1)]
            for r0 in list(range(0, tm, CONV_RB)) + [tm]:
                inside = r0 < tm
                rows = CONV_RB if inside else HALO
                xs = _shifted(buf, taps, r0, rows, cs)
                d = d_ref[r0:r0 + rows, cs] if inside else jnp.where(i < nt - 1, dn_ref[:, cs], 0.0)
                dpre = d * _silu_grad(_taps_fwd(xs, wv, bv))
                bufd[r0:r0 + rows, cs] = dpre
                if inside:
                    acc[taps] = acc[taps] + _fold8(dpre)
                    for k in range(taps):
                        acc[k] = acc[k] + _fold8(dpre * xs[k])
            gb_ref[:, cs] += jnp.sum(acc[taps], axis=0, keepdims=True)
            for k in range(taps):
                gw_ref[k:k + 1, cs] += jnp.sum(acc[k], axis=0, keepdims=True)
            for r0 in range(0, tm, CONV_RB):
                dx_ref[r0:r0 + CONV_RB, cs] = _taps_bwd(bufd, wv, taps, r0, CONV_RB, cs).astype(BF16)

    xt, xb, xa = _halo_specs(tm, wd, t)
    dt_, _, dn = _halo_specs(tm, wd, t)
    par = lambda rows: pl.BlockSpec((rows, wd), lambda c, i: (0, c))
    return pl.pallas_call(
        kern, name="ssm_conv_bwd", grid=(XBC_DIM // wd, nt), in_specs=[xt, xb, xa, dt_, dn, par(taps), par(1)],
        out_specs=[pl.BlockSpec((tm, wd), lambda c, i: (i, c)), par(taps), par(1)],
        out_shape=[jax.ShapeDtypeStruct((t, XBC_DIM), BF16), jax.ShapeDtypeStruct((taps, XBC_DIM), F32),
                   jax.ShapeDtypeStruct((1, XBC_DIM), F32)],
        scratch_shapes=[pltpu.VMEM((tm + 2 * HALO, wd), F32), pltpu.VMEM((tm + HALO, wd), F32)],
        compiler_params=_cparams(("parallel", "arbitrary")))(proj, proj, proj, dact, dact, w, b)


def _ffn_act_fwd(u, w, b):
    t = u.shape[1]
    tm, wd = min(FFN_CONV_TM, t), FFN_CONV_W
    nt, taps, nc = t // tm, w.shape[0], D_FF // FFN_CONV_W

    def kern(g_ref, gb_ref, v_ref, vb_ref, wg_ref, wv_ref, bg_ref, bv_ref, a_ref, bufg, bufv):
        i = pl.program_id(1)
        _fill_ext(bufg, g_ref, gb_ref, None, i, nt)
        _fill_ext(bufv, v_ref, vb_ref, None, i, nt)
        for cs in _lane_chunks(wd):
            wg, wv, bg, bv = wg_ref[:, cs], wv_ref[:, cs], bg_ref[:, cs], bv_ref[:, cs]
            for r0 in range(0, tm, CONV_RB):
                g = _taps_fwd(_shifted(bufg, taps, r0, CONV_RB, cs), wg, bg)
                v = _taps_fwd(_shifted(bufv, taps, r0, CONV_RB, cs), wv, bv)
                a_ref[r0:r0 + CONV_RB, cs] = (g * _sigmoid(g) * v).astype(BF16)

    gt, gbf, _ = _halo_specs(tm, wd, t, lead=0)
    vt, vbf, _ = _halo_specs(tm, wd, t, lead=1)
    par = lambda rows, off: pl.BlockSpec((rows, wd), functools.partial(lambda c, i, off: (0, c + off), off=off))
    return pl.pallas_call(
        kern, name="ffn_act_fwd", grid=(nc, nt),
        in_specs=[gt, gbf, vt, vbf, par(taps, 0), par(taps, nc), par(1, 0), par(1, nc)],
        out_specs=pl.BlockSpec((tm, wd), lambda c, i: (i, c)), out_shape=jax.ShapeDtypeStruct((t, D_FF), BF16),
        scratch_shapes=[pltpu.VMEM((tm + HALO, wd), F32)] * 2,
        compiler_params=_cparams(("parallel", "parallel")))(u, u, u, u, w, w, b, b)


def _ffn_act_bwd(u, w, b, da):
    t = u.shape[1]
    tm, wd = min(FFN_CONV_TM, t), FFN_CONV_W
    nt, taps, nc = t // tm, w.shape[0], D_FF // FFN_CONV_W

    def kern(g_ref, gb_ref, ga_ref, v_ref, vb_ref, va_ref, d_ref, dn_ref, wg_ref, wv_ref, bg_ref, bv_ref,
             du_ref, gwg_ref, gwv_ref, gbg_ref, gbv_ref, bufg, bufv, bufdg, bufdv):
        i = pl.program_id(1)
        _fill_ext(bufg, g_ref, gb_ref, ga_ref, i, nt)
        _fill_ext(bufv, v_ref, vb_ref, va_ref, i, nt)

        @pl.when(i == 0)
        def _():
            for r in (gwg_ref, gwv_ref, gbg_ref, gbv_ref):
                r[...] = jnp.zeros(r.shape, F32)
        for cs in _lane_chunks(wd):
            wg, wv, bg, bv = wg_ref[:, cs], wv_ref[:, cs], bg_ref[:, cs], bv_ref[:, cs]
            zero = jnp.zeros((HALO, cs.stop - cs.start), F32)
            accg, accv = [zero] * (taps + 1), [zero] * (taps + 1)
            for r0 in list(range(0, tm, CONV_RB)) + [tm]:
                inside = r0 < tm
                rows = CONV_RB if inside else HALO
                xg, xv = _shifted(bufg, taps, r0, rows, cs), _shifted(bufv, taps, r0, rows, cs)
                g, v = _taps_fwd(xg, wg, bg), _taps_fwd(xv, wv, bv)
                dav = d_ref[r0:r0 + rows, cs] if inside else jnp.where(i < nt - 1, dn_ref[:, cs], 0.0)
                sg = _sigmoid(g)
                dg = dav * v * (sg * (1.0 + g * (1.0 - sg)))
                dv = dav * (g * sg)
                bufdg[r0:r0 + rows, cs] = dg
                bufdv[r0:r0 + rows, cs] = dv
                if inside:
                    accg[taps], accv[taps] = accg[taps] + _fold8(dg), accv[taps] + _fold8(dv)
                    for k in range(taps):
                        accg[k], accv[k] = accg[k] + _fold8(dg * xg[k]), accv[k] + _fold8(dv * xv[k])
            gbg_ref[:, cs] += jnp.sum(accg[taps], axis=0, keepdims=True)
            gbv_ref[:, cs] += jnp.sum(accv[taps], axis=0, keepdims=True)
            for k in range(taps):
                gwg_ref[k:k + 1, cs] += jnp.sum(accg[k], axis=0, keepdims=True)
                gwv_ref[k:k + 1, cs] += jnp.sum(accv[k], axis=0, keepdims=True)
            for r0 in range(0, tm, CONV_RB):
                du_ref[0, r0:r0 + CONV_RB, cs] = _taps_bwd(bufdg, wg, taps, r0, CONV_RB, cs).astype(BF16)
                du_ref[1, r0:r0 + CONV_RB, cs] = _taps_bwd(bufdv, wv, taps, r0, CONV_RB, cs).astype(BF16)

    gt, gbf, gaf = _halo_specs(tm, wd, t, lead=0)
    vt, vbf, vaf = _halo_specs(tm, wd, t, lead=1)
    dt_, _, dn = _halo_specs(tm, wd, t)
    par = lambda rows, off: pl.BlockSpec((rows, wd), functools.partial(lambda c, i, off: (0, c + off), off=off))
    return pl.pallas_call(
        kern, name="ffn_act_bwd", grid=(nc, nt),
        in_specs=[gt, gbf, gaf, vt, vbf, vaf, dt_, dn, par(taps, 0), par(taps, nc), par(1, 0), par(1, nc)],
        out_specs=[pl.BlockSpec((2, tm, wd), lambda c, i: (0, i, c)), par(taps, 0), par(taps, 0), par(1, 0), par(1, 0)],
        out_shape=[jax.ShapeDtypeStruct((2, t, D_FF), BF16)] + [jax.ShapeDtypeStruct((taps, D_FF), F32)] * 2
        + [jax.ShapeDtypeStruct((1, D_FF), F32)] * 2,
        scratch_shapes=[pltpu.VMEM((tm + 2 * HALO, wd), F32)] * 2 + [pltpu.VMEM((tm + HALO, wd), F32)] * 2,
        compiler_params=_cparams(("parallel", "arbitrary")))(u, u, u, u, u, u, da, da, w, w, b, b)


def _softplus(x):
    e = jnp.exp(-jnp.abs(x))
    return jnp.maximum(x, 0.0) + jnp.where(e < 1e-4, e - 0.5 * e * e, jnp.log(1.0 + e))


def _tri(lower):
    r = lax.broadcasted_iota(jnp.int32, (CHUNK, CHUNK), 0)
    c = lax.broadcasted_iota(jnp.int32, (CHUNK, CHUNK), 1)
    return (r >= c) if lower else (r <= c)


def _cum(mat_bool, x):
    return jnp.dot(mat_bool.astype(F32), x, precision=lax.Precision.HIGHEST, preferred_element_type=F32)


def _pair_sel(lane_lo, tile, h0):
    return jnp.where(lane_lo, tile[:, h0:h0 + 1], tile[:, h0 + 1:h0 + 2])


def _ssd_fwd(xbc_act, proj, dt_bias_p, a_log_p, dskip_t):
    t = xbc_act.shape[0]
    nch = t // CHUNK

    def kern(xa_ref, dtr_ref, bias_ref, alog_ref, dsk_ref, y_ref, dt_ref, hs_ref, hst):
        @pl.when(pl.program_id(0) == 0)
        def _():
            hst[...] = jnp.zeros(hst.shape, F32)
        dt = _softplus(dtr_ref[...] + bias_ref[...])
        dt_ref[...] = dt
        acum = _cum(_tri(True), dt * (-jnp.exp(alog_ref[...])))
        acum_t = acum.T
        ea = jnp.exp(acum)
        a_last = acum[CHUNK - 1:CHUNK, :]
        dend = jnp.exp(a_last - acum)
        ea_last = jnp.exp(a_last)
        causal = _tri(True)
        lane_lo = lax.broadcasted_iota(jnp.int32, (CHUNK, LANE), 1) < HEAD_DIM
        row_lo = lax.broadcasted_iota(jnp.int32, (CHUNK, LANE), 0) < HEAD_DIM
        for g in range(2):
            bg = xa_ref[:, SSM_INNER + g * SSM_STATE:SSM_INNER + (g + 1) * SSM_STATE].astype(BF16)
            cg = xa_ref[:, SSM_INNER + BC_DIM + g * SSM_STATE:SSM_INNER + BC_DIM + (g + 1) * SSM_STATE].astype(BF16)
            cb = _dot_nt(cg, bg)
            for j in range(4 * g, 4 * g + 4):
                h0 = 2 * j
                cols = slice(j * LANE, (j + 1) * LANE)
                xp = xa_ref[:, cols]
                xdt = xp * _pair_sel(lane_lo, dt, h0)
                ydiag = None
                for hh, sel in ((h0, lane_lo), (h0 + 1, ~lane_lo)):
                    seg = acum[:, hh:hh + 1] - acum_t[hh:hh + 1, :]
                    mm = (cb * jnp.where(causal, jnp.exp(jnp.minimum(seg, 0.0)), 0.0)).astype(BF16)
                    d = _dot(mm, jnp.where(sel, xdt, 0.0).astype(BF16))
                    ydiag = d if ydiag is None else ydiag + d
                hp = hst[cols, :]
                hs_ref[cols, :] = hp
                yoff = _dot_nt(cg, hp.astype(BF16)) * _pair_sel(lane_lo, ea, h0)
                y_ref[:, cols] = ydiag + yoff + dsk_ref[:, cols] * xp
                xw = (xdt * _pair_sel(lane_lo, dend, h0)).astype(BF16)
                rowf = jnp.where(row_lo, ea_last[:, h0:h0 + 1], ea_last[:, h0 + 1:h0 + 2])
                hst[cols, :] = hp * rowf + _dot_tn(xw, bg)

    return pl.pallas_call(
        kern, name="ssd_fwd", grid=(nch,),
        in_specs=[pl.BlockSpec((CHUNK, XBC_DIM), lambda c: (c, 0)), pl.BlockSpec((CHUNK, LANE), lambda c: (c, OFF_DT // LANE)),
                  pl.BlockSpec((1, LANE), lambda c: (0, 0)), pl.BlockSpec((1, LANE), lambda c: (0, 0)),
                  pl.BlockSpec((1, SSM_INNER), lambda c: (0, 0))],
        out_specs=[pl.BlockSpec((CHUNK, SSM_INNER), lambda c: (c, 0)), pl.BlockSpec((CHUNK, LANE), lambda c: (c, 0)),
                   pl.BlockSpec((None, SSM_INNER, SSM_STATE), lambda c: (c, 0, 0))],
        out_shape=[jax.ShapeDtypeStruct((t, SSM_INNER), F32), jax.ShapeDtypeStruct((t, LANE), F32),
                   jax.ShapeDtypeStruct((nch, SSM_INNER, SSM_STATE), F32)],
        scratch_shapes=[pltpu.VMEM((SSM_INNER, SSM_STATE), F32)],
        compiler_params=_cparams(("arbitrary",)))(xbc_act, proj, dt_bias_p, a_log_p, dskip_t)


def _ssd_bwd(xbc_act, proj, dt_sp, hstates, dy, dt_bias_p, a_log_p, dskip_t):
    t = xbc_act.shape[0]
    nch = t // CHUNK

    def kern(xa_ref, dtr_ref, dt_ref, hs_ref, dy_ref, bias_ref, alog_ref, dsk_ref,
             dact_ref, ddtr_ref, da_ref, dbias_ref, ddsk_ref, dh):
        @pl.when(pl.program_id(0) == 0)
        def _():
            dh[...] = jnp.zeros(dh.shape, F32)
            for r in (da_ref, dbias_ref, ddsk_ref):
                r[...] = jnp.zeros(r.shape, F32)
        dt = dt_ref[...]
        a_neg = -jnp.exp(alog_ref[...])
        acum = _cum(_tri(True), dt * a_neg)
        acum_t = acum.T
        ea = jnp.exp(acum)
        a_last = acum[CHUNK - 1:CHUNK, :]
        dend = jnp.exp(a_last - acum)
        ea_last = jnp.exp(a_last)
        causal = _tri(True)
        lane = lax.broadcasted_iota(jnp.int32, (CHUNK, LANE), 1)
        rowi = lax.broadcasted_iota(jnp.int32, (CHUNK, LANE), 0)
        lane_lo, row_lo, last_row = lane < HEAD_DIM, rowi < HEAD_DIM, rowi == CHUNK - 1
        d_dt = jnp.zeros((CHUNK, LANE), F32)
        d_acum = jnp.zeros((CHUNK, LANE), F32)
        d_acum_t = jnp.zeros((CHUNK, LANE), F32)

        def half_sums(v):
            lo = jnp.sum(jnp.where(lane_lo, v, 0.0), axis=1, keepdims=True)
            return lo, jnp.sum(v, axis=1, keepdims=True) - lo

        for g in range(2):
            bcols = slice(SSM_INNER + g * SSM_STATE, SSM_INNER + (g + 1) * SSM_STATE)
            ccols = slice(SSM_INNER + BC_DIM + g * SSM_STATE, SSM_INNER + BC_DIM + (g + 1) * SSM_STATE)
            bg, cg = xa_ref[:, bcols].astype(BF16), xa_ref[:, ccols].astype(BF16)
            cb = _dot_nt(cg, bg)
            dg_sum = jnp.zeros((CHUNK, CHUNK), F32)
            dcg = jnp.zeros((CHUNK, SSM_STATE), F32)
            dbg = jnp.zeros((CHUNK, SSM_STATE), F32)
            for j in range(4 * g, 4 * g + 4):
                h0 = 2 * j
                cols = slice(j * LANE, (j + 1) * LANE)
                xp, dyp = xa_ref[:, cols], dy_ref[:, cols]
                dtsel = _pair_sel(lane_lo, dt, h0)
                xdt = xp * dtsel
                xdt_b = xdt.astype(BF16)
                hp, dhp = hs_ref[cols, :], dh[cols, :]
                hp_b, dhp_b = hp.astype(BF16), dhp.astype(BF16)
                easel, dendsel = _pair_sel(lane_lo, ea, h0), _pair_sel(lane_lo, dend, h0)
                dx = None
                for hh, sel in ((h0, lane_lo), (h0 + 1, ~lane_lo)):
                    dyh = jnp.where(sel, dyp, 0.0).astype(BF16)
                    seg = acum[:, hh:hh + 1] - acum_t[hh:hh + 1, :]
                    dec = jnp.where(causal, jnp.exp(jnp.minimum(seg, 0.0)), 0.0)
                    mm = cb * dec
                    e = _dot_nt(dyh, xdt_b)
                    d = _dot_tn(mm.astype(BF16), dyh)
                    dx = d if dx is None else dx + d
                    dg_sum = dg_sum + dec * e
                    qm = mm * e
                    d_acum = d_acum + jnp.where(lane == hh, jnp.sum(qm, axis=1, keepdims=True), 0.0)
                    d_acum_t = d_acum_t - jnp.where(rowi == hh, jnp.sum(qm, axis=0, keepdims=True), 0.0)
                g2 = _dot_nt(bg, dhp_b)
                dx = dx + g2 * dendsel
                yoff = _dot_nt(cg, hp_b) * easel
                t_lo, t_hi = half_sums(xdt * g2 * dendsel)
                y_lo, y_hi = half_sums(dyp * yoff)
                hh_prod = dhp * hp
                s_lo = jnp.sum(jnp.where(row_lo, hh_prod, 0.0), keepdims=True).reshape(1, 1)
                s_hi = jnp.sum(hh_prod, keepdims=True).reshape(1, 1) - s_lo
                end_lo = ea_last[:, h0:h0 + 1] * s_lo + jnp.sum(t_lo, axis=0, keepdims=True)
                end_hi = ea_last[:, h0 + 1:h0 + 2] * s_hi + jnp.sum(t_hi, axis=0, keepdims=True)
                d_acum = d_acum + jnp.where(lane == h0, y_lo - t_lo + jnp.where(last_row, end_lo, 0.0), 0.0)
                d_acum = d_acum + jnp.where(lane == h0 + 1, y_hi - t_hi + jnp.where(last_row, end_hi, 0.0), 0.0)
                dye = (dyp * easel).astype(BF16)
                dcg = dcg + _dot(dye, hp_b)
                dbg = dbg + _dot((xdt * dendsel).astype(BF16), dhp_b)
                rowf = jnp.where(row_lo, ea_last[:, h0:h0 + 1], ea_last[:, h0 + 1:h0 + 2])
                dh[cols, :] = dhp * rowf + _dot_tn(dye, cg)
                dact_ref[:, cols] = dx * dtsel + dsk_ref[:, cols] * dyp
                x_lo, x_hi = half_sums(dx * xp)
                d_dt = d_dt + jnp.where(lane == h0, x_lo, 0.0) + jnp.where(lane == h0 + 1, x_hi, 0.0)
                ddsk_ref[:, cols] += jnp.sum(dyp * xp, axis=0, keepdims=True)
            dg_b = dg_sum.astype(BF16)
            dact_ref[:, ccols] = dcg + _dot(dg_b, bg)
            dact_ref[:, bcols] = dbg + _dot_tn(dg_b, cg)
        d_adt = _cum(_tri(False), d_acum + d_acum_t.T)
        d_dt = d_dt + d_adt * a_neg
        da_ref[...] += jnp.sum(d_adt * dt, axis=0, keepdims=True)
        d_raw = jnp.where(lane < SSM_HEADS, d_dt * _sigmoid(dtr_ref[...] + bias_ref[...]), 0.0)
        ddtr_ref[...] = d_raw.astype(BF16)
        dbias_ref[...] += jnp.sum(d_raw, axis=0, keepdims=True)

    rev = lambda c: (nch - 1 - c, 0)
    return pl.pallas_call(
        kern, name="ssd_bwd", grid=(nch,),
        in_specs=[pl.BlockSpec((CHUNK, XBC_DIM), rev), pl.BlockSpec((CHUNK, LANE), lambda c: (nch - 1 - c, OFF_DT // LANE)),
                  pl.BlockSpec((CHUNK, LANE), rev), pl.BlockSpec((None, SSM_INNER, SSM_STATE), lambda c: (nch - 1 - c, 0, 0)),
                  pl.BlockSpec((CHUNK, SSM_INNER), rev),
                  pl.BlockSpec((1, LANE), lambda c: (0, 0)), pl.BlockSpec((1, LANE), lambda c: (0, 0)),
                  pl.BlockSpec((1, SSM_INNER), lambda c: (0, 0))],
        out_specs=[pl.BlockSpec((CHUNK, XBC_DIM), rev), pl.BlockSpec((CHUNK, LANE), rev),
                   pl.BlockSpec((1, LANE), lambda c: (0, 0)), pl.BlockSpec((1, LANE), lambda c: (0, 0)),
                   pl.BlockSpec((1, SSM_INNER), lambda c: (0, 0))],
        out_shape=[jax.ShapeDtypeStruct((t, XBC_DIM), F32), jax.ShapeDtypeStruct((t, LANE), BF16),
                   jax.ShapeDtypeStruct((1, LANE), F32), jax.ShapeDtypeStruct((1, LANE), F32),
                   jax.ShapeDtypeStruct((1, SSM_INNER), F32)],
        scratch_shapes=[pltpu.VMEM((SSM_INNER, SSM_STATE), F32)],
        compiler_params=_cparams(("arbitrary",)))(xbc_act, proj, dt_sp, hstates, dy, dt_bias_p, a_log_p, dskip_t)


def _ssm_post_fwd(y, proj, g):
    def body(y_ref, z_ref, g_ref, o_ref):
        z = z_ref[...]
        yz = y_ref[...] * (z * _sigmoid(z))
        r = lax.rsqrt(jnp.mean(yz * yz, axis=-1, keepdims=True) + EPS)
        o_ref[...] = (yz * r * g_ref[...]).astype(BF16)
    return _rows("ssm_post_fwd", body, [("t", y), ("tc", proj, SSM_INNER, OFF_Z // SSM_INNER), ("p", g)],
                 [(SSM_INNER, BF16)])[0]


def _ssm_post_bwd(dmix, y, proj, g):
    def body(do_ref, y_ref, z_ref, g_ref, dy_ref, dz_ref, dg_ref):
        z, yv, dout = z_ref[...], y_ref[...], do_ref[...]
        sg = _sigmoid(z)
        gz = z * sg
        yz = yv * gz
        r = lax.rsqrt(jnp.mean(yz * yz, axis=-1, keepdims=True) + EPS)
        gd = dout * g_ref[...]
        dyz = r * gd - yz * (r * r * r * jnp.mean(yz * gd, axis=-1, keepdims=True))
        dy_ref[...] = dyz * gz
        dz_ref[...] = (dyz * yv * (sg * (1.0 + z * (1.0 - sg)))).astype(BF16)
        dg_ref[...] += jnp.sum(dout * yz * r, axis=0, keepdims=True)
    return _rows("ssm_post_bwd", body,
                 [("tc", dmix, SSM_INNER, 0), ("t", y), ("tc", proj, SSM_INNER, OFF_Z // SSM_INNER), ("p", g)],
                 [(SSM_INNER, F32), (SSM_INNER, BF16)], accs=[(1, SSM_INNER)])


def _ple_loss(gl, pp, x2, tgt):
    d = x2.shape[1]

    def body(gl_ref, pp_ref, x_ref, t_ref, dy_ref, dgl_ref, dpp_ref, sq_ref):
        s = _sigmoid(gl_ref[...])
        ppv = pp_ref[...]
        diff = x_ref[...] + s * ppv - t_ref[...]
        dy = diff * (1.0 / d)
        dy_ref[...] = dy
        dgl_ref[...] = (dy * ppv * s * (1.0 - s)).astype(BF16)
        dpp_ref[...] = (dy * s).astype(BF16)
        sq_ref[...] += jnp.sum(diff * diff, axis=0, keepdims=True)
    return _rows("ple_loss", body, [("t", gl), ("t", pp), ("t", x2), ("t", tgt)], [(d, F32), (d, BF16), (d, BF16)],
                 accs=[(1, d)])


def _pad_lanes(v, width=LANE):
    return jnp.pad(v, ((0, 0), (0, width - v.shape[1])))


def _local_step(x, p, tgt, wts):
    g_attn, g_ssm, g_ffn, g_ple = wts["attn_norm_g"], wts["ssm_norm_g"], wts["ffn_norm_g"], wts["ple_norm_g"]
    w_in_p, w_out_s, w_out_a = wts["w_in_p"], wts["w_out_ssm"], wts["w_out_attn"]
    w_up, w_down, w_gate, w_proj = wts["w_up"], wts["w_down"], wts["w_ple_gate"], wts["w_ple_proj"]
    gq_t = jnp.tile(wts["q_norm_g"], (1, ATTN_DIM // HEAD_DIM))
    gk_t = jnp.tile(wts["k_norm_g"], (1, KV_DIM // HEAD_DIM))
    dt_bias_p, a_log_p = _pad_lanes(wts["dt_bias"]), _pad_lanes(wts["a_log"])
    dskip_t = jnp.repeat(wts["d_skip"], HEAD_DIM, axis=1)

    h1 = _rms_fwd("rms_attn", x, g_attn)
    proj = _mm_nn("in_proj", [(h1, w_in_p)], F32)
    q_hm, kv_hm = _qknorm_fwd2(proj, gq_t, gk_t)
    pats = [_attn_fwd2(q_hm, kv_hm, d) for d in DILATIONS]
    attn_out, lse = _attn_merge2([o for o, _ in pats], [l for _, l in pats])
    xbc_act = _ssm_conv_fwd(proj, wts["ssm_conv_w"], wts["ssm_conv_b"])
    y_ssd, dt_sp, hstates = _ssd_fwd(xbc_act, proj, dt_bias_p, a_log_p, dskip_t)
    ssm_out = _ssm_post_fwd(y_ssd, proj, g_ssm)
    x1 = _mm_nn("out_proj", [(ssm_out, w_out_s), (attn_out, w_out_a)], F32, res=x, tm=1024)
    h2 = _rms_fwd("rms_ffn", x1, g_ffn)
    u = _mm_nn("ffn_up", [(h2, w_up)], F32, tm=1024, tn=1408, halves=True)
    a = _ffn_act_fwd(u, wts["ffn_conv_w"], wts["ffn_conv_b"])
    x2 = _mm_nn("ffn_down", [(a, w_down)], F32, res=x1, tm=1024, tn=512)
    h3 = _rms_fwd("rms_ple", x2, g_ple)
    gl = _mm_nn("ple_gate", [(h3, w_gate)], F32, tm=2048)
    pb = p.astype(BF16)
    pp = _mm_nn("ple_proj", [(pb, w_proj)], F32, tm=2048)
    dy, dgl, dpp, sq = _ple_loss(gl, pp, x2, tgt)

    grads = {}
    grads["w_ple_proj"] = _mm_tn("g_ple_proj", pb, dpp, tn=PLE_DIM, chip_cols=True)
    grads["w_ple_gate"] = _mm_tn("g_ple_gate", h3, dgl)
    dh3 = _mm_nt("d_h3", [(dgl, w_gate, 0)], F32, tm=2048)
    dx2, dx2b, grads["ple_norm_g"] = _rms_bwd("rms_ple_bwd", dh3, x2, g_ple, dy)
    da = _mm_nt("d_ffn_act", [(dx2b, w_down, 0)], F32, tm=1024, tn=1408)
    grads["w_down"] = _mm_tn("g_ffn_down", a, dx2b, tm=1408)
    du, gwg, gwv, gbg, gbv = _ffn_act_bwd(u, wts["ffn_conv_w"], wts["ffn_conv_b"], da)
    grads["ffn_conv_w"] = jnp.concatenate([gwg, gwv], axis=1)
    grads["ffn_conv_b"] = jnp.concatenate([gbg, gbv], axis=1)
    grads["w_up"] = _mm_tn("g_ffn_up", h2, du, tn=1408, chip_cols=True)
    dh2 = _mm_nt("d_h2", [(du, w_up, 0, 0), (du, w_up, 1, 1)], F32, tm=1024, tn=512)
    dx1, dx1b, grads["ffn_norm_g"] = _rms_bwd("rms_ffn_bwd", dh2, x1, g_ffn, dx2)
    dmix = _mm_nt("d_mix", [(dx1b, jnp.concatenate([w_out_s, w_out_a], axis=0), 0)], F32, tm=1024)
    grads["w_out"] = jnp.concatenate([_mm_tn("g_out_attn", attn_out, dx1b), _mm_tn("g_out_ssm", ssm_out, dx1b)], axis=0)
    dy_ssd, dz, grads["ssm_norm_g"] = _ssm_post_bwd(dmix, y_ssd, proj, g_ssm)
    dact, ddtr, d_a, d_bias, d_dsk = _ssd_bwd(xbc_act, proj, dt_sp, hstates, dy_ssd, dt_bias_p, a_log_p, dskip_t)
    grads["dt_bias"] = d_bias[:, :SSM_HEADS]
    grads["a_log"] = d_a[:, :SSM_HEADS] * (-jnp.exp(wts["a_log"]))
    grads["d_skip"] = jnp.sum(d_dsk.reshape(SSM_HEADS, HEAD_DIM), axis=1)[None, :]
    dxbc, grads["ssm_conv_w"], grads["ssm_conv_b"] = _ssm_conv_bwd(proj, wts["ssm_conv_w"], wts["ssm_conv_b"], dact)
    do_hm, dsum = _attn_bwd_prep2(dmix, attn_out)
    dqs = [_attn_dq2(q_hm, kv_hm, do_hm, lse, dsum, d) for d in DILATIONS]
    dkvs = [_attn_dkv2(q_hm, kv_hm, do_hm, lse, dsum, d) for d in DILATIONS]
    dq, dk, dv, dgq, dgk = _qknorm_bwd2(proj, gq_t, gk_t, dqs, dkvs)
    grads["q_norm_g"] = jnp.sum(dgq.reshape(ATTN_DIM // HEAD_DIM, HEAD_DIM), axis=0)[None, :]
    grads["k_norm_g"] = jnp.sum(dgk.reshape(KV_DIM // HEAD_DIM, HEAD_DIM), axis=0)[None, :]
    dproj = jnp.concatenate([dxbc, dq, dz, dk, dv, ddtr], axis=1)
    grads["w_in_p"] = _mm_tn("g_in_proj", h1, dproj, tm=512)
    dh1 = _mm_nt("d_h1", [(dproj, w_in_p, 0)], F32, tm=1024, tn=512)
    grad_x, _, grads["attn_norm_g"] = _rms_bwd("rms_attn_bwd", dh1, x, g_attn, dx1)
    return sq, grad_x, grads


MESH_IDS = pl.DeviceIdType.MESH
N_CHIPS = 4
ANY_SPEC = pl.BlockSpec(memory_space=pl.ANY)
SMALL_ROWS = 96
ALL_SMALL_ROWS = 272


def _place():
    x, y, c = lax.axis_index("x"), lax.axis_index("y"), lax.axis_index("c")
    return x, y, c, [(1 - x, y), (x, 1 - y), (1 - x, 1 - y)]


def _gather_over_chips(arrs):
    n = len(arrs)
    split = [a.shape[0] % 64 == 0 for a in arrs]

    def body(*refs):
        ins, outs = refs[:n], refs[n:2 * n]
        ici_send, ici_recv, d2d_send, d2d_recv = refs[2 * n:2 * n + 4]
        x, y, c, chips = _place()
        mine = 2 * x + y

        def part(ref, a, core):
            if not split[a]:
                return ref
            half = arrs[a].shape[0] // 2
            return ref.at[pl.ds(core * half, half)]

        def ici(a, k, src_chip_slot, core):
            px, py = chips[k]
            return pltpu.make_async_remote_copy(
                src_ref=part(ins[a], a, core), dst_ref=part(outs[a].at[src_chip_slot], a, core), send_sem=ici_send.at[3 * a + k],
                recv_sem=ici_recv.at[3 * a + k], device_id=(px, py, c), device_id_type=MESH_IDS)

        def d2d(a, k, core):
            px, py = chips[k]
            piece = part(outs[a].at[2 * px + py], a, core)
            return pltpu.make_async_remote_copy(src_ref=piece, dst_ref=piece, send_sem=d2d_send.at[3 * a + k],
                                                recv_sem=d2d_recv.at[3 * a + k], device_id=(x, y, 1 - c), device_id_type=MESH_IDS)

        for a in range(n):
            for k in range(3):
                ici(a, k, mine, c).start()
        passed = []
        for a in range(n):
            for k, (px, py) in enumerate(chips):
                ici(a, k, 2 * px + py, c).wait_recv()
                if split[a]:
                    fwd = d2d(a, k, c)
                    fwd.start()
                    passed.append(fwd)
        for a in range(n):
            for k in range(3):
                if split[a]:
                    d2d(a, k, 1 - c).wait_recv()
                ici(a, k, mine, c).wait_send()
        for fwd in passed:
            fwd.wait_send()

    sems = [pltpu.SemaphoreType.DMA((3 * n,))] * 4
    return pl.pallas_call(
        body, name="gather_weights", in_specs=[ANY_SPEC] * n, out_specs=[ANY_SPEC] * n,
        out_shape=[jax.ShapeDtypeStruct((N_CHIPS,) + a.shape, a.dtype) for a in arrs], scratch_shapes=sems)(*arrs)


def _row_tile(rows, cap=256):
    return max(d for d in range(8, cap + 1, 8) if rows % d == 0)


def _swap_halves(gs):
    n = len(gs)

    def body(*refs):
        ins, outs, send, recv = refs[:n], refs[n:2 * n], refs[2 * n], refs[2 * n + 1]
        x, y, c, _ = _place()
        cps = []
        for a in range(n):
            half = gs[a].shape[1] // 2
            for q in range(N_CHIPS):
                cps.append(pltpu.make_async_remote_copy(
                    src_ref=ins[a].at[q, pl.ds((1 - c) * half, half)], dst_ref=outs[a].at[q], send_sem=send.at[N_CHIPS * a + q],
                    recv_sem=recv.at[N_CHIPS * a + q], device_id=(x, y, 1 - c), device_id_type=MESH_IDS))
        for cp in cps:
            cp.start()
        for cp in cps:
            cp.wait()

    return pl.pallas_call(
        body, name="grad_swap_halves", in_specs=[ANY_SPEC] * n, out_specs=[ANY_SPEC] * n,
        out_shape=[jax.ShapeDtypeStruct((N_CHIPS, g.shape[1] // 2, g.shape[2]), g.dtype) for g in gs],
        scratch_shapes=[pltpu.SemaphoreType.DMA((N_CHIPS * n,))] * 2)(*gs)


def _add_halves(name, g, got, c_idx):
    rows, cols = got.shape[1:]
    tm = _row_tile(rows)
    per = rows // tm

    def kern(c_ref, g_ref, r_ref, o_ref):
        o_ref[...] = (g_ref[...] + r_ref[...]).astype(BF16)

    return pl.pallas_call(
        kern, name=name,
        grid_spec=pltpu.PrefetchScalarGridSpec(
            num_scalar_prefetch=1, grid=(N_CHIPS, per),
            in_specs=[pl.BlockSpec((None, tm, cols), lambda q, i, c_ref: (q, c_ref[0] * per + i, 0)),
                      pl.BlockSpec((None, tm, cols), lambda q, i, c_ref: (q, i, 0))],
            out_specs=pl.BlockSpec((None, tm, cols), lambda q, i, c_ref: (q, i, 0))),
        out_shape=jax.ShapeDtypeStruct((N_CHIPS, rows, cols), BF16),
        compiler_params=_cparams(("parallel", "parallel")))(c_idx, g, got)


def _scatter_over_chips(ss):
    n = len(ss)

    def body(*refs):
        ins, outs, send, recv = refs[:n], refs[n:2 * n], refs[2 * n], refs[2 * n + 1]
        x, y, c, chips = _place()
        mine = 2 * x + y
        for a in range(n):
            for k, (px, py) in enumerate(chips):
                pltpu.make_async_remote_copy(src_ref=ins[a].at[2 * px + py], dst_ref=outs[a].at[mine], send_sem=send.at[3 * a + k],
                                             recv_sem=recv.at[3 * a + k], device_id=(px, py, c), device_id_type=MESH_IDS).start()
        for a in range(n):
            for k, (px, py) in enumerate(chips):
                pltpu.make_async_remote_copy(src_ref=ins[a].at[2 * px + py], dst_ref=outs[a].at[2 * px + py], send_sem=send.at[3 * a + k],
                                             recv_sem=recv.at[3 * a + k], device_id=(px, py, c), device_id_type=MESH_IDS).wait()

    return pl.pallas_call(
        body, name="grad_scatter_chips", in_specs=[ANY_SPEC] * n, out_specs=[ANY_SPEC] * n,
        out_shape=[jax.ShapeDtypeStruct(s.shape, s.dtype) for s in ss],
        scratch_shapes=[pltpu.SemaphoreType.DMA((3 * n,))] * 2)(*ss)


def _sum_chips(name, own, parts, idx):
    rows, cols = parts.shape[1:]
    tm = _row_tile(rows)
    per = rows // tm

    def kern(o_idx, a_ref, b_ref, c_ref, d_ref, o_ref):
        o_ref[...] = ((a_ref[...].astype(F32) + b_ref[...].astype(F32)) + c_ref[...].astype(F32)) + d_ref[...].astype(F32)

    def spec(k):
        return pl.BlockSpec((None, tm, cols), functools.partial(lambda i, o_idx, k: (o_idx[k], i, 0), k=k))

    return pl.pallas_call(
        kern, name=name,
        grid_spec=pltpu.PrefetchScalarGridSpec(
            num_scalar_prefetch=1, grid=(per,), in_specs=[spec(0), spec(1), spec(2), spec(3)],
            out_specs=pl.BlockSpec((None, tm, cols), lambda i, o_idx: (0, o_idx[4] * per + i, 0))),
        out_shape=jax.ShapeDtypeStruct((1, 2 * rows, cols), F32), compiler_params=_cparams(("parallel",)))(idx, own, parts, parts, parts)


def _share_with_sibling(gs):
    n = len(gs)

    def body(*refs):
        ins, send, recv = refs[:n], refs[2 * n], refs[2 * n + 1]
        x, y, c, _ = _place()
        cps = []
        for a in range(n):
            half = gs[a].shape[1] // 2
            mine = pl.ds(c * half, half)
            cps.append(pltpu.make_async_remote_copy(src_ref=ins[a].at[0, mine], dst_ref=refs[n + a].at[0, mine], send_sem=send.at[a],
                                                    recv_sem=recv.at[a], device_id=(x, y, 1 - c), device_id_type=MESH_IDS))
        for cp in cps:
            cp.start()
        for cp in cps:
            cp.wait()

    return pl.pallas_call(
        body, name="grad_share_sibling", in_specs=[ANY_SPEC] * n, out_specs=[ANY_SPEC] * n,
        out_shape=[jax.ShapeDtypeStruct(g.shape, g.dtype) for g in gs], input_output_aliases={a: a for a in range(n)},
        scratch_shapes=[pltpu.SemaphoreType.DMA((n,))] * 2)(*gs)


def _allreduce_small(v):
    def body(v_ref, o_ref, land, send, recv):
        x, y, c, _ = _place()
        me = 4 * x + 2 * y + c
        land[me] = v_ref[...]
        cps = []
        for rel in range(1, 8):
            bx, by, bc = (rel >> 2) & 1, (rel >> 1) & 1, rel & 1
            peer = (1 - x if bx else x, 1 - y if by else y, 1 - c if bc else c)
            cps.append(pltpu.make_async_remote_copy(src_ref=v_ref, dst_ref=land.at[me], send_sem=send.at[rel - 1],
                                                    recv_sem=recv.at[rel - 1], device_id=peer, device_id_type=MESH_IDS))
        for cp in cps:
            cp.start()
        for cp in cps:
            cp.wait()
        acc = land[0]
        for d in range(1, 8):
            acc = acc + land[d]
        o_ref[...] = acc

    vm = pl.BlockSpec(memory_space=pltpu.VMEM)
    return pl.pallas_call(
        body, name="allreduce_small", in_specs=[vm], out_specs=vm, out_shape=jax.ShapeDtypeStruct(v.shape, F32),
        scratch_shapes=[pltpu.VMEM((8,) + v.shape, F32), pltpu.SemaphoreType.DMA((7,)), pltpu.SemaphoreType.DMA((7,))])(v)


def _adamw(name, w, g, m, v):
    _, rows, cols = w.shape
    tm = rows if rows * cols <= 128 * 1024 else _row_tile(rows)
    c1 = 1.0 / (1.0 - ADAM_B1 ** ADAM_STEP)
    c2 = 1.0 / (1.0 - ADAM_B2 ** ADAM_STEP)

    def kern(w_ref, g_ref, m_ref, v_ref, d_ref, mo_ref, vo_ref):
        gv = g_ref[...]
        mn = ADAM_B1 * m_ref[...] + (1.0 - ADAM_B1) * gv
        vn = ADAM_B2 * v_ref[...] + (1.0 - ADAM_B2) * (gv * gv)
        d_ref[...] = -ADAM_LR * ((mn * c1) / (jnp.sqrt(vn * c2) + ADAM_EPS) + ADAM_WD * w_ref[...])
        mo_ref[...] = mn
        vo_ref[...] = vn

    spec = pl.BlockSpec((None, tm, cols), lambda i: (0, i, 0))
    return pl.pallas_call(
        kern, name=name, grid=(rows // tm,), in_specs=[spec] * 4, out_specs=[spec] * 3,
        out_shape=[jax.ShapeDtypeStruct(w.shape, F32)] * 3, compiler_params=_cparams(("parallel",)))(w, g, m, v)


SHARDED = (("w_in", 1), ("w_out", 0), ("w_up", 1), ("w_down", 0), ("w_ple_gate", 0), ("w_ple_proj", 1),
           ("ssm_conv_w", 1), ("ffn_conv_w", 1))
MATRICES = ("w_in", "w_out", "w_up", "w_down", "w_ple_gate", "w_ple_proj")
REPLICATED = ("attn_norm_g", "q_norm_g", "k_norm_g", "ssm_conv_b", "dt_bias", "a_log", "d_skip", "ssm_norm_g",
              "ffn_norm_g", "ffn_conv_b", "ple_norm_g")
WEIGHT_ORDER = ("attn_norm_g", "w_in", "q_norm_g", "k_norm_g", "ssm_conv_w", "ssm_conv_b", "dt_bias", "a_log", "d_skip",
                "ssm_norm_g", "w_out", "ffn_norm_g", "w_up", "ffn_conv_w", "ffn_conv_b", "w_down", "ple_norm_g",
                "w_ple_gate", "w_ple_proj")


def _join_chips(g, axis):
    if axis == 0:
        return g.reshape(g.shape[0] * g.shape[1], g.shape[2])
    return jnp.transpose(g, (1, 0, 2)).reshape(g.shape[1], g.shape[0] * g.shape[2])


def _split_chips(g, axis):
    if axis == 0:
        return g.reshape(N_CHIPS, g.shape[0] // N_CHIPS, g.shape[1])
    r, c = g.shape
    return jnp.transpose(g.reshape(r, N_CHIPS, c // N_CHIPS), (1, 0, 2))


def _pack_small(vals, rows=SMALL_ROWS):
    flat = jnp.concatenate([v.reshape(-1) for v in vals])
    return jnp.pad(flat, (0, rows * LANE - flat.shape[0])).reshape(rows, LANE)


def _unpack_small(packed, like):
    flat, out, off = packed.reshape(-1), [], 0
    for v in like:
        out.append(flat[off:off + v.size].reshape(v.shape))
        off += v.size
    return out


def kernel(x, p, attn_norm_g, w_in, q_norm_g, k_norm_g, ssm_conv_w, ssm_conv_b, dt_bias, a_log, d_skip, ssm_norm_g, w_out, ffn_norm_g, w_up, ffn_conv_w, ffn_conv_b, w_down, ple_norm_g, w_ple_gate, w_ple_proj, loss_target, m_attn_norm_g, m_w_in, m_q_norm_g, m_k_norm_g, m_ssm_conv_w, m_ssm_conv_b, m_dt_bias, m_a_log, m_d_skip, m_ssm_norm_g, m_w_out, m_ffn_norm_g, m_w_up, m_ffn_conv_w, m_ffn_conv_b, m_w_down, m_ple_norm_g, m_w_ple_gate, m_w_ple_proj, v_attn_norm_g, v_w_in, v_q_norm_g, v_k_norm_g, v_ssm_conv_w, v_ssm_conv_b, v_dt_bias, v_a_log, v_d_skip, v_ssm_norm_g, v_w_out, v_ffn_norm_g, v_w_up, v_ffn_conv_w, v_ffn_conv_b, v_w_down, v_ple_norm_g, v_w_ple_gate, v_w_ple_proj):
    given = dict(locals())
    w2 = {n: given[n].reshape(given[n].shape[-2:]) if given[n].ndim == 3 else given[n] for n in WEIGHT_ORDER}

    cx, cy, cc = lax.axis_index("x"), lax.axis_index("y"), lax.axis_index("c")
    chip = 2 * cx + cy
    shards = [w2[n].astype(BF16) if n in MATRICES else w2[n] for n, _ in SHARDED]
    full = {n: _join_chips(lax.dynamic_update_index_in_dim(g, s, chip, 0), ax)
            for (n, ax), g, s in zip(SHARDED, _gather_over_chips(shards), shards)}
    win = full["w_in"]
    w_in_p = jnp.concatenate([win[:, 2048:3584], win[:, 0:512], win[:, 1024:2048], win[:, 512:768], win[:, 768:1024],
                              win[:, 3584:3600], jnp.zeros((D_MODEL, PROJ_P - IN_PROJ), BF16)], axis=1)
    wts = {n: w2[n] for n in REPLICATED}
    wts.update(w_in_p=w_in_p, w_out_attn=full["w_out"][:ATTN_DIM], w_out_ssm=full["w_out"][ATTN_DIM:], w_up=full["w_up"],
               w_down=full["w_down"], w_ple_gate=full["w_ple_gate"], w_ple_proj=full["w_ple_proj"],
               ssm_conv_w=full["ssm_conv_w"], ffn_conv_w=full["ffn_conv_w"])

    sq, grad_x, grads = _local_step(x[0], p[0, 0], loss_target[0], wts)
    gi = grads.pop("w_in_p")
    grads["w_in"] = jnp.concatenate([gi[:, OFF_Q:OFF_Q + ATTN_DIM], gi[:, OFF_K:OFF_K + KV_DIM], gi[:, OFF_V:OFF_V + KV_DIM],
                                     gi[:, OFF_Z:OFF_Z + SSM_INNER], gi[:, OFF_XBC:OFF_XBC + XBC_DIM], gi[:, OFF_DT:OFF_DT + SSM_HEADS]],
                                    axis=1)

    chip_major = [grads[n] if grads[n].ndim == 3 else _split_chips(grads[n], ax) for n, ax in SHARDED if n in MATRICES]
    core = cc.astype(jnp.int32).reshape(1)
    chip_sums = [_add_halves("grad_add_halves_" + n, g, got, core) for n, g, got in zip(MATRICES, chip_major, _swap_halves(chip_major))]
    idx = jnp.stack([chip, 2 * (1 - cx) + cy, 2 * cx + (1 - cy), 2 * (1 - cx) + (1 - cy), cc]).astype(jnp.int32)
    halves = [_sum_chips("grad_sum_chips_" + n, s, got, idx) for n, s, got in zip(MATRICES, chip_sums, _scatter_over_chips(chip_sums))]
    g_shard = dict(zip(MATRICES, _share_with_sibling(halves)))

    small_names = REPLICATED + ("ssm_conv_w", "ffn_conv_w")
    small_like = [grads[n] for n in small_names] + [jnp.zeros((1,), F32)]
    small = _allreduce_small(_pack_small([grads[n] for n in small_names] + [jnp.sum(sq).reshape(1)], ALL_SMALL_ROWS))
    small_vals = dict(zip(small_names + ("loss",), _unpack_small(small, small_like)))
    loss = (0.5 / D_MODEL) * small_vals["loss"][0]
    for n in ("ssm_conv_w", "ffn_conv_w"):
        cols = w2[n].shape[1]
        g_shard[n] = lax.dynamic_slice_in_dim(small_vals[n], chip * cols, cols, axis=1)[None]

    delta, new_m, new_v = {}, {}, {}
    for n, _ in SHARDED:
        delta[n], new_m[n], new_v[n] = _adamw("adamw_" + n, given[n], g_shard[n], given["m_" + n], given["v_" + n])
    packed = lambda prefix: _pack_small([given[prefix + n] for n in REPLICATED])[None]
    sm = _adamw("adamw_small", packed(""), _pack_small([small_vals[n] for n in REPLICATED])[None], packed("m_"), packed("v_"))
    for n in REPLICATED:
        g_shard[n] = small_vals[n]
    for dst, packed_out in zip((delta, new_m, new_v), sm):
        for n, val in zip(REPLICATED, _unpack_small(packed_out[0], [w2[n] for n in REPLICATED])):
            dst[n] = val

    def shaped(d):
        return [d[n].reshape(given[n].shape) for n in WEIGHT_ORDER]
    return (loss, grad_x[None], *shaped(g_shard), *shaped(delta), *shaped(new_m), *shaped(new_v))
```

```python
import functools

import jax
import jax.numpy as jnp
from jax import lax
from jax.experimental import pallas as pl
from jax.experimental.pallas import tpu as pltpu

F32 = jnp.float32
BF16 = jnp.bfloat16

D_MODEL = 1024
HEAD_DIM = 64
ATTN_DIM = 512
KV_DIM = 256
N_KV = 4
SSM_INNER = 1024
SSM_HEADS = 16
SSM_STATE = 128
BC_DIM = 256
XBC_DIM = SSM_INNER + 2 * BC_DIM
MIX_DIM = ATTN_DIM + SSM_INNER
IN_PROJ = 3600
D_FF = 2816
PLE_DIM = 256
CHUNK = 128
DILATIONS = (1, 4, 16)
EPS = 1e-6
ADAM_LR, ADAM_B1, ADAM_B2, ADAM_EPS, ADAM_WD, ADAM_STEP = 0.001, 0.9, 0.999, 1e-08, 0.01, 10

PROJ_P = 3712
OFF_XBC, OFF_Q, OFF_Z, OFF_K, OFF_V, OFF_DT = 0, 1536, 2048, 3072, 3328, 3584
LANE = 128
VMEM_LIMIT = 48 * 1024 * 1024
NEG = -1e30


def _cparams(sem):
    return pltpu.CompilerParams(dimension_semantics=sem, vmem_limit_bytes=VMEM_LIMIT)


def _sigmoid(x):
    return 1.0 / (1.0 + jnp.exp(-x))


def _dot(a, b):
    return jnp.dot(a, b, preferred_element_type=F32)


def _dot_nt(a, b):
    return lax.dot_general(a, b, (((1,), (1,)), ((), ())), preferred_element_type=F32)


def _dot_tn(a, b):
    return lax.dot_general(a, b, (((0,), (0,)), ((), ())), preferred_element_type=F32)


def _dot_split(x, m):
    hi = x.astype(BF16)
    lo = (x - hi.astype(F32)).astype(BF16)
    return _dot(hi, m) + _dot(lo, m)


def _rows(name, body, ins, outs, accs=(), tm=512):
    t_rows = next(s[1].shape[0] for s in ins if s[0] in ("t", "tc"))
    tm = min(tm, t_rows)
    in_specs, args = [], []
    for s in ins:
        if s[0] == "t":
            in_specs.append(pl.BlockSpec((tm, s[1].shape[1]), lambda i: (i, 0)))
        elif s[0] == "tc":
            in_specs.append(pl.BlockSpec((tm, s[2]), functools.partial(lambda i, c: (i, c), c=s[3])))
        else:
            in_specs.append(pl.BlockSpec(s[1].shape, lambda i: (0, 0)))
        args.append(s[1])
    out_shape = [jax.ShapeDtypeStruct((t_rows, w), dt) for w, dt in outs]
    out_specs = [pl.BlockSpec((tm, w), lambda i: (i, 0)) for w, _ in outs]
    out_shape += [jax.ShapeDtypeStruct(a, F32) for a in accs]
    out_specs += [pl.BlockSpec(a, lambda i: (0, 0)) for a in accs]
    n_acc = len(accs)

    def kern(*refs):
        if n_acc:
            @pl.when(pl.program_id(0) == 0)
            def _():
                for r in refs[len(refs) - n_acc:]:
                    r[...] = jnp.zeros(r.shape, F32)
        body(*refs)

    return pl.pallas_call(
        kern, name=name, grid=(t_rows // tm,), in_specs=in_specs, out_specs=out_specs, out_shape=out_shape,
        compiler_params=_cparams(("arbitrary",) if n_acc else ("parallel",)))(*args)


NCHUNK = 512


def _col_chunks(n):
    return [(c, min(NCHUNK, n - c)) for c in range(0, n, NCHUNK)]


def _mm_nn(name, pairs, out_dtype, res=None, tm=512, tn=None, halves=False):
    m, n = pairs[0][0].shape[0], pairs[0][1].shape[1]
    tn = n if tn is None else tn
    tm = min(tm, m)
    np_ = len(pairs)
    if halves:
        per = n // 2 // tn
        out_spec = pl.BlockSpec((None, tm, tn), lambda j, i: (j // per, i, j % per))
        out_shape = jax.ShapeDtypeStruct((2, m, n // 2), out_dtype)
    else:
        out_spec = pl.BlockSpec((tm, tn), lambda j, i: (i, j))
        out_shape = jax.ShapeDtypeStruct((m, n), out_dtype)
    in_specs, args = [], []
    for a, w in pairs:
        in_specs += [pl.BlockSpec((tm, a.shape[1]), lambda j, i: (i, 0)), pl.BlockSpec((w.shape[0], tn), lambda j, i: (0, j))]
        args += [a, w]
    if res is not None:
        in_specs.append(pl.BlockSpec((tm, tn), lambda j, i: (i, j)))
        args.append(res)

    def kern(*refs):
        o_ref = refs[-1]
        for c0, cw in _col_chunks(tn):
            acc = None
            for q in range(np_):
                d = _dot(refs[2 * q][...], refs[2 * q + 1][:, c0:c0 + cw])
                acc = d if acc is None else acc + d
            if res is not None:
                acc = acc + refs[2 * np_][:, c0:c0 + cw]
            o_ref[:, c0:c0 + cw] = acc.astype(o_ref.dtype)

    return pl.pallas_call(
        kern, name=name, grid=(n // tn, m // tm), in_specs=in_specs, out_specs=out_spec, out_shape=out_shape,
        compiler_params=_cparams(("parallel", "parallel")))(*args)


def _mm_nt(name, pairs, out_dtype, tm=512, tn=None):
    m, n = pairs[0][0].shape[-2], pairs[0][1].shape[0]
    tn = n if tn is None else tn
    tm = min(tm, m)
    np_ = len(pairs)
    in_specs, args = [], []
    for a, w, kb, *lead in pairs:
        if lead:
            in_specs.append(pl.BlockSpec((None, tm, a.shape[2]), functools.partial(lambda j, i, ld: (ld, i, 0), ld=lead[0])))
        else:
            in_specs.append(pl.BlockSpec((tm, a.shape[1]), lambda j, i: (i, 0)))
        in_specs.append(pl.BlockSpec((tn, a.shape[-1]), functools.partial(lambda j, i, kb: (j, kb), kb=kb)))
        args += [a, w]

    def kern(*refs):
        o_ref = refs[-1]
        for c0, cw in _col_chunks(tn):
            acc = None
            for q in range(np_):
                d = _dot_nt(refs[2 * q][...], refs[2 * q + 1][c0:c0 + cw, :])
                acc = d if acc is None else acc + d
            o_ref[:, c0:c0 + cw] = acc.astype(o_ref.dtype)

    return pl.pallas_call(
        kern, name=name, grid=(n // tn, m // tm), in_specs=in_specs,
        out_specs=pl.BlockSpec((tm, tn), lambda j, i: (i, j)),
        out_shape=jax.ShapeDtypeStruct((m, n), out_dtype), compiler_params=_cparams(("parallel", "parallel")))(*args)


def _mm_tn(name, a, b, tm=None, tn=None, tk=1024, chip_cols=False):
    t, m = a.shape
    n = b.shape[-1] * (2 if b.ndim == 3 else 1)
    tm = m if tm is None else tm
    tn = n if tn is None else tn
    tk = min(tk, t)
    if b.ndim == 3:
        per = n // 2 // tn
        b_spec = pl.BlockSpec((None, tk, tn), lambda i, j, k: (j // per, k, j % per))
    else:
        b_spec = pl.BlockSpec((tk, tn), lambda i, j, k: (k, j))
    if chip_cols:
        out_spec = pl.BlockSpec((None, tm, tn), lambda i, j, k: (j, i, 0))
        out_shape = jax.ShapeDtypeStruct((n // tn, m, tn), F32)
    else:
        out_spec = pl.BlockSpec((tm, tn), lambda i, j, k: (i, j))
        out_shape = jax.ShapeDtypeStruct((m, n), F32)

    def kern(a_ref, b_ref, o_ref):
        @pl.when(pl.program_id(2) == 0)
        def _():
            o_ref[...] = jnp.zeros(o_ref.shape, F32)
        for c0, cw in _col_chunks(tn):
            o_ref[:, c0:c0 + cw] += _dot_tn(a_ref[...], b_ref[:, c0:c0 + cw])

    return pl.pallas_call(
        kern, name=name, grid=(m // tm, n // tn, t // tk),
        in_specs=[pl.BlockSpec((tk, tm), lambda i, j, k: (k, i)), b_spec], out_specs=out_spec, out_shape=out_shape,
        compiler_params=_cparams(("parallel", "parallel", "arbitrary")))(a, b)


def _rms_fwd(name, x, g):
    def body(x_ref, g_ref, h_ref):
        xv = x_ref[...]
        r = lax.rsqrt(jnp.mean(xv * xv, axis=-1, keepdims=True) + EPS)
        h_ref[...] = (xv * r * g_ref[...]).astype(BF16)
    return _rows(name, body, [("t", x), ("p", g)], [(x.shape[1], BF16)])[0]


def _rms_bwd(name, dh, x, g, dres):
    d = x.shape[1]

    def body(dh_ref, x_ref, g_ref, dres_ref, dx_ref, dxb_ref, dg_ref):
        xv, dhv = x_ref[...], dh_ref[...]
        r = lax.rsqrt(jnp.mean(xv * xv, axis=-1, keepdims=True) + EPS)
        gd = dhv * g_ref[...]
        dx = dres_ref[...] + r * gd - xv * (r * r * r * jnp.mean(xv * gd, axis=-1, keepdims=True))
        dx_ref[...] = dx
        dxb_ref[...] = dx.astype(BF16)
        dg_ref[...] += jnp.sum(dhv * xv * r, axis=0, keepdims=True)
    return _rows(name, body, [("t", dh), ("t", x), ("p", g), ("t", dres)], [(d, F32), (d, BF16)], accs=[(1, d)])


def _head_mean_matrix(width):
    i = jnp.arange(width) // HEAD_DIM
    return jnp.where(i[:, None] == i[None, :], 1.0 / HEAD_DIM, 0.0).astype(BF16)


ATT_SPAN = 2048
N_QH = 8


def _lane_lo(rows):
    return lax.broadcasted_iota(jnp.int32, (rows, LANE), 1) < HEAD_DIM


def _swap_halves_lanes(x):
    return pltpu.roll(x, HEAD_DIM, axis=1)


def _qknorm_fwd2(proj, gq_t, gk_t, tm=256):
    t = proj.shape[0]
    bq, bk = _head_mean_matrix(ATTN_DIM), _head_mean_matrix(KV_DIM)
    scale = HEAD_DIM ** -0.5

    def kern(q_ref, k_ref, v_ref, gq_ref, gk_ref, bq_ref, bk_ref, qo_ref, kvo_ref):
        q, k, v = q_ref[...], k_ref[...], v_ref[...]
        qn = (q * lax.rsqrt(_dot_split(q * q, bq_ref[...]) + EPS) * gq_ref[...]) * scale
        kn = k * lax.rsqrt(_dot_split(k * k, bk_ref[...]) + EPS) * gk_ref[...]
        lo = _lane_lo(tm)
        for j in range(N_KV):
            blk = qn[:, j * LANE:(j + 1) * LANE]
            qo_ref[2 * j] = jnp.where(lo, blk, 0.0)
            qo_ref[2 * j + 1] = jnp.where(lo, _swap_halves_lanes(blk), 0.0)
        for j in range(2):
            kb, vb = kn[:, j * LANE:(j + 1) * LANE], v[:, j * LANE:(j + 1) * LANE]
            kvo_ref[2 * j] = jnp.where(lo, kb, _swap_halves_lanes(vb))
            kvo_ref[2 * j + 1] = jnp.where(lo, _swap_halves_lanes(kb), vb)

    col = lambda w, idx: pl.BlockSpec((tm, w), functools.partial(lambda i, idx: (i, idx), idx=idx))
    par = lambda a: pl.BlockSpec(a.shape, lambda i: (0, 0))
    return pl.pallas_call(
        kern, name="qknorm_fwd", grid=(t // tm,),
        in_specs=[col(ATTN_DIM, OFF_Q // ATTN_DIM), col(KV_DIM, OFF_K // KV_DIM), col(KV_DIM, OFF_V // KV_DIM),
                  par(gq_t), par(gk_t), par(bq), par(bk)],
        out_specs=[pl.BlockSpec((N_QH, tm, LANE), lambda i: (0, i, 0)), pl.BlockSpec((N_KV, tm, LANE), lambda i: (0, i, 0))],
        out_shape=[jax.ShapeDtypeStruct((N_QH, t, LANE), F32), jax.ShapeDtypeStruct((N_KV, t, LANE), F32)],
        compiler_params=_cparams(("parallel",)))(proj, proj, proj, gq_t, gk_t, bq, bk)


def _qknorm_bwd2(proj, gq_t, gk_t, dqs, dkvs, tm=256):
    t = proj.shape[0]
    bq, bk = _head_mean_matrix(ATTN_DIM), _head_mean_matrix(KV_DIM)
    scale = HEAD_DIM ** -0.5

    def kern(q_ref, k_ref, gq_ref, gk_ref, bq_ref, bk_ref, a1, a2, a3, b1, b2, b3, dq_ref, dk_ref, dv_ref, dgq_ref, dgk_ref):
        @pl.when(pl.program_id(0) == 0)
        def _():
            dgq_ref[...] = jnp.zeros(dgq_ref.shape, F32)
            dgk_ref[...] = jnp.zeros(dgk_ref.shape, F32)
        lo = _lane_lo(tm)
        sq = [a1[h] + a2[h] + a3[h] for h in range(N_QH)]
        skv = [b1[h] + b2[h] + b3[h] for h in range(N_KV)]
        dqn = jnp.concatenate([jnp.where(lo, sq[2 * j], _swap_halves_lanes(sq[2 * j + 1])) for j in range(N_KV)], axis=1) * scale
        dkn = jnp.concatenate([jnp.where(lo, skv[2 * j], _swap_halves_lanes(skv[2 * j + 1])) for j in range(2)], axis=1)
        dv = jnp.concatenate([jnp.where(lo, _swap_halves_lanes(skv[2 * j]), skv[2 * j + 1]) for j in range(2)], axis=1)
        q, k = q_ref[...], k_ref[...]
        rq = lax.rsqrt(_dot_split(q * q, bq_ref[...]) + EPS)
        rk = lax.rsqrt(_dot_split(k * k, bk_ref[...]) + EPS)
        gdq, gdk = dqn * gq_ref[...], dkn * gk_ref[...]
        dq_ref[...] = (rq * gdq - q * (rq * rq * rq * _dot_split(q * gdq, bq_ref[...]))).astype(BF16)
        dk_ref[...] = (rk * gdk - k * (rk * rk * rk * _dot_split(k * gdk, bk_ref[...]))).astype(BF16)
        dv_ref[...] = dv.astype(BF16)
        dgq_ref[...] += jnp.sum(dqn * q * rq, axis=0, keepdims=True)
        dgk_ref[...] += jnp.sum(dkn * k * rk, axis=0, keepdims=True)

    col = lambda w, idx: pl.BlockSpec((tm, w), functools.partial(lambda i, idx: (i, idx), idx=idx))
    par = lambda a: pl.BlockSpec(a.shape, lambda i: (0, 0))
    blk = lambda n: pl.BlockSpec((n, tm, LANE), lambda i: (0, i, 0))
    row = lambda w: pl.BlockSpec((tm, w), lambda i: (i, 0))
    acc = lambda w: pl.BlockSpec((1, w), lambda i: (0, 0))
    return pl.pallas_call(
        kern, name="qknorm_bwd", grid=(t // tm,),
        in_specs=[col(ATTN_DIM, OFF_Q // ATTN_DIM), col(KV_DIM, OFF_K // KV_DIM), par(gq_t), par(gk_t), par(bq), par(bk)]
        + [blk(N_QH)] * 3 + [blk(N_KV)] * 3,
        out_specs=[row(ATTN_DIM), row(KV_DIM), row(KV_DIM), acc(ATTN_DIM), acc(KV_DIM)],
        out_shape=[jax.ShapeDtypeStruct((t, ATTN_DIM), BF16), jax.ShapeDtypeStruct((t, KV_DIM), BF16),
                   jax.ShapeDtypeStruct((t, KV_DIM), BF16), jax.ShapeDtypeStruct((1, ATTN_DIM), F32),
                   jax.ShapeDtypeStruct((1, KV_DIM), F32)],
        compiler_params=_cparams(("arbitrary",)))(proj, proj, gq_t, gk_t, bq, bk, *dqs, *dkvs)


def _att_rows(b, r, dil):
    if dil == 1:
        return pl.ds(b * CHUNK, CHUNK)
    return pl.ds(b * CHUNK * dil + r, CHUNK, stride=dil)


def _for_residues(dil, unit):
    for r in range(dil):
        unit(r, 0)


def _band_qk(first):
    ri = lax.broadcasted_iota(jnp.int32, (CHUNK, 2 * CHUNK), 0)
    cj = lax.broadcasted_iota(jnp.int32, (CHUNK, 2 * CHUNK), 1)
    band = (cj - ri >= 0) & (cj - ri <= CHUNK)
    return band if first is None else band & (jnp.logical_not(first) | (cj >= CHUNK))


def _band_kq(last):
    rj = lax.broadcasted_iota(jnp.int32, (CHUNK, 2 * CHUNK), 0)
    ci = lax.broadcasted_iota(jnp.int32, (CHUNK, 2 * CHUNK), 1)
    band = (ci - rj >= 0) & (ci - rj <= CHUNK)
    return band if last is None else band & (jnp.logical_not(last) | (ci < CHUNK))


def _att_specs(t, dil):
    sub = CHUNK * dil
    nb, last = ATT_SPAN // sub, t // sub - 1
    cur = lambda heads: pl.BlockSpec((heads, ATT_SPAN, LANE), lambda kh, n: (kh, n, 0))
    prev = lambda heads: pl.BlockSpec((heads, sub, LANE), lambda kh, n: (kh, jnp.maximum(n * nb - 1, 0), 0))
    nxt = lambda heads: pl.BlockSpec((heads, sub, LANE), lambda kh, n: (kh, jnp.minimum((n + 1) * nb, last), 0))
    return sub, nb, cur, prev, nxt


def _attn_fwd2(q, kv, dil):
    t = q.shape[1]
    sub, nb, cur, prev, _ = _att_specs(t, dil)

    def kern(q_ref, kvp_ref, kvc_ref, o_ref, lse_ref):
        n = pl.program_id(1)
        lane = lax.broadcasted_iota(jnp.int32, (CHUNK, LANE), 1)
        for b in range(nb):
            mask = _band_qk((n == 0) if b == 0 else None)

            def unit(r, carry, b=b, mask=mask):
                rows = _att_rows(b, r, dil)
                kvp = kvc_ref[_att_rows(b - 1, r, dil), :] if b > 0 else kvp_ref[_att_rows(0, r, dil), :]
                kvcat = jnp.concatenate([kvp, kvc_ref[rows, :]], axis=0).astype(BF16)
                lse_tile = jnp.zeros((CHUNK, LANE), F32)
                for g in range(2):
                    s = jnp.where(mask, _dot_nt(q_ref.at[g][rows, :].astype(BF16), kvcat), NEG)
                    m = jnp.max(s, axis=1, keepdims=True)
                    p = jnp.exp(s - m)
                    l = jnp.sum(p, axis=1, keepdims=True)
                    o_ref.at[g][rows, :] = _dot(p.astype(BF16), kvcat) * (1.0 / l)
                    lse_tile = jnp.where(lane == g, m + jnp.log(l), lse_tile)
                lse_ref[rows, :] = lse_tile
                return carry
            _for_residues(dil, unit)

    return pl.pallas_call(
        kern, name=f"attn_fwd_d{dil}", grid=(N_KV, t // ATT_SPAN), in_specs=[cur(2), prev(None), cur(None)],
        out_specs=[cur(2), cur(None)],
        out_shape=[jax.ShapeDtypeStruct((N_QH, t, LANE), F32), jax.ShapeDtypeStruct((N_KV, t, LANE), F32)],
        compiler_params=_cparams(("parallel", "parallel")))(q, kv, kv)


def _attn_merge2(os_, lses, tm=256):
    t = os_[0].shape[1]

    def kern(o1, o2, o3, l1, l2, l3, out_ref, lse_ref):
        pieces = []
        for kh in range(N_KV):
            a, b, c = l1[kh], l2[kh], l3[kh]
            m = jnp.maximum(jnp.maximum(a, b), c)
            tot = m + jnp.log(jnp.exp(a - m) + jnp.exp(b - m) + jnp.exp(c - m))
            lse_ref[kh] = tot
            wa, wb, wc = jnp.exp(a - tot), jnp.exp(b - tot), jnp.exp(c - tot)
            for g in range(2):
                h = 2 * kh + g
                acc = wa[:, g:g + 1] * o1[h] + wb[:, g:g + 1] * o2[h] + wc[:, g:g + 1] * o3[h]
                pieces.append(acc[:, HEAD_DIM:])
        out_ref[...] = jnp.concatenate(pieces, axis=1).astype(BF16)

    blk = lambda n: pl.BlockSpec((n, tm, LANE), lambda i: (0, i, 0))
    return pl.pallas_call(
        kern, name="attn_merge", grid=(t // tm,), in_specs=[blk(N_QH)] * 3 + [blk(N_KV)] * 3,
        out_specs=[pl.BlockSpec((tm, ATTN_DIM), lambda i: (i, 0)), blk(N_KV)],
        out_shape=[jax.ShapeDtypeStruct((t, ATTN_DIM), BF16), jax.ShapeDtypeStruct((N_KV, t, LANE), F32)],
        compiler_params=_cparams(("parallel",)))(*os_, *lses)


def _attn_bwd_prep2(dmix, attn_out, tm=256):
    t = attn_out.shape[0]

    def kern(do_ref, o_ref, dot_ref, d_ref):
        do = do_ref[...]
        prod = do * o_ref[...].astype(F32)
        lo = _lane_lo(tm)
        lane = lax.broadcasted_iota(jnp.int32, (tm, LANE), 1)
        for kh in range(N_KV):
            blk, pb = do[:, kh * LANE:(kh + 1) * LANE], prod[:, kh * LANE:(kh + 1) * LANE]
            dot_ref[2 * kh] = jnp.where(lo, 0.0, _swap_halves_lanes(blk))
            dot_ref[2 * kh + 1] = jnp.where(lo, 0.0, blk)
            s_lo = jnp.sum(jnp.where(lo, pb, 0.0), axis=1, keepdims=True)
            s_hi = jnp.sum(pb, axis=1, keepdims=True) - s_lo
            d_ref[kh] = jnp.where(lane == 0, s_lo, jnp.where(lane == 1, s_hi, 0.0))

    blk = lambda n: pl.BlockSpec((n, tm, LANE), lambda i: (0, i, 0))
    return pl.pallas_call(
        kern, name="attn_bwd_prep", grid=(t // tm,),
        in_specs=[pl.BlockSpec((tm, ATTN_DIM), lambda i: (i, SSM_INNER // ATTN_DIM)), pl.BlockSpec((tm, ATTN_DIM), lambda i: (i, 0))],
        out_specs=[blk(N_QH), blk(N_KV)],
        out_shape=[jax.ShapeDtypeStruct((N_QH, t, LANE), F32), jax.ShapeDtypeStruct((N_KV, t, LANE), F32)],
        compiler_params=_cparams(("parallel",)))(dmix, attn_out)


def _attn_dq2(q, kv, dot, lse, dsum, dil):
    t = q.shape[1]
    sub, nb, cur, prev, _ = _att_specs(t, dil)

    def kern(q_ref, kvp_ref, kvc_ref, do_ref, lse_ref, d_ref, dq_ref):
        n = pl.program_id(1)
        for b in range(nb):
            mask = _band_qk((n == 0) if b == 0 else None)

            def unit(r, carry, b=b, mask=mask):
                rows = _att_rows(b, r, dil)
                kvp = kvc_ref[_att_rows(b - 1, r, dil), :] if b > 0 else kvp_ref[_att_rows(0, r, dil), :]
                kvcat = jnp.concatenate([kvp, kvc_ref[rows, :]], axis=0).astype(BF16)
                lse_t, d_t = lse_ref[rows, :], d_ref[rows, :]
                for g in range(2):
                    s = jnp.where(mask, _dot_nt(q_ref.at[g][rows, :].astype(BF16), kvcat), NEG)
                    p = jnp.exp(s - lse_t[:, g:g + 1])
                    dp = _dot_nt(do_ref.at[g][rows, :].astype(BF16), kvcat)
                    ds = p * (dp - d_t[:, g:g + 1])
                    dq_ref.at[g][rows, :] = _dot(ds.astype(BF16), kvcat)
                return carry
            _for_residues(dil, unit)

    return pl.pallas_call(
        kern, name=f"attn_dq_d{dil}", grid=(N_KV, t // ATT_SPAN),
        in_specs=[cur(2), prev(None), cur(None), cur(2), cur(None), cur(None)], out_specs=cur(2),
        out_shape=jax.ShapeDtypeStruct((N_QH, t, LANE), F32),
        compiler_params=_cparams(("parallel", "parallel")))(q, kv, kv, dot, lse, dsum)


def _attn_dkv2(q, kv, dot, lse, dsum, dil):
    t = q.shape[1]
    sub, nb, cur, _, nxt = _att_specs(t, dil)
    nsteps = t // ATT_SPAN

    def kern(kv_ref, qc_ref, qn_ref, doc_ref, don_ref, lc_ref, ln_ref, dc_ref, dn_ref, dkv_ref):
        n = pl.program_id(1)
        for b in range(nb):
            inside = b < nb - 1
            mask = _band_kq(None if inside else (n == nsteps - 1))

            def unit(r, carry, b=b, inside=inside, mask=mask):
                rows = _att_rows(b, r, dil)
                nrows = _att_rows(b + 1, r, dil) if inside else _att_rows(0, r, dil)
                kvb = kv_ref[rows, :].astype(BF16)
                follow = lambda cref, nref: (cref if inside else nref)[nrows, :]
                lse_t = jnp.concatenate([lc_ref[rows, :].T, follow(lc_ref, ln_ref).T], axis=1)
                d_t = jnp.concatenate([dc_ref[rows, :].T, follow(dc_ref, dn_ref).T], axis=1)
                acc = jnp.zeros((CHUNK, LANE), F32)
                for g in range(2):
                    qdo = jnp.concatenate([qc_ref.at[g][rows, :], follow(qc_ref.at[g], qn_ref.at[g]),
                                           doc_ref.at[g][rows, :], follow(doc_ref.at[g], don_ref.at[g])], axis=0).astype(BF16)
                    both = _dot_nt(kvb, qdo)
                    pt = jnp.exp(jnp.where(mask, both[:, :2 * CHUNK], NEG) - lse_t[g:g + 1, :])
                    dst = pt * (both[:, 2 * CHUNK:] - d_t[g:g + 1, :])
                    acc = acc + _dot(jnp.concatenate([dst, pt], axis=1).astype(BF16), qdo)
                dkv_ref[rows, :] = acc
                return carry
            _for_residues(dil, unit)

    return pl.pallas_call(
        kern, name=f"attn_dkv_d{dil}", grid=(N_KV, nsteps),
        in_specs=[cur(None), cur(2), nxt(2), cur(2), nxt(2), cur(None), nxt(None), cur(None), nxt(None)], out_specs=cur(None),
        out_shape=jax.ShapeDtypeStruct((N_KV, t, LANE), F32),
        compiler_params=_cparams(("parallel", "parallel")))(kv, q, q, dot, dot, lse, lse, dsum, dsum)


HALO = 8
SSM_CONV_TM, SSM_CONV_W = 512, 512
FFN_CONV_TM, FFN_CONV_W = 256, 1408


def _halo_specs(tm, width, t_rows, col_off=0, lead=None):
    per, last = tm // HALO, t_rows // HALO - 1
    row_maps = (lambda i: i, lambda i: jnp.maximum(i * per - 1, 0), lambda i: jnp.minimum((i + 1) * per, last))
    specs = []
    for rows, rm in zip((tm, HALO, HALO), row_maps):
        if lead is None:
            specs.append(pl.BlockSpec((rows, width), functools.partial(lambda c, i, rm: (rm(i), c + col_off), rm=rm)))
        else:
            specs.append(pl.BlockSpec((None, rows, width), functools.partial(lambda c, i, rm: (lead, rm(i), c + col_off), rm=rm)))
    return specs


def _fill_ext(buf, tile_ref, before_ref, after_ref, i, nt):
    tm = tile_ref.shape[0]
    buf[0:HALO, :] = jnp.where(i > 0, before_ref[...].astype(F32), 0.0)
    buf[HALO:HALO + tm, :] = tile_ref[...].astype(F32)
    if after_ref is not None:
        buf[HALO + tm:, :] = jnp.where(i < nt - 1, after_ref[...].astype(F32), 0.0)


CONV_RB, CONV_CW = 16, 256


def _lane_chunks(width):
    return [slice(c0, min(c0 + CONV_CW, width)) for c0 in range(0, width, CONV_CW)]


def _shifted(buf, taps, r0, rows, cs):
    return [buf[pl.ds(HALO - (taps - 1) + k + r0, rows), cs] for k in range(taps)]


def _taps_fwd(xs, w, b):
    acc = b
    for k, xk in enumerate(xs):
        acc = acc + w[k:k + 1, :] * xk
    return acc


def _taps_bwd(bufd, w, taps, r0, rows, cs):
    acc = None
    for k in range(taps):
        term = w[k:k + 1, :] * bufd[pl.ds(r0 + (taps - 1) - k, rows), cs]
        acc = term if acc is None else acc + term
    return acc


def _fold8(z):
    return z[:HALO] + z[HALO:] if z.shape[0] == 2 * HALO else z


def _silu_grad(pre):
    sg = _sigmoid(pre)
    return sg * (1.0 + pre * (1.0 - sg))


def _ssm_conv_fwd(proj, w, b):
    t = proj.shape[0]
    tm, wd = min(SSM_CONV_TM, t), SSM_CONV_W
    nt, taps = t // tm, w.shape[0]

    def kern(x_ref, xb_ref, w_ref, b_ref, o_ref, buf):
        _fill_ext(buf, x_ref, xb_ref, None, pl.program_id(1), nt)
        for cs in _lane_chunks(wd):
            wv, bv = w_ref[:, cs], b_ref[:, cs]
            for r0 in range(0, tm, CONV_RB):
                pre = _taps_fwd(_shifted(buf, taps, r0, CONV_RB, cs), wv, bv)
                o_ref[r0:r0 + CONV_RB, cs] = pre * _sigmoid(pre)

    tile, before, _ = _halo_specs(tm, wd, t)
    par = lambda rows: pl.BlockSpec((rows, wd), lambda c, i: (0, c))
    return pl.pallas_call(
        kern, name="ssm_conv_fwd", grid=(XBC_DIM // wd, nt), in_specs=[tile, before, par(taps), par(1)],
        out_specs=pl.BlockSpec((tm, wd), lambda c, i: (i, c)), out_shape=jax.ShapeDtypeStruct((t, XBC_DIM), F32),
        scratch_shapes=[pltpu.VMEM((tm + HALO, wd), F32)],
        compiler_params=_cparams(("parallel", "parallel")))(proj, proj, w, b)


def _ssm_conv_bwd(proj, w, b, dact):
    t = proj.shape[0]
    tm, wd = min(SSM_CONV_TM, t), SSM_CONV_W
    nt, taps = t // tm, w.shape[0]

    def kern(x_ref, xb_ref, xa_ref, d_ref, dn_ref, w_ref, b_ref, dx_ref, gw_ref, gb_ref, buf, bufd):
        i = pl.program_id(1)
        _fill_ext(buf, x_ref, xb_ref, xa_ref, i, nt)

        @pl.when(i == 0)
        def _():
            gw_ref[...] = jnp.zeros(gw_ref.shape, F32)
            gb_ref[...] = jnp.zeros(gb_ref.shape, F32)
        for cs in _lane_chunks(wd):
            wv, bv = w_ref[:, cs], b_ref[:, cs]
            acc = [jnp.zeros((HALO, cs.stop - cs.start), F32) for _ in range(taps + 1)]
            for r0 in list(range(0, tm, CONV_RB)) + [tm]:
                inside = r0 < tm
                rows = CONV_RB if inside else HALO
                xs = _shifted(buf, taps, r0, rows, cs)
                d = d_ref[r0:r0 + rows, cs] if inside else jnp.where(i < nt - 1, dn_ref[:, cs], 0.0)
                dpre = d * _silu_grad(_taps_fwd(xs, wv, bv))
                bufd[r0:r0 + rows, cs] = dpre
                if inside:
                    acc[taps] = acc[taps] + _fold8(dpre)
                    for k in range(taps):
                        acc[k] = acc[k] + _fold8(dpre * xs[k])
            gb_ref[:, cs] += jnp.sum(acc[taps], axis=0, keepdims=True)
            for k in range(taps):
                gw_ref[k:k + 1, cs] += jnp.sum(acc[k], axis=0, keepdims=True)
            for r0 in range(0, tm, CONV_RB):
                dx_ref[r0:r0 + CONV_RB, cs] = _taps_bwd(bufd, wv, taps, r0, CONV_RB, cs).astype(BF16)

    xt, xb, xa = _halo_specs(tm, wd, t)
    dt_, _, dn = _halo_specs(tm, wd, t)
    par = lambda rows: pl.BlockSpec((rows, wd), lambda c, i: (0, c))
    return pl.pallas_call(
        kern, name="ssm_conv_bwd", grid=(XBC_DIM // wd, nt), in_specs=[xt, xb, xa, dt_, dn, par(taps), par(1)],
        out_specs=[pl.BlockSpec((tm, wd), lambda c, i: (i, c)), par(taps), par(1)],
        out_shape=[jax.ShapeDtypeStruct((t, XBC_DIM), BF16), jax.ShapeDtypeStruct((taps, XBC_DIM), F32),
                   jax.ShapeDtypeStruct((1, XBC_DIM), F32)],
        scratch_shapes=[pltpu.VMEM((tm + 2 * HALO, wd), F32), pltpu.VMEM((tm + HALO, wd), F32)],
        compiler_params=_cparams(("parallel", "arbitrary")))(proj, proj, proj, dact, dact, w, b)


def _ffn_act_fwd(u, w, b):
    t = u.shape[1]
    tm, wd = min(FFN_CONV_TM, t), FFN_CONV_W
    nt, taps, nc = t // tm, w.shape[0], D_FF // FFN_CONV_W

    def kern(g_ref, gb_ref, v_ref, vb_ref, wg_ref, wv_ref, bg_ref, bv_ref, a_ref, bufg, bufv):
        i = pl.program_id(1)
        _fill_ext(bufg, g_ref, gb_ref, None, i, nt)
        _fill_ext(bufv, v_ref, vb_ref, None, i, nt)
        for cs in _lane_chunks(wd):
            wg, wv, bg, bv = wg_ref[:, cs], wv_ref[:, cs], bg_ref[:, cs], bv_ref[:, cs]
            for r0 in range(0, tm, CONV_RB):
                g = _taps_fwd(_shifted(bufg, taps, r0, CONV_RB, cs), wg, bg)
                v = _taps_fwd(_shifted(bufv, taps, r0, CONV_RB, cs), wv, bv)
                a_ref[r0:r0 + CONV_RB, cs] = (g * _sigmoid(g) * v).astype(BF16)

    gt, gbf, _ = _halo_specs(tm, wd, t, lead=0)
    vt, vbf, _ = _halo_specs(tm, wd, t, lead=1)
    par = lambda rows, off: pl.BlockSpec((rows, wd), functools.partial(lambda c, i, off: (0, c + off), off=off))
    return pl.pallas_call(
        kern, name="ffn_act_fwd", grid=(nc, nt),
        in_specs=[gt, gbf, vt, vbf, par(taps, 0), par(taps, nc), par(1, 0), par(1, nc)],
        out_specs=pl.BlockSpec((tm, wd), lambda c, i: (i, c)), out_shape=jax.ShapeDtypeStruct((t, D_FF), BF16),
        scratch_shapes=[pltpu.VMEM((tm + HALO, wd), F32)] * 2,
        compiler_params=_cparams(("parallel", "parallel")))(u, u, u, u, w, w, b, b)


def _ffn_act_bwd(u, w, b, da):
    t = u.shape[1]
    tm, wd = min(FFN_CONV_TM, t), FFN_CONV_W
    nt, taps, nc = t // tm, w.shape[0], D_FF // FFN_CONV_W

    def kern(g_ref, gb_ref, ga_ref, v_ref, vb_ref, va_ref, d_ref, dn_ref, wg_ref, wv_ref, bg_ref, bv_ref,
             du_ref, gwg_ref, gwv_ref, gbg_ref, gbv_ref, bufg, bufv, bufdg, bufdv):
        i = pl.program_id(1)
        _fill_ext(bufg, g_ref, gb_ref, ga_ref, i, nt)
        _fill_ext(bufv, v_ref, vb_ref, va_ref, i, nt)

        @pl.when(i == 0)
        def _():
            for r in (gwg_ref, gwv_ref, gbg_ref, gbv_ref):
                r[...] = jnp.zeros(r.shape, F32)
        for cs in _lane_chunks(wd):
            wg, wv, bg, bv = wg_ref[:, cs], wv_ref[:, cs], bg_ref[:, cs], bv_ref[:, cs]
            zero = jnp.zeros((HALO, cs.stop - cs.start), F32)
            accg, accv = [zero] * (taps + 1), [zero] * (taps + 1)
            for r0 in list(range(0, tm, CONV_RB)) + [tm]:
                inside = r0 < tm
                rows = CONV_RB if inside else HALO
                xg, xv = _shifted(bufg, taps, r0, rows, cs), _shifted(bufv, taps, r0, rows, cs)
                g, v = _taps_fwd(xg, wg, bg), _taps_fwd(xv, wv, bv)
                dav = d_ref[r0:r0 + rows, cs] if inside else jnp.where(i < nt - 1, dn_ref[:, cs], 0.0)
                sg = _sigmoid(g)
                dg = dav * v * (sg * (1.0 + g * (1.0 - sg)))
                dv = dav * (g * sg)
                bufdg[r0:r0 + rows, cs] = dg
                bufdv[r0:r0 + rows, cs] = dv
                if inside:
                    accg[taps], accv[taps] = accg[taps] + _fold8(dg), accv[taps] + _fold8(dv)
                    for k in range(taps):
                        accg[k], accv[k] = accg[k] + _fold8(dg * xg[k]), accv[k] + _fold8(dv * xv[k])
            gbg_ref[:, cs] += jnp.sum(accg[taps], axis=0, keepdims=True)
            gbv_ref[:, cs] += jnp.sum(accv[taps], axis=0, keepdims=True)
            for k in range(taps):
                gwg_ref[k:k + 1, cs] += jnp.sum(accg[k], axis=0, keepdims=True)
                gwv_ref[k:k + 1, cs] += jnp.sum(accv[k], axis=0, keepdims=True)
            for r0 in range(0, tm, CONV_RB):
                du_ref[0, r0:r0 + CONV_RB, cs] = _taps_bwd(bufdg, wg, taps, r0, CONV_RB, cs).astype(BF16)
                du_ref[1, r0:r0 + CONV_RB, cs] = _taps_bwd(bufdv, wv, taps, r0, CONV_RB, cs).astype(BF16)

    gt, gbf, gaf = _halo_specs(tm, wd, t, lead=0)
    vt, vbf, vaf = _halo_specs(tm, wd, t, lead=1)
    dt_, _, dn = _halo_specs(tm, wd, t)
    par = lambda rows, off: pl.BlockSpec((rows, wd), functools.partial(lambda c, i, off: (0, c + off), off=off))
    return pl.pallas_call(
        kern, name="ffn_act_bwd", grid=(nc, nt),
        in_specs=[gt, gbf, gaf, vt, vbf, vaf, dt_, dn, par(taps, 0), par(taps, nc), par(1, 0), par(1, nc)],
        out_specs=[pl.BlockSpec((2, tm, wd), lambda c, i: (0, i, c)), par(taps, 0), par(taps, 0), par(1, 0), par(1, 0)],
        out_shape=[jax.ShapeDtypeStruct((2, t, D_FF), BF16)] + [jax.ShapeDtypeStruct((taps, D_FF), F32)] * 2
        + [jax.ShapeDtypeStruct((1, D_FF), F32)] * 2,
        scratch_shapes=[pltpu.VMEM((tm + 2 * HALO, wd), F32)] * 2 + [pltpu.VMEM((tm + HALO, wd), F32)] * 2,
        compiler_params=_cparams(("parallel", "arbitrary")))(u, u, u, u, u, u, da, da, w, w, b, b)


def _softplus(x):
    e = jnp.exp(-jnp.abs(x))
    return jnp.maximum(x, 0.0) + jnp.where(e < 1e-4, e - 0.5 * e * e, jnp.log(1.0 + e))


def _tri(lower):
    r = lax.broadcasted_iota(jnp.int32, (CHUNK, CHUNK), 0)
    c = lax.broadcasted_iota(jnp.int32, (CHUNK, CHUNK), 1)
    return (r >= c) if lower else (r <= c)


def _cum(mat_bool, x):
    return jnp.dot(mat_bool.astype(F32), x, precision=lax.Precision.HIGHEST, preferred_element_type=F32)


def _pair_sel(lane_lo, tile, h0):
    return jnp.where(lane_lo, tile[:, h0:h0 + 1], tile[:, h0 + 1:h0 + 2])


def _ssd_fwd(xbc_act, proj, dt_bias_p, a_log_p, dskip_t):
    t = xbc_act.shape[0]
    nch = t // CHUNK

    def kern(xa_ref, dtr_ref, bias_ref, alog_ref, dsk_ref, y_ref, dt_ref, hs_ref, hst):
        @pl.when(pl.program_id(0) == 0)
        def _():
            hst[...] = jnp.zeros(hst.shape, F32)
        dt = _softplus(dtr_ref[...] + bias_ref[...])
        dt_ref[...] = dt
        acum = _cum(_tri(True), dt * (-jnp.exp(alog_ref[...])))
        acum_t = acum.T
        ea = jnp.exp(acum)
        a_last = acum[CHUNK - 1:CHUNK, :]
        dend = jnp.exp(a_last - acum)
        ea_last = jnp.exp(a_last)
        causal = _tri(True)
        lane_lo = lax.broadcasted_iota(jnp.int32, (CHUNK, LANE), 1) < HEAD_DIM
        row_lo = lax.broadcasted_iota(jnp.int32, (CHUNK, LANE), 0) < HEAD_DIM
        for g in range(2):
            bg = xa_ref[:, SSM_INNER + g * SSM_STATE:SSM_INNER + (g + 1) * SSM_STATE].astype(BF16)
            cg = xa_ref[:, SSM_INNER + BC_DIM + g * SSM_STATE:SSM_INNER + BC_DIM + (g + 1) * SSM_STATE].astype(BF16)
            cb = _dot_nt(cg, bg)
            for j in range(4 * g, 4 * g + 4):
                h0 = 2 * j
                cols = slice(j * LANE, (j + 1) * LANE)
                xp = xa_ref[:, cols]
                xdt = xp * _pair_sel(lane_lo, dt, h0)
                ydiag = None
                for hh, sel in ((h0, lane_lo), (h0 + 1, ~lane_lo)):
                    seg = acum[:, hh:hh + 1] - acum_t[hh:hh + 1, :]
                    mm = (cb * jnp.where(causal, jnp.exp(jnp.minimum(seg, 0.0)), 0.0)).astype(BF16)
                    d = _dot(mm, jnp.where(sel, xdt, 0.0).astype(BF16))
                    ydiag = d if ydiag is None else ydiag + d
                hp = hst[cols, :]
                hs_ref[cols, :] = hp
                yoff = _dot_nt(cg, hp.astype(BF16)) * _pair_sel(lane_lo, ea, h0)
                y_ref[:, cols] = ydiag + yoff + dsk_ref[:, cols] * xp
                xw = (xdt * _pair_sel(lane_lo, dend, h0)).astype(BF16)
                rowf = jnp.where(row_lo, ea_last[:, h0:h0 + 1], ea_last[:, h0 + 1:h0 + 2])
                hst[cols, :] = hp * rowf + _dot_tn(xw, bg)

    return pl.pallas_call(
        kern, name="ssd_fwd", grid=(nch,),
        in_specs=[pl.BlockSpec((CHUNK, XBC_DIM), lambda c: (c, 0)), pl.BlockSpec((CHUNK, LANE), lambda c: (c, OFF_DT // LANE)),
                  pl.BlockSpec((1, LANE), lambda c: (0, 0)), pl.BlockSpec((1, LANE), lambda c: (0, 0)),
                  pl.BlockSpec((1, SSM_INNER), lambda c: (0, 0))],
        out_specs=[pl.BlockSpec((CHUNK, SSM_INNER), lambda c: (c, 0)), pl.BlockSpec((CHUNK, LANE), lambda c: (c, 0)),
                   pl.BlockSpec((None, SSM_INNER, SSM_STATE), lambda c: (c, 0, 0))],
        out_shape=[jax.ShapeDtypeStruct((t, SSM_INNER), F32), jax.ShapeDtypeStruct((t, LANE), F32),
                   jax.ShapeDtypeStruct((nch, SSM_INNER, SSM_STATE), F32)],
        scratch_shapes=[pltpu.VMEM((SSM_INNER, SSM_STATE), F32)],
        compiler_params=_cparams(("arbitrary",)))(xbc_act, proj, dt_bias_p, a_log_p, dskip_t)


def _ssd_bwd(xbc_act, proj, dt_sp, hstates, dy, dt_bias_p, a_log_p, dskip_t):
    t = xbc_act.shape[0]
    nch = t // CHUNK

    pair = jnp.arange(SSM_HEADS // 2)[:, None, None]
    psel = (jnp.arange(LANE)[None, None, :] == 2 * pair + (jnp.arange(LANE) // HEAD_DIM)[None, :, None]).astype(BF16)

    def kern(xa_ref, dtr_ref, dt_ref, hs_ref, dy_ref, bias_ref, alog_ref, dsk_ref, psel_ref,
             dact_ref, ddtr_ref, da_ref, dbias_ref, ddsk_ref, dh):
        @pl.when(pl.program_id(0) == 0)
        def _():
            dh[...] = jnp.zeros(dh.shape, F32)
            for r in (da_ref, dbias_ref, ddsk_ref):
                r[...] = jnp.zeros(r.shape, F32)
        dt = dt_ref[...]
        a_neg = -jnp.exp(alog_ref[...])
        acum = _cum(_tri(True), dt * a_neg)
        acum_t = acum.T
        ea = jnp.exp(acum)
        a_last = acum[CHUNK - 1:CHUNK, :]
        dend = jnp.exp(a_last - acum)
        ea_last = jnp.exp(a_last)
        causal = _tri(True)
        lane = lax.broadcasted_iota(jnp.int32, (CHUNK, LANE), 1)
        rowi = lax.broadcasted_iota(jnp.int32, (CHUNK, LANE), 0)
        lane_lo, row_lo, last_row = lane < HEAD_DIM, rowi < HEAD_DIM, rowi == CHUNK - 1
        d_dt = jnp.zeros((CHUNK, LANE), F32)
        d_acum = jnp.zeros((CHUNK, LANE), F32)
        for g in range(2):
            bcols = slice(SSM_INNER + g * SSM_STATE, SSM_INNER + (g + 1) * SSM_STATE)
            ccols = slice(SSM_INNER + BC_DIM + g * SSM_STATE, SSM_INNER + BC_DIM + (g + 1) * SSM_STATE)
            bg, cg = xa_ref[:, bcols].astype(BF16), xa_ref[:, ccols].astype(BF16)
            cb = _dot_nt(cg, bg)
            dg_sum = jnp.zeros((CHUNK, CHUNK), F32)
            dcg = jnp.zeros((CHUNK, SSM_STATE), F32)
            dbg = jnp.zeros((CHUNK, SSM_STATE), F32)
            for j in range(4 * g, 4 * g + 4):
                h0 = 2 * j
                cols = slice(j * LANE, (j + 1) * LANE)
                xp, dyp = xa_ref[:, cols], dy_ref[:, cols]
                dtsel = _pair_sel(lane_lo, dt, h0)
                xdt = xp * dtsel
                xdt_b = xdt.astype(BF16)
                hp, dhp = hs_ref[cols, :], dh[cols, :]
                hp_b, dhp_b = hp.astype(BF16), dhp.astype(BF16)
                easel, dendsel = _pair_sel(lane_lo, ea, h0), _pair_sel(lane_lo, dend, h0)
                dx, ydiag = None, None
                for hh, sel in ((h0, lane_lo), (h0 + 1, ~lane_lo)):
                    dyh = jnp.where(sel, dyp, 0.0).astype(BF16)
                    seg = acum[:, hh:hh + 1] - acum_t[hh:hh + 1, :]
                    dec = jnp.where(causal, jnp.exp(jnp.minimum(seg, 0.0)), 0.0)
                    mm_b = (cb * dec).astype(BF16)
                    dg_sum = dg_sum + dec * _dot_nt(dyh, xdt_b)
                    d = _dot_tn(mm_b, dyh)
                    y = _dot(mm_b, jnp.where(sel, xdt, 0.0).astype(BF16))
                    dx = d if dx is None else dx + d
                    ydiag = y if ydiag is None else ydiag + y
                g2 = _dot_nt(bg, dhp_b)
                tprod = xdt * g2 * dendsel
                yoff = _dot_nt(cg, hp_b) * easel
                yc = dyp.astype(BF16).astype(F32) * ydiag + dyp * yoff - (xdt_b.astype(F32) * dx + tprod)
                dx = dx + g2 * dendsel
                psel = psel_ref[j]
                t_lo = jnp.sum(jnp.where(lane_lo, tprod, 0.0), keepdims=True).reshape(1, 1)
                t_hi = jnp.sum(tprod, keepdims=True).reshape(1, 1) - t_lo
                hh_prod = dhp * hp
                s_lo = jnp.sum(jnp.where(row_lo, hh_prod, 0.0), keepdims=True).reshape(1, 1)
                s_hi = jnp.sum(hh_prod, keepdims=True).reshape(1, 1) - s_lo
                end_lo = ea_last[:, h0:h0 + 1] * s_lo + t_lo
                end_hi = ea_last[:, h0 + 1:h0 + 2] * s_hi + t_hi
                ends = jnp.where(lane == h0, end_lo, jnp.where(lane == h0 + 1, end_hi, 0.0))
                d_acum = d_acum + _dot_split(yc, psel) + jnp.where(last_row, ends, 0.0)
                dye = (dyp * easel).astype(BF16)
                dcg = dcg + _dot(dye, hp_b)
                dbg = dbg + _dot((xdt * dendsel).astype(BF16), dhp_b)
                rowf = jnp.where(row_lo, ea_last[:, h0:h0 + 1], ea_last[:, h0 + 1:h0 + 2])
                dh[cols, :] = dhp * rowf + _dot_tn(dye, cg)
                dact_ref[:, cols] = dx * dtsel + dsk_ref[:, cols] * dyp
                d_dt = d_dt + _dot_split(dx * xp, psel)
                ddsk_ref[:, cols] += jnp.sum(dyp * xp, axis=0, keepdims=True)
            dg_b = dg_sum.astype(BF16)
            dact_ref[:, ccols] = dcg + _dot(dg_b, bg)
            dact_ref[:, bcols] = dbg + _dot_tn(dg_b, cg)
        d_adt = _cum(_tri(False), d_acum)
        d_dt = d_dt + d_adt * a_neg
        da_ref[...] += jnp.sum(d_adt * dt, axis=0, keepdims=True)
        d_raw = jnp.where(lane < SSM_HEADS, d_dt * _sigmoid(dtr_ref[...] + bias_ref[...]), 0.0)
        ddtr_ref[...] = d_raw.astype(BF16)
        dbias_ref[...] += jnp.sum(d_raw, axis=0, keepdims=True)

    rev = lambda c: (nch - 1 - c, 0)
    return pl.pallas_call(
        kern, name="ssd_bwd", grid=(nch,),
        in_specs=[pl.BlockSpec((CHUNK, XBC_DIM), rev), pl.BlockSpec((CHUNK, LANE), lambda c: (nch - 1 - c, OFF_DT // LANE)),
                  pl.BlockSpec((CHUNK, LANE), rev), pl.BlockSpec((None, SSM_INNER, SSM_STATE), lambda c: (nch - 1 - c, 0, 0)),
                  pl.BlockSpec((CHUNK, SSM_INNER), rev),
                  pl.BlockSpec((1, LANE), lambda c: (0, 0)), pl.BlockSpec((1, LANE), lambda c: (0, 0)),
                  pl.BlockSpec((1, SSM_INNER), lambda c: (0, 0)), pl.BlockSpec(psel.shape, lambda c: (0, 0, 0))],
        out_specs=[pl.BlockSpec((CHUNK, XBC_DIM), rev), pl.BlockSpec((CHUNK, LANE), rev),
                   pl.BlockSpec((1, LANE), lambda c: (0, 0)), pl.BlockSpec((1, LANE), lambda c: (0, 0)),
                   pl.BlockSpec((1, SSM_INNER), lambda c: (0, 0))],
        out_shape=[jax.ShapeDtypeStruct((t, XBC_DIM), F32), jax.ShapeDtypeStruct((t, LANE), BF16),
                   jax.ShapeDtypeStruct((1, LANE), F32), jax.ShapeDtypeStruct((1, LANE), F32),
                   jax.ShapeDtypeStruct((1, SSM_INNER), F32)],
        scratch_shapes=[pltpu.VMEM((SSM_INNER, SSM_STATE), F32)],
        compiler_params=_cparams(("arbitrary",)))(xbc_act, proj, dt_sp, hstates, dy, dt_bias_p, a_log_p, dskip_t, psel)


def _ssm_post_fwd(y, proj, g):
    def body(y_ref, z_ref, g_ref, o_ref):
        z = z_ref[...]
        yz = y_ref[...] * (z * _sigmoid(z))
        r = lax.rsqrt(jnp.mean(yz * yz, axis=-1, keepdims=True) + EPS)
        o_ref[...] = (yz * r * g_ref[...]).astype(BF16)
    return _rows("ssm_post_fwd", body, [("t", y), ("tc", proj, SSM_INNER, OFF_Z // SSM_INNER), ("p", g)],
                 [(SSM_INNER, BF16)])[0]


def _ssm_post_bwd(dmix, y, proj, g):
    def body(do_ref, y_ref, z_ref, g_ref, dy_ref, dz_ref, dg_ref):
        z, yv, dout = z_ref[...], y_ref[...], do_ref[...]
        sg = _sigmoid(z)
        gz = z * sg
        yz = yv * gz
        r = lax.rsqrt(jnp.mean(yz * yz, axis=-1, keepdims=True) + EPS)
        gd = dout * g_ref[...]
        dyz = r * gd - yz * (r * r * r * jnp.mean(yz * gd, axis=-1, keepdims=True))
        dy_ref[...] = dyz * gz
        dz_ref[...] = (dyz * yv * (sg * (1.0 + z * (1.0 - sg)))).astype(BF16)
        dg_ref[...] += jnp.sum(dout * yz * r, axis=0, keepdims=True)
    return _rows("ssm_post_bwd", body,
                 [("tc", dmix, SSM_INNER, 0), ("t", y), ("tc", proj, SSM_INNER, OFF_Z // SSM_INNER), ("p", g)],
                 [(SSM_INNER, F32), (SSM_INNER, BF16)], accs=[(1, SSM_INNER)])


def _ple_loss(gl, pp, x2, tgt):
    d = x2.shape[1]

    def body(gl_ref, pp_ref, x_ref, t_ref, dy_ref, dgl_ref, dpp_ref, sq_ref):
        s = _sigmoid(gl_ref[...])
        ppv = pp_ref[...]
        diff = x_ref[...] + s * ppv - t_ref[...]
        dy = diff * (1.0 / d)
        dy_ref[...] = dy
        dgl_ref[...] = (dy * ppv * s * (1.0 - s)).astype(BF16)
        dpp_ref[...] = (dy * s).astype(BF16)
        sq_ref[...] += jnp.sum(diff * diff, axis=0, keepdims=True)
    return _rows("ple_loss", body, [("t", gl), ("t", pp), ("t", x2), ("t", tgt)], [(d, F32), (d, BF16), (d, BF16)],
                 accs=[(1, d)])


def _pad_lanes(v, width=LANE):
    return jnp.pad(v, ((0, 0), (0, width - v.shape[1])))


def _local_step(x, p, tgt, wts):
    g_attn, g_ssm, g_ffn, g_ple = wts["attn_norm_g"], wts["ssm_norm_g"], wts["ffn_norm_g"], wts["ple_norm_g"]
    w_in_p, w_out_s, w_out_a = wts["w_in_p"], wts["w_out_ssm"], wts["w_out_attn"]
    w_up, w_down, w_gate, w_proj = wts["w_up"], wts["w_down"], wts["w_ple_gate"], wts["w_ple_proj"]
    gq_t = jnp.tile(wts["q_norm_g"], (1, ATTN_DIM // HEAD_DIM))
    gk_t = jnp.tile(wts["k_norm_g"], (1, KV_DIM // HEAD_DIM))
    dt_bias_p, a_log_p = _pad_lanes(wts["dt_bias"]), _pad_lanes(wts["a_log"])
    dskip_t = jnp.repeat(wts["d_skip"], HEAD_DIM, axis=1)

    h1 = _rms_fwd("rms_attn", x, g_attn)
    proj = _mm_nn("in_proj", [(h1, w_in_p)], F32)
    q_hm, kv_hm = _qknorm_fwd2(proj, gq_t, gk_t)
    pats = [_attn_fwd2(q_hm, kv_hm, d) for d in DILATIONS]
    attn_out, lse = _attn_merge2([o for o, _ in pats], [l for _, l in pats])
    xbc_act = _ssm_conv_fwd(proj, wts["ssm_conv_w"], wts["ssm_conv_b"])
    y_ssd, dt_sp, hstates = _ssd_fwd(xbc_act, proj, dt_bias_p, a_log_p, dskip_t)
    ssm_out = _ssm_post_fwd(y_ssd, proj, g_ssm)
    x1 = _mm_nn("out_proj", [(ssm_out, w_out_s), (attn_out, w_out_a)], F32, res=x, tm=1024)
    h2 = _rms_fwd("rms_ffn", x1, g_ffn)
    u = _mm_nn("ffn_up", [(h2, w_up)], F32, tm=1024, tn=1408, halves=True)
    a = _ffn_act_fwd(u, wts["ffn_conv_w"], wts["ffn_conv_b"])
    x2 = _mm_nn("ffn_down", [(a, w_down)], F32, res=x1, tm=1024, tn=512)
    h3 = _rms_fwd("rms_ple", x2, g_ple)
    gl = _mm_nn("ple_gate", [(h3, w_gate)], F32, tm=2048)
    pb = p.astype(BF16)
    pp = _mm_nn("ple_proj", [(pb, w_proj)], F32, tm=2048)
    dy, dgl, dpp, sq = _ple_loss(gl, pp, x2, tgt)

    grads = {}
    grads["w_ple_proj"] = _mm_tn("g_ple_proj", pb, dpp, tn=PLE_DIM, chip_cols=True)
    grads["w_ple_gate"] = _mm_tn("g_ple_gate", h3, dgl)
    dh3 = _mm_nt("d_h3", [(dgl, w_gate, 0)], F32, tm=2048)
    dx2, dx2b, grads["ple_norm_g"] = _rms_bwd("rms_ple_bwd", dh3, x2, g_ple, dy)
    da = _mm_nt("d_ffn_act", [(dx2b, w_down, 0)], F32, tm=1024, tn=1408)
    grads["w_down"] = _mm_tn("g_ffn_down", a, dx2b, tm=1408)
    du, gwg, gwv, gbg, gbv = _ffn_act_bwd(u, wts["ffn_conv_w"], wts["ffn_conv_b"], da)
    grads["ffn_conv_w"] = jnp.concatenate([gwg, gwv], axis=1)
    grads["ffn_conv_b"] = jnp.concatenate([gbg, gbv], axis=1)
    grads["w_up"] = _mm_tn("g_ffn_up", h2, du, tn=1408, chip_cols=True)
    dh2 = _mm_nt("d_h2", [(du, w_up, 0, 0), (du, w_up, 1, 1)], F32, tm=1024, tn=512)
    dx1, dx1b, grads["ffn_norm_g"] = _rms_bwd("rms_ffn_bwd", dh2, x1, g_ffn, dx2)
    dmix = _mm_nt("d_mix", [(dx1b, jnp.concatenate([w_out_s, w_out_a], axis=0), 0)], F32, tm=1024)
    grads["w_out"] = jnp.concatenate([_mm_tn("g_out_attn", attn_out, dx1b), _mm_tn("g_out_ssm", ssm_out, dx1b)], axis=0)
    dy_ssd, dz, grads["ssm_norm_g"] = _ssm_post_bwd(dmix, y_ssd, proj, g_ssm)
    dact, ddtr, d_a, d_bias, d_dsk = _ssd_bwd(xbc_act, proj, dt_sp, hstates, dy_ssd, dt_bias_p, a_log_p, dskip_t)
    grads["dt_bias"] = d_bias[:, :SSM_HEADS]
    grads["a_log"] = d_a[:, :SSM_HEADS] * (-jnp.exp(wts["a_log"]))
    grads["d_skip"] = jnp.sum(d_dsk.reshape(SSM_HEADS, HEAD_DIM), axis=1)[None, :]
    dxbc, grads["ssm_conv_w"], grads["ssm_conv_b"] = _ssm_conv_bwd(proj, wts["ssm_conv_w"], wts["ssm_conv_b"], dact)
    do_hm, dsum = _attn_bwd_prep2(dmix, attn_out)
    dqs = [_attn_dq2(q_hm, kv_hm, do_hm, lse, dsum, d) for d in DILATIONS]
    dkvs = [_attn_dkv2(q_hm, kv_hm, do_hm, lse, dsum, d) for d in DILATIONS]
    dq, dk, dv, dgq, dgk = _qknorm_bwd2(proj, gq_t, gk_t, dqs, dkvs)
    grads["q_norm_g"] = jnp.sum(dgq.reshape(ATTN_DIM // HEAD_DIM, HEAD_DIM), axis=0)[None, :]
    grads["k_norm_g"] = jnp.sum(dgk.reshape(KV_DIM // HEAD_DIM, HEAD_DIM), axis=0)[None, :]
    dproj = jnp.concatenate([dxbc, dq, dz, dk, dv, ddtr], axis=1)
    grads["w_in_p"] = _mm_tn("g_in_proj", h1, dproj, tm=512)
    dh1 = _mm_nt("d_h1", [(dproj, w_in_p, 0)], F32, tm=1024, tn=512)
    grad_x, _, grads["attn_norm_g"] = _rms_bwd("rms_attn_bwd", dh1, x, g_attn, dx1)
    return sq, grad_x, grads


MESH_IDS = pl.DeviceIdType.MESH
N_CHIPS = 4
ANY_SPEC = pl.BlockSpec(memory_space=pl.ANY)
SMALL_ROWS = 96
ALL_SMALL_ROWS = 272


def _place():
    x, y, c = lax.axis_index("x"), lax.axis_index("y"), lax.axis_index("c")
    return x, y, c, [(1 - x, y), (x, 1 - y), (1 - x, 1 - y)]


def _gather_over_chips(arrs):
    n = len(arrs)
    split = [a.shape[0] % 64 == 0 for a in arrs]

    def body(*refs):
        ins, outs = refs[:n], refs[n:2 * n]
        ici_send, ici_recv, d2d_send, d2d_recv = refs[2 * n:2 * n + 4]
        x, y, c, chips = _place()
        mine = 2 * x + y

        def part(ref, a, core):
            if not split[a]:
                return ref
            half = arrs[a].shape[0] // 2
            return ref.at[pl.ds(core * half, half)]

        def ici(a, k, src_chip_slot, core):
            px, py = chips[k]
            return pltpu.make_async_remote_copy(
                src_ref=part(ins[a], a, core), dst_ref=part(outs[a].at[src_chip_slot], a, core), send_sem=ici_send.at[3 * a + k],
                recv_sem=ici_recv.at[3 * a + k], device_id=(px, py, c), device_id_type=MESH_IDS)

        def d2d(a, k, core):
            px, py = chips[k]
            piece = part(outs[a].at[2 * px + py], a, core)
            return pltpu.make_async_remote_copy(src_ref=piece, dst_ref=piece, send_sem=d2d_send.at[3 * a + k],
                                                recv_sem=d2d_recv.at[3 * a + k], device_id=(x, y, 1 - c), device_id_type=MESH_IDS)

        for a in range(n):
            for k in range(3):
                ici(a, k, mine, c).start()
        passed = []
        for a in range(n):
            for k, (px, py) in enumerate(chips):
                ici(a, k, 2 * px + py, c).wait_recv()
                if split[a]:
                    fwd = d2d(a, k, c)
                    fwd.start()
                    passed.append(fwd)
        for a in range(n):
            for k in range(3):
                if split[a]:
                    d2d(a, k, 1 - c).wait_recv()
                ici(a, k, mine, c).wait_send()
        for fwd in passed:
            fwd.wait_send()

    sems = [pltpu.SemaphoreType.DMA((3 * n,))] * 4
    return pl.pallas_call(
        body, name="gather_weights", in_specs=[ANY_SPEC] * n, out_specs=[ANY_SPEC] * n,
        out_shape=[jax.ShapeDtypeStruct((N_CHIPS,) + a.shape, a.dtype) for a in arrs], scratch_shapes=sems)(*arrs)


def _row_tile(rows, cap=256):
    return max(d for d in range(8, cap + 1, 8) if rows % d == 0)


def _swap_halves(gs):
    n = len(gs)

    def body(*refs):
        ins, outs, send, recv = refs[:n], refs[n:2 * n], refs[2 * n], refs[2 * n + 1]
        x, y, c, _ = _place()
        cps = []
        for a in range(n):
            half = gs[a].shape[1] // 2
            for q in range(N_CHIPS):
                cps.append(pltpu.make_async_remote_copy(
                    src_ref=ins[a].at[q, pl.ds((1 - c) * half, half)], dst_ref=outs[a].at[q], send_sem=send.at[N_CHIPS * a + q],
                    recv_sem=recv.at[N_CHIPS * a + q], device_id=(x, y, 1 - c), device_id_type=MESH_IDS))
        for cp in cps:
            cp.start()
        for cp in cps:
            cp.wait()

    return pl.pallas_call(
        body, name="grad_swap_halves", in_specs=[ANY_SPEC] * n, out_specs=[ANY_SPEC] * n,
        out_shape=[jax.ShapeDtypeStruct((N_CHIPS, g.shape[1] // 2, g.shape[2]), g.dtype) for g in gs],
        scratch_shapes=[pltpu.SemaphoreType.DMA((N_CHIPS * n,))] * 2)(*gs)


def _add_halves(name, g, got, c_idx):
    rows, cols = got.shape[1:]
    tm = _row_tile(rows)
    per = rows // tm

    def kern(c_ref, g_ref, r_ref, o_ref):
        o_ref[...] = (g_ref[...] + r_ref[...]).astype(BF16)

    return pl.pallas_call(
        kern, name=name,
        grid_spec=pltpu.PrefetchScalarGridSpec(
            num_scalar_prefetch=1, grid=(N_CHIPS, per),
            in_specs=[pl.BlockSpec((None, tm, cols), lambda q, i, c_ref: (q, c_ref[0] * per + i, 0)),
                      pl.BlockSpec((None, tm, cols), lambda q, i, c_ref: (q, i, 0))],
            out_specs=pl.BlockSpec((None, tm, cols), lambda q, i, c_ref: (q, i, 0))),
        out_shape=jax.ShapeDtypeStruct((N_CHIPS, rows, cols), BF16),
        compiler_params=_cparams(("parallel", "parallel")))(c_idx, g, got)


def _scatter_over_chips(ss):
    n = len(ss)

    def body(*refs):
        ins, outs, send, recv = refs[:n], refs[n:2 * n], refs[2 * n], refs[2 * n + 1]
        x, y, c, chips = _place()
        mine = 2 * x + y
        for a in range(n):
            for k, (px, py) in enumerate(chips):
                pltpu.make_async_remote_copy(src_ref=ins[a].at[2 * px + py], dst_ref=outs[a].at[mine], send_sem=send.at[3 * a + k],
                                             recv_sem=recv.at[3 * a + k], device_id=(px, py, c), device_id_type=MESH_IDS).start()
        for a in range(n):
            for k, (px, py) in enumerate(chips):
                pltpu.make_async_remote_copy(src_ref=ins[a].at[2 * px + py], dst_ref=outs[a].at[2 * px + py], send_sem=send.at[3 * a + k],
                                             recv_sem=recv.at[3 * a + k], device_id=(px, py, c), device_id_type=MESH_IDS).wait()

    return pl.pallas_call(
        body, name="grad_scatter_chips", in_specs=[ANY_SPEC] * n, out_specs=[ANY_SPEC] * n,
        out_shape=[jax.ShapeDtypeStruct(s.shape, s.dtype) for s in ss],
        scratch_shapes=[pltpu.SemaphoreType.DMA((3 * n,))] * 2)(*ss)


def _sum_chips(name, own, parts, idx):
    rows, cols = parts.shape[1:]
    tm = _row_tile(rows)
    per = rows // tm

    def kern(o_idx, a_ref, b_ref, c_ref, d_ref, o_ref):
        o_ref[...] = ((a_ref[...].astype(F32) + b_ref[...].astype(F32)) + c_ref[...].astype(F32)) + d_ref[...].astype(F32)

    def spec(k):
        return pl.BlockSpec((None, tm, cols), functools.partial(lambda i, o_idx, k: (o_idx[k], i, 0), k=k))

    return pl.pallas_call(
        kern, name=name,
        grid_spec=pltpu.PrefetchScalarGridSpec(
            num_scalar_prefetch=1, grid=(per,), in_specs=[spec(0), spec(1), spec(2), spec(3)],
            out_specs=pl.BlockSpec((None, tm, cols), lambda i, o_idx: (0, o_idx[4] * per + i, 0))),
        out_shape=jax.ShapeDtypeStruct((1, 2 * rows, cols), F32), compiler_params=_cparams(("parallel",)))(idx, own, parts, parts, parts)


def _share_with_sibling(gs):
    n = len(gs)

    def body(*refs):
        ins, send, recv = refs[:n], refs[2 * n], refs[2 * n + 1]
        x, y, c, _ = _place()
        cps = []
        for a in range(n):
            half = gs[a].shape[1] // 2
            mine = pl.ds(c * half, half)
            cps.append(pltpu.make_async_remote_copy(src_ref=ins[a].at[0, mine], dst_ref=refs[n + a].at[0, mine], send_sem=send.at[a],
                                                    recv_sem=recv.at[a], device_id=(x, y, 1 - c), device_id_type=MESH_IDS))
        for cp in cps:
            cp.start()
        for cp in cps:
            cp.wait()

    return pl.pallas_call(
        body, name="grad_share_sibling", in_specs=[ANY_SPEC] * n, out_specs=[ANY_SPEC] * n,
        out_shape=[jax.ShapeDtypeStruct(g.shape, g.dtype) for g in gs], input_output_aliases={a: a for a in range(n)},
        scratch_shapes=[pltpu.SemaphoreType.DMA((n,))] * 2)(*gs)


def _allreduce_small(v):
    def body(v_ref, o_ref, land, send, recv):
        x, y, c, _ = _place()
        me = 4 * x + 2 * y + c
        land[me] = v_ref[...]
        cps = []
        for rel in range(1, 8):
            bx, by, bc = (rel >> 2) & 1, (rel >> 1) & 1, rel & 1
            peer = (1 - x if bx else x, 1 - y if by else y, 1 - c if bc else c)
            cps.append(pltpu.make_async_remote_copy(src_ref=v_ref, dst_ref=land.at[me], send_sem=send.at[rel - 1],
                                                    recv_sem=recv.at[rel - 1], device_id=peer, device_id_type=MESH_IDS))
        for cp in cps:
            cp.start()
        for cp in cps:
            cp.wait()
        acc = land[0]
        for d in range(1, 8):
            acc = acc + land[d]
        o_ref[...] = acc

    vm = pl.BlockSpec(memory_space=pltpu.VMEM)
    return pl.pallas_call(
        body, name="allreduce_small", in_specs=[vm], out_specs=vm, out_shape=jax.ShapeDtypeStruct(v.shape, F32),
        scratch_shapes=[pltpu.VMEM((8,) + v.shape, F32), pltpu.SemaphoreType.DMA((7,)), pltpu.SemaphoreType.DMA((7,))])(v)


def _adamw(name, w, g, m, v):
    _, rows, cols = w.shape
    tm = rows if rows * cols <= 128 * 1024 else _row_tile(rows)
    c1 = 1.0 / (1.0 - ADAM_B1 ** ADAM_STEP)
    c2 = 1.0 / (1.0 - ADAM_B2 ** ADAM_STEP)

    def kern(w_ref, g_ref, m_ref, v_ref, d_ref, mo_ref, vo_ref):
        gv = g_ref[...]
        mn = ADAM_B1 * m_ref[...] + (1.0 - ADAM_B1) * gv
        vn = ADAM_B2 * v_ref[...] + (1.0 - ADAM_B2) * (gv * gv)
        d_ref[...] = -ADAM_LR * ((mn * c1) / (jnp.sqrt(vn * c2) + ADAM_EPS) + ADAM_WD * w_ref[...])
        mo_ref[...] = mn
        vo_ref[...] = vn

    spec = pl.BlockSpec((None, tm, cols), lambda i: (0, i, 0))
    return pl.pallas_call(
        kern, name=name, grid=(rows // tm,), in_specs=[spec] * 4, out_specs=[spec] * 3,
        out_shape=[jax.ShapeDtypeStruct(w.shape, F32)] * 3, compiler_params=_cparams(("parallel",)))(w, g, m, v)


SHARDED = (("w_in", 1), ("w_out", 0), ("w_up", 1), ("w_down", 0), ("w_ple_gate", 0), ("w_ple_proj", 1),
           ("ssm_conv_w", 1), ("ffn_conv_w", 1))
MATRICES = ("w_in", "w_out", "w_up", "w_down", "w_ple_gate", "w_ple_proj")
REPLICATED = ("attn_norm_g", "q_norm_g", "k_norm_g", "ssm_conv_b", "dt_bias", "a_log", "d_skip", "ssm_norm_g",
              "ffn_norm_g", "ffn_conv_b", "ple_norm_g")
WEIGHT_ORDER = ("attn_norm_g", "w_in", "q_norm_g", "k_norm_g", "ssm_conv_w", "ssm_conv_b", "dt_bias", "a_log", "d_skip",
                "ssm_norm_g", "w_out", "ffn_norm_g", "w_up", "ffn_conv_w", "ffn_conv_b", "w_down", "ple_norm_g",
                "w_ple_gate", "w_ple_proj")


def _join_chips(g, axis):
    if axis == 0:
        return g.reshape(g.shape[0] * g.shape[1], g.shape[2])
    return jnp.transpose(g, (1, 0, 2)).reshape(g.shape[1], g.shape[0] * g.shape[2])


def _split_chips(g, axis):
    if axis == 0:
        return g.reshape(N_CHIPS, g.shape[0] // N_CHIPS, g.shape[1])
    r, c = g.shape
    return jnp.transpose(g.reshape(r, N_CHIPS, c // N_CHIPS), (1, 0, 2))


def _pack_small(vals, rows=SMALL_ROWS):
    flat = jnp.concatenate([v.reshape(-1) for v in vals])
    return jnp.pad(flat, (0, rows * LANE - flat.shape[0])).reshape(rows, LANE)


def _unpack_small(packed, like):
    flat, out, off = packed.reshape(-1), [], 0
    for v in like:
        out.append(flat[off:off + v.size].reshape(v.shape))
        off += v.size
    return out


def kernel(x, p, attn_norm_g, w_in, q_norm_g, k_norm_g, ssm_conv_w, ssm_conv_b, dt_bias, a_log, d_skip, ssm_norm_g, w_out, ffn_norm_g, w_up, ffn_conv_w, ffn_conv_b, w_down, ple_norm_g, w_ple_gate, w_ple_proj, loss_target, m_attn_norm_g, m_w_in, m_q_norm_g, m_k_norm_g, m_ssm_conv_w, m_ssm_conv_b, m_dt_bias, m_a_log, m_d_skip, m_ssm_norm_g, m_w_out, m_ffn_norm_g, m_w_up, m_ffn_conv_w, m_ffn_conv_b, m_w_down, m_ple_norm_g, m_w_ple_gate, m_w_ple_proj, v_attn_norm_g, v_w_in, v_q_norm_g, v_k_norm_g, v_ssm_conv_w, v_ssm_conv_b, v_dt_bias, v_a_log, v_d_skip, v_ssm_norm_g, v_w_out, v_ffn_norm_g, v_w_up, v_ffn_conv_w, v_ffn_conv_b, v_w_down, v_ple_norm_g, v_w_ple_gate, v_w_ple_proj):
    given = dict(locals())
    w2 = {n: given[n].reshape(given[n].shape[-2:]) if given[n].ndim == 3 else given[n] for n in WEIGHT_ORDER}

    cx, cy, cc = lax.axis_index("x"), lax.axis_index("y"), lax.axis_index("c")
    chip = 2 * cx + cy
    shards = [w2[n].astype(BF16) if n in MATRICES else w2[n] for n, _ in SHARDED]
    full = {n: _join_chips(lax.dynamic_update_index_in_dim(g, s, chip, 0), ax)
            for (n, ax), g, s in zip(SHARDED, _gather_over_chips(shards), shards)}
    win = full["w_in"]
    w_in_p = jnp.concatenate([win[:, 2048:3584], win[:, 0:512], win[:, 1024:2048], win[:, 512:768], win[:, 768:1024],
                              win[:, 3584:3600], jnp.zeros((D_MODEL, PROJ_P - IN_PROJ), BF16)], axis=1)
    wts = {n: w2[n] for n in REPLICATED}
    wts.update(w_in_p=w_in_p, w_out_attn=full["w_out"][:ATTN_DIM], w_out_ssm=full["w_out"][ATTN_DIM:], w_up=full["w_up"],
               w_down=full["w_down"], w_ple_gate=full["w_ple_gate"], w_ple_proj=full["w_ple_proj"],
               ssm_conv_w=full["ssm_conv_w"], ffn_conv_w=full["ffn_conv_w"])

    sq, grad_x, grads = _local_step(x[0], p[0, 0], loss_target[0], wts)
    gi = grads.pop("w_in_p")
    grads["w_in"] = jnp.concatenate([gi[:, OFF_Q:OFF_Q + ATTN_DIM], gi[:, OFF_K:OFF_K + KV_DIM], gi[:, OFF_V:OFF_V + KV_DIM],
                                     gi[:, OFF_Z:OFF_Z + SSM_INNER], gi[:, OFF_XBC:OFF_XBC + XBC_DIM], gi[:, OFF_DT:OFF_DT + SSM_HEADS]],
                                    axis=1)

    chip_major = [grads[n] if grads[n].ndim == 3 else _split_chips(grads[n], ax) for n, ax in SHARDED if n in MATRICES]
    core = cc.astype(jnp.int32).reshape(1)
    chip_sums = [_add_halves("grad_add_halves_" + n, g, got, core) for n, g, got in zip(MATRICES, chip_major, _swap_halves(chip_major))]
    idx = jnp.stack([chip, 2 * (1 - cx) + cy, 2 * cx + (1 - cy), 2 * (1 - cx) + (1 - cy), cc]).astype(jnp.int32)
    halves = [_sum_chips("grad_sum_chips_" + n, s, got, idx) for n, s, got in zip(MATRICES, chip_sums, _scatter_over_chips(chip_sums))]
    g_shard = dict(zip(MATRICES, _share_with_sibling(halves)))

    small_names = REPLICATED + ("ssm_conv_w", "ffn_conv_w")
    small_like = [grads[n] for n in small_names] + [jnp.zeros((1,), F32)]
    small = _allreduce_small(_pack_small([grads[n] for n in small_names] + [jnp.sum(sq).reshape(1)], ALL_SMALL_ROWS))
    small_vals = dict(zip(small_names + ("loss",), _unpack_small(small, small_like)))
    loss = (0.5 / D_MODEL) * small_vals["loss"][0]
    for n in ("ssm_conv_w", "ffn_conv_w"):
        cols = w2[n].shape[1]
        g_shard[n] = lax.dynamic_slice_in_dim(small_vals[n], chip * cols, cols, axis=1)[None]

    delta, new_m, new_v = {}, {}, {}
    for n, _ in SHARDED:
        delta[n], new_m[n], new_v[n] = _adamw("adamw_" + n, given[n], g_shard[n], given["m_" + n], given["v_" + n])
    packed = lambda prefix: _pack_small([given[prefix + n] for n in REPLICATED])[None]
    sm = _adamw("adamw_small", packed(""), _pack_small([small_vals[n] for n in REPLICATED])[None], packed("m_"), packed("v_"))
    for n in REPLICATED:
        g_shard[n] = small_vals[n]
    for dst, packed_out in zip((delta, new_m, new_v), sm):
        for n, val in zip(REPLICATED, _unpack_small(packed_out[0], [w2[n] for n in REPLICATED])):
            dst[n] = val

    def shaped(d):
        return [d[n].reshape(given[n].shape) for n in WEIGHT_ORDER]
    return (loss, grad_x[None], *shaped(g_shard), *shaped(delta), *shaped(new_m), *shaped(new_v))
```

```python
import functools

import jax
import jax.numpy as jnp
from jax import lax
from jax.experimental import pallas as pl
from jax.experimental.pallas import tpu as pltpu

F32 = jnp.float32
BF16 = jnp.bfloat16

D_MODEL = 1024
HEAD_DIM = 64
ATTN_DIM = 512
KV_DIM = 256
N_KV = 4
SSM_INNER = 1024
SSM_HEADS = 16
SSM_STATE = 128
BC_DIM = 256
XBC_DIM = SSM_INNER + 2 * BC_DIM
MIX_DIM = ATTN_DIM + SSM_INNER
IN_PROJ = 3600
D_FF = 2816
PLE_DIM = 256
CHUNK = 128
DILATIONS = (1, 4, 16)
EPS = 1e-6
ADAM_LR, ADAM_B1, ADAM_B2, ADAM_EPS, ADAM_WD, ADAM_STEP = 0.001, 0.9, 0.999, 1e-08, 0.01, 10

PROJ_P = 3712
OFF_XBC, OFF_Q, OFF_Z, OFF_K, OFF_V, OFF_DT = 0, 1536, 2048, 3072, 3328, 3584
LANE = 128
VMEM_LIMIT = 48 * 1024 * 1024
NEG = -1e30


def _cparams(sem):
    return pltpu.CompilerParams(dimension_semantics=sem, vmem_limit_bytes=VMEM_LIMIT)


def _sigmoid(x):
    return 1.0 / (1.0 + jnp.exp(-x))


def _dot(a, b):
    return jnp.dot(a, b, preferred_element_type=F32)


def _dot_nt(a, b):
    return lax.dot_general(a, b, (((1,), (1,)), ((), ())), preferred_element_type=F32)


def _dot_tn(a, b):
    return lax.dot_general(a, b, (((0,), (0,)), ((), ())), preferred_element_type=F32)


def _dot_split(x, m):
    hi = x.astype(BF16)
    lo = (x - hi.astype(F32)).astype(BF16)
    return _dot(hi, m) + _dot(lo, m)


def _rows(name, body, ins, outs, accs=(), tm=512):
    t_rows = next(s[1].shape[0] for s in ins if s[0] in ("t", "tc"))
    tm = min(tm, t_rows)
    in_specs, args = [], []
    for s in ins:
        if s[0] == "t":
            in_specs.append(pl.BlockSpec((tm, s[1].shape[1]), lambda i: (i, 0)))
        elif s[0] == "tc":
            in_specs.append(pl.BlockSpec((tm, s[2]), functools.partial(lambda i, c: (i, c), c=s[3])))
        else:
            in_specs.append(pl.BlockSpec(s[1].shape, lambda i: (0, 0)))
        args.append(s[1])
    out_shape = [jax.ShapeDtypeStruct((t_rows, w), dt) for w, dt in outs]
    out_specs = [pl.BlockSpec((tm, w), lambda i: (i, 0)) for w, _ in outs]
    out_shape += [jax.ShapeDtypeStruct(a, F32) for a in accs]
    out_specs += [pl.BlockSpec(a, lambda i: (0, 0)) for a in accs]
    n_acc = len(accs)

    def kern(*refs):
        if n_acc:
            @pl.when(pl.program_id(0) == 0)
            def _():
                for r in refs[len(refs) - n_acc:]:
                    r[...] = jnp.zeros(r.shape, F32)
        body(*refs)

    return pl.pallas_call(
        kern, name=name, grid=(t_rows // tm,), in_specs=in_specs, out_specs=out_specs, out_shape=out_shape,
        compiler_params=_cparams(("arbitrary",) if n_acc else ("parallel",)))(*args)


NCHUNK = 512


def _col_chunks(n):
    return [(c, min(NCHUNK, n - c)) for c in range(0, n, NCHUNK)]


def _mm_nn(name, pairs, out_dtype, res=None, tm=512, tn=None, halves=False):
    m, n = pairs[0][0].shape[0], pairs[0][1].shape[1]
    tn = n if tn is None else tn
    tm = min(tm, m)
    np_ = len(pairs)
    if halves:
        per = n // 2 // tn
        out_spec = pl.BlockSpec((None, tm, tn), lambda j, i: (j // per, i, j % per))
        out_shape = jax.ShapeDtypeStruct((2, m, n // 2), out_dtype)
    else:
        out_spec = pl.BlockSpec((tm, tn), lambda j, i: (i, j))
        out_shape = jax.ShapeDtypeStruct((m, n), out_dtype)
    in_specs, args = [], []
    for a, w in pairs:
        in_specs += [pl.BlockSpec((tm, a.shape[1]), lambda j, i: (i, 0)), pl.BlockSpec((w.shape[0], tn), lambda j, i: (0, j))]
        args += [a, w]
    if res is not None:
        in_specs.append(pl.BlockSpec((tm, tn), lambda j, i: (i, j)))
        args.append(res)

    def kern(*refs):
        o_ref = refs[-1]
        for c0, cw in _col_chunks(tn):
            acc = None
            for q in range(np_):
                d = _dot(refs[2 * q][...], refs[2 * q + 1][:, c0:c0 + cw])
                acc = d if acc is None else acc + d
            if res is not None:
                acc = acc + refs[2 * np_][:, c0:c0 + cw]
            o_ref[:, c0:c0 + cw] = acc.astype(o_ref.dtype)

    return pl.pallas_call(
        kern, name=name, grid=(n // tn, m // tm), in_specs=in_specs, out_specs=out_spec, out_shape=out_shape,
        compiler_params=_cparams(("parallel", "parallel")))(*args)


def _mm_nt(name, pairs, out_dtype, tm=512, tn=None):
    m, n = pairs[0][0].shape[-2], pairs[0][1].shape[0]
    tn = n if tn is None else tn
    tm = min(tm, m)
    np_ = len(pairs)
    in_specs, args = [], []
    for a, w, kb, *lead in pairs:
        if lead:
            in_specs.append(pl.BlockSpec((None, tm, a.shape[2]), functools.partial(lambda j, i, ld: (ld, i, 0), ld=lead[0])))
        else:
            in_specs.append(pl.BlockSpec((tm, a.shape[1]), lambda j, i: (i, 0)))
        in_specs.append(pl.BlockSpec((tn, a.shape[-1]), functools.partial(lambda j, i, kb: (j, kb), kb=kb)))
        args += [a, w]

    def kern(*refs):
        o_ref = refs[-1]
        for c0, cw in _col_chunks(tn):
            acc = None
            for q in range(np_):
                d = _dot_nt(refs[2 * q][...], refs[2 * q + 1][c0:c0 + cw, :])
                acc = d if acc is None else acc + d
            o_ref[:, c0:c0 + cw] = acc.astype(o_ref.dtype)

    return pl.pallas_call(
        kern, name=name, grid=(n // tn, m // tm), in_specs=in_specs,
        out_specs=pl.BlockSpec((tm, tn), lambda j, i: (i, j)),
        out_shape=jax.ShapeDtypeStruct((m, n), out_dtype), compiler_params=_cparams(("parallel", "parallel")))(*args)


def _mm_tn(name, a, b, tm=None, tn=None, tk=1024, chip_cols=False):
    t, m = a.shape
    n = b.shape[-1] * (2 if b.ndim == 3 else 1)
    tm = m if tm is None else tm
    tn = n if tn is None else tn
    tk = min(tk, t)
    if b.ndim == 3:
        per = n // 2 // tn
        b_spec = pl.BlockSpec((None, tk, tn), lambda i, j, k: (j // per, k, j % per))
    else:
        b_spec = pl.BlockSpec((tk, tn), lambda i, j, k: (k, j))
    if chip_cols:
        out_spec = pl.BlockSpec((None, tm, tn), lambda i, j, k: (j, i, 0))
        out_shape = jax.ShapeDtypeStruct((n // tn, m, tn), F32)
    else:
        out_spec = pl.BlockSpec((tm, tn), lambda i, j, k: (i, j))
        out_shape = jax.ShapeDtypeStruct((m, n), F32)

    def kern(a_ref, b_ref, o_ref):
        @pl.when(pl.program_id(2) == 0)
        def _():
            o_ref[...] = jnp.zeros(o_ref.shape, F32)
        for c0, cw in _col_chunks(tn):
            o_ref[:, c0:c0 + cw] += _dot_tn(a_ref[...], b_ref[:, c0:c0 + cw])

    return pl.pallas_call(
        kern, name=name, grid=(m // tm, n // tn, t // tk),
        in_specs=[pl.BlockSpec((tk, tm), lambda i, j, k: (k, i)), b_spec], out_specs=out_spec, out_shape=out_shape,
        compiler_params=_cparams(("parallel", "parallel", "arbitrary")))(a, b)


def _rms_fwd(name, x, g):
    def body(x_ref, g_ref, h_ref):
        xv = x_ref[...]
        r = lax.rsqrt(jnp.mean(xv * xv, axis=-1, keepdims=True) + EPS)
        h_ref[...] = (xv * r * g_ref[...]).astype(BF16)
    return _rows(name, body, [("t", x), ("p", g)], [(x.shape[1], BF16)])[0]


def _rms_bwd(name, dh, x, g, dres):
    d = x.shape[1]

    def body(dh_ref, x_ref, g_ref, dres_ref, dx_ref, dxb_ref, dg_ref):
        xv, dhv = x_ref[...], dh_ref[...]
        r = lax.rsqrt(jnp.mean(xv * xv, axis=-1, keepdims=True) + EPS)
        gd = dhv * g_ref[...]
        dx = dres_ref[...] + r * gd - xv * (r * r * r * jnp.mean(xv * gd, axis=-1, keepdims=True))
        dx_ref[...] = dx
        dxb_ref[...] = dx.astype(BF16)
        dg_ref[...] += jnp.sum(dhv * xv * r, axis=0, keepdims=True)
    return _rows(name, body, [("t", dh), ("t", x), ("p", g), ("t", dres)], [(d, F32), (d, BF16)], accs=[(1, d)])


def _head_mean_matrix(width):
    i = jnp.arange(width) // HEAD_DIM
    return jnp.where(i[:, None] == i[None, :], 1.0 / HEAD_DIM, 0.0).astype(BF16)


ATT_SPAN = 2048
N_QH = 8


def _lane_lo(rows):
    return lax.broadcasted_iota(jnp.int32, (rows, LANE), 1) < HEAD_DIM


def _swap_halves_lanes(x):
    return pltpu.roll(x, HEAD_DIM, axis=1)


def _qknorm_fwd2(proj, gq_t, gk_t, tm=256):
    t = proj.shape[0]
    bq, bk = _head_mean_matrix(ATTN_DIM), _head_mean_matrix(KV_DIM)
    scale = HEAD_DIM ** -0.5

    def kern(q_ref, k_ref, v_ref, gq_ref, gk_ref, bq_ref, bk_ref, qo_ref, kvo_ref):
        q, k, v = q_ref[...], k_ref[...], v_ref[...]
        qn = (q * lax.rsqrt(_dot_split(q * q, bq_ref[...]) + EPS) * gq_ref[...]) * scale
        kn = k * lax.rsqrt(_dot_split(k * k, bk_ref[...]) + EPS) * gk_ref[...]
        lo = _lane_lo(tm)
        for j in range(N_KV):
            blk = qn[:, j * LANE:(j + 1) * LANE]
            qo_ref[2 * j] = jnp.where(lo, blk, 0.0)
            qo_ref[2 * j + 1] = jnp.where(lo, _swap_halves_lanes(blk), 0.0)
        for j in range(2):
            kb, vb = kn[:, j * LANE:(j + 1) * LANE], v[:, j * LANE:(j + 1) * LANE]
            kvo_ref[2 * j] = jnp.where(lo, kb, _swap_halves_lanes(vb))
            kvo_ref[2 * j + 1] = jnp.where(lo, _swap_halves_lanes(kb), vb)

    col = lambda w, idx: pl.BlockSpec((tm, w), functools.partial(lambda i, idx: (i, idx), idx=idx))
    par = lambda a: pl.BlockSpec(a.shape, lambda i: (0, 0))
    return pl.pallas_call(
        kern, name="qknorm_fwd", grid=(t // tm,),
        in_specs=[col(ATTN_DIM, OFF_Q // ATTN_DIM), col(KV_DIM, OFF_K // KV_DIM), col(KV_DIM, OFF_V // KV_DIM),
                  par(gq_t), par(gk_t), par(bq), par(bk)],
        out_specs=[pl.BlockSpec((N_QH, tm, LANE), lambda i: (0, i, 0)), pl.BlockSpec((N_KV, tm, LANE), lambda i: (0, i, 0))],
        out_shape=[jax.ShapeDtypeStruct((N_QH, t, LANE), F32), jax.ShapeDtypeStruct((N_KV, t, LANE), F32)],
        compiler_params=_cparams(("parallel",)))(proj, proj, proj, gq_t, gk_t, bq, bk)


def _qknorm_bwd2(proj, gq_t, gk_t, dqs, dkvs, tm=256):
    t = proj.shape[0]
    bq, bk = _head_mean_matrix(ATTN_DIM), _head_mean_matrix(KV_DIM)
    scale = HEAD_DIM ** -0.5

    def kern(q_ref, k_ref, gq_ref, gk_ref, bq_ref, bk_ref, a1, a2, a3, b1, b2, b3, dq_ref, dk_ref, dv_ref, dgq_ref, dgk_ref):
        @pl.when(pl.program_id(0) == 0)
        def _():
            dgq_ref[...] = jnp.zeros(dgq_ref.shape, F32)
            dgk_ref[...] = jnp.zeros(dgk_ref.shape, F32)
        lo = _lane_lo(tm)
        sq = [a1[h] + a2[h] + a3[h] for h in range(N_QH)]
        skv = [b1[h] + b2[h] + b3[h] for h in range(N_KV)]
        dqn = jnp.concatenate([jnp.where(lo, sq[2 * j], _swap_halves_lanes(sq[2 * j + 1])) for j in range(N_KV)], axis=1) * scale
        dkn = jnp.concatenate([jnp.where(lo, skv[2 * j], _swap_halves_lanes(skv[2 * j + 1])) for j in range(2)], axis=1)
        dv = jnp.concatenate([jnp.where(lo, _swap_halves_lanes(skv[2 * j]), skv[2 * j + 1]) for j in range(2)], axis=1)
        q, k = q_ref[...], k_ref[...]
        rq = lax.rsqrt(_dot_split(q * q, bq_ref[...]) + EPS)
        rk = lax.rsqrt(_dot_split(k * k, bk_ref[...]) + EPS)
        gdq, gdk = dqn * gq_ref[...], dkn * gk_ref[...]
        dq_ref[...] = (rq * gdq - q * (rq * rq * rq * _dot_split(q * gdq, bq_ref[...]))).astype(BF16)
        dk_ref[...] = (rk * gdk - k * (rk * rk * rk * _dot_split(k * gdk, bk_ref[...]))).astype(BF16)
        dv_ref[...] = dv.astype(BF16)
        dgq_ref[...] += jnp.sum(dqn * q * rq, axis=0, keepdims=True)
        dgk_ref[...] += jnp.sum(dkn * k * rk, axis=0, keepdims=True)

    col = lambda w, idx: pl.BlockSpec((tm, w), functools.partial(lambda i, idx: (i, idx), idx=idx))
    par = lambda a: pl.BlockSpec(a.shape, lambda i: (0, 0))
    blk = lambda n: pl.BlockSpec((n, tm, LANE), lambda i: (0, i, 0))
    row = lambda w: pl.BlockSpec((tm, w), lambda i: (i, 0))
    acc = lambda w: pl.BlockSpec((1, w), lambda i: (0, 0))
    return pl.pallas_call(
        kern, name="qknorm_bwd", grid=(t // tm,),
        in_specs=[col(ATTN_DIM, OFF_Q // ATTN_DIM), col(KV_DIM, OFF_K // KV_DIM), par(gq_t), par(gk_t), par(bq), par(bk)]
        + [blk(N_QH)] * 3 + [blk(N_KV)] * 3,
        out_specs=[row(ATTN_DIM), row(KV_DIM), row(KV_DIM), acc(ATTN_DIM), acc(KV_DIM)],
        out_shape=[jax.ShapeDtypeStruct((t, ATTN_DIM), BF16), jax.ShapeDtypeStruct((t, KV_DIM), BF16),
                   jax.ShapeDtypeStruct((t, KV_DIM), BF16), jax.ShapeDtypeStruct((1, ATTN_DIM), F32),
                   jax.ShapeDtypeStruct((1, KV_DIM), F32)],
        compiler_params=_cparams(("arbitrary",)))(proj, proj, gq_t, gk_t, bq, bk, *dqs, *dkvs)


def _att_rows(b, r, dil):
    if dil == 1:
        return pl.ds(b * CHUNK, CHUNK)
    return pl.ds(b * CHUNK * dil + r, CHUNK, stride=dil)


def _for_residues(dil, unit):
    for r in range(dil):
        unit(r, 0)


def _band_qk(first):
    ri = lax.broadcasted_iota(jnp.int32, (CHUNK, 2 * CHUNK), 0)
    cj = lax.broadcasted_iota(jnp.int32, (CHUNK, 2 * CHUNK), 1)
    band = (cj - ri >= 0) & (cj - ri <= CHUNK)
    return band if first is None else band & (jnp.logical_not(first) | (cj >= CHUNK))


def _band_kq(last):
    rj = lax.broadcasted_iota(jnp.int32, (CHUNK, 2 * CHUNK), 0)
    ci = lax.broadcasted_iota(jnp.int32, (CHUNK, 2 * CHUNK), 1)
    band = (ci - rj >= 0) & (ci - rj <= CHUNK)
    return band if last is None else band & (jnp.logical_not(last) | (ci < CHUNK))


def _att_specs(t, dil):
    sub = CHUNK * dil
    nb, last = ATT_SPAN // sub, t // sub - 1
    cur = lambda heads: pl.BlockSpec((heads, ATT_SPAN, LANE), lambda kh, n: (kh, n, 0))
    prev = lambda heads: pl.BlockSpec((heads, sub, LANE), lambda kh, n: (kh, jnp.maximum(n * nb - 1, 0), 0))
    nxt = lambda heads: pl.BlockSpec((heads, sub, LANE), lambda kh, n: (kh, jnp.minimum((n + 1) * nb, last), 0))
    return sub, nb, cur, prev, nxt


def _attn_fwd2(q, kv, dil):
    t = q.shape[1]
    sub, nb, cur, prev, _ = _att_specs(t, dil)

    def kern(q_ref, kvp_ref, kvc_ref, o_ref, lse_ref):
        n = pl.program_id(1)
        lane = lax.broadcasted_iota(jnp.int32, (CHUNK, LANE), 1)
        for b in range(nb):
            mask = _band_qk((n == 0) if b == 0 else None)

            def unit(r, carry, b=b, mask=mask):
                rows = _att_rows(b, r, dil)
                kvp = kvc_ref[_att_rows(b - 1, r, dil), :] if b > 0 else kvp_ref[_att_rows(0, r, dil), :]
                kvcat = jnp.concatenate([kvp, kvc_ref[rows, :]], axis=0).astype(BF16)
                lse_tile = jnp.zeros((CHUNK, LANE), F32)
                for g in range(2):
                    s = jnp.where(mask, _dot_nt(q_ref.at[g][rows, :].astype(BF16), kvcat), NEG)
                    m = jnp.max(s, axis=1, keepdims=True)
                    p = jnp.exp(s - m)
                    l = jnp.sum(p, axis=1, keepdims=True)
                    o_ref.at[g][rows, :] = _dot(p.astype(BF16), kvcat) * (1.0 / l)
                    lse_tile = jnp.where(lane == g, m + jnp.log(l), lse_tile)
                lse_ref[rows, :] = lse_tile
                return carry
            _for_residues(dil, unit)

    return pl.pallas_call(
        kern, name=f"attn_fwd_d{dil}", grid=(N_KV, t // ATT_SPAN), in_specs=[cur(2), prev(None), cur(None)],
        out_specs=[cur(2), cur(None)],
        out_shape=[jax.ShapeDtypeStruct((N_QH, t, LANE), F32), jax.ShapeDtypeStruct((N_KV, t, LANE), F32)],
        compiler_params=_cparams(("parallel", "parallel")))(q, kv, kv)


def _attn_merge2(os_, lses, tm=256):
    t = os_[0].shape[1]

    def kern(o1, o2, o3, l1, l2, l3, out_ref, lse_ref):
        pieces = []
        for kh in range(N_KV):
            a, b, c = l1[kh], l2[kh], l3[kh]
            m = jnp.maximum(jnp.maximum(a, b), c)
            tot = m + jnp.log(jnp.exp(a - m) + jnp.exp(b - m) + jnp.exp(c - m))
            lse_ref[kh] = tot
            wa, wb, wc = jnp.exp(a - tot), jnp.exp(b - tot), jnp.exp(c - tot)
            for g in range(2):
                h = 2 * kh + g
                acc = wa[:, g:g + 1] * o1[h] + wb[:, g:g + 1] * o2[h] + wc[:, g:g + 1] * o3[h]
                pieces.append(acc[:, HEAD_DIM:])
        out_ref[...] = jnp.concatenate(pieces, axis=1).astype(BF16)

    blk = lambda n: pl.BlockSpec((n, tm, LANE), lambda i: (0, i, 0))
    return pl.pallas_call(
        kern, name="attn_merge", grid=(t // tm,), in_specs=[blk(N_QH)] * 3 + [blk(N_KV)] * 3,
        out_specs=[pl.BlockSpec((tm, ATTN_DIM), lambda i: (i, 0)), blk(N_KV)],
        out_shape=[jax.ShapeDtypeStruct((t, ATTN_DIM), BF16), jax.ShapeDtypeStruct((N_KV, t, LANE), F32)],
        compiler_params=_cparams(("parallel",)))(*os_, *lses)


def _attn_bwd_prep2(dmix, attn_out, tm=256):
    t = attn_out.shape[0]

    def kern(do_ref, o_ref, dot_ref, d_ref):
        do = do_ref[...]
        prod = do * o_ref[...].astype(F32)
        lo = _lane_lo(tm)
        lane = lax.broadcasted_iota(jnp.int32, (tm, LANE), 1)
        for kh in range(N_KV):
            blk, pb = do[:, kh * LANE:(kh + 1) * LANE], prod[:, kh * LANE:(kh + 1) * LANE]
            dot_ref[2 * kh] = jnp.where(lo, 0.0, _swap_halves_lanes(blk))
            dot_ref[2 * kh + 1] = jnp.where(lo, 0.0, blk)
            s_lo = jnp.sum(jnp.where(lo, pb, 0.0), axis=1, keepdims=True)
            s_hi = jnp.sum(pb, axis=1, keepdims=True) - s_lo
            d_ref[kh] = jnp.where(lane == 0, s_lo, jnp.where(lane == 1, s_hi, 0.0))

    blk = lambda n: pl.BlockSpec((n, tm, LANE), lambda i: (0, i, 0))
    return pl.pallas_call(
        kern, name="attn_bwd_prep", grid=(t // tm,),
        in_specs=[pl.BlockSpec((tm, ATTN_DIM), lambda i: (i, SSM_INNER // ATTN_DIM)), pl.BlockSpec((tm, ATTN_DIM), lambda i: (i, 0))],
        out_specs=[blk(N_QH), blk(N_KV)],
        out_shape=[jax.ShapeDtypeStruct((N_QH, t, LANE), F32), jax.ShapeDtypeStruct((N_KV, t, LANE), F32)],
        compiler_params=_cparams(("parallel",)))(dmix, attn_out)


def _attn_dq2(q, kv, dot, lse, dsum, dil):
    t = q.shape[1]
    sub, nb, cur, prev, _ = _att_specs(t, dil)

    def kern(q_ref, kvp_ref, kvc_ref, do_ref, lse_ref, d_ref, dq_ref):
        n = pl.program_id(1)
        for b in range(nb):
            mask = _band_qk((n == 0) if b == 0 else None)

            def unit(r, carry, b=b, mask=mask):
                rows = _att_rows(b, r, dil)
                kvp = kvc_ref[_att_rows(b - 1, r, dil), :] if b > 0 else kvp_ref[_att_rows(0, r, dil), :]
                kvcat = jnp.concatenate([kvp, kvc_ref[rows, :]], axis=0).astype(BF16)
                lse_t, d_t = lse_ref[rows, :], d_ref[rows, :]
                for g in range(2):
                    s = jnp.where(mask, _dot_nt(q_ref.at[g][rows, :].astype(BF16), kvcat), NEG)
                    p = jnp.exp(s - lse_t[:, g:g + 1])
                    dp = _dot_nt(do_ref.at[g][rows, :].astype(BF16), kvcat)
                    ds = p * (dp - d_t[:, g:g + 1])
                    dq_ref.at[g][rows, :] = _dot(ds.astype(BF16), kvcat)
                return carry
            _for_residues(dil, unit)

    return pl.pallas_call(
        kern, name=f"attn_dq_d{dil}", grid=(N_KV, t // ATT_SPAN),
        in_specs=[cur(2), prev(None), cur(None), cur(2), cur(None), cur(None)], out_specs=cur(2),
        out_shape=jax.ShapeDtypeStruct((N_QH, t, LANE), F32),
        compiler_params=_cparams(("parallel", "parallel")))(q, kv, kv, dot, lse, dsum)


def _attn_dkv2(q, kv, dot, lse, dsum, dil):
    t = q.shape[1]
    sub, nb, cur, _, nxt = _att_specs(t, dil)
    nsteps = t // ATT_SPAN

    def kern(kv_ref, qc_ref, qn_ref, doc_ref, don_ref, lc_ref, ln_ref, dc_ref, dn_ref, dkv_ref):
        n = pl.program_id(1)
        for b in range(nb):
            inside = b < nb - 1
            mask = _band_kq(None if inside else (n == nsteps - 1))

            def unit(r, carry, b=b, inside=inside, mask=mask):
                rows = _att_rows(b, r, dil)
                nrows = _att_rows(b + 1, r, dil) if inside else _att_rows(0, r, dil)
                kvb = kv_ref[rows, :].astype(BF16)
                follow = lambda cref, nref: (cref if inside else nref)[nrows, :]
                lse_t = jnp.concatenate([lc_ref[rows, :].T, follow(lc_ref, ln_ref).T], axis=1)
                d_t = jnp.concatenate([dc_ref[rows, :].T, follow(dc_ref, dn_ref).T], axis=1)
                acc = jnp.zeros((CHUNK, LANE), F32)
                for g in range(2):
                    qdo = jnp.concatenate([qc_ref.at[g][rows, :], follow(qc_ref.at[g], qn_ref.at[g]),
                                           doc_ref.at[g][rows, :], follow(doc_ref.at[g], don_ref.at[g])], axis=0).astype(BF16)
                    both = _dot_nt(kvb, qdo)
                    pt = jnp.exp(jnp.where(mask, both[:, :2 * CHUNK], NEG) - lse_t[g:g + 1, :])
                    dst = pt * (both[:, 2 * CHUNK:] - d_t[g:g + 1, :])
                    acc = acc + _dot(jnp.concatenate([dst, pt], axis=1).astype(BF16), qdo)
                dkv_ref[rows, :] = acc
                return carry
            _for_residues(dil, unit)

    return pl.pallas_call(
        kern, name=f"attn_dkv_d{dil}", grid=(N_KV, nsteps),
        in_specs=[cur(None), cur(2), nxt(2), cur(2), nxt(2), cur(None), nxt(None), cur(None), nxt(None)], out_specs=cur(None),
        out_shape=jax.ShapeDtypeStruct((N_KV, t, LANE), F32),
        compiler_params=_cparams(("parallel", "parallel")))(kv, q, q, dot, dot, lse, lse, dsum, dsum)


HALO = 8
SSM_CONV_TM, SSM_CONV_W = 512, 512
FFN_CONV_TM, FFN_CONV_W = 256, 1408


def _halo_specs(tm, width, t_rows, col_off=0, lead=None):
    per, last = tm // HALO, t_rows // HALO - 1
    row_maps = (lambda i: i, lambda i: jnp.maximum(i * per - 1, 0), lambda i: jnp.minimum((i + 1) * per, last))
    specs = []
    for rows, rm in zip((tm, HALO, HALO), row_maps):
        if lead is None:
            specs.append(pl.BlockSpec((rows, width), functools.partial(lambda c, i, rm: (rm(i), c + col_off), rm=rm)))
        else:
            specs.append(pl.BlockSpec((None, rows, width), functools.partial(lambda c, i, rm: (lead, rm(i), c + col_off), rm=rm)))
    return specs


def _fill_ext(buf, tile_ref, before_ref, after_ref, i, nt):
    tm = tile_ref.shape[0]
    buf[0:HALO, :] = jnp.where(i > 0, before_ref[...].astype(F32), 0.0)
    buf[HALO:HALO + tm, :] = tile_ref[...].astype(F32)
    if after_ref is not None:
        buf[HALO + tm:, :] = jnp.where(i < nt - 1, after_ref[...].astype(F32), 0.0)


CONV_RB, CONV_CW = 16, 256


def _lane_chunks(width):
    return [slice(c0, min(c0 + CONV_CW, width)) for c0 in range(0, width, CONV_CW)]


def _shifted(buf, taps, r0, rows, cs):
    return [buf[pl.ds(HALO - (taps - 1) + k + r0, rows), cs] for k in range(taps)]


def _taps_fwd(xs, w, b):
    acc = b
    for k, xk in enumerate(xs):
        acc = acc + w[k:k + 1, :] * xk
    return acc


def _taps_bwd(bufd, w, taps, r0, rows, cs):
    acc = None
    for k in range(taps):
        term = w[k:k + 1, :] * bufd[pl.ds(r0 + (taps - 1) - k, rows), cs]
        acc = term if acc is None else acc + term
    return acc


def _fold8(z):
    return z[:HALO] + z[HALO:] if z.shape[0] == 2 * HALO else z


def _silu_grad(pre):
    sg = _sigmoid(pre)
    return sg * (1.0 + pre * (1.0 - sg))


def _ssm_conv_fwd(proj, w, b):
    t = proj.shape[0]
    tm, wd = min(SSM_CONV_TM, t), SSM_CONV_W
    nt, taps = t // tm, w.shape[0]

    def kern(x_ref, xb_ref, w_ref, b_ref, o_ref, buf):
        _fill_ext(buf, x_ref, xb_ref, None, pl.program_id(1), nt)
        for cs in _lane_chunks(wd):
            wv, bv = w_ref[:, cs], b_ref[:, cs]
            for r0 in range(0, tm, CONV_RB):
                pre = _taps_fwd(_shifted(buf, taps, r0, CONV_RB, cs), wv, bv)
                o_ref[r0:r0 + CONV_RB, cs] = pre * _sigmoid(pre)

    tile, before, _ = _halo_specs(tm, wd, t)
    par = lambda rows: pl.BlockSpec((rows, wd), lambda c, i: (0, c))
    return pl.pallas_call(
        kern, name="ssm_conv_fwd", grid=(XBC_DIM // wd, nt), in_specs=[tile, before, par(taps), par(1)],
        out_specs=pl.BlockSpec((tm, wd), lambda c, i: (i, c)), out_shape=jax.ShapeDtypeStruct((t, XBC_DIM), F32),
        scratch_shapes=[pltpu.VMEM((tm + HALO, wd), F32)],
        compiler_params=_cparams(("parallel", "parallel")))(proj, proj, w, b)


def _ssm_conv_bwd(proj, w, b, dact, parts):
    t = proj.shape[0]
    tm, wd = min(SSM_CONV_TM, t), SSM_CONV_W
    nt, taps, ncol, ns = t // tm, w.shape[0], XBC_DIM // SSM_CONV_W, len(parts)

    def kern(x_ref, xb_ref, xa_ref, d_ref, dn_ref, w_ref, b_ref, *rest):
        dx_ref, gw_ref, gb_ref = rest[ns:ns + 3]
        buf, bufd = rest[2 * ns + 3:2 * ns + 5]
        i = pl.program_id(1)
        if ns:
            start, finish = _scatter_steps(ns, rest[:ns], rest[ns + 3:2 * ns + 3], rest[2 * ns + 5:])
            pl.when((pl.program_id(0) == 0) & (i == 0))(start)
            pl.when((pl.program_id(0) == ncol - 1) & (i == nt - 1))(finish)
        _fill_ext(buf, x_ref, xb_ref, xa_ref, i, nt)

        @pl.when(i == 0)
        def _():
            gw_ref[...] = jnp.zeros(gw_ref.shape, F32)
            gb_ref[...] = jnp.zeros(gb_ref.shape, F32)
        for cs in _lane_chunks(wd):
            wv, bv = w_ref[:, cs], b_ref[:, cs]
            acc = [jnp.zeros((HALO, cs.stop - cs.start), F32) for _ in range(taps + 1)]
            for r0 in list(range(0, tm, CONV_RB)) + [tm]:
                inside = r0 < tm
                rows = CONV_RB if inside else HALO
                xs = _shifted(buf, taps, r0, rows, cs)
                d = d_ref[r0:r0 + rows, cs] if inside else jnp.where(i < nt - 1, dn_ref[:, cs], 0.0)
                dpre = d * _silu_grad(_taps_fwd(xs, wv, bv))
                bufd[r0:r0 + rows, cs] = dpre
                if inside:
                    acc[taps] = acc[taps] + _fold8(dpre)
                    for k in range(taps):
                        acc[k] = acc[k] + _fold8(dpre * xs[k])
            gb_ref[:, cs] += jnp.sum(acc[taps], axis=0, keepdims=True)
            for k in range(taps):
                gw_ref[k:k + 1, cs] += jnp.sum(acc[k], axis=0, keepdims=True)
            for r0 in range(0, tm, CONV_RB):
                dx_ref[r0:r0 + CONV_RB, cs] = _taps_bwd(bufd, wv, taps, r0, CONV_RB, cs).astype(BF16)

    xt, xb, xa = _halo_specs(tm, wd, t)
    dt_, _, dn = _halo_specs(tm, wd, t)
    par = lambda rows: pl.BlockSpec((rows, wd), lambda c, i: (0, c))
    return pl.pallas_call(
        kern, name="ssm_conv_bwd", grid=(ncol, nt), in_specs=[xt, xb, xa, dt_, dn, par(taps), par(1)] + [ANY_SPEC] * ns,
        out_specs=[pl.BlockSpec((tm, wd), lambda c, i: (i, c)), par(taps), par(1)] + [ANY_SPEC] * ns,
        out_shape=[jax.ShapeDtypeStruct((t, XBC_DIM), BF16), jax.ShapeDtypeStruct((taps, XBC_DIM), F32),
                   jax.ShapeDtypeStruct((1, XBC_DIM), F32)] + [jax.ShapeDtypeStruct(s.shape, s.dtype) for s in parts],
        scratch_shapes=[pltpu.VMEM((tm + 2 * HALO, wd), F32), pltpu.VMEM((tm + HALO, wd), F32)] + (_scatter_sems(ns) if ns else []),
        compiler_params=_cparams(("arbitrary", "arbitrary")))(proj, proj, proj, dact, dact, w, b, *parts)


def _ffn_act_fwd(u, w, b):
    t = u.shape[1]
    tm, wd = min(FFN_CONV_TM, t), FFN_CONV_W
    nt, taps, nc = t // tm, w.shape[0], D_FF // FFN_CONV_W

    def kern(g_ref, gb_ref, v_ref, vb_ref, wg_ref, wv_ref, bg_ref, bv_ref, a_ref, bufg, bufv):
        i = pl.program_id(1)
        _fill_ext(bufg, g_ref, gb_ref, None, i, nt)
        _fill_ext(bufv, v_ref, vb_ref, None, i, nt)
        for cs in _lane_chunks(wd):
            wg, wv, bg, bv = wg_ref[:, cs], wv_ref[:, cs], bg_ref[:, cs], bv_ref[:, cs]
            for r0 in range(0, tm, CONV_RB):
                g = _taps_fwd(_shifted(bufg, taps, r0, CONV_RB, cs), wg, bg)
                v = _taps_fwd(_shifted(bufv, taps, r0, CONV_RB, cs), wv, bv)
                a_ref[r0:r0 + CONV_RB, cs] = (g * _sigmoid(g) * v).astype(BF16)

    gt, gbf, _ = _halo_specs(tm, wd, t, lead=0)
    vt, vbf, _ = _halo_specs(tm, wd, t, lead=1)
    par = lambda rows, off: pl.BlockSpec((rows, wd), functools.partial(lambda c, i, off: (0, c + off), off=off))
    return pl.pallas_call(
        kern, name="ffn_act_fwd", grid=(nc, nt),
        in_specs=[gt, gbf, vt, vbf, par(taps, 0), par(taps, nc), par(1, 0), par(1, nc)],
        out_specs=pl.BlockSpec((tm, wd), lambda c, i: (i, c)), out_shape=jax.ShapeDtypeStruct((t, D_FF), BF16),
        scratch_shapes=[pltpu.VMEM((tm + HALO, wd), F32)] * 2,
        compiler_params=_cparams(("parallel", "parallel")))(u, u, u, u, w, w, b, b)


def _ffn_act_bwd(u, w, b, da):
    t = u.shape[1]
    tm, wd = min(FFN_CONV_TM, t), FFN_CONV_W
    nt, taps, nc = t // tm, w.shape[0], D_FF // FFN_CONV_W

    def kern(g_ref, gb_ref, ga_ref, v_ref, vb_ref, va_ref, d_ref, dn_ref, wg_ref, wv_ref, bg_ref, bv_ref,
             du_ref, gwg_ref, gwv_ref, gbg_ref, gbv_ref, bufg, bufv, bufdg, bufdv):
        i = pl.program_id(1)
        _fill_ext(bufg, g_ref, gb_ref, ga_ref, i, nt)
        _fill_ext(bufv, v_ref, vb_ref, va_ref, i, nt)

        @pl.when(i == 0)
        def _():
            for r in (gwg_ref, gwv_ref, gbg_ref, gbv_ref):
                r[...] = jnp.zeros(r.shape, F32)
        for cs in _lane_chunks(wd):
            wg, wv, bg, bv = wg_ref[:, cs], wv_ref[:, cs], bg_ref[:, cs], bv_ref[:, cs]
            zero = jnp.zeros((HALO, cs.stop - cs.start), F32)
            accg, accv = [zero] * (taps + 1), [zero] * (taps + 1)
            for r0 in list(range(0, tm, CONV_RB)) + [tm]:
                inside = r0 < tm
                rows = CONV_RB if inside else HALO
                xg, xv = _shifted(bufg, taps, r0, rows, cs), _shifted(bufv, taps, r0, rows, cs)
                g, v = _taps_fwd(xg, wg, bg), _taps_fwd(xv, wv, bv)
                dav = d_ref[r0:r0 + rows, cs] if inside else jnp.where(i < nt - 1, dn_ref[:, cs], 0.0)
                sg = _sigmoid(g)
                dg = dav * v * (sg * (1.0 + g * (1.0 - sg)))
                dv = dav * (g * sg)
                bufdg[r0:r0 + rows, cs] = dg
                bufdv[r0:r0 + rows, cs] = dv
                if inside:
                    accg[taps], accv[taps] = accg[taps] + _fold8(dg), accv[taps] + _fold8(dv)
                    for k in range(taps):
                        accg[k], accv[k] = accg[k] + _fold8(dg * xg[k]), accv[k] + _fold8(dv * xv[k])
            gbg_ref[:, cs] += jnp.sum(accg[taps], axis=0, keepdims=True)
            gbv_ref[:, cs] += jnp.sum(accv[taps], axis=0, keepdims=True)
            for k in range(taps):
                gwg_ref[k:k + 1, cs] += jnp.sum(accg[k], axis=0, keepdims=True)
                gwv_ref[k:k + 1, cs] += jnp.sum(accv[k], axis=0, keepdims=True)
            for r0 in range(0, tm, CONV_RB):
                du_ref[0, r0:r0 + CONV_RB, cs] = _taps_bwd(bufdg, wg, taps, r0, CONV_RB, cs).astype(BF16)
                du_ref[1, r0:r0 + CONV_RB, cs] = _taps_bwd(bufdv, wv, taps, r0, CONV_RB, cs).astype(BF16)

    gt, gbf, gaf = _halo_specs(tm, wd, t, lead=0)
    vt, vbf, vaf = _halo_specs(tm, wd, t, lead=1)
    dt_, _, dn = _halo_specs(tm, wd, t)
    par = lambda rows, off: pl.BlockSpec((rows, wd), functools.partial(lambda c, i, off: (0, c + off), off=off))
    return pl.pallas_call(
        kern, name="ffn_act_bwd", grid=(nc, nt),
        in_specs=[gt, gbf, gaf, vt, vbf, vaf, dt_, dn, par(taps, 0), par(taps, nc), par(1, 0), par(1, nc)],
        out_specs=[pl.BlockSpec((2, tm, wd), lambda c, i: (0, i, c)), par(taps, 0), par(taps, 0), par(1, 0), par(1, 0)],
        out_shape=[jax.ShapeDtypeStruct((2, t, D_FF), BF16)] + [jax.ShapeDtypeStruct((taps, D_FF), F32)] * 2
        + [jax.ShapeDtypeStruct((1, D_FF), F32)] * 2,
        scratch_shapes=[pltpu.VMEM((tm + 2 * HALO, wd), F32)] * 2 + [pltpu.VMEM((tm + HALO, wd), F32)] * 2,
        compiler_params=_cparams(("parallel", "arbitrary")))(u, u, u, u, u, u, da, da, w, w, b, b)


def _softplus(x):
    e = jnp.exp(-jnp.abs(x))
    return jnp.maximum(x, 0.0) + jnp.where(e < 1e-4, e - 0.5 * e * e, jnp.log(1.0 + e))


def _tri(lower):
    r = lax.broadcasted_iota(jnp.int32, (CHUNK, CHUNK), 0)
    c = lax.broadcasted_iota(jnp.int32, (CHUNK, CHUNK), 1)
    return (r >= c) if lower else (r <= c)


def _cum(mat_bool, x):
    return jnp.dot(mat_bool.astype(F32), x, precision=lax.Precision.HIGHEST, preferred_element_type=F32)


def _pair_sel(lane_lo, tile, h0):
    return jnp.where(lane_lo, tile[:, h0:h0 + 1], tile[:, h0 + 1:h0 + 2])


def _ssd_fwd(xbc_act, proj, dt_bias_p, a_log_p, dskip_t, shards):
    t = xbc_act.shape[0]
    nch = t // CHUNK
    ns = len(shards)

    def kern(xa_ref, dtr_ref, bias_ref, alog_ref, dsk_ref, *rest):
        y_ref, dt_ref, hs_ref = rest[ns:ns + 3]
        hst = rest[2 * ns + 3]
        if ns:
            start, forward, finish = _gather_steps(shards, rest[:ns], rest[ns + 3:2 * ns + 3], rest[2 * ns + 4:])
            pl.when(pl.program_id(0) == 0)(start)
            pl.when(pl.program_id(0) == (3 * nch) // 4)(forward)
            pl.when(pl.program_id(0) == nch - 1)(finish)

        @pl.when(pl.program_id(0) == 0)
        def _():
            hst[...] = jnp.zeros(hst.shape, F32)
        dt = _softplus(dtr_ref[...] + bias_ref[...])
        dt_ref[...] = dt
        acum = _cum(_tri(True), dt * (-jnp.exp(alog_ref[...])))
        acum_t = acum.T
        ea = jnp.exp(acum)
        a_last = acum[CHUNK - 1:CHUNK, :]
        dend = jnp.exp(a_last - acum)
        ea_last = jnp.exp(a_last)
        causal = _tri(True)
        lane_lo = lax.broadcasted_iota(jnp.int32, (CHUNK, LANE), 1) < HEAD_DIM
        row_lo = lax.broadcasted_iota(jnp.int32, (CHUNK, LANE), 0) < HEAD_DIM
        for g in range(2):
            bg = xa_ref[:, SSM_INNER + g * SSM_STATE:SSM_INNER + (g + 1) * SSM_STATE].astype(BF16)
            cg = xa_ref[:, SSM_INNER + BC_DIM + g * SSM_STATE:SSM_INNER + BC_DIM + (g + 1) * SSM_STATE].astype(BF16)
            cb = _dot_nt(cg, bg)
            for j in range(4 * g, 4 * g + 4):
                h0 = 2 * j
                cols = slice(j * LANE, (j + 1) * LANE)
                xp = xa_ref[:, cols]
                xdt = xp * _pair_sel(lane_lo, dt, h0)
                ydiag = None
                for hh, sel in ((h0, lane_lo), (h0 + 1, ~lane_lo)):
                    seg = acum[:, hh:hh + 1] - acum_t[hh:hh + 1, :]
                    mm = (cb * jnp.where(causal, jnp.exp(jnp.minimum(seg, 0.0)), 0.0)).astype(BF16)
                    d = _dot(mm, jnp.where(sel, xdt, 0.0).astype(BF16))
                    ydiag = d if ydiag is None else ydiag + d
                hp = hst[cols, :]
                hs_ref[cols, :] = hp
                yoff = _dot_nt(cg, hp.astype(BF16)) * _pair_sel(lane_lo, ea, h0)
                y_ref[:, cols] = ydiag + yoff + dsk_ref[:, cols] * xp
                xw = (xdt * _pair_sel(lane_lo, dend, h0)).astype(BF16)
                rowf = jnp.where(row_lo, ea_last[:, h0:h0 + 1], ea_last[:, h0 + 1:h0 + 2])
                hst[cols, :] = hp * rowf + _dot_tn(xw, bg)

    return pl.pallas_call(
        kern, name="ssd_fwd", grid=(nch,),
        in_specs=[pl.BlockSpec((CHUNK, XBC_DIM), lambda c: (c, 0)), pl.BlockSpec((CHUNK, LANE), lambda c: (c, OFF_DT // LANE)),
                  pl.BlockSpec((1, LANE), lambda c: (0, 0)), pl.BlockSpec((1, LANE), lambda c: (0, 0)),
                  pl.BlockSpec((1, SSM_INNER), lambda c: (0, 0))] + [ANY_SPEC] * ns,
        out_specs=[pl.BlockSpec((CHUNK, SSM_INNER), lambda c: (c, 0)), pl.BlockSpec((CHUNK, LANE), lambda c: (c, 0)),
                   pl.BlockSpec((None, SSM_INNER, SSM_STATE), lambda c: (c, 0, 0))] + [ANY_SPEC] * ns,
        out_shape=[jax.ShapeDtypeStruct((t, SSM_INNER), F32), jax.ShapeDtypeStruct((t, LANE), F32),
                   jax.ShapeDtypeStruct((nch, SSM_INNER, SSM_STATE), F32)] + _gather_out_shapes(shards),
        scratch_shapes=[pltpu.VMEM((SSM_INNER, SSM_STATE), F32)] + (_gather_sems(ns) if ns else []),
        compiler_params=_cparams(("arbitrary",)))(xbc_act, proj, dt_bias_p, a_log_p, dskip_t, *shards)


def _ssd_bwd(xbc_act, proj, dt_sp, hstates, dy, dt_bias_p, a_log_p, dskip_t):
    t = xbc_act.shape[0]
    nch = t // CHUNK

    pair = jnp.arange(SSM_HEADS // 2)[:, None, None]
    psel = (jnp.arange(LANE)[None, None, :] == 2 * pair + (jnp.arange(LANE) // HEAD_DIM)[None, :, None]).astype(BF16)

    def kern(xa_ref, dtr_ref, dt_ref, hs_ref, dy_ref, bias_ref, alog_ref, dsk_ref, psel_ref,
             dact_ref, ddtr_ref, da_ref, dbias_ref, ddsk_ref, dh):
        @pl.when(pl.program_id(0) == 0)
        def _():
            dh[...] = jnp.zeros(dh.shape, F32)
            for r in (da_ref, dbias_ref, ddsk_ref):
                r[...] = jnp.zeros(r.shape, F32)
        dt = dt_ref[...]
        a_neg = -jnp.exp(alog_ref[...])
        acum = _cum(_tri(True), dt * a_neg)
        acum_t = acum.T
        ea = jnp.exp(acum)
        a_last = acum[CHUNK - 1:CHUNK, :]
        dend = jnp.exp(a_last - acum)
        ea_last = jnp.exp(a_last)
        causal = _tri(True)
        lane = lax.broadcasted_iota(jnp.int32, (CHUNK, LANE), 1)
        rowi = lax.broadcasted_iota(jnp.int32, (CHUNK, LANE), 0)
        lane_lo, row_lo, last_row = lane < HEAD_DIM, rowi < HEAD_DIM, rowi == CHUNK - 1
        d_dt = jnp.zeros((CHUNK, LANE), F32)
        d_acum = jnp.zeros((CHUNK, LANE), F32)
        for g in range(2):
            bcols = slice(SSM_INNER + g * SSM_STATE, SSM_INNER + (g + 1) * SSM_STATE)
            ccols = slice(SSM_INNER + BC_DIM + g * SSM_STATE, SSM_INNER + BC_DIM + (g + 1) * SSM_STATE)
            bg, cg = xa_ref[:, bcols].astype(BF16), xa_ref[:, ccols].astype(BF16)
            cb = _dot_nt(cg, bg)
            dg_sum = jnp.zeros((CHUNK, CHUNK), F32)
            dcg = jnp.zeros((CHUNK, SSM_STATE), F32)
            dbg = jnp.zeros((CHUNK, SSM_STATE), F32)
            for j in range(4 * g, 4 * g + 4):
                h0 = 2 * j
                cols = slice(j * LANE, (j + 1) * LANE)
                xp, dyp = xa_ref[:, cols], dy_ref[:, cols]
                dtsel = _pair_sel(lane_lo, dt, h0)
                xdt = xp * dtsel
                xdt_b = xdt.astype(BF16)
                hp, dhp = hs_ref[cols, :], dh[cols, :]
                hp_b, dhp_b = hp.astype(BF16), dhp.astype(BF16)
                easel, dendsel = _pair_sel(lane_lo, ea, h0), _pair_sel(lane_lo, dend, h0)
                dx, ydiag = None, None
                for hh, sel in ((h0, lane_lo), (h0 + 1, ~lane_lo)):
                    dyh = jnp.where(sel, dyp, 0.0).astype(BF16)
                    seg = acum[:, hh:hh + 1] - acum_t[hh:hh + 1, :]
                    dec = jnp.where(causal, jnp.exp(jnp.minimum(seg, 0.0)), 0.0)
                    mm_b = (cb * dec).astype(BF16)
                    dg_sum = dg_sum + dec * _dot_nt(dyh, xdt_b)
                    d = _dot_tn(mm_b, dyh)
                    y = _dot(mm_b, jnp.where(sel, xdt, 0.0).astype(BF16))
                    dx = d if dx is None else dx + d
                    ydiag = y if ydiag is None else ydiag + y
                g2 = _dot_nt(bg, dhp_b)
                tprod = xdt * g2 * dendsel
                yoff = _dot_nt(cg, hp_b) * easel
                yc = dyp.astype(BF16).astype(F32) * ydiag + dyp * yoff - (xdt_b.astype(F32) * dx + tprod)
                dx = dx + g2 * dendsel
                psel = psel_ref[j]
                t_lo = jnp.sum(jnp.where(lane_lo, tprod, 0.0), keepdims=True).reshape(1, 1)
                t_hi = jnp.sum(tprod, keepdims=True).reshape(1, 1) - t_lo
                hh_prod = dhp * hp
                s_lo = jnp.sum(jnp.where(row_lo, hh_prod, 0.0), keepdims=True).reshape(1, 1)
                s_hi = jnp.sum(hh_prod, keepdims=True).reshape(1, 1) - s_lo
                end_lo = ea_last[:, h0:h0 + 1] * s_lo + t_lo
                end_hi = ea_last[:, h0 + 1:h0 + 2] * s_hi + t_hi
                ends = jnp.where(lane == h0, end_lo, jnp.where(lane == h0 + 1, end_hi, 0.0))
                d_acum = d_acum + _dot_split(yc, psel) + jnp.where(last_row, ends, 0.0)
                dye = (dyp * easel).astype(BF16)
                dcg = dcg + _dot(dye, hp_b)
                dbg = dbg + _dot((xdt * dendsel).astype(BF16), dhp_b)
                rowf = jnp.where(row_lo, ea_last[:, h0:h0 + 1], ea_last[:, h0 + 1:h0 + 2])
                dh[cols, :] = dhp * rowf + _dot_tn(dye, cg)
                dact_ref[:, cols] = dx * dtsel + dsk_ref[:, cols] * dyp
                d_dt = d_dt + _dot_split(dx * xp, psel)
                ddsk_ref[:, cols] += jnp.sum(dyp * xp, axis=0, keepdims=True)
            dg_b = dg_sum.astype(BF16)
            dact_ref[:, ccols] = dcg + _dot(dg_b, bg)
            dact_ref[:, bcols] = dbg + _dot_tn(dg_b, cg)
        d_adt = _cum(_tri(False), d_acum)
        d_dt = d_dt + d_adt * a_neg
        da_ref[...] += jnp.sum(d_adt * dt, axis=0, keepdims=True)
        d_raw = jnp.where(lane < SSM_HEADS, d_dt * _sigmoid(dtr_ref[...] + bias_ref[...]), 0.0)
        ddtr_ref[...] = d_raw.astype(BF16)
        dbias_ref[...] += jnp.sum(d_raw, axis=0, keepdims=True)

    rev = lambda c: (nch - 1 - c, 0)
    return pl.pallas_call(
        kern, name="ssd_bwd", grid=(nch,),
        in_specs=[pl.BlockSpec((CHUNK, XBC_DIM), rev), pl.BlockSpec((CHUNK, LANE), lambda c: (nch - 1 - c, OFF_DT // LANE)),
                  pl.BlockSpec((CHUNK, LANE), rev), pl.BlockSpec((None, SSM_INNER, SSM_STATE), lambda c: (nch - 1 - c, 0, 0)),
                  pl.BlockSpec((CHUNK, SSM_INNER), rev),
                  pl.BlockSpec((1, LANE), lambda c: (0, 0)), pl.BlockSpec((1, LANE), lambda c: (0, 0)),
                  pl.BlockSpec((1, SSM_INNER), lambda c: (0, 0)), pl.BlockSpec(psel.shape, lambda c: (0, 0, 0))],
        out_specs=[pl.BlockSpec((CHUNK, XBC_DIM), rev), pl.BlockSpec((CHUNK, LANE), rev),
                   pl.BlockSpec((1, LANE), lambda c: (0, 0)), pl.BlockSpec((1, LANE), lambda c: (0, 0)),
                   pl.BlockSpec((1, SSM_INNER), lambda c: (0, 0))],
        out_shape=[jax.ShapeDtypeStruct((t, XBC_DIM), F32), jax.ShapeDtypeStruct((t, LANE), BF16),
                   jax.ShapeDtypeStruct((1, LANE), F32), jax.ShapeDtypeStruct((1, LANE), F32),
                   jax.ShapeDtypeStruct((1, SSM_INNER), F32)],
        scratch_shapes=[pltpu.VMEM((SSM_INNER, SSM_STATE), F32)],
        compiler_params=_cparams(("arbitrary",)))(xbc_act, proj, dt_sp, hstates, dy, dt_bias_p, a_log_p, dskip_t, psel)


def _ssm_post_fwd(y, proj, g):
    def body(y_ref, z_ref, g_ref, o_ref):
        z = z_ref[...]
        yz = y_ref[...] * (z * _sigmoid(z))
        r = lax.rsqrt(jnp.mean(yz * yz, axis=-1, keepdims=True) + EPS)
        o_ref[...] = (yz * r * g_ref[...]).astype(BF16)
    return _rows("ssm_post_fwd", body, [("t", y), ("tc", proj, SSM_INNER, OFF_Z // SSM_INNER), ("p", g)],
                 [(SSM_INNER, BF16)])[0]


def _ssm_post_bwd(dmix, y, proj, g):
    def body(do_ref, y_ref, z_ref, g_ref, dy_ref, dz_ref, dg_ref):
        z, yv, dout = z_ref[...], y_ref[...], do_ref[...]
        sg = _sigmoid(z)
        gz = z * sg
        yz = yv * gz
        r = lax.rsqrt(jnp.mean(yz * yz, axis=-1, keepdims=True) + EPS)
        gd = dout * g_ref[...]
        dyz = r * gd - yz * (r * r * r * jnp.mean(yz * gd, axis=-1, keepdims=True))
        dy_ref[...] = dyz * gz
        dz_ref[...] = (dyz * yv * (sg * (1.0 + z * (1.0 - sg)))).astype(BF16)
        dg_ref[...] += jnp.sum(dout * yz * r, axis=0, keepdims=True)
    return _rows("ssm_post_bwd", body,
                 [("tc", dmix, SSM_INNER, 0), ("t", y), ("tc", proj, SSM_INNER, OFF_Z // SSM_INNER), ("p", g)],
                 [(SSM_INNER, F32), (SSM_INNER, BF16)], accs=[(1, SSM_INNER)])


def _ple_loss(gl, pp, x2, tgt):
    d = x2.shape[1]

    def body(gl_ref, pp_ref, x_ref, t_ref, dy_ref, dgl_ref, dpp_ref, sq_ref):
        s = _sigmoid(gl_ref[...])
        ppv = pp_ref[...]
        diff = x_ref[...] + s * ppv - t_ref[...]
        dy = diff * (1.0 / d)
        dy_ref[...] = dy
        dgl_ref[...] = (dy * ppv * s * (1.0 - s)).astype(BF16)
        dpp_ref[...] = (dy * s).astype(BF16)
        sq_ref[...] += jnp.sum(diff * diff, axis=0, keepdims=True)
    return _rows("ple_loss", body, [("t", gl), ("t", pp), ("t", x2), ("t", tgt)], [(d, F32), (d, BF16), (d, BF16)],
                 accs=[(1, d)])


def _pad_lanes(v, width=LANE):
    return jnp.pad(v, ((0, 0), (0, width - v.shape[1])))


def _local_step(x, p, tgt, wts, late_shards=(), join_late=None, reduce_early=None):
    g_attn, g_ssm, g_ffn, g_ple = wts["attn_norm_g"], wts["ssm_norm_g"], wts["ffn_norm_g"], wts["ple_norm_g"]
    w_in_p = wts["w_in_p"]
    gq_t = jnp.tile(wts["q_norm_g"], (1, ATTN_DIM // HEAD_DIM))
    gk_t = jnp.tile(wts["k_norm_g"], (1, KV_DIM // HEAD_DIM))
    dt_bias_p, a_log_p = _pad_lanes(wts["dt_bias"]), _pad_lanes(wts["a_log"])
    dskip_t = jnp.repeat(wts["d_skip"], HEAD_DIM, axis=1)

    h1 = _rms_fwd("rms_attn", x, g_attn)
    proj = _mm_nn("in_proj", [(h1, w_in_p)], F32)
    q_hm, kv_hm = _qknorm_fwd2(proj, gq_t, gk_t)
    pats = [_attn_fwd2(q_hm, kv_hm, d) for d in DILATIONS]
    attn_out, lse = _attn_merge2([o for o, _ in pats], [l for _, l in pats])
    xbc_act = _ssm_conv_fwd(proj, wts["ssm_conv_w"], wts["ssm_conv_b"])
    y_ssd, dt_sp, hstates, *gathered = _ssd_fwd(xbc_act, proj, dt_bias_p, a_log_p, dskip_t, list(late_shards))
    if join_late is not None:
        wts = {**wts, **join_late(gathered)}
    w_out_s, w_out_a = wts["w_out_ssm"], wts["w_out_attn"]
    w_up, w_down, w_gate, w_proj = wts["w_up"], wts["w_down"], wts["w_ple_gate"], wts["w_ple_proj"]
    ssm_out = _ssm_post_fwd(y_ssd, proj, g_ssm)
    x1 = _mm_nn("out_proj", [(ssm_out, w_out_s), (attn_out, w_out_a)], F32, res=x, tm=1024)
    h2 = _rms_fwd("rms_ffn", x1, g_ffn)
    u = _mm_nn("ffn_up", [(h2, w_up)], F32, tm=1024, tn=1408, halves=True)
    a = _ffn_act_fwd(u, wts["ffn_conv_w"], wts["ffn_conv_b"])
    x2 = _mm_nn("ffn_down", [(a, w_down)], F32, res=x1, tm=1024, tn=512)
    h3 = _rms_fwd("rms_ple", x2, g_ple)
    gl = _mm_nn("ple_gate", [(h3, w_gate)], F32, tm=2048)
    pb = p.astype(BF16)
    pp = _mm_nn("ple_proj", [(pb, w_proj)], F32, tm=2048)
    dy, dgl, dpp, sq = _ple_loss(gl, pp, x2, tgt)

    grads = {}
    grads["w_ple_proj"] = _mm_tn("g_ple_proj", pb, dpp, tn=PLE_DIM, chip_cols=True)
    grads["w_ple_gate"] = _mm_tn("g_ple_gate", h3, dgl)
    dh3 = _mm_nt("d_h3", [(dgl, w_gate, 0)], F32, tm=2048)
    dx2, dx2b, grads["ple_norm_g"] = _rms_bwd("rms_ple_bwd", dh3, x2, g_ple, dy)
    da = _mm_nt("d_ffn_act", [(dx2b, w_down, 0)], F32, tm=1024, tn=1408)
    grads["w_down"] = _mm_tn("g_ffn_down", a, dx2b, tm=1408)
    du, gwg, gwv, gbg, gbv = _ffn_act_bwd(u, wts["ffn_conv_w"], wts["ffn_conv_b"], da)
    grads["ffn_conv_w"] = jnp.concatenate([gwg, gwv], axis=1)
    grads["ffn_conv_b"] = jnp.concatenate([gbg, gbv], axis=1)
    grads["w_up"] = _mm_tn("g_ffn_up", h2, du, tn=1408, chip_cols=True)
    dh2 = _mm_nt("d_h2", [(du, w_up, 0, 0), (du, w_up, 1, 1)], F32, tm=1024, tn=512)
    dx1, dx1b, grads["ffn_norm_g"] = _rms_bwd("rms_ffn_bwd", dh2, x1, g_ffn, dx2)
    dmix = _mm_nt("d_mix", [(dx1b, jnp.concatenate([w_out_s, w_out_a], axis=0), 0)], F32, tm=1024)
    grads["w_out"] = jnp.concatenate([_mm_tn("g_out_attn", attn_out, dx1b), _mm_tn("g_out_ssm", ssm_out, dx1b)], axis=0)
    dy_ssd, dz, grads["ssm_norm_g"] = _ssm_post_bwd(dmix, y_ssd, proj, g_ssm)
    dact, ddtr, d_a, d_bias, d_dsk = _ssd_bwd(xbc_act, proj, dt_sp, hstates, dy_ssd, dt_bias_p, a_log_p, dskip_t)
    grads["dt_bias"] = d_bias[:, :SSM_HEADS]
    grads["a_log"] = d_a[:, :SSM_HEADS] * (-jnp.exp(wts["a_log"]))
    grads["d_skip"] = jnp.sum(d_dsk.reshape(SSM_HEADS, HEAD_DIM), axis=1)[None, :]
    chip_sums = reduce_early(grads) if reduce_early is not None else []
    dxbc, grads["ssm_conv_w"], grads["ssm_conv_b"], *scattered = _ssm_conv_bwd(proj, wts["ssm_conv_w"], wts["ssm_conv_b"], dact,
                                                                                chip_sums)
    do_hm, dsum = _attn_bwd_prep2(dmix, attn_out)
    dqs = [_attn_dq2(q_hm, kv_hm, do_hm, lse, dsum, d) for d in DILATIONS]
    dkvs = [_attn_dkv2(q_hm, kv_hm, do_hm, lse, dsum, d) for d in DILATIONS]
    dq, dk, dv, dgq, dgk = _qknorm_bwd2(proj, gq_t, gk_t, dqs, dkvs)
    grads["q_norm_g"] = jnp.sum(dgq.reshape(ATTN_DIM // HEAD_DIM, HEAD_DIM), axis=0)[None, :]
    grads["k_norm_g"] = jnp.sum(dgk.reshape(KV_DIM // HEAD_DIM, HEAD_DIM), axis=0)[None, :]
    dproj = jnp.concatenate([dxbc, dq, dz, dk, dv, ddtr], axis=1)
    grads["w_in_p"] = _mm_tn("g_in_proj", h1, dproj, tm=512)
    dh1 = _mm_nt("d_h1", [(dproj, w_in_p, 0)], F32, tm=1024, tn=512)
    grad_x, _, grads["attn_norm_g"] = _rms_bwd("rms_attn_bwd", dh1, x, g_attn, dx1)
    return sq, grad_x, grads, (chip_sums, scattered)


MESH_IDS = pl.DeviceIdType.MESH
N_CHIPS = 4
ANY_SPEC = pl.BlockSpec(memory_space=pl.ANY)
SMALL_ROWS = 96
ALL_SMALL_ROWS = 272


def _place():
    x, y, c = lax.axis_index("x"), lax.axis_index("y"), lax.axis_index("c")
    return x, y, c, [(1 - x, y), (x, 1 - y), (1 - x, 1 - y)]


def _gather_over_chips(arrs):
    n = len(arrs)

    def body(*refs):
        steps = _gather_steps(arrs, refs[:n], refs[n:2 * n], refs[2 * n:2 * n + 4])
        for step in steps:
            step()

    return pl.pallas_call(
        body, name="gather_weights", in_specs=[ANY_SPEC] * n, out_specs=[ANY_SPEC] * n,
        out_shape=_gather_out_shapes(arrs), scratch_shapes=_gather_sems(n))(*arrs)


def _gather_out_shapes(arrs):
    return [jax.ShapeDtypeStruct((N_CHIPS,) + a.shape, a.dtype) for a in arrs]


def _gather_sems(n):
    return [pltpu.SemaphoreType.DMA((3 * n,))] * 4


def _gather_steps(arrs, ins, outs, sems):
    n = len(arrs)
    split = [a.shape[0] % 64 == 0 for a in arrs]
    ici_send, ici_recv, d2d_send, d2d_recv = sems

    def place():
        x, y, c, chips = _place()
        return x, y, c, chips, 2 * x + y

    def part(ref, a, core):
        if not split[a]:
            return ref
        half = arrs[a].shape[0] // 2
        return ref.at[pl.ds(core * half, half)]

    def ici(a, k, slot, where):
        x, y, c, chips, _ = where
        px, py = chips[k]
        return pltpu.make_async_remote_copy(
            src_ref=part(ins[a], a, c), dst_ref=part(outs[a].at[slot], a, c), send_sem=ici_send.at[3 * a + k],
            recv_sem=ici_recv.at[3 * a + k], device_id=(px, py, c), device_id_type=MESH_IDS)

    def d2d(a, k, core, where):
        x, y, c, chips, _ = where
        px, py = chips[k]
        piece = part(outs[a].at[2 * px + py], a, core)
        return pltpu.make_async_remote_copy(src_ref=piece, dst_ref=piece, send_sem=d2d_send.at[3 * a + k],
                                            recv_sem=d2d_recv.at[3 * a + k], device_id=(x, y, 1 - c), device_id_type=MESH_IDS)

    def start():
        w = place()
        for a in range(n):
            for k in range(3):
                ici(a, k, w[4], w).start()

    def forward():
        w = place()
        for a in range(n):
            for k, (px, py) in enumerate(w[3]):
                ici(a, k, 2 * px + py, w).wait_recv()
                if split[a]:
                    d2d(a, k, w[2], w).start()

    def finish():
        w = place()
        for a in range(n):
            for k in range(3):
                if split[a]:
                    d2d(a, k, 1 - w[2], w).wait_recv()
                    d2d(a, k, w[2], w).wait_send()
                ici(a, k, w[4], w).wait_send()

    return start, forward, finish


def _row_tile(rows, cap=256):
    return max(d for d in range(8, cap + 1, 8) if rows % d == 0)


def _swap_halves(name, gs):
    n = len(gs)

    def body(*refs):
        ins, outs, send, recv = refs[:n], refs[n:2 * n], refs[2 * n], refs[2 * n + 1]
        x, y, c, _ = _place()
        cps = []
        for a in range(n):
            half = gs[a].shape[1] // 2
            for q in range(N_CHIPS):
                cps.append(pltpu.make_async_remote_copy(
                    src_ref=ins[a].at[q, pl.ds((1 - c) * half, half)], dst_ref=outs[a].at[q], send_sem=send.at[N_CHIPS * a + q],
                    recv_sem=recv.at[N_CHIPS * a + q], device_id=(x, y, 1 - c), device_id_type=MESH_IDS))
        for cp in cps:
            cp.start()
        for cp in cps:
            cp.wait()

    return pl.pallas_call(
        body, name=name, in_specs=[ANY_SPEC] * n, out_specs=[ANY_SPEC] * n,
        out_shape=[jax.ShapeDtypeStruct((N_CHIPS, g.shape[1] // 2, g.shape[2]), g.dtype) for g in gs],
        scratch_shapes=[pltpu.SemaphoreType.DMA((N_CHIPS * n,))] * 2)(*gs)


def _add_halves(name, g, got, c_idx):
    rows, cols = got.shape[1:]
    tm = _row_tile(rows)
    per = rows // tm

    def kern(c_ref, g_ref, r_ref, o_ref):
        o_ref[...] = (g_ref[...] + r_ref[...]).astype(BF16)

    return pl.pallas_call(
        kern, name=name,
        grid_spec=pltpu.PrefetchScalarGridSpec(
            num_scalar_prefetch=1, grid=(N_CHIPS, per),
            in_specs=[pl.BlockSpec((None, tm, cols), lambda q, i, c_ref: (q, c_ref[0] * per + i, 0)),
                      pl.BlockSpec((None, tm, cols), lambda q, i, c_ref: (q, i, 0))],
            out_specs=pl.BlockSpec((None, tm, cols), lambda q, i, c_ref: (q, i, 0))),
        out_shape=jax.ShapeDtypeStruct((N_CHIPS, rows, cols), BF16),
        compiler_params=_cparams(("parallel", "parallel")))(c_idx, g, got)


def _scatter_over_chips(ss):
    n = len(ss)

    def body(*refs):
        for step in _scatter_steps(n, refs[:n], refs[n:2 * n], refs[2 * n:2 * n + 2]):
            step()

    return pl.pallas_call(
        body, name="grad_scatter_chips", in_specs=[ANY_SPEC] * n, out_specs=[ANY_SPEC] * n,
        out_shape=[jax.ShapeDtypeStruct(s.shape, s.dtype) for s in ss], scratch_shapes=_scatter_sems(n))(*ss)


def _scatter_sems(n):
    return [pltpu.SemaphoreType.DMA((3 * n,))] * 2


def _scatter_steps(n, ins, outs, sems):
    send, recv = sems

    def copy(a, k, slot, where):
        x, y, c, chips = where
        px, py = chips[k]
        return pltpu.make_async_remote_copy(src_ref=ins[a].at[2 * px + py], dst_ref=outs[a].at[slot], send_sem=send.at[3 * a + k],
                                            recv_sem=recv.at[3 * a + k], device_id=(px, py, c), device_id_type=MESH_IDS)

    def start():
        w = _place()
        for a in range(n):
            for k in range(3):
                copy(a, k, 2 * w[0] + w[1], w).start()

    def finish():
        w = _place()
        for a in range(n):
            for k, (px, py) in enumerate(w[3]):
                copy(a, k, 2 * px + py, w).wait()

    return start, finish


def _sum_chips(name, own, parts, idx):
    rows, cols = parts.shape[1:]
    tm = _row_tile(rows)
    per = rows // tm

    def kern(o_idx, a_ref, b_ref, c_ref, d_ref, o_ref):
        o_ref[...] = ((a_ref[...].astype(F32) + b_ref[...].astype(F32)) + c_ref[...].astype(F32)) + d_ref[...].astype(F32)

    def spec(k):
        return pl.BlockSpec((None, tm, cols), functools.partial(lambda i, o_idx, k: (o_idx[k], i, 0), k=k))

    return pl.pallas_call(
        kern, name=name,
        grid_spec=pltpu.PrefetchScalarGridSpec(
            num_scalar_prefetch=1, grid=(per,), in_specs=[spec(0), spec(1), spec(2), spec(3)],
            out_specs=pl.BlockSpec((None, tm, cols), lambda i, o_idx: (0, o_idx[4] * per + i, 0))),
        out_shape=jax.ShapeDtypeStruct((1, 2 * rows, cols), F32), compiler_params=_cparams(("parallel",)))(idx, own, parts, parts, parts)


def _share_with_sibling(gs):
    n = len(gs)

    def body(*refs):
        ins, send, recv = refs[:n], refs[2 * n], refs[2 * n + 1]
        x, y, c, _ = _place()
        cps = []
        for a in range(n):
            half = gs[a].shape[1] // 2
            mine = pl.ds(c * half, half)
            cps.append(pltpu.make_async_remote_copy(src_ref=ins[a].at[0, mine], dst_ref=refs[n + a].at[0, mine], send_sem=send.at[a],
                                                    recv_sem=recv.at[a], device_id=(x, y, 1 - c), device_id_type=MESH_IDS))
        for cp in cps:
            cp.start()
        for cp in cps:
            cp.wait()

    return pl.pallas_call(
        body, name="grad_share_sibling", in_specs=[ANY_SPEC] * n, out_specs=[ANY_SPEC] * n,
        out_shape=[jax.ShapeDtypeStruct(g.shape, g.dtype) for g in gs], input_output_aliases={a: a for a in range(n)},
        scratch_shapes=[pltpu.SemaphoreType.DMA((n,))] * 2)(*gs)


def _allreduce_small(v):
    def body(v_ref, o_ref, land, send, recv):
        x, y, c, _ = _place()
        me = 4 * x + 2 * y + c
        land[me] = v_ref[...]
        cps = []
        for rel in range(1, 8):
            bx, by, bc = (rel >> 2) & 1, (rel >> 1) & 1, rel & 1
            peer = (1 - x if bx else x, 1 - y if by else y, 1 - c if bc else c)
            cps.append(pltpu.make_async_remote_copy(src_ref=v_ref, dst_ref=land.at[me], send_sem=send.at[rel - 1],
                                                    recv_sem=recv.at[rel - 1], device_id=peer, device_id_type=MESH_IDS))
        for cp in cps:
            cp.start()
        for cp in cps:
            cp.wait()
        acc = land[0]
        for d in range(1, 8):
            acc = acc + land[d]
        o_ref[...] = acc

    vm = pl.BlockSpec(memory_space=pltpu.VMEM)
    return pl.pallas_call(
        body, name="allreduce_small", in_specs=[vm], out_specs=vm, out_shape=jax.ShapeDtypeStruct(v.shape, F32),
        scratch_shapes=[pltpu.VMEM((8,) + v.shape, F32), pltpu.SemaphoreType.DMA((7,)), pltpu.SemaphoreType.DMA((7,))])(v)


def _adamw(name, w, g, m, v):
    _, rows, cols = w.shape
    tm = rows if rows * cols <= 128 * 1024 else _row_tile(rows)
    c1 = 1.0 / (1.0 - ADAM_B1 ** ADAM_STEP)
    c2 = 1.0 / (1.0 - ADAM_B2 ** ADAM_STEP)

    def kern(w_ref, g_ref, m_ref, v_ref, d_ref, mo_ref, vo_ref):
        gv = g_ref[...]
        mn = ADAM_B1 * m_ref[...] + (1.0 - ADAM_B1) * gv
        vn = ADAM_B2 * v_ref[...] + (1.0 - ADAM_B2) * (gv * gv)
        d_ref[...] = -ADAM_LR * ((mn * c1) / (jnp.sqrt(vn * c2) + ADAM_EPS) + ADAM_WD * w_ref[...])
        mo_ref[...] = mn
        vo_ref[...] = vn

    spec = pl.BlockSpec((None, tm, cols), lambda i: (0, i, 0))
    return pl.pallas_call(
        kern, name=name, grid=(rows // tm,), in_specs=[spec] * 4, out_specs=[spec] * 3,
        out_shape=[jax.ShapeDtypeStruct(w.shape, F32)] * 3, compiler_params=_cparams(("parallel",)))(w, g, m, v)


SHARDED = (("w_in", 1), ("w_out", 0), ("w_up", 1), ("w_down", 0), ("w_ple_gate", 0), ("w_ple_proj", 1),
           ("ssm_conv_w", 1), ("ffn_conv_w", 1))
MATRICES = ("w_in", "w_out", "w_up", "w_down", "w_ple_gate", "w_ple_proj")
EARLY_REDUCED = MATRICES[1:]
REPLICATED = ("attn_norm_g", "q_norm_g", "k_norm_g", "ssm_conv_b", "dt_bias", "a_log", "d_skip", "ssm_norm_g",
              "ffn_norm_g", "ffn_conv_b", "ple_norm_g")
WEIGHT_ORDER = ("attn_norm_g", "w_in", "q_norm_g", "k_norm_g", "ssm_conv_w", "ssm_conv_b", "dt_bias", "a_log", "d_skip",
                "ssm_norm_g", "w_out", "ffn_norm_g", "w_up", "ffn_conv_w", "ffn_conv_b", "w_down", "ple_norm_g",
                "w_ple_gate", "w_ple_proj")


def _join_chips(g, axis):
    if axis == 0:
        return g.reshape(g.shape[0] * g.shape[1], g.shape[2])
    return jnp.transpose(g, (1, 0, 2)).reshape(g.shape[1], g.shape[0] * g.shape[2])


def _split_chips(g, axis):
    if axis == 0:
        return g.reshape(N_CHIPS, g.shape[0] // N_CHIPS, g.shape[1])
    r, c = g.shape
    return jnp.transpose(g.reshape(r, N_CHIPS, c // N_CHIPS), (1, 0, 2))


def _pack_small(vals, rows=SMALL_ROWS):
    flat = jnp.concatenate([v.reshape(-1) for v in vals])
    return jnp.pad(flat, (0, rows * LANE - flat.shape[0])).reshape(rows, LANE)


def _unpack_small(packed, like):
    flat, out, off = packed.reshape(-1), [], 0
    for v in like:
        out.append(flat[off:off + v.size].reshape(v.shape))
        off += v.size
    return out


def kernel(x, p, attn_norm_g, w_in, q_norm_g, k_norm_g, ssm_conv_w, ssm_conv_b, dt_bias, a_log, d_skip, ssm_norm_g, w_out, ffn_norm_g, w_up, ffn_conv_w, ffn_conv_b, w_down, ple_norm_g, w_ple_gate, w_ple_proj, loss_target, m_attn_norm_g, m_w_in, m_q_norm_g, m_k_norm_g, m_ssm_conv_w, m_ssm_conv_b, m_dt_bias, m_a_log, m_d_skip, m_ssm_norm_g, m_w_out, m_ffn_norm_g, m_w_up, m_ffn_conv_w, m_ffn_conv_b, m_w_down, m_ple_norm_g, m_w_ple_gate, m_w_ple_proj, v_attn_norm_g, v_w_in, v_q_norm_g, v_k_norm_g, v_ssm_conv_w, v_ssm_conv_b, v_dt_bias, v_a_log, v_d_skip, v_ssm_norm_g, v_w_out, v_ffn_norm_g, v_w_up, v_ffn_conv_w, v_ffn_conv_b, v_w_down, v_ple_norm_g, v_w_ple_gate, v_w_ple_proj):
    given = dict(locals())
    w2 = {n: given[n].reshape(given[n].shape[-2:]) if given[n].ndim == 3 else given[n] for n in WEIGHT_ORDER}

    cx, cy, cc = lax.axis_index("x"), lax.axis_index("y"), lax.axis_index("c")
    chip = 2 * cx + cy
    axis_of = dict(SHARDED)
    shard = lambda n: w2[n].astype(BF16) if n in MATRICES else w2[n]
    join = lambda n, g: _join_chips(lax.dynamic_update_index_in_dim(g, shard(n), chip, 0), axis_of[n])
    first = ("w_in", "ssm_conv_w", "ffn_conv_w")
    full = {n: join(n, g) for n, g in zip(first, _gather_over_chips([shard(n) for n in first]))}
    win = full["w_in"]
    w_in_p = jnp.concatenate([win[:, 2048:3584], win[:, 0:512], win[:, 1024:2048], win[:, 512:768], win[:, 768:1024],
                              win[:, 3584:3600], jnp.zeros((D_MODEL, PROJ_P - IN_PROJ), BF16)], axis=1)
    wts = {n: w2[n] for n in REPLICATED}
    wts.update(w_in_p=w_in_p, ssm_conv_w=full["ssm_conv_w"], ffn_conv_w=full["ffn_conv_w"])

    def join_late(gathered):
        late = {n: join(n, g) for n, g in zip(EARLY_REDUCED, gathered)}
        return dict(w_out_attn=late["w_out"][:ATTN_DIM], w_out_ssm=late["w_out"][ATTN_DIM:], w_up=late["w_up"],
                    w_down=late["w_down"], w_ple_gate=late["w_ple_gate"], w_ple_proj=late["w_ple_proj"])

    core = cc.astype(jnp.int32).reshape(1)
    idx = jnp.stack([chip, 2 * (1 - cx) + cy, 2 * cx + (1 - cy), 2 * (1 - cx) + (1 - cy), cc]).astype(jnp.int32)

    def chip_sums_of(tag, names, gd):
        major = [gd[n] if gd[n].ndim == 3 else _split_chips(gd[n], axis_of[n]) for n in names]
        return [_add_halves("grad_add_halves_" + n, g, got, core) for n, g, got in zip(names, major, _swap_halves(tag, major))]

    sq, grad_x, grads, (early_sums, early_parts) = _local_step(
        x[0], p[0, 0], loss_target[0], wts, [shard(n) for n in EARLY_REDUCED], join_late,
        functools.partial(chip_sums_of, "grad_swap_halves_early", EARLY_REDUCED))
    gi = grads.pop("w_in_p")
    grads["w_in"] = jnp.concatenate([gi[:, OFF_Q:OFF_Q + ATTN_DIM], gi[:, OFF_K:OFF_K + KV_DIM], gi[:, OFF_V:OFF_V + KV_DIM],
                                     gi[:, OFF_Z:OFF_Z + SSM_INNER], gi[:, OFF_XBC:OFF_XBC + XBC_DIM], gi[:, OFF_DT:OFF_DT + SSM_HEADS]],
                                    axis=1)
    late_sums = chip_sums_of("grad_swap_halves_late", ("w_in",), grads)
    sums = dict(zip(EARLY_REDUCED + ("w_in",), list(zip(early_sums, early_parts)) + list(zip(late_sums, _scatter_over_chips(late_sums)))))
    halves = [_sum_chips("grad_sum_chips_" + n, *sums[n], idx) for n in MATRICES]
    g_shard = dict(zip(MATRICES, _share_with_sibling(halves)))

    small_names = REPLICATED + ("ssm_conv_w", "ffn_conv_w")
    small_like = [grads[n] for n in small_names] + [jnp.zeros((1,), F32)]
    small = _allreduce_small(_pack_small([grads[n] for n in small_names] + [jnp.sum(sq).reshape(1)], ALL_SMALL_ROWS))
    small_vals = dict(zip(small_names + ("loss",), _unpack_small(small, small_like)))
    loss = (0.5 / D_MODEL) * small_vals["loss"][0]
    for n in ("ssm_conv_w", "ffn_conv_w"):
        cols = w2[n].shape[1]
        g_shard[n] = lax.dynamic_slice_in_dim(small_vals[n], chip * cols, cols, axis=1)[None]

    delta, new_m, new_v = {}, {}, {}
    for n, _ in SHARDED:
        delta[n], new_m[n], new_v[n] = _adamw("adamw_" + n, given[n], g_shard[n], given["m_" + n], given["v_" + n])
    packed = lambda prefix: _pack_small([given[prefix + n] for n in REPLICATED])[None]
    sm = _adamw("adamw_small", packed(""), _pack_small([small_vals[n] for n in REPLICATED])[None], packed("m_"), packed("v_"))
    for n in REPLICATED:
        g_shard[n] = small_vals[n]
    for dst, packed_out in zip((delta, new_m, new_v), sm):
        for n, val in zip(REPLICATED, _unpack_small(packed_out[0], [w2[n] for n in REPLICATED])):
            dst[n] = val

    def shaped(d):
        return [d[n].reshape(given[n].shape) for n in WEIGHT_ORDER]
    return (loss, grad_x[None], *shaped(g_shard), *shaped(delta), *shaped(new_m), *shaped(new_v))
```

```python
import functools

import jax
import jax.numpy as jnp
from jax import lax
from jax.experimental import pallas as pl
from jax.experimental.pallas import tpu as pltpu

F32 = jnp.float32
BF16 = jnp.bfloat16

D_MODEL = 1024
HEAD_DIM = 64
ATTN_DIM = 512
KV_DIM = 256
N_KV = 4
SSM_INNER = 1024
SSM_HEADS = 16
SSM_STATE = 128
BC_DIM = 256
XBC_DIM = SSM_INNER + 2 * BC_DIM
MIX_DIM = ATTN_DIM + SSM_INNER
IN_PROJ = 3600
D_FF = 2816
PLE_DIM = 256
CHUNK = 128
DILATIONS = (1, 4, 16)
EPS = 1e-6
ADAM_LR, ADAM_B1, ADAM_B2, ADAM_EPS, ADAM_WD, ADAM_STEP = 0.001, 0.9, 0.999, 1e-08, 0.01, 10

PROJ_P = 3712
OFF_XBC, OFF_Q, OFF_Z, OFF_K, OFF_V, OFF_DT = 0, 1536, 2048, 3072, 3328, 3584
LANE = 128
VMEM_LIMIT = 48 * 1024 * 1024
NEG = -1e30


def _cparams(sem):
    return pltpu.CompilerParams(dimension_semantics=sem, vmem_limit_bytes=VMEM_LIMIT)


def _sigmoid(x):
    return 1.0 / (1.0 + jnp.exp(-x))


def _dot(a, b):
    return jnp.dot(a, b, preferred_element_type=F32)


def _dot_nt(a, b):
    return lax.dot_general(a, b, (((1,), (1,)), ((), ())), preferred_element_type=F32)


def _dot_tn(a, b):
    return lax.dot_general(a, b, (((0,), (0,)), ((), ())), preferred_element_type=F32)


def _dot_split(x, m):
    hi = x.astype(BF16)
    lo = (x - hi.astype(F32)).astype(BF16)
    return _dot(hi, m) + _dot(lo, m)


def _rows(name, body, ins, outs, accs=(), tm=512):
    t_rows = next(s[1].shape[0] for s in ins if s[0] in ("t", "tc"))
    tm = min(tm, t_rows)
    in_specs, args = [], []
    for s in ins:
        if s[0] == "t":
            in_specs.append(pl.BlockSpec((tm, s[1].shape[1]), lambda i: (i, 0)))
        elif s[0] == "tc":
            in_specs.append(pl.BlockSpec((tm, s[2]), functools.partial(lambda i, c: (i, c), c=s[3])))
        else:
            in_specs.append(pl.BlockSpec(s[1].shape, lambda i: (0, 0)))
        args.append(s[1])
    out_shape = [jax.ShapeDtypeStruct((t_rows, w), dt) for w, dt in outs]
    out_specs = [pl.BlockSpec((tm, w), lambda i: (i, 0)) for w, _ in outs]
    out_shape += [jax.ShapeDtypeStruct(a, F32) for a in accs]
    out_specs += [pl.BlockSpec(a, lambda i: (0, 0)) for a in accs]
    n_acc = len(accs)

    def kern(*refs):
        if n_acc:
            @pl.when(pl.program_id(0) == 0)
            def _():
                for r in refs[len(refs) - n_acc:]:
                    r[...] = jnp.zeros(r.shape, F32)
        body(*refs)

    return pl.pallas_call(
        kern, name=name, grid=(t_rows // tm,), in_specs=in_specs, out_specs=out_specs, out_shape=out_shape,
        compiler_params=_cparams(("arbitrary",) if n_acc else ("parallel",)))(*args)


NCHUNK = 512


def _col_chunks(n):
    return [(c, min(NCHUNK, n - c)) for c in range(0, n, NCHUNK)]


def _mm_nn(name, pairs, out_dtype, res=None, tm=512, tn=None, halves=False):
    m, n = pairs[0][0].shape[0], pairs[0][1].shape[1]
    tn = n if tn is None else tn
    tm = min(tm, m)
    np_ = len(pairs)
    if halves:
        per = n // 2 // tn
        out_spec = pl.BlockSpec((None, tm, tn), lambda j, i: (j // per, i, j % per))
        out_shape = jax.ShapeDtypeStruct((2, m, n // 2), out_dtype)
    else:
        out_spec = pl.BlockSpec((tm, tn), lambda j, i: (i, j))
        out_shape = jax.ShapeDtypeStruct((m, n), out_dtype)
    in_specs, args = [], []
    for a, w in pairs:
        in_specs += [pl.BlockSpec((tm, a.shape[1]), lambda j, i: (i, 0)), pl.BlockSpec((w.shape[0], tn), lambda j, i: (0, j))]
        args += [a, w]
    if res is not None:
        in_specs.append(pl.BlockSpec((tm, tn), lambda j, i: (i, j)))
        args.append(res)

    def kern(*refs):
        o_ref = refs[-1]
        for c0, cw in _col_chunks(tn):
            acc = None
            for q in range(np_):
                d = _dot(refs[2 * q][...], refs[2 * q + 1][:, c0:c0 + cw])
                acc = d if acc is None else acc + d
            if res is not None:
                acc = acc + refs[2 * np_][:, c0:c0 + cw]
            o_ref[:, c0:c0 + cw] = acc.astype(o_ref.dtype)

    return pl.pallas_call(
        kern, name=name, grid=(n // tn, m // tm), in_specs=in_specs, out_specs=out_spec, out_shape=out_shape,
        compiler_params=_cparams(("parallel", "parallel")))(*args)


def _mm_nt(name, pairs, out_dtype, tm=512, tn=None):
    m, n = pairs[0][0].shape[-2], pairs[0][1].shape[0]
    tn = n if tn is None else tn
    tm = min(tm, m)
    np_ = len(pairs)
    in_specs, args = [], []
    for a, w, kb, *lead in pairs:
        if lead:
            in_specs.append(pl.BlockSpec((None, tm, a.shape[2]), functools.partial(lambda j, i, ld: (ld, i, 0), ld=lead[0])))
        else:
            in_specs.append(pl.BlockSpec((tm, a.shape[1]), lambda j, i: (i, 0)))
        in_specs.append(pl.BlockSpec((tn, a.shape[-1]), functools.partial(lambda j, i, kb: (j, kb), kb=kb)))
        args += [a, w]

    def kern(*refs):
        o_ref = refs[-1]
        for c0, cw in _col_chunks(tn):
            acc = None
            for q in range(np_):
                d = _dot_nt(refs[2 * q][...], refs[2 * q + 1][c0:c0 + cw, :])
                acc = d if acc is None else acc + d
            o_ref[:, c0:c0 + cw] = acc.astype(o_ref.dtype)

    return pl.pallas_call(
        kern, name=name, grid=(n // tn, m // tm), in_specs=in_specs,
        out_specs=pl.BlockSpec((tm, tn), lambda j, i: (i, j)),
        out_shape=jax.ShapeDtypeStruct((m, n), out_dtype), compiler_params=_cparams(("parallel", "parallel")))(*args)


def _mm_tn(name, a, b, tm=None, tn=None, tk=1024, chip_cols=False):
    t, m = a.shape
    n = b.shape[-1] * (2 if b.ndim == 3 else 1)
    tm = m if tm is None else tm
    tn = n if tn is None else tn
    tk = min(tk, t)
    if b.ndim == 3:
        per = n // 2 // tn
        b_spec = pl.BlockSpec((None, tk, tn), lambda i, j, k: (j // per, k, j % per))
    else:
        b_spec = pl.BlockSpec((tk, tn), lambda i, j, k: (k, j))
    if chip_cols:
        out_spec = pl.BlockSpec((None, tm, tn), lambda i, j, k: (j, i, 0))
        out_shape = jax.ShapeDtypeStruct((n // tn, m, tn), F32)
    else:
        out_spec = pl.BlockSpec((tm, tn), lambda i, j, k: (i, j))
        out_shape = jax.ShapeDtypeStruct((m, n), F32)

    def kern(a_ref, b_ref, o_ref):
        @pl.when(pl.program_id(2) == 0)
        def _():
            o_ref[...] = jnp.zeros(o_ref.shape, F32)
        for c0, cw in _col_chunks(tn):
            o_ref[:, c0:c0 + cw] += _dot_tn(a_ref[...], b_ref[:, c0:c0 + cw])

    return pl.pallas_call(
        kern, name=name, grid=(m // tm, n // tn, t // tk),
        in_specs=[pl.BlockSpec((tk, tm), lambda i, j, k: (k, i)), b_spec], out_specs=out_spec, out_shape=out_shape,
        compiler_params=_cparams(("parallel", "parallel", "arbitrary")))(a, b)


def _rms_fwd(name, x, g):
    def body(x_ref, g_ref, h_ref):
        xv = x_ref[...]
        r = lax.rsqrt(jnp.mean(xv * xv, axis=-1, keepdims=True) + EPS)
        h_ref[...] = (xv * r * g_ref[...]).astype(BF16)
    return _rows(name, body, [("t", x), ("p", g)], [(x.shape[1], BF16)])[0]


def _rms_bwd(name, dh, x, g, dres):
    d = x.shape[1]

    def body(dh_ref, x_ref, g_ref, dres_ref, dx_ref, dxb_ref, dg_ref):
        xv, dhv = x_ref[...], dh_ref[...]
        r = lax.rsqrt(jnp.mean(xv * xv, axis=-1, keepdims=True) + EPS)
        gd = dhv * g_ref[...]
        dx = dres_ref[...] + r * gd - xv * (r * r * r * jnp.mean(xv * gd, axis=-1, keepdims=True))
        dx_ref[...] = dx
        dxb_ref[...] = dx.astype(BF16)
        dg_ref[...] += jnp.sum(dhv * xv * r, axis=0, keepdims=True)
    return _rows(name, body, [("t", dh), ("t", x), ("p", g), ("t", dres)], [(d, F32), (d, BF16)], accs=[(1, d)])


def _norm_mm(name, x, g, w, tm=512):
    m, k = x.shape
    n = w.shape[1]

    def kern(x_ref, g_ref, w_ref, h_ref, o_ref):
        xv = x_ref[...]
        h = (xv * lax.rsqrt(jnp.mean(xv * xv, axis=-1, keepdims=True) + EPS) * g_ref[...]).astype(BF16)
        h_ref[...] = h
        for c0, cw in _col_chunks(n):
            o_ref[:, c0:c0 + cw] = _dot(h, w_ref[:, c0:c0 + cw])

    return pl.pallas_call(
        kern, name=name, grid=(m // tm,),
        in_specs=[pl.BlockSpec((tm, k), lambda i: (i, 0)), pl.BlockSpec((1, k), lambda i: (0, 0)), pl.BlockSpec((k, n), lambda i: (0, 0))],
        out_specs=[pl.BlockSpec((tm, k), lambda i: (i, 0)), pl.BlockSpec((tm, n), lambda i: (i, 0))],
        out_shape=[jax.ShapeDtypeStruct((m, k), BF16), jax.ShapeDtypeStruct((m, n), F32)],
        compiler_params=_cparams(("parallel",)))(x, g, w)


def _mm_nt_rms_bwd(name, a, w, x, g, dres, parts=(), tm=512):
    m, k = a.shape
    n = w.shape[0]
    ns, steps = len(parts), m // tm

    def kern(a_ref, w_ref, x_ref, g_ref, dres_ref, *rest):
        dx_ref, dxb_ref, dg_ref = rest[ns:ns + 3]
        dh = rest[2 * ns + 3]
        if ns:
            start, finish = _scatter_steps(ns, rest[:ns], rest[ns + 3:2 * ns + 3], rest[2 * ns + 4:])
            pl.when(pl.program_id(0) == 0)(start)
            pl.when(pl.program_id(0) == steps - 1)(finish)

        @pl.when(pl.program_id(0) == 0)
        def _():
            dg_ref[...] = jnp.zeros(dg_ref.shape, F32)
        av = a_ref[...]
        for c0, cw in _col_chunks(n):
            dh[:, c0:c0 + cw] = _dot_nt(av, w_ref[c0:c0 + cw, :])
        xv, dhv = x_ref[...], dh[...]
        r = lax.rsqrt(jnp.mean(xv * xv, axis=-1, keepdims=True) + EPS)
        gd = dhv * g_ref[...]
        dx = dres_ref[...] + r * gd - xv * (r * r * r * jnp.mean(xv * gd, axis=-1, keepdims=True))
        dx_ref[...] = dx
        dxb_ref[...] = dx.astype(BF16)
        dg_ref[...] += jnp.sum(dhv * xv * r, axis=0, keepdims=True)

    row = lambda width: pl.BlockSpec((tm, width), lambda i: (i, 0))
    return pl.pallas_call(
        kern, name=name, grid=(steps,),
        in_specs=[row(k), pl.BlockSpec((n, k), lambda i: (0, 0)), row(n), pl.BlockSpec((1, n), lambda i: (0, 0)), row(n)]
        + [ANY_SPEC] * ns,
        out_specs=[row(n), row(n), pl.BlockSpec((1, n), lambda i: (0, 0))] + [ANY_SPEC] * ns,
        out_shape=[jax.ShapeDtypeStruct((m, n), F32), jax.ShapeDtypeStruct((m, n), BF16), jax.ShapeDtypeStruct((1, n), F32)]
        + [jax.ShapeDtypeStruct(s.shape, s.dtype) for s in parts],
        scratch_shapes=[pltpu.VMEM((tm, n), F32)] + (_scatter_sems(ns) if ns else []),
        compiler_params=_cparams(("arbitrary",)))(a, w, x, g, dres, *parts)


def _head_mean_matrix(width):
    i = jnp.arange(width) // HEAD_DIM
    return jnp.where(i[:, None] == i[None, :], 1.0 / HEAD_DIM, 0.0).astype(BF16)


ATT_SPAN = 2048
N_QH = 8


def _lane_lo(rows):
    return lax.broadcasted_iota(jnp.int32, (rows, LANE), 1) < HEAD_DIM


def _swap_halves_lanes(x):
    return pltpu.roll(x, HEAD_DIM, axis=1)


def _qknorm_fwd2(proj, gq_t, gk_t, tm=256):
    t = proj.shape[0]
    bq, bk = _head_mean_matrix(ATTN_DIM), _head_mean_matrix(KV_DIM)
    scale = HEAD_DIM ** -0.5

    def kern(q_ref, k_ref, v_ref, gq_ref, gk_ref, bq_ref, bk_ref, qo_ref, kvo_ref):
        q, k, v = q_ref[...], k_ref[...], v_ref[...]
        qn = (q * lax.rsqrt(_dot_split(q * q, bq_ref[...]) + EPS) * gq_ref[...]) * scale
        kn = k * lax.rsqrt(_dot_split(k * k, bk_ref[...]) + EPS) * gk_ref[...]
        lo = _lane_lo(tm)
        for j in range(N_KV):
            blk = qn[:, j * LANE:(j + 1) * LANE]
            qo_ref[2 * j] = jnp.where(lo, blk, 0.0)
            qo_ref[2 * j + 1] = jnp.where(lo, _swap_halves_lanes(blk), 0.0)
        for j in range(2):
            kb, vb = kn[:, j * LANE:(j + 1) * LANE], v[:, j * LANE:(j + 1) * LANE]
            kvo_ref[2 * j] = jnp.where(lo, kb, _swap_halves_lanes(vb))
            kvo_ref[2 * j + 1] = jnp.where(lo, _swap_halves_lanes(kb), vb)

    col = lambda w, idx: pl.BlockSpec((tm, w), functools.partial(lambda i, idx: (i, idx), idx=idx))
    par = lambda a: pl.BlockSpec(a.shape, lambda i: (0, 0))
    return pl.pallas_call(
        kern, name="qknorm_fwd", grid=(t // tm,),
        in_specs=[col(ATTN_DIM, OFF_Q // ATTN_DIM), col(KV_DIM, OFF_K // KV_DIM), col(KV_DIM, OFF_V // KV_DIM),
                  par(gq_t), par(gk_t), par(bq), par(bk)],
        out_specs=[pl.BlockSpec((N_QH, tm, LANE), lambda i: (0, i, 0)), pl.BlockSpec((N_KV, tm, LANE), lambda i: (0, i, 0))],
        out_shape=[jax.ShapeDtypeStruct((N_QH, t, LANE), F32), jax.ShapeDtypeStruct((N_KV, t, LANE), F32)],
        compiler_params=_cparams(("parallel",)))(proj, proj, proj, gq_t, gk_t, bq, bk)


def _qknorm_bwd2(proj, gq_t, gk_t, dqs, dkvs, tm=256):
    t = proj.shape[0]
    bq, bk = _head_mean_matrix(ATTN_DIM), _head_mean_matrix(KV_DIM)
    scale = HEAD_DIM ** -0.5

    def kern(q_ref, k_ref, gq_ref, gk_ref, bq_ref, bk_ref, a1, a2, a3, b1, b2, b3, dq_ref, dk_ref, dv_ref, dgq_ref, dgk_ref):
        @pl.when(pl.program_id(0) == 0)
        def _():
            dgq_ref[...] = jnp.zeros(dgq_ref.shape, F32)
            dgk_ref[...] = jnp.zeros(dgk_ref.shape, F32)
        lo = _lane_lo(tm)
        sq = [a1[h] + a2[h] + a3[h] for h in range(N_QH)]
        skv = [b1[h] + b2[h] + b3[h] for h in range(N_KV)]
        dqn = jnp.concatenate([jnp.where(lo, sq[2 * j], _swap_halves_lanes(sq[2 * j + 1])) for j in range(N_KV)], axis=1) * scale
        dkn = jnp.concatenate([jnp.where(lo, skv[2 * j], _swap_halves_lanes(skv[2 * j + 1])) for j in range(2)], axis=1)
        dv = jnp.concatenate([jnp.where(lo, _swap_halves_lanes(skv[2 * j]), skv[2 * j + 1]) for j in range(2)], axis=1)
        q, k = q_ref[...], k_ref[...]
        rq = lax.rsqrt(_dot_split(q * q, bq_ref[...]) + EPS)
        rk = lax.rsqrt(_dot_split(k * k, bk_ref[...]) + EPS)
        gdq, gdk = dqn * gq_ref[...], dkn * gk_ref[...]
        dq_ref[...] = (rq * gdq - q * (rq * rq * rq * _dot_split(q * gdq, bq_ref[...]))).astype(BF16)
        dk_ref[...] = (rk * gdk - k * (rk * rk * rk * _dot_split(k * gdk, bk_ref[...]))).astype(BF16)
        dv_ref[...] = dv.astype(BF16)
        dgq_ref[...] += jnp.sum(dqn * q * rq, axis=0, keepdims=True)
        dgk_ref[...] += jnp.sum(dkn * k * rk, axis=0, keepdims=True)

    col = lambda w, idx: pl.BlockSpec((tm, w), functools.partial(lambda i, idx: (i, idx), idx=idx))
    par = lambda a: pl.BlockSpec(a.shape, lambda i: (0, 0))
    blk = lambda n: pl.BlockSpec((n, tm, LANE), lambda i: (0, i, 0))
    row = lambda w: pl.BlockSpec((tm, w), lambda i: (i, 0))
    acc = lambda w: pl.BlockSpec((1, w), lambda i: (0, 0))
    return pl.pallas_call(
        kern, name="qknorm_bwd", grid=(t // tm,),
        in_specs=[col(ATTN_DIM, OFF_Q // ATTN_DIM), col(KV_DIM, OFF_K // KV_DIM), par(gq_t), par(gk_t), par(bq), par(bk)]
        + [blk(N_QH)] * 3 + [blk(N_KV)] * 3,
        out_specs=[row(ATTN_DIM), row(KV_DIM), row(KV_DIM), acc(ATTN_DIM), acc(KV_DIM)],
        out_shape=[jax.ShapeDtypeStruct((t, ATTN_DIM), BF16), jax.ShapeDtypeStruct((t, KV_DIM), BF16),
                   jax.ShapeDtypeStruct((t, KV_DIM), BF16), jax.ShapeDtypeStruct((1, ATTN_DIM), F32),
                   jax.ShapeDtypeStruct((1, KV_DIM), F32)],
        compiler_params=_cparams(("arbitrary",)))(proj, proj, gq_t, gk_t, bq, bk, *dqs, *dkvs)


def _att_rows(b, r, dil):
    if dil == 1:
        return pl.ds(b * CHUNK, CHUNK)
    return pl.ds(b * CHUNK * dil + r, CHUNK, stride=dil)


def _for_residues(dil, unit):
    for r in range(dil):
        unit(r, 0)


def _band_qk(first):
    ri = lax.broadcasted_iota(jnp.int32, (CHUNK, 2 * CHUNK), 0)
    cj = lax.broadcasted_iota(jnp.int32, (CHUNK, 2 * CHUNK), 1)
    band = (cj - ri >= 0) & (cj - ri <= CHUNK)
    return band if first is None else band & (jnp.logical_not(first) | (cj >= CHUNK))


def _band_kq(last):
    rj = lax.broadcasted_iota(jnp.int32, (CHUNK, 2 * CHUNK), 0)
    ci = lax.broadcasted_iota(jnp.int32, (CHUNK, 2 * CHUNK), 1)
    band = (ci - rj >= 0) & (ci - rj <= CHUNK)
    return band if last is None else band & (jnp.logical_not(last) | (ci < CHUNK))


def _att_specs(t, dil):
    sub = CHUNK * dil
    nb, last = ATT_SPAN // sub, t // sub - 1
    cur = lambda heads: pl.BlockSpec((heads, ATT_SPAN, LANE), lambda kh, n: (kh, n, 0))
    prev = lambda heads: pl.BlockSpec((heads, sub, LANE), lambda kh, n: (kh, jnp.maximum(n * nb - 1, 0), 0))
    nxt = lambda heads: pl.BlockSpec((heads, sub, LANE), lambda kh, n: (kh, jnp.minimum((n + 1) * nb, last), 0))
    return sub, nb, cur, prev, nxt


def _attn_fwd2(q, kv, dil):
    t = q.shape[1]
    sub, nb, cur, prev, _ = _att_specs(t, dil)

    def kern(q_ref, kvp_ref, kvc_ref, o_ref, lse_ref):
        n = pl.program_id(1)
        lane = lax.broadcasted_iota(jnp.int32, (CHUNK, LANE), 1)
        for b in range(nb):
            mask = _band_qk((n == 0) if b == 0 else None)

            def unit(r, carry, b=b, mask=mask):
                rows = _att_rows(b, r, dil)
                kvp = kvc_ref[_att_rows(b - 1, r, dil), :] if b > 0 else kvp_ref[_att_rows(0, r, dil), :]
                kvcat = jnp.concatenate([kvp, kvc_ref[rows, :]], axis=0).astype(BF16)
                lse_tile = jnp.zeros((CHUNK, LANE), F32)
                for g in range(2):
                    s = jnp.where(mask, _dot_nt(q_ref.at[g][rows, :].astype(BF16), kvcat), NEG)
                    m = jnp.max(s, axis=1, keepdims=True)
                    p = jnp.exp(s - m)
                    l = jnp.sum(p, axis=1, keepdims=True)
                    o_ref.at[g][rows, :] = _dot(p.astype(BF16), kvcat) * (1.0 / l)
                    lse_tile = jnp.where(lane == g, m + jnp.log(l), lse_tile)
                lse_ref[rows, :] = lse_tile
                return carry
            _for_residues(dil, unit)

    return pl.pallas_call(
        kern, name=f"attn_fwd_d{dil}", grid=(N_KV, t // ATT_SPAN), in_specs=[cur(2), prev(None), cur(None)],
        out_specs=[cur(2), cur(None)],
        out_shape=[jax.ShapeDtypeStruct((N_QH, t, LANE), F32), jax.ShapeDtypeStruct((N_KV, t, LANE), F32)],
        compiler_params=_cparams(("parallel", "parallel")))(q, kv, kv)


def _attn_merge2(os_, lses, tm=256):
    t = os_[0].shape[1]

    def kern(o1, o2, o3, l1, l2, l3, out_ref, lse_ref):
        pieces = []
        for kh in range(N_KV):
            a, b, c = l1[kh], l2[kh], l3[kh]
            m = jnp.maximum(jnp.maximum(a, b), c)
            tot = m + jnp.log(jnp.exp(a - m) + jnp.exp(b - m) + jnp.exp(c - m))
            lse_ref[kh] = tot
            wa, wb, wc = jnp.exp(a - tot), jnp.exp(b - tot), jnp.exp(c - tot)
            for g in range(2):
                h = 2 * kh + g
                acc = wa[:, g:g + 1] * o1[h] + wb[:, g:g + 1] * o2[h] + wc[:, g:g + 1] * o3[h]
                pieces.append(acc[:, HEAD_DIM:])
        out_ref[...] = jnp.concatenate(pieces, axis=1).astype(BF16)

    blk = lambda n: pl.BlockSpec((n, tm, LANE), lambda i: (0, i, 0))
    return pl.pallas_call(
        kern, name="attn_merge", grid=(t // tm,), in_specs=[blk(N_QH)] * 3 + [blk(N_KV)] * 3,
        out_specs=[pl.BlockSpec((tm, ATTN_DIM), lambda i: (i, 0)), blk(N_KV)],
        out_shape=[jax.ShapeDtypeStruct((t, ATTN_DIM), BF16), jax.ShapeDtypeStruct((N_KV, t, LANE), F32)],
        compiler_params=_cparams(("parallel",)))(*os_, *lses)


def _attn_bwd_prep2(dmix, attn_out, tm=256):
    t = attn_out.shape[0]

    def kern(do_ref, o_ref, dot_ref, d_ref):
        do = do_ref[...]
        prod = do * o_ref[...].astype(F32)
        lo = _lane_lo(tm)
        lane = lax.broadcasted_iota(jnp.int32, (tm, LANE), 1)
        for kh in range(N_KV):
            blk, pb = do[:, kh * LANE:(kh + 1) * LANE], prod[:, kh * LANE:(kh + 1) * LANE]
            dot_ref[2 * kh] = jnp.where(lo, 0.0, _swap_halves_lanes(blk))
            dot_ref[2 * kh + 1] = jnp.where(lo, 0.0, blk)
            s_lo = jnp.sum(jnp.where(lo, pb, 0.0), axis=1, keepdims=True)
            s_hi = jnp.sum(pb, axis=1, keepdims=True) - s_lo
            d_ref[kh] = jnp.where(lane == 0, s_lo, jnp.where(lane == 1, s_hi, 0.0))

    blk = lambda n: pl.BlockSpec((n, tm, LANE), lambda i: (0, i, 0))
    return pl.pallas_call(
        kern, name="attn_bwd_prep", grid=(t // tm,),
        in_specs=[pl.BlockSpec((tm, ATTN_DIM), lambda i: (i, SSM_INNER // ATTN_DIM)), pl.BlockSpec((tm, ATTN_DIM), lambda i: (i, 0))],
        out_specs=[blk(N_QH), blk(N_KV)],
        out_shape=[jax.ShapeDtypeStruct((N_QH, t, LANE), F32), jax.ShapeDtypeStruct((N_KV, t, LANE), F32)],
        compiler_params=_cparams(("parallel",)))(dmix, attn_out)


def _attn_dq2(q, kv, dot, lse, dsum, dil):
    t = q.shape[1]
    sub, nb, cur, prev, _ = _att_specs(t, dil)

    def kern(q_ref, kvp_ref, kvc_ref, do_ref, lse_ref, d_ref, dq_ref):
        n = pl.program_id(1)
        for b in range(nb):
            mask = _band_qk((n == 0) if b == 0 else None)

            def unit(r, carry, b=b, mask=mask):
                rows = _att_rows(b, r, dil)
                kvp = kvc_ref[_att_rows(b - 1, r, dil), :] if b > 0 else kvp_ref[_att_rows(0, r, dil), :]
                kvcat = jnp.concatenate([kvp, kvc_ref[rows, :]], axis=0).astype(BF16)
                lse_t, d_t = lse_ref[rows, :], d_ref[rows, :]
                for g in range(2):
                    s = jnp.where(mask, _dot_nt(q_ref.at[g][rows, :].astype(BF16), kvcat), NEG)
                    p = jnp.exp(s - lse_t[:, g:g + 1])
                    dp = _dot_nt(do_ref.at[g][rows, :].astype(BF16), kvcat)
                    ds = p * (dp - d_t[:, g:g + 1])
                    dq_ref.at[g][rows, :] = _dot(ds.astype(BF16), kvcat)
                return carry
            _for_residues(dil, unit)

    return pl.pallas_call(
        kern, name=f"attn_dq_d{dil}", grid=(N_KV, t // ATT_SPAN),
        in_specs=[cur(2), prev(None), cur(None), cur(2), cur(None), cur(None)], out_specs=cur(2),
        out_shape=jax.ShapeDtypeStruct((N_QH, t, LANE), F32),
        compiler_params=_cparams(("parallel", "parallel")))(q, kv, kv, dot, lse, dsum)


def _attn_dkv2(q, kv, dot, lse, dsum, dil):
    t = q.shape[1]
    sub, nb, cur, _, nxt = _att_specs(t, dil)
    nsteps = t // ATT_SPAN

    def kern(kv_ref, qc_ref, qn_ref, doc_ref, don_ref, lc_ref, ln_ref, dc_ref, dn_ref, dkv_ref):
        n = pl.program_id(1)
        for b in range(nb):
            inside = b < nb - 1
            mask = _band_kq(None if inside else (n == nsteps - 1))

            def unit(r, carry, b=b, inside=inside, mask=mask):
                rows = _att_rows(b, r, dil)
                nrows = _att_rows(b + 1, r, dil) if inside else _att_rows(0, r, dil)
                kvb = kv_ref[rows, :].astype(BF16)
                follow = lambda cref, nref: (cref if inside else nref)[nrows, :]
                lse_t = jnp.concatenate([lc_ref[rows, :].T, follow(lc_ref, ln_ref).T], axis=1)
                d_t = jnp.concatenate([dc_ref[rows, :].T, follow(dc_ref, dn_ref).T], axis=1)
                acc = jnp.zeros((CHUNK, LANE), F32)
                for g in range(2):
                    qdo = jnp.concatenate([qc_ref.at[g][rows, :], follow(qc_ref.at[g], qn_ref.at[g]),
                                           doc_ref.at[g][rows, :], follow(doc_ref.at[g], don_ref.at[g])], axis=0).astype(BF16)
                    both = _dot_nt(kvb, qdo)
                    pt = jnp.exp(jnp.where(mask, both[:, :2 * CHUNK], NEG) - lse_t[g:g + 1, :])
                    dst = pt * (both[:, 2 * CHUNK:] - d_t[g:g + 1, :])
                    acc = acc + _dot(jnp.concatenate([dst, pt], axis=1).astype(BF16), qdo)
                dkv_ref[rows, :] = acc
                return carry
            _for_residues(dil, unit)

    return pl.pallas_call(
        kern, name=f"attn_dkv_d{dil}", grid=(N_KV, nsteps),
        in_specs=[cur(None), cur(2), nxt(2), cur(2), nxt(2), cur(None), nxt(None), cur(None), nxt(None)], out_specs=cur(None),
        out_shape=jax.ShapeDtypeStruct((N_KV, t, LANE), F32),
        compiler_params=_cparams(("parallel", "parallel")))(kv, q, q, dot, dot, lse, lse, dsum, dsum)


HALO = 8
SSM_CONV_TM, SSM_CONV_W = 512, 512
FFN_CONV_TM, FFN_CONV_W = 256, 1408


def _halo_specs(tm, width, t_rows, col_off=0, lead=None):
    per, last = tm // HALO, t_rows // HALO - 1
    row_maps = (lambda i: i, lambda i: jnp.maximum(i * per - 1, 0), lambda i: jnp.minimum((i + 1) * per, last))
    specs = []
    for rows, rm in zip((tm, HALO, HALO), row_maps):
        if lead is None:
            specs.append(pl.BlockSpec((rows, width), functools.partial(lambda c, i, rm: (rm(i), c + col_off), rm=rm)))
        else:
            specs.append(pl.BlockSpec((None, rows, width), functools.partial(lambda c, i, rm: (lead, rm(i), c + col_off), rm=rm)))
    return specs


def _fill_ext(buf, tile_ref, before_ref, after_ref, i, nt):
    tm = tile_ref.shape[0]
    buf[0:HALO, :] = jnp.where(i > 0, before_ref[...].astype(F32), 0.0)
    buf[HALO:HALO + tm, :] = tile_ref[...].astype(F32)
    if after_ref is not None:
        buf[HALO + tm:, :] = jnp.where(i < nt - 1, after_ref[...].astype(F32), 0.0)


CONV_RB, CONV_CW = 16, 256


def _lane_chunks(width):
    return [slice(c0, min(c0 + CONV_CW, width)) for c0 in range(0, width, CONV_CW)]


def _shifted(buf, taps, r0, rows, cs):
    return [buf[pl.ds(HALO - (taps - 1) + k + r0, rows), cs] for k in range(taps)]


def _taps_fwd(xs, w, b):
    acc = b
    for k, xk in enumerate(xs):
        acc = acc + w[k:k + 1, :] * xk
    return acc


def _taps_bwd(bufd, w, taps, r0, rows, cs):
    acc = None
    for k in range(taps):
        term = w[k:k + 1, :] * bufd[pl.ds(r0 + (taps - 1) - k, rows), cs]
        acc = term if acc is None else acc + term
    return acc


def _fold8(z):
    return z[:HALO] + z[HALO:] if z.shape[0] == 2 * HALO else z


def _silu_grad(pre):
    sg = _sigmoid(pre)
    return sg * (1.0 + pre * (1.0 - sg))


def _ssm_conv_fwd(proj, w, b):
    t = proj.shape[0]
    tm, wd = min(SSM_CONV_TM, t), SSM_CONV_W
    nt, taps = t // tm, w.shape[0]

    def kern(x_ref, xb_ref, w_ref, b_ref, o_ref, buf):
        _fill_ext(buf, x_ref, xb_ref, None, pl.program_id(1), nt)
        for cs in _lane_chunks(wd):
            wv, bv = w_ref[:, cs], b_ref[:, cs]
            for r0 in range(0, tm, CONV_RB):
                pre = _taps_fwd(_shifted(buf, taps, r0, CONV_RB, cs), wv, bv)
                o_ref[r0:r0 + CONV_RB, cs] = pre * _sigmoid(pre)

    tile, before, _ = _halo_specs(tm, wd, t)
    par = lambda rows: pl.BlockSpec((rows, wd), lambda c, i: (0, c))
    return pl.pallas_call(
        kern, name="ssm_conv_fwd", grid=(XBC_DIM // wd, nt), in_specs=[tile, before, par(taps), par(1)],
        out_specs=pl.BlockSpec((tm, wd), lambda c, i: (i, c)), out_shape=jax.ShapeDtypeStruct((t, XBC_DIM), F32),
        scratch_shapes=[pltpu.VMEM((tm + HALO, wd), F32)],
        compiler_params=_cparams(("parallel", "parallel")))(proj, proj, w, b)


def _ssm_conv_bwd(proj, w, b, dact, parts):
    t = proj.shape[0]
    tm, wd = min(SSM_CONV_TM, t), SSM_CONV_W
    nt, taps, ncol, ns = t // tm, w.shape[0], XBC_DIM // SSM_CONV_W, len(parts)

    def kern(x_ref, xb_ref, xa_ref, d_ref, dn_ref, w_ref, b_ref, *rest):
        dx_ref, gw_ref, gb_ref = rest[ns:ns + 3]
        buf, bufd = rest[2 * ns + 3:2 * ns + 5]
        i = pl.program_id(1)
        if ns:
            start, finish = _scatter_steps(ns, rest[:ns], rest[ns + 3:2 * ns + 3], rest[2 * ns + 5:])
            pl.when((pl.program_id(0) == 0) & (i == 0))(start)
            pl.when((pl.program_id(0) == ncol - 1) & (i == nt - 1))(finish)
        _fill_ext(buf, x_ref, xb_ref, xa_ref, i, nt)

        @pl.when(i == 0)
        def _():
            gw_ref[...] = jnp.zeros(gw_ref.shape, F32)
            gb_ref[...] = jnp.zeros(gb_ref.shape, F32)
        for cs in _lane_chunks(wd):
            wv, bv = w_ref[:, cs], b_ref[:, cs]
            acc = [jnp.zeros((HALO, cs.stop - cs.start), F32) for _ in range(taps + 1)]
            for r0 in list(range(0, tm, CONV_RB)) + [tm]:
                inside = r0 < tm
                rows = CONV_RB if inside else HALO
                xs = _shifted(buf, taps, r0, rows, cs)
                d = d_ref[r0:r0 + rows, cs] if inside else jnp.where(i < nt - 1, dn_ref[:, cs], 0.0)
                dpre = d * _silu_grad(_taps_fwd(xs, wv, bv))
                bufd[r0:r0 + rows, cs] = dpre
                if inside:
                    acc[taps] = acc[taps] + _fold8(dpre)
                    for k in range(taps):
                        acc[k] = acc[k] + _fold8(dpre * xs[k])
            gb_ref[:, cs] += jnp.sum(acc[taps], axis=0, keepdims=True)
            for k in range(taps):
                gw_ref[k:k + 1, cs] += jnp.sum(acc[k], axis=0, keepdims=True)
            for r0 in range(0, tm, CONV_RB):
                dx_ref[r0:r0 + CONV_RB, cs] = _taps_bwd(bufd, wv, taps, r0, CONV_RB, cs).astype(BF16)

    xt, xb, xa = _halo_specs(tm, wd, t)
    dt_, _, dn = _halo_specs(tm, wd, t)
    par = lambda rows: pl.BlockSpec((rows, wd), lambda c, i: (0, c))
    return pl.pallas_call(
        kern, name="ssm_conv_bwd", grid=(ncol, nt), in_specs=[xt, xb, xa, dt_, dn, par(taps), par(1)] + [ANY_SPEC] * ns,
        out_specs=[pl.BlockSpec((tm, wd), lambda c, i: (i, c)), par(taps), par(1)] + [ANY_SPEC] * ns,
        out_shape=[jax.ShapeDtypeStruct((t, XBC_DIM), BF16), jax.ShapeDtypeStruct((taps, XBC_DIM), F32),
                   jax.ShapeDtypeStruct((1, XBC_DIM), F32)] + [jax.ShapeDtypeStruct(s.shape, s.dtype) for s in parts],
        scratch_shapes=[pltpu.VMEM((tm + 2 * HALO, wd), F32), pltpu.VMEM((tm + HALO, wd), F32)] + (_scatter_sems(ns) if ns else []),
        compiler_params=_cparams(("arbitrary", "arbitrary")))(proj, proj, proj, dact, dact, w, b, *parts)


def _ffn_act_fwd(u, w, b):
    t = u.shape[1]
    tm, wd = min(FFN_CONV_TM, t), FFN_CONV_W
    nt, taps, nc = t // tm, w.shape[0], D_FF // FFN_CONV_W

    def kern(g_ref, gb_ref, v_ref, vb_ref, wg_ref, wv_ref, bg_ref, bv_ref, a_ref, bufg, bufv):
        i = pl.program_id(1)
        _fill_ext(bufg, g_ref, gb_ref, None, i, nt)
        _fill_ext(bufv, v_ref, vb_ref, None, i, nt)
        for cs in _lane_chunks(wd):
            wg, wv, bg, bv = wg_ref[:, cs], wv_ref[:, cs], bg_ref[:, cs], bv_ref[:, cs]
            for r0 in range(0, tm, CONV_RB):
                g = _taps_fwd(_shifted(bufg, taps, r0, CONV_RB, cs), wg, bg)
                v = _taps_fwd(_shifted(bufv, taps, r0, CONV_RB, cs), wv, bv)
                a_ref[r0:r0 + CONV_RB, cs] = (g * _sigmoid(g) * v).astype(BF16)

    gt, gbf, _ = _halo_specs(tm, wd, t, lead=0)
    vt, vbf, _ = _halo_specs(tm, wd, t, lead=1)
    par = lambda rows, off: pl.BlockSpec((rows, wd), functools.partial(lambda c, i, off: (0, c + off), off=off))
    return pl.pallas_call(
        kern, name="ffn_act_fwd", grid=(nc, nt),
        in_specs=[gt, gbf, vt, vbf, par(taps, 0), par(taps, nc), par(1, 0), par(1, nc)],
        out_specs=pl.BlockSpec((tm, wd), lambda c, i: (i, c)), out_shape=jax.ShapeDtypeStruct((t, D_FF), BF16),
        scratch_shapes=[pltpu.VMEM((tm + HALO, wd), F32)] * 2,
        compiler_params=_cparams(("parallel", "parallel")))(u, u, u, u, w, w, b, b)


def _ffn_act_bwd(u, w, b, da):
    t = u.shape[1]
    tm, wd = min(FFN_CONV_TM, t), FFN_CONV_W
    nt, taps, nc = t // tm, w.shape[0], D_FF // FFN_CONV_W

    def kern(g_ref, gb_ref, ga_ref, v_ref, vb_ref, va_ref, d_ref, dn_ref, wg_ref, wv_ref, bg_ref, bv_ref,
             du_ref, gwg_ref, gwv_ref, gbg_ref, gbv_ref, bufg, bufv, bufdg, bufdv):
        i = pl.program_id(1)
        _fill_ext(bufg, g_ref, gb_ref, ga_ref, i, nt)
        _fill_ext(bufv, v_ref, vb_ref, va_ref, i, nt)

        @pl.when(i == 0)
        def _():
            for r in (gwg_ref, gwv_ref, gbg_ref, gbv_ref):
                r[...] = jnp.zeros(r.shape, F32)
        for cs in _lane_chunks(wd):
            wg, wv, bg, bv = wg_ref[:, cs], wv_ref[:, cs], bg_ref[:, cs], bv_ref[:, cs]
            zero = jnp.zeros((HALO, cs.stop - cs.start), F32)
            accg, accv = [zero] * (taps + 1), [zero] * (taps + 1)
            for r0 in list(range(0, tm, CONV_RB)) + [tm]:
                inside = r0 < tm
                rows = CONV_RB if inside else HALO
                xg, xv = _shifted(bufg, taps, r0, rows, cs), _shifted(bufv, taps, r0, rows, cs)
                g, v = _taps_fwd(xg, wg, bg), _taps_fwd(xv, wv, bv)
                dav = d_ref[r0:r0 + rows, cs] if inside else jnp.where(i < nt - 1, dn_ref[:, cs], 0.0)
                sg = _sigmoid(g)
                dg = dav * v * (sg * (1.0 + g * (1.0 - sg)))
                dv = dav * (g * sg)
                bufdg[r0:r0 + rows, cs] = dg
                bufdv[r0:r0 + rows, cs] = dv
                if inside:
                    accg[taps], accv[taps] = accg[taps] + _fold8(dg), accv[taps] + _fold8(dv)
                    for k in range(taps):
                        accg[k], accv[k] = accg[k] + _fold8(dg * xg[k]), accv[k] + _fold8(dv * xv[k])
            gbg_ref[:, cs] += jnp.sum(accg[taps], axis=0, keepdims=True)
            gbv_ref[:, cs] += jnp.sum(accv[taps], axis=0, keepdims=True)
            for k in range(taps):
                gwg_ref[k:k + 1, cs] += jnp.sum(accg[k], axis=0, keepdims=True)
                gwv_ref[k:k + 1, cs] += jnp.sum(accv[k], axis=0, keepdims=True)
            for r0 in range(0, tm, CONV_RB):
                du_ref[0, r0:r0 + CONV_RB, cs] = _taps_bwd(bufdg, wg, taps, r0, CONV_RB, cs).astype(BF16)
                du_ref[1, r0:r0 + CONV_RB, cs] = _taps_bwd(bufdv, wv, taps, r0, CONV_RB, cs).astype(BF16)

    gt, gbf, gaf = _halo_specs(tm, wd, t, lead=0)
    vt, vbf, vaf = _halo_specs(tm, wd, t, lead=1)
    dt_, _, dn = _halo_specs(tm, wd, t)
    par = lambda rows, off: pl.BlockSpec((rows, wd), functools.partial(lambda c, i, off: (0, c + off), off=off))
    return pl.pallas_call(
        kern, name="ffn_act_bwd", grid=(nc, nt),
        in_specs=[gt, gbf, gaf, vt, vbf, vaf, dt_, dn, par(taps, 0), par(taps, nc), par(1, 0), par(1, nc)],
        out_specs=[pl.BlockSpec((2, tm, wd), lambda c, i: (0, i, c)), par(taps, 0), par(taps, 0), par(1, 0), par(1, 0)],
        out_shape=[jax.ShapeDtypeStruct((2, t, D_FF), BF16)] + [jax.ShapeDtypeStruct((taps, D_FF), F32)] * 2
        + [jax.ShapeDtypeStruct((1, D_FF), F32)] * 2,
        scratch_shapes=[pltpu.VMEM((tm + 2 * HALO, wd), F32)] * 2 + [pltpu.VMEM((tm + HALO, wd), F32)] * 2,
        compiler_params=_cparams(("parallel", "arbitrary")))(u, u, u, u, u, u, da, da, w, w, b, b)


def _softplus(x):
    e = jnp.exp(-jnp.abs(x))
    return jnp.maximum(x, 0.0) + jnp.where(e < 1e-4, e - 0.5 * e * e, jnp.log(1.0 + e))


def _tri(lower):
    r = lax.broadcasted_iota(jnp.int32, (CHUNK, CHUNK), 0)
    c = lax.broadcasted_iota(jnp.int32, (CHUNK, CHUNK), 1)
    return (r >= c) if lower else (r <= c)


def _cum(mat_bool, x):
    return jnp.dot(mat_bool.astype(F32), x, precision=lax.Precision.HIGHEST, preferred_element_type=F32)


def _pair_sel(lane_lo, tile, h0):
    return jnp.where(lane_lo, tile[:, h0:h0 + 1], tile[:, h0 + 1:h0 + 2])


def _ssd_fwd(xbc_act, proj, dt_bias_p, a_log_p, dskip_t, shards):
    t = xbc_act.shape[0]
    nch = t // CHUNK
    ns = len(shards)

    def kern(xa_ref, dtr_ref, bias_ref, alog_ref, dsk_ref, *rest):
        y_ref, dt_ref, hs_ref = rest[ns:ns + 3]
        hst = rest[2 * ns + 3]
        if ns:
            start, forward, finish = _gather_steps(shards, rest[:ns], rest[ns + 3:2 * ns + 3], rest[2 * ns + 4:])
            pl.when(pl.program_id(0) == 0)(start)
            pl.when(pl.program_id(0) == (3 * nch) // 4)(forward)
            pl.when(pl.program_id(0) == nch - 1)(finish)

        @pl.when(pl.program_id(0) == 0)
        def _():
            hst[...] = jnp.zeros(hst.shape, F32)
        dt = _softplus(dtr_ref[...] + bias_ref[...])
        dt_ref[...] = dt
        acum = _cum(_tri(True), dt * (-jnp.exp(alog_ref[...])))
        acum_t = acum.T
        ea = jnp.exp(acum)
        a_last = acum[CHUNK - 1:CHUNK, :]
        dend = jnp.exp(a_last - acum)
        ea_last = jnp.exp(a_last)
        causal = _tri(True)
        lane_lo = lax.broadcasted_iota(jnp.int32, (CHUNK, LANE), 1) < HEAD_DIM
        row_lo = lax.broadcasted_iota(jnp.int32, (CHUNK, LANE), 0) < HEAD_DIM
        for g in range(2):
            bg = xa_ref[:, SSM_INNER + g * SSM_STATE:SSM_INNER + (g + 1) * SSM_STATE].astype(BF16)
            cg = xa_ref[:, SSM_INNER + BC_DIM + g * SSM_STATE:SSM_INNER + BC_DIM + (g + 1) * SSM_STATE].astype(BF16)
            cb = _dot_nt(cg, bg)
            for j in range(4 * g, 4 * g + 4):
                h0 = 2 * j
                cols = slice(j * LANE, (j + 1) * LANE)
                xp = xa_ref[:, cols]
                xdt = xp * _pair_sel(lane_lo, dt, h0)
                ydiag = None
                for hh, sel in ((h0, lane_lo), (h0 + 1, ~lane_lo)):
                    seg = acum[:, hh:hh + 1] - acum_t[hh:hh + 1, :]
                    mm = (cb * jnp.where(causal, jnp.exp(jnp.minimum(seg, 0.0)), 0.0)).astype(BF16)
                    d = _dot(mm, jnp.where(sel, xdt, 0.0).astype(BF16))
                    ydiag = d if ydiag is None else ydiag + d
                hp = hst[cols, :]
                hs_ref[cols, :] = hp
                yoff = _dot_nt(cg, hp.astype(BF16)) * _pair_sel(lane_lo, ea, h0)
                y_ref[:, cols] = ydiag + yoff + dsk_ref[:, cols] * xp
                xw = (xdt * _pair_sel(lane_lo, dend, h0)).astype(BF16)
                rowf = jnp.where(row_lo, ea_last[:, h0:h0 + 1], ea_last[:, h0 + 1:h0 + 2])
                hst[cols, :] = hp * rowf + _dot_tn(xw, bg)

    return pl.pallas_call(
        kern, name="ssd_fwd", grid=(nch,),
        in_specs=[pl.BlockSpec((CHUNK, XBC_DIM), lambda c: (c, 0)), pl.BlockSpec((CHUNK, LANE), lambda c: (c, OFF_DT // LANE)),
                  pl.BlockSpec((1, LANE), lambda c: (0, 0)), pl.BlockSpec((1, LANE), lambda c: (0, 0)),
                  pl.BlockSpec((1, SSM_INNER), lambda c: (0, 0))] + [ANY_SPEC] * ns,
        out_specs=[pl.BlockSpec((CHUNK, SSM_INNER), lambda c: (c, 0)), pl.BlockSpec((CHUNK, LANE), lambda c: (c, 0)),
                   pl.BlockSpec((None, SSM_INNER, SSM_STATE), lambda c: (c, 0, 0))] + [ANY_SPEC] * ns,
        out_shape=[jax.ShapeDtypeStruct((t, SSM_INNER), F32), jax.ShapeDtypeStruct((t, LANE), F32),
                   jax.ShapeDtypeStruct((nch, SSM_INNER, SSM_STATE), F32)] + _gather_out_shapes(shards),
        scratch_shapes=[pltpu.VMEM((SSM_INNER, SSM_STATE), F32)] + (_gather_sems(ns) if ns else []),
        compiler_params=_cparams(("arbitrary",)))(xbc_act, proj, dt_bias_p, a_log_p, dskip_t, *shards)


def _ssd_bwd(xbc_act, proj, dt_sp, hstates, dy, dt_bias_p, a_log_p, dskip_t):
    t = xbc_act.shape[0]
    nch = t // CHUNK

    pair = jnp.arange(SSM_HEADS // 2)[:, None, None]
    psel = (jnp.arange(LANE)[None, None, :] == 2 * pair + (jnp.arange(LANE) // HEAD_DIM)[None, :, None]).astype(BF16)

    def kern(xa_ref, dtr_ref, dt_ref, hs_ref, dy_ref, bias_ref, alog_ref, dsk_ref, psel_ref,
             dact_ref, ddtr_ref, da_ref, dbias_ref, ddsk_ref, dh):
        @pl.when(pl.program_id(0) == 0)
        def _():
            dh[...] = jnp.zeros(dh.shape, F32)
            for r in (da_ref, dbias_ref, ddsk_ref):
                r[...] = jnp.zeros(r.shape, F32)
        dt = dt_ref[...]
        a_neg = -jnp.exp(alog_ref[...])
        acum = _cum(_tri(True), dt * a_neg)
        acum_t = acum.T
        ea = jnp.exp(acum)
        a_last = acum[CHUNK - 1:CHUNK, :]
        dend = jnp.exp(a_last - acum)
        ea_last = jnp.exp(a_last)
        causal = _tri(True)
        lane = lax.broadcasted_iota(jnp.int32, (CHUNK, LANE), 1)
        rowi = lax.broadcasted_iota(jnp.int32, (CHUNK, LANE), 0)
        lane_lo, row_lo, last_row = lane < HEAD_DIM, rowi < HEAD_DIM, rowi == CHUNK - 1
        d_dt = jnp.zeros((CHUNK, LANE), F32)
        d_acum = jnp.zeros((CHUNK, LANE), F32)
        for g in range(2):
            bcols = slice(SSM_INNER + g * SSM_STATE, SSM_INNER + (g + 1) * SSM_STATE)
            ccols = slice(SSM_INNER + BC_DIM + g * SSM_STATE, SSM_INNER + BC_DIM + (g + 1) * SSM_STATE)
            bg, cg = xa_ref[:, bcols].astype(BF16), xa_ref[:, ccols].astype(BF16)
            cb = _dot_nt(cg, bg)
            dg_sum = jnp.zeros((CHUNK, CHUNK), F32)
            dcg = jnp.zeros((CHUNK, SSM_STATE), F32)
            dbg = jnp.zeros((CHUNK, SSM_STATE), F32)
            for j in range(4 * g, 4 * g + 4):
                h0 = 2 * j
                cols = slice(j * LANE, (j + 1) * LANE)
                xp, dyp = xa_ref[:, cols], dy_ref[:, cols]
                dtsel = _pair_sel(lane_lo, dt, h0)
                xdt = xp * dtsel
                xdt_b = xdt.astype(BF16)
                hp, dhp = hs_ref[cols, :], dh[cols, :]
                hp_b, dhp_b = hp.astype(BF16), dhp.astype(BF16)
                easel, dendsel = _pair_sel(lane_lo, ea, h0), _pair_sel(lane_lo, dend, h0)
                dx, ydiag = None, None
                for hh, sel in ((h0, lane_lo), (h0 + 1, ~lane_lo)):
                    dyh = jnp.where(sel, dyp, 0.0).astype(BF16)
                    seg = acum[:, hh:hh + 1] - acum_t[hh:hh + 1, :]
                    dec = jnp.where(causal, jnp.exp(jnp.minimum(seg, 0.0)), 0.0)
                    mm_b = (cb * dec).astype(BF16)
                    dg_sum = dg_sum + dec * _dot_nt(dyh, xdt_b)
                    d = _dot_tn(mm_b, dyh)
                    y = _dot(mm_b, jnp.where(sel, xdt, 0.0).astype(BF16))
                    dx = d if dx is None else dx + d
                    ydiag = y if ydiag is None else ydiag + y
                g2 = _dot_nt(bg, dhp_b)
                tprod = xdt * g2 * dendsel
                yoff = _dot_nt(cg, hp_b) * easel
                yc = dyp.astype(BF16).astype(F32) * ydiag + dyp * yoff - (xdt_b.astype(F32) * dx + tprod)
                dx = dx + g2 * dendsel
                psel = psel_ref[j]
                t_lo = jnp.sum(jnp.where(lane_lo, tprod, 0.0), keepdims=True).reshape(1, 1)
                t_hi = jnp.sum(tprod, keepdims=True).reshape(1, 1) - t_lo
                hh_prod = dhp * hp
                s_lo = jnp.sum(jnp.where(row_lo, hh_prod, 0.0), keepdims=True).reshape(1, 1)
                s_hi = jnp.sum(hh_prod, keepdims=True).reshape(1, 1) - s_lo
                end_lo = ea_last[:, h0:h0 + 1] * s_lo + t_lo
                end_hi = ea_last[:, h0 + 1:h0 + 2] * s_hi + t_hi
                ends = jnp.where(lane == h0, end_lo, jnp.where(lane == h0 + 1, end_hi, 0.0))
                d_acum = d_acum + _dot_split(yc, psel) + jnp.where(last_row, ends, 0.0)
                dye = (dyp * easel).astype(BF16)
                dcg = dcg + _dot(dye, hp_b)
                dbg = dbg + _dot((xdt * dendsel).astype(BF16), dhp_b)
                rowf = jnp.where(row_lo, ea_last[:, h0:h0 + 1], ea_last[:, h0 + 1:h0 + 2])
                dh[cols, :] = dhp * rowf + _dot_tn(dye, cg)
                dact_ref[:, cols] = dx * dtsel + dsk_ref[:, cols] * dyp
                d_dt = d_dt + _dot_split(dx * xp, psel)
                ddsk_ref[:, cols] += jnp.sum(dyp * xp, axis=0, keepdims=True)
            dg_b = dg_sum.astype(BF16)
            dact_ref[:, ccols] = dcg + _dot(dg_b, bg)
            dact_ref[:, bcols] = dbg + _dot_tn(dg_b, cg)
        d_adt = _cum(_tri(False), d_acum)
        d_dt = d_dt + d_adt * a_neg
        da_ref[...] += jnp.sum(d_adt * dt, axis=0, keepdims=True)
        d_raw = jnp.where(lane < SSM_HEADS, d_dt * _sigmoid(dtr_ref[...] + bias_ref[...]), 0.0)
        ddtr_ref[...] = d_raw.astype(BF16)
        dbias_ref[...] += jnp.sum(d_raw, axis=0, keepdims=True)

    rev = lambda c: (nch - 1 - c, 0)
    return pl.pallas_call(
        kern, name="ssd_bwd", grid=(nch,),
        in_specs=[pl.BlockSpec((CHUNK, XBC_DIM), rev), pl.BlockSpec((CHUNK, LANE), lambda c: (nch - 1 - c, OFF_DT // LANE)),
                  pl.BlockSpec((CHUNK, LANE), rev), pl.BlockSpec((None, SSM_INNER, SSM_STATE), lambda c: (nch - 1 - c, 0, 0)),
                  pl.BlockSpec((CHUNK, SSM_INNER), rev),
                  pl.BlockSpec((1, LANE), lambda c: (0, 0)), pl.BlockSpec((1, LANE), lambda c: (0, 0)),
                  pl.BlockSpec((1, SSM_INNER), lambda c: (0, 0)), pl.BlockSpec(psel.shape, lambda c: (0, 0, 0))],
        out_specs=[pl.BlockSpec((CHUNK, XBC_DIM), rev), pl.BlockSpec((CHUNK, LANE), rev),
                   pl.BlockSpec((1, LANE), lambda c: (0, 0)), pl.BlockSpec((1, LANE), lambda c: (0, 0)),
                   pl.BlockSpec((1, SSM_INNER), lambda c: (0, 0))],
        out_shape=[jax.ShapeDtypeStruct((t, XBC_DIM), F32), jax.ShapeDtypeStruct((t, LANE), BF16),
                   jax.ShapeDtypeStruct((1, LANE), F32), jax.ShapeDtypeStruct((1, LANE), F32),
                   jax.ShapeDtypeStruct((1, SSM_INNER), F32)],
        scratch_shapes=[pltpu.VMEM((SSM_INNER, SSM_STATE), F32)],
        compiler_params=_cparams(("arbitrary",)))(xbc_act, proj, dt_sp, hstates, dy, dt_bias_p, a_log_p, dskip_t, psel)


def _ssm_post_fwd(y, proj, g):
    def body(y_ref, z_ref, g_ref, o_ref):
        z = z_ref[...]
        yz = y_ref[...] * (z * _sigmoid(z))
        r = lax.rsqrt(jnp.mean(yz * yz, axis=-1, keepdims=True) + EPS)
        o_ref[...] = (yz * r * g_ref[...]).astype(BF16)
    return _rows("ssm_post_fwd", body, [("t", y), ("tc", proj, SSM_INNER, OFF_Z // SSM_INNER), ("p", g)],
                 [(SSM_INNER, BF16)])[0]


def _ssm_post_bwd(dmix, y, proj, g):
    def body(do_ref, y_ref, z_ref, g_ref, dy_ref, dz_ref, dg_ref):
        z, yv, dout = z_ref[...], y_ref[...], do_ref[...]
        sg = _sigmoid(z)
        gz = z * sg
        yz = yv * gz
        r = lax.rsqrt(jnp.mean(yz * yz, axis=-1, keepdims=True) + EPS)
        gd = dout * g_ref[...]
        dyz = r * gd - yz * (r * r * r * jnp.mean(yz * gd, axis=-1, keepdims=True))
        dy_ref[...] = dyz * gz
        dz_ref[...] = (dyz * yv * (sg * (1.0 + z * (1.0 - sg)))).astype(BF16)
        dg_ref[...] += jnp.sum(dout * yz * r, axis=0, keepdims=True)
    return _rows("ssm_post_bwd", body,
                 [("tc", dmix, SSM_INNER, 0), ("t", y), ("tc", proj, SSM_INNER, OFF_Z // SSM_INNER), ("p", g)],
                 [(SSM_INNER, F32), (SSM_INNER, BF16)], accs=[(1, SSM_INNER)])


def _ple_loss(gl, pp, x2, tgt):
    d = x2.shape[1]

    def body(gl_ref, pp_ref, x_ref, t_ref, dy_ref, dgl_ref, dpp_ref, sq_ref):
        s = _sigmoid(gl_ref[...])
        ppv = pp_ref[...]
        diff = x_ref[...] + s * ppv - t_ref[...]
        dy = diff * (1.0 / d)
        dy_ref[...] = dy
        dgl_ref[...] = (dy * ppv * s * (1.0 - s)).astype(BF16)
        dpp_ref[...] = (dy * s).astype(BF16)
        sq_ref[...] += jnp.sum(diff * diff, axis=0, keepdims=True)
    return _rows("ple_loss", body, [("t", gl), ("t", pp), ("t", x2), ("t", tgt)], [(d, F32), (d, BF16), (d, BF16)],
                 accs=[(1, d)])


def _pad_lanes(v, width=LANE):
    return jnp.pad(v, ((0, 0), (0, width - v.shape[1])))


def _local_step(x, p, tgt, wts, late_shards=(), join_late=None, reduce_early=None, reduce_late=None):
    g_attn, g_ssm, g_ffn, g_ple = wts["attn_norm_g"], wts["ssm_norm_g"], wts["ffn_norm_g"], wts["ple_norm_g"]
    w_in_p = wts["w_in_p"]
    gq_t = jnp.tile(wts["q_norm_g"], (1, ATTN_DIM // HEAD_DIM))
    gk_t = jnp.tile(wts["k_norm_g"], (1, KV_DIM // HEAD_DIM))
    dt_bias_p, a_log_p = _pad_lanes(wts["dt_bias"]), _pad_lanes(wts["a_log"])
    dskip_t = jnp.repeat(wts["d_skip"], HEAD_DIM, axis=1)

    h1, proj = _norm_mm("in_proj", x, g_attn, w_in_p)
    q_hm, kv_hm = _qknorm_fwd2(proj, gq_t, gk_t)
    pats = [_attn_fwd2(q_hm, kv_hm, d) for d in DILATIONS]
    attn_out, lse = _attn_merge2([o for o, _ in pats], [l for _, l in pats])
    xbc_act = _ssm_conv_fwd(proj, wts["ssm_conv_w"], wts["ssm_conv_b"])
    y_ssd, dt_sp, hstates, *gathered = _ssd_fwd(xbc_act, proj, dt_bias_p, a_log_p, dskip_t, list(late_shards))
    if join_late is not None:
        wts = {**wts, **join_late(gathered)}
    w_out_s, w_out_a = wts["w_out_ssm"], wts["w_out_attn"]
    w_up, w_down, w_gate, w_proj = wts["w_up"], wts["w_down"], wts["w_ple_gate"], wts["w_ple_proj"]
    ssm_out = _ssm_post_fwd(y_ssd, proj, g_ssm)
    x1 = _mm_nn("out_proj", [(ssm_out, w_out_s), (attn_out, w_out_a)], F32, res=x, tm=1024)
    h2 = _rms_fwd("rms_ffn", x1, g_ffn)
    u = _mm_nn("ffn_up", [(h2, w_up)], F32, tm=1024, tn=1408, halves=True)
    a = _ffn_act_fwd(u, wts["ffn_conv_w"], wts["ffn_conv_b"])
    x2 = _mm_nn("ffn_down", [(a, w_down)], F32, res=x1, tm=1024, tn=512)
    h3, gl = _norm_mm("ple_gate", x2, g_ple, w_gate, tm=1024)
    pb = p.astype(BF16)
    pp = _mm_nn("ple_proj", [(pb, w_proj)], F32, tm=2048)
    dy, dgl, dpp, sq = _ple_loss(gl, pp, x2, tgt)

    grads = {}
    grads["w_ple_proj"] = _mm_tn("g_ple_proj", pb, dpp, tn=PLE_DIM, chip_cols=True)
    grads["w_ple_gate"] = _mm_tn("g_ple_gate", h3, dgl)
    dx2, dx2b, grads["ple_norm_g"] = _mm_nt_rms_bwd("d_h3", dgl, w_gate, x2, g_ple, dy)
    da = _mm_nt("d_ffn_act", [(dx2b, w_down, 0)], F32, tm=1024, tn=1408)
    grads["w_down"] = _mm_tn("g_ffn_down", a, dx2b, tm=1408)
    du, gwg, gwv, gbg, gbv = _ffn_act_bwd(u, wts["ffn_conv_w"], wts["ffn_conv_b"], da)
    grads["ffn_conv_w"] = jnp.concatenate([gwg, gwv], axis=1)
    grads["ffn_conv_b"] = jnp.concatenate([gbg, gbv], axis=1)
    grads["w_up"] = _mm_tn("g_ffn_up", h2, du, tn=1408, chip_cols=True)
    dh2 = _mm_nt("d_h2", [(du, w_up, 0, 0), (du, w_up, 1, 1)], F32, tm=1024, tn=512)
    dx1, dx1b, grads["ffn_norm_g"] = _rms_bwd("rms_ffn_bwd", dh2, x1, g_ffn, dx2)
    dmix = _mm_nt("d_mix", [(dx1b, jnp.concatenate([w_out_s, w_out_a], axis=0), 0)], F32, tm=1024)
    grads["w_out"] = jnp.concatenate([_mm_tn("g_out_attn", attn_out, dx1b), _mm_tn("g_out_ssm", ssm_out, dx1b)], axis=0)
    dy_ssd, dz, grads["ssm_norm_g"] = _ssm_post_bwd(dmix, y_ssd, proj, g_ssm)
    dact, ddtr, d_a, d_bias, d_dsk = _ssd_bwd(xbc_act, proj, dt_sp, hstates, dy_ssd, dt_bias_p, a_log_p, dskip_t)
    grads["dt_bias"] = d_bias[:, :SSM_HEADS]
    grads["a_log"] = d_a[:, :SSM_HEADS] * (-jnp.exp(wts["a_log"]))
    grads["d_skip"] = jnp.sum(d_dsk.reshape(SSM_HEADS, HEAD_DIM), axis=1)[None, :]
    chip_sums = reduce_early(grads) if reduce_early is not None else []
    dxbc, grads["ssm_conv_w"], grads["ssm_conv_b"], *scattered = _ssm_conv_bwd(proj, wts["ssm_conv_w"], wts["ssm_conv_b"], dact,
                                                                                chip_sums)
    do_hm, dsum = _attn_bwd_prep2(dmix, attn_out)
    dqs = [_attn_dq2(q_hm, kv_hm, do_hm, lse, dsum, d) for d in DILATIONS]
    dkvs = [_attn_dkv2(q_hm, kv_hm, do_hm, lse, dsum, d) for d in DILATIONS]
    dq, dk, dv, dgq, dgk = _qknorm_bwd2(proj, gq_t, gk_t, dqs, dkvs)
    grads["q_norm_g"] = jnp.sum(dgq.reshape(ATTN_DIM // HEAD_DIM, HEAD_DIM), axis=0)[None, :]
    grads["k_norm_g"] = jnp.sum(dgk.reshape(KV_DIM // HEAD_DIM, HEAD_DIM), axis=0)[None, :]
    dproj = jnp.concatenate([dxbc, dq, dz, dk, dv, ddtr], axis=1)
    grads["w_in_p"] = _mm_tn("g_in_proj", h1, dproj, tm=512)
    late_sums = reduce_late(grads) if reduce_late is not None else []
    grad_x, _, grads["attn_norm_g"], *late_scattered = _mm_nt_rms_bwd("d_h1", dproj, w_in_p, x, g_attn, dx1, late_sums)
    return sq, grad_x, grads, (chip_sums, scattered), (late_sums, late_scattered)


MESH_IDS = pl.DeviceIdType.MESH
N_CHIPS = 4
ANY_SPEC = pl.BlockSpec(memory_space=pl.ANY)
SMALL_ROWS = 96
ALL_SMALL_ROWS = 272


def _place():
    x, y, c = lax.axis_index("x"), lax.axis_index("y"), lax.axis_index("c")
    return x, y, c, [(1 - x, y), (x, 1 - y), (1 - x, 1 - y)]


def _gather_over_chips(arrs):
    n = len(arrs)

    def body(*refs):
        steps = _gather_steps(arrs, refs[:n], refs[n:2 * n], refs[2 * n:2 * n + 4])
        for step in steps:
            step()

    return pl.pallas_call(
        body, name="gather_weights", in_specs=[ANY_SPEC] * n, out_specs=[ANY_SPEC] * n,
        out_shape=_gather_out_shapes(arrs), scratch_shapes=_gather_sems(n))(*arrs)


def _gather_out_shapes(arrs):
    return [jax.ShapeDtypeStruct((N_CHIPS,) + a.shape, a.dtype) for a in arrs]


def _gather_sems(n):
    return [pltpu.SemaphoreType.DMA((3 * n,))] * 4


def _gather_steps(arrs, ins, outs, sems):
    n = len(arrs)
    split = [a.shape[0] % 64 == 0 for a in arrs]
    ici_send, ici_recv, d2d_send, d2d_recv = sems

    def place():
        x, y, c, chips = _place()
        return x, y, c, chips, 2 * x + y

    def part(ref, a, core):
        if not split[a]:
            return ref
        half = arrs[a].shape[0] // 2
        return ref.at[pl.ds(core * half, half)]

    def ici(a, k, slot, where):
        x, y, c, chips, _ = where
        px, py = chips[k]
        return pltpu.make_async_remote_copy(
            src_ref=part(ins[a], a, c), dst_ref=part(outs[a].at[slot], a, c), send_sem=ici_send.at[3 * a + k],
            recv_sem=ici_recv.at[3 * a + k], device_id=(px, py, c), device_id_type=MESH_IDS)

    def d2d(a, k, core, where):
        x, y, c, chips, _ = where
        px, py = chips[k]
        piece = part(outs[a].at[2 * px + py], a, core)
        return pltpu.make_async_remote_copy(src_ref=piece, dst_ref=piece, send_sem=d2d_send.at[3 * a + k],
                                            recv_sem=d2d_recv.at[3 * a + k], device_id=(x, y, 1 - c), device_id_type=MESH_IDS)

    def start():
        w = place()
        for a in range(n):
            for k in range(3):
                ici(a, k, w[4], w).start()

    def forward():
        w = place()
        for a in range(n):
            for k, (px, py) in enumerate(w[3]):
                ici(a, k, 2 * px + py, w).wait_recv()
                if split[a]:
                    d2d(a, k, w[2], w).start()

    def finish():
        w = place()
        for a in range(n):
            for k in range(3):
                if split[a]:
                    d2d(a, k, 1 - w[2], w).wait_recv()
                    d2d(a, k, w[2], w).wait_send()
                ici(a, k, w[4], w).wait_send()

    return start, forward, finish


def _row_tile(rows, cap=256):
    return max(d for d in range(8, cap + 1, 8) if rows % d == 0)


def _swap_halves(name, gs):
    n = len(gs)

    def body(*refs):
        ins, outs, send, recv = refs[:n], refs[n:2 * n], refs[2 * n], refs[2 * n + 1]
        x, y, c, _ = _place()
        cps = []
        for a in range(n):
            half = gs[a].shape[1] // 2
            for q in range(N_CHIPS):
                cps.append(pltpu.make_async_remote_copy(
                    src_ref=ins[a].at[q, pl.ds((1 - c) * half, half)], dst_ref=outs[a].at[q], send_sem=send.at[N_CHIPS * a + q],
                    recv_sem=recv.at[N_CHIPS * a + q], device_id=(x, y, 1 - c), device_id_type=MESH_IDS))
        for cp in cps:
            cp.start()
        for cp in cps:
            cp.wait()

    return pl.pallas_call(
        body, name=name, in_specs=[ANY_SPEC] * n, out_specs=[ANY_SPEC] * n,
        out_shape=[jax.ShapeDtypeStruct((N_CHIPS, g.shape[1] // 2, g.shape[2]), g.dtype) for g in gs],
        scratch_shapes=[pltpu.SemaphoreType.DMA((N_CHIPS * n,))] * 2)(*gs)


def _add_halves(name, g, got, c_idx):
    rows, cols = got.shape[1:]
    tm = _row_tile(rows)
    per = rows // tm

    def kern(c_ref, g_ref, r_ref, o_ref):
        o_ref[...] = (g_ref[...] + r_ref[...]).astype(BF16)

    return pl.pallas_call(
        kern, name=name,
        grid_spec=pltpu.PrefetchScalarGridSpec(
            num_scalar_prefetch=1, grid=(N_CHIPS, per),
            in_specs=[pl.BlockSpec((None, tm, cols), lambda q, i, c_ref: (q, c_ref[0] * per + i, 0)),
                      pl.BlockSpec((None, tm, cols), lambda q, i, c_ref: (q, i, 0))],
            out_specs=pl.BlockSpec((None, tm, cols), lambda q, i, c_ref: (q, i, 0))),
        out_shape=jax.ShapeDtypeStruct((N_CHIPS, rows, cols), BF16),
        compiler_params=_cparams(("parallel", "parallel")))(c_idx, g, got)


def _scatter_over_chips(ss):
    n = len(ss)

    def body(*refs):
        for step in _scatter_steps(n, refs[:n], refs[n:2 * n], refs[2 * n:2 * n + 2]):
            step()

    return pl.pallas_call(
        body, name="grad_scatter_chips", in_specs=[ANY_SPEC] * n, out_specs=[ANY_SPEC] * n,
        out_shape=[jax.ShapeDtypeStruct(s.shape, s.dtype) for s in ss], scratch_shapes=_scatter_sems(n))(*ss)


def _scatter_sems(n):
    return [pltpu.SemaphoreType.DMA((3 * n,))] * 2


def _scatter_steps(n, ins, outs, sems):
    send, recv = sems

    def copy(a, k, slot, where):
        x, y, c, chips = where
        px, py = chips[k]
        return pltpu.make_async_remote_copy(src_ref=ins[a].at[2 * px + py], dst_ref=outs[a].at[slot], send_sem=send.at[3 * a + k],
                                            recv_sem=recv.at[3 * a + k], device_id=(px, py, c), device_id_type=MESH_IDS)

    def start():
        w = _place()
        for a in range(n):
            for k in range(3):
                copy(a, k, 2 * w[0] + w[1], w).start()

    def finish():
        w = _place()
        for a in range(n):
            for k, (px, py) in enumerate(w[3]):
                copy(a, k, 2 * px + py, w).wait()

    return start, finish


def _sum_chips(name, own, parts, idx):
    rows, cols = parts.shape[1:]
    tm = _row_tile(rows)
    per = rows // tm

    def kern(o_idx, a_ref, b_ref, c_ref, d_ref, o_ref):
        o_ref[...] = ((a_ref[...].astype(F32) + b_ref[...].astype(F32)) + c_ref[...].astype(F32)) + d_ref[...].astype(F32)

    def spec(k):
        return pl.BlockSpec((None, tm, cols), functools.partial(lambda i, o_idx, k: (o_idx[k], i, 0), k=k))

    return pl.pallas_call(
        kern, name=name,
        grid_spec=pltpu.PrefetchScalarGridSpec(
            num_scalar_prefetch=1, grid=(per,), in_specs=[spec(0), spec(1), spec(2), spec(3)],
            out_specs=pl.BlockSpec((None, tm, cols), lambda i, o_idx: (0, o_idx[4] * per + i, 0))),
        out_shape=jax.ShapeDtypeStruct((1, 2 * rows, cols), F32), compiler_params=_cparams(("parallel",)))(idx, own, parts, parts, parts)


def _share_with_sibling(gs):
    n = len(gs)

    def body(*refs):
        ins, send, recv = refs[:n], refs[2 * n], refs[2 * n + 1]
        x, y, c, _ = _place()
        cps = []
        for a in range(n):
            half = gs[a].shape[1] // 2
            mine = pl.ds(c * half, half)
            cps.append(pltpu.make_async_remote_copy(src_ref=ins[a].at[0, mine], dst_ref=refs[n + a].at[0, mine], send_sem=send.at[a],
                                                    recv_sem=recv.at[a], device_id=(x, y, 1 - c), device_id_type=MESH_IDS))
        for cp in cps:
            cp.start()
        for cp in cps:
            cp.wait()

    return pl.pallas_call(
        body, name="grad_share_sibling", in_specs=[ANY_SPEC] * n, out_specs=[ANY_SPEC] * n,
        out_shape=[jax.ShapeDtypeStruct(g.shape, g.dtype) for g in gs], input_output_aliases={a: a for a in range(n)},
        scratch_shapes=[pltpu.SemaphoreType.DMA((n,))] * 2)(*gs)


def _allreduce_small(v):
    def body(v_ref, o_ref, land, send, recv):
        x, y, c, _ = _place()
        me = 4 * x + 2 * y + c
        land[me] = v_ref[...]
        cps = []
        for rel in range(1, 8):
            bx, by, bc = (rel >> 2) & 1, (rel >> 1) & 1, rel & 1
            peer = (1 - x if bx else x, 1 - y if by else y, 1 - c if bc else c)
            cps.append(pltpu.make_async_remote_copy(src_ref=v_ref, dst_ref=land.at[me], send_sem=send.at[rel - 1],
                                                    recv_sem=recv.at[rel - 1], device_id=peer, device_id_type=MESH_IDS))
        for cp in cps:
            cp.start()
        for cp in cps:
            cp.wait()
        acc = land[0]
        for d in range(1, 8):
            acc = acc + land[d]
        o_ref[...] = acc

    vm = pl.BlockSpec(memory_space=pltpu.VMEM)
    return pl.pallas_call(
        body, name="allreduce_small", in_specs=[vm], out_specs=vm, out_shape=jax.ShapeDtypeStruct(v.shape, F32),
        scratch_shapes=[pltpu.VMEM((8,) + v.shape, F32), pltpu.SemaphoreType.DMA((7,)), pltpu.SemaphoreType.DMA((7,))])(v)


def _adamw(name, w, g, m, v):
    _, rows, cols = w.shape
    tm = rows if rows * cols <= 128 * 1024 else _row_tile(rows)
    c1 = 1.0 / (1.0 - ADAM_B1 ** ADAM_STEP)
    c2 = 1.0 / (1.0 - ADAM_B2 ** ADAM_STEP)

    def kern(w_ref, g_ref, m_ref, v_ref, d_ref, mo_ref, vo_ref):
        gv = g_ref[...]
        mn = ADAM_B1 * m_ref[...] + (1.0 - ADAM_B1) * gv
        vn = ADAM_B2 * v_ref[...] + (1.0 - ADAM_B2) * (gv * gv)
        d_ref[...] = -ADAM_LR * ((mn * c1) / (jnp.sqrt(vn * c2) + ADAM_EPS) + ADAM_WD * w_ref[...])
        mo_ref[...] = mn
        vo_ref[...] = vn

    spec = pl.BlockSpec((None, tm, cols), lambda i: (0, i, 0))
    return pl.pallas_call(
        kern, name=name, grid=(rows // tm,), in_specs=[spec] * 4, out_specs=[spec] * 3,
        out_shape=[jax.ShapeDtypeStruct(w.shape, F32)] * 3, compiler_params=_cparams(("parallel",)))(w, g, m, v)


SHARDED = (("w_in", 1), ("w_out", 0), ("w_up", 1), ("w_down", 0), ("w_ple_gate", 0), ("w_ple_proj", 1),
           ("ssm_conv_w", 1), ("ffn_conv_w", 1))
MATRICES = ("w_in", "w_out", "w_up", "w_down", "w_ple_gate", "w_ple_proj")
EARLY_REDUCED = MATRICES[1:]
REPLICATED = ("attn_norm_g", "q_norm_g", "k_norm_g", "ssm_conv_b", "dt_bias", "a_log", "d_skip", "ssm_norm_g",
              "ffn_norm_g", "ffn_conv_b", "ple_norm_g")
WEIGHT_ORDER = ("attn_norm_g", "w_in", "q_norm_g", "k_norm_g", "ssm_conv_w", "ssm_conv_b", "dt_bias", "a_log", "d_skip",
                "ssm_norm_g", "w_out", "ffn_norm_g", "w_up", "ffn_conv_w", "ffn_conv_b", "w_down", "ple_norm_g",
                "w_ple_gate", "w_ple_proj")


def _join_chips(g, axis):
    if axis == 0:
        return g.reshape(g.shape[0] * g.shape[1], g.shape[2])
    return jnp.transpose(g, (1, 0, 2)).reshape(g.shape[1], g.shape[0] * g.shape[2])


def _split_chips(g, axis):
    if axis == 0:
        return g.reshape(N_CHIPS, g.shape[0] // N_CHIPS, g.shape[1])
    r, c = g.shape
    return jnp.transpose(g.reshape(r, N_CHIPS, c // N_CHIPS), (1, 0, 2))


def _pack_small(vals, rows=SMALL_ROWS):
    flat = jnp.concatenate([v.reshape(-1) for v in vals])
    return jnp.pad(flat, (0, rows * LANE - flat.shape[0])).reshape(rows, LANE)


def _unpack_small(packed, like):
    flat, out, off = packed.reshape(-1), [], 0
    for v in like:
        out.append(flat[off:off + v.size].reshape(v.shape))
        off += v.size
    return out


def kernel(x, p, attn_norm_g, w_in, q_norm_g, k_norm_g, ssm_conv_w, ssm_conv_b, dt_bias, a_log, d_skip, ssm_norm_g, w_out, ffn_norm_g, w_up, ffn_conv_w, ffn_conv_b, w_down, ple_norm_g, w_ple_gate, w_ple_proj, loss_target, m_attn_norm_g, m_w_in, m_q_norm_g, m_k_norm_g, m_ssm_conv_w, m_ssm_conv_b, m_dt_bias, m_a_log, m_d_skip, m_ssm_norm_g, m_w_out, m_ffn_norm_g, m_w_up, m_ffn_conv_w, m_ffn_conv_b, m_w_down, m_ple_norm_g, m_w_ple_gate, m_w_ple_proj, v_attn_norm_g, v_w_in, v_q_norm_g, v_k_norm_g, v_ssm_conv_w, v_ssm_conv_b, v_dt_bias, v_a_log, v_d_skip, v_ssm_norm_g, v_w_out, v_ffn_norm_g, v_w_up, v_ffn_conv_w, v_ffn_conv_b, v_w_down, v_ple_norm_g, v_w_ple_gate, v_w_ple_proj):
    given = dict(locals())
    w2 = {n: given[n].reshape(given[n].shape[-2:]) if given[n].ndim == 3 else given[n] for n in WEIGHT_ORDER}

    cx, cy, cc = lax.axis_index("x"), lax.axis_index("y"), lax.axis_index("c")
    chip = 2 * cx + cy
    axis_of = dict(SHARDED)
    shard = lambda n: w2[n].astype(BF16) if n in MATRICES else w2[n]
    join = lambda n, g: _join_chips(lax.dynamic_update_index_in_dim(g, shard(n), chip, 0), axis_of[n])
    first = ("w_in", "ssm_conv_w", "ffn_conv_w")
    full = {n: join(n, g) for n, g in zip(first, _gather_over_chips([shard(n) for n in first]))}
    win = full["w_in"]
    w_in_p = jnp.concatenate([win[:, 2048:3584], win[:, 0:512], win[:, 1024:2048], win[:, 512:768], win[:, 768:1024],
                              win[:, 3584:3600], jnp.zeros((D_MODEL, PROJ_P - IN_PROJ), BF16)], axis=1)
    wts = {n: w2[n] for n in REPLICATED}
    wts.update(w_in_p=w_in_p, ssm_conv_w=full["ssm_conv_w"], ffn_conv_w=full["ffn_conv_w"])

    def join_late(gathered):
        late = {n: join(n, g) for n, g in zip(EARLY_REDUCED, gathered)}
        return dict(w_out_attn=late["w_out"][:ATTN_DIM], w_out_ssm=late["w_out"][ATTN_DIM:], w_up=late["w_up"],
                    w_down=late["w_down"], w_ple_gate=late["w_ple_gate"], w_ple_proj=late["w_ple_proj"])

    core = cc.astype(jnp.int32).reshape(1)
    idx = jnp.stack([chip, 2 * (1 - cx) + cy, 2 * cx + (1 - cy), 2 * (1 - cx) + (1 - cy), cc]).astype(jnp.int32)

    def chip_sums_of(tag, names, gd):
        major = [gd[n] if gd[n].ndim == 3 else _split_chips(gd[n], axis_of[n]) for n in names]
        return [_add_halves("grad_add_halves_" + n, g, got, core) for n, g, got in zip(names, major, _swap_halves(tag, major))]

    def w_in_sums(gd):
        gi = gd["w_in_p"]
        gd["w_in"] = jnp.concatenate([gi[:, OFF_Q:OFF_Q + ATTN_DIM], gi[:, OFF_K:OFF_K + KV_DIM], gi[:, OFF_V:OFF_V + KV_DIM],
                                      gi[:, OFF_Z:OFF_Z + SSM_INNER], gi[:, OFF_XBC:OFF_XBC + XBC_DIM], gi[:, OFF_DT:OFF_DT + SSM_HEADS]],
                                     axis=1)
        return chip_sums_of("grad_swap_halves_late", ("w_in",), gd)

    sq, grad_x, grads, early, late = _local_step(
        x[0], p[0, 0], loss_target[0], wts, [shard(n) for n in EARLY_REDUCED], join_late,
        functools.partial(chip_sums_of, "grad_swap_halves_early", EARLY_REDUCED), w_in_sums)
    sums = dict(zip(EARLY_REDUCED + ("w_in",), list(zip(*early)) + list(zip(*late))))
    halves = [_sum_chips("grad_sum_chips_" + n, *sums[n], idx) for n in MATRICES]
    g_shard = dict(zip(MATRICES, _share_with_sibling(halves)))

    small_names = REPLICATED + ("ssm_conv_w", "ffn_conv_w")
    small_like = [grads[n] for n in small_names] + [jnp.zeros((1,), F32)]
    small = _allreduce_small(_pack_small([grads[n] for n in small_names] + [jnp.sum(sq).reshape(1)], ALL_SMALL_ROWS))
    small_vals = dict(zip(small_names + ("loss",), _unpack_small(small, small_like)))
    loss = (0.5 / D_MODEL) * small_vals["loss"][0]
    for n in ("ssm_conv_w", "ffn_conv_w"):
        cols = w2[n].shape[1]
        g_shard[n] = lax.dynamic_slice_in_dim(small_vals[n], chip * cols, cols, axis=1)[None]

    delta, new_m, new_v = {}, {}, {}
    for n, _ in SHARDED:
        delta[n], new_m[n], new_v[n] = _adamw("adamw_" + n, given[n], g_shard[n], given["m_" + n], given["v_" + n])
    packed = lambda prefix: _pack_small([given[prefix + n] for n in REPLICATED])[None]
    sm = _adamw("adamw_small", packed(""), _pack_small([small_vals[n] for n in REPLICATED])[None], packed("m_"), packed("v_"))
    for n in REPLICATED:
        g_shard[n] = small_vals[n]
    for dst, packed_out in zip((delta, new_m, new_v), sm):
        for n, val in zip(REPLICATED, _unpack_small(packed_out[0], [w2[n] for n in REPLICATED])):
            dst[n] = val

    def shaped(d):
        return [d[n].reshape(given[n].shape) for n in WEIGHT_ORDER]
    return (loss, grad_x[None], *shaped(g_shard), *shaped(delta), *shaped(new_m), *shaped(new_v))
```

```python
import functools

import jax
import jax.numpy as jnp
from jax import lax
from jax.experimental import pallas as pl
from jax.experimental.pallas import tpu as pltpu

F32 = jnp.float32
BF16 = jnp.bfloat16

D_MODEL = 1024
HEAD_DIM = 64
ATTN_DIM = 512
KV_DIM = 256
N_KV = 4
SSM_INNER = 1024
SSM_HEADS = 16
SSM_STATE = 128
BC_DIM = 256
XBC_DIM = SSM_INNER + 2 * BC_DIM
MIX_DIM = ATTN_DIM + SSM_INNER
IN_PROJ = 3600
D_FF = 2816
PLE_DIM = 256
CHUNK = 128
DILATIONS = (1, 4, 16)
EPS = 1e-6
ADAM_LR, ADAM_B1, ADAM_B2, ADAM_EPS, ADAM_WD, ADAM_STEP = 0.001, 0.9, 0.999, 1e-08, 0.01, 10

PROJ_P = 3712
OFF_XBC, OFF_Q, OFF_Z, OFF_K, OFF_V, OFF_DT = 0, 1536, 2048, 3072, 3328, 3584
LANE = 128
VMEM_LIMIT = 48 * 1024 * 1024
NEG = -1e30


def _cparams(sem):
    return pltpu.CompilerParams(dimension_semantics=sem, vmem_limit_bytes=VMEM_LIMIT)


def _sigmoid(x):
    return 1.0 / (1.0 + jnp.exp(-x))


def _dot(a, b):
    return jnp.dot(a, b, preferred_element_type=F32)


def _dot_nt(a, b):
    return lax.dot_general(a, b, (((1,), (1,)), ((), ())), preferred_element_type=F32)


def _dot_tn(a, b):
    return lax.dot_general(a, b, (((0,), (0,)), ((), ())), preferred_element_type=F32)


def _dot_split(x, m):
    hi = x.astype(BF16)
    lo = (x - hi.astype(F32)).astype(BF16)
    return _dot(hi, m) + _dot(lo, m)


def _rows(name, body, ins, outs, accs=(), tm=512):
    t_rows = next(s[1].shape[0] for s in ins if s[0] in ("t", "tc"))
    tm = min(tm, t_rows)
    in_specs, args = [], []
    for s in ins:
        if s[0] == "t":
            in_specs.append(pl.BlockSpec((tm, s[1].shape[1]), lambda i: (i, 0)))
        elif s[0] == "tc":
            in_specs.append(pl.BlockSpec((tm, s[2]), functools.partial(lambda i, c: (i, c), c=s[3])))
        else:
            in_specs.append(pl.BlockSpec(s[1].shape, lambda i: (0, 0)))
        args.append(s[1])
    out_shape = [jax.ShapeDtypeStruct((t_rows, w), dt) for w, dt in outs]
    out_specs = [pl.BlockSpec((tm, w), lambda i: (i, 0)) for w, _ in outs]
    out_shape += [jax.ShapeDtypeStruct(a, F32) for a in accs]
    out_specs += [pl.BlockSpec(a, lambda i: (0, 0)) for a in accs]
    n_acc = len(accs)

    def kern(*refs):
        if n_acc:
            @pl.when(pl.program_id(0) == 0)
            def _():
                for r in refs[len(refs) - n_acc:]:
                    r[...] = jnp.zeros(r.shape, F32)
        body(*refs)

    return pl.pallas_call(
        kern, name=name, grid=(t_rows // tm,), in_specs=in_specs, out_specs=out_specs, out_shape=out_shape,
        compiler_params=_cparams(("arbitrary",) if n_acc else ("parallel",)))(*args)


NCHUNK = 512


def _col_chunks(n):
    return [(c, min(NCHUNK, n - c)) for c in range(0, n, NCHUNK)]


def _mm_nn(name, pairs, out_dtype, res=None, tm=512, tn=None, halves=False):
    m, n = pairs[0][0].shape[0], pairs[0][1].shape[1]
    tn = n if tn is None else tn
    tm = min(tm, m)
    np_ = len(pairs)
    if halves:
        per = n // 2 // tn
        out_spec = pl.BlockSpec((None, tm, tn), lambda j, i: (j // per, i, j % per))
        out_shape = jax.ShapeDtypeStruct((2, m, n // 2), out_dtype)
    else:
        out_spec = pl.BlockSpec((tm, tn), lambda j, i: (i, j))
        out_shape = jax.ShapeDtypeStruct((m, n), out_dtype)
    in_specs, args = [], []
    for a, w in pairs:
        in_specs += [pl.BlockSpec((tm, a.shape[1]), lambda j, i: (i, 0)), pl.BlockSpec((w.shape[0], tn), lambda j, i: (0, j))]
        args += [a, w]
    if res is not None:
        in_specs.append(pl.BlockSpec((tm, tn), lambda j, i: (i, j)))
        args.append(res)

    def kern(*refs):
        o_ref = refs[-1]
        for c0, cw in _col_chunks(tn):
            acc = None
            for q in range(np_):
                d = _dot(refs[2 * q][...], refs[2 * q + 1][:, c0:c0 + cw])
                acc = d if acc is None else acc + d
            if res is not None:
                acc = acc + refs[2 * np_][:, c0:c0 + cw]
            o_ref[:, c0:c0 + cw] = acc.astype(o_ref.dtype)

    return pl.pallas_call(
        kern, name=name, grid=(n // tn, m // tm), in_specs=in_specs, out_specs=out_spec, out_shape=out_shape,
        compiler_params=_cparams(("parallel", "parallel")))(*args)


def _mm_nt(name, pairs, out_dtype, tm=512, tn=None):
    m, n = pairs[0][0].shape[-2], pairs[0][1].shape[0]
    tn = n if tn is None else tn
    tm = min(tm, m)
    np_ = len(pairs)
    in_specs, args = [], []
    for a, w, kb, *lead in pairs:
        if lead:
            in_specs.append(pl.BlockSpec((None, tm, a.shape[2]), functools.partial(lambda j, i, ld: (ld, i, 0), ld=lead[0])))
        else:
            in_specs.append(pl.BlockSpec((tm, a.shape[1]), lambda j, i: (i, 0)))
        in_specs.append(pl.BlockSpec((tn, a.shape[-1]), functools.partial(lambda j, i, kb: (j, kb), kb=kb)))
        args += [a, w]

    def kern(*refs):
        o_ref = refs[-1]
        for c0, cw in _col_chunks(tn):
            acc = None
            for q in range(np_):
                d = _dot_nt(refs[2 * q][...], refs[2 * q + 1][c0:c0 + cw, :])
                acc = d if acc is None else acc + d
            o_ref[:, c0:c0 + cw] = acc.astype(o_ref.dtype)

    return pl.pallas_call(
        kern, name=name, grid=(n // tn, m // tm), in_specs=in_specs,
        out_specs=pl.BlockSpec((tm, tn), lambda j, i: (i, j)),
        out_shape=jax.ShapeDtypeStruct((m, n), out_dtype), compiler_params=_cparams(("parallel", "parallel")))(*args)


def _mm_tn(name, a, b, tm=None, tn=None, tk=1024, chip_cols=False):
    t, m = a.shape
    n = b.shape[-1] * (2 if b.ndim == 3 else 1)
    tm = m if tm is None else tm
    tn = n if tn is None else tn
    tk = min(tk, t)
    if b.ndim == 3:
        per = n // 2 // tn
        b_spec = pl.BlockSpec((None, tk, tn), lambda i, j, k: (j // per, k, j % per))
    else:
        b_spec = pl.BlockSpec((tk, tn), lambda i, j, k: (k, j))
    if chip_cols:
        out_spec = pl.BlockSpec((None, tm, tn), lambda i, j, k: (j, i, 0))
        out_shape = jax.ShapeDtypeStruct((n // tn, m, tn), F32)
    else:
        out_spec = pl.BlockSpec((tm, tn), lambda i, j, k: (i, j))
        out_shape = jax.ShapeDtypeStruct((m, n), F32)

    def kern(a_ref, b_ref, o_ref):
        @pl.when(pl.program_id(2) == 0)
        def _():
            o_ref[...] = jnp.zeros(o_ref.shape, F32)
        for c0, cw in _col_chunks(tn):
            o_ref[:, c0:c0 + cw] += _dot_tn(a_ref[...], b_ref[:, c0:c0 + cw])

    return pl.pallas_call(
        kern, name=name, grid=(m // tm, n // tn, t // tk),
        in_specs=[pl.BlockSpec((tk, tm), lambda i, j, k: (k, i)), b_spec], out_specs=out_spec, out_shape=out_shape,
        compiler_params=_cparams(("parallel", "parallel", "arbitrary")))(a, b)


def _rms_fwd(name, x, g):
    def body(x_ref, g_ref, h_ref):
        xv = x_ref[...]
        r = lax.rsqrt(jnp.mean(xv * xv, axis=-1, keepdims=True) + EPS)
        h_ref[...] = (xv * r * g_ref[...]).astype(BF16)
    return _rows(name, body, [("t", x), ("p", g)], [(x.shape[1], BF16)])[0]


def _rms_bwd(name, dh, x, g, dres):
    d = x.shape[1]

    def body(dh_ref, x_ref, g_ref, dres_ref, dx_ref, dxb_ref, dg_ref):
        xv, dhv = x_ref[...], dh_ref[...]
        r = lax.rsqrt(jnp.mean(xv * xv, axis=-1, keepdims=True) + EPS)
        gd = dhv * g_ref[...]
        dx = dres_ref[...] + r * gd - xv * (r * r * r * jnp.mean(xv * gd, axis=-1, keepdims=True))
        dx_ref[...] = dx
        dxb_ref[...] = dx.astype(BF16)
        dg_ref[...] += jnp.sum(dhv * xv * r, axis=0, keepdims=True)
    return _rows(name, body, [("t", dh), ("t", x), ("p", g), ("t", dres)], [(d, F32), (d, BF16)], accs=[(1, d)])


def _norm_mm(name, x, g, w, tm=512):
    m, k = x.shape
    n = w.shape[1]

    def kern(x_ref, g_ref, w_ref, h_ref, o_ref):
        xv = x_ref[...]
        h = (xv * lax.rsqrt(jnp.mean(xv * xv, axis=-1, keepdims=True) + EPS) * g_ref[...]).astype(BF16)
        h_ref[...] = h
        for c0, cw in _col_chunks(n):
            o_ref[:, c0:c0 + cw] = _dot(h, w_ref[:, c0:c0 + cw])

    return pl.pallas_call(
        kern, name=name, grid=(m // tm,),
        in_specs=[pl.BlockSpec((tm, k), lambda i: (i, 0)), pl.BlockSpec((1, k), lambda i: (0, 0)), pl.BlockSpec((k, n), lambda i: (0, 0))],
        out_specs=[pl.BlockSpec((tm, k), lambda i: (i, 0)), pl.BlockSpec((tm, n), lambda i: (i, 0))],
        out_shape=[jax.ShapeDtypeStruct((m, k), BF16), jax.ShapeDtypeStruct((m, n), F32)],
        compiler_params=_cparams(("parallel",)))(x, g, w)


def _mm_nt_rms_bwd(name, a, w, x, g, dres, parts=(), tm=512):
    m, k = a.shape
    n = w.shape[0]
    ns, steps = len(parts), m // tm

    def kern(a_ref, w_ref, x_ref, g_ref, dres_ref, *rest):
        dx_ref, dxb_ref, dg_ref = rest[ns:ns + 3]
        dh = rest[2 * ns + 3]
        if ns:
            start, finish = _scatter_steps(ns, rest[:ns], rest[ns + 3:2 * ns + 3], rest[2 * ns + 4:])
            pl.when(pl.program_id(0) == 0)(start)
            pl.when(pl.program_id(0) == steps - 1)(finish)

        @pl.when(pl.program_id(0) == 0)
        def _():
            dg_ref[...] = jnp.zeros(dg_ref.shape, F32)
        av = a_ref[...]
        for c0, cw in _col_chunks(n):
            dh[:, c0:c0 + cw] = _dot_nt(av, w_ref[c0:c0 + cw, :])
        xv, dhv = x_ref[...], dh[...]
        r = lax.rsqrt(jnp.mean(xv * xv, axis=-1, keepdims=True) + EPS)
        gd = dhv * g_ref[...]
        dx = dres_ref[...] + r * gd - xv * (r * r * r * jnp.mean(xv * gd, axis=-1, keepdims=True))
        dx_ref[...] = dx
        dxb_ref[...] = dx.astype(BF16)
        dg_ref[...] += jnp.sum(dhv * xv * r, axis=0, keepdims=True)

    row = lambda width: pl.BlockSpec((tm, width), lambda i: (i, 0))
    return pl.pallas_call(
        kern, name=name, grid=(steps,),
        in_specs=[row(k), pl.BlockSpec((n, k), lambda i: (0, 0)), row(n), pl.BlockSpec((1, n), lambda i: (0, 0)), row(n)]
        + [ANY_SPEC] * ns,
        out_specs=[row(n), row(n), pl.BlockSpec((1, n), lambda i: (0, 0))] + [ANY_SPEC] * ns,
        out_shape=[jax.ShapeDtypeStruct((m, n), F32), jax.ShapeDtypeStruct((m, n), BF16), jax.ShapeDtypeStruct((1, n), F32)]
        + [jax.ShapeDtypeStruct(s.shape, s.dtype) for s in parts],
        scratch_shapes=[pltpu.VMEM((tm, n), F32)] + (_scatter_sems(ns) if ns else []),
        compiler_params=_cparams(("arbitrary",)))(a, w, x, g, dres, *parts)


def _head_mean_matrix(width):
    i = jnp.arange(width) // HEAD_DIM
    return jnp.where(i[:, None] == i[None, :], 1.0 / HEAD_DIM, 0.0).astype(BF16)


ATT_SPAN = 2048
N_QH = 8


def _lane_lo(rows):
    return lax.broadcasted_iota(jnp.int32, (rows, LANE), 1) < HEAD_DIM


def _swap_halves_lanes(x):
    return pltpu.roll(x, HEAD_DIM, axis=1)


def _qknorm_fwd2(proj, gq_t, gk_t, tm=256):
    t = proj.shape[0]
    bq, bk = _head_mean_matrix(ATTN_DIM), _head_mean_matrix(KV_DIM)
    scale = HEAD_DIM ** -0.5

    def kern(q_ref, k_ref, v_ref, gq_ref, gk_ref, bq_ref, bk_ref, qo_ref, kvo_ref):
        q, k, v = q_ref[...], k_ref[...], v_ref[...]
        qn = (q * lax.rsqrt(_dot_split(q * q, bq_ref[...]) + EPS) * gq_ref[...]) * scale
        kn = k * lax.rsqrt(_dot_split(k * k, bk_ref[...]) + EPS) * gk_ref[...]
        lo = _lane_lo(tm)
        for j in range(N_KV):
            blk = qn[:, j * LANE:(j + 1) * LANE]
            qo_ref[2 * j] = jnp.where(lo, blk, 0.0)
            qo_ref[2 * j + 1] = jnp.where(lo, _swap_halves_lanes(blk), 0.0)
        for j in range(2):
            kb, vb = kn[:, j * LANE:(j + 1) * LANE], v[:, j * LANE:(j + 1) * LANE]
            kvo_ref[2 * j] = jnp.where(lo, kb, _swap_halves_lanes(vb))
            kvo_ref[2 * j + 1] = jnp.where(lo, _swap_halves_lanes(kb), vb)

    col = lambda w, idx: pl.BlockSpec((tm, w), functools.partial(lambda i, idx: (i, idx), idx=idx))
    par = lambda a: pl.BlockSpec(a.shape, lambda i: (0, 0))
    return pl.pallas_call(
        kern, name="qknorm_fwd", grid=(t // tm,),
        in_specs=[col(ATTN_DIM, OFF_Q // ATTN_DIM), col(KV_DIM, OFF_K // KV_DIM), col(KV_DIM, OFF_V // KV_DIM),
                  par(gq_t), par(gk_t), par(bq), par(bk)],
        out_specs=[pl.BlockSpec((N_QH, tm, LANE), lambda i: (0, i, 0)), pl.BlockSpec((N_KV, tm, LANE), lambda i: (0, i, 0))],
        out_shape=[jax.ShapeDtypeStruct((N_QH, t, LANE), F32), jax.ShapeDtypeStruct((N_KV, t, LANE), F32)],
        compiler_params=_cparams(("parallel",)))(proj, proj, proj, gq_t, gk_t, bq, bk)


def _qknorm_bwd2(proj, gq_t, gk_t, dqs, dkvs, tm=256):
    t = proj.shape[0]
    bq, bk = _head_mean_matrix(ATTN_DIM), _head_mean_matrix(KV_DIM)
    scale = HEAD_DIM ** -0.5

    def kern(q_ref, k_ref, gq_ref, gk_ref, bq_ref, bk_ref, a1, a2, a3, b1, b2, b3, dq_ref, dk_ref, dv_ref, dgq_ref, dgk_ref):
        @pl.when(pl.program_id(0) == 0)
        def _():
            dgq_ref[...] = jnp.zeros(dgq_ref.shape, F32)
            dgk_ref[...] = jnp.zeros(dgk_ref.shape, F32)
        lo = _lane_lo(tm)
        sq = [a1[h] + a2[h] + a3[h] for h in range(N_QH)]
        skv = [b1[h] + b2[h] + b3[h] for h in range(N_KV)]
        dqn = jnp.concatenate([jnp.where(lo, sq[2 * j], _swap_halves_lanes(sq[2 * j + 1])) for j in range(N_KV)], axis=1) * scale
        dkn = jnp.concatenate([jnp.where(lo, skv[2 * j], _swap_halves_lanes(skv[2 * j + 1])) for j in range(2)], axis=1)
        dv = jnp.concatenate([jnp.where(lo, _swap_halves_lanes(skv[2 * j]), skv[2 * j + 1]) for j in range(2)], axis=1)
        q, k = q_ref[...], k_ref[...]
        rq = lax.rsqrt(_dot_split(q * q, bq_ref[...]) + EPS)
        rk = lax.rsqrt(_dot_split(k * k, bk_ref[...]) + EPS)
        gdq, gdk = dqn * gq_ref[...], dkn * gk_ref[...]
        dq_ref[...] = (rq * gdq - q * (rq * rq * rq * _dot_split(q * gdq, bq_ref[...]))).astype(BF16)
        dk_ref[...] = (rk * gdk - k * (rk * rk * rk * _dot_split(k * gdk, bk_ref[...]))).astype(BF16)
        dv_ref[...] = dv.astype(BF16)
        dgq_ref[...] += jnp.sum(dqn * q * rq, axis=0, keepdims=True)
        dgk_ref[...] += jnp.sum(dkn * k * rk, axis=0, keepdims=True)

    col = lambda w, idx: pl.BlockSpec((tm, w), functools.partial(lambda i, idx: (i, idx), idx=idx))
    par = lambda a: pl.BlockSpec(a.shape, lambda i: (0, 0))
    blk = lambda n: pl.BlockSpec((n, tm, LANE), lambda i: (0, i, 0))
    row = lambda w: pl.BlockSpec((tm, w), lambda i: (i, 0))
    acc = lambda w: pl.BlockSpec((1, w), lambda i: (0, 0))
    return pl.pallas_call(
        kern, name="qknorm_bwd", grid=(t // tm,),
        in_specs=[col(ATTN_DIM, OFF_Q // ATTN_DIM), col(KV_DIM, OFF_K // KV_DIM), par(gq_t), par(gk_t), par(bq), par(bk)]
        + [blk(N_QH)] * 3 + [blk(N_KV)] * 3,
        out_specs=[row(ATTN_DIM), row(KV_DIM), row(KV_DIM), acc(ATTN_DIM), acc(KV_DIM)],
        out_shape=[jax.ShapeDtypeStruct((t, ATTN_DIM), BF16), jax.ShapeDtypeStruct((t, KV_DIM), BF16),
                   jax.ShapeDtypeStruct((t, KV_DIM), BF16), jax.ShapeDtypeStruct((1, ATTN_DIM), F32),
                   jax.ShapeDtypeStruct((1, KV_DIM), F32)],
        compiler_params=_cparams(("arbitrary",)))(proj, proj, gq_t, gk_t, bq, bk, *dqs, *dkvs)


def _att_rows(b, r, dil):
    if dil == 1:
        return pl.ds(b * CHUNK, CHUNK)
    return pl.ds(b * CHUNK * dil + r, CHUNK, stride=dil)


def _for_residues(dil, unit):
    for r in range(dil):
        unit(r, 0)


def _band_qk(first):
    ri = lax.broadcasted_iota(jnp.int32, (CHUNK, 2 * CHUNK), 0)
    cj = lax.broadcasted_iota(jnp.int32, (CHUNK, 2 * CHUNK), 1)
    band = (cj - ri >= 0) & (cj - ri <= CHUNK)
    return band if first is None else band & (jnp.logical_not(first) | (cj >= CHUNK))


def _band_kq(last):
    rj = lax.broadcasted_iota(jnp.int32, (CHUNK, 2 * CHUNK), 0)
    ci = lax.broadcasted_iota(jnp.int32, (CHUNK, 2 * CHUNK), 1)
    band = (ci - rj >= 0) & (ci - rj <= CHUNK)
    return band if last is None else band & (jnp.logical_not(last) | (ci < CHUNK))


def _att_specs(t, dil):
    sub = CHUNK * dil
    nb, last = ATT_SPAN // sub, t // sub - 1
    cur = lambda heads: pl.BlockSpec((heads, ATT_SPAN, LANE), lambda kh, n: (kh, n, 0))
    prev = lambda heads: pl.BlockSpec((heads, sub, LANE), lambda kh, n: (kh, jnp.maximum(n * nb - 1, 0), 0))
    nxt = lambda heads: pl.BlockSpec((heads, sub, LANE), lambda kh, n: (kh, jnp.minimum((n + 1) * nb, last), 0))
    return sub, nb, cur, prev, nxt


def _attn_fwd2(q, kv, dil):
    t = q.shape[1]
    sub, nb, cur, prev, _ = _att_specs(t, dil)

    def kern(q_ref, kvp_ref, kvc_ref, o_ref, lse_ref):
        n = pl.program_id(1)
        lane = lax.broadcasted_iota(jnp.int32, (CHUNK, LANE), 1)
        for b in range(nb):
            mask = _band_qk((n == 0) if b == 0 else None)

            def unit(r, carry, b=b, mask=mask):
                rows = _att_rows(b, r, dil)
                kvp = kvc_ref[_att_rows(b - 1, r, dil), :] if b > 0 else kvp_ref[_att_rows(0, r, dil), :]
                kvcat = jnp.concatenate([kvp, kvc_ref[rows, :]], axis=0).astype(BF16)
                lse_tile = jnp.zeros((CHUNK, LANE), F32)
                for g in range(2):
                    s = jnp.where(mask, _dot_nt(q_ref.at[g][rows, :].astype(BF16), kvcat), NEG)
                    m = jnp.max(s, axis=1, keepdims=True)
                    p = jnp.exp(s - m)
                    l = jnp.sum(p, axis=1, keepdims=True)
                    o_ref.at[g][rows, :] = _dot(p.astype(BF16), kvcat) * (1.0 / l)
                    lse_tile = jnp.where(lane == g, m + jnp.log(l), lse_tile)
                lse_ref[rows, :] = lse_tile
                return carry
            _for_residues(dil, unit)

    return pl.pallas_call(
        kern, name=f"attn_fwd_d{dil}", grid=(N_KV, t // ATT_SPAN), in_specs=[cur(2), prev(None), cur(None)],
        out_specs=[cur(2), cur(None)],
        out_shape=[jax.ShapeDtypeStruct((N_QH, t, LANE), F32), jax.ShapeDtypeStruct((N_KV, t, LANE), F32)],
        compiler_params=_cparams(("parallel", "parallel")))(q, kv, kv)


def _attn_merge2(os_, lses, tm=256):
    t = os_[0].shape[1]

    def kern(o1, o2, o3, l1, l2, l3, out_ref, lse_ref):
        pieces = []
        for kh in range(N_KV):
            a, b, c = l1[kh], l2[kh], l3[kh]
            m = jnp.maximum(jnp.maximum(a, b), c)
            tot = m + jnp.log(jnp.exp(a - m) + jnp.exp(b - m) + jnp.exp(c - m))
            lse_ref[kh] = tot
            wa, wb, wc = jnp.exp(a - tot), jnp.exp(b - tot), jnp.exp(c - tot)
            for g in range(2):
                h = 2 * kh + g
                acc = wa[:, g:g + 1] * o1[h] + wb[:, g:g + 1] * o2[h] + wc[:, g:g + 1] * o3[h]
                pieces.append(acc[:, HEAD_DIM:])
        out_ref[...] = jnp.concatenate(pieces, axis=1).astype(BF16)

    blk = lambda n: pl.BlockSpec((n, tm, LANE), lambda i: (0, i, 0))
    return pl.pallas_call(
        kern, name="attn_merge", grid=(t // tm,), in_specs=[blk(N_QH)] * 3 + [blk(N_KV)] * 3,
        out_specs=[pl.BlockSpec((tm, ATTN_DIM), lambda i: (i, 0)), blk(N_KV)],
        out_shape=[jax.ShapeDtypeStruct((t, ATTN_DIM), BF16), jax.ShapeDtypeStruct((N_KV, t, LANE), F32)],
        compiler_params=_cparams(("parallel",)))(*os_, *lses)


def _attn_bwd_prep2(dmix, attn_out, tm=256):
    t = attn_out.shape[0]

    def kern(do_ref, o_ref, dot_ref, d_ref):
        do = do_ref[...]
        prod = do * o_ref[...].astype(F32)
        lo = _lane_lo(tm)
        lane = lax.broadcasted_iota(jnp.int32, (tm, LANE), 1)
        for kh in range(N_KV):
            blk, pb = do[:, kh * LANE:(kh + 1) * LANE], prod[:, kh * LANE:(kh + 1) * LANE]
            dot_ref[2 * kh] = jnp.where(lo, 0.0, _swap_halves_lanes(blk))
            dot_ref[2 * kh + 1] = jnp.where(lo, 0.0, blk)
            s_lo = jnp.sum(jnp.where(lo, pb, 0.0), axis=1, keepdims=True)
            s_hi = jnp.sum(pb, axis=1, keepdims=True) - s_lo
            d_ref[kh] = jnp.where(lane == 0, s_lo, jnp.where(lane == 1, s_hi, 0.0))

    blk = lambda n: pl.BlockSpec((n, tm, LANE), lambda i: (0, i, 0))
    return pl.pallas_call(
        kern, name="attn_bwd_prep", grid=(t // tm,),
        in_specs=[pl.BlockSpec((tm, ATTN_DIM), lambda i: (i, SSM_INNER // ATTN_DIM)), pl.BlockSpec((tm, ATTN_DIM), lambda i: (i, 0))],
        out_specs=[blk(N_QH), blk(N_KV)],
        out_shape=[jax.ShapeDtypeStruct((N_QH, t, LANE), F32), jax.ShapeDtypeStruct((N_KV, t, LANE), F32)],
        compiler_params=_cparams(("parallel",)))(dmix, attn_out)


def _attn_dq2(q, kv, dot, lse, dsum, dil):
    t = q.shape[1]
    sub, nb, cur, prev, _ = _att_specs(t, dil)

    def kern(q_ref, kvp_ref, kvc_ref, do_ref, lse_ref, d_ref, dq_ref):
        n = pl.program_id(1)
        for b in range(nb):
            mask = _band_qk((n == 0) if b == 0 else None)

            def unit(r, carry, b=b, mask=mask):
                rows = _att_rows(b, r, dil)
                kvp = kvc_ref[_att_rows(b - 1, r, dil), :] if b > 0 else kvp_ref[_att_rows(0, r, dil), :]
                kvcat = jnp.concatenate([kvp, kvc_ref[rows, :]], axis=0).astype(BF16)
                lse_t, d_t = lse_ref[rows, :], d_ref[rows, :]
                for g in range(2):
                    s = jnp.where(mask, _dot_nt(q_ref.at[g][rows, :].astype(BF16), kvcat), NEG)
                    p = jnp.exp(s - lse_t[:, g:g + 1])
                    dp = _dot_nt(do_ref.at[g][rows, :].astype(BF16), kvcat)
                    ds = p * (dp - d_t[:, g:g + 1])
                    dq_ref.at[g][rows, :] = _dot(ds.astype(BF16), kvcat)
                return carry
            _for_residues(dil, unit)

    return pl.pallas_call(
        kern, name=f"attn_dq_d{dil}", grid=(N_KV, t // ATT_SPAN),
        in_specs=[cur(2), prev(None), cur(None), cur(2), cur(None), cur(None)], out_specs=cur(2),
        out_shape=jax.ShapeDtypeStruct((N_QH, t, LANE), F32),
        compiler_params=_cparams(("parallel", "parallel")))(q, kv, kv, dot, lse, dsum)


def _attn_dkv2(q, kv, dot, lse, dsum, dil):
    t = q.shape[1]
    sub, nb, cur, _, nxt = _att_specs(t, dil)
    nsteps = t // ATT_SPAN

    def kern(kv_ref, qc_ref, qn_ref, doc_ref, don_ref, lc_ref, ln_ref, dc_ref, dn_ref, dkv_ref):
        n = pl.program_id(1)
        for b in range(nb):
            inside = b < nb - 1
            mask = _band_kq(None if inside else (n == nsteps - 1))

            def unit(r, carry, b=b, inside=inside, mask=mask):
                rows = _att_rows(b, r, dil)
                nrows = _att_rows(b + 1, r, dil) if inside else _att_rows(0, r, dil)
                kvb = kv_ref[rows, :].astype(BF16)
                follow = lambda cref, nref: (cref if inside else nref)[nrows, :]
                lse_t = jnp.concatenate([lc_ref[rows, :].T, follow(lc_ref, ln_ref).T], axis=1)
                d_t = jnp.concatenate([dc_ref[rows, :].T, follow(dc_ref, dn_ref).T], axis=1)
                acc = jnp.zeros((CHUNK, LANE), F32)
                for g in range(2):
                    qdo = jnp.concatenate([qc_ref.at[g][rows, :], follow(qc_ref.at[g], qn_ref.at[g]),
                                           doc_ref.at[g][rows, :], follow(doc_ref.at[g], don_ref.at[g])], axis=0).astype(BF16)
                    both = _dot_nt(kvb, qdo)
                    pt = jnp.exp(jnp.where(mask, both[:, :2 * CHUNK], NEG) - lse_t[g:g + 1, :])
                    dst = pt * (both[:, 2 * CHUNK:] - d_t[g:g + 1, :])
                    acc = acc + _dot(jnp.concatenate([dst, pt], axis=1).astype(BF16), qdo)
                dkv_ref[rows, :] = acc
                return carry
            _for_residues(dil, unit)

    return pl.pallas_call(
        kern, name=f"attn_dkv_d{dil}", grid=(N_KV, nsteps),
        in_specs=[cur(None), cur(2), nxt(2), cur(2), nxt(2), cur(None), nxt(None), cur(None), nxt(None)], out_specs=cur(None),
        out_shape=jax.ShapeDtypeStruct((N_KV, t, LANE), F32),
        compiler_params=_cparams(("parallel", "parallel")))(kv, q, q, dot, dot, lse, lse, dsum, dsum)


HALO = 8
SSM_CONV_TM, SSM_CONV_W = 512, 512
FFN_CONV_TM, FFN_CONV_W = 256, 1408


def _halo_specs(tm, width, t_rows, col_off=0, lead=None):
    per, last = tm // HALO, t_rows // HALO - 1
    row_maps = (lambda i: i, lambda i: jnp.maximum(i * per - 1, 0), lambda i: jnp.minimum((i + 1) * per, last))
    specs = []
    for rows, rm in zip((tm, HALO, HALO), row_maps):
        if lead is None:
            specs.append(pl.BlockSpec((rows, width), functools.partial(lambda c, i, rm: (rm(i), c + col_off), rm=rm)))
        else:
            specs.append(pl.BlockSpec((None, rows, width), functools.partial(lambda c, i, rm: (lead, rm(i), c + col_off), rm=rm)))
    return specs


def _fill_ext(buf, tile_ref, before_ref, after_ref, i, nt):
    tm = tile_ref.shape[0]
    buf[0:HALO, :] = jnp.where(i > 0, before_ref[...].astype(F32), 0.0)
    buf[HALO:HALO + tm, :] = tile_ref[...].astype(F32)
    if after_ref is not None:
        buf[HALO + tm:, :] = jnp.where(i < nt - 1, after_ref[...].astype(F32), 0.0)


CONV_RB, CONV_CW = 16, 256


def _lane_chunks(width):
    return [slice(c0, min(c0 + CONV_CW, width)) for c0 in range(0, width, CONV_CW)]


def _shifted(buf, taps, r0, rows, cs):
    return [buf[pl.ds(HALO - (taps - 1) + k + r0, rows), cs] for k in range(taps)]


def _taps_fwd(xs, w, b):
    acc = b
    for k, xk in enumerate(xs):
        acc = acc + w[k:k + 1, :] * xk
    return acc


def _taps_bwd(bufd, w, taps, r0, rows, cs):
    acc = None
    for k in range(taps):
        term = w[k:k + 1, :] * bufd[pl.ds(r0 + (taps - 1) - k, rows), cs]
        acc = term if acc is None else acc + term
    return acc


def _fold8(z):
    return z[:HALO] + z[HALO:] if z.shape[0] == 2 * HALO else z


def _silu_grad(pre):
    sg = _sigmoid(pre)
    return sg * (1.0 + pre * (1.0 - sg))


def _ssm_conv_fwd(proj, w, b):
    t = proj.shape[0]
    tm, wd = min(SSM_CONV_TM, t), SSM_CONV_W
    nt, taps = t // tm, w.shape[0]

    def kern(x_ref, xb_ref, w_ref, b_ref, o_ref, buf):
        _fill_ext(buf, x_ref, xb_ref, None, pl.program_id(1), nt)
        for cs in _lane_chunks(wd):
            wv, bv = w_ref[:, cs], b_ref[:, cs]
            for r0 in range(0, tm, CONV_RB):
                pre = _taps_fwd(_shifted(buf, taps, r0, CONV_RB, cs), wv, bv)
                o_ref[r0:r0 + CONV_RB, cs] = pre * _sigmoid(pre)

    tile, before, _ = _halo_specs(tm, wd, t)
    par = lambda rows: pl.BlockSpec((rows, wd), lambda c, i: (0, c))
    return pl.pallas_call(
        kern, name="ssm_conv_fwd", grid=(XBC_DIM // wd, nt), in_specs=[tile, before, par(taps), par(1)],
        out_specs=pl.BlockSpec((tm, wd), lambda c, i: (i, c)), out_shape=jax.ShapeDtypeStruct((t, XBC_DIM), F32),
        scratch_shapes=[pltpu.VMEM((tm + HALO, wd), F32)],
        compiler_params=_cparams(("parallel", "parallel")))(proj, proj, w, b)


def _ssm_conv_bwd(proj, w, b, dact, parts):
    t = proj.shape[0]
    tm, wd = min(SSM_CONV_TM, t), SSM_CONV_W
    nt, taps, ncol, ns = t // tm, w.shape[0], XBC_DIM // SSM_CONV_W, len(parts)

    def kern(x_ref, xb_ref, xa_ref, d_ref, dn_ref, w_ref, b_ref, *rest):
        dx_ref, gw_ref, gb_ref = rest[ns:ns + 3]
        buf, bufd = rest[2 * ns + 3:2 * ns + 5]
        i = pl.program_id(1)
        if ns:
            start, finish = _scatter_steps(ns, rest[:ns], rest[ns + 3:2 * ns + 3], rest[2 * ns + 5:])
            pl.when((pl.program_id(0) == 0) & (i == 0))(start)
            pl.when((pl.program_id(0) == ncol - 1) & (i == nt - 1))(finish)
        _fill_ext(buf, x_ref, xb_ref, xa_ref, i, nt)

        @pl.when(i == 0)
        def _():
            gw_ref[...] = jnp.zeros(gw_ref.shape, F32)
            gb_ref[...] = jnp.zeros(gb_ref.shape, F32)
        for cs in _lane_chunks(wd):
            wv, bv = w_ref[:, cs], b_ref[:, cs]
            acc = [jnp.zeros((HALO, cs.stop - cs.start), F32) for _ in range(taps + 1)]
            for r0 in list(range(0, tm, CONV_RB)) + [tm]:
                inside = r0 < tm
                rows = CONV_RB if inside else HALO
                xs = _shifted(buf, taps, r0, rows, cs)
                d = d_ref[r0:r0 + rows, cs] if inside else jnp.where(i < nt - 1, dn_ref[:, cs], 0.0)
                dpre = d * _silu_grad(_taps_fwd(xs, wv, bv))
                bufd[r0:r0 + rows, cs] = dpre
                if inside:
                    acc[taps] = acc[taps] + _fold8(dpre)
                    for k in range(taps):
                        acc[k] = acc[k] + _fold8(dpre * xs[k])
            gb_ref[:, cs] += jnp.sum(acc[taps], axis=0, keepdims=True)
            for k in range(taps):
                gw_ref[k:k + 1, cs] += jnp.sum(acc[k], axis=0, keepdims=True)
            for r0 in range(0, tm, CONV_RB):
                dx_ref[r0:r0 + CONV_RB, cs] = _taps_bwd(bufd, wv, taps, r0, CONV_RB, cs).astype(BF16)

    xt, xb, xa = _halo_specs(tm, wd, t)
    dt_, _, dn = _halo_specs(tm, wd, t)
    par = lambda rows: pl.BlockSpec((rows, wd), lambda c, i: (0, c))
    return pl.pallas_call(
        kern, name="ssm_conv_bwd", grid=(ncol, nt), in_specs=[xt, xb, xa, dt_, dn, par(taps), par(1)] + [ANY_SPEC] * ns,
        out_specs=[pl.BlockSpec((tm, wd), lambda c, i: (i, c)), par(taps), par(1)] + [ANY_SPEC] * ns,
        out_shape=[jax.ShapeDtypeStruct((t, XBC_DIM), BF16), jax.ShapeDtypeStruct((taps, XBC_DIM), F32),
                   jax.ShapeDtypeStruct((1, XBC_DIM), F32)] + [jax.ShapeDtypeStruct(s.shape, s.dtype) for s in parts],
        scratch_shapes=[pltpu.VMEM((tm + 2 * HALO, wd), F32), pltpu.VMEM((tm + HALO, wd), F32)] + (_scatter_sems(ns) if ns else []),
        compiler_params=_cparams(("arbitrary", "arbitrary")))(proj, proj, proj, dact, dact, w, b, *parts)


def _ffn_act_down(u, w, b, w_down, x1):
    t = u.shape[1]
    tm, wd = min(FFN_CONV_TM, t), D_FF
    nt, taps = t // tm, w.shape[0]

    def kern(g_ref, gb_ref, v_ref, vb_ref, wg_ref, wv_ref, bg_ref, bv_ref, wd_ref, x1_ref, a_ref, x2_ref, bufg, bufv):
        i = pl.program_id(1)
        _fill_ext(bufg, g_ref, gb_ref, None, i, nt)
        _fill_ext(bufv, v_ref, vb_ref, None, i, nt)
        acc = x1_ref[...]
        for cs in _lane_chunks(wd):
            wg, wv, bg, bv = wg_ref[:, cs], wv_ref[:, cs], bg_ref[:, cs], bv_ref[:, cs]
            for r0 in range(0, tm, CONV_RB):
                g = _taps_fwd(_shifted(bufg, taps, r0, CONV_RB, cs), wg, bg)
                v = _taps_fwd(_shifted(bufv, taps, r0, CONV_RB, cs), wv, bv)
                a_ref[r0:r0 + CONV_RB, cs] = (g * _sigmoid(g) * v).astype(BF16)
            acc = acc + _dot(a_ref[:, cs], wd_ref[cs, :])
        x2_ref[...] = acc

    gt, gbf, _ = _halo_specs(tm, wd, t, lead=0)
    vt, vbf, _ = _halo_specs(tm, wd, t, lead=1)
    par = lambda rows, off: pl.BlockSpec((rows, wd), functools.partial(lambda c, i, off: (0, c + off), off=off))
    row = lambda width: pl.BlockSpec((tm, width), lambda c, i: (i, 0))
    return pl.pallas_call(
        kern, name="ffn_act_down", grid=(1, nt),
        in_specs=[gt, gbf, vt, vbf, par(taps, 0), par(taps, 1), par(1, 0), par(1, 1),
                  pl.BlockSpec(w_down.shape, lambda c, i: (0, 0)), row(D_MODEL)],
        out_specs=[row(wd), row(D_MODEL)],
        out_shape=[jax.ShapeDtypeStruct((t, D_FF), BF16), jax.ShapeDtypeStruct((t, D_MODEL), F32)],
        scratch_shapes=[pltpu.VMEM((tm + HALO, wd), F32)] * 2,
        compiler_params=_cparams(("parallel", "parallel")))(u, u, u, u, w, w, b, b, w_down, x1)


FFN_BWD_TM = 128


def _ffn_act_bwd(u, w, b, da, w_up):
    t = u.shape[1]
    tm, wd = min(FFN_BWD_TM, t), D_FF
    nt, taps = t // tm, w.shape[0]

    def kern(g_ref, gb_ref, ga_ref, v_ref, vb_ref, va_ref, d_ref, dn_ref, wg_ref, wv_ref, bg_ref, bv_ref, wup_ref,
             du_ref, gwg_ref, gwv_ref, gbg_ref, gbv_ref, dh_ref, bufg, bufv, bufdg, bufdv):
        i = pl.program_id(1)
        _fill_ext(bufg, g_ref, gb_ref, ga_ref, i, nt)
        _fill_ext(bufv, v_ref, vb_ref, va_ref, i, nt)

        @pl.when(i == 0)
        def _():
            for r in (gwg_ref, gwv_ref, gbg_ref, gbv_ref):
                r[...] = jnp.zeros(r.shape, F32)
        dh = None
        for cs in _lane_chunks(wd):
            wg, wv, bg, bv = wg_ref[:, cs], wv_ref[:, cs], bg_ref[:, cs], bv_ref[:, cs]
            zero = jnp.zeros((HALO, cs.stop - cs.start), F32)
            accg, accv = [zero] * (taps + 1), [zero] * (taps + 1)
            for r0 in list(range(0, tm, CONV_RB)) + [tm]:
                inside = r0 < tm
                rows = CONV_RB if inside else HALO
                xg, xv = _shifted(bufg, taps, r0, rows, cs), _shifted(bufv, taps, r0, rows, cs)
                g, v = _taps_fwd(xg, wg, bg), _taps_fwd(xv, wv, bv)
                dav = d_ref[r0:r0 + rows, cs] if inside else jnp.where(i < nt - 1, dn_ref[:, cs], 0.0)
                sg = _sigmoid(g)
                dg = dav * v * (sg * (1.0 + g * (1.0 - sg)))
                dv = dav * (g * sg)
                bufdg[r0:r0 + rows, cs] = dg
                bufdv[r0:r0 + rows, cs] = dv
                if inside:
                    accg[taps], accv[taps] = accg[taps] + _fold8(dg), accv[taps] + _fold8(dv)
                    for k in range(taps):
                        accg[k], accv[k] = accg[k] + _fold8(dg * xg[k]), accv[k] + _fold8(dv * xv[k])
            gbg_ref[:, cs] += jnp.sum(accg[taps], axis=0, keepdims=True)
            gbv_ref[:, cs] += jnp.sum(accv[taps], axis=0, keepdims=True)
            for k in range(taps):
                gwg_ref[k:k + 1, cs] += jnp.sum(accg[k], axis=0, keepdims=True)
                gwv_ref[k:k + 1, cs] += jnp.sum(accv[k], axis=0, keepdims=True)
            for r0 in range(0, tm, CONV_RB):
                du_ref[0, r0:r0 + CONV_RB, cs] = _taps_bwd(bufdg, wg, taps, r0, CONV_RB, cs).astype(BF16)
                du_ref[1, r0:r0 + CONV_RB, cs] = _taps_bwd(bufdv, wv, taps, r0, CONV_RB, cs).astype(BF16)
            part = (_dot_nt(du_ref[0, :, cs], wup_ref[:, cs])
                    + _dot_nt(du_ref[1, :, cs], wup_ref[:, slice(wd + cs.start, wd + cs.stop)]))
            dh = part if dh is None else dh + part
        dh_ref[...] = dh

    gt, gbf, gaf = _halo_specs(tm, wd, t, lead=0)
    vt, vbf, vaf = _halo_specs(tm, wd, t, lead=1)
    dt_, _, dn = _halo_specs(tm, wd, t)
    par = lambda rows, off: pl.BlockSpec((rows, wd), functools.partial(lambda c, i, off: (0, c + off), off=off))
    return pl.pallas_call(
        kern, name="ffn_act_bwd", grid=(1, nt),
        in_specs=[gt, gbf, gaf, vt, vbf, vaf, dt_, dn, par(taps, 0), par(taps, 1), par(1, 0), par(1, 1),
                  pl.BlockSpec(w_up.shape, lambda c, i: (0, 0))],
        out_specs=[pl.BlockSpec((2, tm, wd), lambda c, i: (0, i, 0)), par(taps, 0), par(taps, 0), par(1, 0), par(1, 0),
                   pl.BlockSpec((tm, D_MODEL), lambda c, i: (i, 0))],
        out_shape=[jax.ShapeDtypeStruct((2, t, D_FF), BF16)] + [jax.ShapeDtypeStruct((taps, D_FF), F32)] * 2
        + [jax.ShapeDtypeStruct((1, D_FF), F32)] * 2 + [jax.ShapeDtypeStruct((t, D_MODEL), F32)],
        scratch_shapes=[pltpu.VMEM((tm + 2 * HALO, wd), F32)] * 2 + [pltpu.VMEM((tm + HALO, wd), F32)] * 2,
        compiler_params=_cparams(("parallel", "arbitrary")))(u, u, u, u, u, u, da, da, w, w, b, b, w_up)


def _softplus(x):
    e = jnp.exp(-jnp.abs(x))
    return jnp.maximum(x, 0.0) + jnp.where(e < 1e-4, e - 0.5 * e * e, jnp.log(1.0 + e))


def _tri(lower):
    r = lax.broadcasted_iota(jnp.int32, (CHUNK, CHUNK), 0)
    c = lax.broadcasted_iota(jnp.int32, (CHUNK, CHUNK), 1)
    return (r >= c) if lower else (r <= c)


def _cum(mat_bool, x):
    return jnp.dot(mat_bool.astype(F32), x, precision=lax.Precision.HIGHEST, preferred_element_type=F32)


def _pair_sel(lane_lo, tile, h0):
    return jnp.where(lane_lo, tile[:, h0:h0 + 1], tile[:, h0 + 1:h0 + 2])


def _ssd_fwd(xbc_act, proj, dt_bias_p, a_log_p, dskip_t, shards):
    t = xbc_act.shape[0]
    nch = t // CHUNK
    ns = len(shards)

    def kern(xa_ref, dtr_ref, bias_ref, alog_ref, dsk_ref, *rest):
        y_ref, dt_ref, hs_ref = rest[ns:ns + 3]
        hst = rest[2 * ns + 3]
        if ns:
            start, forward, finish = _gather_steps(shards, rest[:ns], rest[ns + 3:2 * ns + 3], rest[2 * ns + 4:])
            pl.when(pl.program_id(0) == 0)(start)
            pl.when(pl.program_id(0) == (3 * nch) // 4)(forward)
            pl.when(pl.program_id(0) == nch - 1)(finish)

        @pl.when(pl.program_id(0) == 0)
        def _():
            hst[...] = jnp.zeros(hst.shape, F32)
        dt = _softplus(dtr_ref[...] + bias_ref[...])
        dt_ref[...] = dt
        acum = _cum(_tri(True), dt * (-jnp.exp(alog_ref[...])))
        acum_t = acum.T
        ea = jnp.exp(acum)
        a_last = acum[CHUNK - 1:CHUNK, :]
        dend = jnp.exp(a_last - acum)
        ea_last = jnp.exp(a_last)
        causal = _tri(True)
        lane_lo = lax.broadcasted_iota(jnp.int32, (CHUNK, LANE), 1) < HEAD_DIM
        row_lo = lax.broadcasted_iota(jnp.int32, (CHUNK, LANE), 0) < HEAD_DIM
        for g in range(2):
            bg = xa_ref[:, SSM_INNER + g * SSM_STATE:SSM_INNER + (g + 1) * SSM_STATE].astype(BF16)
            cg = xa_ref[:, SSM_INNER + BC_DIM + g * SSM_STATE:SSM_INNER + BC_DIM + (g + 1) * SSM_STATE].astype(BF16)
            cb = _dot_nt(cg, bg)
            for j in range(4 * g, 4 * g + 4):
                h0 = 2 * j
                cols = slice(j * LANE, (j + 1) * LANE)
                xp = xa_ref[:, cols]
                xdt = xp * _pair_sel(lane_lo, dt, h0)
                ydiag = None
                for hh, sel in ((h0, lane_lo), (h0 + 1, ~lane_lo)):
                    seg = acum[:, hh:hh + 1] - acum_t[hh:hh + 1, :]
                    mm = (cb * jnp.where(causal, jnp.exp(jnp.minimum(seg, 0.0)), 0.0)).astype(BF16)
                    d = _dot(mm, jnp.where(sel, xdt, 0.0).astype(BF16))
                    ydiag = d if ydiag is None else ydiag + d
                hp = hst[cols, :]
                hs_ref[cols, :] = hp
                yoff = _dot_nt(cg, hp.astype(BF16)) * _pair_sel(lane_lo, ea, h0)
                y_ref[:, cols] = ydiag + yoff + dsk_ref[:, cols] * xp
                xw = (xdt * _pair_sel(lane_lo, dend, h0)).astype(BF16)
                rowf = jnp.where(row_lo, ea_last[:, h0:h0 + 1], ea_last[:, h0 + 1:h0 + 2])
                hst[cols, :] = hp * rowf + _dot_tn(xw, bg)

    return pl.pallas_call(
        kern, name="ssd_fwd", grid=(nch,),
        in_specs=[pl.BlockSpec((CHUNK, XBC_DIM), lambda c: (c, 0)), pl.BlockSpec((CHUNK, LANE), lambda c: (c, OFF_DT // LANE)),
                  pl.BlockSpec((1, LANE), lambda c: (0, 0)), pl.BlockSpec((1, LANE), lambda c: (0, 0)),
                  pl.BlockSpec((1, SSM_INNER), lambda c: (0, 0))] + [ANY_SPEC] * ns,
        out_specs=[pl.BlockSpec((CHUNK, SSM_INNER), lambda c: (c, 0)), pl.BlockSpec((CHUNK, LANE), lambda c: (c, 0)),
                   pl.BlockSpec((None, SSM_INNER, SSM_STATE), lambda c: (c, 0, 0))] + [ANY_SPEC] * ns,
        out_shape=[jax.ShapeDtypeStruct((t, SSM_INNER), F32), jax.ShapeDtypeStruct((t, LANE), F32),
                   jax.ShapeDtypeStruct((nch, SSM_INNER, SSM_STATE), F32)] + _gather_out_shapes(shards),
        scratch_shapes=[pltpu.VMEM((SSM_INNER, SSM_STATE), F32)] + (_gather_sems(ns) if ns else []),
        compiler_params=_cparams(("arbitrary",)))(xbc_act, proj, dt_bias_p, a_log_p, dskip_t, *shards)


def _ssd_bwd(xbc_act, proj, dt_sp, hstates, dy, dt_bias_p, a_log_p, dskip_t):
    t = xbc_act.shape[0]
    nch = t // CHUNK

    pair = jnp.arange(SSM_HEADS // 2)[:, None, None]
    psel = (jnp.arange(LANE)[None, None, :] == 2 * pair + (jnp.arange(LANE) // HEAD_DIM)[None, :, None]).astype(BF16)

    def kern(xa_ref, dtr_ref, dt_ref, hs_ref, dy_ref, bias_ref, alog_ref, dsk_ref, psel_ref,
             dact_ref, ddtr_ref, da_ref, dbias_ref, ddsk_ref, dh):
        @pl.when(pl.program_id(0) == 0)
        def _():
            dh[...] = jnp.zeros(dh.shape, F32)
            for r in (da_ref, dbias_ref, ddsk_ref):
                r[...] = jnp.zeros(r.shape, F32)
        dt = dt_ref[...]
        a_neg = -jnp.exp(alog_ref[...])
        acum = _cum(_tri(True), dt * a_neg)
        acum_t = acum.T
        ea = jnp.exp(acum)
        a_last = acum[CHUNK - 1:CHUNK, :]
        dend = jnp.exp(a_last - acum)
        ea_last = jnp.exp(a_last)
        causal = _tri(True)
        lane = lax.broadcasted_iota(jnp.int32, (CHUNK, LANE), 1)
        rowi = lax.broadcasted_iota(jnp.int32, (CHUNK, LANE), 0)
        lane_lo, row_lo, last_row = lane < HEAD_DIM, rowi < HEAD_DIM, rowi == CHUNK - 1
        d_dt = jnp.zeros((CHUNK, LANE), F32)
        d_acum = jnp.zeros((CHUNK, LANE), F32)
        for g in range(2):
            bcols = slice(SSM_INNER + g * SSM_STATE, SSM_INNER + (g + 1) * SSM_STATE)
            ccols = slice(SSM_INNER + BC_DIM + g * SSM_STATE, SSM_INNER + BC_DIM + (g + 1) * SSM_STATE)
            bg, cg = xa_ref[:, bcols].astype(BF16), xa_ref[:, ccols].astype(BF16)
            cb = _dot_nt(cg, bg)
            dg_sum = jnp.zeros((CHUNK, CHUNK), F32)
            dcg = jnp.zeros((CHUNK, SSM_STATE), F32)
            dbg = jnp.zeros((CHUNK, SSM_STATE), F32)
            for j in range(4 * g, 4 * g + 4):
                h0 = 2 * j
                cols = slice(j * LANE, (j + 1) * LANE)
                xp, dyp = xa_ref[:, cols], dy_ref[:, cols]
                dtsel = _pair_sel(lane_lo, dt, h0)
                xdt = xp * dtsel
                xdt_b = xdt.astype(BF16)
                hp, dhp = hs_ref[cols, :], dh[cols, :]
                hp_b, dhp_b = hp.astype(BF16), dhp.astype(BF16)
                easel, dendsel = _pair_sel(lane_lo, ea, h0), _pair_sel(lane_lo, dend, h0)
                dx, ydiag = None, None
                for hh, sel in ((h0, lane_lo), (h0 + 1, ~lane_lo)):
                    dyh = jnp.where(sel, dyp, 0.0).astype(BF16)
                    seg = acum[:, hh:hh + 1] - acum_t[hh:hh + 1, :]
                    dec = jnp.where(causal, jnp.exp(jnp.minimum(seg, 0.0)), 0.0)
                    mm_b = (cb * dec).astype(BF16)
                    dg_sum = dg_sum + dec * _dot_nt(dyh, xdt_b)
                    d = _dot_tn(mm_b, dyh)
                    y = _dot(mm_b, jnp.where(sel, xdt, 0.0).astype(BF16))
                    dx = d if dx is None else dx + d
                    ydiag = y if ydiag is None else ydiag + y
                g2 = _dot_nt(bg, dhp_b)
                tprod = xdt * g2 * dendsel
                yoff = _dot_nt(cg, hp_b) * easel
                yc = dyp.astype(BF16).astype(F32) * ydiag + dyp * yoff - (xdt_b.astype(F32) * dx + tprod)
                dx = dx + g2 * dendsel
                psel = psel_ref[j]
                t_lo = jnp.sum(jnp.where(lane_lo, tprod, 0.0), keepdims=True).reshape(1, 1)
                t_hi = jnp.sum(tprod, keepdims=True).reshape(1, 1) - t_lo
                hh_prod = dhp * hp
                s_lo = jnp.sum(jnp.where(row_lo, hh_prod, 0.0), keepdims=True).reshape(1, 1)
                s_hi = jnp.sum(hh_prod, keepdims=True).reshape(1, 1) - s_lo
                end_lo = ea_last[:, h0:h0 + 1] * s_lo + t_lo
                end_hi = ea_last[:, h0 + 1:h0 + 2] * s_hi + t_hi
                ends = jnp.where(lane == h0, end_lo, jnp.where(lane == h0 + 1, end_hi, 0.0))
                d_acum = d_acum + _dot_split(yc, psel) + jnp.where(last_row, ends, 0.0)
                dye = (dyp * easel).astype(BF16)
                dcg = dcg + _dot(dye, hp_b)
                dbg = dbg + _dot((xdt * dendsel).astype(BF16), dhp_b)
                rowf = jnp.where(row_lo, ea_last[:, h0:h0 + 1], ea_last[:, h0 + 1:h0 + 2])
                dh[cols, :] = dhp * rowf + _dot_tn(dye, cg)
                dact_ref[:, cols] = dx * dtsel + dsk_ref[:, cols] * dyp
                d_dt = d_dt + _dot_split(dx * xp, psel)
                ddsk_ref[:, cols] += jnp.sum(dyp * xp, axis=0, keepdims=True)
            dg_b = dg_sum.astype(BF16)
            dact_ref[:, ccols] = dcg + _dot(dg_b, bg)
            dact_ref[:, bcols] = dbg + _dot_tn(dg_b, cg)
        d_adt = _cum(_tri(False), d_acum)
        d_dt = d_dt + d_adt * a_neg
        da_ref[...] += jnp.sum(d_adt * dt, axis=0, keepdims=True)
        d_raw = jnp.where(lane < SSM_HEADS, d_dt * _sigmoid(dtr_ref[...] + bias_ref[...]), 0.0)
        ddtr_ref[...] = d_raw.astype(BF16)
        dbias_ref[...] += jnp.sum(d_raw, axis=0, keepdims=True)

    rev = lambda c: (nch - 1 - c, 0)
    return pl.pallas_call(
        kern, name="ssd_bwd", grid=(nch,),
        in_specs=[pl.BlockSpec((CHUNK, XBC_DIM), rev), pl.BlockSpec((CHUNK, LANE), lambda c: (nch - 1 - c, OFF_DT // LANE)),
                  pl.BlockSpec((CHUNK, LANE), rev), pl.BlockSpec((None, SSM_INNER, SSM_STATE), lambda c: (nch - 1 - c, 0, 0)),
                  pl.BlockSpec((CHUNK, SSM_INNER), rev),
                  pl.BlockSpec((1, LANE), lambda c: (0, 0)), pl.BlockSpec((1, LANE), lambda c: (0, 0)),
                  pl.BlockSpec((1, SSM_INNER), lambda c: (0, 0)), pl.BlockSpec(psel.shape, lambda c: (0, 0, 0))],
        out_specs=[pl.BlockSpec((CHUNK, XBC_DIM), rev), pl.BlockSpec((CHUNK, LANE), rev),
                   pl.BlockSpec((1, LANE), lambda c: (0, 0)), pl.BlockSpec((1, LANE), lambda c: (0, 0)),
                   pl.BlockSpec((1, SSM_INNER), lambda c: (0, 0))],
        out_shape=[jax.ShapeDtypeStruct((t, XBC_DIM), F32), jax.ShapeDtypeStruct((t, LANE), BF16),
                   jax.ShapeDtypeStruct((1, LANE), F32), jax.ShapeDtypeStruct((1, LANE), F32),
                   jax.ShapeDtypeStruct((1, SSM_INNER), F32)],
        scratch_shapes=[pltpu.VMEM((SSM_INNER, SSM_STATE), F32)],
        compiler_params=_cparams(("arbitrary",)))(xbc_act, proj, dt_sp, hstates, dy, dt_bias_p, a_log_p, dskip_t, psel)


def _ssm_post_fwd(y, proj, g):
    def body(y_ref, z_ref, g_ref, o_ref):
        z = z_ref[...]
        yz = y_ref[...] * (z * _sigmoid(z))
        r = lax.rsqrt(jnp.mean(yz * yz, axis=-1, keepdims=True) + EPS)
        o_ref[...] = (yz * r * g_ref[...]).astype(BF16)
    return _rows("ssm_post_fwd", body, [("t", y), ("tc", proj, SSM_INNER, OFF_Z // SSM_INNER), ("p", g)],
                 [(SSM_INNER, BF16)])[0]


def _ssm_post_bwd(dmix, y, proj, g):
    def body(do_ref, y_ref, z_ref, g_ref, dy_ref, dz_ref, dg_ref):
        z, yv, dout = z_ref[...], y_ref[...], do_ref[...]
        sg = _sigmoid(z)
        gz = z * sg
        yz = yv * gz
        r = lax.rsqrt(jnp.mean(yz * yz, axis=-1, keepdims=True) + EPS)
        gd = dout * g_ref[...]
        dyz = r * gd - yz * (r * r * r * jnp.mean(yz * gd, axis=-1, keepdims=True))
        dy_ref[...] = dyz * gz
        dz_ref[...] = (dyz * yv * (sg * (1.0 + z * (1.0 - sg)))).astype(BF16)
        dg_ref[...] += jnp.sum(dout * yz * r, axis=0, keepdims=True)
    return _rows("ssm_post_bwd", body,
                 [("tc", dmix, SSM_INNER, 0), ("t", y), ("tc", proj, SSM_INNER, OFF_Z // SSM_INNER), ("p", g)],
                 [(SSM_INNER, F32), (SSM_INNER, BF16)], accs=[(1, SSM_INNER)])


def _ple_loss(gl, pp, x2, tgt):
    d = x2.shape[1]

    def body(gl_ref, pp_ref, x_ref, t_ref, dy_ref, dgl_ref, dpp_ref, sq_ref):
        s = _sigmoid(gl_ref[...])
        ppv = pp_ref[...]
        diff = x_ref[...] + s * ppv - t_ref[...]
        dy = diff * (1.0 / d)
        dy_ref[...] = dy
        dgl_ref[...] = (dy * ppv * s * (1.0 - s)).astype(BF16)
        dpp_ref[...] = (dy * s).astype(BF16)
        sq_ref[...] += jnp.sum(diff * diff, axis=0, keepdims=True)
    return _rows("ple_loss", body, [("t", gl), ("t", pp), ("t", x2), ("t", tgt)], [(d, F32), (d, BF16), (d, BF16)],
                 accs=[(1, d)])


def _pad_lanes(v, width=LANE):
    return jnp.pad(v, ((0, 0), (0, width - v.shape[1])))


def _local_step(x, p, tgt, wts, late_shards=(), join_late=None, reduce_early=None, reduce_late=None):
    g_attn, g_ssm, g_ffn, g_ple = wts["attn_norm_g"], wts["ssm_norm_g"], wts["ffn_norm_g"], wts["ple_norm_g"]
    w_in_p = wts["w_in_p"]
    gq_t = jnp.tile(wts["q_norm_g"], (1, ATTN_DIM // HEAD_DIM))
    gk_t = jnp.tile(wts["k_norm_g"], (1, KV_DIM // HEAD_DIM))
    dt_bias_p, a_log_p = _pad_lanes(wts["dt_bias"]), _pad_lanes(wts["a_log"])
    dskip_t = jnp.repeat(wts["d_skip"], HEAD_DIM, axis=1)

    h1, proj = _norm_mm("in_proj", x, g_attn, w_in_p)
    q_hm, kv_hm = _qknorm_fwd2(proj, gq_t, gk_t)
    pats = [_attn_fwd2(q_hm, kv_hm, d) for d in DILATIONS]
    attn_out, lse = _attn_merge2([o for o, _ in pats], [l for _, l in pats])
    xbc_act = _ssm_conv_fwd(proj, wts["ssm_conv_w"], wts["ssm_conv_b"])
    y_ssd, dt_sp, hstates, *gathered = _ssd_fwd(xbc_act, proj, dt_bias_p, a_log_p, dskip_t, list(late_shards))
    if join_late is not None:
        wts = {**wts, **join_late(gathered)}
    w_out_s, w_out_a = wts["w_out_ssm"], wts["w_out_attn"]
    w_up, w_down, w_gate, w_proj = wts["w_up"], wts["w_down"], wts["w_ple_gate"], wts["w_ple_proj"]
    ssm_out = _ssm_post_fwd(y_ssd, proj, g_ssm)
    x1 = _mm_nn("out_proj", [(ssm_out, w_out_s), (attn_out, w_out_a)], F32, res=x, tm=1024)
    h2 = _rms_fwd("rms_ffn", x1, g_ffn)
    u = _mm_nn("ffn_up", [(h2, w_up)], F32, tm=1024, tn=1408, halves=True)
    a, x2 = _ffn_act_down(u, wts["ffn_conv_w"], wts["ffn_conv_b"], w_down, x1)
    h3, gl = _norm_mm("ple_gate", x2, g_ple, w_gate, tm=1024)
    pb = p.astype(BF16)
    pp = _mm_nn("ple_proj", [(pb, w_proj)], F32, tm=2048)
    dy, dgl, dpp, sq = _ple_loss(gl, pp, x2, tgt)

    grads = {}
    grads["w_ple_proj"] = _mm_tn("g_ple_proj", pb, dpp, tn=PLE_DIM, chip_cols=True)
    grads["w_ple_gate"] = _mm_tn("g_ple_gate", h3, dgl)
    dx2, dx2b, grads["ple_norm_g"] = _mm_nt_rms_bwd("d_h3", dgl, w_gate, x2, g_ple, dy)
    da = _mm_nt("d_ffn_act", [(dx2b, w_down, 0)], F32, tm=1024, tn=1408)
    grads["w_down"] = _mm_tn("g_ffn_down", a, dx2b, tm=1408)
    du, gwg, gwv, gbg, gbv, dh2 = _ffn_act_bwd(u, wts["ffn_conv_w"], wts["ffn_conv_b"], da, w_up)
    grads["ffn_conv_w"] = jnp.concatenate([gwg, gwv], axis=1)
    grads["ffn_conv_b"] = jnp.concatenate([gbg, gbv], axis=1)
    grads["w_up"] = _mm_tn("g_ffn_up", h2, du, tn=1408, chip_cols=True)
    dx1, dx1b, grads["ffn_norm_g"] = _rms_bwd("rms_ffn_bwd", dh2, x1, g_ffn, dx2)
    dmix = _mm_nt("d_mix", [(dx1b, jnp.concatenate([w_out_s, w_out_a], axis=0), 0)], F32, tm=1024)
    grads["w_out"] = jnp.concatenate([_mm_tn("g_out_attn", attn_out, dx1b), _mm_tn("g_out_ssm", ssm_out, dx1b)], axis=0)
    dy_ssd, dz, grads["ssm_norm_g"] = _ssm_post_bwd(dmix, y_ssd, proj, g_ssm)
    dact, ddtr, d_a, d_bias, d_dsk = _ssd_bwd(xbc_act, proj, dt_sp, hstates, dy_ssd, dt_bias_p, a_log_p, dskip_t)
    grads["dt_bias"] = d_bias[:, :SSM_HEADS]
    grads["a_log"] = d_a[:, :SSM_HEADS] * (-jnp.exp(wts["a_log"]))
    grads["d_skip"] = jnp.sum(d_dsk.reshape(SSM_HEADS, HEAD_DIM), axis=1)[None, :]
    chip_sums = reduce_early(grads) if reduce_early is not None else []
    dxbc, grads["ssm_conv_w"], grads["ssm_conv_b"], *scattered = _ssm_conv_bwd(proj, wts["ssm_conv_w"], wts["ssm_conv_b"], dact,
                                                                                chip_sums)
    do_hm, dsum = _attn_bwd_prep2(dmix, attn_out)
    dqs = [_attn_dq2(q_hm, kv_hm, do_hm, lse, dsum, d) for d in DILATIONS]
    dkvs = [_attn_dkv2(q_hm, kv_hm, do_hm, lse, dsum, d) for d in DILATIONS]
    dq, dk, dv, dgq, dgk = _qknorm_bwd2(proj, gq_t, gk_t, dqs, dkvs)
    grads["q_norm_g"] = jnp.sum(dgq.reshape(ATTN_DIM // HEAD_DIM, HEAD_DIM), axis=0)[None, :]
    grads["k_norm_g"] = jnp.sum(dgk.reshape(KV_DIM // HEAD_DIM, HEAD_DIM), axis=0)[None, :]
    dproj = jnp.concatenate([dxbc, dq, dz, dk, dv, ddtr], axis=1)
    grads["w_in_p"] = _mm_tn("g_in_proj", h1, dproj, tm=512)
    late_sums = reduce_late(grads) if reduce_late is not None else []
    grad_x, _, grads["attn_norm_g"], *late_scattered = _mm_nt_rms_bwd("d_h1", dproj, w_in_p, x, g_attn, dx1, late_sums)
    return sq, grad_x, grads, (chip_sums, scattered), (late_sums, late_scattered)


MESH_IDS = pl.DeviceIdType.MESH
N_CHIPS = 4
ANY_SPEC = pl.BlockSpec(memory_space=pl.ANY)
SMALL_ROWS = 96
ALL_SMALL_ROWS = 272


def _place():
    x, y, c = lax.axis_index("x"), lax.axis_index("y"), lax.axis_index("c")
    return x, y, c, [(1 - x, y), (x, 1 - y), (1 - x, 1 - y)]


def _gather_over_chips(arrs):
    n = len(arrs)

    def body(*refs):
        steps = _gather_steps(arrs, refs[:n], refs[n:2 * n], refs[2 * n:2 * n + 4])
        for step in steps:
            step()

    return pl.pallas_call(
        body, name="gather_weights", in_specs=[ANY_SPEC] * n, out_specs=[ANY_SPEC] * n,
        out_shape=_gather_out_shapes(arrs), scratch_shapes=_gather_sems(n))(*arrs)


def _gather_out_shapes(arrs):
    return [jax.ShapeDtypeStruct((N_CHIPS,) + a.shape, a.dtype) for a in arrs]


def _gather_sems(n):
    return [pltpu.SemaphoreType.DMA((3 * n,))] * 4


def _gather_steps(arrs, ins, outs, sems):
    n = len(arrs)
    split = [a.shape[0] % 64 == 0 for a in arrs]
    ici_send, ici_recv, d2d_send, d2d_recv = sems

    def place():
        x, y, c, chips = _place()
        return x, y, c, chips, 2 * x + y

    def part(ref, a, core):
        if not split[a]:
            return ref
        half = arrs[a].shape[0] // 2
        return ref.at[pl.ds(core * half, half)]

    def ici(a, k, slot, where):
        x, y, c, chips, _ = where
        px, py = chips[k]
        return pltpu.make_async_remote_copy(
            src_ref=part(ins[a], a, c), dst_ref=part(outs[a].at[slot], a, c), send_sem=ici_send.at[3 * a + k],
            recv_sem=ici_recv.at[3 * a + k], device_id=(px, py, c), device_id_type=MESH_IDS)

    def d2d(a, k, core, where):
        x, y, c, chips, _ = where
        px, py = chips[k]
        piece = part(outs[a].at[2 * px + py], a, core)
        return pltpu.make_async_remote_copy(src_ref=piece, dst_ref=piece, send_sem=d2d_send.at[3 * a + k],
                                            recv_sem=d2d_recv.at[3 * a + k], device_id=(x, y, 1 - c), device_id_type=MESH_IDS)

    def start():
        w = place()
        for a in range(n):
            for k in range(3):
                ici(a, k, w[4], w).start()

    def forward():
        w = place()
        for a in range(n):
            for k, (px, py) in enumerate(w[3]):
                ici(a, k, 2 * px + py, w).wait_recv()
                if split[a]:
                    d2d(a, k, w[2], w).start()

    def finish():
        w = place()
        for a in range(n):
            for k in range(3):
                if split[a]:
                    d2d(a, k, 1 - w[2], w).wait_recv()
                    d2d(a, k, w[2], w).wait_send()
                ici(a, k, w[4], w).wait_send()

    return start, forward, finish


def _row_tile(rows, cap=256):
    return max(d for d in range(8, cap + 1, 8) if rows % d == 0)


def _swap_halves(name, gs):
    n = len(gs)

    def body(*refs):
        ins, outs, send, recv = refs[:n], refs[n:2 * n], refs[2 * n], refs[2 * n + 1]
        x, y, c, _ = _place()
        cps = []
        for a in range(n):
            half = gs[a].shape[1] // 2
            for q in range(N_CHIPS):
                cps.append(pltpu.make_async_remote_copy(
                    src_ref=ins[a].at[q, pl.ds((1 - c) * half, half)], dst_ref=outs[a].at[q], send_sem=send.at[N_CHIPS * a + q],
                    recv_sem=recv.at[N_CHIPS * a + q], device_id=(x, y, 1 - c), device_id_type=MESH_IDS))
        for cp in cps:
            cp.start()
        for cp in cps:
            cp.wait()

    return pl.pallas_call(
        body, name=name, in_specs=[ANY_SPEC] * n, out_specs=[ANY_SPEC] * n,
        out_shape=[jax.ShapeDtypeStruct((N_CHIPS, g.shape[1] // 2, g.shape[2]), g.dtype) for g in gs],
        scratch_shapes=[pltpu.SemaphoreType.DMA((N_CHIPS * n,))] * 2)(*gs)


def _add_halves(name, g, got, c_idx):
    rows, cols = got.shape[1:]
    tm = _row_tile(rows)
    per = rows // tm

    def kern(c_ref, g_ref, r_ref, o_ref):
        o_ref[...] = (g_ref[...] + r_ref[...]).astype(BF16)

    return pl.pallas_call(
        kern, name=name,
        grid_spec=pltpu.PrefetchScalarGridSpec(
            num_scalar_prefetch=1, grid=(N_CHIPS, per),
            in_specs=[pl.BlockSpec((None, tm, cols), lambda q, i, c_ref: (q, c_ref[0] * per + i, 0)),
                      pl.BlockSpec((None, tm, cols), lambda q, i, c_ref: (q, i, 0))],
            out_specs=pl.BlockSpec((None, tm, cols), lambda q, i, c_ref: (q, i, 0))),
        out_shape=jax.ShapeDtypeStruct((N_CHIPS, rows, cols), BF16),
        compiler_params=_cparams(("parallel", "parallel")))(c_idx, g, got)


def _scatter_over_chips(ss):
    n = len(ss)

    def body(*refs):
        for step in _scatter_steps(n, refs[:n], refs[n:2 * n], refs[2 * n:2 * n + 2]):
            step()

    return pl.pallas_call(
        body, name="grad_scatter_chips", in_specs=[ANY_SPEC] * n, out_specs=[ANY_SPEC] * n,
        out_shape=[jax.ShapeDtypeStruct(s.shape, s.dtype) for s in ss], scratch_shapes=_scatter_sems(n))(*ss)


def _scatter_sems(n):
    return [pltpu.SemaphoreType.DMA((3 * n,))] * 2


def _scatter_steps(n, ins, outs, sems):
    send, recv = sems

    def copy(a, k, slot, where):
        x, y, c, chips = where
        px, py = chips[k]
        return pltpu.make_async_remote_copy(src_ref=ins[a].at[2 * px + py], dst_ref=outs[a].at[slot], send_sem=send.at[3 * a + k],
                                            recv_sem=recv.at[3 * a + k], device_id=(px, py, c), device_id_type=MESH_IDS)

    def start():
        w = _place()
        for a in range(n):
            for k in range(3):
                copy(a, k, 2 * w[0] + w[1], w).start()

    def finish():
        w = _place()
        for a in range(n):
            for k, (px, py) in enumerate(w[3]):
                copy(a, k, 2 * px + py, w).wait()

    return start, finish


def _sum_chips(name, own, parts, idx):
    rows, cols = parts.shape[1:]
    tm = _row_tile(rows)
    per = rows // tm

    def kern(o_idx, a_ref, b_ref, c_ref, d_ref, o_ref):
        o_ref[...] = ((a_ref[...].astype(F32) + b_ref[...].astype(F32)) + c_ref[...].astype(F32)) + d_ref[...].astype(F32)

    def spec(k):
        return pl.BlockSpec((None, tm, cols), functools.partial(lambda i, o_idx, k: (o_idx[k], i, 0), k=k))

    return pl.pallas_call(
        kern, name=name,
        grid_spec=pltpu.PrefetchScalarGridSpec(
            num_scalar_prefetch=1, grid=(per,), in_specs=[spec(0), spec(1), spec(2), spec(3)],
            out_specs=pl.BlockSpec((None, tm, cols), lambda i, o_idx: (0, o_idx[4] * per + i, 0))),
        out_shape=jax.ShapeDtypeStruct((1, 2 * rows, cols), F32), compiler_params=_cparams(("parallel",)))(idx, own, parts, parts, parts)


def _share_with_sibling(gs):
    n = len(gs)

    def body(*refs):
        ins, send, recv = refs[:n], refs[2 * n], refs[2 * n + 1]
        x, y, c, _ = _place()
        cps = []
        for a in range(n):
            half = gs[a].shape[1] // 2
            mine = pl.ds(c * half, half)
            cps.append(pltpu.make_async_remote_copy(src_ref=ins[a].at[0, mine], dst_ref=refs[n + a].at[0, mine], send_sem=send.at[a],
                                                    recv_sem=recv.at[a], device_id=(x, y, 1 - c), device_id_type=MESH_IDS))
        for cp in cps:
            cp.start()
        for cp in cps:
            cp.wait()

    return pl.pallas_call(
        body, name="grad_share_sibling", in_specs=[ANY_SPEC] * n, out_specs=[ANY_SPEC] * n,
        out_shape=[jax.ShapeDtypeStruct(g.shape, g.dtype) for g in gs], input_output_aliases={a: a for a in range(n)},
        scratch_shapes=[pltpu.SemaphoreType.DMA((n,))] * 2)(*gs)


def _allreduce_small(v):
    def body(v_ref, o_ref, land, send, recv):
        x, y, c, _ = _place()
        me = 4 * x + 2 * y + c
        land[me] = v_ref[...]
        cps = []
        for rel in range(1, 8):
            bx, by, bc = (rel >> 2) & 1, (rel >> 1) & 1, rel & 1
            peer = (1 - x if bx else x, 1 - y if by else y, 1 - c if bc else c)
            cps.append(pltpu.make_async_remote_copy(src_ref=v_ref, dst_ref=land.at[me], send_sem=send.at[rel - 1],
                                                    recv_sem=recv.at[rel - 1], device_id=peer, device_id_type=MESH_IDS))
        for cp in cps:
            cp.start()
        for cp in cps:
            cp.wait()
        acc = land[0]
        for d in range(1, 8):
            acc = acc + land[d]
        o_ref[...] = acc

    vm = pl.BlockSpec(memory_space=pltpu.VMEM)
    return pl.pallas_call(
        body, name="allreduce_small", in_specs=[vm], out_specs=vm, out_shape=jax.ShapeDtypeStruct(v.shape, F32),
        scratch_shapes=[pltpu.VMEM((8,) + v.shape, F32), pltpu.SemaphoreType.DMA((7,)), pltpu.SemaphoreType.DMA((7,))])(v)


def _adamw(name, w, g, m, v):
    _, rows, cols = w.shape
    tm = rows if rows * cols <= 128 * 1024 else _row_tile(rows)
    c1 = 1.0 / (1.0 - ADAM_B1 ** ADAM_STEP)
    c2 = 1.0 / (1.0 - ADAM_B2 ** ADAM_STEP)

    def kern(w_ref, g_ref, m_ref, v_ref, d_ref, mo_ref, vo_ref):
        gv = g_ref[...]
        mn = ADAM_B1 * m_ref[...] + (1.0 - ADAM_B1) * gv
        vn = ADAM_B2 * v_ref[...] + (1.0 - ADAM_B2) * (gv * gv)
        d_ref[...] = -ADAM_LR * ((mn * c1) / (jnp.sqrt(vn * c2) + ADAM_EPS) + ADAM_WD * w_ref[...])
        mo_ref[...] = mn
        vo_ref[...] = vn

    spec = pl.BlockSpec((None, tm, cols), lambda i: (0, i, 0))
    return pl.pallas_call(
        kern, name=name, grid=(rows // tm,), in_specs=[spec] * 4, out_specs=[spec] * 3,
        out_shape=[jax.ShapeDtypeStruct(w.shape, F32)] * 3, compiler_params=_cparams(("parallel",)))(w, g, m, v)


SHARDED = (("w_in", 1), ("w_out", 0), ("w_up", 1), ("w_down", 0), ("w_ple_gate", 0), ("w_ple_proj", 1),
           ("ssm_conv_w", 1), ("ffn_conv_w", 1))
MATRICES = ("w_in", "w_out", "w_up", "w_down", "w_ple_gate", "w_ple_proj")
EARLY_REDUCED = MATRICES[1:]
REPLICATED = ("attn_norm_g", "q_norm_g", "k_norm_g", "ssm_conv_b", "dt_bias", "a_log", "d_skip", "ssm_norm_g",
              "ffn_norm_g", "ffn_conv_b", "ple_norm_g")
WEIGHT_ORDER = ("attn_norm_g", "w_in", "q_norm_g", "k_norm_g", "ssm_conv_w", "ssm_conv_b", "dt_bias", "a_log", "d_skip",
                "ssm_norm_g", "w_out", "ffn_norm_g", "w_up", "ffn_conv_w", "ffn_conv_b", "w_down", "ple_norm_g",
                "w_ple_gate", "w_ple_proj")


def _join_chips(g, axis):
    if axis == 0:
        return g.reshape(g.shape[0] * g.shape[1], g.shape[2])
    return jnp.transpose(g, (1, 0, 2)).reshape(g.shape[1], g.shape[0] * g.shape[2])


def _split_chips(g, axis):
    if axis == 0:
        return g.reshape(N_CHIPS, g.shape[0] // N_CHIPS, g.shape[1])
    r, c = g.shape
    return jnp.transpose(g.reshape(r, N_CHIPS, c // N_CHIPS), (1, 0, 2))


def _pack_small(vals, rows=SMALL_ROWS):
    flat = jnp.concatenate([v.reshape(-1) for v in vals])
    return jnp.pad(flat, (0, rows * LANE - flat.shape[0])).reshape(rows, LANE)


def _unpack_small(packed, like):
    flat, out, off = packed.reshape(-1), [], 0
    for v in like:
        out.append(flat[off:off + v.size].reshape(v.shape))
        off += v.size
    return out


def kernel(x, p, attn_norm_g, w_in, q_norm_g, k_norm_g, ssm_conv_w, ssm_conv_b, dt_bias, a_log, d_skip, ssm_norm_g, w_out, ffn_norm_g, w_up, ffn_conv_w, ffn_conv_b, w_down, ple_norm_g, w_ple_gate, w_ple_proj, loss_target, m_attn_norm_g, m_w_in, m_q_norm_g, m_k_norm_g, m_ssm_conv_w, m_ssm_conv_b, m_dt_bias, m_a_log, m_d_skip, m_ssm_norm_g, m_w_out, m_ffn_norm_g, m_w_up, m_ffn_conv_w, m_ffn_conv_b, m_w_down, m_ple_norm_g, m_w_ple_gate, m_w_ple_proj, v_attn_norm_g, v_w_in, v_q_norm_g, v_k_norm_g, v_ssm_conv_w, v_ssm_conv_b, v_dt_bias, v_a_log, v_d_skip, v_ssm_norm_g, v_w_out, v_ffn_norm_g, v_w_up, v_ffn_conv_w, v_ffn_conv_b, v_w_down, v_ple_norm_g, v_w_ple_gate, v_w_ple_proj):
    given = dict(locals())
    w2 = {n: given[n].reshape(given[n].shape[-2:]) if given[n].ndim == 3 else given[n] for n in WEIGHT_ORDER}

    cx, cy, cc = lax.axis_index("x"), lax.axis_index("y"), lax.axis_index("c")
    chip = 2 * cx + cy
    axis_of = dict(SHARDED)
    shard = lambda n: w2[n].astype(BF16) if n in MATRICES else w2[n]
    join = lambda n, g: _join_chips(lax.dynamic_update_index_in_dim(g, shard(n), chip, 0), axis_of[n])
    first = ("w_in", "ssm_conv_w", "ffn_conv_w")
    full = {n: join(n, g) for n, g in zip(first, _gather_over_chips([shard(n) for n in first]))}
    win = full["w_in"]
    w_in_p = jnp.concatenate([win[:, 2048:3584], win[:, 0:512], win[:, 1024:2048], win[:, 512:768], win[:, 768:1024],
                              win[:, 3584:3600], jnp.zeros((D_MODEL, PROJ_P - IN_PROJ), BF16)], axis=1)
    wts = {n: w2[n] for n in REPLICATED}
    wts.update(w_in_p=w_in_p, ssm_conv_w=full["ssm_conv_w"], ffn_conv_w=full["ffn_conv_w"])

    def join_late(gathered):
        late = {n: join(n, g) for n, g in zip(EARLY_REDUCED, gathered)}
        return dict(w_out_attn=late["w_out"][:ATTN_DIM], w_out_ssm=late["w_out"][ATTN_DIM:], w_up=late["w_up"],
                    w_down=late["w_down"], w_ple_gate=late["w_ple_gate"], w_ple_proj=late["w_ple_proj"])

    core = cc.astype(jnp.int32).reshape(1)
    idx = jnp.stack([chip, 2 * (1 - cx) + cy, 2 * cx + (1 - cy), 2 * (1 - cx) + (1 - cy), cc]).astype(jnp.int32)

    def chip_sums_of(tag, names, gd):
        major = [gd[n] if gd[n].ndim == 3 else _split_chips(gd[n], axis_of[n]) for n in names]
        return [_add_halves("grad_add_halves_" + n, g, got, core) for n, g, got in zip(names, major, _swap_halves(tag, major))]

    def w_in_sums(gd):
        gi = gd["w_in_p"]
        gd["w_in"] = jnp.concatenate([gi[:, OFF_Q:OFF_Q + ATTN_DIM], gi[:, OFF_K:OFF_K + KV_DIM], gi[:, OFF_V:OFF_V + KV_DIM],
                                      gi[:, OFF_Z:OFF_Z + SSM_INNER], gi[:, OFF_XBC:OFF_XBC + XBC_DIM], gi[:, OFF_DT:OFF_DT + SSM_HEADS]],
                                     axis=1)
        return chip_sums_of("grad_swap_halves_late", ("w_in",), gd)

    sq, grad_x, grads, early, late = _local_step(
        x[0], p[0, 0], loss_target[0], wts, [shard(n) for n in EARLY_REDUCED], join_late,
        functools.partial(chip_sums_of, "grad_swap_halves_early", EARLY_REDUCED), w_in_sums)
    sums = dict(zip(EARLY_REDUCED + ("w_in",), list(zip(*early)) + list(zip(*late))))
    halves = [_sum_chips("grad_sum_chips_" + n, *sums[n], idx) for n in MATRICES]
    g_shard = dict(zip(MATRICES, _share_with_sibling(halves)))

    small_names = REPLICATED + ("ssm_conv_w", "ffn_conv_w")
    small_like = [grads[n] for n in small_names] + [jnp.zeros((1,), F32)]
    small = _allreduce_small(_pack_small([grads[n] for n in small_names] + [jnp.sum(sq).reshape(1)], ALL_SMALL_ROWS))
    small_vals = dict(zip(small_names + ("loss",), _unpack_small(small, small_like)))
    loss = (0.5 / D_MODEL) * small_vals["loss"][0]
    for n in ("ssm_conv_w", "ffn_conv_w"):
        cols = w2[n].shape[1]
        g_shard[n] = lax.dynamic_slice_in_dim(small_vals[n], chip * cols, cols, axis=1)[None]

    delta, new_m, new_v = {}, {}, {}
    for n, _ in SHARDED:
        delta[n], new_m[n], new_v[n] = _adamw("adamw_" + n, given[n], g_shard[n], given["m_" + n], given["v_" + n])
    packed = lambda prefix: _pack_small([given[prefix + n] for n in REPLICATED])[None]
    sm = _adamw("adamw_small", packed(""), _pack_small([small_vals[n] for n in REPLICATED])[None], packed("m_"), packed("v_"))
    for n in REPLICATED:
        g_shard[n] = small_vals[n]
    for dst, packed_out in zip((delta, new_m, new_v), sm):
        for n, val in zip(REPLICATED, _unpack_small(packed_out[0], [w2[n] for n in REPLICATED])):
            dst[n] = val

    def shaped(d):
        return [d[n].reshape(given[n].shape) for n in WEIGHT_ORDER]
    return (loss, grad_x[None], *shaped(g_shard), *shaped(delta), *shaped(new_m), *shaped(new_v))
```

```python
import functools

import jax
import jax.numpy as jnp
from jax import lax
from jax.experimental import pallas as pl
from jax.experimental.pallas import tpu as pltpu

F32 = jnp.float32
BF16 = jnp.bfloat16

D_MODEL = 1024
HEAD_DIM = 64
ATTN_DIM = 512
KV_DIM = 256
N_KV = 4
SSM_INNER = 1024
SSM_HEADS = 16
SSM_STATE = 128
BC_DIM = 256
XBC_DIM = SSM_INNER + 2 * BC_DIM
MIX_DIM = ATTN_DIM + SSM_INNER
IN_PROJ = 3600
D_FF = 2816
PLE_DIM = 256
CHUNK = 128
DILATIONS = (1, 4, 16)
EPS = 1e-6
ADAM_LR, ADAM_B1, ADAM_B2, ADAM_EPS, ADAM_WD, ADAM_STEP = 0.001, 0.9, 0.999, 1e-08, 0.01, 10

PROJ_P = 3712
OFF_XBC, OFF_Q, OFF_Z, OFF_K, OFF_V, OFF_DT = 0, 1536, 2048, 3072, 3328, 3584
LANE = 128
VMEM_LIMIT = 48 * 1024 * 1024
NEG = -1e30


def _cparams(sem):
    return pltpu.CompilerParams(dimension_semantics=sem, vmem_limit_bytes=VMEM_LIMIT)


def _sigmoid(x):
    return 1.0 / (1.0 + jnp.exp(-x))


def _dot(a, b):
    return jnp.dot(a, b, preferred_element_type=F32)


def _dot_nt(a, b):
    return lax.dot_general(a, b, (((1,), (1,)), ((), ())), preferred_element_type=F32)


def _dot_tn(a, b):
    return lax.dot_general(a, b, (((0,), (0,)), ((), ())), preferred_element_type=F32)


def _dot_split(x, m):
    hi = x.astype(BF16)
    lo = (x - hi.astype(F32)).astype(BF16)
    return _dot(hi, m) + _dot(lo, m)


def _rows(name, body, ins, outs, accs=(), tm=512):
    t_rows = next(s[1].shape[0] for s in ins if s[0] in ("t", "tc"))
    tm = min(tm, t_rows)
    in_specs, args = [], []
    for s in ins:
        if s[0] == "t":
            in_specs.append(pl.BlockSpec((tm, s[1].shape[1]), lambda i: (i, 0)))
        elif s[0] == "tc":
            in_specs.append(pl.BlockSpec((tm, s[2]), functools.partial(lambda i, c: (i, c), c=s[3])))
        else:
            in_specs.append(pl.BlockSpec(s[1].shape, lambda i: (0, 0)))
        args.append(s[1])
    out_shape = [jax.ShapeDtypeStruct((t_rows, w), dt) for w, dt in outs]
    out_specs = [pl.BlockSpec((tm, w), lambda i: (i, 0)) for w, _ in outs]
    out_shape += [jax.ShapeDtypeStruct(a, F32) for a in accs]
    out_specs += [pl.BlockSpec(a, lambda i: (0, 0)) for a in accs]
    n_acc = len(accs)

    def kern(*refs):
        if n_acc:
            @pl.when(pl.program_id(0) == 0)
            def _():
                for r in refs[len(refs) - n_acc:]:
                    r[...] = jnp.zeros(r.shape, F32)
        body(*refs)

    return pl.pallas_call(
        kern, name=name, grid=(t_rows // tm,), in_specs=in_specs, out_specs=out_specs, out_shape=out_shape,
        compiler_params=_cparams(("arbitrary",) if n_acc else ("parallel",)))(*args)


NCHUNK = 512


def _col_chunks(n):
    return [(c, min(NCHUNK, n - c)) for c in range(0, n, NCHUNK)]


def _mm_nn(name, pairs, out_dtype, res=None, tm=512, tn=None, halves=False):
    m, n = pairs[0][0].shape[0], pairs[0][1].shape[1]
    tn = n if tn is None else tn
    tm = min(tm, m)
    np_ = len(pairs)
    if halves:
        per = n // 2 // tn
        out_spec = pl.BlockSpec((None, tm, tn), lambda j, i: (j // per, i, j % per))
        out_shape = jax.ShapeDtypeStruct((2, m, n // 2), out_dtype)
    else:
        out_spec = pl.BlockSpec((tm, tn), lambda j, i: (i, j))
        out_shape = jax.ShapeDtypeStruct((m, n), out_dtype)
    in_specs, args = [], []
    for a, w in pairs:
        in_specs += [pl.BlockSpec((tm, a.shape[1]), lambda j, i: (i, 0)), pl.BlockSpec((w.shape[0], tn), lambda j, i: (0, j))]
        args += [a, w]
    if res is not None:
        in_specs.append(pl.BlockSpec((tm, tn), lambda j, i: (i, j)))
        args.append(res)

    def kern(*refs):
        o_ref = refs[-1]
        for c0, cw in _col_chunks(tn):
            acc = None
            for q in range(np_):
                d = _dot(refs[2 * q][...], refs[2 * q + 1][:, c0:c0 + cw])
                acc = d if acc is None else acc + d
            if res is not None:
                acc = acc + refs[2 * np_][:, c0:c0 + cw]
            o_ref[:, c0:c0 + cw] = acc.astype(o_ref.dtype)

    return pl.pallas_call(
        kern, name=name, grid=(n // tn, m // tm), in_specs=in_specs, out_specs=out_spec, out_shape=out_shape,
        compiler_params=_cparams(("parallel", "parallel")))(*args)


def _mm_nt(name, pairs, out_dtype, tm=512, tn=None):
    m, n = pairs[0][0].shape[-2], pairs[0][1].shape[0]
    tn = n if tn is None else tn
    tm = min(tm, m)
    np_ = len(pairs)
    in_specs, args = [], []
    for a, w, kb, *lead in pairs:
        if lead:
            in_specs.append(pl.BlockSpec((None, tm, a.shape[2]), functools.partial(lambda j, i, ld: (ld, i, 0), ld=lead[0])))
        else:
            in_specs.append(pl.BlockSpec((tm, a.shape[1]), lambda j, i: (i, 0)))
        in_specs.append(pl.BlockSpec((tn, a.shape[-1]), functools.partial(lambda j, i, kb: (j, kb), kb=kb)))
        args += [a, w]

    def kern(*refs):
        o_ref = refs[-1]
        for c0, cw in _col_chunks(tn):
            acc = None
            for q in range(np_):
                d = _dot_nt(refs[2 * q][...], refs[2 * q + 1][c0:c0 + cw, :])
                acc = d if acc is None else acc + d
            o_ref[:, c0:c0 + cw] = acc.astype(o_ref.dtype)

    return pl.pallas_call(
        kern, name=name, grid=(n // tn, m // tm), in_specs=in_specs,
        out_specs=pl.BlockSpec((tm, tn), lambda j, i: (i, j)),
        out_shape=jax.ShapeDtypeStruct((m, n), out_dtype), compiler_params=_cparams(("parallel", "parallel")))(*args)


def _mm_tn(name, a, b, tm=None, tn=None, tk=1024, chip_cols=False):
    t, m = a.shape
    n = b.shape[-1] * (2 if b.ndim == 3 else 1)
    tm = m if tm is None else tm
    tn = n if tn is None else tn
    tk = min(tk, t)
    if b.ndim == 3:
        per = n // 2 // tn
        b_spec = pl.BlockSpec((None, tk, tn), lambda i, j, k: (j // per, k, j % per))
    else:
        b_spec = pl.BlockSpec((tk, tn), lambda i, j, k: (k, j))
    if chip_cols:
        out_spec = pl.BlockSpec((None, tm, tn), lambda i, j, k: (j, i, 0))
        out_shape = jax.ShapeDtypeStruct((n // tn, m, tn), F32)
    else:
        out_spec = pl.BlockSpec((tm, tn), lambda i, j, k: (i, j))
        out_shape = jax.ShapeDtypeStruct((m, n), F32)

    def kern(a_ref, b_ref, o_ref):
        @pl.when(pl.program_id(2) == 0)
        def _():
            o_ref[...] = jnp.zeros(o_ref.shape, F32)
        for c0, cw in _col_chunks(tn):
            o_ref[:, c0:c0 + cw] += _dot_tn(a_ref[...], b_ref[:, c0:c0 + cw])

    return pl.pallas_call(
        kern, name=name, grid=(m // tm, n // tn, t // tk),
        in_specs=[pl.BlockSpec((tk, tm), lambda i, j, k: (k, i)), b_spec], out_specs=out_spec, out_shape=out_shape,
        compiler_params=_cparams(("parallel", "parallel", "arbitrary")))(a, b)


def _rms_fwd(name, x, g):
    def body(x_ref, g_ref, h_ref):
        xv = x_ref[...]
        r = lax.rsqrt(jnp.mean(xv * xv, axis=-1, keepdims=True) + EPS)
        h_ref[...] = (xv * r * g_ref[...]).astype(BF16)
    return _rows(name, body, [("t", x), ("p", g)], [(x.shape[1], BF16)])[0]


def _rms_bwd(name, dh, x, g, dres):
    d = x.shape[1]

    def body(dh_ref, x_ref, g_ref, dres_ref, dx_ref, dxb_ref, dg_ref):
        xv, dhv = x_ref[...], dh_ref[...]
        r = lax.rsqrt(jnp.mean(xv * xv, axis=-1, keepdims=True) + EPS)
        gd = dhv * g_ref[...]
        dx = dres_ref[...] + r * gd - xv * (r * r * r * jnp.mean(xv * gd, axis=-1, keepdims=True))
        dx_ref[...] = dx
        dxb_ref[...] = dx.astype(BF16)
        dg_ref[...] += jnp.sum(dhv * xv * r, axis=0, keepdims=True)
    return _rows(name, body, [("t", dh), ("t", x), ("p", g), ("t", dres)], [(d, F32), (d, BF16)], accs=[(1, d)])


def _norm_mm(name, x, g, w, tm=512, tn=None, halves=False):
    m, k = x.shape
    n = w.shape[1]
    tn = n if tn is None else tn
    if halves:
        per = n // 2 // tn
        o_spec = pl.BlockSpec((None, tm, tn), lambda i, j: (j // per, i, j % per))
        o_shape = jax.ShapeDtypeStruct((2, m, n // 2), F32)
    else:
        o_spec = pl.BlockSpec((tm, tn), lambda i, j: (i, j))
        o_shape = jax.ShapeDtypeStruct((m, n), F32)

    def kern(x_ref, g_ref, w_ref, h_ref, o_ref):
        xv = x_ref[...]
        h = (xv * lax.rsqrt(jnp.mean(xv * xv, axis=-1, keepdims=True) + EPS) * g_ref[...]).astype(BF16)
        h_ref[...] = h
        for c0, cw in _col_chunks(tn):
            o_ref[:, c0:c0 + cw] = _dot(h, w_ref[:, c0:c0 + cw])

    return pl.pallas_call(
        kern, name=name, grid=(m // tm, n // tn),
        in_specs=[pl.BlockSpec((tm, k), lambda i, j: (i, 0)), pl.BlockSpec((1, k), lambda i, j: (0, 0)),
                  pl.BlockSpec((k, tn), lambda i, j: (0, j))],
        out_specs=[pl.BlockSpec((tm, k), lambda i, j: (i, 0)), o_spec],
        out_shape=[jax.ShapeDtypeStruct((m, k), BF16), o_shape],
        compiler_params=_cparams(("parallel", "arbitrary")))(x, g, w)


def _mm_nt_rms_bwd(name, a, w, x, g, dres, parts=(), tm=512):
    m, k = a.shape
    n = w.shape[0]
    ns, steps = len(parts), m // tm

    def kern(a_ref, w_ref, x_ref, g_ref, dres_ref, *rest):
        dx_ref, dxb_ref, dg_ref = rest[ns:ns + 3]
        dh = rest[2 * ns + 3]
        if ns:
            start, finish = _scatter_steps(ns, rest[:ns], rest[ns + 3:2 * ns + 3], rest[2 * ns + 4:])
            pl.when(pl.program_id(0) == 0)(start)
            pl.when(pl.program_id(0) == steps - 1)(finish)

        @pl.when(pl.program_id(0) == 0)
        def _():
            dg_ref[...] = jnp.zeros(dg_ref.shape, F32)
        av = a_ref[...]
        for c0, cw in _col_chunks(n):
            dh[:, c0:c0 + cw] = _dot_nt(av, w_ref[c0:c0 + cw, :])
        xv, dhv = x_ref[...], dh[...]
        r = lax.rsqrt(jnp.mean(xv * xv, axis=-1, keepdims=True) + EPS)
        gd = dhv * g_ref[...]
        dx = dres_ref[...] + r * gd - xv * (r * r * r * jnp.mean(xv * gd, axis=-1, keepdims=True))
        dx_ref[...] = dx
        dxb_ref[...] = dx.astype(BF16)
        dg_ref[...] += jnp.sum(dhv * xv * r, axis=0, keepdims=True)

    row = lambda width: pl.BlockSpec((tm, width), lambda i: (i, 0))
    return pl.pallas_call(
        kern, name=name, grid=(steps,),
        in_specs=[row(k), pl.BlockSpec((n, k), lambda i: (0, 0)), row(n), pl.BlockSpec((1, n), lambda i: (0, 0)), row(n)]
        + [ANY_SPEC] * ns,
        out_specs=[row(n), row(n), pl.BlockSpec((1, n), lambda i: (0, 0))] + [ANY_SPEC] * ns,
        out_shape=[jax.ShapeDtypeStruct((m, n), F32), jax.ShapeDtypeStruct((m, n), BF16), jax.ShapeDtypeStruct((1, n), F32)]
        + [jax.ShapeDtypeStruct(s.shape, s.dtype) for s in parts],
        scratch_shapes=[pltpu.VMEM((tm, n), F32)] + (_scatter_sems(ns) if ns else []),
        compiler_params=_cparams(("arbitrary",)))(a, w, x, g, dres, *parts)


def _head_mean_matrix(width):
    i = jnp.arange(width) // HEAD_DIM
    return jnp.where(i[:, None] == i[None, :], 1.0 / HEAD_DIM, 0.0).astype(BF16)


ATT_SPAN = 2048
N_QH = 8


def _lane_lo(rows):
    return lax.broadcasted_iota(jnp.int32, (rows, LANE), 1) < HEAD_DIM


def _swap_halves_lanes(x):
    return pltpu.roll(x, HEAD_DIM, axis=1)


def _qknorm_fwd2(proj, gq_t, gk_t, tm=256):
    t = proj.shape[0]
    bq, bk = _head_mean_matrix(ATTN_DIM), _head_mean_matrix(KV_DIM)
    scale = HEAD_DIM ** -0.5

    def kern(q_ref, k_ref, v_ref, gq_ref, gk_ref, bq_ref, bk_ref, qo_ref, kvo_ref):
        q, k, v = q_ref[...], k_ref[...], v_ref[...]
        qn = (q * lax.rsqrt(_dot_split(q * q, bq_ref[...]) + EPS) * gq_ref[...]) * scale
        kn = k * lax.rsqrt(_dot_split(k * k, bk_ref[...]) + EPS) * gk_ref[...]
        lo = _lane_lo(tm)
        for j in range(N_KV):
            blk = qn[:, j * LANE:(j + 1) * LANE]
            qo_ref[2 * j] = jnp.where(lo, blk, 0.0)
            qo_ref[2 * j + 1] = jnp.where(lo, _swap_halves_lanes(blk), 0.0)
        for j in range(2):
            kb, vb = kn[:, j * LANE:(j + 1) * LANE], v[:, j * LANE:(j + 1) * LANE]
            kvo_ref[2 * j] = jnp.where(lo, kb, _swap_halves_lanes(vb))
            kvo_ref[2 * j + 1] = jnp.where(lo, _swap_halves_lanes(kb), vb)

    col = lambda w, idx: pl.BlockSpec((tm, w), functools.partial(lambda i, idx: (i, idx), idx=idx))
    par = lambda a: pl.BlockSpec(a.shape, lambda i: (0, 0))
    return pl.pallas_call(
        kern, name="qknorm_fwd", grid=(t // tm,),
        in_specs=[col(ATTN_DIM, OFF_Q // ATTN_DIM), col(KV_DIM, OFF_K // KV_DIM), col(KV_DIM, OFF_V // KV_DIM),
                  par(gq_t), par(gk_t), par(bq), par(bk)],
        out_specs=[pl.BlockSpec((N_QH, tm, LANE), lambda i: (0, i, 0)), pl.BlockSpec((N_KV, tm, LANE), lambda i: (0, i, 0))],
        out_shape=[jax.ShapeDtypeStruct((N_QH, t, LANE), F32), jax.ShapeDtypeStruct((N_KV, t, LANE), F32)],
        compiler_params=_cparams(("parallel",)))(proj, proj, proj, gq_t, gk_t, bq, bk)


def _qknorm_bwd2(proj, gq_t, gk_t, dqs, dkvs, tm=256):
    t = proj.shape[0]
    bq, bk = _head_mean_matrix(ATTN_DIM), _head_mean_matrix(KV_DIM)
    scale = HEAD_DIM ** -0.5

    def kern(q_ref, k_ref, gq_ref, gk_ref, bq_ref, bk_ref, a1, a2, a3, b1, b2, b3, dq_ref, dk_ref, dv_ref, dgq_ref, dgk_ref):
        @pl.when(pl.program_id(0) == 0)
        def _():
            dgq_ref[...] = jnp.zeros(dgq_ref.shape, F32)
            dgk_ref[...] = jnp.zeros(dgk_ref.shape, F32)
        lo = _lane_lo(tm)
        sq = [a1[h] + a2[h] + a3[h] for h in range(N_QH)]
        skv = [b1[h] + b2[h] + b3[h] for h in range(N_KV)]
        dqn = jnp.concatenate([jnp.where(lo, sq[2 * j], _swap_halves_lanes(sq[2 * j + 1])) for j in range(N_KV)], axis=1) * scale
        dkn = jnp.concatenate([jnp.where(lo, skv[2 * j], _swap_halves_lanes(skv[2 * j + 1])) for j in range(2)], axis=1)
        dv = jnp.concatenate([jnp.where(lo, _swap_halves_lanes(skv[2 * j]), skv[2 * j + 1]) for j in range(2)], axis=1)
        q, k = q_ref[...], k_ref[...]
        rq = lax.rsqrt(_dot_split(q * q, bq_ref[...]) + EPS)
        rk = lax.rsqrt(_dot_split(k * k, bk_ref[...]) + EPS)
        gdq, gdk = dqn * gq_ref[...], dkn * gk_ref[...]
        dq_ref[...] = (rq * gdq - q * (rq * rq * rq * _dot_split(q * gdq, bq_ref[...]))).astype(BF16)
        dk_ref[...] = (rk * gdk - k * (rk * rk * rk * _dot_split(k * gdk, bk_ref[...]))).astype(BF16)
        dv_ref[...] = dv.astype(BF16)
        dgq_ref[...] += jnp.sum(dqn * q * rq, axis=0, keepdims=True)
        dgk_ref[...] += jnp.sum(dkn * k * rk, axis=0, keepdims=True)

    col = lambda w, idx: pl.BlockSpec((tm, w), functools.partial(lambda i, idx: (i, idx), idx=idx))
    par = lambda a: pl.BlockSpec(a.shape, lambda i: (0, 0))
    blk = lambda n: pl.BlockSpec((n, tm, LANE), lambda i: (0, i, 0))
    row = lambda w: pl.BlockSpec((tm, w), lambda i: (i, 0))
    acc = lambda w: pl.BlockSpec((1, w), lambda i: (0, 0))
    return pl.pallas_call(
        kern, name="qknorm_bwd", grid=(t // tm,),
        in_specs=[col(ATTN_DIM, OFF_Q // ATTN_DIM), col(KV_DIM, OFF_K // KV_DIM), par(gq_t), par(gk_t), par(bq), par(bk)]
        + [blk(N_QH)] * 3 + [blk(N_KV)] * 3,
        out_specs=[row(ATTN_DIM), row(KV_DIM), row(KV_DIM), acc(ATTN_DIM), acc(KV_DIM)],
        out_shape=[jax.ShapeDtypeStruct((t, ATTN_DIM), BF16), jax.ShapeDtypeStruct((t, KV_DIM), BF16),
                   jax.ShapeDtypeStruct((t, KV_DIM), BF16), jax.ShapeDtypeStruct((1, ATTN_DIM), F32),
                   jax.ShapeDtypeStruct((1, KV_DIM), F32)],
        compiler_params=_cparams(("arbitrary",)))(proj, proj, gq_t, gk_t, bq, bk, *dqs, *dkvs)


def _att_rows(b, r, dil):
    if dil == 1:
        return pl.ds(b * CHUNK, CHUNK)
    return pl.ds(b * CHUNK * dil + r, CHUNK, stride=dil)


def _for_residues(dil, unit):
    for r in range(dil):
        unit(r, 0)


def _band_qk(first):
    ri = lax.broadcasted_iota(jnp.int32, (CHUNK, 2 * CHUNK), 0)
    cj = lax.broadcasted_iota(jnp.int32, (CHUNK, 2 * CHUNK), 1)
    band = (cj - ri >= 0) & (cj - ri <= CHUNK)
    return band if first is None else band & (jnp.logical_not(first) | (cj >= CHUNK))


def _band_kq(last):
    rj = lax.broadcasted_iota(jnp.int32, (CHUNK, 2 * CHUNK), 0)
    ci = lax.broadcasted_iota(jnp.int32, (CHUNK, 2 * CHUNK), 1)
    band = (ci - rj >= 0) & (ci - rj <= CHUNK)
    return band if last is None else band & (jnp.logical_not(last) | (ci < CHUNK))


def _att_specs(t, dil):
    sub = CHUNK * dil
    nb, last = ATT_SPAN // sub, t // sub - 1
    cur = lambda heads: pl.BlockSpec((heads, ATT_SPAN, LANE), lambda kh, n: (kh, n, 0))
    prev = lambda heads: pl.BlockSpec((heads, sub, LANE), lambda kh, n: (kh, jnp.maximum(n * nb - 1, 0), 0))
    nxt = lambda heads: pl.BlockSpec((heads, sub, LANE), lambda kh, n: (kh, jnp.minimum((n + 1) * nb, last), 0))
    return sub, nb, cur, prev, nxt


def _attn_fwd2(q, kv, dil):
    t = q.shape[1]
    sub, nb, cur, prev, _ = _att_specs(t, dil)

    def kern(q_ref, kvp_ref, kvc_ref, o_ref, lse_ref):
        n = pl.program_id(1)
        lane = lax.broadcasted_iota(jnp.int32, (CHUNK, LANE), 1)
        for b in range(nb):
            mask = _band_qk((n == 0) if b == 0 else None)

            def unit(r, carry, b=b, mask=mask):
                rows = _att_rows(b, r, dil)
                kvp = kvc_ref[_att_rows(b - 1, r, dil), :] if b > 0 else kvp_ref[_att_rows(0, r, dil), :]
                kvcat = jnp.concatenate([kvp, kvc_ref[rows, :]], axis=0).astype(BF16)
                lse_tile = jnp.zeros((CHUNK, LANE), F32)
                for g in range(2):
                    s = jnp.where(mask, _dot_nt(q_ref.at[g][rows, :].astype(BF16), kvcat), NEG)
                    m = jnp.max(s, axis=1, keepdims=True)
                    p = jnp.exp(s - m)
                    l = jnp.sum(p, axis=1, keepdims=True)
                    o_ref.at[g][rows, :] = _dot(p.astype(BF16), kvcat) * (1.0 / l)
                    lse_tile = jnp.where(lane == g, m + jnp.log(l), lse_tile)
                lse_ref[rows, :] = lse_tile
                return carry
            _for_residues(dil, unit)

    return pl.pallas_call(
        kern, name=f"attn_fwd_d{dil}", grid=(N_KV, t // ATT_SPAN), in_specs=[cur(2), prev(None), cur(None)],
        out_specs=[cur(2), cur(None)],
        out_shape=[jax.ShapeDtypeStruct((N_QH, t, LANE), F32), jax.ShapeDtypeStruct((N_KV, t, LANE), F32)],
        compiler_params=_cparams(("parallel", "parallel")))(q, kv, kv)


def _attn_merge2(os_, lses, tm=256):
    t = os_[0].shape[1]

    def kern(o1, o2, o3, l1, l2, l3, out_ref, lse_ref):
        pieces = []
        for kh in range(N_KV):
            a, b, c = l1[kh], l2[kh], l3[kh]
            m = jnp.maximum(jnp.maximum(a, b), c)
            tot = m + jnp.log(jnp.exp(a - m) + jnp.exp(b - m) + jnp.exp(c - m))
            lse_ref[kh] = tot
            wa, wb, wc = jnp.exp(a - tot), jnp.exp(b - tot), jnp.exp(c - tot)
            for g in range(2):
                h = 2 * kh + g
                acc = wa[:, g:g + 1] * o1[h] + wb[:, g:g + 1] * o2[h] + wc[:, g:g + 1] * o3[h]
                pieces.append(acc[:, HEAD_DIM:])
        out_ref[...] = jnp.concatenate(pieces, axis=1).astype(BF16)

    blk = lambda n: pl.BlockSpec((n, tm, LANE), lambda i: (0, i, 0))
    return pl.pallas_call(
        kern, name="attn_merge", grid=(t // tm,), in_specs=[blk(N_QH)] * 3 + [blk(N_KV)] * 3,
        out_specs=[pl.BlockSpec((tm, ATTN_DIM), lambda i: (i, 0)), blk(N_KV)],
        out_shape=[jax.ShapeDtypeStruct((t, ATTN_DIM), BF16), jax.ShapeDtypeStruct((N_KV, t, LANE), F32)],
        compiler_params=_cparams(("parallel",)))(*os_, *lses)


def _attn_bwd_prep2(dmix, attn_out, tm=256):
    t = attn_out.shape[0]

    def kern(do_ref, o_ref, dot_ref, d_ref):
        do = do_ref[...]
        prod = do * o_ref[...].astype(F32)
        lo = _lane_lo(tm)
        lane = lax.broadcasted_iota(jnp.int32, (tm, LANE), 1)
        for kh in range(N_KV):
            blk, pb = do[:, kh * LANE:(kh + 1) * LANE], prod[:, kh * LANE:(kh + 1) * LANE]
            dot_ref[2 * kh] = jnp.where(lo, 0.0, _swap_halves_lanes(blk))
            dot_ref[2 * kh + 1] = jnp.where(lo, 0.0, blk)
            s_lo = jnp.sum(jnp.where(lo, pb, 0.0), axis=1, keepdims=True)
            s_hi = jnp.sum(pb, axis=1, keepdims=True) - s_lo
            d_ref[kh] = jnp.where(lane == 0, s_lo, jnp.where(lane == 1, s_hi, 0.0))

    blk = lambda n: pl.BlockSpec((n, tm, LANE), lambda i: (0, i, 0))
    return pl.pallas_call(
        kern, name="attn_bwd_prep", grid=(t // tm,),
        in_specs=[pl.BlockSpec((tm, ATTN_DIM), lambda i: (i, SSM_INNER // ATTN_DIM)), pl.BlockSpec((tm, ATTN_DIM), lambda i: (i, 0))],
        out_specs=[blk(N_QH), blk(N_KV)],
        out_shape=[jax.ShapeDtypeStruct((N_QH, t, LANE), F32), jax.ShapeDtypeStruct((N_KV, t, LANE), F32)],
        compiler_params=_cparams(("parallel",)))(dmix, attn_out)


def _attn_dq2(q, kv, dot, lse, dsum, dil):
    t = q.shape[1]
    sub, nb, cur, prev, _ = _att_specs(t, dil)

    def kern(q_ref, kvp_ref, kvc_ref, do_ref, lse_ref, d_ref, dq_ref):
        n = pl.program_id(1)
        for b in range(nb):
            mask = _band_qk((n == 0) if b == 0 else None)

            def unit(r, carry, b=b, mask=mask):
                rows = _att_rows(b, r, dil)
                kvp = kvc_ref[_att_rows(b - 1, r, dil), :] if b > 0 else kvp_ref[_att_rows(0, r, dil), :]
                kvcat = jnp.concatenate([kvp, kvc_ref[rows, :]], axis=0).astype(BF16)
                lse_t, d_t = lse_ref[rows, :], d_ref[rows, :]
                for g in range(2):
                    s = jnp.where(mask, _dot_nt(q_ref.at[g][rows, :].astype(BF16), kvcat), NEG)
                    p = jnp.exp(s - lse_t[:, g:g + 1])
                    dp = _dot_nt(do_ref.at[g][rows, :].astype(BF16), kvcat)
                    ds = p * (dp - d_t[:, g:g + 1])
                    dq_ref.at[g][rows, :] = _dot(ds.astype(BF16), kvcat)
                return carry
            _for_residues(dil, unit)

    return pl.pallas_call(
        kern, name=f"attn_dq_d{dil}", grid=(N_KV, t // ATT_SPAN),
        in_specs=[cur(2), prev(None), cur(None), cur(2), cur(None), cur(None)], out_specs=cur(2),
        out_shape=jax.ShapeDtypeStruct((N_QH, t, LANE), F32),
        compiler_params=_cparams(("parallel", "parallel")))(q, kv, kv, dot, lse, dsum)


def _attn_dkv2(q, kv, dot, lse, dsum, dil):
    t = q.shape[1]
    sub, nb, cur, _, nxt = _att_specs(t, dil)
    nsteps = t // ATT_SPAN

    def kern(kv_ref, qc_ref, qn_ref, doc_ref, don_ref, lc_ref, ln_ref, dc_ref, dn_ref, dkv_ref):
        n = pl.program_id(1)
        for b in range(nb):
            inside = b < nb - 1
            mask = _band_kq(None if inside else (n == nsteps - 1))

            def unit(r, carry, b=b, inside=inside, mask=mask):
                rows = _att_rows(b, r, dil)
                nrows = _att_rows(b + 1, r, dil) if inside else _att_rows(0, r, dil)
                kvb = kv_ref[rows, :].astype(BF16)
                follow = lambda cref, nref: (cref if inside else nref)[nrows, :]
                lse_t = jnp.concatenate([lc_ref[rows, :].T, follow(lc_ref, ln_ref).T], axis=1)
                d_t = jnp.concatenate([dc_ref[rows, :].T, follow(dc_ref, dn_ref).T], axis=1)
                acc = jnp.zeros((CHUNK, LANE), F32)
                for g in range(2):
                    qdo = jnp.concatenate([qc_ref.at[g][rows, :], follow(qc_ref.at[g], qn_ref.at[g]),
                                           doc_ref.at[g][rows, :], follow(doc_ref.at[g], don_ref.at[g])], axis=0).astype(BF16)
                    both = _dot_nt(kvb, qdo)
                    pt = jnp.exp(jnp.where(mask, both[:, :2 * CHUNK], NEG) - lse_t[g:g + 1, :])
                    dst = pt * (both[:, 2 * CHUNK:] - d_t[g:g + 1, :])
                    acc = acc + _dot(jnp.concatenate([dst, pt], axis=1).astype(BF16), qdo)
                dkv_ref[rows, :] = acc
                return carry
            _for_residues(dil, unit)

    return pl.pallas_call(
        kern, name=f"attn_dkv_d{dil}", grid=(N_KV, nsteps),
        in_specs=[cur(None), cur(2), nxt(2), cur(2), nxt(2), cur(None), nxt(None), cur(None), nxt(None)], out_specs=cur(None),
        out_shape=jax.ShapeDtypeStruct((N_KV, t, LANE), F32),
        compiler_params=_cparams(("parallel", "parallel")))(kv, q, q, dot, dot, lse, lse, dsum, dsum)


HALO = 8
SSM_CONV_TM, SSM_CONV_W = 512, 512
FFN_CONV_TM, FFN_CONV_W = 256, 1408


def _halo_specs(tm, width, t_rows, col_off=0, lead=None):
    per, last = tm // HALO, t_rows // HALO - 1
    row_maps = (lambda i: i, lambda i: jnp.maximum(i * per - 1, 0), lambda i: jnp.minimum((i + 1) * per, last))
    specs = []
    for rows, rm in zip((tm, HALO, HALO), row_maps):
        if lead is None:
            specs.append(pl.BlockSpec((rows, width), functools.partial(lambda c, i, rm: (rm(i), c + col_off), rm=rm)))
        else:
            specs.append(pl.BlockSpec((None, rows, width), functools.partial(lambda c, i, rm: (lead, rm(i), c + col_off), rm=rm)))
    return specs


def _fill_ext(buf, tile_ref, before_ref, after_ref, i, nt):
    tm = tile_ref.shape[0]
    buf[0:HALO, :] = jnp.where(i > 0, before_ref[...].astype(F32), 0.0)
    buf[HALO:HALO + tm, :] = tile_ref[...].astype(F32)
    if after_ref is not None:
        buf[HALO + tm:, :] = jnp.where(i < nt - 1, after_ref[...].astype(F32), 0.0)


CONV_RB, CONV_CW = 16, 256


def _lane_chunks(width):
    return [slice(c0, min(c0 + CONV_CW, width)) for c0 in range(0, width, CONV_CW)]


def _shifted(buf, taps, r0, rows, cs):
    return [buf[pl.ds(HALO - (taps - 1) + k + r0, rows), cs] for k in range(taps)]


def _taps_fwd(xs, w, b):
    acc = b
    for k, xk in enumerate(xs):
        acc = acc + w[k:k + 1, :] * xk
    return acc


def _taps_bwd(bufd, w, taps, r0, rows, cs):
    acc = None
    for k in range(taps):
        term = w[k:k + 1, :] * bufd[pl.ds(r0 + (taps - 1) - k, rows), cs]
        acc = term if acc is None else acc + term
    return acc


def _fold8(z):
    return z[:HALO] + z[HALO:] if z.shape[0] == 2 * HALO else z


def _silu_grad(pre):
    sg = _sigmoid(pre)
    return sg * (1.0 + pre * (1.0 - sg))


def _ssm_conv_fwd(proj, w, b):
    t = proj.shape[0]
    tm, wd = min(SSM_CONV_TM, t), SSM_CONV_W
    nt, taps = t // tm, w.shape[0]

    def kern(x_ref, xb_ref, w_ref, b_ref, o_ref, buf):
        _fill_ext(buf, x_ref, xb_ref, None, pl.program_id(1), nt)
        for cs in _lane_chunks(wd):
            wv, bv = w_ref[:, cs], b_ref[:, cs]
            for r0 in range(0, tm, CONV_RB):
                pre = _taps_fwd(_shifted(buf, taps, r0, CONV_RB, cs), wv, bv)
                o_ref[r0:r0 + CONV_RB, cs] = pre * _sigmoid(pre)

    tile, before, _ = _halo_specs(tm, wd, t)
    par = lambda rows: pl.BlockSpec((rows, wd), lambda c, i: (0, c))
    return pl.pallas_call(
        kern, name="ssm_conv_fwd", grid=(XBC_DIM // wd, nt), in_specs=[tile, before, par(taps), par(1)],
        out_specs=pl.BlockSpec((tm, wd), lambda c, i: (i, c)), out_shape=jax.ShapeDtypeStruct((t, XBC_DIM), F32),
        scratch_shapes=[pltpu.VMEM((tm + HALO, wd), F32)],
        compiler_params=_cparams(("parallel", "parallel")))(proj, proj, w, b)


def _ssm_conv_bwd(proj, w, b, dact, parts):
    t = proj.shape[0]
    tm, wd = min(SSM_CONV_TM, t), SSM_CONV_W
    nt, taps, ncol, ns = t // tm, w.shape[0], XBC_DIM // SSM_CONV_W, len(parts)

    def kern(x_ref, xb_ref, xa_ref, d_ref, dn_ref, w_ref, b_ref, *rest):
        dx_ref, gw_ref, gb_ref = rest[ns:ns + 3]
        buf, bufd = rest[2 * ns + 3:2 * ns + 5]
        i = pl.program_id(1)
        if ns:
            start, finish = _scatter_steps(ns, rest[:ns], rest[ns + 3:2 * ns + 3], rest[2 * ns + 5:])
            pl.when((pl.program_id(0) == 0) & (i == 0))(start)
            pl.when((pl.program_id(0) == ncol - 1) & (i == nt - 1))(finish)
        _fill_ext(buf, x_ref, xb_ref, xa_ref, i, nt)

        @pl.when(i == 0)
        def _():
            gw_ref[...] = jnp.zeros(gw_ref.shape, F32)
            gb_ref[...] = jnp.zeros(gb_ref.shape, F32)
        for cs in _lane_chunks(wd):
            wv, bv = w_ref[:, cs], b_ref[:, cs]
            acc = [jnp.zeros((HALO, cs.stop - cs.start), F32) for _ in range(taps + 1)]
            for r0 in list(range(0, tm, CONV_RB)) + [tm]:
                inside = r0 < tm
                rows = CONV_RB if inside else HALO
                xs = _shifted(buf, taps, r0, rows, cs)
                d = d_ref[r0:r0 + rows, cs] if inside else jnp.where(i < nt - 1, dn_ref[:, cs], 0.0)
                dpre = d * _silu_grad(_taps_fwd(xs, wv, bv))
                bufd[r0:r0 + rows, cs] = dpre
                if inside:
                    acc[taps] = acc[taps] + _fold8(dpre)
                    for k in range(taps):
                        acc[k] = acc[k] + _fold8(dpre * xs[k])
            gb_ref[:, cs] += jnp.sum(acc[taps], axis=0, keepdims=True)
            for k in range(taps):
                gw_ref[k:k + 1, cs] += jnp.sum(acc[k], axis=0, keepdims=True)
            for r0 in range(0, tm, CONV_RB):
                dx_ref[r0:r0 + CONV_RB, cs] = _taps_bwd(bufd, wv, taps, r0, CONV_RB, cs).astype(BF16)

    xt, xb, xa = _halo_specs(tm, wd, t)
    dt_, _, dn = _halo_specs(tm, wd, t)
    par = lambda rows: pl.BlockSpec((rows, wd), lambda c, i: (0, c))
    return pl.pallas_call(
        kern, name="ssm_conv_bwd", grid=(ncol, nt), in_specs=[xt, xb, xa, dt_, dn, par(taps), par(1)] + [ANY_SPEC] * ns,
        out_specs=[pl.BlockSpec((tm, wd), lambda c, i: (i, c)), par(taps), par(1)] + [ANY_SPEC] * ns,
        out_shape=[jax.ShapeDtypeStruct((t, XBC_DIM), BF16), jax.ShapeDtypeStruct((taps, XBC_DIM), F32),
                   jax.ShapeDtypeStruct((1, XBC_DIM), F32)] + [jax.ShapeDtypeStruct(s.shape, s.dtype) for s in parts],
        scratch_shapes=[pltpu.VMEM((tm + 2 * HALO, wd), F32), pltpu.VMEM((tm + HALO, wd), F32)] + (_scatter_sems(ns) if ns else []),
        compiler_params=_cparams(("arbitrary", "arbitrary")))(proj, proj, proj, dact, dact, w, b, *parts)


def _ffn_act_down(u, w, b, w_down, x1):
    t = u.shape[1]
    tm, wd = min(FFN_CONV_TM, t), D_FF
    nt, taps = t // tm, w.shape[0]

    def kern(g_ref, gb_ref, v_ref, vb_ref, wg_ref, wv_ref, bg_ref, bv_ref, wd_ref, x1_ref, a_ref, x2_ref, bufg, bufv):
        i = pl.program_id(1)
        _fill_ext(bufg, g_ref, gb_ref, None, i, nt)
        _fill_ext(bufv, v_ref, vb_ref, None, i, nt)
        acc = x1_ref[...]
        for cs in _lane_chunks(wd):
            wg, wv, bg, bv = wg_ref[:, cs], wv_ref[:, cs], bg_ref[:, cs], bv_ref[:, cs]
            for r0 in range(0, tm, CONV_RB):
                g = _taps_fwd(_shifted(bufg, taps, r0, CONV_RB, cs), wg, bg)
                v = _taps_fwd(_shifted(bufv, taps, r0, CONV_RB, cs), wv, bv)
                a_ref[r0:r0 + CONV_RB, cs] = (g * _sigmoid(g) * v).astype(BF16)
            acc = acc + _dot(a_ref[:, cs], wd_ref[cs, :])
        x2_ref[...] = acc

    gt, gbf, _ = _halo_specs(tm, wd, t, lead=0)
    vt, vbf, _ = _halo_specs(tm, wd, t, lead=1)
    par = lambda rows, off: pl.BlockSpec((rows, wd), functools.partial(lambda c, i, off: (0, c + off), off=off))
    row = lambda width: pl.BlockSpec((tm, width), lambda c, i: (i, 0))
    return pl.pallas_call(
        kern, name="ffn_act_down", grid=(1, nt),
        in_specs=[gt, gbf, vt, vbf, par(taps, 0), par(taps, 1), par(1, 0), par(1, 1),
                  pl.BlockSpec(w_down.shape, lambda c, i: (0, 0)), row(D_MODEL)],
        out_specs=[row(wd), row(D_MODEL)],
        out_shape=[jax.ShapeDtypeStruct((t, D_FF), BF16), jax.ShapeDtypeStruct((t, D_MODEL), F32)],
        scratch_shapes=[pltpu.VMEM((tm + HALO, wd), F32)] * 2,
        compiler_params=_cparams(("parallel", "parallel")))(u, u, u, u, w, w, b, b, w_down, x1)


FFN_BWD_TM = 128


def _ffn_act_bwd(u, w, b, da, w_up):
    t = u.shape[1]
    tm, wd = min(FFN_BWD_TM, t), D_FF
    nt, taps = t // tm, w.shape[0]

    def kern(g_ref, gb_ref, ga_ref, v_ref, vb_ref, va_ref, d_ref, dn_ref, wg_ref, wv_ref, bg_ref, bv_ref, wup_ref,
             du_ref, gwg_ref, gwv_ref, gbg_ref, gbv_ref, dh_ref, bufg, bufv, bufdg, bufdv):
        i = pl.program_id(1)
        _fill_ext(bufg, g_ref, gb_ref, ga_ref, i, nt)
        _fill_ext(bufv, v_ref, vb_ref, va_ref, i, nt)

        @pl.when(i == 0)
        def _():
            for r in (gwg_ref, gwv_ref, gbg_ref, gbv_ref):
                r[...] = jnp.zeros(r.shape, F32)
        dh = None
        for cs in _lane_chunks(wd):
            wg, wv, bg, bv = wg_ref[:, cs], wv_ref[:, cs], bg_ref[:, cs], bv_ref[:, cs]
            zero = jnp.zeros((HALO, cs.stop - cs.start), F32)
            accg, accv = [zero] * (taps + 1), [zero] * (taps + 1)
            for r0 in list(range(0, tm, CONV_RB)) + [tm]:
                inside = r0 < tm
                rows = CONV_RB if inside else HALO
                xg, xv = _shifted(bufg, taps, r0, rows, cs), _shifted(bufv, taps, r0, rows, cs)
                g, v = _taps_fwd(xg, wg, bg), _taps_fwd(xv, wv, bv)
                dav = d_ref[r0:r0 + rows, cs] if inside else jnp.where(i < nt - 1, dn_ref[:, cs], 0.0)
                sg = _sigmoid(g)
                dg = dav * v * (sg * (1.0 + g * (1.0 - sg)))
                dv = dav * (g * sg)
                bufdg[r0:r0 + rows, cs] = dg
                bufdv[r0:r0 + rows, cs] = dv
                if inside:
                    accg[taps], accv[taps] = accg[taps] + _fold8(dg), accv[taps] + _fold8(dv)
                    for k in range(taps):
                        accg[k], accv[k] = accg[k] + _fold8(dg * xg[k]), accv[k] + _fold8(dv * xv[k])
            gbg_ref[:, cs] += jnp.sum(accg[taps], axis=0, keepdims=True)
            gbv_ref[:, cs] += jnp.sum(accv[taps], axis=0, keepdims=True)
            for k in range(taps):
                gwg_ref[k:k + 1, cs] += jnp.sum(accg[k], axis=0, keepdims=True)
                gwv_ref[k:k + 1, cs] += jnp.sum(accv[k], axis=0, keepdims=True)
            for r0 in range(0, tm, CONV_RB):
                du_ref[0, r0:r0 + CONV_RB, cs] = _taps_bwd(bufdg, wg, taps, r0, CONV_RB, cs).astype(BF16)
                du_ref[1, r0:r0 + CONV_RB, cs] = _taps_bwd(bufdv, wv, taps, r0, CONV_RB, cs).astype(BF16)
            part = (_dot_nt(wup_ref[:, cs], du_ref[0, :, cs])
                    + _dot_nt(wup_ref[:, slice(wd + cs.start, wd + cs.stop)], du_ref[1, :, cs]))
            dh = part if dh is None else dh + part
        dh_ref[...] = dh.T

    gt, gbf, gaf = _halo_specs(tm, wd, t, lead=0)
    vt, vbf, vaf = _halo_specs(tm, wd, t, lead=1)
    dt_, _, dn = _halo_specs(tm, wd, t)
    par = lambda rows, off: pl.BlockSpec((rows, wd), functools.partial(lambda c, i, off: (0, c + off), off=off))
    return pl.pallas_call(
        kern, name="ffn_act_bwd", grid=(1, nt),
        in_specs=[gt, gbf, gaf, vt, vbf, vaf, dt_, dn, par(taps, 0), par(taps, 1), par(1, 0), par(1, 1),
                  pl.BlockSpec(w_up.shape, lambda c, i: (0, 0))],
        out_specs=[pl.BlockSpec((2, tm, wd), lambda c, i: (0, i, 0)), par(taps, 0), par(taps, 0), par(1, 0), par(1, 0),
                   pl.BlockSpec((tm, D_MODEL), lambda c, i: (i, 0))],
        out_shape=[jax.ShapeDtypeStruct((2, t, D_FF), BF16)] + [jax.ShapeDtypeStruct((taps, D_FF), F32)] * 2
        + [jax.ShapeDtypeStruct((1, D_FF), F32)] * 2 + [jax.ShapeDtypeStruct((t, D_MODEL), F32)],
        scratch_shapes=[pltpu.VMEM((tm + 2 * HALO, wd), F32)] * 2 + [pltpu.VMEM((tm + HALO, wd), F32)] * 2,
        compiler_params=_cparams(("parallel", "arbitrary")))(u, u, u, u, u, u, da, da, w, w, b, b, w_up)


def _softplus(x):
    e = jnp.exp(-jnp.abs(x))
    return jnp.maximum(x, 0.0) + jnp.where(e < 1e-4, e - 0.5 * e * e, jnp.log(1.0 + e))


def _tri(lower):
    r = lax.broadcasted_iota(jnp.int32, (CHUNK, CHUNK), 0)
    c = lax.broadcasted_iota(jnp.int32, (CHUNK, CHUNK), 1)
    return (r >= c) if lower else (r <= c)


def _cum(mat_bool, x):
    return jnp.dot(mat_bool.astype(F32), x, precision=lax.Precision.HIGHEST, preferred_element_type=F32)


def _pair_sel(lane_lo, tile, h0):
    return jnp.where(lane_lo, tile[:, h0:h0 + 1], tile[:, h0 + 1:h0 + 2])


def _ssd_fwd(xbc_act, proj, dt_bias_p, a_log_p, dskip_t, shards):
    t = xbc_act.shape[0]
    nch = t // CHUNK
    ns = len(shards)

    def kern(xa_ref, dtr_ref, bias_ref, alog_ref, dsk_ref, *rest):
        y_ref, dt_ref, hs_ref = rest[ns:ns + 3]
        hst = rest[2 * ns + 3]
        if ns:
            start, forward, finish = _gather_steps(shards, rest[:ns], rest[ns + 3:2 * ns + 3], rest[2 * ns + 4:])
            pl.when(pl.program_id(0) == 0)(start)
            pl.when(pl.program_id(0) == (3 * nch) // 4)(forward)
            pl.when(pl.program_id(0) == nch - 1)(finish)

        @pl.when(pl.program_id(0) == 0)
        def _():
            hst[...] = jnp.zeros(hst.shape, F32)
        dt = _softplus(dtr_ref[...] + bias_ref[...])
        dt_ref[...] = dt
        acum = _cum(_tri(True), dt * (-jnp.exp(alog_ref[...])))
        acum_t = acum.T
        ea = jnp.exp(acum)
        a_last = acum[CHUNK - 1:CHUNK, :]
        dend = jnp.exp(a_last - acum)
        ea_last = jnp.exp(a_last)
        causal = _tri(True)
        lane_lo = lax.broadcasted_iota(jnp.int32, (CHUNK, LANE), 1) < HEAD_DIM
        row_lo = lax.broadcasted_iota(jnp.int32, (CHUNK, LANE), 0) < HEAD_DIM
        for g in range(2):
            bg = xa_ref[:, SSM_INNER + g * SSM_STATE:SSM_INNER + (g + 1) * SSM_STATE].astype(BF16)
            cg = xa_ref[:, SSM_INNER + BC_DIM + g * SSM_STATE:SSM_INNER + BC_DIM + (g + 1) * SSM_STATE].astype(BF16)
            cb = _dot_nt(cg, bg)
            for j in range(4 * g, 4 * g + 4):
                h0 = 2 * j
                cols = slice(j * LANE, (j + 1) * LANE)
                xp = xa_ref[:, cols]
                xdt = xp * _pair_sel(lane_lo, dt, h0)
                ydiag = None
                for hh, sel in ((h0, lane_lo), (h0 + 1, ~lane_lo)):
                    seg = acum[:, hh:hh + 1] - acum_t[hh:hh + 1, :]
                    mm = (cb * jnp.where(causal, jnp.exp(jnp.minimum(seg, 0.0)), 0.0)).astype(BF16)
                    d = _dot(mm, jnp.where(sel, xdt, 0.0).astype(BF16))
                    ydiag = d if ydiag is None else ydiag + d
                hp = hst[cols, :]
                hs_ref[cols, :] = hp
                yoff = _dot_nt(cg, hp.astype(BF16)) * _pair_sel(lane_lo, ea, h0)
                y_ref[:, cols] = ydiag + yoff + dsk_ref[:, cols] * xp
                xw = (xdt * _pair_sel(lane_lo, dend, h0)).astype(BF16)
                rowf = jnp.where(row_lo, ea_last[:, h0:h0 + 1], ea_last[:, h0 + 1:h0 + 2])
                hst[cols, :] = hp * rowf + _dot_tn(xw, bg)

    return pl.pallas_call(
        kern, name="ssd_fwd", grid=(nch,),
        in_specs=[pl.BlockSpec((CHUNK, XBC_DIM), lambda c: (c, 0)), pl.BlockSpec((CHUNK, LANE), lambda c: (c, OFF_DT // LANE)),
                  pl.BlockSpec((1, LANE), lambda c: (0, 0)), pl.BlockSpec((1, LANE), lambda c: (0, 0)),
                  pl.BlockSpec((1, SSM_INNER), lambda c: (0, 0))] + [ANY_SPEC] * ns,
        out_specs=[pl.BlockSpec((CHUNK, SSM_INNER), lambda c: (c, 0)), pl.BlockSpec((CHUNK, LANE), lambda c: (c, 0)),
                   pl.BlockSpec((None, SSM_INNER, SSM_STATE), lambda c: (c, 0, 0))] + [ANY_SPEC] * ns,
        out_shape=[jax.ShapeDtypeStruct((t, SSM_INNER), F32), jax.ShapeDtypeStruct((t, LANE), F32),
                   jax.ShapeDtypeStruct((nch, SSM_INNER, SSM_STATE), F32)] + _gather_out_shapes(shards),
        scratch_shapes=[pltpu.VMEM((SSM_INNER, SSM_STATE), F32)] + (_gather_sems(ns) if ns else []),
        compiler_params=_cparams(("arbitrary",)))(xbc_act, proj, dt_bias_p, a_log_p, dskip_t, *shards)


def _ssd_bwd(xbc_act, proj, dt_sp, hstates, dy, dt_bias_p, a_log_p, dskip_t):
    t = xbc_act.shape[0]
    nch = t // CHUNK

    pair = jnp.arange(SSM_HEADS // 2)[:, None, None]
    psel = (jnp.arange(LANE)[None, None, :] == 2 * pair + (jnp.arange(LANE) // HEAD_DIM)[None, :, None]).astype(BF16)

    def kern(xa_ref, dtr_ref, dt_ref, hs_ref, dy_ref, bias_ref, alog_ref, dsk_ref, psel_ref,
             dact_ref, ddtr_ref, da_ref, dbias_ref, ddsk_ref, dh):
        @pl.when(pl.program_id(0) == 0)
        def _():
            dh[...] = jnp.zeros(dh.shape, F32)
            for r in (da_ref, dbias_ref, ddsk_ref):
                r[...] = jnp.zeros(r.shape, F32)
        dt = dt_ref[...]
        a_neg = -jnp.exp(alog_ref[...])
        acum = _cum(_tri(True), dt * a_neg)
        acum_t = acum.T
        ea = jnp.exp(acum)
        a_last = acum[CHUNK - 1:CHUNK, :]
        dend = jnp.exp(a_last - acum)
        ea_last = jnp.exp(a_last)
        causal = _tri(True)
        lane = lax.broadcasted_iota(jnp.int32, (CHUNK, LANE), 1)
        rowi = lax.broadcasted_iota(jnp.int32, (CHUNK, LANE), 0)
        lane_lo, row_lo, last_row = lane < HEAD_DIM, rowi < HEAD_DIM, rowi == CHUNK - 1
        d_dt = jnp.zeros((CHUNK, LANE), F32)
        d_acum = jnp.zeros((CHUNK, LANE), F32)
        for g in range(2):
            bcols = slice(SSM_INNER + g * SSM_STATE, SSM_INNER + (g + 1) * SSM_STATE)
            ccols = slice(SSM_INNER + BC_DIM + g * SSM_STATE, SSM_INNER + BC_DIM + (g + 1) * SSM_STATE)
            bg, cg = xa_ref[:, bcols].astype(BF16), xa_ref[:, ccols].astype(BF16)
            cb = _dot_nt(cg, bg)
            dg_sum = jnp.zeros((CHUNK, CHUNK), F32)
            dcg = jnp.zeros((CHUNK, SSM_STATE), F32)
            dbg = jnp.zeros((CHUNK, SSM_STATE), F32)
            for j in range(4 * g, 4 * g + 4):
                h0 = 2 * j
                cols = slice(j * LANE, (j + 1) * LANE)
                xp, dyp = xa_ref[:, cols], dy_ref[:, cols]
                dtsel = _pair_sel(lane_lo, dt, h0)
                xdt = xp * dtsel
                xdt_b = xdt.astype(BF16)
                hp, dhp = hs_ref[cols, :], dh[cols, :]
                hp_b, dhp_b = hp.astype(BF16), dhp.astype(BF16)
                easel, dendsel = _pair_sel(lane_lo, ea, h0), _pair_sel(lane_lo, dend, h0)
                dx, ydiag = None, None
                for hh, sel in ((h0, lane_lo), (h0 + 1, ~lane_lo)):
                    dyh = jnp.where(sel, dyp, 0.0).astype(BF16)
                    seg = acum[:, hh:hh + 1] - acum_t[hh:hh + 1, :]
                    dec = jnp.where(causal, jnp.exp(jnp.minimum(seg, 0.0)), 0.0)
                    mm_b = (cb * dec).astype(BF16)
                    dg_sum = dg_sum + dec * _dot_nt(dyh, xdt_b)
                    d = _dot_tn(mm_b, dyh)
                    y = _dot(mm_b, jnp.where(sel, xdt, 0.0).astype(BF16))
                    dx = d if dx is None else dx + d
                    ydiag = y if ydiag is None else ydiag + y
                g2 = _dot_nt(bg, dhp_b)
                tprod = xdt * g2 * dendsel
                yoff = _dot_nt(cg, hp_b) * easel
                yc = dyp.astype(BF16).astype(F32) * ydiag + dyp * yoff - (xdt_b.astype(F32) * dx + tprod)
                dx = dx + g2 * dendsel
                psel = psel_ref[j]
                t_lo = jnp.sum(jnp.where(lane_lo, tprod, 0.0), keepdims=True).reshape(1, 1)
                t_hi = jnp.sum(tprod, keepdims=True).reshape(1, 1) - t_lo
                hh_prod = dhp * hp
                s_lo = jnp.sum(jnp.where(row_lo, hh_prod, 0.0), keepdims=True).reshape(1, 1)
                s_hi = jnp.sum(hh_prod, keepdims=True).reshape(1, 1) - s_lo
                end_lo = ea_last[:, h0:h0 + 1] * s_lo + t_lo
                end_hi = ea_last[:, h0 + 1:h0 + 2] * s_hi + t_hi
                ends = jnp.where(lane == h0, end_lo, jnp.where(lane == h0 + 1, end_hi, 0.0))
                d_acum = d_acum + _dot_split(yc, psel) + jnp.where(last_row, ends, 0.0)
                dye = (dyp * easel).astype(BF16)
                dcg = dcg + _dot(dye, hp_b)
                dbg = dbg + _dot((xdt * dendsel).astype(BF16), dhp_b)
                rowf = jnp.where(row_lo, ea_last[:, h0:h0 + 1], ea_last[:, h0 + 1:h0 + 2])
                dh[cols, :] = dhp * rowf + _dot_tn(dye, cg)
                dact_ref[:, cols] = dx * dtsel + dsk_ref[:, cols] * dyp
                d_dt = d_dt + _dot_split(dx * xp, psel)
                ddsk_ref[:, cols] += jnp.sum(dyp * xp, axis=0, keepdims=True)
            dg_b = dg_sum.astype(BF16)
            dact_ref[:, ccols] = dcg + _dot(dg_b, bg)
            dact_ref[:, bcols] = dbg + _dot_tn(dg_b, cg)
        d_adt = _cum(_tri(False), d_acum)
        d_dt = d_dt + d_adt * a_neg
        da_ref[...] += jnp.sum(d_adt * dt, axis=0, keepdims=True)
        d_raw = jnp.where(lane < SSM_HEADS, d_dt * _sigmoid(dtr_ref[...] + bias_ref[...]), 0.0)
        ddtr_ref[...] = d_raw.astype(BF16)
        dbias_ref[...] += jnp.sum(d_raw, axis=0, keepdims=True)

    rev = lambda c: (nch - 1 - c, 0)
    return pl.pallas_call(
        kern, name="ssd_bwd", grid=(nch,),
        in_specs=[pl.BlockSpec((CHUNK, XBC_DIM), rev), pl.BlockSpec((CHUNK, LANE), lambda c: (nch - 1 - c, OFF_DT // LANE)),
                  pl.BlockSpec((CHUNK, LANE), rev), pl.BlockSpec((None, SSM_INNER, SSM_STATE), lambda c: (nch - 1 - c, 0, 0)),
                  pl.BlockSpec((CHUNK, SSM_INNER), rev),
                  pl.BlockSpec((1, LANE), lambda c: (0, 0)), pl.BlockSpec((1, LANE), lambda c: (0, 0)),
                  pl.BlockSpec((1, SSM_INNER), lambda c: (0, 0)), pl.BlockSpec(psel.shape, lambda c: (0, 0, 0))],
        out_specs=[pl.BlockSpec((CHUNK, XBC_DIM), rev), pl.BlockSpec((CHUNK, LANE), rev),
                   pl.BlockSpec((1, LANE), lambda c: (0, 0)), pl.BlockSpec((1, LANE), lambda c: (0, 0)),
                   pl.BlockSpec((1, SSM_INNER), lambda c: (0, 0))],
        out_shape=[jax.ShapeDtypeStruct((t, XBC_DIM), F32), jax.ShapeDtypeStruct((t, LANE), BF16),
                   jax.ShapeDtypeStruct((1, LANE), F32), jax.ShapeDtypeStruct((1, LANE), F32),
                   jax.ShapeDtypeStruct((1, SSM_INNER), F32)],
        scratch_shapes=[pltpu.VMEM((SSM_INNER, SSM_STATE), F32)],
        compiler_params=_cparams(("arbitrary",)))(xbc_act, proj, dt_sp, hstates, dy, dt_bias_p, a_log_p, dskip_t, psel)


def _ssm_post_fwd(y, proj, g):
    def body(y_ref, z_ref, g_ref, o_ref):
        z = z_ref[...]
        yz = y_ref[...] * (z * _sigmoid(z))
        r = lax.rsqrt(jnp.mean(yz * yz, axis=-1, keepdims=True) + EPS)
        o_ref[...] = (yz * r * g_ref[...]).astype(BF16)
    return _rows("ssm_post_fwd", body, [("t", y), ("tc", proj, SSM_INNER, OFF_Z // SSM_INNER), ("p", g)],
                 [(SSM_INNER, BF16)])[0]


def _ssm_post_bwd(dmix, y, proj, g):
    def body(do_ref, y_ref, z_ref, g_ref, dy_ref, dz_ref, dg_ref):
        z, yv, dout = z_ref[...], y_ref[...], do_ref[...]
        sg = _sigmoid(z)
        gz = z * sg
        yz = yv * gz
        r = lax.rsqrt(jnp.mean(yz * yz, axis=-1, keepdims=True) + EPS)
        gd = dout * g_ref[...]
        dyz = r * gd - yz * (r * r * r * jnp.mean(yz * gd, axis=-1, keepdims=True))
        dy_ref[...] = dyz * gz
        dz_ref[...] = (dyz * yv * (sg * (1.0 + z * (1.0 - sg)))).astype(BF16)
        dg_ref[...] += jnp.sum(dout * yz * r, axis=0, keepdims=True)
    return _rows("ssm_post_bwd", body,
                 [("tc", dmix, SSM_INNER, 0), ("t", y), ("tc", proj, SSM_INNER, OFF_Z // SSM_INNER), ("p", g)],
                 [(SSM_INNER, F32), (SSM_INNER, BF16)], accs=[(1, SSM_INNER)])


def _ple_loss(gl, pp, x2, tgt):
    d = x2.shape[1]

    def body(gl_ref, pp_ref, x_ref, t_ref, dy_ref, dgl_ref, dpp_ref, sq_ref):
        s = _sigmoid(gl_ref[...])
        ppv = pp_ref[...]
        diff = x_ref[...] + s * ppv - t_ref[...]
        dy = diff * (1.0 / d)
        dy_ref[...] = dy
        dgl_ref[...] = (dy * ppv * s * (1.0 - s)).astype(BF16)
        dpp_ref[...] = (dy * s).astype(BF16)
        sq_ref[...] += jnp.sum(diff * diff, axis=0, keepdims=True)
    return _rows("ple_loss", body, [("t", gl), ("t", pp), ("t", x2), ("t", tgt)], [(d, F32), (d, BF16), (d, BF16)],
                 accs=[(1, d)])


def _pad_lanes(v, width=LANE):
    return jnp.pad(v, ((0, 0), (0, width - v.shape[1])))


def _local_step(x, p, tgt, wts, late_shards=(), join_late=None, reduce_early=None, reduce_late=None):
    g_attn, g_ssm, g_ffn, g_ple = wts["attn_norm_g"], wts["ssm_norm_g"], wts["ffn_norm_g"], wts["ple_norm_g"]
    w_in_p = wts["w_in_p"]
    gq_t = jnp.tile(wts["q_norm_g"], (1, ATTN_DIM // HEAD_DIM))
    gk_t = jnp.tile(wts["k_norm_g"], (1, KV_DIM // HEAD_DIM))
    dt_bias_p, a_log_p = _pad_lanes(wts["dt_bias"]), _pad_lanes(wts["a_log"])
    dskip_t = jnp.repeat(wts["d_skip"], HEAD_DIM, axis=1)

    h1, proj = _norm_mm("in_proj", x, g_attn, w_in_p)
    q_hm, kv_hm = _qknorm_fwd2(proj, gq_t, gk_t)
    pats = [_attn_fwd2(q_hm, kv_hm, d) for d in DILATIONS]
    attn_out, lse = _attn_merge2([o for o, _ in pats], [l for _, l in pats])
    xbc_act = _ssm_conv_fwd(proj, wts["ssm_conv_w"], wts["ssm_conv_b"])
    y_ssd, dt_sp, hstates, *gathered = _ssd_fwd(xbc_act, proj, dt_bias_p, a_log_p, dskip_t, list(late_shards))
    if join_late is not None:
        wts = {**wts, **join_late(gathered)}
    w_out_s, w_out_a = wts["w_out_ssm"], wts["w_out_attn"]
    w_up, w_down, w_gate, w_proj = wts["w_up"], wts["w_down"], wts["w_ple_gate"], wts["w_ple_proj"]
    ssm_out = _ssm_post_fwd(y_ssd, proj, g_ssm)
    x1 = _mm_nn("out_proj", [(ssm_out, w_out_s), (attn_out, w_out_a)], F32, res=x, tm=1024)
    h2, u = _norm_mm("ffn_up", x1, g_ffn, w_up, tm=1024, tn=1408, halves=True)
    a, x2 = _ffn_act_down(u, wts["ffn_conv_w"], wts["ffn_conv_b"], w_down, x1)
    h3, gl = _norm_mm("ple_gate", x2, g_ple, w_gate, tm=1024)
    pb = p.astype(BF16)
    pp = _mm_nn("ple_proj", [(pb, w_proj)], F32, tm=2048)
    dy, dgl, dpp, sq = _ple_loss(gl, pp, x2, tgt)

    grads = {}
    grads["w_ple_proj"] = _mm_tn("g_ple_proj", pb, dpp, tn=PLE_DIM, chip_cols=True)
    grads["w_ple_gate"] = _mm_tn("g_ple_gate", h3, dgl)
    dx2, dx2b, grads["ple_norm_g"] = _mm_nt_rms_bwd("d_h3", dgl, w_gate, x2, g_ple, dy)
    da = _mm_nt("d_ffn_act", [(dx2b, w_down, 0)], F32, tm=1024, tn=1408)
    grads["w_down"] = _mm_tn("g_ffn_down", a, dx2b, tm=1408)
    du, gwg, gwv, gbg, gbv, dh2 = _ffn_act_bwd(u, wts["ffn_conv_w"], wts["ffn_conv_b"], da, w_up)
    grads["ffn_conv_w"] = jnp.concatenate([gwg, gwv], axis=1)
    grads["ffn_conv_b"] = jnp.concatenate([gbg, gbv], axis=1)
    grads["w_up"] = _mm_tn("g_ffn_up", h2, du, tn=1408, chip_cols=True)
    dx1, dx1b, grads["ffn_norm_g"] = _rms_bwd("rms_ffn_bwd", dh2, x1, g_ffn, dx2)
    dmix = _mm_nt("d_mix", [(dx1b, jnp.concatenate([w_out_s, w_out_a], axis=0), 0)], F32, tm=1024)
    grads["w_out"] = jnp.concatenate([_mm_tn("g_out_attn", attn_out, dx1b), _mm_tn("g_out_ssm", ssm_out, dx1b)], axis=0)
    dy_ssd, dz, grads["ssm_norm_g"] = _ssm_post_bwd(dmix, y_ssd, proj, g_ssm)
    dact, ddtr, d_a, d_bias, d_dsk = _ssd_bwd(xbc_act, proj, dt_sp, hstates, dy_ssd, dt_bias_p, a_log_p, dskip_t)
    grads["dt_bias"] = d_bias[:, :SSM_HEADS]
    grads["a_log"] = d_a[:, :SSM_HEADS] * (-jnp.exp(wts["a_log"]))
    grads["d_skip"] = jnp.sum(d_dsk.reshape(SSM_HEADS, HEAD_DIM), axis=1)[None, :]
    chip_sums = reduce_early(grads) if reduce_early is not None else []
    dxbc, grads["ssm_conv_w"], grads["ssm_conv_b"], *scattered = _ssm_conv_bwd(proj, wts["ssm_conv_w"], wts["ssm_conv_b"], dact,
                                                                                chip_sums)
    do_hm, dsum = _attn_bwd_prep2(dmix, attn_out)
    dqs = [_attn_dq2(q_hm, kv_hm, do_hm, lse, dsum, d) for d in DILATIONS]
    dkvs = [_attn_dkv2(q_hm, kv_hm, do_hm, lse, dsum, d) for d in DILATIONS]
    dq, dk, dv, dgq, dgk = _qknorm_bwd2(proj, gq_t, gk_t, dqs, dkvs)
    grads["q_norm_g"] = jnp.sum(dgq.reshape(ATTN_DIM // HEAD_DIM, HEAD_DIM), axis=0)[None, :]
    grads["k_norm_g"] = jnp.sum(dgk.reshape(KV_DIM // HEAD_DIM, HEAD_DIM), axis=0)[None, :]
    dproj = jnp.concatenate([dxbc, dq, dz, dk, dv, ddtr], axis=1)
    grads["w_in_p"] = _mm_tn("g_in_proj", h1, dproj, tm=512)
    late_sums = reduce_late(grads) if reduce_late is not None else []
    grad_x, _, grads["attn_norm_g"], *late_scattered = _mm_nt_rms_bwd("d_h1", dproj, w_in_p, x, g_attn, dx1, late_sums)
    return sq, grad_x, grads, (chip_sums, scattered), (late_sums, late_scattered)


MESH_IDS = pl.DeviceIdType.MESH
N_CHIPS = 4
ANY_SPEC = pl.BlockSpec(memory_space=pl.ANY)
SMALL_ROWS = 96
ALL_SMALL_ROWS = 272


def _place():
    x, y, c = lax.axis_index("x"), lax.axis_index("y"), lax.axis_index("c")
    return x, y, c, [(1 - x, y), (x, 1 - y), (1 - x, 1 - y)]


def _gather_over_chips(arrs):
    n = len(arrs)

    def body(*refs):
        steps = _gather_steps(arrs, refs[:n], refs[n:2 * n], refs[2 * n:2 * n + 4])
        for step in steps:
            step()

    return pl.pallas_call(
        body, name="gather_weights", in_specs=[ANY_SPEC] * n, out_specs=[ANY_SPEC] * n,
        out_shape=_gather_out_shapes(arrs), scratch_shapes=_gather_sems(n))(*arrs)


def _gather_out_shapes(arrs):
    return [jax.ShapeDtypeStruct((N_CHIPS,) + a.shape, a.dtype) for a in arrs]


def _gather_sems(n):
    return [pltpu.SemaphoreType.DMA((3 * n,))] * 4


def _gather_steps(arrs, ins, outs, sems):
    n = len(arrs)
    split = [a.shape[0] % 64 == 0 for a in arrs]
    ici_send, ici_recv, d2d_send, d2d_recv = sems

    def place():
        x, y, c, chips = _place()
        return x, y, c, chips, 2 * x + y

    def part(ref, a, core):
        if not split[a]:
            return ref
        half = arrs[a].shape[0] // 2
        return ref.at[pl.ds(core * half, half)]

    def ici(a, k, slot, where):
        x, y, c, chips, _ = where
        px, py = chips[k]
        return pltpu.make_async_remote_copy(
            src_ref=part(ins[a], a, c), dst_ref=part(outs[a].at[slot], a, c), send_sem=ici_send.at[3 * a + k],
            recv_sem=ici_recv.at[3 * a + k], device_id=(px, py, c), device_id_type=MESH_IDS)

    def d2d(a, k, core, where):
        x, y, c, chips, _ = where
        px, py = chips[k]
        piece = part(outs[a].at[2 * px + py], a, core)
        return pltpu.make_async_remote_copy(src_ref=piece, dst_ref=piece, send_sem=d2d_send.at[3 * a + k],
                                            recv_sem=d2d_recv.at[3 * a + k], device_id=(x, y, 1 - c), device_id_type=MESH_IDS)

    def start():
        w = place()
        for a in range(n):
            for k in range(3):
                ici(a, k, w[4], w).start()

    def forward():
        w = place()
        for a in range(n):
            for k, (px, py) in enumerate(w[3]):
                ici(a, k, 2 * px + py, w).wait_recv()
                if split[a]:
                    d2d(a, k, w[2], w).start()

    def finish():
        w = place()
        for a in range(n):
            for k in range(3):
                if split[a]:
                    d2d(a, k, 1 - w[2], w).wait_recv()
                    d2d(a, k, w[2], w).wait_send()
                ici(a, k, w[4], w).wait_send()

    return start, forward, finish


def _row_tile(rows, cap=256):
    return max(d for d in range(8, cap + 1, 8) if rows % d == 0)


def _swap_halves(name, gs):
    n = len(gs)

    def body(*refs):
        ins, outs, send, recv = refs[:n], refs[n:2 * n], refs[2 * n], refs[2 * n + 1]
        x, y, c, _ = _place()
        cps = []
        for a in range(n):
            half = gs[a].shape[1] // 2
            for q in range(N_CHIPS):
                cps.append(pltpu.make_async_remote_copy(
                    src_ref=ins[a].at[q, pl.ds((1 - c) * half, half)], dst_ref=outs[a].at[q], send_sem=send.at[N_CHIPS * a + q],
                    recv_sem=recv.at[N_CHIPS * a + q], device_id=(x, y, 1 - c), device_id_type=MESH_IDS))
        for cp in cps:
            cp.start()
        for cp in cps:
            cp.wait()

    return pl.pallas_call(
        body, name=name, in_specs=[ANY_SPEC] * n, out_specs=[ANY_SPEC] * n,
        out_shape=[jax.ShapeDtypeStruct((N_CHIPS, g.shape[1] // 2, g.shape[2]), g.dtype) for g in gs],
        scratch_shapes=[pltpu.SemaphoreType.DMA((N_CHIPS * n,))] * 2)(*gs)


def _add_halves(name, g, got, c_idx):
    rows, cols = got.shape[1:]
    tm = _row_tile(rows)
    per = rows // tm

    def kern(c_ref, g_ref, r_ref, o_ref):
        o_ref[...] = (g_ref[...] + r_ref[...]).astype(BF16)

    return pl.pallas_call(
        kern, name=name,
        grid_spec=pltpu.PrefetchScalarGridSpec(
            num_scalar_prefetch=1, grid=(N_CHIPS, per),
            in_specs=[pl.BlockSpec((None, tm, cols), lambda q, i, c_ref: (q, c_ref[0] * per + i, 0)),
                      pl.BlockSpec((None, tm, cols), lambda q, i, c_ref: (q, i, 0))],
            out_specs=pl.BlockSpec((None, tm, cols), lambda q, i, c_ref: (q, i, 0))),
        out_shape=jax.ShapeDtypeStruct((N_CHIPS, rows, cols), BF16),
        compiler_params=_cparams(("parallel", "parallel")))(c_idx, g, got)


def _scatter_over_chips(ss):
    n = len(ss)

    def body(*refs):
        for step in _scatter_steps(n, refs[:n], refs[n:2 * n], refs[2 * n:2 * n + 2]):
            step()

    return pl.pallas_call(
        body, name="grad_scatter_chips", in_specs=[ANY_SPEC] * n, out_specs=[ANY_SPEC] * n,
        out_shape=[jax.ShapeDtypeStruct(s.shape, s.dtype) for s in ss], scratch_shapes=_scatter_sems(n))(*ss)


def _scatter_sems(n):
    return [pltpu.SemaphoreType.DMA((3 * n,))] * 2


def _scatter_steps(n, ins, outs, sems):
    send, recv = sems

    def copy(a, k, slot, where):
        x, y, c, chips = where
        px, py = chips[k]
        return pltpu.make_async_remote_copy(src_ref=ins[a].at[2 * px + py], dst_ref=outs[a].at[slot], send_sem=send.at[3 * a + k],
                                            recv_sem=recv.at[3 * a + k], device_id=(px, py, c), device_id_type=MESH_IDS)

    def start():
        w = _place()
        for a in range(n):
            for k in range(3):
                copy(a, k, 2 * w[0] + w[1], w).start()

    def finish():
        w = _place()
        for a in range(n):
            for k, (px, py) in enumerate(w[3]):
                copy(a, k, 2 * px + py, w).wait()

    return start, finish


def _sum_chips(name, own, parts, idx):
    rows, cols = parts.shape[1:]
    tm = _row_tile(rows)
    per = rows // tm

    def kern(o_idx, a_ref, b_ref, c_ref, d_ref, o_ref):
        o_ref[...] = ((a_ref[...].astype(F32) + b_ref[...].astype(F32)) + c_ref[...].astype(F32)) + d_ref[...].astype(F32)

    def spec(k):
        return pl.BlockSpec((None, tm, cols), functools.partial(lambda i, o_idx, k: (o_idx[k], i, 0), k=k))

    return pl.pallas_call(
        kern, name=name,
        grid_spec=pltpu.PrefetchScalarGridSpec(
            num_scalar_prefetch=1, grid=(per,), in_specs=[spec(0), spec(1), spec(2), spec(3)],
            out_specs=pl.BlockSpec((None, tm, cols), lambda i, o_idx: (0, o_idx[4] * per + i, 0))),
        out_shape=jax.ShapeDtypeStruct((1, 2 * rows, cols), F32), compiler_params=_cparams(("parallel",)))(idx, own, parts, parts, parts)


def _share_with_sibling(gs):
    n = len(gs)

    def body(*refs):
        ins, send, recv = refs[:n], refs[2 * n], refs[2 * n + 1]
        x, y, c, _ = _place()
        cps = []
        for a in range(n):
            half = gs[a].shape[1] // 2
            mine = pl.ds(c * half, half)
            cps.append(pltpu.make_async_remote_copy(src_ref=ins[a].at[0, mine], dst_ref=refs[n + a].at[0, mine], send_sem=send.at[a],
                                                    recv_sem=recv.at[a], device_id=(x, y, 1 - c), device_id_type=MESH_IDS))
        for cp in cps:
            cp.start()
        for cp in cps:
            cp.wait()

    return pl.pallas_call(
        body, name="grad_share_sibling", in_specs=[ANY_SPEC] * n, out_specs=[ANY_SPEC] * n,
        out_shape=[jax.ShapeDtypeStruct(g.shape, g.dtype) for g in gs], input_output_aliases={a: a for a in range(n)},
        scratch_shapes=[pltpu.SemaphoreType.DMA((n,))] * 2)(*gs)


def _allreduce_small(v):
    def body(v_ref, o_ref, land, send, recv):
        x, y, c, _ = _place()
        me = 4 * x + 2 * y + c
        land[me] = v_ref[...]
        cps = []
        for rel in range(1, 8):
            bx, by, bc = (rel >> 2) & 1, (rel >> 1) & 1, rel & 1
            peer = (1 - x if bx else x, 1 - y if by else y, 1 - c if bc else c)
            cps.append(pltpu.make_async_remote_copy(src_ref=v_ref, dst_ref=land.at[me], send_sem=send.at[rel - 1],
                                                    recv_sem=recv.at[rel - 1], device_id=peer, device_id_type=MESH_IDS))
        for cp in cps:
            cp.start()
        for cp in cps:
            cp.wait()
        acc = land[0]
        for d in range(1, 8):
            acc = acc + land[d]
        o_ref[...] = acc

    vm = pl.BlockSpec(memory_space=pltpu.VMEM)
    return pl.pallas_call(
        body, name="allreduce_small", in_specs=[vm], out_specs=vm, out_shape=jax.ShapeDtypeStruct(v.shape, F32),
        scratch_shapes=[pltpu.VMEM((8,) + v.shape, F32), pltpu.SemaphoreType.DMA((7,)), pltpu.SemaphoreType.DMA((7,))])(v)


def _adamw(name, w, g, m, v):
    _, rows, cols = w.shape
    tm = rows if rows * cols <= 128 * 1024 else _row_tile(rows)
    c1 = 1.0 / (1.0 - ADAM_B1 ** ADAM_STEP)
    c2 = 1.0 / (1.0 - ADAM_B2 ** ADAM_STEP)

    def kern(w_ref, g_ref, m_ref, v_ref, d_ref, mo_ref, vo_ref):
        gv = g_ref[...]
        mn = ADAM_B1 * m_ref[...] + (1.0 - ADAM_B1) * gv
        vn = ADAM_B2 * v_ref[...] + (1.0 - ADAM_B2) * (gv * gv)
        d_ref[...] = -ADAM_LR * ((mn * c1) / (jnp.sqrt(vn * c2) + ADAM_EPS) + ADAM_WD * w_ref[...])
        mo_ref[...] = mn
        vo_ref[...] = vn

    spec = pl.BlockSpec((None, tm, cols), lambda i: (0, i, 0))
    return pl.pallas_call(
        kern, name=name, grid=(rows // tm,), in_specs=[spec] * 4, out_specs=[spec] * 3,
        out_shape=[jax.ShapeDtypeStruct(w.shape, F32)] * 3, compiler_params=_cparams(("parallel",)))(w, g, m, v)


SHARDED = (("w_in", 1), ("w_out", 0), ("w_up", 1), ("w_down", 0), ("w_ple_gate", 0), ("w_ple_proj", 1),
           ("ssm_conv_w", 1), ("ffn_conv_w", 1))
MATRICES = ("w_in", "w_out", "w_up", "w_down", "w_ple_gate", "w_ple_proj")
EARLY_REDUCED = MATRICES[1:]
REPLICATED = ("attn_norm_g", "q_norm_g", "k_norm_g", "ssm_conv_b", "dt_bias", "a_log", "d_skip", "ssm_norm_g",
              "ffn_norm_g", "ffn_conv_b", "ple_norm_g")
WEIGHT_ORDER = ("attn_norm_g", "w_in", "q_norm_g", "k_norm_g", "ssm_conv_w", "ssm_conv_b", "dt_bias", "a_log", "d_skip",
                "ssm_norm_g", "w_out", "ffn_norm_g", "w_up", "ffn_conv_w", "ffn_conv_b", "w_down", "ple_norm_g",
                "w_ple_gate", "w_ple_proj")


def _join_chips(g, axis):
    if axis == 0:
        return g.reshape(g.shape[0] * g.shape[1], g.shape[2])
    return jnp.transpose(g, (1, 0, 2)).reshape(g.shape[1], g.shape[0] * g.shape[2])


def _split_chips(g, axis):
    if axis == 0:
        return g.reshape(N_CHIPS, g.shape[0] // N_CHIPS, g.shape[1])
    r, c = g.shape
    return jnp.transpose(g.reshape(r, N_CHIPS, c // N_CHIPS), (1, 0, 2))


def _pack_small(vals, rows=SMALL_ROWS):
    flat = jnp.concatenate([v.reshape(-1) for v in vals])
    return jnp.pad(flat, (0, rows * LANE - flat.shape[0])).reshape(rows, LANE)


def _unpack_small(packed, like):
    flat, out, off = packed.reshape(-1), [], 0
    for v in like:
        out.append(flat[off:off + v.size].reshape(v.shape))
        off += v.size
    return out


def kernel(x, p, attn_norm_g, w_in, q_norm_g, k_norm_g, ssm_conv_w, ssm_conv_b, dt_bias, a_log, d_skip, ssm_norm_g, w_out, ffn_norm_g, w_up, ffn_conv_w, ffn_conv_b, w_down, ple_norm_g, w_ple_gate, w_ple_proj, loss_target, m_attn_norm_g, m_w_in, m_q_norm_g, m_k_norm_g, m_ssm_conv_w, m_ssm_conv_b, m_dt_bias, m_a_log, m_d_skip, m_ssm_norm_g, m_w_out, m_ffn_norm_g, m_w_up, m_ffn_conv_w, m_ffn_conv_b, m_w_down, m_ple_norm_g, m_w_ple_gate, m_w_ple_proj, v_attn_norm_g, v_w_in, v_q_norm_g, v_k_norm_g, v_ssm_conv_w, v_ssm_conv_b, v_dt_bias, v_a_log, v_d_skip, v_ssm_norm_g, v_w_out, v_ffn_norm_g, v_w_up, v_ffn_conv_w, v_ffn_conv_b, v_w_down, v_ple_norm_g, v_w_ple_gate, v_w_ple_proj):
    given = dict(locals())
    w2 = {n: given[n].reshape(given[n].shape[-2:]) if given[n].ndim == 3 else given[n] for n in WEIGHT_ORDER}

    cx, cy, cc = lax.axis_index("x"), lax.axis_index("y"), lax.axis_index("c")
    chip = 2 * cx + cy
    axis_of = dict(SHARDED)
    shard = lambda n: w2[n].astype(BF16) if n in MATRICES else w2[n]
    join = lambda n, g: _join_chips(lax.dynamic_update_index_in_dim(g, shard(n), chip, 0), axis_of[n])
    first = ("w_in", "ssm_conv_w", "ffn_conv_w")
    full = {n: join(n, g) for n, g in zip(first, _gather_over_chips([shard(n) for n in first]))}
    win = full["w_in"]
    w_in_p = jnp.concatenate([win[:, 2048:3584], win[:, 0:512], win[:, 1024:2048], win[:, 512:768], win[:, 768:1024],
                              win[:, 3584:3600], jnp.zeros((D_MODEL, PROJ_P - IN_PROJ), BF16)], axis=1)
    wts = {n: w2[n] for n in REPLICATED}
    wts.update(w_in_p=w_in_p, ssm_conv_w=full["ssm_conv_w"], ffn_conv_w=full["ffn_conv_w"])

    def join_late(gathered):
        late = {n: join(n, g) for n, g in zip(EARLY_REDUCED, gathered)}
        return dict(w_out_attn=late["w_out"][:ATTN_DIM], w_out_ssm=late["w_out"][ATTN_DIM:], w_up=late["w_up"],
                    w_down=late["w_down"], w_ple_gate=late["w_ple_gate"], w_ple_proj=late["w_ple_proj"])

    core = cc.astype(jnp.int32).reshape(1)
    idx = jnp.stack([chip, 2 * (1 - cx) + cy, 2 * cx + (1 - cy), 2 * (1 - cx) + (1 - cy), cc]).astype(jnp.int32)

    def chip_sums_of(tag, names, gd):
        major = [gd[n] if gd[n].ndim == 3 else _split_chips(gd[n], axis_of[n]) for n in names]
        return [_add_halves("grad_add_halves_" + n, g, got, core) for n, g, got in zip(names, major, _swap_halves(tag, major))]

    def w_in_sums(gd):
        gi = gd["w_in_p"]
        gd["w_in"] = jnp.concatenate([gi[:, OFF_Q:OFF_Q + ATTN_DIM], gi[:, OFF_K:OFF_K + KV_DIM], gi[:, OFF_V:OFF_V + KV_DIM],
                                      gi[:, OFF_Z:OFF_Z + SSM_INNER], gi[:, OFF_XBC:OFF_XBC + XBC_DIM], gi[:, OFF_DT:OFF_DT + SSM_HEADS]],
                                     axis=1)
        return chip_sums_of("grad_swap_halves_late", ("w_in",), gd)

    sq, grad_x, grads, early, late = _local_step(
        x[0], p[0, 0], loss_target[0], wts, [shard(n) for n in EARLY_REDUCED], join_late,
        functools.partial(chip_sums_of, "grad_swap_halves_early", EARLY_REDUCED), w_in_sums)
    sums = dict(zip(EARLY_REDUCED + ("w_in",), list(zip(*early)) + list(zip(*late))))
    halves = [_sum_chips("grad_sum_chips_" + n, *sums[n], idx) for n in MATRICES]
    g_shard = dict(zip(MATRICES, _share_with_sibling(halves)))

    small_names = REPLICATED + ("ssm_conv_w", "ffn_conv_w")
    small_like = [grads[n] for n in small_names] + [jnp.zeros((1,), F32)]
    small = _allreduce_small(_pack_small([grads[n] for n in small_names] + [jnp.sum(sq).reshape(1)], ALL_SMALL_ROWS))
    small_vals = dict(zip(small_names + ("loss",), _unpack_small(small, small_like)))
    loss = (0.5 / D_MODEL) * small_vals["loss"][0]
    for n in ("ssm_conv_w", "ffn_conv_w"):
        cols = w2[n].shape[1]
        g_shard[n] = lax.dynamic_slice_in_dim(small_vals[n], chip * cols, cols, axis=1)[None]

    delta, new_m, new_v = {}, {}, {}
    for n, _ in SHARDED:
        delta[n], new_m[n], new_v[n] = _adamw("adamw_" + n, given[n], g_shard[n], given["m_" + n], given["v_" + n])
    packed = lambda prefix: _pack_small([given[prefix + n] for n in REPLICATED])[None]
    sm = _adamw("adamw_small", packed(""), _pack_small([small_vals[n] for n in REPLICATED])[None], packed("m_"), packed("v_"))
    for n in REPLICATED:
        g_shard[n] = small_vals[n]
    for dst, packed_out in zip((delta, new_m, new_v), sm):
        for n, val in zip(REPLICATED, _unpack_small(packed_out[0], [w2[n] for n in REPLICATED])):
            dst[n] = val

    def shaped(d):
        return [d[n].reshape(given[n].shape) for n in WEIGHT_ORDER]
    return (loss, grad_x[None], *shaped(g_shard), *shaped(delta), *shaped(new_m), *shaped(new_v))
```

```python
import functools

import jax
import jax.numpy as jnp
from jax import lax
from jax.experimental import pallas as pl
from jax.experimental.pallas import tpu as pltpu

F32 = jnp.float32
BF16 = jnp.bfloat16

D_MODEL = 1024
HEAD_DIM = 64
ATTN_DIM = 512
KV_DIM = 256
N_KV = 4
SSM_INNER = 1024
SSM_HEADS = 16
SSM_STATE = 128
BC_DIM = 256
XBC_DIM = SSM_INNER + 2 * BC_DIM
MIX_DIM = ATTN_DIM + SSM_INNER
IN_PROJ = 3600
D_FF = 2816
PLE_DIM = 256
CHUNK = 128
DILATIONS = (1, 4, 16)
EPS = 1e-6
ADAM_LR, ADAM_B1, ADAM_B2, ADAM_EPS, ADAM_WD, ADAM_STEP = 0.001, 0.9, 0.999, 1e-08, 0.01, 10

PROJ_P = 3712
OFF_XBC, OFF_Q, OFF_Z, OFF_K, OFF_V, OFF_DT = 0, 1536, 2048, 3072, 3328, 3584
LANE = 128
VMEM_LIMIT = 48 * 1024 * 1024
NEG = -1e30


def _cparams(sem):
    return pltpu.CompilerParams(dimension_semantics=sem, vmem_limit_bytes=VMEM_LIMIT)


def _sigmoid(x):
    return 1.0 / (1.0 + jnp.exp(-x))


def _dot(a, b):
    return jnp.dot(a, b, preferred_element_type=F32)


def _dot_nt(a, b):
    return lax.dot_general(a, b, (((1,), (1,)), ((), ())), preferred_element_type=F32)


def _dot_tn(a, b):
    return lax.dot_general(a, b, (((0,), (0,)), ((), ())), preferred_element_type=F32)


def _dot_split(x, m):
    hi = x.astype(BF16)
    lo = (x - hi.astype(F32)).astype(BF16)
    return _dot(hi, m) + _dot(lo, m)


def _rows(name, body, ins, outs, accs=(), tm=512):
    t_rows = next(s[1].shape[0] for s in ins if s[0] in ("t", "tc"))
    tm = min(tm, t_rows)
    in_specs, args = [], []
    for s in ins:
        if s[0] == "t":
            in_specs.append(pl.BlockSpec((tm, s[1].shape[1]), lambda i: (i, 0)))
        elif s[0] == "tc":
            in_specs.append(pl.BlockSpec((tm, s[2]), functools.partial(lambda i, c: (i, c), c=s[3])))
        else:
            in_specs.append(pl.BlockSpec(s[1].shape, lambda i: (0, 0)))
        args.append(s[1])
    out_shape = [jax.ShapeDtypeStruct((t_rows, w), dt) for w, dt in outs]
    out_specs = [pl.BlockSpec((tm, w), lambda i: (i, 0)) for w, _ in outs]
    out_shape += [jax.ShapeDtypeStruct(a, F32) for a in accs]
    out_specs += [pl.BlockSpec(a, lambda i: (0, 0)) for a in accs]
    n_acc = len(accs)

    def kern(*refs):
        if n_acc:
            @pl.when(pl.program_id(0) == 0)
            def _():
                for r in refs[len(refs) - n_acc:]:
                    r[...] = jnp.zeros(r.shape, F32)
        body(*refs)

    return pl.pallas_call(
        kern, name=name, grid=(t_rows // tm,), in_specs=in_specs, out_specs=out_specs, out_shape=out_shape,
        compiler_params=_cparams(("arbitrary",) if n_acc else ("parallel",)))(*args)


NCHUNK = 512


def _col_chunks(n):
    return [(c, min(NCHUNK, n - c)) for c in range(0, n, NCHUNK)]


def _mm_nt(name, pairs, out_dtype, tm=512, tn=None):
    m, n = pairs[0][0].shape[-2], pairs[0][1].shape[0]
    tn = n if tn is None else tn
    tm = min(tm, m)
    np_ = len(pairs)
    in_specs, args = [], []
    for a, w, kb, *lead in pairs:
        if lead:
            in_specs.append(pl.BlockSpec((None, tm, a.shape[2]), functools.partial(lambda j, i, ld: (ld, i, 0), ld=lead[0])))
        else:
            in_specs.append(pl.BlockSpec((tm, a.shape[1]), lambda j, i: (i, 0)))
        in_specs.append(pl.BlockSpec((tn, a.shape[-1]), functools.partial(lambda j, i, kb: (j, kb), kb=kb)))
        args += [a, w]

    def kern(*refs):
        o_ref = refs[-1]
        for c0, cw in _col_chunks(tn):
            acc = None
            for q in range(np_):
                d = _dot_nt(refs[2 * q][...], refs[2 * q + 1][c0:c0 + cw, :])
                acc = d if acc is None else acc + d
            o_ref[:, c0:c0 + cw] = acc.astype(o_ref.dtype)

    return pl.pallas_call(
        kern, name=name, grid=(n // tn, m // tm), in_specs=in_specs,
        out_specs=pl.BlockSpec((tm, tn), lambda j, i: (i, j)),
        out_shape=jax.ShapeDtypeStruct((m, n), out_dtype), compiler_params=_cparams(("parallel", "parallel")))(*args)


def _mm_tn(name, a, b, tm=None, tn=None, tk=1024, chip_cols=False):
    t, m = a.shape
    n = b.shape[-1] * (2 if b.ndim == 3 else 1)
    tm = m if tm is None else tm
    tn = n if tn is None else tn
    tk = min(tk, t)
    if b.ndim == 3:
        per = n // 2 // tn
        b_spec = pl.BlockSpec((None, tk, tn), lambda i, j, k: (j // per, k, j % per))
    else:
        b_spec = pl.BlockSpec((tk, tn), lambda i, j, k: (k, j))
    if chip_cols:
        out_spec = pl.BlockSpec((None, tm, tn), lambda i, j, k: (j, i, 0))
        out_shape = jax.ShapeDtypeStruct((n // tn, m, tn), F32)
    else:
        out_spec = pl.BlockSpec((tm, tn), lambda i, j, k: (i, j))
        out_shape = jax.ShapeDtypeStruct((m, n), F32)

    def kern(a_ref, b_ref, o_ref):
        @pl.when(pl.program_id(2) == 0)
        def _():
            o_ref[...] = jnp.zeros(o_ref.shape, F32)
        for c0, cw in _col_chunks(tn):
            o_ref[:, c0:c0 + cw] += _dot_tn(a_ref[...], b_ref[:, c0:c0 + cw])

    return pl.pallas_call(
        kern, name=name, grid=(m // tm, n // tn, t // tk),
        in_specs=[pl.BlockSpec((tk, tm), lambda i, j, k: (k, i)), b_spec], out_specs=out_spec, out_shape=out_shape,
        compiler_params=_cparams(("parallel", "parallel", "arbitrary")))(a, b)


def _rms_bwd(name, dh, x, g, dres):
    d = x.shape[1]

    def body(dh_ref, x_ref, g_ref, dres_ref, dx_ref, dxb_ref, dg_ref):
        xv, dhv = x_ref[...], dh_ref[...]
        r = lax.rsqrt(jnp.mean(xv * xv, axis=-1, keepdims=True) + EPS)
        gd = dhv * g_ref[...]
        dx = dres_ref[...] + r * gd - xv * (r * r * r * jnp.mean(xv * gd, axis=-1, keepdims=True))
        dx_ref[...] = dx
        dxb_ref[...] = dx.astype(BF16)
        dg_ref[...] += jnp.sum(dhv * xv * r, axis=0, keepdims=True)
    return _rows(name, body, [("t", dh), ("t", x), ("p", g), ("t", dres)], [(d, F32), (d, BF16)], accs=[(1, d)])


def _norm_mm(name, x, g, w, tm=512, tn=None, halves=False):
    m, k = x.shape
    n = w.shape[1]
    tn = n if tn is None else tn
    if halves:
        per = n // 2 // tn
        o_spec = pl.BlockSpec((None, tm, tn), lambda i, j: (j // per, i, j % per))
        o_shape = jax.ShapeDtypeStruct((2, m, n // 2), F32)
    else:
        o_spec = pl.BlockSpec((tm, tn), lambda i, j: (i, j))
        o_shape = jax.ShapeDtypeStruct((m, n), F32)

    def kern(x_ref, g_ref, w_ref, h_ref, o_ref):
        xv = x_ref[...]
        h = (xv * lax.rsqrt(jnp.mean(xv * xv, axis=-1, keepdims=True) + EPS) * g_ref[...]).astype(BF16)
        h_ref[...] = h
        for c0, cw in _col_chunks(tn):
            o_ref[:, c0:c0 + cw] = _dot(h, w_ref[:, c0:c0 + cw])

    return pl.pallas_call(
        kern, name=name, grid=(m // tm, n // tn),
        in_specs=[pl.BlockSpec((tm, k), lambda i, j: (i, 0)), pl.BlockSpec((1, k), lambda i, j: (0, 0)),
                  pl.BlockSpec((k, tn), lambda i, j: (0, j))],
        out_specs=[pl.BlockSpec((tm, k), lambda i, j: (i, 0)), o_spec],
        out_shape=[jax.ShapeDtypeStruct((m, k), BF16), o_shape],
        compiler_params=_cparams(("parallel", "arbitrary")))(x, g, w)


def _ple_head(x2, g, w_gate, pb, w_proj, tgt, tm=512):
    m, d = x2.shape

    def kern(x_ref, g_ref, wg_ref, p_ref, wp_ref, t_ref, h_ref, dy_ref, dgl_ref, dpp_ref, sq_ref):
        @pl.when(pl.program_id(0) == 0)
        def _():
            sq_ref[...] = jnp.zeros(sq_ref.shape, F32)
        xv = x_ref[...]
        h = (xv * lax.rsqrt(jnp.mean(xv * xv, axis=-1, keepdims=True) + EPS) * g_ref[...]).astype(BF16)
        h_ref[...] = h
        pv = p_ref[...]
        for c0, cw in _col_chunks(d):
            cs = slice(c0, c0 + cw)
            s = _sigmoid(_dot(h, wg_ref[:, cs]))
            ppv = _dot(pv, wp_ref[:, cs])
            diff = x_ref[:, cs] + s * ppv - t_ref[:, cs]
            dy = diff * (1.0 / d)
            dy_ref[:, cs] = dy
            dgl_ref[:, cs] = (dy * ppv * s * (1.0 - s)).astype(BF16)
            dpp_ref[:, cs] = (dy * s).astype(BF16)
            sq_ref[:, cs] += jnp.sum(diff * diff, axis=0, keepdims=True)

    row = lambda width: pl.BlockSpec((tm, width), lambda i: (i, 0))
    full = lambda a: pl.BlockSpec(a.shape, lambda i: (0, 0))
    return pl.pallas_call(
        kern, name="ple_head", grid=(m // tm,),
        in_specs=[row(d), full(g), full(w_gate), row(pb.shape[1]), full(w_proj), row(d)],
        out_specs=[row(d), row(d), row(d), row(d), pl.BlockSpec((1, d), lambda i: (0, 0))],
        out_shape=[jax.ShapeDtypeStruct((m, d), BF16), jax.ShapeDtypeStruct((m, d), F32), jax.ShapeDtypeStruct((m, d), BF16),
                   jax.ShapeDtypeStruct((m, d), BF16), jax.ShapeDtypeStruct((1, d), F32)],
        compiler_params=_cparams(("arbitrary",)))(x2, g, w_gate, pb, w_proj, tgt)


def _ssm_out_proj(y, proj, g, w_ssm, attn_out, w_attn, x, tm=1024):
    m, d = y.shape

    def kern(y_ref, z_ref, g_ref, ws_ref, a_ref, wa_ref, x_ref, s_ref, o_ref):
        z = z_ref[...]
        yz = y_ref[...] * (z * _sigmoid(z))
        s = (yz * lax.rsqrt(jnp.mean(yz * yz, axis=-1, keepdims=True) + EPS) * g_ref[...]).astype(BF16)
        s_ref[...] = s
        av = a_ref[...]
        for c0, cw in _col_chunks(d):
            cs = slice(c0, c0 + cw)
            o_ref[:, cs] = x_ref[:, cs] + _dot(s, ws_ref[:, cs]) + _dot(av, wa_ref[:, cs])

    row = lambda width: pl.BlockSpec((tm, width), lambda i: (i, 0))
    full = lambda a: pl.BlockSpec(a.shape, lambda i: (0, 0))
    return pl.pallas_call(
        kern, name="out_proj", grid=(m // tm,),
        in_specs=[row(d), pl.BlockSpec((tm, d), lambda i: (i, OFF_Z // SSM_INNER)), full(g), full(w_ssm), row(attn_out.shape[1]),
                  full(w_attn), row(d)],
        out_specs=[row(d), row(d)],
        out_shape=[jax.ShapeDtypeStruct((m, d), BF16), jax.ShapeDtypeStruct((m, d), F32)],
        compiler_params=_cparams(("parallel",)))(y, proj, g, w_ssm, attn_out, w_attn, x)


def _mm_nt_rms_bwd(name, a, w, x, g, dres, parts=(), tm=512):
    m, k = a.shape
    n = w.shape[0]
    ns, steps = len(parts), m // tm

    def kern(a_ref, w_ref, x_ref, g_ref, dres_ref, *rest):
        dx_ref, dxb_ref, dg_ref = rest[ns:ns + 3]
        dh = rest[2 * ns + 3]
        if ns:
            start, finish = _scatter_steps(ns, rest[:ns], rest[ns + 3:2 * ns + 3], rest[2 * ns + 4:])
            pl.when(pl.program_id(0) == 0)(start)
            pl.when(pl.program_id(0) == steps - 1)(finish)

        @pl.when(pl.program_id(0) == 0)
        def _():
            dg_ref[...] = jnp.zeros(dg_ref.shape, F32)
        av = a_ref[...]
        for c0, cw in _col_chunks(n):
            dh[:, c0:c0 + cw] = _dot_nt(av, w_ref[c0:c0 + cw, :])
        xv, dhv = x_ref[...], dh[...]
        r = lax.rsqrt(jnp.mean(xv * xv, axis=-1, keepdims=True) + EPS)
        gd = dhv * g_ref[...]
        dx = dres_ref[...] + r * gd - xv * (r * r * r * jnp.mean(xv * gd, axis=-1, keepdims=True))
        dx_ref[...] = dx
        dxb_ref[...] = dx.astype(BF16)
        dg_ref[...] += jnp.sum(dhv * xv * r, axis=0, keepdims=True)

    row = lambda width: pl.BlockSpec((tm, width), lambda i: (i, 0))
    return pl.pallas_call(
        kern, name=name, grid=(steps,),
        in_specs=[row(k), pl.BlockSpec((n, k), lambda i: (0, 0)), row(n), pl.BlockSpec((1, n), lambda i: (0, 0)), row(n)]
        + [ANY_SPEC] * ns,
        out_specs=[row(n), row(n), pl.BlockSpec((1, n), lambda i: (0, 0))] + [ANY_SPEC] * ns,
        out_shape=[jax.ShapeDtypeStruct((m, n), F32), jax.ShapeDtypeStruct((m, n), BF16), jax.ShapeDtypeStruct((1, n), F32)]
        + [jax.ShapeDtypeStruct(s.shape, s.dtype) for s in parts],
        scratch_shapes=[pltpu.VMEM((tm, n), F32)] + (_scatter_sems(ns) if ns else []),
        compiler_params=_cparams(("arbitrary",)))(a, w, x, g, dres, *parts)


def _head_mean_matrix(width):
    i = jnp.arange(width) // HEAD_DIM
    return jnp.where(i[:, None] == i[None, :], 1.0 / HEAD_DIM, 0.0).astype(BF16)


ATT_SPAN = 2048
N_QH = 8


def _lane_lo(rows):
    return lax.broadcasted_iota(jnp.int32, (rows, LANE), 1) < HEAD_DIM


def _swap_halves_lanes(x):
    return pltpu.roll(x, HEAD_DIM, axis=1)


def _qknorm_fwd2(proj, gq_t, gk_t, tm=256):
    t = proj.shape[0]
    bq, bk = _head_mean_matrix(ATTN_DIM), _head_mean_matrix(KV_DIM)
    scale = HEAD_DIM ** -0.5

    def kern(q_ref, k_ref, v_ref, gq_ref, gk_ref, bq_ref, bk_ref, qo_ref, kvo_ref):
        q, k, v = q_ref[...], k_ref[...], v_ref[...]
        qn = (q * lax.rsqrt(_dot_split(q * q, bq_ref[...]) + EPS) * gq_ref[...]) * scale
        kn = k * lax.rsqrt(_dot_split(k * k, bk_ref[...]) + EPS) * gk_ref[...]
        lo = _lane_lo(tm)
        for j in range(N_KV):
            blk = qn[:, j * LANE:(j + 1) * LANE]
            qo_ref[2 * j] = jnp.where(lo, blk, 0.0)
            qo_ref[2 * j + 1] = jnp.where(lo, _swap_halves_lanes(blk), 0.0)
        for j in range(2):
            kb, vb = kn[:, j * LANE:(j + 1) * LANE], v[:, j * LANE:(j + 1) * LANE]
            kvo_ref[2 * j] = jnp.where(lo, kb, _swap_halves_lanes(vb))
            kvo_ref[2 * j + 1] = jnp.where(lo, _swap_halves_lanes(kb), vb)

    col = lambda w, idx: pl.BlockSpec((tm, w), functools.partial(lambda i, idx: (i, idx), idx=idx))
    par = lambda a: pl.BlockSpec(a.shape, lambda i: (0, 0))
    return pl.pallas_call(
        kern, name="qknorm_fwd", grid=(t // tm,),
        in_specs=[col(ATTN_DIM, OFF_Q // ATTN_DIM), col(KV_DIM, OFF_K // KV_DIM), col(KV_DIM, OFF_V // KV_DIM),
                  par(gq_t), par(gk_t), par(bq), par(bk)],
        out_specs=[pl.BlockSpec((N_QH, tm, LANE), lambda i: (0, i, 0)), pl.BlockSpec((N_KV, tm, LANE), lambda i: (0, i, 0))],
        out_shape=[jax.ShapeDtypeStruct((N_QH, t, LANE), F32), jax.ShapeDtypeStruct((N_KV, t, LANE), F32)],
        compiler_params=_cparams(("parallel",)))(proj, proj, proj, gq_t, gk_t, bq, bk)


def _qknorm_bwd2(proj, gq_t, gk_t, dqs, dkvs, tm=256):
    t = proj.shape[0]
    bq, bk = _head_mean_matrix(ATTN_DIM), _head_mean_matrix(KV_DIM)
    scale = HEAD_DIM ** -0.5

    def kern(q_ref, k_ref, gq_ref, gk_ref, bq_ref, bk_ref, a1, a2, a3, b1, b2, b3, dq_ref, dk_ref, dv_ref, dgq_ref, dgk_ref):
        @pl.when(pl.program_id(0) == 0)
        def _():
            dgq_ref[...] = jnp.zeros(dgq_ref.shape, F32)
            dgk_ref[...] = jnp.zeros(dgk_ref.shape, F32)
        lo = _lane_lo(tm)
        sq = [a1[h] + a2[h] + a3[h] for h in range(N_QH)]
        skv = [b1[h] + b2[h] + b3[h] for h in range(N_KV)]
        dqn = jnp.concatenate([jnp.where(lo, sq[2 * j], _swap_halves_lanes(sq[2 * j + 1])) for j in range(N_KV)], axis=1) * scale
        dkn = jnp.concatenate([jnp.where(lo, skv[2 * j], _swap_halves_lanes(skv[2 * j + 1])) for j in range(2)], axis=1)
        dv = jnp.concatenate([jnp.where(lo, _swap_halves_lanes(skv[2 * j]), skv[2 * j + 1]) for j in range(2)], axis=1)
        q, k = q_ref[...], k_ref[...]
        rq = lax.rsqrt(_dot_split(q * q, bq_ref[...]) + EPS)
        rk = lax.rsqrt(_dot_split(k * k, bk_ref[...]) + EPS)
        gdq, gdk = dqn * gq_ref[...], dkn * gk_ref[...]
        dq_ref[...] = (rq * gdq - q * (rq * rq * rq * _dot_split(q * gdq, bq_ref[...]))).astype(BF16)
        dk_ref[...] = (rk * gdk - k * (rk * rk * rk * _dot_split(k * gdk, bk_ref[...]))).astype(BF16)
        dv_ref[...] = dv.astype(BF16)
        dgq_ref[...] += jnp.sum(dqn * q * rq, axis=0, keepdims=True)
        dgk_ref[...] += jnp.sum(dkn * k * rk, axis=0, keepdims=True)

    col = lambda w, idx: pl.BlockSpec((tm, w), functools.partial(lambda i, idx: (i, idx), idx=idx))
    par = lambda a: pl.BlockSpec(a.shape, lambda i: (0, 0))
    blk = lambda n: pl.BlockSpec((n, tm, LANE), lambda i: (0, i, 0))
    row = lambda w: pl.BlockSpec((tm, w), lambda i: (i, 0))
    acc = lambda w: pl.BlockSpec((1, w), lambda i: (0, 0))
    return pl.pallas_call(
        kern, name="qknorm_bwd", grid=(t // tm,),
        in_specs=[col(ATTN_DIM, OFF_Q // ATTN_DIM), col(KV_DIM, OFF_K // KV_DIM), par(gq_t), par(gk_t), par(bq), par(bk)]
        + [blk(N_QH)] * 3 + [blk(N_KV)] * 3,
        out_specs=[row(ATTN_DIM), row(KV_DIM), row(KV_DIM), acc(ATTN_DIM), acc(KV_DIM)],
        out_shape=[jax.ShapeDtypeStruct((t, ATTN_DIM), BF16), jax.ShapeDtypeStruct((t, KV_DIM), BF16),
                   jax.ShapeDtypeStruct((t, KV_DIM), BF16), jax.ShapeDtypeStruct((1, ATTN_DIM), F32),
                   jax.ShapeDtypeStruct((1, KV_DIM), F32)],
        compiler_params=_cparams(("arbitrary",)))(proj, proj, gq_t, gk_t, bq, bk, *dqs, *dkvs)


def _att_rows(b, r, dil):
    if dil == 1:
        return pl.ds(b * CHUNK, CHUNK)
    return pl.ds(b * CHUNK * dil + r, CHUNK, stride=dil)


def _for_residues(dil, unit):
    for r in range(dil):
        unit(r, 0)


def _band_qk(first):
    ri = lax.broadcasted_iota(jnp.int32, (CHUNK, 2 * CHUNK), 0)
    cj = lax.broadcasted_iota(jnp.int32, (CHUNK, 2 * CHUNK), 1)
    band = (cj - ri >= 0) & (cj - ri <= CHUNK)
    return band if first is None else band & (jnp.logical_not(first) | (cj >= CHUNK))


def _band_kq(last):
    rj = lax.broadcasted_iota(jnp.int32, (CHUNK, 2 * CHUNK), 0)
    ci = lax.broadcasted_iota(jnp.int32, (CHUNK, 2 * CHUNK), 1)
    band = (ci - rj >= 0) & (ci - rj <= CHUNK)
    return band if last is None else band & (jnp.logical_not(last) | (ci < CHUNK))


def _att_specs(t, dil):
    sub = CHUNK * dil
    nb, last = ATT_SPAN // sub, t // sub - 1
    cur = lambda heads: pl.BlockSpec((heads, ATT_SPAN, LANE), lambda kh, n: (kh, n, 0))
    prev = lambda heads: pl.BlockSpec((heads, sub, LANE), lambda kh, n: (kh, jnp.maximum(n * nb - 1, 0), 0))
    nxt = lambda heads: pl.BlockSpec((heads, sub, LANE), lambda kh, n: (kh, jnp.minimum((n + 1) * nb, last), 0))
    return sub, nb, cur, prev, nxt


def _attn_fwd2(q, kv, dil):
    t = q.shape[1]
    sub, nb, cur, prev, _ = _att_specs(t, dil)

    def kern(q_ref, kvp_ref, kvc_ref, o_ref, lse_ref):
        n = pl.program_id(1)
        lane = lax.broadcasted_iota(jnp.int32, (CHUNK, LANE), 1)
        for b in range(nb):
            mask = _band_qk((n == 0) if b == 0 else None)

            def unit(r, carry, b=b, mask=mask):
                rows = _att_rows(b, r, dil)
                kvp = kvc_ref[_att_rows(b - 1, r, dil), :] if b > 0 else kvp_ref[_att_rows(0, r, dil), :]
                kvcat = jnp.concatenate([kvp, kvc_ref[rows, :]], axis=0).astype(BF16)
                lse_tile = jnp.zeros((CHUNK, LANE), F32)
                for g in range(2):
                    s = jnp.where(mask, _dot_nt(q_ref.at[g][rows, :].astype(BF16), kvcat), NEG)
                    m = jnp.max(s, axis=1, keepdims=True)
                    p = jnp.exp(s - m)
                    l = jnp.sum(p, axis=1, keepdims=True)
                    o_ref.at[g][rows, :] = _dot(p.astype(BF16), kvcat) * (1.0 / l)
                    lse_tile = jnp.where(lane == g, m + jnp.log(l), lse_tile)
                lse_ref[rows, :] = lse_tile
                return carry
            _for_residues(dil, unit)

    return pl.pallas_call(
        kern, name=f"attn_fwd_d{dil}", grid=(N_KV, t // ATT_SPAN), in_specs=[cur(2), prev(None), cur(None)],
        out_specs=[cur(2), cur(None)],
        out_shape=[jax.ShapeDtypeStruct((N_QH, t, LANE), F32), jax.ShapeDtypeStruct((N_KV, t, LANE), F32)],
        compiler_params=_cparams(("parallel", "parallel")))(q, kv, kv)


def _attn_merge2(os_, lses, tm=256):
    t = os_[0].shape[1]

    def kern(o1, o2, o3, l1, l2, l3, out_ref, lse_ref):
        pieces = []
        for kh in range(N_KV):
            a, b, c = l1[kh], l2[kh], l3[kh]
            m = jnp.maximum(jnp.maximum(a, b), c)
            tot = m + jnp.log(jnp.exp(a - m) + jnp.exp(b - m) + jnp.exp(c - m))
            lse_ref[kh] = tot
            wa, wb, wc = jnp.exp(a - tot), jnp.exp(b - tot), jnp.exp(c - tot)
            for g in range(2):
                h = 2 * kh + g
                acc = wa[:, g:g + 1] * o1[h] + wb[:, g:g + 1] * o2[h] + wc[:, g:g + 1] * o3[h]
                pieces.append(acc[:, HEAD_DIM:])
        out_ref[...] = jnp.concatenate(pieces, axis=1).astype(BF16)

    blk = lambda n: pl.BlockSpec((n, tm, LANE), lambda i: (0, i, 0))
    return pl.pallas_call(
        kern, name="attn_merge", grid=(t // tm,), in_specs=[blk(N_QH)] * 3 + [blk(N_KV)] * 3,
        out_specs=[pl.BlockSpec((tm, ATTN_DIM), lambda i: (i, 0)), blk(N_KV)],
        out_shape=[jax.ShapeDtypeStruct((t, ATTN_DIM), BF16), jax.ShapeDtypeStruct((N_KV, t, LANE), F32)],
        compiler_params=_cparams(("parallel",)))(*os_, *lses)


def _attn_bwd_prep2(dmix, attn_out, tm=256):
    t = attn_out.shape[0]

    def kern(do_ref, o_ref, dot_ref, d_ref):
        do = do_ref[...]
        prod = do * o_ref[...].astype(F32)
        lo = _lane_lo(tm)
        lane = lax.broadcasted_iota(jnp.int32, (tm, LANE), 1)
        for kh in range(N_KV):
            blk, pb = do[:, kh * LANE:(kh + 1) * LANE], prod[:, kh * LANE:(kh + 1) * LANE]
            dot_ref[2 * kh] = jnp.where(lo, 0.0, _swap_halves_lanes(blk))
            dot_ref[2 * kh + 1] = jnp.where(lo, 0.0, blk)
            s_lo = jnp.sum(jnp.where(lo, pb, 0.0), axis=1, keepdims=True)
            s_hi = jnp.sum(pb, axis=1, keepdims=True) - s_lo
            d_ref[kh] = jnp.where(lane == 0, s_lo, jnp.where(lane == 1, s_hi, 0.0))

    blk = lambda n: pl.BlockSpec((n, tm, LANE), lambda i: (0, i, 0))
    return pl.pallas_call(
        kern, name="attn_bwd_prep", grid=(t // tm,),
        in_specs=[pl.BlockSpec((tm, ATTN_DIM), lambda i: (i, SSM_INNER // ATTN_DIM)), pl.BlockSpec((tm, ATTN_DIM), lambda i: (i, 0))],
        out_specs=[blk(N_QH), blk(N_KV)],
        out_shape=[jax.ShapeDtypeStruct((N_QH, t, LANE), F32), jax.ShapeDtypeStruct((N_KV, t, LANE), F32)],
        compiler_params=_cparams(("parallel",)))(dmix, attn_out)


def _attn_dq2(q, kv, dot, lse, dsum, dil):
    t = q.shape[1]
    sub, nb, cur, prev, _ = _att_specs(t, dil)

    def kern(q_ref, kvp_ref, kvc_ref, do_ref, lse_ref, d_ref, dq_ref):
        n = pl.program_id(1)
        for b in range(nb):
            mask = _band_qk((n == 0) if b == 0 else None)

            def unit(r, carry, b=b, mask=mask):
                rows = _att_rows(b, r, dil)
                kvp = kvc_ref[_att_rows(b - 1, r, dil), :] if b > 0 else kvp_ref[_att_rows(0, r, dil), :]
                kvcat = jnp.concatenate([kvp, kvc_ref[rows, :]], axis=0).astype(BF16)
                lse_t, d_t = lse_ref[rows, :], d_ref[rows, :]
                for g in range(2):
                    s = jnp.where(mask, _dot_nt(q_ref.at[g][rows, :].astype(BF16), kvcat), NEG)
                    p = jnp.exp(s - lse_t[:, g:g + 1])
                    dp = _dot_nt(do_ref.at[g][rows, :].astype(BF16), kvcat)
                    ds = p * (dp - d_t[:, g:g + 1])
                    dq_ref.at[g][rows, :] = _dot(ds.astype(BF16), kvcat)
                return carry
            _for_residues(dil, unit)

    return pl.pallas_call(
        kern, name=f"attn_dq_d{dil}", grid=(N_KV, t // ATT_SPAN),
        in_specs=[cur(2), prev(None), cur(None), cur(2), cur(None), cur(None)], out_specs=cur(2),
        out_shape=jax.ShapeDtypeStruct((N_QH, t, LANE), F32),
        compiler_params=_cparams(("parallel", "parallel")))(q, kv, kv, dot, lse, dsum)


def _attn_dkv2(q, kv, dot, lse, dsum, dil):
    t = q.shape[1]
    sub, nb, cur, _, nxt = _att_specs(t, dil)
    nsteps = t // ATT_SPAN

    def kern(kv_ref, qc_ref, qn_ref, doc_ref, don_ref, lc_ref, ln_ref, dc_ref, dn_ref, dkv_ref):
        n = pl.program_id(1)
        for b in range(nb):
            inside = b < nb - 1
            mask = _band_kq(None if inside else (n == nsteps - 1))

            def unit(r, carry, b=b, inside=inside, mask=mask):
                rows = _att_rows(b, r, dil)
                nrows = _att_rows(b + 1, r, dil) if inside else _att_rows(0, r, dil)
                kvb = kv_ref[rows, :].astype(BF16)
                follow = lambda cref, nref: (cref if inside else nref)[nrows, :]
                lse_t = jnp.concatenate([lc_ref[rows, :].T, follow(lc_ref, ln_ref).T], axis=1)
                d_t = jnp.concatenate([dc_ref[rows, :].T, follow(dc_ref, dn_ref).T], axis=1)
                acc = jnp.zeros((CHUNK, LANE), F32)
                for g in range(2):
                    qdo = jnp.concatenate([qc_ref.at[g][rows, :], follow(qc_ref.at[g], qn_ref.at[g]),
                                           doc_ref.at[g][rows, :], follow(doc_ref.at[g], don_ref.at[g])], axis=0).astype(BF16)
                    both = _dot_nt(kvb, qdo)
                    pt = jnp.exp(jnp.where(mask, both[:, :2 * CHUNK], NEG) - lse_t[g:g + 1, :])
                    dst = pt * (both[:, 2 * CHUNK:] - d_t[g:g + 1, :])
                    acc = acc + _dot(jnp.concatenate([dst, pt], axis=1).astype(BF16), qdo)
                dkv_ref[rows, :] = acc
                return carry
            _for_residues(dil, unit)

    return pl.pallas_call(
        kern, name=f"attn_dkv_d{dil}", grid=(N_KV, nsteps),
        in_specs=[cur(None), cur(2), nxt(2), cur(2), nxt(2), cur(None), nxt(None), cur(None), nxt(None)], out_specs=cur(None),
        out_shape=jax.ShapeDtypeStruct((N_KV, t, LANE), F32),
        compiler_params=_cparams(("parallel", "parallel")))(kv, q, q, dot, dot, lse, lse, dsum, dsum)


HALO = 8
SSM_CONV_TM, SSM_CONV_W = 512, 512
FFN_CONV_TM, FFN_CONV_W = 256, 1408


def _halo_specs(tm, width, t_rows, col_off=0, lead=None):
    per, last = tm // HALO, t_rows // HALO - 1
    row_maps = (lambda i: i, lambda i: jnp.maximum(i * per - 1, 0), lambda i: jnp.minimum((i + 1) * per, last))
    specs = []
    for rows, rm in zip((tm, HALO, HALO), row_maps):
        if lead is None:
            specs.append(pl.BlockSpec((rows, width), functools.partial(lambda c, i, rm: (rm(i), c + col_off), rm=rm)))
        else:
            specs.append(pl.BlockSpec((None, rows, width), functools.partial(lambda c, i, rm: (lead, rm(i), c + col_off), rm=rm)))
    return specs


def _fill_ext(buf, tile_ref, before_ref, after_ref, i, nt):
    tm = tile_ref.shape[0]
    buf[0:HALO, :] = jnp.where(i > 0, before_ref[...].astype(F32), 0.0)
    buf[HALO:HALO + tm, :] = tile_ref[...].astype(F32)
    if after_ref is not None:
        buf[HALO + tm:, :] = jnp.where(i < nt - 1, after_ref[...].astype(F32), 0.0)


CONV_RB, CONV_CW = 16, 256


def _lane_chunks(width):
    return [slice(c0, min(c0 + CONV_CW, width)) for c0 in range(0, width, CONV_CW)]


def _shifted(buf, taps, r0, rows, cs):
    return [buf[pl.ds(HALO - (taps - 1) + k + r0, rows), cs] for k in range(taps)]


def _taps_fwd(xs, w, b):
    acc = b
    for k, xk in enumerate(xs):
        acc = acc + w[k:k + 1, :] * xk
    return acc


def _taps_bwd(bufd, w, taps, r0, rows, cs):
    acc = None
    for k in range(taps):
        term = w[k:k + 1, :] * bufd[pl.ds(r0 + (taps - 1) - k, rows), cs]
        acc = term if acc is None else acc + term
    return acc


def _fold8(z):
    return z[:HALO] + z[HALO:] if z.shape[0] == 2 * HALO else z


def _silu_grad(pre):
    sg = _sigmoid(pre)
    return sg * (1.0 + pre * (1.0 - sg))


def _ssm_conv_fwd(proj, w, b):
    t = proj.shape[0]
    tm, wd = min(SSM_CONV_TM, t), SSM_CONV_W
    nt, taps = t // tm, w.shape[0]

    def kern(x_ref, xb_ref, w_ref, b_ref, o_ref, buf):
        _fill_ext(buf, x_ref, xb_ref, None, pl.program_id(1), nt)
        for cs in _lane_chunks(wd):
            wv, bv = w_ref[:, cs], b_ref[:, cs]
            for r0 in range(0, tm, CONV_RB):
                pre = _taps_fwd(_shifted(buf, taps, r0, CONV_RB, cs), wv, bv)
                o_ref[r0:r0 + CONV_RB, cs] = pre * _sigmoid(pre)

    tile, before, _ = _halo_specs(tm, wd, t)
    par = lambda rows: pl.BlockSpec((rows, wd), lambda c, i: (0, c))
    return pl.pallas_call(
        kern, name="ssm_conv_fwd", grid=(XBC_DIM // wd, nt), in_specs=[tile, before, par(taps), par(1)],
        out_specs=pl.BlockSpec((tm, wd), lambda c, i: (i, c)), out_shape=jax.ShapeDtypeStruct((t, XBC_DIM), F32),
        scratch_shapes=[pltpu.VMEM((tm + HALO, wd), F32)],
        compiler_params=_cparams(("parallel", "parallel")))(proj, proj, w, b)


def _ssm_conv_bwd(proj, w, b, dact, parts):
    t = proj.shape[0]
    tm, wd = min(SSM_CONV_TM, t), SSM_CONV_W
    nt, taps, ncol, ns = t // tm, w.shape[0], XBC_DIM // SSM_CONV_W, len(parts)

    def kern(x_ref, xb_ref, xa_ref, d_ref, dn_ref, w_ref, b_ref, *rest):
        dx_ref, gw_ref, gb_ref = rest[ns:ns + 3]
        buf, bufd = rest[2 * ns + 3:2 * ns + 5]
        i = pl.program_id(1)
        if ns:
            start, finish = _scatter_steps(ns, rest[:ns], rest[ns + 3:2 * ns + 3], rest[2 * ns + 5:])
            pl.when((pl.program_id(0) == 0) & (i == 0))(start)
            pl.when((pl.program_id(0) == ncol - 1) & (i == nt - 1))(finish)
        _fill_ext(buf, x_ref, xb_ref, xa_ref, i, nt)

        @pl.when(i == 0)
        def _():
            gw_ref[...] = jnp.zeros(gw_ref.shape, F32)
            gb_ref[...] = jnp.zeros(gb_ref.shape, F32)
        for cs in _lane_chunks(wd):
            wv, bv = w_ref[:, cs], b_ref[:, cs]
            acc = [jnp.zeros((HALO, cs.stop - cs.start), F32) for _ in range(taps + 1)]
            for r0 in list(range(0, tm, CONV_RB)) + [tm]:
                inside = r0 < tm
                rows = CONV_RB if inside else HALO
                xs = _shifted(buf, taps, r0, rows, cs)
                d = d_ref[r0:r0 + rows, cs] if inside else jnp.where(i < nt - 1, dn_ref[:, cs], 0.0)
                dpre = d * _silu_grad(_taps_fwd(xs, wv, bv))
                bufd[r0:r0 + rows, cs] = dpre
                if inside:
                    acc[taps] = acc[taps] + _fold8(dpre)
                    for k in range(taps):
                        acc[k] = acc[k] + _fold8(dpre * xs[k])
            gb_ref[:, cs] += jnp.sum(acc[taps], axis=0, keepdims=True)
            for k in range(taps):
                gw_ref[k:k + 1, cs] += jnp.sum(acc[k], axis=0, keepdims=True)
            for r0 in range(0, tm, CONV_RB):
                dx_ref[r0:r0 + CONV_RB, cs] = _taps_bwd(bufd, wv, taps, r0, CONV_RB, cs).astype(BF16)

    xt, xb, xa = _halo_specs(tm, wd, t)
    dt_, _, dn = _halo_specs(tm, wd, t)
    par = lambda rows: pl.BlockSpec((rows, wd), lambda c, i: (0, c))
    return pl.pallas_call(
        kern, name="ssm_conv_bwd", grid=(ncol, nt), in_specs=[xt, xb, xa, dt_, dn, par(taps), par(1)] + [ANY_SPEC] * ns,
        out_specs=[pl.BlockSpec((tm, wd), lambda c, i: (i, c)), par(taps), par(1)] + [ANY_SPEC] * ns,
        out_shape=[jax.ShapeDtypeStruct((t, XBC_DIM), BF16), jax.ShapeDtypeStruct((taps, XBC_DIM), F32),
                   jax.ShapeDtypeStruct((1, XBC_DIM), F32)] + [jax.ShapeDtypeStruct(s.shape, s.dtype) for s in parts],
        scratch_shapes=[pltpu.VMEM((tm + 2 * HALO, wd), F32), pltpu.VMEM((tm + HALO, wd), F32)] + (_scatter_sems(ns) if ns else []),
        compiler_params=_cparams(("arbitrary", "arbitrary")))(proj, proj, proj, dact, dact, w, b, *parts)


def _ffn_act_down(u, w, b, w_down, x1):
    t = u.shape[1]
    tm, wd = min(FFN_CONV_TM, t), D_FF
    nt, taps = t // tm, w.shape[0]

    def kern(g_ref, gb_ref, v_ref, vb_ref, wg_ref, wv_ref, bg_ref, bv_ref, wd_ref, x1_ref, a_ref, x2_ref, bufg, bufv):
        i = pl.program_id(1)
        _fill_ext(bufg, g_ref, gb_ref, None, i, nt)
        _fill_ext(bufv, v_ref, vb_ref, None, i, nt)
        acc = x1_ref[...]
        for cs in _lane_chunks(wd):
            wg, wv, bg, bv = wg_ref[:, cs], wv_ref[:, cs], bg_ref[:, cs], bv_ref[:, cs]
            for r0 in range(0, tm, CONV_RB):
                g = _taps_fwd(_shifted(bufg, taps, r0, CONV_RB, cs), wg, bg)
                v = _taps_fwd(_shifted(bufv, taps, r0, CONV_RB, cs), wv, bv)
                a_ref[r0:r0 + CONV_RB, cs] = (g * _sigmoid(g) * v).astype(BF16)
            acc = acc + _dot(a_ref[:, cs], wd_ref[cs, :])
        x2_ref[...] = acc

    gt, gbf, _ = _halo_specs(tm, wd, t, lead=0)
    vt, vbf, _ = _halo_specs(tm, wd, t, lead=1)
    par = lambda rows, off: pl.BlockSpec((rows, wd), functools.partial(lambda c, i, off: (0, c + off), off=off))
    row = lambda width: pl.BlockSpec((tm, width), lambda c, i: (i, 0))
    return pl.pallas_call(
        kern, name="ffn_act_down", grid=(1, nt),
        in_specs=[gt, gbf, vt, vbf, par(taps, 0), par(taps, 1), par(1, 0), par(1, 1),
                  pl.BlockSpec(w_down.shape, lambda c, i: (0, 0)), row(D_MODEL)],
        out_specs=[row(wd), row(D_MODEL)],
        out_shape=[jax.ShapeDtypeStruct((t, D_FF), BF16), jax.ShapeDtypeStruct((t, D_MODEL), F32)],
        scratch_shapes=[pltpu.VMEM((tm + HALO, wd), F32)] * 2,
        compiler_params=_cparams(("parallel", "parallel")))(u, u, u, u, w, w, b, b, w_down, x1)


def _ffn_act_bwd(u, w, b, da):
    t = u.shape[1]
    tm, wd = min(FFN_CONV_TM, t), FFN_CONV_W
    nt, taps, nc = t // tm, w.shape[0], D_FF // FFN_CONV_W

    def kern(g_ref, gb_ref, ga_ref, v_ref, vb_ref, va_ref, d_ref, dn_ref, wg_ref, wv_ref, bg_ref, bv_ref,
             du_ref, gwg_ref, gwv_ref, gbg_ref, gbv_ref, bufg, bufv, bufdg, bufdv):
        i = pl.program_id(1)
        _fill_ext(bufg, g_ref, gb_ref, ga_ref, i, nt)
        _fill_ext(bufv, v_ref, vb_ref, va_ref, i, nt)

        @pl.when(i == 0)
        def _():
            for r in (gwg_ref, gwv_ref, gbg_ref, gbv_ref):
                r[...] = jnp.zeros(r.shape, F32)
        for cs in _lane_chunks(wd):
            wg, wv, bg, bv = wg_ref[:, cs], wv_ref[:, cs], bg_ref[:, cs], bv_ref[:, cs]
            zero = jnp.zeros((HALO, cs.stop - cs.start), F32)
            accg, accv = [zero] * (taps + 1), [zero] * (taps + 1)
            for r0 in list(range(0, tm, CONV_RB)) + [tm]:
                inside = r0 < tm
                rows = CONV_RB if inside else HALO
                xg, xv = _shifted(bufg, taps, r0, rows, cs), _shifted(bufv, taps, r0, rows, cs)
                g, v = _taps_fwd(xg, wg, bg), _taps_fwd(xv, wv, bv)
                dav = d_ref[r0:r0 + rows, cs] if inside else jnp.where(i < nt - 1, dn_ref[:, cs], 0.0)
                sg = _sigmoid(g)
                dg = dav * v * (sg * (1.0 + g * (1.0 - sg)))
                dv = dav * (g * sg)
                bufdg[r0:r0 + rows, cs] = dg
                bufdv[r0:r0 + rows, cs] = dv
                if inside:
                    accg[taps], accv[taps] = accg[taps] + _fold8(dg), accv[taps] + _fold8(dv)
                    for k in range(taps):
                        accg[k], accv[k] = accg[k] + _fold8(dg * xg[k]), accv[k] + _fold8(dv * xv[k])
            gbg_ref[:, cs] += jnp.sum(accg[taps], axis=0, keepdims=True)
            gbv_ref[:, cs] += jnp.sum(accv[taps], axis=0, keepdims=True)
            for k in range(taps):
                gwg_ref[k:k + 1, cs] += jnp.sum(accg[k], axis=0, keepdims=True)
                gwv_ref[k:k + 1, cs] += jnp.sum(accv[k], axis=0, keepdims=True)
            for r0 in range(0, tm, CONV_RB):
                du_ref[0, r0:r0 + CONV_RB, cs] = _taps_bwd(bufdg, wg, taps, r0, CONV_RB, cs).astype(BF16)
                du_ref[1, r0:r0 + CONV_RB, cs] = _taps_bwd(bufdv, wv, taps, r0, CONV_RB, cs).astype(BF16)

    gt, gbf, gaf = _halo_specs(tm, wd, t, lead=0)
    vt, vbf, vaf = _halo_specs(tm, wd, t, lead=1)
    dt_, _, dn = _halo_specs(tm, wd, t)
    par = lambda rows, off: pl.BlockSpec((rows, wd), functools.partial(lambda c, i, off: (0, c + off), off=off))
    return pl.pallas_call(
        kern, name="ffn_act_bwd", grid=(nc, nt),
        in_specs=[gt, gbf, gaf, vt, vbf, vaf, dt_, dn, par(taps, 0), par(taps, nc), par(1, 0), par(1, nc)],
        out_specs=[pl.BlockSpec((2, tm, wd), lambda c, i: (0, i, c)), par(taps, 0), par(taps, 0), par(1, 0), par(1, 0)],
        out_shape=[jax.ShapeDtypeStruct((2, t, D_FF), BF16)] + [jax.ShapeDtypeStruct((taps, D_FF), F32)] * 2
        + [jax.ShapeDtypeStruct((1, D_FF), F32)] * 2,
        scratch_shapes=[pltpu.VMEM((tm + 2 * HALO, wd), F32)] * 2 + [pltpu.VMEM((tm + HALO, wd), F32)] * 2,
        compiler_params=_cparams(("parallel", "arbitrary")))(u, u, u, u, u, u, da, da, w, w, b, b)


def _softplus(x):
    e = jnp.exp(-jnp.abs(x))
    return jnp.maximum(x, 0.0) + jnp.where(e < 1e-4, e - 0.5 * e * e, jnp.log(1.0 + e))


def _tri(lower):
    r = lax.broadcasted_iota(jnp.int32, (CHUNK, CHUNK), 0)
    c = lax.broadcasted_iota(jnp.int32, (CHUNK, CHUNK), 1)
    return (r >= c) if lower else (r <= c)


def _cum(mat_bool, x):
    return jnp.dot(mat_bool.astype(F32), x, precision=lax.Precision.HIGHEST, preferred_element_type=F32)


def _pair_sel(lane_lo, tile, h0):
    return jnp.where(lane_lo, tile[:, h0:h0 + 1], tile[:, h0 + 1:h0 + 2])


def _ssd_fwd(xbc_act, proj, dt_bias_p, a_log_p, dskip_t, shards):
    t = xbc_act.shape[0]
    nch = t // CHUNK
    ns = len(shards)

    def kern(xa_ref, dtr_ref, bias_ref, alog_ref, dsk_ref, *rest):
        y_ref, dt_ref, hs_ref = rest[ns:ns + 3]
        hst = rest[2 * ns + 3]
        if ns:
            start, forward, finish = _gather_steps(shards, rest[:ns], rest[ns + 3:2 * ns + 3], rest[2 * ns + 4:])
            pl.when(pl.program_id(0) == 0)(start)
            pl.when(pl.program_id(0) == (3 * nch) // 4)(forward)
            pl.when(pl.program_id(0) == nch - 1)(finish)

        @pl.when(pl.program_id(0) == 0)
        def _():
            hst[...] = jnp.zeros(hst.shape, F32)
        dt = _softplus(dtr_ref[...] + bias_ref[...])
        dt_ref[...] = dt
        acum = _cum(_tri(True), dt * (-jnp.exp(alog_ref[...])))
        acum_t = acum.T
        ea = jnp.exp(acum)
        a_last = acum[CHUNK - 1:CHUNK, :]
        dend = jnp.exp(a_last - acum)
        ea_last = jnp.exp(a_last)
        causal = _tri(True)
        lane_lo = lax.broadcasted_iota(jnp.int32, (CHUNK, LANE), 1) < HEAD_DIM
        row_lo = lax.broadcasted_iota(jnp.int32, (CHUNK, LANE), 0) < HEAD_DIM
        for g in range(2):
            bg = xa_ref[:, SSM_INNER + g * SSM_STATE:SSM_INNER + (g + 1) * SSM_STATE].astype(BF16)
            cg = xa_ref[:, SSM_INNER + BC_DIM + g * SSM_STATE:SSM_INNER + BC_DIM + (g + 1) * SSM_STATE].astype(BF16)
            cb = _dot_nt(cg, bg)
            for j in range(4 * g, 4 * g + 4):
                h0 = 2 * j
                cols = slice(j * LANE, (j + 1) * LANE)
                xp = xa_ref[:, cols]
                xdt = xp * _pair_sel(lane_lo, dt, h0)
                ydiag = None
                for hh, sel in ((h0, lane_lo), (h0 + 1, ~lane_lo)):
                    seg = acum[:, hh:hh + 1] - acum_t[hh:hh + 1, :]
                    mm = (cb * jnp.where(causal, jnp.exp(jnp.minimum(seg, 0.0)), 0.0)).astype(BF16)
                    d = _dot(mm, jnp.where(sel, xdt, 0.0).astype(BF16))
                    ydiag = d if ydiag is None else ydiag + d
                hp = hst[cols, :]
                hs_ref[cols, :] = hp
                yoff = _dot_nt(cg, hp.astype(BF16)) * _pair_sel(lane_lo, ea, h0)
                y_ref[:, cols] = ydiag + yoff + dsk_ref[:, cols] * xp
                xw = (xdt * _pair_sel(lane_lo, dend, h0)).astype(BF16)
                rowf = jnp.where(row_lo, ea_last[:, h0:h0 + 1], ea_last[:, h0 + 1:h0 + 2])
                hst[cols, :] = hp * rowf + _dot_tn(xw, bg)

    return pl.pallas_call(
        kern, name="ssd_fwd", grid=(nch,),
        in_specs=[pl.BlockSpec((CHUNK, XBC_DIM), lambda c: (c, 0)), pl.BlockSpec((CHUNK, LANE), lambda c: (c, OFF_DT // LANE)),
                  pl.BlockSpec((1, LANE), lambda c: (0, 0)), pl.BlockSpec((1, LANE), lambda c: (0, 0)),
                  pl.BlockSpec((1, SSM_INNER), lambda c: (0, 0))] + [ANY_SPEC] * ns,
        out_specs=[pl.BlockSpec((CHUNK, SSM_INNER), lambda c: (c, 0)), pl.BlockSpec((CHUNK, LANE), lambda c: (c, 0)),
                   pl.BlockSpec((None, SSM_INNER, SSM_STATE), lambda c: (c, 0, 0))] + [ANY_SPEC] * ns,
        out_shape=[jax.ShapeDtypeStruct((t, SSM_INNER), F32), jax.ShapeDtypeStruct((t, LANE), F32),
                   jax.ShapeDtypeStruct((nch, SSM_INNER, SSM_STATE), F32)] + _gather_out_shapes(shards),
        scratch_shapes=[pltpu.VMEM((SSM_INNER, SSM_STATE), F32)] + (_gather_sems(ns) if ns else []),
        compiler_params=_cparams(("arbitrary",)))(xbc_act, proj, dt_bias_p, a_log_p, dskip_t, *shards)


def _ssd_bwd(xbc_act, proj, dt_sp, hstates, dy, dt_bias_p, a_log_p, dskip_t):
    t = xbc_act.shape[0]
    nch = t // CHUNK

    pair = jnp.arange(SSM_HEADS // 2)[:, None, None]
    psel = (jnp.arange(LANE)[None, None, :] == 2 * pair + (jnp.arange(LANE) // HEAD_DIM)[None, :, None]).astype(BF16)

    def kern(xa_ref, dtr_ref, dt_ref, hs_ref, dy_ref, bias_ref, alog_ref, dsk_ref, psel_ref,
             dact_ref, ddtr_ref, da_ref, dbias_ref, ddsk_ref, dh):
        @pl.when(pl.program_id(0) == 0)
        def _():
            dh[...] = jnp.zeros(dh.shape, F32)
            for r in (da_ref, dbias_ref, ddsk_ref):
                r[...] = jnp.zeros(r.shape, F32)
        dt = dt_ref[...]
        a_neg = -jnp.exp(alog_ref[...])
        acum = _cum(_tri(True), dt * a_neg)
        acum_t = acum.T
        ea = jnp.exp(acum)
        a_last = acum[CHUNK - 1:CHUNK, :]
        dend = jnp.exp(a_last - acum)
        ea_last = jnp.exp(a_last)
        causal = _tri(True)
        lane = lax.broadcasted_iota(jnp.int32, (CHUNK, LANE), 1)
        rowi = lax.broadcasted_iota(jnp.int32, (CHUNK, LANE), 0)
        lane_lo, row_lo, last_row = lane < HEAD_DIM, rowi < HEAD_DIM, rowi == CHUNK - 1
        d_dt = jnp.zeros((CHUNK, LANE), F32)
        d_acum = jnp.zeros((CHUNK, LANE), F32)
        for g in range(2):
            bcols = slice(SSM_INNER + g * SSM_STATE, SSM_INNER + (g + 1) * SSM_STATE)
            ccols = slice(SSM_INNER + BC_DIM + g * SSM_STATE, SSM_INNER + BC_DIM + (g + 1) * SSM_STATE)
            bg, cg = xa_ref[:, bcols].astype(BF16), xa_ref[:, ccols].astype(BF16)
            cb = _dot_nt(cg, bg)
            dg_sum = jnp.zeros((CHUNK, CHUNK), F32)
            dcg = jnp.zeros((CHUNK, SSM_STATE), F32)
            dbg = jnp.zeros((CHUNK, SSM_STATE), F32)
            for j in range(4 * g, 4 * g + 4):
                h0 = 2 * j
                cols = slice(j * LANE, (j + 1) * LANE)
                xp, dyp = xa_ref[:, cols], dy_ref[:, cols]
                dtsel = _pair_sel(lane_lo, dt, h0)
                xdt = xp * dtsel
                xdt_b = xdt.astype(BF16)
                hp, dhp = hs_ref[cols, :], dh[cols, :]
                hp_b, dhp_b = hp.astype(BF16), dhp.astype(BF16)
                easel, dendsel = _pair_sel(lane_lo, ea, h0), _pair_sel(lane_lo, dend, h0)
                dx, ydiag = None, None
                for hh, sel in ((h0, lane_lo), (h0 + 1, ~lane_lo)):
                    dyh = jnp.where(sel, dyp, 0.0).astype(BF16)
                    seg = acum[:, hh:hh + 1] - acum_t[hh:hh + 1, :]
                    dec = jnp.where(causal, jnp.exp(jnp.minimum(seg, 0.0)), 0.0)
                    mm_b = (cb * dec).astype(BF16)
                    dg_sum = dg_sum + dec * _dot_nt(dyh, xdt_b)
                    d = _dot_tn(mm_b, dyh)
                    y = _dot(mm_b, jnp.where(sel, xdt, 0.0).astype(BF16))
                    dx = d if dx is None else dx + d
                    ydiag = y if ydiag is None else ydiag + y
                g2 = _dot_nt(bg, dhp_b)
                tprod = xdt * g2 * dendsel
                yoff = _dot_nt(cg, hp_b) * easel
                yc = dyp.astype(BF16).astype(F32) * ydiag + dyp * yoff - (xdt_b.astype(F32) * dx + tprod)
                dx = dx + g2 * dendsel
                psel = psel_ref[j]
                t_lo = jnp.sum(jnp.where(lane_lo, tprod, 0.0), keepdims=True).reshape(1, 1)
                t_hi = jnp.sum(tprod, keepdims=True).reshape(1, 1) - t_lo
                hh_prod = dhp * hp
                s_lo = jnp.sum(jnp.where(row_lo, hh_prod, 0.0), keepdims=True).reshape(1, 1)
                s_hi = jnp.sum(hh_prod, keepdims=True).reshape(1, 1) - s_lo
                end_lo = ea_last[:, h0:h0 + 1] * s_lo + t_lo
                end_hi = ea_last[:, h0 + 1:h0 + 2] * s_hi + t_hi
                ends = jnp.where(lane == h0, end_lo, jnp.where(lane == h0 + 1, end_hi, 0.0))
                d_acum = d_acum + _dot_split(yc, psel) + jnp.where(last_row, ends, 0.0)
                dye = (dyp * easel).astype(BF16)
                dcg = dcg + _dot(dye, hp_b)
                dbg = dbg + _dot((xdt * dendsel).astype(BF16), dhp_b)
                rowf = jnp.where(row_lo, ea_last[:, h0:h0 + 1], ea_last[:, h0 + 1:h0 + 2])
                dh[cols, :] = dhp * rowf + _dot_tn(dye, cg)
                dact_ref[:, cols] = dx * dtsel + dsk_ref[:, cols] * dyp
                d_dt = d_dt + _dot_split(dx * xp, psel)
                ddsk_ref[:, cols] += jnp.sum(dyp * xp, axis=0, keepdims=True)
            dg_b = dg_sum.astype(BF16)
            dact_ref[:, ccols] = dcg + _dot(dg_b, bg)
            dact_ref[:, bcols] = dbg + _dot_tn(dg_b, cg)
        d_adt = _cum(_tri(False), d_acum)
        d_dt = d_dt + d_adt * a_neg
        da_ref[...] += jnp.sum(d_adt * dt, axis=0, keepdims=True)
        d_raw = jnp.where(lane < SSM_HEADS, d_dt * _sigmoid(dtr_ref[...] + bias_ref[...]), 0.0)
        ddtr_ref[...] = d_raw.astype(BF16)
        dbias_ref[...] += jnp.sum(d_raw, axis=0, keepdims=True)

    rev = lambda c: (nch - 1 - c, 0)
    return pl.pallas_call(
        kern, name="ssd_bwd", grid=(nch,),
        in_specs=[pl.BlockSpec((CHUNK, XBC_DIM), rev), pl.BlockSpec((CHUNK, LANE), lambda c: (nch - 1 - c, OFF_DT // LANE)),
                  pl.BlockSpec((CHUNK, LANE), rev), pl.BlockSpec((None, SSM_INNER, SSM_STATE), lambda c: (nch - 1 - c, 0, 0)),
                  pl.BlockSpec((CHUNK, SSM_INNER), rev),
                  pl.BlockSpec((1, LANE), lambda c: (0, 0)), pl.BlockSpec((1, LANE), lambda c: (0, 0)),
                  pl.BlockSpec((1, SSM_INNER), lambda c: (0, 0)), pl.BlockSpec(psel.shape, lambda c: (0, 0, 0))],
        out_specs=[pl.BlockSpec((CHUNK, XBC_DIM), rev), pl.BlockSpec((CHUNK, LANE), rev),
                   pl.BlockSpec((1, LANE), lambda c: (0, 0)), pl.BlockSpec((1, LANE), lambda c: (0, 0)),
                   pl.BlockSpec((1, SSM_INNER), lambda c: (0, 0))],
        out_shape=[jax.ShapeDtypeStruct((t, XBC_DIM), F32), jax.ShapeDtypeStruct((t, LANE), BF16),
                   jax.ShapeDtypeStruct((1, LANE), F32), jax.ShapeDtypeStruct((1, LANE), F32),
                   jax.ShapeDtypeStruct((1, SSM_INNER), F32)],
        scratch_shapes=[pltpu.VMEM((SSM_INNER, SSM_STATE), F32)],
        compiler_params=_cparams(("arbitrary",)))(xbc_act, proj, dt_sp, hstates, dy, dt_bias_p, a_log_p, dskip_t, psel)


def _ssm_post_bwd(dmix, y, proj, g):
    def body(do_ref, y_ref, z_ref, g_ref, dy_ref, dz_ref, dg_ref):
        z, yv, dout = z_ref[...], y_ref[...], do_ref[...]
        sg = _sigmoid(z)
        gz = z * sg
        yz = yv * gz
        r = lax.rsqrt(jnp.mean(yz * yz, axis=-1, keepdims=True) + EPS)
        gd = dout * g_ref[...]
        dyz = r * gd - yz * (r * r * r * jnp.mean(yz * gd, axis=-1, keepdims=True))
        dy_ref[...] = dyz * gz
        dz_ref[...] = (dyz * yv * (sg * (1.0 + z * (1.0 - sg)))).astype(BF16)
        dg_ref[...] += jnp.sum(dout * yz * r, axis=0, keepdims=True)
    return _rows("ssm_post_bwd", body,
                 [("tc", dmix, SSM_INNER, 0), ("t", y), ("tc", proj, SSM_INNER, OFF_Z // SSM_INNER), ("p", g)],
                 [(SSM_INNER, F32), (SSM_INNER, BF16)], accs=[(1, SSM_INNER)])


def _pad_lanes(v, width=LANE):
    return jnp.pad(v, ((0, 0), (0, width - v.shape[1])))


def _local_step(x, p, tgt, wts, late_shards=(), join_late=None, reduce_early=None, reduce_late=None):
    g_attn, g_ssm, g_ffn, g_ple = wts["attn_norm_g"], wts["ssm_norm_g"], wts["ffn_norm_g"], wts["ple_norm_g"]
    w_in_p = wts["w_in_p"]
    gq_t = jnp.tile(wts["q_norm_g"], (1, ATTN_DIM // HEAD_DIM))
    gk_t = jnp.tile(wts["k_norm_g"], (1, KV_DIM // HEAD_DIM))
    dt_bias_p, a_log_p = _pad_lanes(wts["dt_bias"]), _pad_lanes(wts["a_log"])
    dskip_t = jnp.repeat(wts["d_skip"], HEAD_DIM, axis=1)

    h1, proj = _norm_mm("in_proj", x, g_attn, w_in_p)
    q_hm, kv_hm = _qknorm_fwd2(proj, gq_t, gk_t)
    pats = [_attn_fwd2(q_hm, kv_hm, d) for d in DILATIONS]
    attn_out, lse = _attn_merge2([o for o, _ in pats], [l for _, l in pats])
    xbc_act = _ssm_conv_fwd(proj, wts["ssm_conv_w"], wts["ssm_conv_b"])
    y_ssd, dt_sp, hstates, *gathered = _ssd_fwd(xbc_act, proj, dt_bias_p, a_log_p, dskip_t, list(late_shards))
    if join_late is not None:
        wts = {**wts, **join_late(gathered)}
    w_out_s, w_out_a = wts["w_out_ssm"], wts["w_out_attn"]
    w_up, w_down, w_gate, w_proj = wts["w_up"], wts["w_down"], wts["w_ple_gate"], wts["w_ple_proj"]
    ssm_out, x1 = _ssm_out_proj(y_ssd, proj, g_ssm, w_out_s, attn_out, w_out_a, x)
    h2, u = _norm_mm("ffn_up", x1, g_ffn, w_up, tm=1024, tn=1408, halves=True)
    a, x2 = _ffn_act_down(u, wts["ffn_conv_w"], wts["ffn_conv_b"], w_down, x1)
    pb = p.astype(BF16)
    h3, dy, dgl, dpp, sq = _ple_head(x2, g_ple, w_gate, pb, w_proj, tgt)

    grads = {}
    grads["w_ple_proj"] = _mm_tn("g_ple_proj", pb, dpp, tn=PLE_DIM, chip_cols=True)
    grads["w_ple_gate"] = _mm_tn("g_ple_gate", h3, dgl)
    dx2, dx2b, grads["ple_norm_g"] = _mm_nt_rms_bwd("d_h3", dgl, w_gate, x2, g_ple, dy)
    da = _mm_nt("d_ffn_act", [(dx2b, w_down, 0)], F32, tm=1024, tn=1408)
    grads["w_down"] = _mm_tn("g_ffn_down", a, dx2b, tm=1408)
    du, gwg, gwv, gbg, gbv = _ffn_act_bwd(u, wts["ffn_conv_w"], wts["ffn_conv_b"], da)
    grads["ffn_conv_w"] = jnp.concatenate([gwg, gwv], axis=1)
    grads["ffn_conv_b"] = jnp.concatenate([gbg, gbv], axis=1)
    grads["w_up"] = _mm_tn("g_ffn_up", h2, du, tn=1408, chip_cols=True)
    dh2 = _mm_nt("d_h2", [(du, w_up, 0, 0), (du, w_up, 1, 1)], F32, tm=1024, tn=512)
    dx1, dx1b, grads["ffn_norm_g"] = _rms_bwd("rms_ffn_bwd", dh2, x1, g_ffn, dx2)
    dmix = _mm_nt("d_mix", [(dx1b, jnp.concatenate([w_out_s, w_out_a], axis=0), 0)], F32, tm=1024)
    grads["w_out"] = jnp.concatenate([_mm_tn("g_out_attn", attn_out, dx1b), _mm_tn("g_out_ssm", ssm_out, dx1b)], axis=0)
    dy_ssd, dz, grads["ssm_norm_g"] = _ssm_post_bwd(dmix, y_ssd, proj, g_ssm)
    dact, ddtr, d_a, d_bias, d_dsk = _ssd_bwd(xbc_act, proj, dt_sp, hstates, dy_ssd, dt_bias_p, a_log_p, dskip_t)
    grads["dt_bias"] = d_bias[:, :SSM_HEADS]
    grads["a_log"] = d_a[:, :SSM_HEADS] * (-jnp.exp(wts["a_log"]))
    grads["d_skip"] = jnp.sum(d_dsk.reshape(SSM_HEADS, HEAD_DIM), axis=1)[None, :]
    chip_sums = reduce_early(grads) if reduce_early is not None else []
    dxbc, grads["ssm_conv_w"], grads["ssm_conv_b"], *scattered = _ssm_conv_bwd(proj, wts["ssm_conv_w"], wts["ssm_conv_b"], dact,
                                                                                chip_sums)
    do_hm, dsum = _attn_bwd_prep2(dmix, attn_out)
    dqs = [_attn_dq2(q_hm, kv_hm, do_hm, lse, dsum, d) for d in DILATIONS]
    dkvs = [_attn_dkv2(q_hm, kv_hm, do_hm, lse, dsum, d) for d in DILATIONS]
    dq, dk, dv, dgq, dgk = _qknorm_bwd2(proj, gq_t, gk_t, dqs, dkvs)
    grads["q_norm_g"] = jnp.sum(dgq.reshape(ATTN_DIM // HEAD_DIM, HEAD_DIM), axis=0)[None, :]
    grads["k_norm_g"] = jnp.sum(dgk.reshape(KV_DIM // HEAD_DIM, HEAD_DIM), axis=0)[None, :]
    dproj = jnp.concatenate([dxbc, dq, dz, dk, dv, ddtr], axis=1)
    grads["w_in_p"] = _mm_tn("g_in_proj", h1, dproj, tm=512)
    late_sums = reduce_late(grads) if reduce_late is not None else []
    grad_x, _, grads["attn_norm_g"], *late_scattered = _mm_nt_rms_bwd("d_h1", dproj, w_in_p, x, g_attn, dx1, late_sums)
    return sq, grad_x, grads, (chip_sums, scattered), (late_sums, late_scattered)


MESH_IDS = pl.DeviceIdType.MESH
N_CHIPS = 4
ANY_SPEC = pl.BlockSpec(memory_space=pl.ANY)
SMALL_ROWS = 96
ALL_SMALL_ROWS = 272


def _place():
    x, y, c = lax.axis_index("x"), lax.axis_index("y"), lax.axis_index("c")
    return x, y, c, [(1 - x, y), (x, 1 - y), (1 - x, 1 - y)]


def _gather_over_chips(arrs):
    n = len(arrs)

    def body(*refs):
        steps = _gather_steps(arrs, refs[:n], refs[n:2 * n], refs[2 * n:2 * n + 4])
        for step in steps:
            step()

    return pl.pallas_call(
        body, name="gather_weights", in_specs=[ANY_SPEC] * n, out_specs=[ANY_SPEC] * n,
        out_shape=_gather_out_shapes(arrs), scratch_shapes=_gather_sems(n))(*arrs)


def _gather_out_shapes(arrs):
    return [jax.ShapeDtypeStruct((N_CHIPS,) + a.shape, a.dtype) for a in arrs]


def _gather_sems(n):
    return [pltpu.SemaphoreType.DMA((3 * n,))] * 4


def _gather_steps(arrs, ins, outs, sems):
    n = len(arrs)
    split = [a.shape[0] % 64 == 0 for a in arrs]
    ici_send, ici_recv, d2d_send, d2d_recv = sems

    def place():
        x, y, c, chips = _place()
        return x, y, c, chips, 2 * x + y

    def part(ref, a, core):
        if not split[a]:
            return ref
        half = arrs[a].shape[0] // 2
        return ref.at[pl.ds(core * half, half)]

    def ici(a, k, slot, where):
        x, y, c, chips, _ = where
        px, py = chips[k]
        return pltpu.make_async_remote_copy(
            src_ref=part(ins[a], a, c), dst_ref=part(outs[a].at[slot], a, c), send_sem=ici_send.at[3 * a + k],
            recv_sem=ici_recv.at[3 * a + k], device_id=(px, py, c), device_id_type=MESH_IDS)

    def d2d(a, k, core, where):
        x, y, c, chips, _ = where
        px, py = chips[k]
        piece = part(outs[a].at[2 * px + py], a, core)
        return pltpu.make_async_remote_copy(src_ref=piece, dst_ref=piece, send_sem=d2d_send.at[3 * a + k],
                                            recv_sem=d2d_recv.at[3 * a + k], device_id=(x, y, 1 - c), device_id_type=MESH_IDS)

    def start():
        w = place()
        for a in range(n):
            for k in range(3):
                ici(a, k, w[4], w).start()

    def forward():
        w = place()
        for a in range(n):
            for k, (px, py) in enumerate(w[3]):
                ici(a, k, 2 * px + py, w).wait_recv()
                if split[a]:
                    d2d(a, k, w[2], w).start()

    def finish():
        w = place()
        for a in range(n):
            for k in range(3):
                if split[a]:
                    d2d(a, k, 1 - w[2], w).wait_recv()
                    d2d(a, k, w[2], w).wait_send()
                ici(a, k, w[4], w).wait_send()

    return start, forward, finish


def _row_tile(rows, cap=256):
    return max(d for d in range(8, cap + 1, 8) if rows % d == 0)


def _swap_halves(name, gs):
    n = len(gs)

    def body(*refs):
        ins, outs, send, recv = refs[:n], refs[n:2 * n], refs[2 * n], refs[2 * n + 1]
        x, y, c, _ = _place()
        cps = []
        for a in range(n):
            half = gs[a].shape[1] // 2
            for q in range(N_CHIPS):
                cps.append(pltpu.make_async_remote_copy(
                    src_ref=ins[a].at[q, pl.ds((1 - c) * half, half)], dst_ref=outs[a].at[q], send_sem=send.at[N_CHIPS * a + q],
                    recv_sem=recv.at[N_CHIPS * a + q], device_id=(x, y, 1 - c), device_id_type=MESH_IDS))
        for cp in cps:
            cp.start()
        for cp in cps:
            cp.wait()

    return pl.pallas_call(
        body, name=name, in_specs=[ANY_SPEC] * n, out_specs=[ANY_SPEC] * n,
        out_shape=[jax.ShapeDtypeStruct((N_CHIPS, g.shape[1] // 2, g.shape[2]), g.dtype) for g in gs],
        scratch_shapes=[pltpu.SemaphoreType.DMA((N_CHIPS * n,))] * 2)(*gs)


def _add_halves(name, g, got, c_idx):
    rows, cols = got.shape[1:]
    tm = _row_tile(rows)
    per = rows // tm

    def kern(c_ref, g_ref, r_ref, o_ref):
        o_ref[...] = (g_ref[...] + r_ref[...]).astype(BF16)

    return pl.pallas_call(
        kern, name=name,
        grid_spec=pltpu.PrefetchScalarGridSpec(
            num_scalar_prefetch=1, grid=(N_CHIPS, per),
            in_specs=[pl.BlockSpec((None, tm, cols), lambda q, i, c_ref: (q, c_ref[0] * per + i, 0)),
                      pl.BlockSpec((None, tm, cols), lambda q, i, c_ref: (q, i, 0))],
            out_specs=pl.BlockSpec((None, tm, cols), lambda q, i, c_ref: (q, i, 0))),
        out_shape=jax.ShapeDtypeStruct((N_CHIPS, rows, cols), BF16),
        compiler_params=_cparams(("parallel", "parallel")))(c_idx, g, got)


def _scatter_sems(n):
    return [pltpu.SemaphoreType.DMA((3 * n,))] * 2


def _scatter_steps(n, ins, outs, sems):
    send, recv = sems

    def copy(a, k, slot, where):
        x, y, c, chips = where
        px, py = chips[k]
        return pltpu.make_async_remote_copy(src_ref=ins[a].at[2 * px + py], dst_ref=outs[a].at[slot], send_sem=send.at[3 * a + k],
                                            recv_sem=recv.at[3 * a + k], device_id=(px, py, c), device_id_type=MESH_IDS)

    def start():
        w = _place()
        for a in range(n):
            for k in range(3):
                copy(a, k, 2 * w[0] + w[1], w).start()

    def finish():
        w = _place()
        for a in range(n):
            for k, (px, py) in enumerate(w[3]):
                copy(a, k, 2 * px + py, w).wait()

    return start, finish


def _sum_chips(name, own, parts, idx):
    rows, cols = parts.shape[1:]
    tm = _row_tile(rows)
    per = rows // tm

    def kern(o_idx, a_ref, b_ref, c_ref, d_ref, o_ref):
        o_ref[...] = ((a_ref[...].astype(F32) + b_ref[...].astype(F32)) + c_ref[...].astype(F32)) + d_ref[...].astype(F32)

    def spec(k):
        return pl.BlockSpec((None, tm, cols), functools.partial(lambda i, o_idx, k: (o_idx[k], i, 0), k=k))

    return pl.pallas_call(
        kern, name=name,
        grid_spec=pltpu.PrefetchScalarGridSpec(
            num_scalar_prefetch=1, grid=(per,), in_specs=[spec(0), spec(1), spec(2), spec(3)],
            out_specs=pl.BlockSpec((None, tm, cols), lambda i, o_idx: (0, o_idx[4] * per + i, 0))),
        out_shape=jax.ShapeDtypeStruct((1, 2 * rows, cols), F32), compiler_params=_cparams(("parallel",)))(idx, own, parts, parts, parts)


def _share_with_sibling(gs):
    n = len(gs)

    def body(*refs):
        ins, send, recv = refs[:n], refs[2 * n], refs[2 * n + 1]
        x, y, c, _ = _place()
        cps = []
        for a in range(n):
            half = gs[a].shape[1] // 2
            mine = pl.ds(c * half, half)
            cps.append(pltpu.make_async_remote_copy(src_ref=ins[a].at[0, mine], dst_ref=refs[n + a].at[0, mine], send_sem=send.at[a],
                                                    recv_sem=recv.at[a], device_id=(x, y, 1 - c), device_id_type=MESH_IDS))
        for cp in cps:
            cp.start()
        for cp in cps:
            cp.wait()

    return pl.pallas_call(
        body, name="grad_share_sibling", in_specs=[ANY_SPEC] * n, out_specs=[ANY_SPEC] * n,
        out_shape=[jax.ShapeDtypeStruct(g.shape, g.dtype) for g in gs], input_output_aliases={a: a for a in range(n)},
        scratch_shapes=[pltpu.SemaphoreType.DMA((n,))] * 2)(*gs)


def _allreduce_small(v):
    def body(v_ref, o_ref, land, send, recv):
        x, y, c, _ = _place()
        me = 4 * x + 2 * y + c
        land[me] = v_ref[...]
        cps = []
        for rel in range(1, 8):
            bx, by, bc = (rel >> 2) & 1, (rel >> 1) & 1, rel & 1
            peer = (1 - x if bx else x, 1 - y if by else y, 1 - c if bc else c)
            cps.append(pltpu.make_async_remote_copy(src_ref=v_ref, dst_ref=land.at[me], send_sem=send.at[rel - 1],
                                                    recv_sem=recv.at[rel - 1], device_id=peer, device_id_type=MESH_IDS))
        for cp in cps:
            cp.start()
        for cp in cps:
            cp.wait()
        acc = land[0]
        for d in range(1, 8):
            acc = acc + land[d]
        o_ref[...] = acc

    vm = pl.BlockSpec(memory_space=pltpu.VMEM)
    return pl.pallas_call(
        body, name="allreduce_small", in_specs=[vm], out_specs=vm, out_shape=jax.ShapeDtypeStruct(v.shape, F32),
        scratch_shapes=[pltpu.VMEM((8,) + v.shape, F32), pltpu.SemaphoreType.DMA((7,)), pltpu.SemaphoreType.DMA((7,))])(v)


def _adamw(name, w, g, m, v):
    _, rows, cols = w.shape
    tm = rows if rows * cols <= 128 * 1024 else _row_tile(rows)
    c1 = 1.0 / (1.0 - ADAM_B1 ** ADAM_STEP)
    c2 = 1.0 / (1.0 - ADAM_B2 ** ADAM_STEP)

    def kern(w_ref, g_ref, m_ref, v_ref, d_ref, mo_ref, vo_ref):
        gv = g_ref[...]
        mn = ADAM_B1 * m_ref[...] + (1.0 - ADAM_B1) * gv
        vn = ADAM_B2 * v_ref[...] + (1.0 - ADAM_B2) * (gv * gv)
        d_ref[...] = -ADAM_LR * ((mn * c1) / (jnp.sqrt(vn * c2) + ADAM_EPS) + ADAM_WD * w_ref[...])
        mo_ref[...] = mn
        vo_ref[...] = vn

    spec = pl.BlockSpec((None, tm, cols), lambda i: (0, i, 0))
    return pl.pallas_call(
        kern, name=name, grid=(rows // tm,), in_specs=[spec] * 4, out_specs=[spec] * 3,
        out_shape=[jax.ShapeDtypeStruct(w.shape, F32)] * 3, compiler_params=_cparams(("parallel",)))(w, g, m, v)


SHARDED = (("w_in", 1), ("w_out", 0), ("w_up", 1), ("w_down", 0), ("w_ple_gate", 0), ("w_ple_proj", 1),
           ("ssm_conv_w", 1), ("ffn_conv_w", 1))
MATRICES = ("w_in", "w_out", "w_up", "w_down", "w_ple_gate", "w_ple_proj")
EARLY_REDUCED = MATRICES[1:]
REPLICATED = ("attn_norm_g", "q_norm_g", "k_norm_g", "ssm_conv_b", "dt_bias", "a_log", "d_skip", "ssm_norm_g",
              "ffn_norm_g", "ffn_conv_b", "ple_norm_g")
WEIGHT_ORDER = ("attn_norm_g", "w_in", "q_norm_g", "k_norm_g", "ssm_conv_w", "ssm_conv_b", "dt_bias", "a_log", "d_skip",
                "ssm_norm_g", "w_out", "ffn_norm_g", "w_up", "ffn_conv_w", "ffn_conv_b", "w_down", "ple_norm_g",
                "w_ple_gate", "w_ple_proj")


def _join_chips(g, axis):
    if axis == 0:
        return g.reshape(g.shape[0] * g.shape[1], g.shape[2])
    return jnp.transpose(g, (1, 0, 2)).reshape(g.shape[1], g.shape[0] * g.shape[2])


def _split_chips(g, axis):
    if axis == 0:
        return g.reshape(N_CHIPS, g.shape[0] // N_CHIPS, g.shape[1])
    r, c = g.shape
    return jnp.transpose(g.reshape(r, N_CHIPS, c // N_CHIPS), (1, 0, 2))


def _pack_small(vals, rows=SMALL_ROWS):
    flat = jnp.concatenate([v.reshape(-1) for v in vals])
    return jnp.pad(flat, (0, rows * LANE - flat.shape[0])).reshape(rows, LANE)


def _unpack_small(packed, like):
    flat, out, off = packed.reshape(-1), [], 0
    for v in like:
        out.append(flat[off:off + v.size].reshape(v.shape))
        off += v.size
    return out


def kernel(x, p, attn_norm_g, w_in, q_norm_g, k_norm_g, ssm_conv_w, ssm_conv_b, dt_bias, a_log, d_skip, ssm_norm_g, w_out, ffn_norm_g, w_up, ffn_conv_w, ffn_conv_b, w_down, ple_norm_g, w_ple_gate, w_ple_proj, loss_target, m_attn_norm_g, m_w_in, m_q_norm_g, m_k_norm_g, m_ssm_conv_w, m_ssm_conv_b, m_dt_bias, m_a_log, m_d_skip, m_ssm_norm_g, m_w_out, m_ffn_norm_g, m_w_up, m_ffn_conv_w, m_ffn_conv_b, m_w_down, m_ple_norm_g, m_w_ple_gate, m_w_ple_proj, v_attn_norm_g, v_w_in, v_q_norm_g, v_k_norm_g, v_ssm_conv_w, v_ssm_conv_b, v_dt_bias, v_a_log, v_d_skip, v_ssm_norm_g, v_w_out, v_ffn_norm_g, v_w_up, v_ffn_conv_w, v_ffn_conv_b, v_w_down, v_ple_norm_g, v_w_ple_gate, v_w_ple_proj):
    given = dict(locals())
    w2 = {n: given[n].reshape(given[n].shape[-2:]) if given[n].ndim == 3 else given[n] for n in WEIGHT_ORDER}

    cx, cy, cc = lax.axis_index("x"), lax.axis_index("y"), lax.axis_index("c")
    chip = 2 * cx + cy
    axis_of = dict(SHARDED)
    shard = lambda n: w2[n].astype(BF16) if n in MATRICES else w2[n]
    join = lambda n, g: _join_chips(lax.dynamic_update_index_in_dim(g, shard(n), chip, 0), axis_of[n])
    first = ("w_in", "ssm_conv_w", "ffn_conv_w")
    full = {n: join(n, g) for n, g in zip(first, _gather_over_chips([shard(n) for n in first]))}
    win = full["w_in"]
    w_in_p = jnp.concatenate([win[:, 2048:3584], win[:, 0:512], win[:, 1024:2048], win[:, 512:768], win[:, 768:1024],
                              win[:, 3584:3600], jnp.zeros((D_MODEL, PROJ_P - IN_PROJ), BF16)], axis=1)
    wts = {n: w2[n] for n in REPLICATED}
    wts.update(w_in_p=w_in_p, ssm_conv_w=full["ssm_conv_w"], ffn_conv_w=full["ffn_conv_w"])

    def join_late(gathered):
        late = {n: join(n, g) for n, g in zip(EARLY_REDUCED, gathered)}
        return dict(w_out_attn=late["w_out"][:ATTN_DIM], w_out_ssm=late["w_out"][ATTN_DIM:], w_up=late["w_up"],
                    w_down=late["w_down"], w_ple_gate=late["w_ple_gate"], w_ple_proj=late["w_ple_proj"])

    core = cc.astype(jnp.int32).reshape(1)
    idx = jnp.stack([chip, 2 * (1 - cx) + cy, 2 * cx + (1 - cy), 2 * (1 - cx) + (1 - cy), cc]).astype(jnp.int32)

    def chip_sums_of(tag, names, gd):
        major = [gd[n] if gd[n].ndim == 3 else _split_chips(gd[n], axis_of[n]) for n in names]
        return [_add_halves("grad_add_halves_" + n, g, got, core) for n, g, got in zip(names, major, _swap_halves(tag, major))]

    def w_in_sums(gd):
        gi = gd["w_in_p"]
        gd["w_in"] = jnp.concatenate([gi[:, OFF_Q:OFF_Q + ATTN_DIM], gi[:, OFF_K:OFF_K + KV_DIM], gi[:, OFF_V:OFF_V + KV_DIM],
                                      gi[:, OFF_Z:OFF_Z + SSM_INNER], gi[:, OFF_XBC:OFF_XBC + XBC_DIM], gi[:, OFF_DT:OFF_DT + SSM_HEADS]],
                                     axis=1)
        return chip_sums_of("grad_swap_halves_late", ("w_in",), gd)

    sq, grad_x, grads, early, late = _local_step(
        x[0], p[0, 0], loss_target[0], wts, [shard(n) for n in EARLY_REDUCED], join_late,
        functools.partial(chip_sums_of, "grad_swap_halves_early", EARLY_REDUCED), w_in_sums)
    sums = dict(zip(EARLY_REDUCED + ("w_in",), list(zip(*early)) + list(zip(*late))))
    halves = [_sum_chips("grad_sum_chips_" + n, *sums[n], idx) for n in MATRICES]
    g_shard = dict(zip(MATRICES, _share_with_sibling(halves)))

    small_names = REPLICATED + ("ssm_conv_w", "ffn_conv_w")
    small_like = [grads[n] for n in small_names] + [jnp.zeros((1,), F32)]
    small = _allreduce_small(_pack_small([grads[n] for n in small_names] + [jnp.sum(sq).reshape(1)], ALL_SMALL_ROWS))
    small_vals = dict(zip(small_names + ("loss",), _unpack_small(small, small_like)))
    loss = (0.5 / D_MODEL) * small_vals["loss"][0]
    for n in ("ssm_conv_w", "ffn_conv_w"):
        cols = w2[n].shape[1]
        g_shard[n] = lax.dynamic_slice_in_dim(small_vals[n], chip * cols, cols, axis=1)[None]

    delta, new_m, new_v = {}, {}, {}
    for n, _ in SHARDED:
        delta[n], new_m[n], new_v[n] = _adamw("adamw_" + n, given[n], g_shard[n], given["m_" + n], given["v_" + n])
    packed = lambda prefix: _pack_small([given[prefix + n] for n in REPLICATED])[None]
    sm = _adamw("adamw_small", packed(""), _pack_small([small_vals[n] for n in REPLICATED])[None], packed("m_"), packed("v_"))
    for n in REPLICATED:
        g_shard[n] = small_vals[n]
    for dst, packed_out in zip((delta, new_m, new_v), sm):
        for n, val in zip(REPLICATED, _unpack_small(packed_out[0], [w2[n] for n in REPLICATED])):
            dst[n] = val

    def shaped(d):
        return [d[n].reshape(given[n].shape) for n in WEIGHT_ORDER]
    return (loss, grad_x[None], *shaped(g_shard), *shaped(delta), *shaped(new_m), *shaped(new_v))
```

```python
import functools

import jax
import jax.numpy as jnp
from jax import lax
from jax.experimental import pallas as pl
from jax.experimental.pallas import tpu as pltpu

F32 = jnp.float32
BF16 = jnp.bfloat16

D_MODEL = 1024
HEAD_DIM = 64
ATTN_DIM = 512
KV_DIM = 256
N_KV = 4
SSM_INNER = 1024
SSM_HEADS = 16
SSM_STATE = 128
BC_DIM = 256
XBC_DIM = SSM_INNER + 2 * BC_DIM
MIX_DIM = ATTN_DIM + SSM_INNER
IN_PROJ = 3600
D_FF = 2816
PLE_DIM = 256
CHUNK = 128
DILATIONS = (1, 4, 16)
EPS = 1e-6
ADAM_LR, ADAM_B1, ADAM_B2, ADAM_EPS, ADAM_WD, ADAM_STEP = 0.001, 0.9, 0.999, 1e-08, 0.01, 10

PROJ_P = 3712
OFF_XBC, OFF_Q, OFF_Z, OFF_K, OFF_V, OFF_DT = 0, 1536, 2048, 3072, 3328, 3584
LANE = 128
VMEM_LIMIT = 48 * 1024 * 1024
NEG = -1e30


def _cparams(sem):
    return pltpu.CompilerParams(dimension_semantics=sem, vmem_limit_bytes=VMEM_LIMIT)


def _sigmoid(x):
    return 1.0 / (1.0 + jnp.exp(-x))


def _dot(a, b):
    return jnp.dot(a, b, preferred_element_type=F32)


def _dot_nt(a, b):
    return lax.dot_general(a, b, (((1,), (1,)), ((), ())), preferred_element_type=F32)


def _dot_tn(a, b):
    return lax.dot_general(a, b, (((0,), (0,)), ((), ())), preferred_element_type=F32)


def _dot_split(x, m):
    hi = x.astype(BF16)
    lo = (x - hi.astype(F32)).astype(BF16)
    return _dot(hi, m) + _dot(lo, m)


def _rows(name, body, ins, outs, accs=(), tm=512):
    t_rows = next(s[1].shape[0] for s in ins if s[0] in ("t", "tc"))
    tm = min(tm, t_rows)
    in_specs, args = [], []
    for s in ins:
        if s[0] == "t":
            in_specs.append(pl.BlockSpec((tm, s[1].shape[1]), lambda i: (i, 0)))
        elif s[0] == "tc":
            in_specs.append(pl.BlockSpec((tm, s[2]), functools.partial(lambda i, c: (i, c), c=s[3])))
        else:
            in_specs.append(pl.BlockSpec(s[1].shape, lambda i: (0, 0)))
        args.append(s[1])
    out_shape = [jax.ShapeDtypeStruct((t_rows, w), dt) for w, dt in outs]
    out_specs = [pl.BlockSpec((tm, w), lambda i: (i, 0)) for w, _ in outs]
    out_shape += [jax.ShapeDtypeStruct(a, F32) for a in accs]
    out_specs += [pl.BlockSpec(a, lambda i: (0, 0)) for a in accs]
    n_acc = len(accs)

    def kern(*refs):
        if n_acc:
            @pl.when(pl.program_id(0) == 0)
            def _():
                for r in refs[len(refs) - n_acc:]:
                    r[...] = jnp.zeros(r.shape, F32)
        body(*refs)

    return pl.pallas_call(
        kern, name=name, grid=(t_rows // tm,), in_specs=in_specs, out_specs=out_specs, out_shape=out_shape,
        compiler_params=_cparams(("arbitrary",) if n_acc else ("parallel",)))(*args)


NCHUNK = 512


def _col_chunks(n):
    return [(c, min(NCHUNK, n - c)) for c in range(0, n, NCHUNK)]


def _mm_nt(name, pairs, out_dtype, tm=512, tn=None):
    m, n = pairs[0][0].shape[-2], pairs[0][1].shape[0]
    tn = n if tn is None else tn
    tm = min(tm, m)
    np_ = len(pairs)
    in_specs, args = [], []
    for a, w, kb, *lead in pairs:
        if lead:
            in_specs.append(pl.BlockSpec((None, tm, a.shape[2]), functools.partial(lambda j, i, ld: (ld, i, 0), ld=lead[0])))
        else:
            in_specs.append(pl.BlockSpec((tm, a.shape[1]), lambda j, i: (i, 0)))
        in_specs.append(pl.BlockSpec((tn, a.shape[-1]), functools.partial(lambda j, i, kb: (j, kb), kb=kb)))
        args += [a, w]

    def kern(*refs):
        o_ref = refs[-1]
        for c0, cw in _col_chunks(tn):
            acc = None
            for q in range(np_):
                d = _dot_nt(refs[2 * q][...], refs[2 * q + 1][c0:c0 + cw, :])
                acc = d if acc is None else acc + d
            o_ref[:, c0:c0 + cw] = acc.astype(o_ref.dtype)

    return pl.pallas_call(
        kern, name=name, grid=(n // tn, m // tm), in_specs=in_specs,
        out_specs=pl.BlockSpec((tm, tn), lambda j, i: (i, j)),
        out_shape=jax.ShapeDtypeStruct((m, n), out_dtype), compiler_params=_cparams(("parallel", "parallel")))(*args)


def _mm_tn(name, a, b, tm=None, tn=None, tk=1024, chip_cols=False):
    t, m = a.shape
    n = b.shape[-1] * (2 if b.ndim == 3 else 1)
    tm = m if tm is None else tm
    tn = n if tn is None else tn
    tk = min(tk, t)
    if b.ndim == 3:
        per = n // 2 // tn
        b_spec = pl.BlockSpec((None, tk, tn), lambda i, j, k: (j // per, k, j % per))
    else:
        b_spec = pl.BlockSpec((tk, tn), lambda i, j, k: (k, j))
    if chip_cols:
        out_spec = pl.BlockSpec((None, tm, tn), lambda i, j, k: (j, i, 0))
        out_shape = jax.ShapeDtypeStruct((n // tn, m, tn), F32)
    else:
        out_spec = pl.BlockSpec((tm, tn), lambda i, j, k: (i, j))
        out_shape = jax.ShapeDtypeStruct((m, n), F32)

    def kern(a_ref, b_ref, o_ref):
        @pl.when(pl.program_id(2) == 0)
        def _():
            o_ref[...] = jnp.zeros(o_ref.shape, F32)
        for c0, cw in _col_chunks(tn):
            o_ref[:, c0:c0 + cw] += _dot_tn(a_ref[...], b_ref[:, c0:c0 + cw])

    return pl.pallas_call(
        kern, name=name, grid=(m // tm, n // tn, t // tk),
        in_specs=[pl.BlockSpec((tk, tm), lambda i, j, k: (k, i)), b_spec], out_specs=out_spec, out_shape=out_shape,
        compiler_params=_cparams(("parallel", "parallel", "arbitrary")))(a, b)


def _rms_bwd(name, dh, x, g, dres):
    d = x.shape[1]

    def body(dh_ref, x_ref, g_ref, dres_ref, dx_ref, dxb_ref, dg_ref):
        xv, dhv = x_ref[...], dh_ref[...]
        r = lax.rsqrt(jnp.mean(xv * xv, axis=-1, keepdims=True) + EPS)
        gd = dhv * g_ref[...]
        dx = dres_ref[...] + r * gd - xv * (r * r * r * jnp.mean(xv * gd, axis=-1, keepdims=True))
        dx_ref[...] = dx
        dxb_ref[...] = dx.astype(BF16)
        dg_ref[...] += jnp.sum(dhv * xv * r, axis=0, keepdims=True)
    return _rows(name, body, [("t", dh), ("t", x), ("p", g), ("t", dres)], [(d, F32), (d, BF16)], accs=[(1, d)])


def _norm_mm(name, x, g, w, tm=512, tn=None, halves=False):
    m, k = x.shape
    n = w.shape[1]
    tn = n if tn is None else tn
    if halves:
        per = n // 2 // tn
        o_spec = pl.BlockSpec((None, tm, tn), lambda i, j: (j // per, i, j % per))
        o_shape = jax.ShapeDtypeStruct((2, m, n // 2), F32)
    else:
        o_spec = pl.BlockSpec((tm, tn), lambda i, j: (i, j))
        o_shape = jax.ShapeDtypeStruct((m, n), F32)

    def kern(x_ref, g_ref, w_ref, h_ref, o_ref):
        xv = x_ref[...]
        h = (xv * lax.rsqrt(jnp.mean(xv * xv, axis=-1, keepdims=True) + EPS) * g_ref[...]).astype(BF16)
        h_ref[...] = h
        for c0, cw in _col_chunks(tn):
            o_ref[:, c0:c0 + cw] = _dot(h, w_ref[:, c0:c0 + cw])

    return pl.pallas_call(
        kern, name=name, grid=(m // tm, n // tn),
        in_specs=[pl.BlockSpec((tm, k), lambda i, j: (i, 0)), pl.BlockSpec((1, k), lambda i, j: (0, 0)),
                  pl.BlockSpec((k, tn), lambda i, j: (0, j))],
        out_specs=[pl.BlockSpec((tm, k), lambda i, j: (i, 0)), o_spec],
        out_shape=[jax.ShapeDtypeStruct((m, k), BF16), o_shape],
        compiler_params=_cparams(("parallel", "arbitrary")))(x, g, w)


def _ple_head(x2, g, w_gate, pb, w_proj, tgt, tm=512):
    m, d = x2.shape

    def kern(x_ref, g_ref, wg_ref, p_ref, wp_ref, t_ref, h_ref, dy_ref, dgl_ref, dpp_ref, sq_ref):
        @pl.when(pl.program_id(0) == 0)
        def _():
            sq_ref[...] = jnp.zeros(sq_ref.shape, F32)
        xv = x_ref[...]
        h = (xv * lax.rsqrt(jnp.mean(xv * xv, axis=-1, keepdims=True) + EPS) * g_ref[...]).astype(BF16)
        h_ref[...] = h
        pv = p_ref[...]
        for c0, cw in _col_chunks(d):
            cs = slice(c0, c0 + cw)
            s = _sigmoid(_dot(h, wg_ref[:, cs]))
            ppv = _dot(pv, wp_ref[:, cs])
            diff = x_ref[:, cs] + s * ppv - t_ref[:, cs]
            dy = diff * (1.0 / d)
            dy_ref[:, cs] = dy
            dgl_ref[:, cs] = (dy * ppv * s * (1.0 - s)).astype(BF16)
            dpp_ref[:, cs] = (dy * s).astype(BF16)
            sq_ref[:, cs] += jnp.sum(diff * diff, axis=0, keepdims=True)

    row = lambda width: pl.BlockSpec((tm, width), lambda i: (i, 0))
    full = lambda a: pl.BlockSpec(a.shape, lambda i: (0, 0))
    return pl.pallas_call(
        kern, name="ple_head", grid=(m // tm,),
        in_specs=[row(d), full(g), full(w_gate), row(pb.shape[1]), full(w_proj), row(d)],
        out_specs=[row(d), row(d), row(d), row(d), pl.BlockSpec((1, d), lambda i: (0, 0))],
        out_shape=[jax.ShapeDtypeStruct((m, d), BF16), jax.ShapeDtypeStruct((m, d), F32), jax.ShapeDtypeStruct((m, d), BF16),
                   jax.ShapeDtypeStruct((m, d), BF16), jax.ShapeDtypeStruct((1, d), F32)],
        compiler_params=_cparams(("arbitrary",)))(x2, g, w_gate, pb, w_proj, tgt)


def _ssm_out_proj(y, proj, g, w_ssm, attn_out, w_attn, x, tm=1024):
    m, d = y.shape

    def kern(y_ref, z_ref, g_ref, ws_ref, a_ref, wa_ref, x_ref, s_ref, o_ref):
        z = z_ref[...]
        yz = y_ref[...] * (z * _sigmoid(z))
        s = (yz * lax.rsqrt(jnp.mean(yz * yz, axis=-1, keepdims=True) + EPS) * g_ref[...]).astype(BF16)
        s_ref[...] = s
        av = a_ref[...]
        for c0, cw in _col_chunks(d):
            cs = slice(c0, c0 + cw)
            o_ref[:, cs] = x_ref[:, cs] + _dot(s, ws_ref[:, cs]) + _dot(av, wa_ref[:, cs])

    row = lambda width: pl.BlockSpec((tm, width), lambda i: (i, 0))
    full = lambda a: pl.BlockSpec(a.shape, lambda i: (0, 0))
    return pl.pallas_call(
        kern, name="out_proj", grid=(m // tm,),
        in_specs=[row(d), pl.BlockSpec((tm, d), lambda i: (i, OFF_Z // SSM_INNER)), full(g), full(w_ssm), row(attn_out.shape[1]),
                  full(w_attn), row(d)],
        out_specs=[row(d), row(d)],
        out_shape=[jax.ShapeDtypeStruct((m, d), BF16), jax.ShapeDtypeStruct((m, d), F32)],
        compiler_params=_cparams(("parallel",)))(y, proj, g, w_ssm, attn_out, w_attn, x)


def _mm_nt_rms_bwd(name, a, w, x, g, dres, parts=(), tm=512):
    m, k = a.shape
    n = w.shape[0]
    ns, steps = len(parts), m // tm

    def kern(a_ref, w_ref, x_ref, g_ref, dres_ref, *rest):
        dx_ref, dxb_ref, dg_ref = rest[ns:ns + 3]
        dh = rest[2 * ns + 3]
        if ns:
            start, finish = _scatter_steps(ns, rest[:ns], rest[ns + 3:2 * ns + 3], rest[2 * ns + 4:])
            pl.when(pl.program_id(0) == 0)(start)
            pl.when(pl.program_id(0) == steps - 1)(finish)

        @pl.when(pl.program_id(0) == 0)
        def _():
            dg_ref[...] = jnp.zeros(dg_ref.shape, F32)
        av = a_ref[...]
        for c0, cw in _col_chunks(n):
            dh[:, c0:c0 + cw] = _dot_nt(av, w_ref[c0:c0 + cw, :])
        xv, dhv = x_ref[...], dh[...]
        r = lax.rsqrt(jnp.mean(xv * xv, axis=-1, keepdims=True) + EPS)
        gd = dhv * g_ref[...]
        dx = dres_ref[...] + r * gd - xv * (r * r * r * jnp.mean(xv * gd, axis=-1, keepdims=True))
        dx_ref[...] = dx
        dxb_ref[...] = dx.astype(BF16)
        dg_ref[...] += jnp.sum(dhv * xv * r, axis=0, keepdims=True)

    row = lambda width: pl.BlockSpec((tm, width), lambda i: (i, 0))
    return pl.pallas_call(
        kern, name=name, grid=(steps,),
        in_specs=[row(k), pl.BlockSpec((n, k), lambda i: (0, 0)), row(n), pl.BlockSpec((1, n), lambda i: (0, 0)), row(n)]
        + [ANY_SPEC] * ns,
        out_specs=[row(n), row(n), pl.BlockSpec((1, n), lambda i: (0, 0))] + [ANY_SPEC] * ns,
        out_shape=[jax.ShapeDtypeStruct((m, n), F32), jax.ShapeDtypeStruct((m, n), BF16), jax.ShapeDtypeStruct((1, n), F32)]
        + [jax.ShapeDtypeStruct(s.shape, s.dtype) for s in parts],
        scratch_shapes=[pltpu.VMEM((tm, n), F32)] + (_scatter_sems(ns) if ns else []),
        compiler_params=_cparams(("arbitrary",)))(a, w, x, g, dres, *parts)


def _head_mean_matrix(width):
    i = jnp.arange(width) // HEAD_DIM
    return jnp.where(i[:, None] == i[None, :], 1.0 / HEAD_DIM, 0.0).astype(BF16)


ATT_SPAN = 2048
N_QH = 8


def _lane_lo(rows):
    return lax.broadcasted_iota(jnp.int32, (rows, LANE), 1) < HEAD_DIM


def _swap_halves_lanes(x):
    return pltpu.roll(x, HEAD_DIM, axis=1)


def _qknorm_fwd2(proj, gq_t, gk_t, tm=256):
    t = proj.shape[0]
    bq, bk = _head_mean_matrix(ATTN_DIM), _head_mean_matrix(KV_DIM)
    scale = HEAD_DIM ** -0.5

    def kern(q_ref, k_ref, v_ref, gq_ref, gk_ref, bq_ref, bk_ref, qo_ref, kvo_ref):
        q, k, v = q_ref[...], k_ref[...], v_ref[...]
        qn = (q * lax.rsqrt(_dot_split(q * q, bq_ref[...]) + EPS) * gq_ref[...]) * scale
        kn = k * lax.rsqrt(_dot_split(k * k, bk_ref[...]) + EPS) * gk_ref[...]
        lo = _lane_lo(tm)
        for j in range(N_KV):
            blk = qn[:, j * LANE:(j + 1) * LANE]
            qo_ref[2 * j] = jnp.where(lo, blk, 0.0)
            qo_ref[2 * j + 1] = jnp.where(lo, _swap_halves_lanes(blk), 0.0)
        for j in range(2):
            kb, vb = kn[:, j * LANE:(j + 1) * LANE], v[:, j * LANE:(j + 1) * LANE]
            kvo_ref[2 * j] = jnp.where(lo, kb, _swap_halves_lanes(vb))
            kvo_ref[2 * j + 1] = jnp.where(lo, _swap_halves_lanes(kb), vb)

    col = lambda w, idx: pl.BlockSpec((tm, w), functools.partial(lambda i, idx: (i, idx), idx=idx))
    par = lambda a: pl.BlockSpec(a.shape, lambda i: (0, 0))
    return pl.pallas_call(
        kern, name="qknorm_fwd", grid=(t // tm,),
        in_specs=[col(ATTN_DIM, OFF_Q // ATTN_DIM), col(KV_DIM, OFF_K // KV_DIM), col(KV_DIM, OFF_V // KV_DIM),
                  par(gq_t), par(gk_t), par(bq), par(bk)],
        out_specs=[pl.BlockSpec((N_QH, tm, LANE), lambda i: (0, i, 0)), pl.BlockSpec((N_KV, tm, LANE), lambda i: (0, i, 0))],
        out_shape=[jax.ShapeDtypeStruct((N_QH, t, LANE), F32), jax.ShapeDtypeStruct((N_KV, t, LANE), F32)],
        compiler_params=_cparams(("parallel",)))(proj, proj, proj, gq_t, gk_t, bq, bk)


def _qknorm_bwd2(proj, gq_t, gk_t, dqs, dkvs, tm=256):
    t = proj.shape[0]
    bq, bk = _head_mean_matrix(ATTN_DIM), _head_mean_matrix(KV_DIM)
    scale = HEAD_DIM ** -0.5

    def kern(q_ref, k_ref, gq_ref, gk_ref, bq_ref, bk_ref, a1, a2, a3, b1, b2, b3, dq_ref, dk_ref, dv_ref, dgq_ref, dgk_ref):
        @pl.when(pl.program_id(0) == 0)
        def _():
            dgq_ref[...] = jnp.zeros(dgq_ref.shape, F32)
            dgk_ref[...] = jnp.zeros(dgk_ref.shape, F32)
        lo = _lane_lo(tm)
        sq = [a1[h] + a2[h] + a3[h] for h in range(N_QH)]
        skv = [b1[h] + b2[h] + b3[h] for h in range(N_KV)]
        dqn = jnp.concatenate([jnp.where(lo, sq[2 * j], _swap_halves_lanes(sq[2 * j + 1])) for j in range(N_KV)], axis=1) * scale
        dkn = jnp.concatenate([jnp.where(lo, skv[2 * j], _swap_halves_lanes(skv[2 * j + 1])) for j in range(2)], axis=1)
        dv = jnp.concatenate([jnp.where(lo, _swap_halves_lanes(skv[2 * j]), skv[2 * j + 1]) for j in range(2)], axis=1)
        q, k = q_ref[...], k_ref[...]
        rq = lax.rsqrt(_dot_split(q * q, bq_ref[...]) + EPS)
        rk = lax.rsqrt(_dot_split(k * k, bk_ref[...]) + EPS)
        gdq, gdk = dqn * gq_ref[...], dkn * gk_ref[...]
        dq_ref[...] = (rq * gdq - q * (rq * rq * rq * _dot_split(q * gdq, bq_ref[...]))).astype(BF16)
        dk_ref[...] = (rk * gdk - k * (rk * rk * rk * _dot_split(k * gdk, bk_ref[...]))).astype(BF16)
        dv_ref[...] = dv.astype(BF16)
        dgq_ref[...] += jnp.sum(dqn * q * rq, axis=0, keepdims=True)
        dgk_ref[...] += jnp.sum(dkn * k * rk, axis=0, keepdims=True)

    col = lambda w, idx: pl.BlockSpec((tm, w), functools.partial(lambda i, idx: (i, idx), idx=idx))
    par = lambda a: pl.BlockSpec(a.shape, lambda i: (0, 0))
    blk = lambda n: pl.BlockSpec((n, tm, LANE), lambda i: (0, i, 0))
    row = lambda w: pl.BlockSpec((tm, w), lambda i: (i, 0))
    acc = lambda w: pl.BlockSpec((1, w), lambda i: (0, 0))
    return pl.pallas_call(
        kern, name="qknorm_bwd", grid=(t // tm,),
        in_specs=[col(ATTN_DIM, OFF_Q // ATTN_DIM), col(KV_DIM, OFF_K // KV_DIM), par(gq_t), par(gk_t), par(bq), par(bk)]
        + [blk(N_QH)] * 3 + [blk(N_KV)] * 3,
        out_specs=[row(ATTN_DIM), row(KV_DIM), row(KV_DIM), acc(ATTN_DIM), acc(KV_DIM)],
        out_shape=[jax.ShapeDtypeStruct((t, ATTN_DIM), BF16), jax.ShapeDtypeStruct((t, KV_DIM), BF16),
                   jax.ShapeDtypeStruct((t, KV_DIM), BF16), jax.ShapeDtypeStruct((1, ATTN_DIM), F32),
                   jax.ShapeDtypeStruct((1, KV_DIM), F32)],
        compiler_params=_cparams(("arbitrary",)))(proj, proj, gq_t, gk_t, bq, bk, *dqs, *dkvs)


def _att_rows(b, r, dil):
    if dil == 1:
        return pl.ds(b * CHUNK, CHUNK)
    return pl.ds(b * CHUNK * dil + r, CHUNK, stride=dil)


def _for_residues(dil, unit):
    for r in range(dil):
        unit(r, 0)


def _band_qk(first):
    ri = lax.broadcasted_iota(jnp.int32, (CHUNK, 2 * CHUNK), 0)
    cj = lax.broadcasted_iota(jnp.int32, (CHUNK, 2 * CHUNK), 1)
    band = (cj - ri >= 0) & (cj - ri <= CHUNK)
    return band if first is None else band & (jnp.logical_not(first) | (cj >= CHUNK))


def _band_kq(last):
    rj = lax.broadcasted_iota(jnp.int32, (CHUNK, 2 * CHUNK), 0)
    ci = lax.broadcasted_iota(jnp.int32, (CHUNK, 2 * CHUNK), 1)
    band = (ci - rj >= 0) & (ci - rj <= CHUNK)
    return band if last is None else band & (jnp.logical_not(last) | (ci < CHUNK))


def _att_specs(t, dil):
    sub = CHUNK * dil
    nb, last = ATT_SPAN // sub, t // sub - 1
    cur = lambda heads: pl.BlockSpec((heads, ATT_SPAN, LANE), lambda kh, n: (kh, n, 0))
    prev = lambda heads: pl.BlockSpec((heads, sub, LANE), lambda kh, n: (kh, jnp.maximum(n * nb - 1, 0), 0))
    nxt = lambda heads: pl.BlockSpec((heads, sub, LANE), lambda kh, n: (kh, jnp.minimum((n + 1) * nb, last), 0))
    return sub, nb, cur, prev, nxt


def _attn_fwd2(q, kv, dil):
    t = q.shape[1]
    sub, nb, cur, prev, _ = _att_specs(t, dil)

    def kern(q_ref, kvp_ref, kvc_ref, o_ref, lse_ref):
        n = pl.program_id(1)
        lane = lax.broadcasted_iota(jnp.int32, (CHUNK, LANE), 1)
        for b in range(nb):
            mask = _band_qk((n == 0) if b == 0 else None)

            def unit(r, carry, b=b, mask=mask):
                rows = _att_rows(b, r, dil)
                kvp = kvc_ref[_att_rows(b - 1, r, dil), :] if b > 0 else kvp_ref[_att_rows(0, r, dil), :]
                kvcat = jnp.concatenate([kvp, kvc_ref[rows, :]], axis=0).astype(BF16)
                lse_tile = jnp.zeros((CHUNK, LANE), F32)
                for g in range(2):
                    s = jnp.where(mask, _dot_nt(q_ref.at[g][rows, :].astype(BF16), kvcat), NEG)
                    m = jnp.max(s, axis=1, keepdims=True)
                    p = jnp.exp(s - m)
                    l = jnp.sum(p, axis=1, keepdims=True)
                    o_ref.at[g][rows, :] = _dot(p.astype(BF16), kvcat) * (1.0 / l)
                    lse_tile = jnp.where(lane == g, m + jnp.log(l), lse_tile)
                lse_ref[rows, :] = lse_tile
                return carry
            _for_residues(dil, unit)

    return pl.pallas_call(
        kern, name=f"attn_fwd_d{dil}", grid=(N_KV, t // ATT_SPAN), in_specs=[cur(2), prev(None), cur(None)],
        out_specs=[cur(2), cur(None)],
        out_shape=[jax.ShapeDtypeStruct((N_QH, t, LANE), F32), jax.ShapeDtypeStruct((N_KV, t, LANE), F32)],
        compiler_params=_cparams(("parallel", "parallel")))(q, kv, kv)


def _attn_merge2(os_, lses, tm=256):
    t = os_[0].shape[1]

    def kern(o1, o2, o3, l1, l2, l3, out_ref, lse_ref):
        pieces = []
        for kh in range(N_KV):
            a, b, c = l1[kh], l2[kh], l3[kh]
            m = jnp.maximum(jnp.maximum(a, b), c)
            tot = m + jnp.log(jnp.exp(a - m) + jnp.exp(b - m) + jnp.exp(c - m))
            lse_ref[kh] = tot
            wa, wb, wc = jnp.exp(a - tot), jnp.exp(b - tot), jnp.exp(c - tot)
            for g in range(2):
                h = 2 * kh + g
                acc = wa[:, g:g + 1] * o1[h] + wb[:, g:g + 1] * o2[h] + wc[:, g:g + 1] * o3[h]
                pieces.append(acc[:, HEAD_DIM:])
        out_ref[...] = jnp.concatenate(pieces, axis=1).astype(BF16)

    blk = lambda n: pl.BlockSpec((n, tm, LANE), lambda i: (0, i, 0))
    return pl.pallas_call(
        kern, name="attn_merge", grid=(t // tm,), in_specs=[blk(N_QH)] * 3 + [blk(N_KV)] * 3,
        out_specs=[pl.BlockSpec((tm, ATTN_DIM), lambda i: (i, 0)), blk(N_KV)],
        out_shape=[jax.ShapeDtypeStruct((t, ATTN_DIM), BF16), jax.ShapeDtypeStruct((N_KV, t, LANE), F32)],
        compiler_params=_cparams(("parallel",)))(*os_, *lses)


def _attn_bwd_prep2(dmix, attn_out, tm=256):
    t = attn_out.shape[0]

    def kern(do_ref, o_ref, dot_ref, d_ref):
        do = do_ref[...]
        prod = do * o_ref[...].astype(F32)
        lo = _lane_lo(tm)
        lane = lax.broadcasted_iota(jnp.int32, (tm, LANE), 1)
        for kh in range(N_KV):
            blk, pb = do[:, kh * LANE:(kh + 1) * LANE], prod[:, kh * LANE:(kh + 1) * LANE]
            dot_ref[2 * kh] = jnp.where(lo, 0.0, _swap_halves_lanes(blk))
            dot_ref[2 * kh + 1] = jnp.where(lo, 0.0, blk)
            s_lo = jnp.sum(jnp.where(lo, pb, 0.0), axis=1, keepdims=True)
            s_hi = jnp.sum(pb, axis=1, keepdims=True) - s_lo
            d_ref[kh] = jnp.where(lane == 0, s_lo, jnp.where(lane == 1, s_hi, 0.0))

    blk = lambda n: pl.BlockSpec((n, tm, LANE), lambda i: (0, i, 0))
    return pl.pallas_call(
        kern, name="attn_bwd_prep", grid=(t // tm,),
        in_specs=[pl.BlockSpec((tm, ATTN_DIM), lambda i: (i, SSM_INNER // ATTN_DIM)), pl.BlockSpec((tm, ATTN_DIM), lambda i: (i, 0))],
        out_specs=[blk(N_QH), blk(N_KV)],
        out_shape=[jax.ShapeDtypeStruct((N_QH, t, LANE), F32), jax.ShapeDtypeStruct((N_KV, t, LANE), F32)],
        compiler_params=_cparams(("parallel",)))(dmix, attn_out)


def _attn_dq2(q, kv, dot, lse, dsum, dil):
    t = q.shape[1]
    sub, nb, cur, prev, _ = _att_specs(t, dil)

    def kern(q_ref, kvp_ref, kvc_ref, do_ref, lse_ref, d_ref, dq_ref):
        n = pl.program_id(1)
        for b in range(nb):
            mask = _band_qk((n == 0) if b == 0 else None)

            def unit(r, carry, b=b, mask=mask):
                rows = _att_rows(b, r, dil)
                kvp = kvc_ref[_att_rows(b - 1, r, dil), :] if b > 0 else kvp_ref[_att_rows(0, r, dil), :]
                kvcat = jnp.concatenate([kvp, kvc_ref[rows, :]], axis=0).astype(BF16)
                lse_t, d_t = lse_ref[rows, :], d_ref[rows, :]
                for g in range(2):
                    s = jnp.where(mask, _dot_nt(q_ref.at[g][rows, :].astype(BF16), kvcat), NEG)
                    p = jnp.exp(s - lse_t[:, g:g + 1])
                    dp = _dot_nt(do_ref.at[g][rows, :].astype(BF16), kvcat)
                    ds = p * (dp - d_t[:, g:g + 1])
                    dq_ref.at[g][rows, :] = _dot(ds.astype(BF16), kvcat)
                return carry
            _for_residues(dil, unit)

    return pl.pallas_call(
        kern, name=f"attn_dq_d{dil}", grid=(N_KV, t // ATT_SPAN),
        in_specs=[cur(2), prev(None), cur(None), cur(2), cur(None), cur(None)], out_specs=cur(2),
        out_shape=jax.ShapeDtypeStruct((N_QH, t, LANE), F32),
        compiler_params=_cparams(("parallel", "parallel")))(q, kv, kv, dot, lse, dsum)


def _attn_dkv2(q, kv, dot, lse, dsum, dil):
    t = q.shape[1]
    sub, nb, cur, _, nxt = _att_specs(t, dil)
    nsteps = t // ATT_SPAN

    def kern(kv_ref, qc_ref, qn_ref, doc_ref, don_ref, lc_ref, ln_ref, dc_ref, dn_ref, dkv_ref):
        n = pl.program_id(1)
        for b in range(nb):
            inside = b < nb - 1
            mask = _band_kq(None if inside else (n == nsteps - 1))

            def unit(r, carry, b=b, inside=inside, mask=mask):
                rows = _att_rows(b, r, dil)
                nrows = _att_rows(b + 1, r, dil) if inside else _att_rows(0, r, dil)
                kvb = kv_ref[rows, :].astype(BF16)
                follow = lambda cref, nref: (cref if inside else nref)[nrows, :]
                lse_t = jnp.concatenate([lc_ref[rows, :].T, follow(lc_ref, ln_ref).T], axis=1)
                d_t = jnp.concatenate([dc_ref[rows, :].T, follow(dc_ref, dn_ref).T], axis=1)
                acc = jnp.zeros((CHUNK, LANE), F32)
                for g in range(2):
                    qdo = jnp.concatenate([qc_ref.at[g][rows, :], follow(qc_ref.at[g], qn_ref.at[g]),
                                           doc_ref.at[g][rows, :], follow(doc_ref.at[g], don_ref.at[g])], axis=0).astype(BF16)
                    both = _dot_nt(kvb, qdo)
                    pt = jnp.exp(jnp.where(mask, both[:, :2 * CHUNK], NEG) - lse_t[g:g + 1, :])
                    dst = pt * (both[:, 2 * CHUNK:] - d_t[g:g + 1, :])
                    acc = acc + _dot(jnp.concatenate([dst, pt], axis=1).astype(BF16), qdo)
                dkv_ref[rows, :] = acc
                return carry
            _for_residues(dil, unit)

    return pl.pallas_call(
        kern, name=f"attn_dkv_d{dil}", grid=(N_KV, nsteps),
        in_specs=[cur(None), cur(2), nxt(2), cur(2), nxt(2), cur(None), nxt(None), cur(None), nxt(None)], out_specs=cur(None),
        out_shape=jax.ShapeDtypeStruct((N_KV, t, LANE), F32),
        compiler_params=_cparams(("parallel", "parallel")))(kv, q, q, dot, dot, lse, lse, dsum, dsum)


HALO = 8
SSM_CONV_TM, SSM_CONV_W = 512, 512
FFN_CONV_TM, FFN_CONV_W = 256, 1408


def _halo_specs(tm, width, t_rows, col_off=0, lead=None):
    per, last = tm // HALO, t_rows // HALO - 1
    row_maps = (lambda i: i, lambda i: jnp.maximum(i * per - 1, 0), lambda i: jnp.minimum((i + 1) * per, last))
    specs = []
    for rows, rm in zip((tm, HALO, HALO), row_maps):
        if lead is None:
            specs.append(pl.BlockSpec((rows, width), functools.partial(lambda c, i, rm: (rm(i), c + col_off), rm=rm)))
        else:
            specs.append(pl.BlockSpec((None, rows, width), functools.partial(lambda c, i, rm: (lead, rm(i), c + col_off), rm=rm)))
    return specs


def _fill_ext(buf, tile_ref, before_ref, after_ref, i, nt):
    tm = tile_ref.shape[0]
    buf[0:HALO, :] = jnp.where(i > 0, before_ref[...].astype(F32), 0.0)
    buf[HALO:HALO + tm, :] = tile_ref[...].astype(F32)
    if after_ref is not None:
        buf[HALO + tm:, :] = jnp.where(i < nt - 1, after_ref[...].astype(F32), 0.0)


CONV_RB, CONV_CW = 16, 256


def _lane_chunks(width):
    return [slice(c0, min(c0 + CONV_CW, width)) for c0 in range(0, width, CONV_CW)]


def _shifted(buf, taps, r0, rows, cs):
    return [buf[pl.ds(HALO - (taps - 1) + k + r0, rows), cs] for k in range(taps)]


def _taps_fwd(xs, w, b):
    acc = b
    for k, xk in enumerate(xs):
        acc = acc + w[k:k + 1, :] * xk
    return acc


def _taps_bwd(bufd, w, taps, r0, rows, cs):
    acc = None
    for k in range(taps):
        term = w[k:k + 1, :] * bufd[pl.ds(r0 + (taps - 1) - k, rows), cs]
        acc = term if acc is None else acc + term
    return acc


def _fold8(z):
    return z[:HALO] + z[HALO:] if z.shape[0] == 2 * HALO else z


def _silu_grad(pre):
    sg = _sigmoid(pre)
    return sg * (1.0 + pre * (1.0 - sg))


def _ssm_conv_fwd(proj, w, b):
    t = proj.shape[0]
    tm, wd = min(SSM_CONV_TM, t), SSM_CONV_W
    nt, taps = t // tm, w.shape[0]

    def kern(x_ref, xb_ref, w_ref, b_ref, o_ref, buf):
        _fill_ext(buf, x_ref, xb_ref, None, pl.program_id(1), nt)
        for cs in _lane_chunks(wd):
            wv, bv = w_ref[:, cs], b_ref[:, cs]
            for r0 in range(0, tm, CONV_RB):
                pre = _taps_fwd(_shifted(buf, taps, r0, CONV_RB, cs), wv, bv)
                o_ref[r0:r0 + CONV_RB, cs] = pre * _sigmoid(pre)

    tile, before, _ = _halo_specs(tm, wd, t)
    par = lambda rows: pl.BlockSpec((rows, wd), lambda c, i: (0, c))
    return pl.pallas_call(
        kern, name="ssm_conv_fwd", grid=(XBC_DIM // wd, nt), in_specs=[tile, before, par(taps), par(1)],
        out_specs=pl.BlockSpec((tm, wd), lambda c, i: (i, c)), out_shape=jax.ShapeDtypeStruct((t, XBC_DIM), F32),
        scratch_shapes=[pltpu.VMEM((tm + HALO, wd), F32)],
        compiler_params=_cparams(("parallel", "parallel")))(proj, proj, w, b)


def _ssm_conv_bwd(proj, w, b, dact, parts):
    t = proj.shape[0]
    tm, wd = min(SSM_CONV_TM, t), SSM_CONV_W
    nt, taps, ncol, ns = t // tm, w.shape[0], XBC_DIM // SSM_CONV_W, len(parts)

    def kern(x_ref, xb_ref, xa_ref, d_ref, dn_ref, w_ref, b_ref, *rest):
        dx_ref, gw_ref, gb_ref = rest[ns:ns + 3]
        buf, bufd = rest[2 * ns + 3:2 * ns + 5]
        i = pl.program_id(1)
        if ns:
            start, finish = _scatter_steps(ns, rest[:ns], rest[ns + 3:2 * ns + 3], rest[2 * ns + 5:])
            pl.when((pl.program_id(0) == 0) & (i == 0))(start)
            pl.when((pl.program_id(0) == ncol - 1) & (i == nt - 1))(finish)
        _fill_ext(buf, x_ref, xb_ref, xa_ref, i, nt)

        @pl.when(i == 0)
        def _():
            gw_ref[...] = jnp.zeros(gw_ref.shape, F32)
            gb_ref[...] = jnp.zeros(gb_ref.shape, F32)
        for cs in _lane_chunks(wd):
            wv, bv = w_ref[:, cs], b_ref[:, cs]
            acc = [jnp.zeros((HALO, cs.stop - cs.start), F32) for _ in range(taps + 1)]
            for r0 in list(range(0, tm, CONV_RB)) + [tm]:
                inside = r0 < tm
                rows = CONV_RB if inside else HALO
                xs = _shifted(buf, taps, r0, rows, cs)
                d = d_ref[r0:r0 + rows, cs] if inside else jnp.where(i < nt - 1, dn_ref[:, cs], 0.0)
                dpre = d * _silu_grad(_taps_fwd(xs, wv, bv))
                bufd[r0:r0 + rows, cs] = dpre
                if inside:
                    acc[taps] = acc[taps] + _fold8(dpre)
                    for k in range(taps):
                        acc[k] = acc[k] + _fold8(dpre * xs[k])
            gb_ref[:, cs] += jnp.sum(acc[taps], axis=0, keepdims=True)
            for k in range(taps):
                gw_ref[k:k + 1, cs] += jnp.sum(acc[k], axis=0, keepdims=True)
            for r0 in range(0, tm, CONV_RB):
                dx_ref[r0:r0 + CONV_RB, cs] = _taps_bwd(bufd, wv, taps, r0, CONV_RB, cs).astype(BF16)

    xt, xb, xa = _halo_specs(tm, wd, t)
    dt_, _, dn = _halo_specs(tm, wd, t)
    par = lambda rows: pl.BlockSpec((rows, wd), lambda c, i: (0, c))
    return pl.pallas_call(
        kern, name="ssm_conv_bwd", grid=(ncol, nt), in_specs=[xt, xb, xa, dt_, dn, par(taps), par(1)] + [ANY_SPEC] * ns,
        out_specs=[pl.BlockSpec((tm, wd), lambda c, i: (i, c)), par(taps), par(1)] + [ANY_SPEC] * ns,
        out_shape=[jax.ShapeDtypeStruct((t, XBC_DIM), BF16), jax.ShapeDtypeStruct((taps, XBC_DIM), F32),
                   jax.ShapeDtypeStruct((1, XBC_DIM), F32)] + [jax.ShapeDtypeStruct(s.shape, s.dtype) for s in parts],
        scratch_shapes=[pltpu.VMEM((tm + 2 * HALO, wd), F32), pltpu.VMEM((tm + HALO, wd), F32)] + (_scatter_sems(ns) if ns else []),
        compiler_params=_cparams(("arbitrary", "arbitrary")))(proj, proj, proj, dact, dact, w, b, *parts)


def _ffn_act_down(u, w, b, w_down, x1):
    t = u.shape[1]
    tm, wd = min(FFN_CONV_TM, t), D_FF
    nt, taps = t // tm, w.shape[0]

    def kern(g_ref, gb_ref, v_ref, vb_ref, wg_ref, wv_ref, bg_ref, bv_ref, wd_ref, x1_ref, a_ref, x2_ref, bufg, bufv):
        i = pl.program_id(1)
        _fill_ext(bufg, g_ref, gb_ref, None, i, nt)
        _fill_ext(bufv, v_ref, vb_ref, None, i, nt)
        acc = x1_ref[...]
        for cs in _lane_chunks(wd):
            wg, wv, bg, bv = wg_ref[:, cs], wv_ref[:, cs], bg_ref[:, cs], bv_ref[:, cs]
            for r0 in range(0, tm, CONV_RB):
                g = _taps_fwd(_shifted(bufg, taps, r0, CONV_RB, cs), wg, bg)
                v = _taps_fwd(_shifted(bufv, taps, r0, CONV_RB, cs), wv, bv)
                a_ref[r0:r0 + CONV_RB, cs] = (g * _sigmoid(g) * v).astype(BF16)
            acc = acc + _dot(a_ref[:, cs], wd_ref[cs, :])
        x2_ref[...] = acc

    gt, gbf, _ = _halo_specs(tm, wd, t, lead=0)
    vt, vbf, _ = _halo_specs(tm, wd, t, lead=1)
    par = lambda rows, off: pl.BlockSpec((rows, wd), functools.partial(lambda c, i, off: (0, c + off), off=off))
    row = lambda width: pl.BlockSpec((tm, width), lambda c, i: (i, 0))
    return pl.pallas_call(
        kern, name="ffn_act_down", grid=(1, nt),
        in_specs=[gt, gbf, vt, vbf, par(taps, 0), par(taps, 1), par(1, 0), par(1, 1),
                  pl.BlockSpec(w_down.shape, lambda c, i: (0, 0)), row(D_MODEL)],
        out_specs=[row(wd), row(D_MODEL)],
        out_shape=[jax.ShapeDtypeStruct((t, D_FF), BF16), jax.ShapeDtypeStruct((t, D_MODEL), F32)],
        scratch_shapes=[pltpu.VMEM((tm + HALO, wd), F32)] * 2,
        compiler_params=_cparams(("parallel", "parallel")))(u, u, u, u, w, w, b, b, w_down, x1)


def _ffn_act_bwd(u, w, b, da):
    t = u.shape[1]
    tm, wd = min(FFN_CONV_TM, t), FFN_CONV_W
    nt, taps, nc = t // tm, w.shape[0], D_FF // FFN_CONV_W

    def kern(g_ref, gb_ref, ga_ref, v_ref, vb_ref, va_ref, d_ref, dn_ref, wg_ref, wv_ref, bg_ref, bv_ref,
             du_ref, gwg_ref, gwv_ref, gbg_ref, gbv_ref, bufg, bufv, bufdg, bufdv):
        i = pl.program_id(1)
        _fill_ext(bufg, g_ref, gb_ref, ga_ref, i, nt)
        _fill_ext(bufv, v_ref, vb_ref, va_ref, i, nt)

        @pl.when(i == 0)
        def _():
            for r in (gwg_ref, gwv_ref, gbg_ref, gbv_ref):
                r[...] = jnp.zeros(r.shape, F32)
        for cs in _lane_chunks(wd):
            wg, wv, bg, bv = wg_ref[:, cs], wv_ref[:, cs], bg_ref[:, cs], bv_ref[:, cs]
            zero = jnp.zeros((HALO, cs.stop - cs.start), F32)
            accg, accv = [zero] * (taps + 1), [zero] * (taps + 1)
            for r0 in list(range(0, tm, CONV_RB)) + [tm]:
                inside = r0 < tm
                rows = CONV_RB if inside else HALO
                xg, xv = _shifted(bufg, taps, r0, rows, cs), _shifted(bufv, taps, r0, rows, cs)
                g, v = _taps_fwd(xg, wg, bg), _taps_fwd(xv, wv, bv)
                dav = d_ref[r0:r0 + rows, cs] if inside else jnp.where(i < nt - 1, dn_ref[:, cs], 0.0)
                sg = _sigmoid(g)
                dg = dav * v * (sg * (1.0 + g * (1.0 - sg)))
                dv = dav * (g * sg)
                bufdg[r0:r0 + rows, cs] = dg
                bufdv[r0:r0 + rows, cs] = dv
                if inside:
                    accg[taps], accv[taps] = accg[taps] + _fold8(dg), accv[taps] + _fold8(dv)
                    for k in range(taps):
                        accg[k], accv[k] = accg[k] + _fold8(dg * xg[k]), accv[k] + _fold8(dv * xv[k])
            gbg_ref[:, cs] += jnp.sum(accg[taps], axis=0, keepdims=True)
            gbv_ref[:, cs] += jnp.sum(accv[taps], axis=0, keepdims=True)
            for k in range(taps):
                gwg_ref[k:k + 1, cs] += jnp.sum(accg[k], axis=0, keepdims=True)
                gwv_ref[k:k + 1, cs] += jnp.sum(accv[k], axis=0, keepdims=True)
            for r0 in range(0, tm, CONV_RB):
                du_ref[0, r0:r0 + CONV_RB, cs] = _taps_bwd(bufdg, wg, taps, r0, CONV_RB, cs).astype(BF16)
                du_ref[1, r0:r0 + CONV_RB, cs] = _taps_bwd(bufdv, wv, taps, r0, CONV_RB, cs).astype(BF16)

    gt, gbf, gaf = _halo_specs(tm, wd, t, lead=0)
    vt, vbf, vaf = _halo_specs(tm, wd, t, lead=1)
    dt_, _, dn = _halo_specs(tm, wd, t)
    par = lambda rows, off: pl.BlockSpec((rows, wd), functools.partial(lambda c, i, off: (0, c + off), off=off))
    return pl.pallas_call(
        kern, name="ffn_act_bwd", grid=(nc, nt),
        in_specs=[gt, gbf, gaf, vt, vbf, vaf, dt_, dn, par(taps, 0), par(taps, nc), par(1, 0), par(1, nc)],
        out_specs=[pl.BlockSpec((2, tm, wd), lambda c, i: (0, i, c)), par(taps, 0), par(taps, 0), par(1, 0), par(1, 0)],
        out_shape=[jax.ShapeDtypeStruct((2, t, D_FF), BF16)] + [jax.ShapeDtypeStruct((taps, D_FF), F32)] * 2
        + [jax.ShapeDtypeStruct((1, D_FF), F32)] * 2,
        scratch_shapes=[pltpu.VMEM((tm + 2 * HALO, wd), F32)] * 2 + [pltpu.VMEM((tm + HALO, wd), F32)] * 2,
        compiler_params=_cparams(("parallel", "arbitrary")))(u, u, u, u, u, u, da, da, w, w, b, b)


def _softplus(x):
    e = jnp.exp(-jnp.abs(x))
    return jnp.maximum(x, 0.0) + jnp.where(e < 1e-4, e - 0.5 * e * e, jnp.log(1.0 + e))


def _tri(lower):
    r = lax.broadcasted_iota(jnp.int32, (CHUNK, CHUNK), 0)
    c = lax.broadcasted_iota(jnp.int32, (CHUNK, CHUNK), 1)
    return (r >= c) if lower else (r <= c)


def _cum(mat_bool, x):
    return jnp.dot(mat_bool.astype(F32), x, precision=lax.Precision.HIGHEST, preferred_element_type=F32)


def _pair_sel(lane_lo, tile, h0):
    return jnp.where(lane_lo, tile[:, h0:h0 + 1], tile[:, h0 + 1:h0 + 2])


def _ssd_fwd(xbc_act, proj, dt_bias_p, a_log_p, dskip_t, shards):
    t = xbc_act.shape[0]
    nch = t // CHUNK
    ns = len(shards)

    def kern(xa_ref, dtr_ref, bias_ref, alog_ref, dsk_ref, *rest):
        y_ref, dt_ref, hs_ref = rest[ns:ns + 3]
        hst = rest[2 * ns + 3]
        if ns:
            start, forward, finish = _gather_steps(shards, rest[:ns], rest[ns + 3:2 * ns + 3], rest[2 * ns + 4:])
            pl.when(pl.program_id(0) == 0)(start)
            pl.when(pl.program_id(0) == (3 * nch) // 4)(forward)
            pl.when(pl.program_id(0) == nch - 1)(finish)

        @pl.when(pl.program_id(0) == 0)
        def _():
            hst[...] = jnp.zeros(hst.shape, F32)
        dt = _softplus(dtr_ref[...] + bias_ref[...])
        dt_ref[...] = dt
        acum = _cum(_tri(True), dt * (-jnp.exp(alog_ref[...])))
        acum_t = acum.T
        ea = jnp.exp(acum)
        a_last = acum[CHUNK - 1:CHUNK, :]
        dend = jnp.exp(a_last - acum)
        ea_last = jnp.exp(a_last)
        causal = _tri(True)
        lane_lo = lax.broadcasted_iota(jnp.int32, (CHUNK, LANE), 1) < HEAD_DIM
        row_lo = lax.broadcasted_iota(jnp.int32, (CHUNK, LANE), 0) < HEAD_DIM
        for g in range(2):
            bg = xa_ref[:, SSM_INNER + g * SSM_STATE:SSM_INNER + (g + 1) * SSM_STATE].astype(BF16)
            cg = xa_ref[:, SSM_INNER + BC_DIM + g * SSM_STATE:SSM_INNER + BC_DIM + (g + 1) * SSM_STATE].astype(BF16)
            cb = _dot_nt(cg, bg)
            for j in range(4 * g, 4 * g + 4):
                h0 = 2 * j
                cols = slice(j * LANE, (j + 1) * LANE)
                xp = xa_ref[:, cols]
                xdt = xp * _pair_sel(lane_lo, dt, h0)
                ydiag = None
                for hh, sel in ((h0, lane_lo), (h0 + 1, ~lane_lo)):
                    seg = acum[:, hh:hh + 1] - acum_t[hh:hh + 1, :]
                    mm = (cb * jnp.where(causal, jnp.exp(jnp.minimum(seg, 0.0)), 0.0)).astype(BF16)
                    d = _dot(mm, jnp.where(sel, xdt, 0.0).astype(BF16))
                    ydiag = d if ydiag is None else ydiag + d
                hp = hst[cols, :]
                hs_ref[cols, :] = hp
                yoff = _dot_nt(cg, hp.astype(BF16)) * _pair_sel(lane_lo, ea, h0)
                y_ref[:, cols] = ydiag + yoff + dsk_ref[:, cols] * xp
                xw = (xdt * _pair_sel(lane_lo, dend, h0)).astype(BF16)
                rowf = jnp.where(row_lo, ea_last[:, h0:h0 + 1], ea_last[:, h0 + 1:h0 + 2])
                hst[cols, :] = hp * rowf + _dot_tn(xw, bg)

    return pl.pallas_call(
        kern, name="ssd_fwd", grid=(nch,),
        in_specs=[pl.BlockSpec((CHUNK, XBC_DIM), lambda c: (c, 0)), pl.BlockSpec((CHUNK, LANE), lambda c: (c, OFF_DT // LANE)),
                  pl.BlockSpec((1, LANE), lambda c: (0, 0)), pl.BlockSpec((1, LANE), lambda c: (0, 0)),
                  pl.BlockSpec((1, SSM_INNER), lambda c: (0, 0))] + [ANY_SPEC] * ns,
        out_specs=[pl.BlockSpec((CHUNK, SSM_INNER), lambda c: (c, 0)), pl.BlockSpec((CHUNK, LANE), lambda c: (c, 0)),
                   pl.BlockSpec((None, SSM_INNER, SSM_STATE), lambda c: (c, 0, 0))] + [ANY_SPEC] * ns,
        out_shape=[jax.ShapeDtypeStruct((t, SSM_INNER), F32), jax.ShapeDtypeStruct((t, LANE), F32),
                   jax.ShapeDtypeStruct((nch, SSM_INNER, SSM_STATE), F32)] + _gather_out_shapes(shards),
        scratch_shapes=[pltpu.VMEM((SSM_INNER, SSM_STATE), F32)] + (_gather_sems(ns) if ns else []),
        compiler_params=_cparams(("arbitrary",)))(xbc_act, proj, dt_bias_p, a_log_p, dskip_t, *shards)


def _ssd_bwd(xbc_act, proj, dt_sp, hstates, dy, dt_bias_p, a_log_p, dskip_t, swaps):
    t = xbc_act.shape[0]
    nch = t // CHUNK
    ns = len(swaps)

    pair = jnp.arange(SSM_HEADS // 2)[:, None, None]
    psel = (jnp.arange(LANE)[None, None, :] == 2 * pair + (jnp.arange(LANE) // HEAD_DIM)[None, :, None]).astype(BF16)

    def kern(xa_ref, dtr_ref, dt_ref, hs_ref, dy_ref, bias_ref, alog_ref, dsk_ref, psel_ref, *rest):
        dact_ref, ddtr_ref, da_ref, dbias_ref, ddsk_ref = rest[ns:ns + 5]
        dh = rest[2 * ns + 5]
        if ns:
            start, finish = _swap_steps(swaps, rest[:ns], rest[ns + 5:2 * ns + 5], rest[2 * ns + 6:])
            pl.when(pl.program_id(0) == 0)(start)
            pl.when(pl.program_id(0) == nch - 1)(finish)

        @pl.when(pl.program_id(0) == 0)
        def _():
            dh[...] = jnp.zeros(dh.shape, F32)
            for r in (da_ref, dbias_ref, ddsk_ref):
                r[...] = jnp.zeros(r.shape, F32)
        dt = dt_ref[...]
        a_neg = -jnp.exp(alog_ref[...])
        acum = _cum(_tri(True), dt * a_neg)
        acum_t = acum.T
        ea = jnp.exp(acum)
        a_last = acum[CHUNK - 1:CHUNK, :]
        dend = jnp.exp(a_last - acum)
        ea_last = jnp.exp(a_last)
        causal = _tri(True)
        lane = lax.broadcasted_iota(jnp.int32, (CHUNK, LANE), 1)
        rowi = lax.broadcasted_iota(jnp.int32, (CHUNK, LANE), 0)
        lane_lo, row_lo, last_row = lane < HEAD_DIM, rowi < HEAD_DIM, rowi == CHUNK - 1
        d_dt = jnp.zeros((CHUNK, LANE), F32)
        d_acum = jnp.zeros((CHUNK, LANE), F32)
        for g in range(2):
            bcols = slice(SSM_INNER + g * SSM_STATE, SSM_INNER + (g + 1) * SSM_STATE)
            ccols = slice(SSM_INNER + BC_DIM + g * SSM_STATE, SSM_INNER + BC_DIM + (g + 1) * SSM_STATE)
            bg, cg = xa_ref[:, bcols].astype(BF16), xa_ref[:, ccols].astype(BF16)
            cb = _dot_nt(cg, bg)
            dg_sum = jnp.zeros((CHUNK, CHUNK), F32)
            dcg = jnp.zeros((CHUNK, SSM_STATE), F32)
            dbg = jnp.zeros((CHUNK, SSM_STATE), F32)
            for j in range(4 * g, 4 * g + 4):
                h0 = 2 * j
                cols = slice(j * LANE, (j + 1) * LANE)
                xp, dyp = xa_ref[:, cols], dy_ref[:, cols]
                dtsel = _pair_sel(lane_lo, dt, h0)
                xdt = xp * dtsel
                xdt_b = xdt.astype(BF16)
                hp, dhp = hs_ref[cols, :], dh[cols, :]
                hp_b, dhp_b = hp.astype(BF16), dhp.astype(BF16)
                easel, dendsel = _pair_sel(lane_lo, ea, h0), _pair_sel(lane_lo, dend, h0)
                dx, ydiag = None, None
                for hh, sel in ((h0, lane_lo), (h0 + 1, ~lane_lo)):
                    dyh = jnp.where(sel, dyp, 0.0).astype(BF16)
                    seg = acum[:, hh:hh + 1] - acum_t[hh:hh + 1, :]
                    dec = jnp.where(causal, jnp.exp(jnp.minimum(seg, 0.0)), 0.0)
                    mm_b = (cb * dec).astype(BF16)
                    dg_sum = dg_sum + dec * _dot_nt(dyh, xdt_b)
                    d = _dot_tn(mm_b, dyh)
                    y = _dot(mm_b, jnp.where(sel, xdt, 0.0).astype(BF16))
                    dx = d if dx is None else dx + d
                    ydiag = y if ydiag is None else ydiag + y
                g2 = _dot_nt(bg, dhp_b)
                tprod = xdt * g2 * dendsel
                yoff = _dot_nt(cg, hp_b) * easel
                yc = dyp.astype(BF16).astype(F32) * ydiag + dyp * yoff - (xdt_b.astype(F32) * dx + tprod)
                dx = dx + g2 * dendsel
                psel = psel_ref[j]
                t_lo = jnp.sum(jnp.where(lane_lo, tprod, 0.0), keepdims=True).reshape(1, 1)
                t_hi = jnp.sum(tprod, keepdims=True).reshape(1, 1) - t_lo
                hh_prod = dhp * hp
                s_lo = jnp.sum(jnp.where(row_lo, hh_prod, 0.0), keepdims=True).reshape(1, 1)
                s_hi = jnp.sum(hh_prod, keepdims=True).reshape(1, 1) - s_lo
                end_lo = ea_last[:, h0:h0 + 1] * s_lo + t_lo
                end_hi = ea_last[:, h0 + 1:h0 + 2] * s_hi + t_hi
                ends = jnp.where(lane == h0, end_lo, jnp.where(lane == h0 + 1, end_hi, 0.0))
                d_acum = d_acum + _dot_split(yc, psel) + jnp.where(last_row, ends, 0.0)
                dye = (dyp * easel).astype(BF16)
                dcg = dcg + _dot(dye, hp_b)
                dbg = dbg + _dot((xdt * dendsel).astype(BF16), dhp_b)
                rowf = jnp.where(row_lo, ea_last[:, h0:h0 + 1], ea_last[:, h0 + 1:h0 + 2])
                dh[cols, :] = dhp * rowf + _dot_tn(dye, cg)
                dact_ref[:, cols] = dx * dtsel + dsk_ref[:, cols] * dyp
                d_dt = d_dt + _dot_split(dx * xp, psel)
                ddsk_ref[:, cols] += jnp.sum(dyp * xp, axis=0, keepdims=True)
            dg_b = dg_sum.astype(BF16)
            dact_ref[:, ccols] = dcg + _dot(dg_b, bg)
            dact_ref[:, bcols] = dbg + _dot_tn(dg_b, cg)
        d_adt = _cum(_tri(False), d_acum)
        d_dt = d_dt + d_adt * a_neg
        da_ref[...] += jnp.sum(d_adt * dt, axis=0, keepdims=True)
        d_raw = jnp.where(lane < SSM_HEADS, d_dt * _sigmoid(dtr_ref[...] + bias_ref[...]), 0.0)
        ddtr_ref[...] = d_raw.astype(BF16)
        dbias_ref[...] += jnp.sum(d_raw, axis=0, keepdims=True)

    rev = lambda c: (nch - 1 - c, 0)
    return pl.pallas_call(
        kern, name="ssd_bwd", grid=(nch,),
        in_specs=[pl.BlockSpec((CHUNK, XBC_DIM), rev), pl.BlockSpec((CHUNK, LANE), lambda c: (nch - 1 - c, OFF_DT // LANE)),
                  pl.BlockSpec((CHUNK, LANE), rev), pl.BlockSpec((None, SSM_INNER, SSM_STATE), lambda c: (nch - 1 - c, 0, 0)),
                  pl.BlockSpec((CHUNK, SSM_INNER), rev),
                  pl.BlockSpec((1, LANE), lambda c: (0, 0)), pl.BlockSpec((1, LANE), lambda c: (0, 0)),
                  pl.BlockSpec((1, SSM_INNER), lambda c: (0, 0)), pl.BlockSpec(psel.shape, lambda c: (0, 0, 0))] + [ANY_SPEC] * ns,
        out_specs=[pl.BlockSpec((CHUNK, XBC_DIM), rev), pl.BlockSpec((CHUNK, LANE), rev),
                   pl.BlockSpec((1, LANE), lambda c: (0, 0)), pl.BlockSpec((1, LANE), lambda c: (0, 0)),
                   pl.BlockSpec((1, SSM_INNER), lambda c: (0, 0))] + [ANY_SPEC] * ns,
        out_shape=[jax.ShapeDtypeStruct((t, XBC_DIM), F32), jax.ShapeDtypeStruct((t, LANE), BF16),
                   jax.ShapeDtypeStruct((1, LANE), F32), jax.ShapeDtypeStruct((1, LANE), F32),
                   jax.ShapeDtypeStruct((1, SSM_INNER), F32)] + _swap_out_shapes(swaps),
        scratch_shapes=[pltpu.VMEM((SSM_INNER, SSM_STATE), F32)] + (_swap_sems(ns) if ns else []),
        compiler_params=_cparams(("arbitrary",)))(xbc_act, proj, dt_sp, hstates, dy, dt_bias_p, a_log_p, dskip_t, psel, *swaps)


def _ssm_post_bwd(dmix, y, proj, g):
    def body(do_ref, y_ref, z_ref, g_ref, dy_ref, dz_ref, dg_ref):
        z, yv, dout = z_ref[...], y_ref[...], do_ref[...]
        sg = _sigmoid(z)
        gz = z * sg
        yz = yv * gz
        r = lax.rsqrt(jnp.mean(yz * yz, axis=-1, keepdims=True) + EPS)
        gd = dout * g_ref[...]
        dyz = r * gd - yz * (r * r * r * jnp.mean(yz * gd, axis=-1, keepdims=True))
        dy_ref[...] = dyz * gz
        dz_ref[...] = (dyz * yv * (sg * (1.0 + z * (1.0 - sg)))).astype(BF16)
        dg_ref[...] += jnp.sum(dout * yz * r, axis=0, keepdims=True)
    return _rows("ssm_post_bwd", body,
                 [("tc", dmix, SSM_INNER, 0), ("t", y), ("tc", proj, SSM_INNER, OFF_Z // SSM_INNER), ("p", g)],
                 [(SSM_INNER, F32), (SSM_INNER, BF16)], accs=[(1, SSM_INNER)])


def _pad_lanes(v, width=LANE):
    return jnp.pad(v, ((0, 0), (0, width - v.shape[1])))


def _local_step(x, p, tgt, wts, late_shards=(), join_late=None, reduce_early=None, reduce_late=None):
    g_attn, g_ssm, g_ffn, g_ple = wts["attn_norm_g"], wts["ssm_norm_g"], wts["ffn_norm_g"], wts["ple_norm_g"]
    w_in_p = wts["w_in_p"]
    gq_t = jnp.tile(wts["q_norm_g"], (1, ATTN_DIM // HEAD_DIM))
    gk_t = jnp.tile(wts["k_norm_g"], (1, KV_DIM // HEAD_DIM))
    dt_bias_p, a_log_p = _pad_lanes(wts["dt_bias"]), _pad_lanes(wts["a_log"])
    dskip_t = jnp.repeat(wts["d_skip"], HEAD_DIM, axis=1)

    h1, proj = _norm_mm("in_proj", x, g_attn, w_in_p)
    q_hm, kv_hm = _qknorm_fwd2(proj, gq_t, gk_t)
    pats = [_attn_fwd2(q_hm, kv_hm, d) for d in DILATIONS]
    attn_out, lse = _attn_merge2([o for o, _ in pats], [l for _, l in pats])
    xbc_act = _ssm_conv_fwd(proj, wts["ssm_conv_w"], wts["ssm_conv_b"])
    y_ssd, dt_sp, hstates, *gathered = _ssd_fwd(xbc_act, proj, dt_bias_p, a_log_p, dskip_t, list(late_shards))
    if join_late is not None:
        wts = {**wts, **join_late(gathered)}
    w_out_s, w_out_a = wts["w_out_ssm"], wts["w_out_attn"]
    w_up, w_down, w_gate, w_proj = wts["w_up"], wts["w_down"], wts["w_ple_gate"], wts["w_ple_proj"]
    ssm_out, x1 = _ssm_out_proj(y_ssd, proj, g_ssm, w_out_s, attn_out, w_out_a, x)
    h2, u = _norm_mm("ffn_up", x1, g_ffn, w_up, tm=1024, tn=1408, halves=True)
    a, x2 = _ffn_act_down(u, wts["ffn_conv_w"], wts["ffn_conv_b"], w_down, x1)
    pb = p.astype(BF16)
    h3, dy, dgl, dpp, sq = _ple_head(x2, g_ple, w_gate, pb, w_proj, tgt)

    grads = {}
    grads["w_ple_proj"] = _mm_tn("g_ple_proj", pb, dpp, tn=PLE_DIM, chip_cols=True)
    grads["w_ple_gate"] = _mm_tn("g_ple_gate", h3, dgl)
    dx2, dx2b, grads["ple_norm_g"] = _mm_nt_rms_bwd("d_h3", dgl, w_gate, x2, g_ple, dy)
    da = _mm_nt("d_ffn_act", [(dx2b, w_down, 0)], F32, tm=1024, tn=1408)
    grads["w_down"] = _mm_tn("g_ffn_down", a, dx2b, tm=1408)
    du, gwg, gwv, gbg, gbv = _ffn_act_bwd(u, wts["ffn_conv_w"], wts["ffn_conv_b"], da)
    grads["ffn_conv_w"] = jnp.concatenate([gwg, gwv], axis=1)
    grads["ffn_conv_b"] = jnp.concatenate([gbg, gbv], axis=1)
    grads["w_up"] = _mm_tn("g_ffn_up", h2, du, tn=1408, chip_cols=True)
    dh2 = _mm_nt("d_h2", [(du, w_up, 0, 0), (du, w_up, 1, 1)], F32, tm=1024, tn=512)
    dx1, dx1b, grads["ffn_norm_g"] = _rms_bwd("rms_ffn_bwd", dh2, x1, g_ffn, dx2)
    dmix = _mm_nt("d_mix", [(dx1b, jnp.concatenate([w_out_s, w_out_a], axis=0), 0)], F32, tm=1024)
    grads["w_out"] = jnp.concatenate([_mm_tn("g_out_attn", attn_out, dx1b), _mm_tn("g_out_ssm", ssm_out, dx1b)], axis=0)
    dy_ssd, dz, grads["ssm_norm_g"] = _ssm_post_bwd(dmix, y_ssd, proj, g_ssm)
    early_major = reduce_early[0](grads) if reduce_early is not None else []
    dact, ddtr, d_a, d_bias, d_dsk, *early_got = _ssd_bwd(xbc_act, proj, dt_sp, hstates, dy_ssd, dt_bias_p, a_log_p, dskip_t,
                                                           early_major)
    grads["dt_bias"] = d_bias[:, :SSM_HEADS]
    grads["a_log"] = d_a[:, :SSM_HEADS] * (-jnp.exp(wts["a_log"]))
    grads["d_skip"] = jnp.sum(d_dsk.reshape(SSM_HEADS, HEAD_DIM), axis=1)[None, :]
    chip_sums = reduce_early[1](early_major, early_got) if reduce_early is not None else []
    dxbc, grads["ssm_conv_w"], grads["ssm_conv_b"], *scattered = _ssm_conv_bwd(proj, wts["ssm_conv_w"], wts["ssm_conv_b"], dact,
                                                                                chip_sums)
    do_hm, dsum = _attn_bwd_prep2(dmix, attn_out)
    dqs = [_attn_dq2(q_hm, kv_hm, do_hm, lse, dsum, d) for d in DILATIONS]
    dkvs = [_attn_dkv2(q_hm, kv_hm, do_hm, lse, dsum, d) for d in DILATIONS]
    dq, dk, dv, dgq, dgk = _qknorm_bwd2(proj, gq_t, gk_t, dqs, dkvs)
    grads["q_norm_g"] = jnp.sum(dgq.reshape(ATTN_DIM // HEAD_DIM, HEAD_DIM), axis=0)[None, :]
    grads["k_norm_g"] = jnp.sum(dgk.reshape(KV_DIM // HEAD_DIM, HEAD_DIM), axis=0)[None, :]
    dproj = jnp.concatenate([dxbc, dq, dz, dk, dv, ddtr], axis=1)
    grads["w_in_p"] = _mm_tn("g_in_proj", h1, dproj, tm=512)
    late_sums = reduce_late(grads) if reduce_late is not None else []
    grad_x, _, grads["attn_norm_g"], *late_scattered = _mm_nt_rms_bwd("d_h1", dproj, w_in_p, x, g_attn, dx1, late_sums)
    return sq, grad_x, grads, (chip_sums, scattered), (late_sums, late_scattered)


MESH_IDS = pl.DeviceIdType.MESH
N_CHIPS = 4
ANY_SPEC = pl.BlockSpec(memory_space=pl.ANY)
SMALL_ROWS = 96
ALL_SMALL_ROWS = 272


def _place():
    x, y, c = lax.axis_index("x"), lax.axis_index("y"), lax.axis_index("c")
    return x, y, c, [(1 - x, y), (x, 1 - y), (1 - x, 1 - y)]


def _gather_over_chips(arrs):
    n = len(arrs)

    def body(*refs):
        steps = _gather_steps(arrs, refs[:n], refs[n:2 * n], refs[2 * n:2 * n + 4])
        for step in steps:
            step()

    return pl.pallas_call(
        body, name="gather_weights", in_specs=[ANY_SPEC] * n, out_specs=[ANY_SPEC] * n,
        out_shape=_gather_out_shapes(arrs), scratch_shapes=_gather_sems(n))(*arrs)


def _gather_out_shapes(arrs):
    return [jax.ShapeDtypeStruct((N_CHIPS,) + a.shape, a.dtype) for a in arrs]


def _gather_sems(n):
    return [pltpu.SemaphoreType.DMA((3 * n,))] * 4


def _gather_steps(arrs, ins, outs, sems):
    n = len(arrs)
    split = [a.shape[0] % 64 == 0 for a in arrs]
    ici_send, ici_recv, d2d_send, d2d_recv = sems

    def place():
        x, y, c, chips = _place()
        return x, y, c, chips, 2 * x + y

    def part(ref, a, core):
        if not split[a]:
            return ref
        half = arrs[a].shape[0] // 2
        return ref.at[pl.ds(core * half, half)]

    def ici(a, k, slot, where):
        x, y, c, chips, _ = where
        px, py = chips[k]
        return pltpu.make_async_remote_copy(
            src_ref=part(ins[a], a, c), dst_ref=part(outs[a].at[slot], a, c), send_sem=ici_send.at[3 * a + k],
            recv_sem=ici_recv.at[3 * a + k], device_id=(px, py, c), device_id_type=MESH_IDS)

    def d2d(a, k, core, where):
        x, y, c, chips, _ = where
        px, py = chips[k]
        piece = part(outs[a].at[2 * px + py], a, core)
        return pltpu.make_async_remote_copy(src_ref=piece, dst_ref=piece, send_sem=d2d_send.at[3 * a + k],
                                            recv_sem=d2d_recv.at[3 * a + k], device_id=(x, y, 1 - c), device_id_type=MESH_IDS)

    def start():
        w = place()
        for a in range(n):
            for k in range(3):
                ici(a, k, w[4], w).start()

    def forward():
        w = place()
        for a in range(n):
            for k, (px, py) in enumerate(w[3]):
                ici(a, k, 2 * px + py, w).wait_recv()
                if split[a]:
                    d2d(a, k, w[2], w).start()

    def finish():
        w = place()
        for a in range(n):
            for k in range(3):
                if split[a]:
                    d2d(a, k, 1 - w[2], w).wait_recv()
                    d2d(a, k, w[2], w).wait_send()
                ici(a, k, w[4], w).wait_send()

    return start, forward, finish


def _row_tile(rows, cap=256):
    return max(d for d in range(8, cap + 1, 8) if rows % d == 0)


def _swap_halves(name, gs):
    n = len(gs)

    def body(*refs):
        for step in _swap_steps(gs, refs[:n], refs[n:2 * n], refs[2 * n:2 * n + 2]):
            step()

    return pl.pallas_call(
        body, name=name, in_specs=[ANY_SPEC] * n, out_specs=[ANY_SPEC] * n, out_shape=_swap_out_shapes(gs),
        scratch_shapes=_swap_sems(n))(*gs)


def _swap_out_shapes(gs):
    return [jax.ShapeDtypeStruct((N_CHIPS, g.shape[1] // 2, g.shape[2]), g.dtype) for g in gs]


def _swap_sems(n):
    return [pltpu.SemaphoreType.DMA((N_CHIPS * n,))] * 2


def _swap_steps(gs, ins, outs, sems):
    send, recv = sems

    def copies():
        x, y, c, _ = _place()
        cps = []
        for a in range(len(gs)):
            half = gs[a].shape[1] // 2
            for q in range(N_CHIPS):
                cps.append(pltpu.make_async_remote_copy(
                    src_ref=ins[a].at[q, pl.ds((1 - c) * half, half)], dst_ref=outs[a].at[q], send_sem=send.at[N_CHIPS * a + q],
                    recv_sem=recv.at[N_CHIPS * a + q], device_id=(x, y, 1 - c), device_id_type=MESH_IDS))
        return cps

    def start():
        for cp in copies():
            cp.start()

    def finish():
        for cp in copies():
            cp.wait()

    return start, finish


def _add_halves(name, g, got, c_idx):
    rows, cols = got.shape[1:]
    tm = _row_tile(rows)
    per = rows // tm

    def kern(c_ref, g_ref, r_ref, o_ref):
        o_ref[...] = (g_ref[...] + r_ref[...]).astype(BF16)

    return pl.pallas_call(
        kern, name=name,
        grid_spec=pltpu.PrefetchScalarGridSpec(
            num_scalar_prefetch=1, grid=(N_CHIPS, per),
            in_specs=[pl.BlockSpec((None, tm, cols), lambda q, i, c_ref: (q, c_ref[0] * per + i, 0)),
                      pl.BlockSpec((None, tm, cols), lambda q, i, c_ref: (q, i, 0))],
            out_specs=pl.BlockSpec((None, tm, cols), lambda q, i, c_ref: (q, i, 0))),
        out_shape=jax.ShapeDtypeStruct((N_CHIPS, rows, cols), BF16),
        compiler_params=_cparams(("parallel", "parallel")))(c_idx, g, got)


def _scatter_sems(n):
    return [pltpu.SemaphoreType.DMA((3 * n,))] * 2


def _scatter_steps(n, ins, outs, sems):
    send, recv = sems

    def copy(a, k, slot, where):
        x, y, c, chips = where
        px, py = chips[k]
        return pltpu.make_async_remote_copy(src_ref=ins[a].at[2 * px + py], dst_ref=outs[a].at[slot], send_sem=send.at[3 * a + k],
                                            recv_sem=recv.at[3 * a + k], device_id=(px, py, c), device_id_type=MESH_IDS)

    def start():
        w = _place()
        for a in range(n):
            for k in range(3):
                copy(a, k, 2 * w[0] + w[1], w).start()

    def finish():
        w = _place()
        for a in range(n):
            for k, (px, py) in enumerate(w[3]):
                copy(a, k, 2 * px + py, w).wait()

    return start, finish


def _sum_chips(name, own, parts, idx):
    rows, cols = parts.shape[1:]
    tm = _row_tile(rows)
    per = rows // tm

    def kern(o_idx, a_ref, b_ref, c_ref, d_ref, o_ref):
        o_ref[...] = ((a_ref[...].astype(F32) + b_ref[...].astype(F32)) + c_ref[...].astype(F32)) + d_ref[...].astype(F32)

    def spec(k):
        return pl.BlockSpec((None, tm, cols), functools.partial(lambda i, o_idx, k: (o_idx[k], i, 0), k=k))

    return pl.pallas_call(
        kern, name=name,
        grid_spec=pltpu.PrefetchScalarGridSpec(
            num_scalar_prefetch=1, grid=(per,), in_specs=[spec(0), spec(1), spec(2), spec(3)],
            out_specs=pl.BlockSpec((None, tm, cols), lambda i, o_idx: (0, o_idx[4] * per + i, 0))),
        out_shape=jax.ShapeDtypeStruct((1, 2 * rows, cols), F32), compiler_params=_cparams(("parallel",)))(idx, own, parts, parts, parts)


def _share_with_sibling(gs):
    n = len(gs)

    def body(*refs):
        ins, send, recv = refs[:n], refs[2 * n], refs[2 * n + 1]
        x, y, c, _ = _place()
        cps = []
        for a in range(n):
            half = gs[a].shape[1] // 2
            mine = pl.ds(c * half, half)
            cps.append(pltpu.make_async_remote_copy(src_ref=ins[a].at[0, mine], dst_ref=refs[n + a].at[0, mine], send_sem=send.at[a],
                                                    recv_sem=recv.at[a], device_id=(x, y, 1 - c), device_id_type=MESH_IDS))
        for cp in cps:
            cp.start()
        for cp in cps:
            cp.wait()

    return pl.pallas_call(
        body, name="grad_share_sibling", in_specs=[ANY_SPEC] * n, out_specs=[ANY_SPEC] * n,
        out_shape=[jax.ShapeDtypeStruct(g.shape, g.dtype) for g in gs], input_output_aliases={a: a for a in range(n)},
        scratch_shapes=[pltpu.SemaphoreType.DMA((n,))] * 2)(*gs)


def _allreduce_small(v):
    def body(v_ref, o_ref, land, send, recv):
        x, y, c, _ = _place()
        me = 4 * x + 2 * y + c
        land[me] = v_ref[...]
        cps = []
        for rel in range(1, 8):
            bx, by, bc = (rel >> 2) & 1, (rel >> 1) & 1, rel & 1
            peer = (1 - x if bx else x, 1 - y if by else y, 1 - c if bc else c)
            cps.append(pltpu.make_async_remote_copy(src_ref=v_ref, dst_ref=land.at[me], send_sem=send.at[rel - 1],
                                                    recv_sem=recv.at[rel - 1], device_id=peer, device_id_type=MESH_IDS))
        for cp in cps:
            cp.start()
        for cp in cps:
            cp.wait()
        acc = land[0]
        for d in range(1, 8):
            acc = acc + land[d]
        o_ref[...] = acc

    vm = pl.BlockSpec(memory_space=pltpu.VMEM)
    return pl.pallas_call(
        body, name="allreduce_small", in_specs=[vm], out_specs=vm, out_shape=jax.ShapeDtypeStruct(v.shape, F32),
        scratch_shapes=[pltpu.VMEM((8,) + v.shape, F32), pltpu.SemaphoreType.DMA((7,)), pltpu.SemaphoreType.DMA((7,))])(v)


def _adamw(name, w, g, m, v):
    _, rows, cols = w.shape
    tm = rows if rows * cols <= 128 * 1024 else _row_tile(rows, max(256, 2048 * LANE // cols))
    c1 = 1.0 / (1.0 - ADAM_B1 ** ADAM_STEP)
    c2 = 1.0 / (1.0 - ADAM_B2 ** ADAM_STEP)

    def kern(w_ref, g_ref, m_ref, v_ref, d_ref, mo_ref, vo_ref):
        gv = g_ref[...]
        mn = ADAM_B1 * m_ref[...] + (1.0 - ADAM_B1) * gv
        vn = ADAM_B2 * v_ref[...] + (1.0 - ADAM_B2) * (gv * gv)
        d_ref[...] = -ADAM_LR * ((mn * c1) / (jnp.sqrt(vn * c2) + ADAM_EPS) + ADAM_WD * w_ref[...])
        mo_ref[...] = mn
        vo_ref[...] = vn

    spec = pl.BlockSpec((None, tm, cols), lambda i: (0, i, 0))
    return pl.pallas_call(
        kern, name=name, grid=(rows // tm,), in_specs=[spec] * 4, out_specs=[spec] * 3,
        out_shape=[jax.ShapeDtypeStruct(w.shape, F32)] * 3, compiler_params=_cparams(("parallel",)))(w, g, m, v)


SHARDED = (("w_in", 1), ("w_out", 0), ("w_up", 1), ("w_down", 0), ("w_ple_gate", 0), ("w_ple_proj", 1),
           ("ssm_conv_w", 1), ("ffn_conv_w", 1))
MATRICES = ("w_in", "w_out", "w_up", "w_down", "w_ple_gate", "w_ple_proj")
EARLY_REDUCED = MATRICES[1:]
REPLICATED = ("attn_norm_g", "q_norm_g", "k_norm_g", "ssm_conv_b", "dt_bias", "a_log", "d_skip", "ssm_norm_g",
              "ffn_norm_g", "ffn_conv_b", "ple_norm_g")
WEIGHT_ORDER = ("attn_norm_g", "w_in", "q_norm_g", "k_norm_g", "ssm_conv_w", "ssm_conv_b", "dt_bias", "a_log", "d_skip",
                "ssm_norm_g", "w_out", "ffn_norm_g", "w_up", "ffn_conv_w", "ffn_conv_b", "w_down", "ple_norm_g",
                "w_ple_gate", "w_ple_proj")


def _join_chips(g, axis):
    if axis == 0:
        return g.reshape(g.shape[0] * g.shape[1], g.shape[2])
    return jnp.transpose(g, (1, 0, 2)).reshape(g.shape[1], g.shape[0] * g.shape[2])


def _split_chips(g, axis):
    if axis == 0:
        return g.reshape(N_CHIPS, g.shape[0] // N_CHIPS, g.shape[1])
    r, c = g.shape
    return jnp.transpose(g.reshape(r, N_CHIPS, c // N_CHIPS), (1, 0, 2))


def _pack_small(vals, rows=SMALL_ROWS):
    flat = jnp.concatenate([v.reshape(-1) for v in vals])
    return jnp.pad(flat, (0, rows * LANE - flat.shape[0])).reshape(rows, LANE)


def _unpack_small(packed, like):
    flat, out, off = packed.reshape(-1), [], 0
    for v in like:
        out.append(flat[off:off + v.size].reshape(v.shape))
        off += v.size
    return out


def kernel(x, p, attn_norm_g, w_in, q_norm_g, k_norm_g, ssm_conv_w, ssm_conv_b, dt_bias, a_log, d_skip, ssm_norm_g, w_out, ffn_norm_g, w_up, ffn_conv_w, ffn_conv_b, w_down, ple_norm_g, w_ple_gate, w_ple_proj, loss_target, m_attn_norm_g, m_w_in, m_q_norm_g, m_k_norm_g, m_ssm_conv_w, m_ssm_conv_b, m_dt_bias, m_a_log, m_d_skip, m_ssm_norm_g, m_w_out, m_ffn_norm_g, m_w_up, m_ffn_conv_w, m_ffn_conv_b, m_w_down, m_ple_norm_g, m_w_ple_gate, m_w_ple_proj, v_attn_norm_g, v_w_in, v_q_norm_g, v_k_norm_g, v_ssm_conv_w, v_ssm_conv_b, v_dt_bias, v_a_log, v_d_skip, v_ssm_norm_g, v_w_out, v_ffn_norm_g, v_w_up, v_ffn_conv_w, v_ffn_conv_b, v_w_down, v_ple_norm_g, v_w_ple_gate, v_w_ple_proj):
    given = dict(locals())
    w2 = {n: given[n].reshape(given[n].shape[-2:]) if given[n].ndim == 3 else given[n] for n in WEIGHT_ORDER}

    cx, cy, cc = lax.axis_index("x"), lax.axis_index("y"), lax.axis_index("c")
    chip = 2 * cx + cy
    axis_of = dict(SHARDED)
    shard = lambda n: w2[n].astype(BF16) if n in MATRICES else w2[n]
    join = lambda n, g: _join_chips(lax.dynamic_update_index_in_dim(g, shard(n), chip, 0), axis_of[n])
    first = ("w_in", "ssm_conv_w", "ffn_conv_w")
    full = {n: join(n, g) for n, g in zip(first, _gather_over_chips([shard(n) for n in first]))}
    win = full["w_in"]
    w_in_p = jnp.concatenate([win[:, 2048:3584], win[:, 0:512], win[:, 1024:2048], win[:, 512:768], win[:, 768:1024],
                              win[:, 3584:3600], jnp.zeros((D_MODEL, PROJ_P - IN_PROJ), BF16)], axis=1)
    wts = {n: w2[n] for n in REPLICATED}
    wts.update(w_in_p=w_in_p, ssm_conv_w=full["ssm_conv_w"], ffn_conv_w=full["ffn_conv_w"])

    def join_late(gathered):
        late = {n: join(n, g) for n, g in zip(EARLY_REDUCED, gathered)}
        return dict(w_out_attn=late["w_out"][:ATTN_DIM], w_out_ssm=late["w_out"][ATTN_DIM:], w_up=late["w_up"],
                    w_down=late["w_down"], w_ple_gate=late["w_ple_gate"], w_ple_proj=late["w_ple_proj"])

    core = cc.astype(jnp.int32).reshape(1)
    idx = jnp.stack([chip, 2 * (1 - cx) + cy, 2 * cx + (1 - cy), 2 * (1 - cx) + (1 - cy), cc]).astype(jnp.int32)

    def major_of(names, gd):
        return [gd[n] if gd[n].ndim == 3 else _split_chips(gd[n], axis_of[n]) for n in names]

    def sums_of(names, major, got):
        return [_add_halves("grad_add_halves_" + n, g, r, core) for n, g, r in zip(names, major, got)]

    def w_in_sums(gd):
        gi = gd["w_in_p"]
        gd["w_in"] = jnp.concatenate([gi[:, OFF_Q:OFF_Q + ATTN_DIM], gi[:, OFF_K:OFF_K + KV_DIM], gi[:, OFF_V:OFF_V + KV_DIM],
                                      gi[:, OFF_Z:OFF_Z + SSM_INNER], gi[:, OFF_XBC:OFF_XBC + XBC_DIM], gi[:, OFF_DT:OFF_DT + SSM_HEADS]],
                                     axis=1)
        major = major_of(("w_in",), gd)
        return sums_of(("w_in",), major, _swap_halves("grad_swap_halves_late", major))

    sq, grad_x, grads, early, late = _local_step(
        x[0], p[0, 0], loss_target[0], wts, [shard(n) for n in EARLY_REDUCED], join_late,
        (functools.partial(major_of, EARLY_REDUCED), functools.partial(sums_of, EARLY_REDUCED)), w_in_sums)
    sums = dict(zip(EARLY_REDUCED + ("w_in",), list(zip(*early)) + list(zip(*late))))
    halves = [_sum_chips("grad_sum_chips_" + n, *sums[n], idx) for n in MATRICES]
    g_shard = dict(zip(MATRICES, _share_with_sibling(halves)))

    small_names = REPLICATED + ("ssm_conv_w", "ffn_conv_w")
    small_like = [grads[n] for n in small_names] + [jnp.zeros((1,), F32)]
    small = _allreduce_small(_pack_small([grads[n] for n in small_names] + [jnp.sum(sq).reshape(1)], ALL_SMALL_ROWS))
    small_vals = dict(zip(small_names + ("loss",), _unpack_small(small, small_like)))
    loss = (0.5 / D_MODEL) * small_vals["loss"][0]
    for n in ("ssm_conv_w", "ffn_conv_w"):
        cols = w2[n].shape[1]
        g_shard[n] = lax.dynamic_slice_in_dim(small_vals[n], chip * cols, cols, axis=1)[None]

    delta, new_m, new_v = {}, {}, {}
    for n, _ in SHARDED:
        if n == "w_in":
            r, c = w2[n].shape
            flat = lambda a: jnp.transpose(a.reshape(r, c)).reshape(1, r * c // LANE, LANE)
            back = lambda a: jnp.transpose(a.reshape(c, r)).reshape(1, r, c)
            outs = _adamw("adamw_" + n, flat(given[n]), flat(g_shard[n]), flat(given["m_" + n]), flat(given["v_" + n]))
            delta[n], new_m[n], new_v[n] = [back(o) for o in outs]
            continue
        delta[n], new_m[n], new_v[n] = _adamw("adamw_" + n, given[n], g_shard[n], given["m_" + n], given["v_" + n])
    packed = lambda prefix: _pack_small([given[prefix + n] for n in REPLICATED])[None]
    sm = _adamw("adamw_small", packed(""), _pack_small([small_vals[n] for n in REPLICATED])[None], packed("m_"), packed("v_"))
    for n in REPLICATED:
        g_shard[n] = small_vals[n]
    for dst, packed_out in zip((delta, new_m, new_v), sm):
        for n, val in zip(REPLICATED, _unpack_small(packed_out[0], [w2[n] for n in REPLICATED])):
            dst[n] = val

    def shaped(d):
        return [d[n].reshape(given[n].shape) for n in WEIGHT_ORDER]
    return (loss, grad_x[None], *shaped(g_shard), *shaped(delta), *shaped(new_m), *shaped(new_v))
```

```python
import functools

import jax
import jax.numpy as jnp
from jax import lax
from jax.experimental import pallas as pl
from jax.experimental.pallas import tpu as pltpu

F32 = jnp.float32
BF16 = jnp.bfloat16

D_MODEL = 1024
HEAD_DIM = 64
ATTN_DIM = 512
KV_DIM = 256
N_KV = 4
SSM_INNER = 1024
SSM_HEADS = 16
SSM_STATE = 128
BC_DIM = 256
XBC_DIM = SSM_INNER + 2 * BC_DIM
MIX_DIM = ATTN_DIM + SSM_INNER
IN_PROJ = 3600
D_FF = 2816
PLE_DIM = 256
CHUNK = 128
DILATIONS = (1, 4, 16)
EPS = 1e-6
ADAM_LR, ADAM_B1, ADAM_B2, ADAM_EPS, ADAM_WD, ADAM_STEP = 0.001, 0.9, 0.999, 1e-08, 0.01, 10

PROJ_P = 3712
OFF_XBC, OFF_Q, OFF_Z, OFF_K, OFF_V, OFF_DT = 0, 1536, 2048, 3072, 3328, 3584
LANE = 128
VMEM_LIMIT = 48 * 1024 * 1024
NEG = -1e30


def _cparams(sem):
    return pltpu.CompilerParams(dimension_semantics=sem, vmem_limit_bytes=VMEM_LIMIT)


def _sigmoid(x):
    return 1.0 / (1.0 + jnp.exp(-x))


def _dot(a, b):
    return jnp.dot(a, b, preferred_element_type=F32)


def _dot_nt(a, b):
    return lax.dot_general(a, b, (((1,), (1,)), ((), ())), preferred_element_type=F32)


def _dot_tn(a, b):
    return lax.dot_general(a, b, (((0,), (0,)), ((), ())), preferred_element_type=F32)


def _dot_split(x, m):
    hi = x.astype(BF16)
    lo = (x - hi.astype(F32)).astype(BF16)
    return _dot(hi, m) + _dot(lo, m)


def _rows(name, body, ins, outs, accs=(), tm=512):
    t_rows = next(s[1].shape[0] for s in ins if s[0] in ("t", "tc"))
    tm = min(tm, t_rows)
    in_specs, args = [], []
    for s in ins:
        if s[0] == "t":
            in_specs.append(pl.BlockSpec((tm, s[1].shape[1]), lambda i: (i, 0)))
        elif s[0] == "tc":
            in_specs.append(pl.BlockSpec((tm, s[2]), functools.partial(lambda i, c: (i, c), c=s[3])))
        else:
            in_specs.append(pl.BlockSpec(s[1].shape, lambda i: (0, 0)))
        args.append(s[1])
    out_shape = [jax.ShapeDtypeStruct((t_rows, w), dt) for w, dt in outs]
    out_specs = [pl.BlockSpec((tm, w), lambda i: (i, 0)) for w, _ in outs]
    out_shape += [jax.ShapeDtypeStruct(a, F32) for a in accs]
    out_specs += [pl.BlockSpec(a, lambda i: (0, 0)) for a in accs]
    n_acc = len(accs)

    def kern(*refs):
        if n_acc:
            @pl.when(pl.program_id(0) == 0)
            def _():
                for r in refs[len(refs) - n_acc:]:
                    r[...] = jnp.zeros(r.shape, F32)
        body(*refs)

    return pl.pallas_call(
        kern, name=name, grid=(t_rows // tm,), in_specs=in_specs, out_specs=out_specs, out_shape=out_shape,
        compiler_params=_cparams(("arbitrary",) if n_acc else ("parallel",)))(*args)


NCHUNK = 512


def _col_chunks(n):
    return [(c, min(NCHUNK, n - c)) for c in range(0, n, NCHUNK)]


def _mm_nt(name, pairs, out_dtype, tm=512, tn=None):
    m, n = pairs[0][0].shape[-2], pairs[0][1].shape[0]
    tn = n if tn is None else tn
    tm = min(tm, m)
    np_ = len(pairs)
    in_specs, args = [], []
    for a, w, kb, *lead in pairs:
        if lead:
            in_specs.append(pl.BlockSpec((None, tm, a.shape[2]), functools.partial(lambda j, i, ld: (ld, i, 0), ld=lead[0])))
        else:
            in_specs.append(pl.BlockSpec((tm, a.shape[1]), lambda j, i: (i, 0)))
        in_specs.append(pl.BlockSpec((tn, a.shape[-1]), functools.partial(lambda j, i, kb: (j, kb), kb=kb)))
        args += [a, w]

    def kern(*refs):
        o_ref = refs[-1]
        for c0, cw in _col_chunks(tn):
            acc = None
            for q in range(np_):
                d = _dot_nt(refs[2 * q][...], refs[2 * q + 1][c0:c0 + cw, :])
                acc = d if acc is None else acc + d
            o_ref[:, c0:c0 + cw] = acc.astype(o_ref.dtype)

    return pl.pallas_call(
        kern, name=name, grid=(n // tn, m // tm), in_specs=in_specs,
        out_specs=pl.BlockSpec((tm, tn), lambda j, i: (i, j)),
        out_shape=jax.ShapeDtypeStruct((m, n), out_dtype), compiler_params=_cparams(("parallel", "parallel")))(*args)


def _mm_tn(name, a, b, tm=None, tn=None, tk=1024, chip_cols=False):
    t, m = a.shape
    n = b.shape[-1] * (2 if b.ndim == 3 else 1)
    tm = m if tm is None else tm
    tn = n if tn is None else tn
    tk = min(tk, t)
    if b.ndim == 3:
        per = n // 2 // tn
        b_spec = pl.BlockSpec((None, tk, tn), lambda i, j, k: (j // per, k, j % per))
    else:
        b_spec = pl.BlockSpec((tk, tn), lambda i, j, k: (k, j))
    if chip_cols:
        out_spec = pl.BlockSpec((None, tm, tn), lambda i, j, k: (j, i, 0))
        out_shape = jax.ShapeDtypeStruct((n // tn, m, tn), F32)
    else:
        out_spec = pl.BlockSpec((tm, tn), lambda i, j, k: (i, j))
        out_shape = jax.ShapeDtypeStruct((m, n), F32)

    def kern(a_ref, b_ref, o_ref):
        @pl.when(pl.program_id(2) == 0)
        def _():
            o_ref[...] = jnp.zeros(o_ref.shape, F32)
        for c0, cw in _col_chunks(tn):
            o_ref[:, c0:c0 + cw] += _dot_tn(a_ref[...], b_ref[:, c0:c0 + cw])

    return pl.pallas_call(
        kern, name=name, grid=(m // tm, n // tn, t // tk),
        in_specs=[pl.BlockSpec((tk, tm), lambda i, j, k: (k, i)), b_spec], out_specs=out_spec, out_shape=out_shape,
        compiler_params=_cparams(("parallel", "parallel", "arbitrary")))(a, b)


def _rms_bwd(name, dh, x, g, dres):
    d = x.shape[1]

    def body(dh_ref, x_ref, g_ref, dres_ref, dx_ref, dxb_ref, dg_ref):
        xv, dhv = x_ref[...], dh_ref[...]
        r = lax.rsqrt(jnp.mean(xv * xv, axis=-1, keepdims=True) + EPS)
        gd = dhv * g_ref[...]
        dx = dres_ref[...] + r * gd - xv * (r * r * r * jnp.mean(xv * gd, axis=-1, keepdims=True))
        dx_ref[...] = dx
        dxb_ref[...] = dx.astype(BF16)
        dg_ref[...] += jnp.sum(dhv * xv * r, axis=0, keepdims=True)
    return _rows(name, body, [("t", dh), ("t", x), ("p", g), ("t", dres)], [(d, F32), (d, BF16)], accs=[(1, d)])


def _norm_mm(name, x, g, w, tm=512, tn=None, halves=False):
    m, k = x.shape
    n = w.shape[1]
    tn = n if tn is None else tn
    if halves:
        per = n // 2 // tn
        o_spec = pl.BlockSpec((None, tm, tn), lambda i, j: (j // per, i, j % per))
        o_shape = jax.ShapeDtypeStruct((2, m, n // 2), F32)
    else:
        o_spec = pl.BlockSpec((tm, tn), lambda i, j: (i, j))
        o_shape = jax.ShapeDtypeStruct((m, n), F32)

    def kern(x_ref, g_ref, w_ref, h_ref, o_ref):
        xv = x_ref[...]
        h = (xv * lax.rsqrt(jnp.mean(xv * xv, axis=-1, keepdims=True) + EPS) * g_ref[...]).astype(BF16)
        h_ref[...] = h
        for c0, cw in _col_chunks(tn):
            o_ref[:, c0:c0 + cw] = _dot(h, w_ref[:, c0:c0 + cw])

    return pl.pallas_call(
        kern, name=name, grid=(m // tm, n // tn),
        in_specs=[pl.BlockSpec((tm, k), lambda i, j: (i, 0)), pl.BlockSpec((1, k), lambda i, j: (0, 0)),
                  pl.BlockSpec((k, tn), lambda i, j: (0, j))],
        out_specs=[pl.BlockSpec((tm, k), lambda i, j: (i, 0)), o_spec],
        out_shape=[jax.ShapeDtypeStruct((m, k), BF16), o_shape],
        compiler_params=_cparams(("parallel", "arbitrary")))(x, g, w)


def _ple_head(x2, g, w_gate, pb, w_proj, tgt, tm=512):
    m, d = x2.shape

    def kern(x_ref, g_ref, wg_ref, p_ref, wp_ref, t_ref, h_ref, dy_ref, dgl_ref, dpp_ref, sq_ref):
        @pl.when(pl.program_id(0) == 0)
        def _():
            sq_ref[...] = jnp.zeros(sq_ref.shape, F32)
        xv = x_ref[...]
        h = (xv * lax.rsqrt(jnp.mean(xv * xv, axis=-1, keepdims=True) + EPS) * g_ref[...]).astype(BF16)
        h_ref[...] = h
        pv = p_ref[...]
        for c0, cw in _col_chunks(d):
            cs = slice(c0, c0 + cw)
            s = _sigmoid(_dot(h, wg_ref[:, cs]))
            ppv = _dot(pv, wp_ref[:, cs])
            diff = x_ref[:, cs] + s * ppv - t_ref[:, cs]
            dy = diff * (1.0 / d)
            dy_ref[:, cs] = dy
            dgl_ref[:, cs] = (dy * ppv * s * (1.0 - s)).astype(BF16)
            dpp_ref[:, cs] = (dy * s).astype(BF16)
            sq_ref[:, cs] += jnp.sum(diff * diff, axis=0, keepdims=True)

    row = lambda width: pl.BlockSpec((tm, width), lambda i: (i, 0))
    full = lambda a: pl.BlockSpec(a.shape, lambda i: (0, 0))
    return pl.pallas_call(
        kern, name="ple_head", grid=(m // tm,),
        in_specs=[row(d), full(g), full(w_gate), row(pb.shape[1]), full(w_proj), row(d)],
        out_specs=[row(d), row(d), row(d), row(d), pl.BlockSpec((1, d), lambda i: (0, 0))],
        out_shape=[jax.ShapeDtypeStruct((m, d), BF16), jax.ShapeDtypeStruct((m, d), F32), jax.ShapeDtypeStruct((m, d), BF16),
                   jax.ShapeDtypeStruct((m, d), BF16), jax.ShapeDtypeStruct((1, d), F32)],
        compiler_params=_cparams(("arbitrary",)))(x2, g, w_gate, pb, w_proj, tgt)


def _ssm_out_proj(y, proj, g, w_ssm, attn_out, w_attn, x, tm=1024):
    m, d = y.shape

    def kern(y_ref, z_ref, g_ref, ws_ref, a_ref, wa_ref, x_ref, s_ref, o_ref):
        z = z_ref[...]
        yz = y_ref[...] * (z * _sigmoid(z))
        s = (yz * lax.rsqrt(jnp.mean(yz * yz, axis=-1, keepdims=True) + EPS) * g_ref[...]).astype(BF16)
        s_ref[...] = s
        av = a_ref[...]
        for c0, cw in _col_chunks(d):
            cs = slice(c0, c0 + cw)
            o_ref[:, cs] = x_ref[:, cs] + _dot(s, ws_ref[:, cs]) + _dot(av, wa_ref[:, cs])

    row = lambda width: pl.BlockSpec((tm, width), lambda i: (i, 0))
    full = lambda a: pl.BlockSpec(a.shape, lambda i: (0, 0))
    return pl.pallas_call(
        kern, name="out_proj", grid=(m // tm,),
        in_specs=[row(d), pl.BlockSpec((tm, d), lambda i: (i, OFF_Z // SSM_INNER)), full(g), full(w_ssm), row(attn_out.shape[1]),
                  full(w_attn), row(d)],
        out_specs=[row(d), row(d)],
        out_shape=[jax.ShapeDtypeStruct((m, d), BF16), jax.ShapeDtypeStruct((m, d), F32)],
        compiler_params=_cparams(("parallel",)))(y, proj, g, w_ssm, attn_out, w_attn, x)


def _mm_nt_rms_bwd(name, a, w, x, g, dres, parts=(), tm=512):
    m, k = a.shape
    n = w.shape[0]
    ns, steps = len(parts), m // tm

    def kern(a_ref, w_ref, x_ref, g_ref, dres_ref, *rest):
        dx_ref, dxb_ref, dg_ref = rest[ns:ns + 3]
        dh = rest[2 * ns + 3]
        if ns:
            start, finish = _scatter_steps(ns, rest[:ns], rest[ns + 3:2 * ns + 3], rest[2 * ns + 4:])
            pl.when(pl.program_id(0) == 0)(start)
            pl.when(pl.program_id(0) == steps - 1)(finish)

        @pl.when(pl.program_id(0) == 0)
        def _():
            dg_ref[...] = jnp.zeros(dg_ref.shape, F32)
        av = a_ref[...]
        for c0, cw in _col_chunks(n):
            dh[:, c0:c0 + cw] = _dot_nt(av, w_ref[c0:c0 + cw, :])
        xv, dhv = x_ref[...], dh[...]
        r = lax.rsqrt(jnp.mean(xv * xv, axis=-1, keepdims=True) + EPS)
        gd = dhv * g_ref[...]
        dx = dres_ref[...] + r * gd - xv * (r * r * r * jnp.mean(xv * gd, axis=-1, keepdims=True))
        dx_ref[...] = dx
        dxb_ref[...] = dx.astype(BF16)
        dg_ref[...] += jnp.sum(dhv * xv * r, axis=0, keepdims=True)

    row = lambda width: pl.BlockSpec((tm, width), lambda i: (i, 0))
    return pl.pallas_call(
        kern, name=name, grid=(steps,),
        in_specs=[row(k), pl.BlockSpec((n, k), lambda i: (0, 0)), row(n), pl.BlockSpec((1, n), lambda i: (0, 0)), row(n)]
        + [ANY_SPEC] * ns,
        out_specs=[row(n), row(n), pl.BlockSpec((1, n), lambda i: (0, 0))] + [ANY_SPEC] * ns,
        out_shape=[jax.ShapeDtypeStruct((m, n), F32), jax.ShapeDtypeStruct((m, n), BF16), jax.ShapeDtypeStruct((1, n), F32)]
        + [jax.ShapeDtypeStruct(s.shape, s.dtype) for s in parts],
        scratch_shapes=[pltpu.VMEM((tm, n), F32)] + (_scatter_sems(ns) if ns else []),
        compiler_params=_cparams(("arbitrary",)))(a, w, x, g, dres, *parts)


def _head_mean_matrix(width):
    i = jnp.arange(width) // HEAD_DIM
    return jnp.where(i[:, None] == i[None, :], 1.0 / HEAD_DIM, 0.0).astype(BF16)


ATT_SPAN = 2048
N_QH = 8


def _lane_lo(rows):
    return lax.broadcasted_iota(jnp.int32, (rows, LANE), 1) < HEAD_DIM


def _swap_halves_lanes(x):
    return pltpu.roll(x, HEAD_DIM, axis=1)


def _head_major_qkv(qn, kn, v, qo_ref, kvo_ref):
    lo = _lane_lo(qn.shape[0])
    for j in range(N_KV):
        blk = qn[:, j * LANE:(j + 1) * LANE]
        qo_ref[2 * j] = jnp.where(lo, blk, 0.0)
        qo_ref[2 * j + 1] = jnp.where(lo, _swap_halves_lanes(blk), 0.0)
    for j in range(2):
        kb, vb = kn[:, j * LANE:(j + 1) * LANE], v[:, j * LANE:(j + 1) * LANE]
        kvo_ref[2 * j] = jnp.where(lo, kb, _swap_halves_lanes(vb))
        kvo_ref[2 * j + 1] = jnp.where(lo, _swap_halves_lanes(kb), vb)


def _in_proj(x, g, w, gq_t, gk_t, tm=512):
    m, k = x.shape
    n = w.shape[1]
    bq, bk = _head_mean_matrix(ATTN_DIM), _head_mean_matrix(KV_DIM)
    scale = HEAD_DIM ** -0.5

    def kern(x_ref, g_ref, w_ref, gq_ref, gk_ref, bq_ref, bk_ref, h_ref, o_ref, qo_ref, kvo_ref):
        xv = x_ref[...]
        h = (xv * lax.rsqrt(jnp.mean(xv * xv, axis=-1, keepdims=True) + EPS) * g_ref[...]).astype(BF16)
        h_ref[...] = h
        for c0, cw in _col_chunks(n):
            o_ref[:, c0:c0 + cw] = _dot(h, w_ref[:, c0:c0 + cw])
        q, kk, v = o_ref[:, OFF_Q:OFF_Q + ATTN_DIM], o_ref[:, OFF_K:OFF_K + KV_DIM], o_ref[:, OFF_V:OFF_V + KV_DIM]
        qn = (q * lax.rsqrt(_dot_split(q * q, bq_ref[...]) + EPS) * gq_ref[...]) * scale
        kn = kk * lax.rsqrt(_dot_split(kk * kk, bk_ref[...]) + EPS) * gk_ref[...]
        _head_major_qkv(qn, kn, v, qo_ref, kvo_ref)

    row = lambda width: pl.BlockSpec((tm, width), lambda i: (i, 0))
    full = lambda a: pl.BlockSpec(a.shape, lambda i: (0, 0))
    blk = lambda heads: pl.BlockSpec((heads, tm, LANE), lambda i: (0, i, 0))
    return pl.pallas_call(
        kern, name="in_proj", grid=(m // tm,),
        in_specs=[row(k), full(g), full(w), full(gq_t), full(gk_t), full(bq), full(bk)],
        out_specs=[row(k), row(n), blk(N_QH), blk(N_KV)],
        out_shape=[jax.ShapeDtypeStruct((m, k), BF16), jax.ShapeDtypeStruct((m, n), F32),
                   jax.ShapeDtypeStruct((N_QH, m, LANE), F32), jax.ShapeDtypeStruct((N_KV, m, LANE), F32)],
        compiler_params=_cparams(("parallel",)))(x, g, w, gq_t, gk_t, bq, bk)


def _d_mix(dx1b, w_cat, y, proj, g, attn_out, tm=512):
    m, k = dx1b.shape
    n = w_cat.shape[0]

    def kern(a_ref, w_ref, y_ref, z_ref, g_ref, o_ref, dy_ref, dz_ref, dg_ref, dot_ref, d_ref, dmix):
        @pl.when(pl.program_id(0) == 0)
        def _():
            dg_ref[...] = jnp.zeros(dg_ref.shape, F32)
        av = a_ref[...]
        for c0, cw in _col_chunks(n):
            dmix[:, c0:c0 + cw] = _dot_nt(av, w_ref[c0:c0 + cw, :])
        z, yv, dout = z_ref[...], y_ref[...], dmix[:, :SSM_INNER]
        sg = _sigmoid(z)
        gz = z * sg
        yz = yv * gz
        r = lax.rsqrt(jnp.mean(yz * yz, axis=-1, keepdims=True) + EPS)
        gd = dout * g_ref[...]
        dyz = r * gd - yz * (r * r * r * jnp.mean(yz * gd, axis=-1, keepdims=True))
        dy_ref[...] = dyz * gz
        dz_ref[...] = (dyz * yv * (sg * (1.0 + z * (1.0 - sg)))).astype(BF16)
        dg_ref[...] += jnp.sum(dout * yz * r, axis=0, keepdims=True)
        do = dmix[:, SSM_INNER:]
        prod = do * o_ref[...].astype(F32)
        lo = _lane_lo(tm)
        lane = lax.broadcasted_iota(jnp.int32, (tm, LANE), 1)
        for kh in range(N_KV):
            blk, pb = do[:, kh * LANE:(kh + 1) * LANE], prod[:, kh * LANE:(kh + 1) * LANE]
            dot_ref[2 * kh] = jnp.where(lo, 0.0, _swap_halves_lanes(blk))
            dot_ref[2 * kh + 1] = jnp.where(lo, 0.0, blk)
            s_lo = jnp.sum(jnp.where(lo, pb, 0.0), axis=1, keepdims=True)
            s_hi = jnp.sum(pb, axis=1, keepdims=True) - s_lo
            d_ref[kh] = jnp.where(lane == 0, s_lo, jnp.where(lane == 1, s_hi, 0.0))

    row = lambda width: pl.BlockSpec((tm, width), lambda i: (i, 0))
    full = lambda a: pl.BlockSpec(a.shape, lambda i: (0, 0))
    blk = lambda heads: pl.BlockSpec((heads, tm, LANE), lambda i: (0, i, 0))
    return pl.pallas_call(
        kern, name="d_mix", grid=(m // tm,),
        in_specs=[row(k), full(w_cat), row(SSM_INNER), pl.BlockSpec((tm, SSM_INNER), lambda i: (i, OFF_Z // SSM_INNER)), full(g),
                  row(ATTN_DIM)],
        out_specs=[row(SSM_INNER), row(SSM_INNER), pl.BlockSpec((1, SSM_INNER), lambda i: (0, 0)), blk(N_QH), blk(N_KV)],
        out_shape=[jax.ShapeDtypeStruct((m, SSM_INNER), F32), jax.ShapeDtypeStruct((m, SSM_INNER), BF16),
                   jax.ShapeDtypeStruct((1, SSM_INNER), F32), jax.ShapeDtypeStruct((N_QH, m, LANE), F32),
                   jax.ShapeDtypeStruct((N_KV, m, LANE), F32)],
        scratch_shapes=[pltpu.VMEM((tm, n), F32)], compiler_params=_cparams(("arbitrary",)))(dx1b, w_cat, y, proj, g, attn_out)


def _qknorm_bwd2(proj, gq_t, gk_t, dqs, dkvs, tm=256):
    t = proj.shape[0]
    bq, bk = _head_mean_matrix(ATTN_DIM), _head_mean_matrix(KV_DIM)
    scale = HEAD_DIM ** -0.5

    def kern(q_ref, k_ref, gq_ref, gk_ref, bq_ref, bk_ref, a1, a2, a3, b1, b2, b3, dq_ref, dk_ref, dv_ref, dgq_ref, dgk_ref):
        @pl.when(pl.program_id(0) == 0)
        def _():
            dgq_ref[...] = jnp.zeros(dgq_ref.shape, F32)
            dgk_ref[...] = jnp.zeros(dgk_ref.shape, F32)
        lo = _lane_lo(tm)
        sq = [a1[h] + a2[h] + a3[h] for h in range(N_QH)]
        skv = [b1[h] + b2[h] + b3[h] for h in range(N_KV)]
        dqn = jnp.concatenate([jnp.where(lo, sq[2 * j], _swap_halves_lanes(sq[2 * j + 1])) for j in range(N_KV)], axis=1) * scale
        dkn = jnp.concatenate([jnp.where(lo, skv[2 * j], _swap_halves_lanes(skv[2 * j + 1])) for j in range(2)], axis=1)
        dv = jnp.concatenate([jnp.where(lo, _swap_halves_lanes(skv[2 * j]), skv[2 * j + 1]) for j in range(2)], axis=1)
        q, k = q_ref[...], k_ref[...]
        rq = lax.rsqrt(_dot_split(q * q, bq_ref[...]) + EPS)
        rk = lax.rsqrt(_dot_split(k * k, bk_ref[...]) + EPS)
        gdq, gdk = dqn * gq_ref[...], dkn * gk_ref[...]
        dq_ref[...] = (rq * gdq - q * (rq * rq * rq * _dot_split(q * gdq, bq_ref[...]))).astype(BF16)
        dk_ref[...] = (rk * gdk - k * (rk * rk * rk * _dot_split(k * gdk, bk_ref[...]))).astype(BF16)
        dv_ref[...] = dv.astype(BF16)
        dgq_ref[...] += jnp.sum(dqn * q * rq, axis=0, keepdims=True)
        dgk_ref[...] += jnp.sum(dkn * k * rk, axis=0, keepdims=True)

    col = lambda w, idx: pl.BlockSpec((tm, w), functools.partial(lambda i, idx: (i, idx), idx=idx))
    par = lambda a: pl.BlockSpec(a.shape, lambda i: (0, 0))
    blk = lambda n: pl.BlockSpec((n, tm, LANE), lambda i: (0, i, 0))
    row = lambda w: pl.BlockSpec((tm, w), lambda i: (i, 0))
    acc = lambda w: pl.BlockSpec((1, w), lambda i: (0, 0))
    return pl.pallas_call(
        kern, name="qknorm_bwd", grid=(t // tm,),
        in_specs=[col(ATTN_DIM, OFF_Q // ATTN_DIM), col(KV_DIM, OFF_K // KV_DIM), par(gq_t), par(gk_t), par(bq), par(bk)]
        + [blk(N_QH)] * 3 + [blk(N_KV)] * 3,
        out_specs=[row(ATTN_DIM), row(KV_DIM), row(KV_DIM), acc(ATTN_DIM), acc(KV_DIM)],
        out_shape=[jax.ShapeDtypeStruct((t, ATTN_DIM), BF16), jax.ShapeDtypeStruct((t, KV_DIM), BF16),
                   jax.ShapeDtypeStruct((t, KV_DIM), BF16), jax.ShapeDtypeStruct((1, ATTN_DIM), F32),
                   jax.ShapeDtypeStruct((1, KV_DIM), F32)],
        compiler_params=_cparams(("arbitrary",)))(proj, proj, gq_t, gk_t, bq, bk, *dqs, *dkvs)


def _att_rows(b, r, dil):
    if dil == 1:
        return pl.ds(b * CHUNK, CHUNK)
    return pl.ds(b * CHUNK * dil + r, CHUNK, stride=dil)


def _for_residues(dil, unit):
    for r in range(dil):
        unit(r, 0)


def _band_qk(first):
    ri = lax.broadcasted_iota(jnp.int32, (CHUNK, 2 * CHUNK), 0)
    cj = lax.broadcasted_iota(jnp.int32, (CHUNK, 2 * CHUNK), 1)
    band = (cj - ri >= 0) & (cj - ri <= CHUNK)
    return band if first is None else band & (jnp.logical_not(first) | (cj >= CHUNK))


def _band_kq(last):
    rj = lax.broadcasted_iota(jnp.int32, (CHUNK, 2 * CHUNK), 0)
    ci = lax.broadcasted_iota(jnp.int32, (CHUNK, 2 * CHUNK), 1)
    band = (ci - rj >= 0) & (ci - rj <= CHUNK)
    return band if last is None else band & (jnp.logical_not(last) | (ci < CHUNK))


def _att_specs(t, dil):
    sub = CHUNK * dil
    nb, last = ATT_SPAN // sub, t // sub - 1
    cur = lambda heads: pl.BlockSpec((heads, ATT_SPAN, LANE), lambda kh, n: (kh, n, 0))
    prev = lambda heads: pl.BlockSpec((heads, sub, LANE), lambda kh, n: (kh, jnp.maximum(n * nb - 1, 0), 0))
    nxt = lambda heads: pl.BlockSpec((heads, sub, LANE), lambda kh, n: (kh, jnp.minimum((n + 1) * nb, last), 0))
    return sub, nb, cur, prev, nxt


def _attn_fwd2(q, kv, dil):
    t = q.shape[1]
    sub, nb, cur, prev, _ = _att_specs(t, dil)

    def kern(q_ref, kvp_ref, kvc_ref, o_ref, lse_ref):
        n = pl.program_id(1)
        lane = lax.broadcasted_iota(jnp.int32, (CHUNK, LANE), 1)
        for b in range(nb):
            mask = _band_qk((n == 0) if b == 0 else None)

            def unit(r, carry, b=b, mask=mask):
                rows = _att_rows(b, r, dil)
                kvp = kvc_ref[_att_rows(b - 1, r, dil), :] if b > 0 else kvp_ref[_att_rows(0, r, dil), :]
                kvcat = jnp.concatenate([kvp, kvc_ref[rows, :]], axis=0).astype(BF16)
                lse_tile = jnp.zeros((CHUNK, LANE), F32)
                for g in range(2):
                    s = jnp.where(mask, _dot_nt(q_ref.at[g][rows, :].astype(BF16), kvcat), NEG)
                    m = jnp.max(s, axis=1, keepdims=True)
                    p = jnp.exp(s - m)
                    l = jnp.sum(p, axis=1, keepdims=True)
                    o_ref.at[g][rows, :] = _dot(p.astype(BF16), kvcat) * (1.0 / l)
                    lse_tile = jnp.where(lane == g, m + jnp.log(l), lse_tile)
                lse_ref[rows, :] = lse_tile
                return carry
            _for_residues(dil, unit)

    return pl.pallas_call(
        kern, name=f"attn_fwd_d{dil}", grid=(N_KV, t // ATT_SPAN), in_specs=[cur(2), prev(None), cur(None)],
        out_specs=[cur(2), cur(None)],
        out_shape=[jax.ShapeDtypeStruct((N_QH, t, LANE), F32), jax.ShapeDtypeStruct((N_KV, t, LANE), F32)],
        compiler_params=_cparams(("parallel", "parallel")))(q, kv, kv)


def _attn_merge2(os_, lses, tm=256):
    t = os_[0].shape[1]

    def kern(o1, o2, o3, l1, l2, l3, out_ref, lse_ref):
        pieces = []
        for kh in range(N_KV):
            a, b, c = l1[kh], l2[kh], l3[kh]
            m = jnp.maximum(jnp.maximum(a, b), c)
            tot = m + jnp.log(jnp.exp(a - m) + jnp.exp(b - m) + jnp.exp(c - m))
            lse_ref[kh] = tot
            wa, wb, wc = jnp.exp(a - tot), jnp.exp(b - tot), jnp.exp(c - tot)
            for g in range(2):
                h = 2 * kh + g
                acc = wa[:, g:g + 1] * o1[h] + wb[:, g:g + 1] * o2[h] + wc[:, g:g + 1] * o3[h]
                pieces.append(acc[:, HEAD_DIM:])
        out_ref[...] = jnp.concatenate(pieces, axis=1).astype(BF16)

    blk = lambda n: pl.BlockSpec((n, tm, LANE), lambda i: (0, i, 0))
    return pl.pallas_call(
        kern, name="attn_merge", grid=(t // tm,), in_specs=[blk(N_QH)] * 3 + [blk(N_KV)] * 3,
        out_specs=[pl.BlockSpec((tm, ATTN_DIM), lambda i: (i, 0)), blk(N_KV)],
        out_shape=[jax.ShapeDtypeStruct((t, ATTN_DIM), BF16), jax.ShapeDtypeStruct((N_KV, t, LANE), F32)],
        compiler_params=_cparams(("parallel",)))(*os_, *lses)


def _attn_dq2(q, kv, dot, lse, dsum, dil):
    t = q.shape[1]
    sub, nb, cur, prev, _ = _att_specs(t, dil)

    def kern(q_ref, kvp_ref, kvc_ref, do_ref, lse_ref, d_ref, dq_ref):
        n = pl.program_id(1)
        for b in range(nb):
            mask = _band_qk((n == 0) if b == 0 else None)

            def unit(r, carry, b=b, mask=mask):
                rows = _att_rows(b, r, dil)
                kvp = kvc_ref[_att_rows(b - 1, r, dil), :] if b > 0 else kvp_ref[_att_rows(0, r, dil), :]
                kvcat = jnp.concatenate([kvp, kvc_ref[rows, :]], axis=0).astype(BF16)
                lse_t, d_t = lse_ref[rows, :], d_ref[rows, :]
                for g in range(2):
                    s = jnp.where(mask, _dot_nt(q_ref.at[g][rows, :].astype(BF16), kvcat), NEG)
                    p = jnp.exp(s - lse_t[:, g:g + 1])
                    dp = _dot_nt(do_ref.at[g][rows, :].astype(BF16), kvcat)
                    ds = p * (dp - d_t[:, g:g + 1])
                    dq_ref.at[g][rows, :] = _dot(ds.astype(BF16), kvcat)
                return carry
            _for_residues(dil, unit)

    return pl.pallas_call(
        kern, name=f"attn_dq_d{dil}", grid=(N_KV, t // ATT_SPAN),
        in_specs=[cur(2), prev(None), cur(None), cur(2), cur(None), cur(None)], out_specs=cur(2),
        out_shape=jax.ShapeDtypeStruct((N_QH, t, LANE), F32),
        compiler_params=_cparams(("parallel", "parallel")))(q, kv, kv, dot, lse, dsum)


def _attn_dkv2(q, kv, dot, lse, dsum, dil):
    t = q.shape[1]
    sub, nb, cur, _, nxt = _att_specs(t, dil)
    nsteps = t // ATT_SPAN

    def kern(kv_ref, qc_ref, qn_ref, doc_ref, don_ref, lc_ref, ln_ref, dc_ref, dn_ref, dkv_ref):
        n = pl.program_id(1)
        for b in range(nb):
            inside = b < nb - 1
            mask = _band_kq(None if inside else (n == nsteps - 1))

            def unit(r, carry, b=b, inside=inside, mask=mask):
                rows = _att_rows(b, r, dil)
                nrows = _att_rows(b + 1, r, dil) if inside else _att_rows(0, r, dil)
                kvb = kv_ref[rows, :].astype(BF16)
                follow = lambda cref, nref: (cref if inside else nref)[nrows, :]
                lse_t = jnp.concatenate([lc_ref[rows, :].T, follow(lc_ref, ln_ref).T], axis=1)
                d_t = jnp.concatenate([dc_ref[rows, :].T, follow(dc_ref, dn_ref).T], axis=1)
                acc = jnp.zeros((CHUNK, LANE), F32)
                for g in range(2):
                    qdo = jnp.concatenate([qc_ref.at[g][rows, :], follow(qc_ref.at[g], qn_ref.at[g]),
                                           doc_ref.at[g][rows, :], follow(doc_ref.at[g], don_ref.at[g])], axis=0).astype(BF16)
                    both = _dot_nt(kvb, qdo)
                    pt = jnp.exp(jnp.where(mask, both[:, :2 * CHUNK], NEG) - lse_t[g:g + 1, :])
                    dst = pt * (both[:, 2 * CHUNK:] - d_t[g:g + 1, :])
                    acc = acc + _dot(jnp.concatenate([dst, pt], axis=1).astype(BF16), qdo)
                dkv_ref[rows, :] = acc
                return carry
            _for_residues(dil, unit)

    return pl.pallas_call(
        kern, name=f"attn_dkv_d{dil}", grid=(N_KV, nsteps),
        in_specs=[cur(None), cur(2), nxt(2), cur(2), nxt(2), cur(None), nxt(None), cur(None), nxt(None)], out_specs=cur(None),
        out_shape=jax.ShapeDtypeStruct((N_KV, t, LANE), F32),
        compiler_params=_cparams(("parallel", "parallel")))(kv, q, q, dot, dot, lse, lse, dsum, dsum)


HALO = 8
SSM_CONV_TM, SSM_CONV_W = 512, 512
FFN_CONV_TM, FFN_CONV_W = 256, 1408


def _halo_specs(tm, width, t_rows, col_off=0, lead=None):
    per, last = tm // HALO, t_rows // HALO - 1
    row_maps = (lambda i: i, lambda i: jnp.maximum(i * per - 1, 0), lambda i: jnp.minimum((i + 1) * per, last))
    specs = []
    for rows, rm in zip((tm, HALO, HALO), row_maps):
        if lead is None:
            specs.append(pl.BlockSpec((rows, width), functools.partial(lambda c, i, rm: (rm(i), c + col_off), rm=rm)))
        else:
            specs.append(pl.BlockSpec((None, rows, width), functools.partial(lambda c, i, rm: (lead, rm(i), c + col_off), rm=rm)))
    return specs


def _fill_ext(buf, tile_ref, before_ref, after_ref, i, nt):
    tm = tile_ref.shape[0]
    buf[0:HALO, :] = jnp.where(i > 0, before_ref[...].astype(F32), 0.0)
    buf[HALO:HALO + tm, :] = tile_ref[...].astype(F32)
    if after_ref is not None:
        buf[HALO + tm:, :] = jnp.where(i < nt - 1, after_ref[...].astype(F32), 0.0)


CONV_RB, CONV_CW = 16, 256


def _lane_chunks(width):
    return [slice(c0, min(c0 + CONV_CW, width)) for c0 in range(0, width, CONV_CW)]


def _shifted(buf, taps, r0, rows, cs):
    return [buf[pl.ds(HALO - (taps - 1) + k + r0, rows), cs] for k in range(taps)]


def _taps_fwd(xs, w, b):
    acc = b
    for k, xk in enumerate(xs):
        acc = acc + w[k:k + 1, :] * xk
    return acc


def _taps_bwd(bufd, w, taps, r0, rows, cs):
    acc = None
    for k in range(taps):
        term = w[k:k + 1, :] * bufd[pl.ds(r0 + (taps - 1) - k, rows), cs]
        acc = term if acc is None else acc + term
    return acc


def _fold8(z):
    return z[:HALO] + z[HALO:] if z.shape[0] == 2 * HALO else z


def _silu_grad(pre):
    sg = _sigmoid(pre)
    return sg * (1.0 + pre * (1.0 - sg))


def _ssm_conv_fwd(proj, w, b):
    t = proj.shape[0]
    tm, wd = min(SSM_CONV_TM, t), SSM_CONV_W
    nt, taps = t // tm, w.shape[0]

    def kern(x_ref, xb_ref, w_ref, b_ref, o_ref, buf):
        _fill_ext(buf, x_ref, xb_ref, None, pl.program_id(1), nt)
        for cs in _lane_chunks(wd):
            wv, bv = w_ref[:, cs], b_ref[:, cs]
            for r0 in range(0, tm, CONV_RB):
                pre = _taps_fwd(_shifted(buf, taps, r0, CONV_RB, cs), wv, bv)
                o_ref[r0:r0 + CONV_RB, cs] = pre * _sigmoid(pre)

    tile, before, _ = _halo_specs(tm, wd, t)
    par = lambda rows: pl.BlockSpec((rows, wd), lambda c, i: (0, c))
    return pl.pallas_call(
        kern, name="ssm_conv_fwd", grid=(XBC_DIM // wd, nt), in_specs=[tile, before, par(taps), par(1)],
        out_specs=pl.BlockSpec((tm, wd), lambda c, i: (i, c)), out_shape=jax.ShapeDtypeStruct((t, XBC_DIM), F32),
        scratch_shapes=[pltpu.VMEM((tm + HALO, wd), F32)],
        compiler_params=_cparams(("parallel", "parallel")))(proj, proj, w, b)


def _ssm_conv_bwd(proj, w, b, dact, parts):
    t = proj.shape[0]
    tm, wd = min(SSM_CONV_TM, t), SSM_CONV_W
    nt, taps, ncol, ns = t // tm, w.shape[0], XBC_DIM // SSM_CONV_W, len(parts)

    def kern(x_ref, xb_ref, xa_ref, d_ref, dn_ref, w_ref, b_ref, *rest):
        dx_ref, gw_ref, gb_ref = rest[ns:ns + 3]
        buf, bufd = rest[2 * ns + 3:2 * ns + 5]
        i = pl.program_id(1)
        if ns:
            start, finish = _scatter_steps(ns, rest[:ns], rest[ns + 3:2 * ns + 3], rest[2 * ns + 5:])
            pl.when((pl.program_id(0) == 0) & (i == 0))(start)
            pl.when((pl.program_id(0) == ncol - 1) & (i == nt - 1))(finish)
        _fill_ext(buf, x_ref, xb_ref, xa_ref, i, nt)

        @pl.when(i == 0)
        def _():
            gw_ref[...] = jnp.zeros(gw_ref.shape, F32)
            gb_ref[...] = jnp.zeros(gb_ref.shape, F32)
        for cs in _lane_chunks(wd):
            wv, bv = w_ref[:, cs], b_ref[:, cs]
            acc = [jnp.zeros((HALO, cs.stop - cs.start), F32) for _ in range(taps + 1)]
            for r0 in list(range(0, tm, CONV_RB)) + [tm]:
                inside = r0 < tm
                rows = CONV_RB if inside else HALO
                xs = _shifted(buf, taps, r0, rows, cs)
                d = d_ref[r0:r0 + rows, cs] if inside else jnp.where(i < nt - 1, dn_ref[:, cs], 0.0)
                dpre = d * _silu_grad(_taps_fwd(xs, wv, bv))
                bufd[r0:r0 + rows, cs] = dpre
                if inside:
                    acc[taps] = acc[taps] + _fold8(dpre)
                    for k in range(taps):
                        acc[k] = acc[k] + _fold8(dpre * xs[k])
            gb_ref[:, cs] += jnp.sum(acc[taps], axis=0, keepdims=True)
            for k in range(taps):
                gw_ref[k:k + 1, cs] += jnp.sum(acc[k], axis=0, keepdims=True)
            for r0 in range(0, tm, CONV_RB):
                dx_ref[r0:r0 + CONV_RB, cs] = _taps_bwd(bufd, wv, taps, r0, CONV_RB, cs).astype(BF16)

    xt, xb, xa = _halo_specs(tm, wd, t)
    dt_, _, dn = _halo_specs(tm, wd, t)
    par = lambda rows: pl.BlockSpec((rows, wd), lambda c, i: (0, c))
    return pl.pallas_call(
        kern, name="ssm_conv_bwd", grid=(ncol, nt), in_specs=[xt, xb, xa, dt_, dn, par(taps), par(1)] + [ANY_SPEC] * ns,
        out_specs=[pl.BlockSpec((tm, wd), lambda c, i: (i, c)), par(taps), par(1)] + [ANY_SPEC] * ns,
        out_shape=[jax.ShapeDtypeStruct((t, XBC_DIM), BF16), jax.ShapeDtypeStruct((taps, XBC_DIM), F32),
                   jax.ShapeDtypeStruct((1, XBC_DIM), F32)] + [jax.ShapeDtypeStruct(s.shape, s.dtype) for s in parts],
        scratch_shapes=[pltpu.VMEM((tm + 2 * HALO, wd), F32), pltpu.VMEM((tm + HALO, wd), F32)] + (_scatter_sems(ns) if ns else []),
        compiler_params=_cparams(("arbitrary", "arbitrary")))(proj, proj, proj, dact, dact, w, b, *parts)


def _ffn_act_down(u, w, b, w_down, x1):
    t = u.shape[1]
    tm, wd = min(FFN_CONV_TM, t), D_FF
    nt, taps = t // tm, w.shape[0]

    def kern(g_ref, gb_ref, v_ref, vb_ref, wg_ref, wv_ref, bg_ref, bv_ref, wd_ref, x1_ref, a_ref, x2_ref, bufg, bufv):
        i = pl.program_id(1)
        _fill_ext(bufg, g_ref, gb_ref, None, i, nt)
        _fill_ext(bufv, v_ref, vb_ref, None, i, nt)
        acc = x1_ref[...]
        for cs in _lane_chunks(wd):
            wg, wv, bg, bv = wg_ref[:, cs], wv_ref[:, cs], bg_ref[:, cs], bv_ref[:, cs]
            for r0 in range(0, tm, CONV_RB):
                g = _taps_fwd(_shifted(bufg, taps, r0, CONV_RB, cs), wg, bg)
                v = _taps_fwd(_shifted(bufv, taps, r0, CONV_RB, cs), wv, bv)
                a_ref[r0:r0 + CONV_RB, cs] = (g * _sigmoid(g) * v).astype(BF16)
            acc = acc + _dot(a_ref[:, cs], wd_ref[cs, :])
        x2_ref[...] = acc

    gt, gbf, _ = _halo_specs(tm, wd, t, lead=0)
    vt, vbf, _ = _halo_specs(tm, wd, t, lead=1)
    par = lambda rows, off: pl.BlockSpec((rows, wd), functools.partial(lambda c, i, off: (0, c + off), off=off))
    row = lambda width: pl.BlockSpec((tm, width), lambda c, i: (i, 0))
    return pl.pallas_call(
        kern, name="ffn_act_down", grid=(1, nt),
        in_specs=[gt, gbf, vt, vbf, par(taps, 0), par(taps, 1), par(1, 0), par(1, 1),
                  pl.BlockSpec(w_down.shape, lambda c, i: (0, 0)), row(D_MODEL)],
        out_specs=[row(wd), row(D_MODEL)],
        out_shape=[jax.ShapeDtypeStruct((t, D_FF), BF16), jax.ShapeDtypeStruct((t, D_MODEL), F32)],
        scratch_shapes=[pltpu.VMEM((tm + HALO, wd), F32)] * 2,
        compiler_params=_cparams(("parallel", "parallel")))(u, u, u, u, w, w, b, b, w_down, x1)


def _ffn_act_bwd(u, w, b, da):
    t = u.shape[1]
    tm, wd = min(FFN_CONV_TM, t), FFN_CONV_W
    nt, taps, nc = t // tm, w.shape[0], D_FF // FFN_CONV_W

    def kern(g_ref, gb_ref, ga_ref, v_ref, vb_ref, va_ref, d_ref, dn_ref, wg_ref, wv_ref, bg_ref, bv_ref,
             du_ref, gwg_ref, gwv_ref, gbg_ref, gbv_ref, bufg, bufv, bufdg, bufdv):
        i = pl.program_id(1)
        _fill_ext(bufg, g_ref, gb_ref, ga_ref, i, nt)
        _fill_ext(bufv, v_ref, vb_ref, va_ref, i, nt)

        @pl.when(i == 0)
        def _():
            for r in (gwg_ref, gwv_ref, gbg_ref, gbv_ref):
                r[...] = jnp.zeros(r.shape, F32)
        for cs in _lane_chunks(wd):
            wg, wv, bg, bv = wg_ref[:, cs], wv_ref[:, cs], bg_ref[:, cs], bv_ref[:, cs]
            zero = jnp.zeros((HALO, cs.stop - cs.start), F32)
            accg, accv = [zero] * (taps + 1), [zero] * (taps + 1)
            for r0 in list(range(0, tm, CONV_RB)) + [tm]:
                inside = r0 < tm
                rows = CONV_RB if inside else HALO
                xg, xv = _shifted(bufg, taps, r0, rows, cs), _shifted(bufv, taps, r0, rows, cs)
                g, v = _taps_fwd(xg, wg, bg), _taps_fwd(xv, wv, bv)
                dav = d_ref[r0:r0 + rows, cs] if inside else jnp.where(i < nt - 1, dn_ref[:, cs], 0.0)
                sg = _sigmoid(g)
                dg = dav * v * (sg * (1.0 + g * (1.0 - sg)))
                dv = dav * (g * sg)
                bufdg[r0:r0 + rows, cs] = dg
                bufdv[r0:r0 + rows, cs] = dv
                if inside:
                    accg[taps], accv[taps] = accg[taps] + _fold8(dg), accv[taps] + _fold8(dv)
                    for k in range(taps):
                        accg[k], accv[k] = accg[k] + _fold8(dg * xg[k]), accv[k] + _fold8(dv * xv[k])
            gbg_ref[:, cs] += jnp.sum(accg[taps], axis=0, keepdims=True)
            gbv_ref[:, cs] += jnp.sum(accv[taps], axis=0, keepdims=True)
            for k in range(taps):
                gwg_ref[k:k + 1, cs] += jnp.sum(accg[k], axis=0, keepdims=True)
                gwv_ref[k:k + 1, cs] += jnp.sum(accv[k], axis=0, keepdims=True)
            for r0 in range(0, tm, CONV_RB):
                du_ref[0, r0:r0 + CONV_RB, cs] = _taps_bwd(bufdg, wg, taps, r0, CONV_RB, cs).astype(BF16)
                du_ref[1, r0:r0 + CONV_RB, cs] = _taps_bwd(bufdv, wv, taps, r0, CONV_RB, cs).astype(BF16)

    gt, gbf, gaf = _halo_specs(tm, wd, t, lead=0)
    vt, vbf, vaf = _halo_specs(tm, wd, t, lead=1)
    dt_, _, dn = _halo_specs(tm, wd, t)
    par = lambda rows, off: pl.BlockSpec((rows, wd), functools.partial(lambda c, i, off: (0, c + off), off=off))
    return pl.pallas_call(
        kern, name="ffn_act_bwd", grid=(nc, nt),
        in_specs=[gt, gbf, gaf, vt, vbf, vaf, dt_, dn, par(taps, 0), par(taps, nc), par(1, 0), par(1, nc)],
        out_specs=[pl.BlockSpec((2, tm, wd), lambda c, i: (0, i, c)), par(taps, 0), par(taps, 0), par(1, 0), par(1, 0)],
        out_shape=[jax.ShapeDtypeStruct((2, t, D_FF), BF16)] + [jax.ShapeDtypeStruct((taps, D_FF), F32)] * 2
        + [jax.ShapeDtypeStruct((1, D_FF), F32)] * 2,
        scratch_shapes=[pltpu.VMEM((tm + 2 * HALO, wd), F32)] * 2 + [pltpu.VMEM((tm + HALO, wd), F32)] * 2,
        compiler_params=_cparams(("parallel", "arbitrary")))(u, u, u, u, u, u, da, da, w, w, b, b)


def _softplus(x):
    e = jnp.exp(-jnp.abs(x))
    return jnp.maximum(x, 0.0) + jnp.where(e < 1e-4, e - 0.5 * e * e, jnp.log(1.0 + e))


def _tri(lower):
    r = lax.broadcasted_iota(jnp.int32, (CHUNK, CHUNK), 0)
    c = lax.broadcasted_iota(jnp.int32, (CHUNK, CHUNK), 1)
    return (r >= c) if lower else (r <= c)


def _cum(mat_bool, x):
    return jnp.dot(mat_bool.astype(F32), x, precision=lax.Precision.HIGHEST, preferred_element_type=F32)


def _pair_sel(lane_lo, tile, h0):
    return jnp.where(lane_lo, tile[:, h0:h0 + 1], tile[:, h0 + 1:h0 + 2])


def _pair_sel_mxu(lane_lo, tile, h0):
    rows = lax.broadcasted_iota(jnp.int32, (LANE, LANE), 0)
    sel = (rows == jnp.where(lane_lo, h0, h0 + 1)).astype(BF16)
    return _dot_split(tile, sel)


def _ssd_fwd(xbc_act, proj, dt_bias_p, a_log_p, dskip_t, shards):
    t = xbc_act.shape[0]
    nch = t // CHUNK
    ns = len(shards)

    def kern(xa_ref, dtr_ref, bias_ref, alog_ref, dsk_ref, *rest):
        y_ref, dt_ref, hs_ref = rest[ns:ns + 3]
        hst = rest[2 * ns + 3]
        if ns:
            start, forward, finish = _gather_steps(shards, rest[:ns], rest[ns + 3:2 * ns + 3], rest[2 * ns + 4:])
            pl.when(pl.program_id(0) == 0)(start)
            pl.when(pl.program_id(0) == (3 * nch) // 4)(forward)
            pl.when(pl.program_id(0) == nch - 1)(finish)

        @pl.when(pl.program_id(0) == 0)
        def _():
            hst[...] = jnp.zeros(hst.shape, F32)
        dt = _softplus(dtr_ref[...] + bias_ref[...])
        dt_ref[...] = dt
        acum = _cum(_tri(True), dt * (-jnp.exp(alog_ref[...])))
        acum_t = acum.T
        ea = jnp.exp(acum)
        a_last = acum[CHUNK - 1:CHUNK, :]
        dend = jnp.exp(a_last - acum)
        ea_last = jnp.exp(a_last)
        causal = _tri(True)
        lane_lo = lax.broadcasted_iota(jnp.int32, (CHUNK, LANE), 1) < HEAD_DIM
        row_lo = lax.broadcasted_iota(jnp.int32, (CHUNK, LANE), 0) < HEAD_DIM
        for g in range(2):
            bg = xa_ref[:, SSM_INNER + g * SSM_STATE:SSM_INNER + (g + 1) * SSM_STATE].astype(BF16)
            cg = xa_ref[:, SSM_INNER + BC_DIM + g * SSM_STATE:SSM_INNER + BC_DIM + (g + 1) * SSM_STATE].astype(BF16)
            cb = _dot_nt(cg, bg)
            for j in range(4 * g, 4 * g + 4):
                h0 = 2 * j
                cols = slice(j * LANE, (j + 1) * LANE)
                xp = xa_ref[:, cols]
                xdt = xp * _pair_sel(lane_lo, dt, h0)
                ydiag = None
                for hh, sel in ((h0, lane_lo), (h0 + 1, ~lane_lo)):
                    seg = acum[:, hh:hh + 1] - acum_t[hh:hh + 1, :]
                    mm = (cb * jnp.where(causal, jnp.exp(jnp.minimum(seg, 0.0)), 0.0)).astype(BF16)
                    d = _dot(mm, jnp.where(sel, xdt, 0.0).astype(BF16))
                    ydiag = d if ydiag is None else ydiag + d
                hp = hst[cols, :]
                hs_ref[cols, :] = hp
                yoff = _dot_nt(cg, hp.astype(BF16)) * _pair_sel(lane_lo, ea, h0)
                y_ref[:, cols] = ydiag + yoff + dsk_ref[:, cols] * xp
                xw = (xdt * _pair_sel(lane_lo, dend, h0)).astype(BF16)
                rowf = jnp.where(row_lo, ea_last[:, h0:h0 + 1], ea_last[:, h0 + 1:h0 + 2])
                hst[cols, :] = hp * rowf + _dot_tn(xw, bg)

    return pl.pallas_call(
        kern, name="ssd_fwd", grid=(nch,),
        in_specs=[pl.BlockSpec((CHUNK, XBC_DIM), lambda c: (c, 0)), pl.BlockSpec((CHUNK, LANE), lambda c: (c, OFF_DT // LANE)),
                  pl.BlockSpec((1, LANE), lambda c: (0, 0)), pl.BlockSpec((1, LANE), lambda c: (0, 0)),
                  pl.BlockSpec((1, SSM_INNER), lambda c: (0, 0))] + [ANY_SPEC] * ns,
        out_specs=[pl.BlockSpec((CHUNK, SSM_INNER), lambda c: (c, 0)), pl.BlockSpec((CHUNK, LANE), lambda c: (c, 0)),
                   pl.BlockSpec((None, SSM_INNER, SSM_STATE), lambda c: (c, 0, 0))] + [ANY_SPEC] * ns,
        out_shape=[jax.ShapeDtypeStruct((t, SSM_INNER), F32), jax.ShapeDtypeStruct((t, LANE), F32),
                   jax.ShapeDtypeStruct((nch, SSM_INNER, SSM_STATE), F32)] + _gather_out_shapes(shards),
        scratch_shapes=[pltpu.VMEM((SSM_INNER, SSM_STATE), F32)] + (_gather_sems(ns) if ns else []),
        compiler_params=_cparams(("arbitrary",)))(xbc_act, proj, dt_bias_p, a_log_p, dskip_t, *shards)


def _ssd_bwd(xbc_act, proj, dt_sp, hstates, dy, dt_bias_p, a_log_p, dskip_t, swaps):
    t = xbc_act.shape[0]
    nch = t // CHUNK
    ns = len(swaps)

    pair = jnp.arange(SSM_HEADS // 2)[:, None, None]
    psel = (jnp.arange(LANE)[None, None, :] == 2 * pair + (jnp.arange(LANE) // HEAD_DIM)[None, :, None]).astype(BF16)

    def kern(xa_ref, dtr_ref, dt_ref, hs_ref, dy_ref, bias_ref, alog_ref, dsk_ref, psel_ref, *rest):
        dact_ref, ddtr_ref, da_ref, dbias_ref, ddsk_ref = rest[ns:ns + 5]
        dh = rest[2 * ns + 5]
        if ns:
            start, finish = _swap_steps(swaps, rest[:ns], rest[ns + 5:2 * ns + 5], rest[2 * ns + 6:])
            pl.when(pl.program_id(0) == 0)(start)
            pl.when(pl.program_id(0) == nch - 1)(finish)

        @pl.when(pl.program_id(0) == 0)
        def _():
            dh[...] = jnp.zeros(dh.shape, F32)
            for r in (da_ref, dbias_ref, ddsk_ref):
                r[...] = jnp.zeros(r.shape, F32)
        dt = dt_ref[...]
        a_neg = -jnp.exp(alog_ref[...])
        acum = _cum(_tri(True), dt * a_neg)
        acum_t = acum.T
        ea = jnp.exp(acum)
        a_last = acum[CHUNK - 1:CHUNK, :]
        dend = jnp.exp(a_last - acum)
        ea_last = jnp.exp(a_last)
        causal = _tri(True)
        lane = lax.broadcasted_iota(jnp.int32, (CHUNK, LANE), 1)
        rowi = lax.broadcasted_iota(jnp.int32, (CHUNK, LANE), 0)
        lane_lo, row_lo, last_row = lane < HEAD_DIM, rowi < HEAD_DIM, rowi == CHUNK - 1
        d_dt = jnp.zeros((CHUNK, LANE), F32)
        d_acum = jnp.zeros((CHUNK, LANE), F32)
        for g in range(2):
            bcols = slice(SSM_INNER + g * SSM_STATE, SSM_INNER + (g + 1) * SSM_STATE)
            ccols = slice(SSM_INNER + BC_DIM + g * SSM_STATE, SSM_INNER + BC_DIM + (g + 1) * SSM_STATE)
            bg, cg = xa_ref[:, bcols].astype(BF16), xa_ref[:, ccols].astype(BF16)
            cb = _dot_nt(cg, bg)
            dg_sum = jnp.zeros((CHUNK, CHUNK), F32)
            dcg = jnp.zeros((CHUNK, SSM_STATE), F32)
            dbg = jnp.zeros((CHUNK, SSM_STATE), F32)
            for j in range(4 * g, 4 * g + 4):
                h0 = 2 * j
                cols = slice(j * LANE, (j + 1) * LANE)
                xp, dyp = xa_ref[:, cols], dy_ref[:, cols]
                dtsel = _pair_sel_mxu(lane_lo, dt, h0)
                xdt = xp * dtsel
                xdt_b = xdt.astype(BF16)
                hp, dhp = hs_ref[cols, :], dh[cols, :]
                hp_b, dhp_b = hp.astype(BF16), dhp.astype(BF16)
                easel, dendsel = _pair_sel_mxu(lane_lo, ea, h0), _pair_sel_mxu(lane_lo, dend, h0)
                dx, ydiag = None, None
                for hh, sel in ((h0, lane_lo), (h0 + 1, ~lane_lo)):
                    dyh = jnp.where(sel, dyp, 0.0).astype(BF16)
                    seg = acum[:, hh:hh + 1] - acum_t[hh:hh + 1, :]
                    dec = jnp.where(causal, jnp.exp(jnp.minimum(seg, 0.0)), 0.0)
                    mm_b = (cb * dec).astype(BF16)
                    dg_sum = dg_sum + dec * _dot_nt(dyh, xdt_b)
                    d = _dot_tn(mm_b, dyh)
                    y = _dot(mm_b, jnp.where(sel, xdt, 0.0).astype(BF16))
                    dx = d if dx is None else dx + d
                    ydiag = y if ydiag is None else ydiag + y
                g2 = _dot_nt(bg, dhp_b)
                tprod = xdt * g2 * dendsel
                yoff = _dot_nt(cg, hp_b) * easel
                yc = dyp.astype(BF16).astype(F32) * ydiag + dyp * yoff - (xdt_b.astype(F32) * dx + tprod)
                dx = dx + g2 * dendsel
                psel = psel_ref[j]
                t_lo = jnp.sum(jnp.where(lane_lo, tprod, 0.0), keepdims=True).reshape(1, 1)
                t_hi = jnp.sum(tprod, keepdims=True).reshape(1, 1) - t_lo
                hh_prod = dhp * hp
                s_lo = jnp.sum(jnp.where(row_lo, hh_prod, 0.0), keepdims=True).reshape(1, 1)
                s_hi = jnp.sum(hh_prod, keepdims=True).reshape(1, 1) - s_lo
                end_lo = ea_last[:, h0:h0 + 1] * s_lo + t_lo
                end_hi = ea_last[:, h0 + 1:h0 + 2] * s_hi + t_hi
                ends = jnp.where(lane == h0, end_lo, jnp.where(lane == h0 + 1, end_hi, 0.0))
                d_acum = d_acum + _dot_split(yc, psel) + jnp.where(last_row, ends, 0.0)
                dye = (dyp * easel).astype(BF16)
                dcg = dcg + _dot(dye, hp_b)
                dbg = dbg + _dot((xdt * dendsel).astype(BF16), dhp_b)
                rowf = jnp.where(row_lo, ea_last[:, h0:h0 + 1], ea_last[:, h0 + 1:h0 + 2])
                dh[cols, :] = dhp * rowf + _dot_tn(dye, cg)
                dact_ref[:, cols] = dx * dtsel + dsk_ref[:, cols] * dyp
                d_dt = d_dt + _dot_split(dx * xp, psel)
                ddsk_ref[:, cols] += jnp.sum(dyp * xp, axis=0, keepdims=True)
            dg_b = dg_sum.astype(BF16)
            dact_ref[:, ccols] = dcg + _dot(dg_b, bg)
            dact_ref[:, bcols] = dbg + _dot_tn(dg_b, cg)
        d_adt = _cum(_tri(False), d_acum)
        d_dt = d_dt + d_adt * a_neg
        da_ref[...] += jnp.sum(d_adt * dt, axis=0, keepdims=True)
        d_raw = jnp.where(lane < SSM_HEADS, d_dt * _sigmoid(dtr_ref[...] + bias_ref[...]), 0.0)
        ddtr_ref[...] = d_raw.astype(BF16)
        dbias_ref[...] += jnp.sum(d_raw, axis=0, keepdims=True)

    rev = lambda c: (nch - 1 - c, 0)
    return pl.pallas_call(
        kern, name="ssd_bwd", grid=(nch,),
        in_specs=[pl.BlockSpec((CHUNK, XBC_DIM), rev), pl.BlockSpec((CHUNK, LANE), lambda c: (nch - 1 - c, OFF_DT // LANE)),
                  pl.BlockSpec((CHUNK, LANE), rev), pl.BlockSpec((None, SSM_INNER, SSM_STATE), lambda c: (nch - 1 - c, 0, 0)),
                  pl.BlockSpec((CHUNK, SSM_INNER), rev),
                  pl.BlockSpec((1, LANE), lambda c: (0, 0)), pl.BlockSpec((1, LANE), lambda c: (0, 0)),
                  pl.BlockSpec((1, SSM_INNER), lambda c: (0, 0)), pl.BlockSpec(psel.shape, lambda c: (0, 0, 0))] + [ANY_SPEC] * ns,
        out_specs=[pl.BlockSpec((CHUNK, XBC_DIM), rev), pl.BlockSpec((CHUNK, LANE), rev),
                   pl.BlockSpec((1, LANE), lambda c: (0, 0)), pl.BlockSpec((1, LANE), lambda c: (0, 0)),
                   pl.BlockSpec((1, SSM_INNER), lambda c: (0, 0))] + [ANY_SPEC] * ns,
        out_shape=[jax.ShapeDtypeStruct((t, XBC_DIM), F32), jax.ShapeDtypeStruct((t, LANE), BF16),
                   jax.ShapeDtypeStruct((1, LANE), F32), jax.ShapeDtypeStruct((1, LANE), F32),
                   jax.ShapeDtypeStruct((1, SSM_INNER), F32)] + _swap_out_shapes(swaps),
        scratch_shapes=[pltpu.VMEM((SSM_INNER, SSM_STATE), F32)] + (_swap_sems(ns) if ns else []),
        compiler_params=_cparams(("arbitrary",)))(xbc_act, proj, dt_sp, hstates, dy, dt_bias_p, a_log_p, dskip_t, psel, *swaps)


def _pad_lanes(v, width=LANE):
    return jnp.pad(v, ((0, 0), (0, width - v.shape[1])))


def _local_step(x, p, tgt, wts, late_shards=(), join_late=None, reduce_early=None, reduce_late=None):
    g_attn, g_ssm, g_ffn, g_ple = wts["attn_norm_g"], wts["ssm_norm_g"], wts["ffn_norm_g"], wts["ple_norm_g"]
    w_in_p = wts["w_in_p"]
    gq_t = jnp.tile(wts["q_norm_g"], (1, ATTN_DIM // HEAD_DIM))
    gk_t = jnp.tile(wts["k_norm_g"], (1, KV_DIM // HEAD_DIM))
    dt_bias_p, a_log_p = _pad_lanes(wts["dt_bias"]), _pad_lanes(wts["a_log"])
    dskip_t = jnp.repeat(wts["d_skip"], HEAD_DIM, axis=1)

    h1, proj, q_hm, kv_hm = _in_proj(x, g_attn, w_in_p, gq_t, gk_t)
    pats = [_attn_fwd2(q_hm, kv_hm, d) for d in DILATIONS]
    attn_out, lse = _attn_merge2([o for o, _ in pats], [l for _, l in pats])
    xbc_act = _ssm_conv_fwd(proj, wts["ssm_conv_w"], wts["ssm_conv_b"])
    y_ssd, dt_sp, hstates, *gathered = _ssd_fwd(xbc_act, proj, dt_bias_p, a_log_p, dskip_t, list(late_shards))
    if join_late is not None:
        wts = {**wts, **join_late(gathered)}
    w_out_s, w_out_a = wts["w_out_ssm"], wts["w_out_attn"]
    w_up, w_down, w_gate, w_proj = wts["w_up"], wts["w_down"], wts["w_ple_gate"], wts["w_ple_proj"]
    ssm_out, x1 = _ssm_out_proj(y_ssd, proj, g_ssm, w_out_s, attn_out, w_out_a, x)
    h2, u = _norm_mm("ffn_up", x1, g_ffn, w_up, tm=1024, tn=1408, halves=True)
    a, x2 = _ffn_act_down(u, wts["ffn_conv_w"], wts["ffn_conv_b"], w_down, x1)
    pb = p.astype(BF16)
    h3, dy, dgl, dpp, sq = _ple_head(x2, g_ple, w_gate, pb, w_proj, tgt)

    grads = {}
    grads["w_ple_proj"] = _mm_tn("g_ple_proj", pb, dpp, tn=PLE_DIM, chip_cols=True)
    grads["w_ple_gate"] = _mm_tn("g_ple_gate", h3, dgl)
    dx2, dx2b, grads["ple_norm_g"] = _mm_nt_rms_bwd("d_h3", dgl, w_gate, x2, g_ple, dy)
    da = _mm_nt("d_ffn_act", [(dx2b, w_down, 0)], F32, tm=1024, tn=1408)
    grads["w_down"] = _mm_tn("g_ffn_down", a, dx2b, tm=1408)
    du, gwg, gwv, gbg, gbv = _ffn_act_bwd(u, wts["ffn_conv_w"], wts["ffn_conv_b"], da)
    grads["ffn_conv_w"] = jnp.concatenate([gwg, gwv], axis=1)
    grads["ffn_conv_b"] = jnp.concatenate([gbg, gbv], axis=1)
    grads["w_up"] = _mm_tn("g_ffn_up", h2, du, tn=1408, chip_cols=True)
    dh2 = _mm_nt("d_h2", [(du, w_up, 0, 0), (du, w_up, 1, 1)], F32, tm=1024, tn=512)
    dx1, dx1b, grads["ffn_norm_g"] = _rms_bwd("rms_ffn_bwd", dh2, x1, g_ffn, dx2)
    dy_ssd, dz, grads["ssm_norm_g"], do_hm, dsum = _d_mix(dx1b, jnp.concatenate([w_out_s, w_out_a], axis=0), y_ssd, proj, g_ssm,
                                                            attn_out)
    grads["w_out"] = jnp.concatenate([_mm_tn("g_out_attn", attn_out, dx1b), _mm_tn("g_out_ssm", ssm_out, dx1b)], axis=0)
    early_major = reduce_early[0](grads) if reduce_early is not None else []
    dact, ddtr, d_a, d_bias, d_dsk, *early_got = _ssd_bwd(xbc_act, proj, dt_sp, hstates, dy_ssd, dt_bias_p, a_log_p, dskip_t,
                                                           early_major)
    grads["dt_bias"] = d_bias[:, :SSM_HEADS]
    grads["a_log"] = d_a[:, :SSM_HEADS] * (-jnp.exp(wts["a_log"]))
    grads["d_skip"] = jnp.sum(d_dsk.reshape(SSM_HEADS, HEAD_DIM), axis=1)[None, :]
    chip_sums = reduce_early[1](early_major, early_got) if reduce_early is not None else []
    dxbc, grads["ssm_conv_w"], grads["ssm_conv_b"], *scattered = _ssm_conv_bwd(proj, wts["ssm_conv_w"], wts["ssm_conv_b"], dact,
                                                                                chip_sums)
    dqs = [_attn_dq2(q_hm, kv_hm, do_hm, lse, dsum, d) for d in DILATIONS]
    dkvs = [_attn_dkv2(q_hm, kv_hm, do_hm, lse, dsum, d) for d in DILATIONS]
    dq, dk, dv, dgq, dgk = _qknorm_bwd2(proj, gq_t, gk_t, dqs, dkvs)
    grads["q_norm_g"] = jnp.sum(dgq.reshape(ATTN_DIM // HEAD_DIM, HEAD_DIM), axis=0)[None, :]
    grads["k_norm_g"] = jnp.sum(dgk.reshape(KV_DIM // HEAD_DIM, HEAD_DIM), axis=0)[None, :]
    dproj = jnp.concatenate([dxbc, dq, dz, dk, dv, ddtr], axis=1)
    grads["w_in_p"] = _mm_tn("g_in_proj", h1, dproj, tm=512)
    late_sums = reduce_late(grads) if reduce_late is not None else []
    grad_x, _, grads["attn_norm_g"], *late_scattered = _mm_nt_rms_bwd("d_h1", dproj, w_in_p, x, g_attn, dx1, late_sums)
    return sq, grad_x, grads, (chip_sums, scattered), (late_sums, late_scattered)


MESH_IDS = pl.DeviceIdType.MESH
N_CHIPS = 4
ANY_SPEC = pl.BlockSpec(memory_space=pl.ANY)
SMALL_ROWS = 96
ALL_SMALL_ROWS = 272


def _place():
    x, y, c = lax.axis_index("x"), lax.axis_index("y"), lax.axis_index("c")
    return x, y, c, [(1 - x, y), (x, 1 - y), (1 - x, 1 - y)]


def _gather_over_chips(arrs):
    n = len(arrs)

    def body(*refs):
        steps = _gather_steps(arrs, refs[:n], refs[n:2 * n], refs[2 * n:2 * n + 4])
        for step in steps:
            step()

    return pl.pallas_call(
        body, name="gather_weights", in_specs=[ANY_SPEC] * n, out_specs=[ANY_SPEC] * n,
        out_shape=_gather_out_shapes(arrs), scratch_shapes=_gather_sems(n))(*arrs)


def _gather_out_shapes(arrs):
    return [jax.ShapeDtypeStruct((N_CHIPS,) + a.shape, a.dtype) for a in arrs]


def _gather_sems(n):
    return [pltpu.SemaphoreType.DMA((3 * n,))] * 4


def _gather_steps(arrs, ins, outs, sems):
    n = len(arrs)
    split = [a.shape[0] % 64 == 0 for a in arrs]
    ici_send, ici_recv, d2d_send, d2d_recv = sems

    def place():
        x, y, c, chips = _place()
        return x, y, c, chips, 2 * x + y

    def part(ref, a, core):
        if not split[a]:
            return ref
        half = arrs[a].shape[0] // 2
        return ref.at[pl.ds(core * half, half)]

    def ici(a, k, slot, where):
        x, y, c, chips, _ = where
        px, py = chips[k]
        return pltpu.make_async_remote_copy(
            src_ref=part(ins[a], a, c), dst_ref=part(outs[a].at[slot], a, c), send_sem=ici_send.at[3 * a + k],
            recv_sem=ici_recv.at[3 * a + k], device_id=(px, py, c), device_id_type=MESH_IDS)

    def d2d(a, k, core, where):
        x, y, c, chips, _ = where
        px, py = chips[k]
        piece = part(outs[a].at[2 * px + py], a, core)
        return pltpu.make_async_remote_copy(src_ref=piece, dst_ref=piece, send_sem=d2d_send.at[3 * a + k],
                                            recv_sem=d2d_recv.at[3 * a + k], device_id=(x, y, 1 - c), device_id_type=MESH_IDS)

    def start():
        w = place()
        for a in range(n):
            for k in range(3):
                ici(a, k, w[4], w).start()

    def forward():
        w = place()
        for a in range(n):
            for k, (px, py) in enumerate(w[3]):
                ici(a, k, 2 * px + py, w).wait_recv()
                if split[a]:
                    d2d(a, k, w[2], w).start()

    def finish():
        w = place()
        for a in range(n):
            for k in range(3):
                if split[a]:
                    d2d(a, k, 1 - w[2], w).wait_recv()
                    d2d(a, k, w[2], w).wait_send()
                ici(a, k, w[4], w).wait_send()

    return start, forward, finish


def _row_tile(rows, cap=256):
    return max(d for d in range(8, cap + 1, 8) if rows % d == 0)


def _swap_halves(name, gs):
    n = len(gs)

    def body(*refs):
        for step in _swap_steps(gs, refs[:n], refs[n:2 * n], refs[2 * n:2 * n + 2]):
            step()

    return pl.pallas_call(
        body, name=name, in_specs=[ANY_SPEC] * n, out_specs=[ANY_SPEC] * n, out_shape=_swap_out_shapes(gs),
        scratch_shapes=_swap_sems(n))(*gs)


def _swap_out_shapes(gs):
    return [jax.ShapeDtypeStruct((N_CHIPS, g.shape[1] // 2, g.shape[2]), g.dtype) for g in gs]


def _swap_sems(n):
    return [pltpu.SemaphoreType.DMA((N_CHIPS * n,))] * 2


def _swap_steps(gs, ins, outs, sems):
    send, recv = sems

    def copies():
        x, y, c, _ = _place()
        cps = []
        for a in range(len(gs)):
            half = gs[a].shape[1] // 2
            for q in range(N_CHIPS):
                cps.append(pltpu.make_async_remote_copy(
                    src_ref=ins[a].at[q, pl.ds((1 - c) * half, half)], dst_ref=outs[a].at[q], send_sem=send.at[N_CHIPS * a + q],
                    recv_sem=recv.at[N_CHIPS * a + q], device_id=(x, y, 1 - c), device_id_type=MESH_IDS))
        return cps

    def start():
        for cp in copies():
            cp.start()

    def finish():
        for cp in copies():
            cp.wait()

    return start, finish


def _add_halves(name, g, got, c_idx):
    rows, cols = got.shape[1:]
    tm = _row_tile(rows)
    per = rows // tm

    def kern(c_ref, g_ref, r_ref, o_ref):
        o_ref[...] = (g_ref[...] + r_ref[...]).astype(BF16)

    return pl.pallas_call(
        kern, name=name,
        grid_spec=pltpu.PrefetchScalarGridSpec(
            num_scalar_prefetch=1, grid=(N_CHIPS, per),
            in_specs=[pl.BlockSpec((None, tm, cols), lambda q, i, c_ref: (q, c_ref[0] * per + i, 0)),
                      pl.BlockSpec((None, tm, cols), lambda q, i, c_ref: (q, i, 0))],
            out_specs=pl.BlockSpec((None, tm, cols), lambda q, i, c_ref: (q, i, 0))),
        out_shape=jax.ShapeDtypeStruct((N_CHIPS, rows, cols), BF16),
        compiler_params=_cparams(("parallel", "parallel")))(c_idx, g, got)


def _scatter_sems(n):
    return [pltpu.SemaphoreType.DMA((3 * n,))] * 2


def _scatter_steps(n, ins, outs, sems):
    send, recv = sems

    def copy(a, k, slot, where):
        x, y, c, chips = where
        px, py = chips[k]
        return pltpu.make_async_remote_copy(src_ref=ins[a].at[2 * px + py], dst_ref=outs[a].at[slot], send_sem=send.at[3 * a + k],
                                            recv_sem=recv.at[3 * a + k], device_id=(px, py, c), device_id_type=MESH_IDS)

    def start():
        w = _place()
        for a in range(n):
            for k in range(3):
                copy(a, k, 2 * w[0] + w[1], w).start()

    def finish():
        w = _place()
        for a in range(n):
            for k, (px, py) in enumerate(w[3]):
                copy(a, k, 2 * px + py, w).wait()

    return start, finish


def _sum_chips(name, own, parts, idx):
    rows, cols = parts.shape[1:]
    tm = _row_tile(rows)
    per = rows // tm

    def kern(o_idx, a_ref, b_ref, c_ref, d_ref, o_ref):
        o_ref[...] = ((a_ref[...].astype(F32) + b_ref[...].astype(F32)) + c_ref[...].astype(F32)) + d_ref[...].astype(F32)

    def spec(k):
        return pl.BlockSpec((None, tm, cols), functools.partial(lambda i, o_idx, k: (o_idx[k], i, 0), k=k))

    return pl.pallas_call(
        kern, name=name,
        grid_spec=pltpu.PrefetchScalarGridSpec(
            num_scalar_prefetch=1, grid=(per,), in_specs=[spec(0), spec(1), spec(2), spec(3)],
            out_specs=pl.BlockSpec((None, tm, cols), lambda i, o_idx: (0, o_idx[4] * per + i, 0))),
        out_shape=jax.ShapeDtypeStruct((1, 2 * rows, cols), F32), compiler_params=_cparams(("parallel",)))(idx, own, parts, parts, parts)


def _share_with_sibling(gs):
    n = len(gs)

    def body(*refs):
        ins, send, recv = refs[:n], refs[2 * n], refs[2 * n + 1]
        x, y, c, _ = _place()
        cps = []
        for a in range(n):
            half = gs[a].shape[1] // 2
            mine = pl.ds(c * half, half)
            cps.append(pltpu.make_async_remote_copy(src_ref=ins[a].at[0, mine], dst_ref=refs[n + a].at[0, mine], send_sem=send.at[a],
                                                    recv_sem=recv.at[a], device_id=(x, y, 1 - c), device_id_type=MESH_IDS))
        for cp in cps:
            cp.start()
        for cp in cps:
            cp.wait()

    return pl.pallas_call(
        body, name="grad_share_sibling", in_specs=[ANY_SPEC] * n, out_specs=[ANY_SPEC] * n,
        out_shape=[jax.ShapeDtypeStruct(g.shape, g.dtype) for g in gs], input_output_aliases={a: a for a in range(n)},
        scratch_shapes=[pltpu.SemaphoreType.DMA((n,))] * 2)(*gs)


def _allreduce_small(v):
    def body(v_ref, o_ref, land, send, recv):
        x, y, c, _ = _place()
        me = 4 * x + 2 * y + c
        land[me] = v_ref[...]
        cps = []
        for rel in range(1, 8):
            bx, by, bc = (rel >> 2) & 1, (rel >> 1) & 1, rel & 1
            peer = (1 - x if bx else x, 1 - y if by else y, 1 - c if bc else c)
            cps.append(pltpu.make_async_remote_copy(src_ref=v_ref, dst_ref=land.at[me], send_sem=send.at[rel - 1],
                                                    recv_sem=recv.at[rel - 1], device_id=peer, device_id_type=MESH_IDS))
        for cp in cps:
            cp.start()
        for cp in cps:
            cp.wait()
        acc = land[0]
        for d in range(1, 8):
            acc = acc + land[d]
        o_ref[...] = acc

    vm = pl.BlockSpec(memory_space=pltpu.VMEM)
    return pl.pallas_call(
        body, name="allreduce_small", in_specs=[vm], out_specs=vm, out_shape=jax.ShapeDtypeStruct(v.shape, F32),
        scratch_shapes=[pltpu.VMEM((8,) + v.shape, F32), pltpu.SemaphoreType.DMA((7,)), pltpu.SemaphoreType.DMA((7,))])(v)


def _adamw(name, w, g, m, v):
    _, rows, cols = w.shape
    tm = rows if rows * cols <= 128 * 1024 else _row_tile(rows, max(256, 2048 * LANE // cols))
    c1 = 1.0 / (1.0 - ADAM_B1 ** ADAM_STEP)
    c2 = 1.0 / (1.0 - ADAM_B2 ** ADAM_STEP)

    def kern(w_ref, g_ref, m_ref, v_ref, d_ref, mo_ref, vo_ref):
        gv = g_ref[...]
        mn = ADAM_B1 * m_ref[...] + (1.0 - ADAM_B1) * gv
        vn = ADAM_B2 * v_ref[...] + (1.0 - ADAM_B2) * (gv * gv)
        d_ref[...] = -ADAM_LR * ((mn * c1) / (jnp.sqrt(vn * c2) + ADAM_EPS) + ADAM_WD * w_ref[...])
        mo_ref[...] = mn
        vo_ref[...] = vn

    spec = pl.BlockSpec((None, tm, cols), lambda i: (0, i, 0))
    return pl.pallas_call(
        kern, name=name, grid=(rows // tm,), in_specs=[spec] * 4, out_specs=[spec] * 3,
        out_shape=[jax.ShapeDtypeStruct(w.shape, F32)] * 3, compiler_params=_cparams(("parallel",)))(w, g, m, v)


SHARDED = (("w_in", 1), ("w_out", 0), ("w_up", 1), ("w_down", 0), ("w_ple_gate", 0), ("w_ple_proj", 1),
           ("ssm_conv_w", 1), ("ffn_conv_w", 1))
MATRICES = ("w_in", "w_out", "w_up", "w_down", "w_ple_gate", "w_ple_proj")
EARLY_REDUCED = MATRICES[1:]
REPLICATED = ("attn_norm_g", "q_norm_g", "k_norm_g", "ssm_conv_b", "dt_bias", "a_log", "d_skip", "ssm_norm_g",
              "ffn_norm_g", "ffn_conv_b", "ple_norm_g")
WEIGHT_ORDER = ("attn_norm_g", "w_in", "q_norm_g", "k_norm_g", "ssm_conv_w", "ssm_conv_b", "dt_bias", "a_log", "d_skip",
                "ssm_norm_g", "w_out", "ffn_norm_g", "w_up", "ffn_conv_w", "ffn_conv_b", "w_down", "ple_norm_g",
                "w_ple_gate", "w_ple_proj")


def _join_chips(g, axis):
    if axis == 0:
        return g.reshape(g.shape[0] * g.shape[1], g.shape[2])
    return jnp.transpose(g, (1, 0, 2)).reshape(g.shape[1], g.shape[0] * g.shape[2])


def _split_chips(g, axis):
    if axis == 0:
        return g.reshape(N_CHIPS, g.shape[0] // N_CHIPS, g.shape[1])
    r, c = g.shape
    return jnp.transpose(g.reshape(r, N_CHIPS, c // N_CHIPS), (1, 0, 2))


def _pack_small(vals, rows=SMALL_ROWS):
    flat = jnp.concatenate([v.reshape(-1) for v in vals])
    return jnp.pad(flat, (0, rows * LANE - flat.shape[0])).reshape(rows, LANE)


def _unpack_small(packed, like):
    flat, out, off = packed.reshape(-1), [], 0
    for v in like:
        out.append(flat[off:off + v.size].reshape(v.shape))
        off += v.size
    return out


def kernel(x, p, attn_norm_g, w_in, q_norm_g, k_norm_g, ssm_conv_w, ssm_conv_b, dt_bias, a_log, d_skip, ssm_norm_g, w_out, ffn_norm_g, w_up, ffn_conv_w, ffn_conv_b, w_down, ple_norm_g, w_ple_gate, w_ple_proj, loss_target, m_attn_norm_g, m_w_in, m_q_norm_g, m_k_norm_g, m_ssm_conv_w, m_ssm_conv_b, m_dt_bias, m_a_log, m_d_skip, m_ssm_norm_g, m_w_out, m_ffn_norm_g, m_w_up, m_ffn_conv_w, m_ffn_conv_b, m_w_down, m_ple_norm_g, m_w_ple_gate, m_w_ple_proj, v_attn_norm_g, v_w_in, v_q_norm_g, v_k_norm_g, v_ssm_conv_w, v_ssm_conv_b, v_dt_bias, v_a_log, v_d_skip, v_ssm_norm_g, v_w_out, v_ffn_norm_g, v_w_up, v_ffn_conv_w, v_ffn_conv_b, v_w_down, v_ple_norm_g, v_w_ple_gate, v_w_ple_proj):
    given = dict(locals())
    w2 = {n: given[n].reshape(given[n].shape[-2:]) if given[n].ndim == 3 else given[n] for n in WEIGHT_ORDER}

    cx, cy, cc = lax.axis_index("x"), lax.axis_index("y"), lax.axis_index("c")
    chip = 2 * cx + cy
    axis_of = dict(SHARDED)
    shard = lambda n: w2[n].astype(BF16) if n in MATRICES else w2[n]
    join = lambda n, g: _join_chips(lax.dynamic_update_index_in_dim(g, shard(n), chip, 0), axis_of[n])
    first = ("w_in", "ssm_conv_w", "ffn_conv_w")
    full = {n: join(n, g) for n, g in zip(first, _gather_over_chips([shard(n) for n in first]))}
    win = full["w_in"]
    w_in_p = jnp.concatenate([win[:, 2048:3584], win[:, 0:512], win[:, 1024:2048], win[:, 512:768], win[:, 768:1024],
                              win[:, 3584:3600], jnp.zeros((D_MODEL, PROJ_P - IN_PROJ), BF16)], axis=1)
    wts = {n: w2[n] for n in REPLICATED}
    wts.update(w_in_p=w_in_p, ssm_conv_w=full["ssm_conv_w"], ffn_conv_w=full["ffn_conv_w"])

    def join_late(gathered):
        late = {n: join(n, g) for n, g in zip(EARLY_REDUCED, gathered)}
        return dict(w_out_attn=late["w_out"][:ATTN_DIM], w_out_ssm=late["w_out"][ATTN_DIM:], w_up=late["w_up"],
                    w_down=late["w_down"], w_ple_gate=late["w_ple_gate"], w_ple_proj=late["w_ple_proj"])

    core = cc.astype(jnp.int32).reshape(1)
    idx = jnp.stack([chip, 2 * (1 - cx) + cy, 2 * cx + (1 - cy), 2 * (1 - cx) + (1 - cy), cc]).astype(jnp.int32)

    def major_of(names, gd):
        return [gd[n] if gd[n].ndim == 3 else _split_chips(gd[n], axis_of[n]) for n in names]

    def sums_of(names, major, got):
        return [_add_halves("grad_add_halves_" + n, g, r, core) for n, g, r in zip(names, major, got)]

    def w_in_sums(gd):
        gi = gd["w_in_p"]
        gd["w_in"] = jnp.concatenate([gi[:, OFF_Q:OFF_Q + ATTN_DIM], gi[:, OFF_K:OFF_K + KV_DIM], gi[:, OFF_V:OFF_V + KV_DIM],
                                      gi[:, OFF_Z:OFF_Z + SSM_INNER], gi[:, OFF_XBC:OFF_XBC + XBC_DIM], gi[:, OFF_DT:OFF_DT + SSM_HEADS]],
                                     axis=1)
        major = major_of(("w_in",), gd)
        return sums_of(("w_in",), major, _swap_halves("grad_swap_halves_late", major))

    sq, grad_x, grads, early, late = _local_step(
        x[0], p[0, 0], loss_target[0], wts, [shard(n) for n in EARLY_REDUCED], join_late,
        (functools.partial(major_of, EARLY_REDUCED), functools.partial(sums_of, EARLY_REDUCED)), w_in_sums)
    sums = dict(zip(EARLY_REDUCED + ("w_in",), list(zip(*early)) + list(zip(*late))))
    halves = [_sum_chips("grad_sum_chips_" + n, *sums[n], idx) for n in MATRICES]
    g_shard = dict(zip(MATRICES, _share_with_sibling(halves)))

    small_names = REPLICATED + ("ssm_conv_w", "ffn_conv_w")
    small_like = [grads[n] for n in small_names] + [jnp.zeros((1,), F32)]
    small = _allreduce_small(_pack_small([grads[n] for n in small_names] + [jnp.sum(sq).reshape(1)], ALL_SMALL_ROWS))
    small_vals = dict(zip(small_names + ("loss",), _unpack_small(small, small_like)))
    loss = (0.5 / D_MODEL) * small_vals["loss"][0]
    for n in ("ssm_conv_w", "ffn_conv_w"):
        cols = w2[n].shape[1]
        g_shard[n] = lax.dynamic_slice_in_dim(small_vals[n], chip * cols, cols, axis=1)[None]

    delta, new_m, new_v = {}, {}, {}
    for n, _ in SHARDED:
        if n == "w_in":
            r, c = w2[n].shape
            flat = lambda a: jnp.transpose(a.reshape(r, c)).reshape(1, r * c // LANE, LANE)
            back = lambda a: jnp.transpose(a.reshape(c, r)).reshape(1, r, c)
            outs = _adamw("adamw_" + n, flat(given[n]), flat(g_shard[n]), flat(given["m_" + n]), flat(given["v_" + n]))
            delta[n], new_m[n], new_v[n] = [back(o) for o in outs]
            continue
        delta[n], new_m[n], new_v[n] = _adamw("adamw_" + n, given[n], g_shard[n], given["m_" + n], given["v_" + n])
    packed = lambda prefix: _pack_small([given[prefix + n] for n in REPLICATED])[None]
    sm = _adamw("adamw_small", packed(""), _pack_small([small_vals[n] for n in REPLICATED])[None], packed("m_"), packed("v_"))
    for n in REPLICATED:
        g_shard[n] = small_vals[n]
    for dst, packed_out in zip((delta, new_m, new_v), sm):
        for n, val in zip(REPLICATED, _unpack_small(packed_out[0], [w2[n] for n in REPLICATED])):
            dst[n] = val

    def shaped(d):
        return [d[n].reshape(given[n].shape) for n in WEIGHT_ORDER]
    return (loss, grad_x[None], *shaped(g_shard), *shaped(delta), *shaped(new_m), *shaped(new_v))
```

```python
import functools

import jax
import jax.numpy as jnp
from jax import lax
from jax.experimental import pallas as pl
from jax.experimental.pallas import tpu as pltpu

F32 = jnp.float32
BF16 = jnp.bfloat16

D_MODEL = 1024
HEAD_DIM = 64
ATTN_DIM = 512
KV_DIM = 256
N_KV = 4
SSM_INNER = 1024
SSM_HEADS = 16
SSM_STATE = 128
BC_DIM = 256
XBC_DIM = SSM_INNER + 2 * BC_DIM
MIX_DIM = ATTN_DIM + SSM_INNER
IN_PROJ = 3600
D_FF = 2816
PLE_DIM = 256
CHUNK = 128
DILATIONS = (1, 4, 16)
EPS = 1e-6
ADAM_LR, ADAM_B1, ADAM_B2, ADAM_EPS, ADAM_WD, ADAM_STEP = 0.001, 0.9, 0.999, 1e-08, 0.01, 10

PROJ_P = 3712
OFF_XBC, OFF_Q, OFF_Z, OFF_K, OFF_V, OFF_DT = 0, 1536, 2048, 3072, 3328, 3584
LANE = 128
VMEM_LIMIT = 48 * 1024 * 1024
NEG = -1e30


def _cparams(sem):
    return pltpu.CompilerParams(dimension_semantics=sem, vmem_limit_bytes=VMEM_LIMIT)


def _sigmoid(x):
    return 1.0 / (1.0 + jnp.exp(-x))


def _dot(a, b):
    return jnp.dot(a, b, preferred_element_type=F32)


def _dot_nt(a, b):
    return lax.dot_general(a, b, (((1,), (1,)), ((), ())), preferred_element_type=F32)


def _dot_tn(a, b):
    return lax.dot_general(a, b, (((0,), (0,)), ((), ())), preferred_element_type=F32)


def _dot_split(x, m):
    hi = x.astype(BF16)
    lo = (x - hi.astype(F32)).astype(BF16)
    return _dot(hi, m) + _dot(lo, m)


def _rows(name, body, ins, outs, accs=(), tm=512):
    t_rows = next(s[1].shape[0] for s in ins if s[0] in ("t", "tc"))
    tm = min(tm, t_rows)
    in_specs, args = [], []
    for s in ins:
        if s[0] == "t":
            in_specs.append(pl.BlockSpec((tm, s[1].shape[1]), lambda i: (i, 0)))
        elif s[0] == "tc":
            in_specs.append(pl.BlockSpec((tm, s[2]), functools.partial(lambda i, c: (i, c), c=s[3])))
        else:
            in_specs.append(pl.BlockSpec(s[1].shape, lambda i: (0, 0)))
        args.append(s[1])
    out_shape = [jax.ShapeDtypeStruct((t_rows, w), dt) for w, dt in outs]
    out_specs = [pl.BlockSpec((tm, w), lambda i: (i, 0)) for w, _ in outs]
    out_shape += [jax.ShapeDtypeStruct(a, F32) for a in accs]
    out_specs += [pl.BlockSpec(a, lambda i: (0, 0)) for a in accs]
    n_acc = len(accs)

    def kern(*refs):
        if n_acc:
            @pl.when(pl.program_id(0) == 0)
            def _():
                for r in refs[len(refs) - n_acc:]:
                    r[...] = jnp.zeros(r.shape, F32)
        body(*refs)

    return pl.pallas_call(
        kern, name=name, grid=(t_rows // tm,), in_specs=in_specs, out_specs=out_specs, out_shape=out_shape,
        compiler_params=_cparams(("arbitrary",) if n_acc else ("parallel",)))(*args)


NCHUNK = 512


def _col_chunks(n):
    return [(c, min(NCHUNK, n - c)) for c in range(0, n, NCHUNK)]


def _mm_nt(name, pairs, out_dtype, tm=512, tn=None):
    m, n = pairs[0][0].shape[-2], pairs[0][1].shape[0]
    tn = n if tn is None else tn
    tm = min(tm, m)
    np_ = len(pairs)
    in_specs, args = [], []
    for a, w, kb, *lead in pairs:
        if lead:
            in_specs.append(pl.BlockSpec((None, tm, a.shape[2]), functools.partial(lambda j, i, ld: (ld, i, 0), ld=lead[0])))
        else:
            in_specs.append(pl.BlockSpec((tm, a.shape[1]), lambda j, i: (i, 0)))
        in_specs.append(pl.BlockSpec((tn, a.shape[-1]), functools.partial(lambda j, i, kb: (j, kb), kb=kb)))
        args += [a, w]

    def kern(*refs):
        o_ref = refs[-1]
        for c0, cw in _col_chunks(tn):
            acc = None
            for q in range(np_):
                d = _dot_nt(refs[2 * q][...], refs[2 * q + 1][c0:c0 + cw, :])
                acc = d if acc is None else acc + d
            o_ref[:, c0:c0 + cw] = acc.astype(o_ref.dtype)

    return pl.pallas_call(
        kern, name=name, grid=(n // tn, m // tm), in_specs=in_specs,
        out_specs=pl.BlockSpec((tm, tn), lambda j, i: (i, j)),
        out_shape=jax.ShapeDtypeStruct((m, n), out_dtype), compiler_params=_cparams(("parallel", "parallel")))(*args)


def _mm_tn(name, a, b, tm=None, tn=None, tk=1024, chip_cols=False):
    t, m = a.shape
    n = b.shape[-1] * (2 if b.ndim == 3 else 1)
    tm = m if tm is None else tm
    tn = n if tn is None else tn
    tk = min(tk, t)
    if b.ndim == 3:
        per = n // 2 // tn
        b_spec = pl.BlockSpec((None, tk, tn), lambda i, j, k: (j // per, k, j % per))
    else:
        b_spec = pl.BlockSpec((tk, tn), lambda i, j, k: (k, j))
    if chip_cols:
        out_spec = pl.BlockSpec((None, tm, tn), lambda i, j, k: (j, i, 0))
        out_shape = jax.ShapeDtypeStruct((n // tn, m, tn), F32)
    else:
        out_spec = pl.BlockSpec((tm, tn), lambda i, j, k: (i, j))
        out_shape = jax.ShapeDtypeStruct((m, n), F32)

    def kern(a_ref, b_ref, o_ref):
        @pl.when(pl.program_id(2) == 0)
        def _():
            o_ref[...] = jnp.zeros(o_ref.shape, F32)
        for c0, cw in _col_chunks(tn):
            o_ref[:, c0:c0 + cw] += _dot_tn(a_ref[...], b_ref[:, c0:c0 + cw])

    return pl.pallas_call(
        kern, name=name, grid=(m // tm, n // tn, t // tk),
        in_specs=[pl.BlockSpec((tk, tm), lambda i, j, k: (k, i)), b_spec], out_specs=out_spec, out_shape=out_shape,
        compiler_params=_cparams(("parallel", "parallel", "arbitrary")))(a, b)


def _rms_bwd(name, dh, x, g, dres):
    d = x.shape[1]

    def body(dh_ref, x_ref, g_ref, dres_ref, dx_ref, dxb_ref, dg_ref):
        xv, dhv = x_ref[...], dh_ref[...]
        r = lax.rsqrt(jnp.mean(xv * xv, axis=-1, keepdims=True) + EPS)
        gd = dhv * g_ref[...]
        dx = dres_ref[...] + r * gd - xv * (r * r * r * jnp.mean(xv * gd, axis=-1, keepdims=True))
        dx_ref[...] = dx
        dxb_ref[...] = dx.astype(BF16)
        dg_ref[...] += jnp.sum(dhv * xv * r, axis=0, keepdims=True)
    return _rows(name, body, [("t", dh), ("t", x), ("p", g), ("t", dres)], [(d, F32), (d, BF16)], accs=[(1, d)])


def _norm_mm(name, x, g, w, tm=512, tn=None, halves=False):
    m, k = x.shape
    n = w.shape[1]
    tn = n if tn is None else tn
    if halves:
        per = n // 2 // tn
        o_spec = pl.BlockSpec((None, tm, tn), lambda i, j: (j // per, i, j % per))
        o_shape = jax.ShapeDtypeStruct((2, m, n // 2), F32)
    else:
        o_spec = pl.BlockSpec((tm, tn), lambda i, j: (i, j))
        o_shape = jax.ShapeDtypeStruct((m, n), F32)

    def kern(x_ref, g_ref, w_ref, h_ref, o_ref):
        xv = x_ref[...]
        h = (xv * lax.rsqrt(jnp.mean(xv * xv, axis=-1, keepdims=True) + EPS) * g_ref[...]).astype(BF16)
        h_ref[...] = h
        for c0, cw in _col_chunks(tn):
            o_ref[:, c0:c0 + cw] = _dot(h, w_ref[:, c0:c0 + cw])

    return pl.pallas_call(
        kern, name=name, grid=(m // tm, n // tn),
        in_specs=[pl.BlockSpec((tm, k), lambda i, j: (i, 0)), pl.BlockSpec((1, k), lambda i, j: (0, 0)),
                  pl.BlockSpec((k, tn), lambda i, j: (0, j))],
        out_specs=[pl.BlockSpec((tm, k), lambda i, j: (i, 0)), o_spec],
        out_shape=[jax.ShapeDtypeStruct((m, k), BF16), o_shape],
        compiler_params=_cparams(("parallel", "arbitrary")))(x, g, w)


def _ple_head(x2, g, w_gate, pb, w_proj, tgt, tm=512):
    m, d = x2.shape

    def kern(x_ref, g_ref, wg_ref, p_ref, wp_ref, t_ref, h_ref, dy_ref, dgl_ref, dpp_ref, sq_ref):
        @pl.when(pl.program_id(0) == 0)
        def _():
            sq_ref[...] = jnp.zeros(sq_ref.shape, F32)
        xv = x_ref[...]
        h = (xv * lax.rsqrt(jnp.mean(xv * xv, axis=-1, keepdims=True) + EPS) * g_ref[...]).astype(BF16)
        h_ref[...] = h
        pv = p_ref[...]
        for c0, cw in _col_chunks(d):
            cs = slice(c0, c0 + cw)
            s = _sigmoid(_dot(h, wg_ref[:, cs]))
            ppv = _dot(pv, wp_ref[:, cs])
            diff = x_ref[:, cs] + s * ppv - t_ref[:, cs]
            dy = diff * (1.0 / d)
            dy_ref[:, cs] = dy
            dgl_ref[:, cs] = (dy * ppv * s * (1.0 - s)).astype(BF16)
            dpp_ref[:, cs] = (dy * s).astype(BF16)
            sq_ref[:, cs] += jnp.sum(diff * diff, axis=0, keepdims=True)

    row = lambda width: pl.BlockSpec((tm, width), lambda i: (i, 0))
    full = lambda a: pl.BlockSpec(a.shape, lambda i: (0, 0))
    return pl.pallas_call(
        kern, name="ple_head", grid=(m // tm,),
        in_specs=[row(d), full(g), full(w_gate), row(pb.shape[1]), full(w_proj), row(d)],
        out_specs=[row(d), row(d), row(d), row(d), pl.BlockSpec((1, d), lambda i: (0, 0))],
        out_shape=[jax.ShapeDtypeStruct((m, d), BF16), jax.ShapeDtypeStruct((m, d), F32), jax.ShapeDtypeStruct((m, d), BF16),
                   jax.ShapeDtypeStruct((m, d), BF16), jax.ShapeDtypeStruct((1, d), F32)],
        compiler_params=_cparams(("arbitrary",)))(x2, g, w_gate, pb, w_proj, tgt)


def _mix_out_proj(y, proj, g, w_ssm, os_, lses, w_attn, x, tm=256):
    m, d = y.shape

    def kern(y_ref, z_ref, g_ref, ws_ref, o1, o2, o3, l1, l2, l3, wa_ref, x_ref, s_ref, a_ref, lse_ref, o_ref):
        z = z_ref[...]
        yz = y_ref[...] * (z * _sigmoid(z))
        s = (yz * lax.rsqrt(jnp.mean(yz * yz, axis=-1, keepdims=True) + EPS) * g_ref[...]).astype(BF16)
        s_ref[...] = s
        pieces = []
        for kh in range(N_KV):
            a, b, c = l1[kh], l2[kh], l3[kh]
            mx = jnp.maximum(jnp.maximum(a, b), c)
            tot = mx + jnp.log(jnp.exp(a - mx) + jnp.exp(b - mx) + jnp.exp(c - mx))
            lse_ref[kh] = tot
            wa, wb, wc = jnp.exp(a - tot), jnp.exp(b - tot), jnp.exp(c - tot)
            for g_ in range(2):
                h = 2 * kh + g_
                acc = wa[:, g_:g_ + 1] * o1[h] + wb[:, g_:g_ + 1] * o2[h] + wc[:, g_:g_ + 1] * o3[h]
                pieces.append(acc[:, HEAD_DIM:])
        av = jnp.concatenate(pieces, axis=1).astype(BF16)
        a_ref[...] = av
        for c0, cw in _col_chunks(d):
            cs = slice(c0, c0 + cw)
            o_ref[:, cs] = x_ref[:, cs] + _dot(s, ws_ref[:, cs]) + _dot(av, wa_ref[:, cs])

    row = lambda width: pl.BlockSpec((tm, width), lambda i: (i, 0))
    full = lambda a: pl.BlockSpec(a.shape, lambda i: (0, 0))
    blk = lambda heads: pl.BlockSpec((heads, tm, LANE), lambda i: (0, i, 0))
    return pl.pallas_call(
        kern, name="out_proj", grid=(m // tm,),
        in_specs=[row(d), pl.BlockSpec((tm, d), lambda i: (i, OFF_Z // SSM_INNER)), full(g), full(w_ssm)] + [blk(N_QH)] * 3
        + [blk(N_KV)] * 3 + [full(w_attn), row(d)],
        out_specs=[row(d), row(ATTN_DIM), blk(N_KV), row(d)],
        out_shape=[jax.ShapeDtypeStruct((m, d), BF16), jax.ShapeDtypeStruct((m, ATTN_DIM), BF16),
                   jax.ShapeDtypeStruct((N_KV, m, LANE), F32), jax.ShapeDtypeStruct((m, d), F32)],
        compiler_params=_cparams(("parallel",)))(y, proj, g, w_ssm, *os_, *lses, w_attn, x)


def _mm_nt_rms_bwd(name, a, w, x, g, dres, parts=(), tm=512):
    m, k = a.shape
    n = w.shape[0]
    ns, steps = len(parts), m // tm

    def kern(a_ref, w_ref, x_ref, g_ref, dres_ref, *rest):
        dx_ref, dxb_ref, dg_ref = rest[ns:ns + 3]
        dh = rest[2 * ns + 3]
        if ns:
            start, finish = _scatter_steps(ns, rest[:ns], rest[ns + 3:2 * ns + 3], rest[2 * ns + 4:])
            pl.when(pl.program_id(0) == 0)(start)
            pl.when(pl.program_id(0) == steps - 1)(finish)

        @pl.when(pl.program_id(0) == 0)
        def _():
            dg_ref[...] = jnp.zeros(dg_ref.shape, F32)
        av = a_ref[...]
        for c0, cw in _col_chunks(n):
            dh[:, c0:c0 + cw] = _dot_nt(av, w_ref[c0:c0 + cw, :])
        xv, dhv = x_ref[...], dh[...]
        r = lax.rsqrt(jnp.mean(xv * xv, axis=-1, keepdims=True) + EPS)
        gd = dhv * g_ref[...]
        dx = dres_ref[...] + r * gd - xv * (r * r * r * jnp.mean(xv * gd, axis=-1, keepdims=True))
        dx_ref[...] = dx
        dxb_ref[...] = dx.astype(BF16)
        dg_ref[...] += jnp.sum(dhv * xv * r, axis=0, keepdims=True)

    row = lambda width: pl.BlockSpec((tm, width), lambda i: (i, 0))
    return pl.pallas_call(
        kern, name=name, grid=(steps,),
        in_specs=[row(k), pl.BlockSpec((n, k), lambda i: (0, 0)), row(n), pl.BlockSpec((1, n), lambda i: (0, 0)), row(n)]
        + [ANY_SPEC] * ns,
        out_specs=[row(n), row(n), pl.BlockSpec((1, n), lambda i: (0, 0))] + [ANY_SPEC] * ns,
        out_shape=[jax.ShapeDtypeStruct((m, n), F32), jax.ShapeDtypeStruct((m, n), BF16), jax.ShapeDtypeStruct((1, n), F32)]
        + [jax.ShapeDtypeStruct(s.shape, s.dtype) for s in parts],
        scratch_shapes=[pltpu.VMEM((tm, n), F32)] + (_scatter_sems(ns) if ns else []),
        compiler_params=_cparams(("arbitrary",)))(a, w, x, g, dres, *parts)


def _head_mean_matrix(width):
    i = jnp.arange(width) // HEAD_DIM
    return jnp.where(i[:, None] == i[None, :], 1.0 / HEAD_DIM, 0.0).astype(BF16)


ATT_SPAN = 2048
N_QH = 8


def _lane_lo(rows):
    return lax.broadcasted_iota(jnp.int32, (rows, LANE), 1) < HEAD_DIM


def _swap_halves_lanes(x):
    return pltpu.roll(x, HEAD_DIM, axis=1)


def _head_major_qkv(qn, kn, v, qo_ref, kvo_ref):
    lo = _lane_lo(qn.shape[0])
    for j in range(N_KV):
        blk = qn[:, j * LANE:(j + 1) * LANE]
        qo_ref[2 * j] = jnp.where(lo, blk, 0.0)
        qo_ref[2 * j + 1] = jnp.where(lo, _swap_halves_lanes(blk), 0.0)
    for j in range(2):
        kb, vb = kn[:, j * LANE:(j + 1) * LANE], v[:, j * LANE:(j + 1) * LANE]
        kvo_ref[2 * j] = jnp.where(lo, kb, _swap_halves_lanes(vb))
        kvo_ref[2 * j + 1] = jnp.where(lo, _swap_halves_lanes(kb), vb)


def _in_proj(x, g, w, gq_t, gk_t, tm=512):
    m, k = x.shape
    n = w.shape[1]
    bq, bk = _head_mean_matrix(ATTN_DIM), _head_mean_matrix(KV_DIM)
    scale = HEAD_DIM ** -0.5

    def kern(x_ref, g_ref, w_ref, gq_ref, gk_ref, bq_ref, bk_ref, h_ref, o_ref, qo_ref, kvo_ref):
        xv = x_ref[...]
        h = (xv * lax.rsqrt(jnp.mean(xv * xv, axis=-1, keepdims=True) + EPS) * g_ref[...]).astype(BF16)
        h_ref[...] = h
        for c0, cw in _col_chunks(n):
            o_ref[:, c0:c0 + cw] = _dot(h, w_ref[:, c0:c0 + cw])
        q, kk, v = o_ref[:, OFF_Q:OFF_Q + ATTN_DIM], o_ref[:, OFF_K:OFF_K + KV_DIM], o_ref[:, OFF_V:OFF_V + KV_DIM]
        qn = (q * lax.rsqrt(_dot_split(q * q, bq_ref[...]) + EPS) * gq_ref[...]) * scale
        kn = kk * lax.rsqrt(_dot_split(kk * kk, bk_ref[...]) + EPS) * gk_ref[...]
        _head_major_qkv(qn, kn, v, qo_ref, kvo_ref)

    row = lambda width: pl.BlockSpec((tm, width), lambda i: (i, 0))
    full = lambda a: pl.BlockSpec(a.shape, lambda i: (0, 0))
    blk = lambda heads: pl.BlockSpec((heads, tm, LANE), lambda i: (0, i, 0))
    return pl.pallas_call(
        kern, name="in_proj", grid=(m // tm,),
        in_specs=[row(k), full(g), full(w), full(gq_t), full(gk_t), full(bq), full(bk)],
        out_specs=[row(k), row(n), blk(N_QH), blk(N_KV)],
        out_shape=[jax.ShapeDtypeStruct((m, k), BF16), jax.ShapeDtypeStruct((m, n), F32),
                   jax.ShapeDtypeStruct((N_QH, m, LANE), F32), jax.ShapeDtypeStruct((N_KV, m, LANE), F32)],
        compiler_params=_cparams(("parallel",)))(x, g, w, gq_t, gk_t, bq, bk)


def _d_mix(dx1b, w_cat, y, proj, g, attn_out, tm=512):
    m, k = dx1b.shape
    n = w_cat.shape[0]

    def kern(a_ref, w_ref, y_ref, z_ref, g_ref, o_ref, dy_ref, dz_ref, dg_ref, dot_ref, d_ref, dmix):
        @pl.when(pl.program_id(0) == 0)
        def _():
            dg_ref[...] = jnp.zeros(dg_ref.shape, F32)
        av = a_ref[...]
        for c0, cw in _col_chunks(n):
            dmix[:, c0:c0 + cw] = _dot_nt(av, w_ref[c0:c0 + cw, :])
        z, yv, dout = z_ref[...], y_ref[...], dmix[:, :SSM_INNER]
        sg = _sigmoid(z)
        gz = z * sg
        yz = yv * gz
        r = lax.rsqrt(jnp.mean(yz * yz, axis=-1, keepdims=True) + EPS)
        gd = dout * g_ref[...]
        dyz = r * gd - yz * (r * r * r * jnp.mean(yz * gd, axis=-1, keepdims=True))
        dy_ref[...] = dyz * gz
        dz_ref[...] = (dyz * yv * (sg * (1.0 + z * (1.0 - sg)))).astype(BF16)
        dg_ref[...] += jnp.sum(dout * yz * r, axis=0, keepdims=True)
        do = dmix[:, SSM_INNER:]
        prod = do * o_ref[...].astype(F32)
        lo = _lane_lo(tm)
        lane = lax.broadcasted_iota(jnp.int32, (tm, LANE), 1)
        for kh in range(N_KV):
            blk, pb = do[:, kh * LANE:(kh + 1) * LANE], prod[:, kh * LANE:(kh + 1) * LANE]
            dot_ref[2 * kh] = jnp.where(lo, 0.0, _swap_halves_lanes(blk))
            dot_ref[2 * kh + 1] = jnp.where(lo, 0.0, blk)
            s_lo = jnp.sum(jnp.where(lo, pb, 0.0), axis=1, keepdims=True)
            s_hi = jnp.sum(pb, axis=1, keepdims=True) - s_lo
            d_ref[kh] = jnp.where(lane == 0, s_lo, jnp.where(lane == 1, s_hi, 0.0))

    row = lambda width: pl.BlockSpec((tm, width), lambda i: (i, 0))
    full = lambda a: pl.BlockSpec(a.shape, lambda i: (0, 0))
    blk = lambda heads: pl.BlockSpec((heads, tm, LANE), lambda i: (0, i, 0))
    return pl.pallas_call(
        kern, name="d_mix", grid=(m // tm,),
        in_specs=[row(k), full(w_cat), row(SSM_INNER), pl.BlockSpec((tm, SSM_INNER), lambda i: (i, OFF_Z // SSM_INNER)), full(g),
                  row(ATTN_DIM)],
        out_specs=[row(SSM_INNER), row(SSM_INNER), pl.BlockSpec((1, SSM_INNER), lambda i: (0, 0)), blk(N_QH), blk(N_KV)],
        out_shape=[jax.ShapeDtypeStruct((m, SSM_INNER), F32), jax.ShapeDtypeStruct((m, SSM_INNER), BF16),
                   jax.ShapeDtypeStruct((1, SSM_INNER), F32), jax.ShapeDtypeStruct((N_QH, m, LANE), F32),
                   jax.ShapeDtypeStruct((N_KV, m, LANE), F32)],
        scratch_shapes=[pltpu.VMEM((tm, n), F32)], compiler_params=_cparams(("arbitrary",)))(dx1b, w_cat, y, proj, g, attn_out)


def _qknorm_bwd2(proj, gq_t, gk_t, dqs, dkvs, tm=256):
    t = proj.shape[0]
    bq, bk = _head_mean_matrix(ATTN_DIM), _head_mean_matrix(KV_DIM)
    scale = HEAD_DIM ** -0.5

    def kern(q_ref, k_ref, gq_ref, gk_ref, bq_ref, bk_ref, a1, a2, a3, b1, b2, b3, dq_ref, dk_ref, dv_ref, dgq_ref, dgk_ref):
        @pl.when(pl.program_id(0) == 0)
        def _():
            dgq_ref[...] = jnp.zeros(dgq_ref.shape, F32)
            dgk_ref[...] = jnp.zeros(dgk_ref.shape, F32)
        lo = _lane_lo(tm)
        sq = [a1[h] + a2[h] + a3[h] for h in range(N_QH)]
        skv = [b1[h] + b2[h] + b3[h] for h in range(N_KV)]
        dqn = jnp.concatenate([jnp.where(lo, sq[2 * j], _swap_halves_lanes(sq[2 * j + 1])) for j in range(N_KV)], axis=1) * scale
        dkn = jnp.concatenate([jnp.where(lo, skv[2 * j], _swap_halves_lanes(skv[2 * j + 1])) for j in range(2)], axis=1)
        dv = jnp.concatenate([jnp.where(lo, _swap_halves_lanes(skv[2 * j]), skv[2 * j + 1]) for j in range(2)], axis=1)
        q, k = q_ref[...], k_ref[...]
        rq = lax.rsqrt(_dot_split(q * q, bq_ref[...]) + EPS)
        rk = lax.rsqrt(_dot_split(k * k, bk_ref[...]) + EPS)
        gdq, gdk = dqn * gq_ref[...], dkn * gk_ref[...]
        dq_ref[...] = (rq * gdq - q * (rq * rq * rq * _dot_split(q * gdq, bq_ref[...]))).astype(BF16)
        dk_ref[...] = (rk * gdk - k * (rk * rk * rk * _dot_split(k * gdk, bk_ref[...]))).astype(BF16)
        dv_ref[...] = dv.astype(BF16)
        dgq_ref[...] += jnp.sum(dqn * q * rq, axis=0, keepdims=True)
        dgk_ref[...] += jnp.sum(dkn * k * rk, axis=0, keepdims=True)

    col = lambda w, idx: pl.BlockSpec((tm, w), functools.partial(lambda i, idx: (i, idx), idx=idx))
    par = lambda a: pl.BlockSpec(a.shape, lambda i: (0, 0))
    blk = lambda n: pl.BlockSpec((n, tm, LANE), lambda i: (0, i, 0))
    row = lambda w: pl.BlockSpec((tm, w), lambda i: (i, 0))
    acc = lambda w: pl.BlockSpec((1, w), lambda i: (0, 0))
    return pl.pallas_call(
        kern, name="qknorm_bwd", grid=(t // tm,),
        in_specs=[col(ATTN_DIM, OFF_Q // ATTN_DIM), col(KV_DIM, OFF_K // KV_DIM), par(gq_t), par(gk_t), par(bq), par(bk)]
        + [blk(N_QH)] * 3 + [blk(N_KV)] * 3,
        out_specs=[row(ATTN_DIM), row(KV_DIM), row(KV_DIM), acc(ATTN_DIM), acc(KV_DIM)],
        out_shape=[jax.ShapeDtypeStruct((t, ATTN_DIM), BF16), jax.ShapeDtypeStruct((t, KV_DIM), BF16),
                   jax.ShapeDtypeStruct((t, KV_DIM), BF16), jax.ShapeDtypeStruct((1, ATTN_DIM), F32),
                   jax.ShapeDtypeStruct((1, KV_DIM), F32)],
        compiler_params=_cparams(("arbitrary",)))(proj, proj, gq_t, gk_t, bq, bk, *dqs, *dkvs)


def _att_rows(b, r, dil):
    if dil == 1:
        return pl.ds(b * CHUNK, CHUNK)
    return pl.ds(b * CHUNK * dil + r, CHUNK, stride=dil)


def _for_residues(dil, unit):
    for r in range(dil):
        unit(r, 0)


def _band_qk(first):
    ri = lax.broadcasted_iota(jnp.int32, (CHUNK, 2 * CHUNK), 0)
    cj = lax.broadcasted_iota(jnp.int32, (CHUNK, 2 * CHUNK), 1)
    band = (cj - ri >= 0) & (cj - ri <= CHUNK)
    return band if first is None else band & (jnp.logical_not(first) | (cj >= CHUNK))


def _band_kq(last):
    rj = lax.broadcasted_iota(jnp.int32, (CHUNK, 2 * CHUNK), 0)
    ci = lax.broadcasted_iota(jnp.int32, (CHUNK, 2 * CHUNK), 1)
    band = (ci - rj >= 0) & (ci - rj <= CHUNK)
    return band if last is None else band & (jnp.logical_not(last) | (ci < CHUNK))


def _att_specs(t, dil):
    sub = CHUNK * dil
    nb, last = ATT_SPAN // sub, t // sub - 1
    cur = lambda heads: pl.BlockSpec((heads, ATT_SPAN, LANE), lambda kh, n: (kh, n, 0))
    prev = lambda heads: pl.BlockSpec((heads, sub, LANE), lambda kh, n: (kh, jnp.maximum(n * nb - 1, 0), 0))
    nxt = lambda heads: pl.BlockSpec((heads, sub, LANE), lambda kh, n: (kh, jnp.minimum((n + 1) * nb, last), 0))
    return sub, nb, cur, prev, nxt


def _attn_fwd2(q, kv, dil):
    t = q.shape[1]
    sub, nb, cur, prev, _ = _att_specs(t, dil)

    def kern(q_ref, kvp_ref, kvc_ref, o_ref, lse_ref):
        n = pl.program_id(1)
        lane = lax.broadcasted_iota(jnp.int32, (CHUNK, LANE), 1)
        for b in range(nb):
            mask = _band_qk((n == 0) if b == 0 else None)

            def unit(r, carry, b=b, mask=mask):
                rows = _att_rows(b, r, dil)
                kvp = kvc_ref[_att_rows(b - 1, r, dil), :] if b > 0 else kvp_ref[_att_rows(0, r, dil), :]
                kvcat = jnp.concatenate([kvp, kvc_ref[rows, :]], axis=0).astype(BF16)
                lse_tile = jnp.zeros((CHUNK, LANE), F32)
                for g in range(2):
                    s = jnp.where(mask, _dot_nt(q_ref.at[g][rows, :].astype(BF16), kvcat), NEG)
                    m = jnp.max(s, axis=1, keepdims=True)
                    p = jnp.exp(s - m)
                    l = jnp.sum(p, axis=1, keepdims=True)
                    o_ref.at[g][rows, :] = _dot(p.astype(BF16), kvcat) * (1.0 / l)
                    lse_tile = jnp.where(lane == g, m + jnp.log(l), lse_tile)
                lse_ref[rows, :] = lse_tile
                return carry
            _for_residues(dil, unit)

    return pl.pallas_call(
        kern, name=f"attn_fwd_d{dil}", grid=(N_KV, t // ATT_SPAN), in_specs=[cur(2), prev(None), cur(None)],
        out_specs=[cur(2), cur(None)],
        out_shape=[jax.ShapeDtypeStruct((N_QH, t, LANE), F32), jax.ShapeDtypeStruct((N_KV, t, LANE), F32)],
        compiler_params=_cparams(("parallel", "parallel")))(q, kv, kv)


def _attn_dq2(q, kv, dot, lse, dsum, dil):
    t = q.shape[1]
    sub, nb, cur, prev, _ = _att_specs(t, dil)

    def kern(q_ref, kvp_ref, kvc_ref, do_ref, lse_ref, d_ref, dq_ref):
        n = pl.program_id(1)
        for b in range(nb):
            mask = _band_qk((n == 0) if b == 0 else None)

            def unit(r, carry, b=b, mask=mask):
                rows = _att_rows(b, r, dil)
                kvp = kvc_ref[_att_rows(b - 1, r, dil), :] if b > 0 else kvp_ref[_att_rows(0, r, dil), :]
                kvcat = jnp.concatenate([kvp, kvc_ref[rows, :]], axis=0).astype(BF16)
                lse_t, d_t = lse_ref[rows, :], d_ref[rows, :]
                for g in range(2):
                    s = jnp.where(mask, _dot_nt(q_ref.at[g][rows, :].astype(BF16), kvcat), NEG)
                    p = jnp.exp(s - lse_t[:, g:g + 1])
                    dp = _dot_nt(do_ref.at[g][rows, :].astype(BF16), kvcat)
                    ds = p * (dp - d_t[:, g:g + 1])
                    dq_ref.at[g][rows, :] = _dot(ds.astype(BF16), kvcat)
                return carry
            _for_residues(dil, unit)

    return pl.pallas_call(
        kern, name=f"attn_dq_d{dil}", grid=(N_KV, t // ATT_SPAN),
        in_specs=[cur(2), prev(None), cur(None), cur(2), cur(None), cur(None)], out_specs=cur(2),
        out_shape=jax.ShapeDtypeStruct((N_QH, t, LANE), F32),
        compiler_params=_cparams(("parallel", "parallel")))(q, kv, kv, dot, lse, dsum)


def _attn_dkv2(q, kv, dot, lse, dsum, dil):
    t = q.shape[1]
    sub, nb, cur, _, nxt = _att_specs(t, dil)
    nsteps = t // ATT_SPAN

    def kern(kv_ref, qc_ref, qn_ref, doc_ref, don_ref, lc_ref, ln_ref, dc_ref, dn_ref, dkv_ref):
        n = pl.program_id(1)
        for b in range(nb):
            inside = b < nb - 1
            mask = _band_kq(None if inside else (n == nsteps - 1))

            def unit(r, carry, b=b, inside=inside, mask=mask):
                rows = _att_rows(b, r, dil)
                nrows = _att_rows(b + 1, r, dil) if inside else _att_rows(0, r, dil)
                kvb = kv_ref[rows, :].astype(BF16)
                follow = lambda cref, nref: (cref if inside else nref)[nrows, :]
                lse_t = jnp.concatenate([lc_ref[rows, :].T, follow(lc_ref, ln_ref).T], axis=1)
                d_t = jnp.concatenate([dc_ref[rows, :].T, follow(dc_ref, dn_ref).T], axis=1)
                acc = jnp.zeros((CHUNK, LANE), F32)
                for g in range(2):
                    qdo = jnp.concatenate([qc_ref.at[g][rows, :], follow(qc_ref.at[g], qn_ref.at[g]),
                                           doc_ref.at[g][rows, :], follow(doc_ref.at[g], don_ref.at[g])], axis=0).astype(BF16)
                    both = _dot_nt(kvb, qdo)
                    pt = jnp.exp(jnp.where(mask, both[:, :2 * CHUNK], NEG) - lse_t[g:g + 1, :])
                    dst = pt * (both[:, 2 * CHUNK:] - d_t[g:g + 1, :])
                    acc = acc + _dot(jnp.concatenate([dst, pt], axis=1).astype(BF16), qdo)
                dkv_ref[rows, :] = acc
                return carry
            _for_residues(dil, unit)

    return pl.pallas_call(
        kern, name=f"attn_dkv_d{dil}", grid=(N_KV, nsteps),
        in_specs=[cur(None), cur(2), nxt(2), cur(2), nxt(2), cur(None), nxt(None), cur(None), nxt(None)], out_specs=cur(None),
        out_shape=jax.ShapeDtypeStruct((N_KV, t, LANE), F32),
        compiler_params=_cparams(("parallel", "parallel")))(kv, q, q, dot, dot, lse, lse, dsum, dsum)


HALO = 8
SSM_CONV_TM, SSM_CONV_W = 512, 512
FFN_CONV_TM, FFN_CONV_W = 256, 1408


def _halo_specs(tm, width, t_rows, col_off=0, lead=None):
    per, last = tm // HALO, t_rows // HALO - 1
    row_maps = (lambda i: i, lambda i: jnp.maximum(i * per - 1, 0), lambda i: jnp.minimum((i + 1) * per, last))
    specs = []
    for rows, rm in zip((tm, HALO, HALO), row_maps):
        if lead is None:
            specs.append(pl.BlockSpec((rows, width), functools.partial(lambda c, i, rm: (rm(i), c + col_off), rm=rm)))
        else:
            specs.append(pl.BlockSpec((None, rows, width), functools.partial(lambda c, i, rm: (lead, rm(i), c + col_off), rm=rm)))
    return specs


def _fill_ext(buf, tile_ref, before_ref, after_ref, i, nt):
    tm = tile_ref.shape[0]
    buf[0:HALO, :] = jnp.where(i > 0, before_ref[...].astype(F32), 0.0)
    buf[HALO:HALO + tm, :] = tile_ref[...].astype(F32)
    if after_ref is not None:
        buf[HALO + tm:, :] = jnp.where(i < nt - 1, after_ref[...].astype(F32), 0.0)


CONV_RB, CONV_CW = 16, 256


def _lane_chunks(width):
    return [slice(c0, min(c0 + CONV_CW, width)) for c0 in range(0, width, CONV_CW)]


def _shifted(buf, taps, r0, rows, cs):
    return [buf[pl.ds(HALO - (taps - 1) + k + r0, rows), cs] for k in range(taps)]


def _taps_fwd(xs, w, b):
    acc = b
    for k, xk in enumerate(xs):
        acc = acc + w[k:k + 1, :] * xk
    return acc


def _taps_bwd(bufd, w, taps, r0, rows, cs):
    acc = None
    for k in range(taps):
        term = w[k:k + 1, :] * bufd[pl.ds(r0 + (taps - 1) - k, rows), cs]
        acc = term if acc is None else acc + term
    return acc


def _fold8(z):
    return z[:HALO] + z[HALO:] if z.shape[0] == 2 * HALO else z


def _silu_grad(pre):
    sg = _sigmoid(pre)
    return sg * (1.0 + pre * (1.0 - sg))


def _ssm_conv_fwd(proj, w, b):
    t = proj.shape[0]
    tm, wd = min(SSM_CONV_TM, t), SSM_CONV_W
    nt, taps = t // tm, w.shape[0]

    def kern(x_ref, xb_ref, w_ref, b_ref, o_ref, buf):
        _fill_ext(buf, x_ref, xb_ref, None, pl.program_id(1), nt)
        for cs in _lane_chunks(wd):
            wv, bv = w_ref[:, cs], b_ref[:, cs]
            for r0 in range(0, tm, CONV_RB):
                pre = _taps_fwd(_shifted(buf, taps, r0, CONV_RB, cs), wv, bv)
                o_ref[r0:r0 + CONV_RB, cs] = pre * _sigmoid(pre)

    tile, before, _ = _halo_specs(tm, wd, t)
    par = lambda rows: pl.BlockSpec((rows, wd), lambda c, i: (0, c))
    return pl.pallas_call(
        kern, name="ssm_conv_fwd", grid=(XBC_DIM // wd, nt), in_specs=[tile, before, par(taps), par(1)],
        out_specs=pl.BlockSpec((tm, wd), lambda c, i: (i, c)), out_shape=jax.ShapeDtypeStruct((t, XBC_DIM), F32),
        scratch_shapes=[pltpu.VMEM((tm + HALO, wd), F32)],
        compiler_params=_cparams(("parallel", "parallel")))(proj, proj, w, b)


def _ssm_conv_bwd(proj, w, b, dact, parts):
    t = proj.shape[0]
    tm, wd = min(SSM_CONV_TM, t), SSM_CONV_W
    nt, taps, ncol, ns = t // tm, w.shape[0], XBC_DIM // SSM_CONV_W, len(parts)

    def kern(x_ref, xb_ref, xa_ref, d_ref, dn_ref, w_ref, b_ref, *rest):
        dx_ref, gw_ref, gb_ref = rest[ns:ns + 3]
        buf, bufd = rest[2 * ns + 3:2 * ns + 5]
        i = pl.program_id(1)
        if ns:
            start, finish = _scatter_steps(ns, rest[:ns], rest[ns + 3:2 * ns + 3], rest[2 * ns + 5:])
            pl.when((pl.program_id(0) == 0) & (i == 0))(start)
            pl.when((pl.program_id(0) == ncol - 1) & (i == nt - 1))(finish)
        _fill_ext(buf, x_ref, xb_ref, xa_ref, i, nt)

        @pl.when(i == 0)
        def _():
            gw_ref[...] = jnp.zeros(gw_ref.shape, F32)
            gb_ref[...] = jnp.zeros(gb_ref.shape, F32)
        for cs in _lane_chunks(wd):
            wv, bv = w_ref[:, cs], b_ref[:, cs]
            acc = [jnp.zeros((HALO, cs.stop - cs.start), F32) for _ in range(taps + 1)]
            for r0 in list(range(0, tm, CONV_RB)) + [tm]:
                inside = r0 < tm
                rows = CONV_RB if inside else HALO
                xs = _shifted(buf, taps, r0, rows, cs)
                d = d_ref[r0:r0 + rows, cs] if inside else jnp.where(i < nt - 1, dn_ref[:, cs], 0.0)
                dpre = d * _silu_grad(_taps_fwd(xs, wv, bv))
                bufd[r0:r0 + rows, cs] = dpre
                if inside:
                    acc[taps] = acc[taps] + _fold8(dpre)
                    for k in range(taps):
                        acc[k] = acc[k] + _fold8(dpre * xs[k])
            gb_ref[:, cs] += jnp.sum(acc[taps], axis=0, keepdims=True)
            for k in range(taps):
                gw_ref[k:k + 1, cs] += jnp.sum(acc[k], axis=0, keepdims=True)
            for r0 in range(0, tm, CONV_RB):
                dx_ref[r0:r0 + CONV_RB, cs] = _taps_bwd(bufd, wv, taps, r0, CONV_RB, cs).astype(BF16)

    xt, xb, xa = _halo_specs(tm, wd, t)
    dt_, _, dn = _halo_specs(tm, wd, t)
    par = lambda rows: pl.BlockSpec((rows, wd), lambda c, i: (0, c))
    return pl.pallas_call(
        kern, name="ssm_conv_bwd", grid=(ncol, nt), in_specs=[xt, xb, xa, dt_, dn, par(taps), par(1)] + [ANY_SPEC] * ns,
        out_specs=[pl.BlockSpec((tm, wd), lambda c, i: (i, c)), par(taps), par(1)] + [ANY_SPEC] * ns,
        out_shape=[jax.ShapeDtypeStruct((t, XBC_DIM), BF16), jax.ShapeDtypeStruct((taps, XBC_DIM), F32),
                   jax.ShapeDtypeStruct((1, XBC_DIM), F32)] + [jax.ShapeDtypeStruct(s.shape, s.dtype) for s in parts],
        scratch_shapes=[pltpu.VMEM((tm + 2 * HALO, wd), F32), pltpu.VMEM((tm + HALO, wd), F32)] + (_scatter_sems(ns) if ns else []),
        compiler_params=_cparams(("arbitrary", "arbitrary")))(proj, proj, proj, dact, dact, w, b, *parts)


def _ffn_act_down(u, w, b, w_down, x1):
    t = u.shape[1]
    tm, wd = min(FFN_CONV_TM, t), D_FF
    nt, taps = t // tm, w.shape[0]

    def kern(g_ref, gb_ref, v_ref, vb_ref, wg_ref, wv_ref, bg_ref, bv_ref, wd_ref, x1_ref, a_ref, x2_ref, bufg, bufv):
        i = pl.program_id(1)
        _fill_ext(bufg, g_ref, gb_ref, None, i, nt)
        _fill_ext(bufv, v_ref, vb_ref, None, i, nt)
        acc = x1_ref[...]
        for cs in _lane_chunks(wd):
            wg, wv, bg, bv = wg_ref[:, cs], wv_ref[:, cs], bg_ref[:, cs], bv_ref[:, cs]
            for r0 in range(0, tm, CONV_RB):
                g = _taps_fwd(_shifted(bufg, taps, r0, CONV_RB, cs), wg, bg)
                v = _taps_fwd(_shifted(bufv, taps, r0, CONV_RB, cs), wv, bv)
                a_ref[r0:r0 + CONV_RB, cs] = (g * _sigmoid(g) * v).astype(BF16)
            acc = acc + _dot(a_ref[:, cs], wd_ref[cs, :])
        x2_ref[...] = acc

    gt, gbf, _ = _halo_specs(tm, wd, t, lead=0)
    vt, vbf, _ = _halo_specs(tm, wd, t, lead=1)
    par = lambda rows, off: pl.BlockSpec((rows, wd), functools.partial(lambda c, i, off: (0, c + off), off=off))
    row = lambda width: pl.BlockSpec((tm, width), lambda c, i: (i, 0))
    return pl.pallas_call(
        kern, name="ffn_act_down", grid=(1, nt),
        in_specs=[gt, gbf, vt, vbf, par(taps, 0), par(taps, 1), par(1, 0), par(1, 1),
                  pl.BlockSpec(w_down.shape, lambda c, i: (0, 0)), row(D_MODEL)],
        out_specs=[row(wd), row(D_MODEL)],
        out_shape=[jax.ShapeDtypeStruct((t, D_FF), BF16), jax.ShapeDtypeStruct((t, D_MODEL), F32)],
        scratch_shapes=[pltpu.VMEM((tm + HALO, wd), F32)] * 2,
        compiler_params=_cparams(("parallel", "parallel")))(u, u, u, u, w, w, b, b, w_down, x1)


def _ffn_act_bwd(u, w, b, da):
    t = u.shape[1]
    tm, wd = min(FFN_CONV_TM, t), FFN_CONV_W
    nt, taps, nc = t // tm, w.shape[0], D_FF // FFN_CONV_W

    def kern(g_ref, gb_ref, ga_ref, v_ref, vb_ref, va_ref, d_ref, dn_ref, wg_ref, wv_ref, bg_ref, bv_ref,
             du_ref, gwg_ref, gwv_ref, gbg_ref, gbv_ref, bufg, bufv, bufdg, bufdv):
        i = pl.program_id(1)
        _fill_ext(bufg, g_ref, gb_ref, ga_ref, i, nt)
        _fill_ext(bufv, v_ref, vb_ref, va_ref, i, nt)

        @pl.when(i == 0)
        def _():
            for r in (gwg_ref, gwv_ref, gbg_ref, gbv_ref):
                r[...] = jnp.zeros(r.shape, F32)
        for cs in _lane_chunks(wd):
            wg, wv, bg, bv = wg_ref[:, cs], wv_ref[:, cs], bg_ref[:, cs], bv_ref[:, cs]
            zero = jnp.zeros((HALO, cs.stop - cs.start), F32)
            accg, accv = [zero] * (taps + 1), [zero] * (taps + 1)
            for r0 in list(range(0, tm, CONV_RB)) + [tm]:
                inside = r0 < tm
                rows = CONV_RB if inside else HALO
                xg, xv = _shifted(bufg, taps, r0, rows, cs), _shifted(bufv, taps, r0, rows, cs)
                g, v = _taps_fwd(xg, wg, bg), _taps_fwd(xv, wv, bv)
                dav = d_ref[r0:r0 + rows, cs] if inside else jnp.where(i < nt - 1, dn_ref[:, cs], 0.0)
                sg = _sigmoid(g)
                dg = dav * v * (sg * (1.0 + g * (1.0 - sg)))
                dv = dav * (g * sg)
                bufdg[r0:r0 + rows, cs] = dg
                bufdv[r0:r0 + rows, cs] = dv
                if inside:
                    accg[taps], accv[taps] = accg[taps] + _fold8(dg), accv[taps] + _fold8(dv)
                    for k in range(taps):
                        accg[k], accv[k] = accg[k] + _fold8(dg * xg[k]), accv[k] + _fold8(dv * xv[k])
            gbg_ref[:, cs] += jnp.sum(accg[taps], axis=0, keepdims=True)
            gbv_ref[:, cs] += jnp.sum(accv[taps], axis=0, keepdims=True)
            for k in range(taps):
                gwg_ref[k:k + 1, cs] += jnp.sum(accg[k], axis=0, keepdims=True)
                gwv_ref[k:k + 1, cs] += jnp.sum(accv[k], axis=0, keepdims=True)
            for r0 in range(0, tm, CONV_RB):
                du_ref[0, r0:r0 + CONV_RB, cs] = _taps_bwd(bufdg, wg, taps, r0, CONV_RB, cs).astype(BF16)
                du_ref[1, r0:r0 + CONV_RB, cs] = _taps_bwd(bufdv, wv, taps, r0, CONV_RB, cs).astype(BF16)

    gt, gbf, gaf = _halo_specs(tm, wd, t, lead=0)
    vt, vbf, vaf = _halo_specs(tm, wd, t, lead=1)
    dt_, _, dn = _halo_specs(tm, wd, t)
    par = lambda rows, off: pl.BlockSpec((rows, wd), functools.partial(lambda c, i, off: (0, c + off), off=off))
    return pl.pallas_call(
        kern, name="ffn_act_bwd", grid=(nc, nt),
        in_specs=[gt, gbf, gaf, vt, vbf, vaf, dt_, dn, par(taps, 0), par(taps, nc), par(1, 0), par(1, nc)],
        out_specs=[pl.BlockSpec((2, tm, wd), lambda c, i: (0, i, c)), par(taps, 0), par(taps, 0), par(1, 0), par(1, 0)],
        out_shape=[jax.ShapeDtypeStruct((2, t, D_FF), BF16)] + [jax.ShapeDtypeStruct((taps, D_FF), F32)] * 2
        + [jax.ShapeDtypeStruct((1, D_FF), F32)] * 2,
        scratch_shapes=[pltpu.VMEM((tm + 2 * HALO, wd), F32)] * 2 + [pltpu.VMEM((tm + HALO, wd), F32)] * 2,
        compiler_params=_cparams(("parallel", "arbitrary")))(u, u, u, u, u, u, da, da, w, w, b, b)


def _softplus(x):
    e = jnp.exp(-jnp.abs(x))
    return jnp.maximum(x, 0.0) + jnp.where(e < 1e-4, e - 0.5 * e * e, jnp.log(1.0 + e))


def _tri(lower):
    r = lax.broadcasted_iota(jnp.int32, (CHUNK, CHUNK), 0)
    c = lax.broadcasted_iota(jnp.int32, (CHUNK, CHUNK), 1)
    return (r >= c) if lower else (r <= c)


def _cum(mat_bool, x):
    return jnp.dot(mat_bool.astype(F32), x, precision=lax.Precision.HIGHEST, preferred_element_type=F32)


def _pair_sel(lane_lo, tile, h0):
    return jnp.where(lane_lo, tile[:, h0:h0 + 1], tile[:, h0 + 1:h0 + 2])


def _pair_sel_mxu(lane_lo, tile, h0):
    rows = lax.broadcasted_iota(jnp.int32, (LANE, LANE), 0)
    sel = (rows == jnp.where(lane_lo, h0, h0 + 1)).astype(BF16)
    return _dot_split(tile, sel)


def _ssd_fwd(xbc_act, proj, dt_bias_p, a_log_p, dskip_t, shards):
    t = xbc_act.shape[0]
    nch = t // CHUNK
    ns = len(shards)

    def kern(xa_ref, dtr_ref, bias_ref, alog_ref, dsk_ref, *rest):
        y_ref, dt_ref, hs_ref = rest[ns:ns + 3]
        hst = rest[2 * ns + 3]
        if ns:
            start, forward, finish = _gather_steps(shards, rest[:ns], rest[ns + 3:2 * ns + 3], rest[2 * ns + 4:])
            pl.when(pl.program_id(0) == 0)(start)
            pl.when(pl.program_id(0) == (3 * nch) // 4)(forward)
            pl.when(pl.program_id(0) == nch - 1)(finish)

        @pl.when(pl.program_id(0) == 0)
        def _():
            hst[...] = jnp.zeros(hst.shape, F32)
        dt = _softplus(dtr_ref[...] + bias_ref[...])
        dt_ref[...] = dt
        acum = _cum(_tri(True), dt * (-jnp.exp(alog_ref[...])))
        acum_t = acum.T
        ea = jnp.exp(acum)
        a_last = acum[CHUNK - 1:CHUNK, :]
        dend = jnp.exp(a_last - acum)
        ea_last = jnp.exp(a_last)
        causal = _tri(True)
        lane_lo = lax.broadcasted_iota(jnp.int32, (CHUNK, LANE), 1) < HEAD_DIM
        row_lo = lax.broadcasted_iota(jnp.int32, (CHUNK, LANE), 0) < HEAD_DIM
        for g in range(2):
            bg = xa_ref[:, SSM_INNER + g * SSM_STATE:SSM_INNER + (g + 1) * SSM_STATE].astype(BF16)
            cg = xa_ref[:, SSM_INNER + BC_DIM + g * SSM_STATE:SSM_INNER + BC_DIM + (g + 1) * SSM_STATE].astype(BF16)
            cb = _dot_nt(cg, bg)
            for j in range(4 * g, 4 * g + 4):
                h0 = 2 * j
                cols = slice(j * LANE, (j + 1) * LANE)
                xp = xa_ref[:, cols]
                xdt = xp * _pair_sel(lane_lo, dt, h0)
                ydiag = None
                for hh, sel in ((h0, lane_lo), (h0 + 1, ~lane_lo)):
                    seg = acum[:, hh:hh + 1] - acum_t[hh:hh + 1, :]
                    mm = (cb * jnp.where(causal, jnp.exp(jnp.minimum(seg, 0.0)), 0.0)).astype(BF16)
                    d = _dot(mm, jnp.where(sel, xdt, 0.0).astype(BF16))
                    ydiag = d if ydiag is None else ydiag + d
                hp = hst[cols, :]
                hs_ref[cols, :] = hp
                yoff = _dot_nt(cg, hp.astype(BF16)) * _pair_sel(lane_lo, ea, h0)
                y_ref[:, cols] = ydiag + yoff + dsk_ref[:, cols] * xp
                xw = (xdt * _pair_sel(lane_lo, dend, h0)).astype(BF16)
                rowf = jnp.where(row_lo, ea_last[:, h0:h0 + 1], ea_last[:, h0 + 1:h0 + 2])
                hst[cols, :] = hp * rowf + _dot_tn(xw, bg)

    return pl.pallas_call(
        kern, name="ssd_fwd", grid=(nch,),
        in_specs=[pl.BlockSpec((CHUNK, XBC_DIM), lambda c: (c, 0)), pl.BlockSpec((CHUNK, LANE), lambda c: (c, OFF_DT // LANE)),
                  pl.BlockSpec((1, LANE), lambda c: (0, 0)), pl.BlockSpec((1, LANE), lambda c: (0, 0)),
                  pl.BlockSpec((1, SSM_INNER), lambda c: (0, 0))] + [ANY_SPEC] * ns,
        out_specs=[pl.BlockSpec((CHUNK, SSM_INNER), lambda c: (c, 0)), pl.BlockSpec((CHUNK, LANE), lambda c: (c, 0)),
                   pl.BlockSpec((None, SSM_INNER, SSM_STATE), lambda c: (c, 0, 0))] + [ANY_SPEC] * ns,
        out_shape=[jax.ShapeDtypeStruct((t, SSM_INNER), F32), jax.ShapeDtypeStruct((t, LANE), F32),
                   jax.ShapeDtypeStruct((nch, SSM_INNER, SSM_STATE), F32)] + _gather_out_shapes(shards),
        scratch_shapes=[pltpu.VMEM((SSM_INNER, SSM_STATE), F32)] + (_gather_sems(ns) if ns else []),
        compiler_params=_cparams(("arbitrary",)))(xbc_act, proj, dt_bias_p, a_log_p, dskip_t, *shards)


def _ssd_bwd(xbc_act, proj, dt_sp, hstates, dy, dt_bias_p, a_log_p, dskip_t, swaps):
    t = xbc_act.shape[0]
    nch = t // CHUNK
    ns = len(swaps)

    pair = jnp.arange(SSM_HEADS // 2)[:, None, None]
    psel = (jnp.arange(LANE)[None, None, :] == 2 * pair + (jnp.arange(LANE) // HEAD_DIM)[None, :, None]).astype(BF16)

    def kern(xa_ref, dtr_ref, dt_ref, hs_ref, dy_ref, bias_ref, alog_ref, dsk_ref, psel_ref, *rest):
        dact_ref, ddtr_ref, da_ref, dbias_ref, ddsk_ref = rest[ns:ns + 5]
        dh = rest[2 * ns + 5]
        if ns:
            start, finish = _swap_steps(swaps, rest[:ns], rest[ns + 5:2 * ns + 5], rest[2 * ns + 6:])
            pl.when(pl.program_id(0) == 0)(start)
            pl.when(pl.program_id(0) == nch - 1)(finish)

        @pl.when(pl.program_id(0) == 0)
        def _():
            dh[...] = jnp.zeros(dh.shape, F32)
            for r in (da_ref, dbias_ref, ddsk_ref):
                r[...] = jnp.zeros(r.shape, F32)
        dt = dt_ref[...]
        a_neg = -jnp.exp(alog_ref[...])
        acum = _cum(_tri(True), dt * a_neg)
        acum_t = acum.T
        ea = jnp.exp(acum)
        a_last = acum[CHUNK - 1:CHUNK, :]
        dend = jnp.exp(a_last - acum)
        ea_last = jnp.exp(a_last)
        causal = _tri(True)
        lane = lax.broadcasted_iota(jnp.int32, (CHUNK, LANE), 1)
        rowi = lax.broadcasted_iota(jnp.int32, (CHUNK, LANE), 0)
        lane_lo, row_lo, last_row = lane < HEAD_DIM, rowi < HEAD_DIM, rowi == CHUNK - 1
        d_dt = jnp.zeros((CHUNK, LANE), F32)
        d_acum = jnp.zeros((CHUNK, LANE), F32)
        for g in range(2):
            bcols = slice(SSM_INNER + g * SSM_STATE, SSM_INNER + (g + 1) * SSM_STATE)
            ccols = slice(SSM_INNER + BC_DIM + g * SSM_STATE, SSM_INNER + BC_DIM + (g + 1) * SSM_STATE)
            bg, cg = xa_ref[:, bcols].astype(BF16), xa_ref[:, ccols].astype(BF16)
            cb = _dot_nt(cg, bg)
            dg_sum = jnp.zeros((CHUNK, CHUNK), F32)
            dcg = jnp.zeros((CHUNK, SSM_STATE), F32)
            dbg = jnp.zeros((CHUNK, SSM_STATE), F32)
            for j in range(4 * g, 4 * g + 4):
                h0 = 2 * j
                cols = slice(j * LANE, (j + 1) * LANE)
                xp, dyp = xa_ref[:, cols], dy_ref[:, cols]
                dtsel = _pair_sel_mxu(lane_lo, dt, h0)
                xdt = xp * dtsel
                xdt_b = xdt.astype(BF16)
                hp, dhp = hs_ref[cols, :], dh[cols, :]
                hp_b, dhp_b = hp.astype(BF16), dhp.astype(BF16)
                easel, dendsel = _pair_sel_mxu(lane_lo, ea, h0), _pair_sel_mxu(lane_lo, dend, h0)
                dx, ydiag = None, None
                for hh, sel in ((h0, lane_lo), (h0 + 1, ~lane_lo)):
                    dyh = jnp.where(sel, dyp, 0.0).astype(BF16)
                    seg = acum[:, hh:hh + 1] - acum_t[hh:hh + 1, :]
                    dec = jnp.where(causal, jnp.exp(jnp.minimum(seg, 0.0)), 0.0)
                    mm_b = (cb * dec).astype(BF16)
                    dg_sum = dg_sum + dec * _dot_nt(dyh, xdt_b)
                    d = _dot_tn(mm_b, dyh)
                    y = _dot(mm_b, jnp.where(sel, xdt, 0.0).astype(BF16))
                    dx = d if dx is None else dx + d
                    ydiag = y if ydiag is None else ydiag + y
                g2 = _dot_nt(bg, dhp_b)
                tprod = xdt * g2 * dendsel
                yoff = _dot_nt(cg, hp_b) * easel
                yc = dyp.astype(BF16).astype(F32) * ydiag + dyp * yoff - (xdt_b.astype(F32) * dx + tprod)
                dx = dx + g2 * dendsel
                psel = psel_ref[j]
                t_lo = jnp.sum(jnp.where(lane_lo, tprod, 0.0), keepdims=True).reshape(1, 1)
                t_hi = jnp.sum(tprod, keepdims=True).reshape(1, 1) - t_lo
                hh_prod = dhp * hp
                s_lo = jnp.sum(jnp.where(row_lo, hh_prod, 0.0), keepdims=True).reshape(1, 1)
                s_hi = jnp.sum(hh_prod, keepdims=True).reshape(1, 1) - s_lo
                end_lo = ea_last[:, h0:h0 + 1] * s_lo + t_lo
                end_hi = ea_last[:, h0 + 1:h0 + 2] * s_hi + t_hi
                ends = jnp.where(lane == h0, end_lo, jnp.where(lane == h0 + 1, end_hi, 0.0))
                d_acum = d_acum + _dot_split(yc, psel) + jnp.where(last_row, ends, 0.0)
                dye = (dyp * easel).astype(BF16)
                dcg = dcg + _dot(dye, hp_b)
                dbg = dbg + _dot((xdt * dendsel).astype(BF16), dhp_b)
                rowf = jnp.where(row_lo, ea_last[:, h0:h0 + 1], ea_last[:, h0 + 1:h0 + 2])
                dh[cols, :] = dhp * rowf + _dot_tn(dye, cg)
                dact_ref[:, cols] = dx * dtsel + dsk_ref[:, cols] * dyp
                d_dt = d_dt + _dot_split(dx * xp, psel)
                ddsk_ref[:, cols] += jnp.sum(dyp * xp, axis=0, keepdims=True)
            dg_b = dg_sum.astype(BF16)
            dact_ref[:, ccols] = dcg + _dot(dg_b, bg)
            dact_ref[:, bcols] = dbg + _dot_tn(dg_b, cg)
        d_adt = _cum(_tri(False), d_acum)
        d_dt = d_dt + d_adt * a_neg
        da_ref[...] += jnp.sum(d_adt * dt, axis=0, keepdims=True)
        d_raw = jnp.where(lane < SSM_HEADS, d_dt * _sigmoid(dtr_ref[...] + bias_ref[...]), 0.0)
        ddtr_ref[...] = d_raw.astype(BF16)
        dbias_ref[...] += jnp.sum(d_raw, axis=0, keepdims=True)

    rev = lambda c: (nch - 1 - c, 0)
    return pl.pallas_call(
        kern, name="ssd_bwd", grid=(nch,),
        in_specs=[pl.BlockSpec((CHUNK, XBC_DIM), rev), pl.BlockSpec((CHUNK, LANE), lambda c: (nch - 1 - c, OFF_DT // LANE)),
                  pl.BlockSpec((CHUNK, LANE), rev), pl.BlockSpec((None, SSM_INNER, SSM_STATE), lambda c: (nch - 1 - c, 0, 0)),
                  pl.BlockSpec((CHUNK, SSM_INNER), rev),
                  pl.BlockSpec((1, LANE), lambda c: (0, 0)), pl.BlockSpec((1, LANE), lambda c: (0, 0)),
                  pl.BlockSpec((1, SSM_INNER), lambda c: (0, 0)), pl.BlockSpec(psel.shape, lambda c: (0, 0, 0))] + [ANY_SPEC] * ns,
        out_specs=[pl.BlockSpec((CHUNK, XBC_DIM), rev), pl.BlockSpec((CHUNK, LANE), rev),
                   pl.BlockSpec((1, LANE), lambda c: (0, 0)), pl.BlockSpec((1, LANE), lambda c: (0, 0)),
                   pl.BlockSpec((1, SSM_INNER), lambda c: (0, 0))] + [ANY_SPEC] * ns,
        out_shape=[jax.ShapeDtypeStruct((t, XBC_DIM), F32), jax.ShapeDtypeStruct((t, LANE), BF16),
                   jax.ShapeDtypeStruct((1, LANE), F32), jax.ShapeDtypeStruct((1, LANE), F32),
                   jax.ShapeDtypeStruct((1, SSM_INNER), F32)] + _swap_out_shapes(swaps),
        scratch_shapes=[pltpu.VMEM((SSM_INNER, SSM_STATE), F32)] + (_swap_sems(ns) if ns else []),
        compiler_params=_cparams(("arbitrary",)))(xbc_act, proj, dt_sp, hstates, dy, dt_bias_p, a_log_p, dskip_t, psel, *swaps)


def _pad_lanes(v, width=LANE):
    return jnp.pad(v, ((0, 0), (0, width - v.shape[1])))


def _local_step(x, p, tgt, wts, late_shards=(), join_late=None, reduce_early=None, reduce_late=None):
    g_attn, g_ssm, g_ffn, g_ple = wts["attn_norm_g"], wts["ssm_norm_g"], wts["ffn_norm_g"], wts["ple_norm_g"]
    w_in_p = wts["w_in_p"]
    gq_t = jnp.tile(wts["q_norm_g"], (1, ATTN_DIM // HEAD_DIM))
    gk_t = jnp.tile(wts["k_norm_g"], (1, KV_DIM // HEAD_DIM))
    dt_bias_p, a_log_p = _pad_lanes(wts["dt_bias"]), _pad_lanes(wts["a_log"])
    dskip_t = jnp.repeat(wts["d_skip"], HEAD_DIM, axis=1)

    h1, proj, q_hm, kv_hm = _in_proj(x, g_attn, w_in_p, gq_t, gk_t)
    pats = [_attn_fwd2(q_hm, kv_hm, d) for d in DILATIONS]
    xbc_act =_ssm_conv_fwd(proj, wts["ssm_conv_w"], wts["ssm_conv_b"])
    y_ssd, dt_sp, hstates, *gathered = _ssd_fwd(xbc_act, proj, dt_bias_p, a_log_p, dskip_t, list(late_shards))
    if join_late is not None:
        wts = {**wts, **join_late(gathered)}
    w_out_s, w_out_a = wts["w_out_ssm"], wts["w_out_attn"]
    w_up, w_down, w_gate, w_proj = wts["w_up"], wts["w_down"], wts["w_ple_gate"], wts["w_ple_proj"]
    ssm_out, attn_out, lse, x1 = _mix_out_proj(y_ssd, proj, g_ssm, w_out_s, [o for o, _ in pats], [l for _, l in pats], w_out_a, x)
    h2, u = _norm_mm("ffn_up", x1, g_ffn, w_up, tm=1024, tn=1408, halves=True)
    a, x2 = _ffn_act_down(u, wts["ffn_conv_w"], wts["ffn_conv_b"], w_down, x1)
    pb = p.astype(BF16)
    h3, dy, dgl, dpp, sq = _ple_head(x2, g_ple, w_gate, pb, w_proj, tgt)

    grads = {}
    grads["w_ple_proj"] = _mm_tn("g_ple_proj", pb, dpp, tn=PLE_DIM, chip_cols=True)
    grads["w_ple_gate"] = _mm_tn("g_ple_gate", h3, dgl)
    dx2, dx2b, grads["ple_norm_g"] = _mm_nt_rms_bwd("d_h3", dgl, w_gate, x2, g_ple, dy)
    da = _mm_nt("d_ffn_act", [(dx2b, w_down, 0)], F32, tm=1024, tn=1408)
    grads["w_down"] = _mm_tn("g_ffn_down", a, dx2b, tm=1408)
    du, gwg, gwv, gbg, gbv = _ffn_act_bwd(u, wts["ffn_conv_w"], wts["ffn_conv_b"], da)
    grads["ffn_conv_w"] = jnp.concatenate([gwg, gwv], axis=1)
    grads["ffn_conv_b"] = jnp.concatenate([gbg, gbv], axis=1)
    grads["w_up"] = _mm_tn("g_ffn_up", h2, du, tn=1408, chip_cols=True)
    dh2 = _mm_nt("d_h2", [(du, w_up, 0, 0), (du, w_up, 1, 1)], F32, tm=1024, tn=512)
    dx1, dx1b, grads["ffn_norm_g"] = _rms_bwd("rms_ffn_bwd", dh2, x1, g_ffn, dx2)
    dy_ssd, dz, grads["ssm_norm_g"], do_hm, dsum = _d_mix(dx1b, jnp.concatenate([w_out_s, w_out_a], axis=0), y_ssd, proj, g_ssm,
                                                            attn_out)
    grads["w_out"] = jnp.concatenate([_mm_tn("g_out_attn", attn_out, dx1b), _mm_tn("g_out_ssm", ssm_out, dx1b)], axis=0)
    early_major = reduce_early[0](grads) if reduce_early is not None else []
    dact, ddtr, d_a, d_bias, d_dsk, *early_got = _ssd_bwd(xbc_act, proj, dt_sp, hstates, dy_ssd, dt_bias_p, a_log_p, dskip_t,
                                                           early_major)
    grads["dt_bias"] = d_bias[:, :SSM_HEADS]
    grads["a_log"] = d_a[:, :SSM_HEADS] * (-jnp.exp(wts["a_log"]))
    grads["d_skip"] = jnp.sum(d_dsk.reshape(SSM_HEADS, HEAD_DIM), axis=1)[None, :]
    chip_sums = reduce_early[1](early_major, early_got) if reduce_early is not None else []
    dxbc, grads["ssm_conv_w"], grads["ssm_conv_b"], *scattered = _ssm_conv_bwd(proj, wts["ssm_conv_w"], wts["ssm_conv_b"], dact,
                                                                                chip_sums)
    dqs = [_attn_dq2(q_hm, kv_hm, do_hm, lse, dsum, d) for d in DILATIONS]
    dkvs = [_attn_dkv2(q_hm, kv_hm, do_hm, lse, dsum, d) for d in DILATIONS]
    dq, dk, dv, dgq, dgk = _qknorm_bwd2(proj, gq_t, gk_t, dqs, dkvs)
    grads["q_norm_g"] = jnp.sum(dgq.reshape(ATTN_DIM // HEAD_DIM, HEAD_DIM), axis=0)[None, :]
    grads["k_norm_g"] = jnp.sum(dgk.reshape(KV_DIM // HEAD_DIM, HEAD_DIM), axis=0)[None, :]
    dproj = jnp.concatenate([dxbc, dq, dz, dk, dv, ddtr], axis=1)
    grads["w_in_p"] = _mm_tn("g_in_proj", h1, dproj, tm=512)
    late_sums = reduce_late(grads) if reduce_late is not None else []
    grad_x, _, grads["attn_norm_g"], *late_scattered = _mm_nt_rms_bwd("d_h1", dproj, w_in_p, x, g_attn, dx1, late_sums)
    return sq, grad_x, grads, (chip_sums, scattered), (late_sums, late_scattered)


MESH_IDS = pl.DeviceIdType.MESH
N_CHIPS = 4
ANY_SPEC = pl.BlockSpec(memory_space=pl.ANY)
SMALL_ROWS = 96
ALL_SMALL_ROWS = 272


def _place():
    x, y, c = lax.axis_index("x"), lax.axis_index("y"), lax.axis_index("c")
    return x, y, c, [(1 - x, y), (x, 1 - y), (1 - x, 1 - y)]


def _gather_over_chips(arrs):
    n = len(arrs)

    def body(*refs):
        steps = _gather_steps(arrs, refs[:n], refs[n:2 * n], refs[2 * n:2 * n + 4])
        for step in steps:
            step()

    return pl.pallas_call(
        body, name="gather_weights", in_specs=[ANY_SPEC] * n, out_specs=[ANY_SPEC] * n,
        out_shape=_gather_out_shapes(arrs), scratch_shapes=_gather_sems(n))(*arrs)


def _gather_out_shapes(arrs):
    return [jax.ShapeDtypeStruct((N_CHIPS,) + a.shape, a.dtype) for a in arrs]


def _gather_sems(n):
    return [pltpu.SemaphoreType.DMA((3 * n,))] * 4


def _gather_steps(arrs, ins, outs, sems):
    n = len(arrs)
    split = [a.shape[0] % 64 == 0 for a in arrs]
    ici_send, ici_recv, d2d_send, d2d_recv = sems

    def place():
        x, y, c, chips = _place()
        return x, y, c, chips, 2 * x + y

    def part(ref, a, core):
        if not split[a]:
            return ref
        half = arrs[a].shape[0] // 2
        return ref.at[pl.ds(core * half, half)]

    def ici(a, k, slot, where):
        x, y, c, chips, _ = where
        px, py = chips[k]
        return pltpu.make_async_remote_copy(
            src_ref=part(ins[a], a, c), dst_ref=part(outs[a].at[slot], a, c), send_sem=ici_send.at[3 * a + k],
            recv_sem=ici_recv.at[3 * a + k], device_id=(px, py, c), device_id_type=MESH_IDS)

    def d2d(a, k, core, where):
        x, y, c, chips, _ = where
        px, py = chips[k]
        piece = part(outs[a].at[2 * px + py], a, core)
        return pltpu.make_async_remote_copy(src_ref=piece, dst_ref=piece, send_sem=d2d_send.at[3 * a + k],
                                            recv_sem=d2d_recv.at[3 * a + k], device_id=(x, y, 1 - c), device_id_type=MESH_IDS)

    def start():
        w = place()
        for a in range(n):
            for k in range(3):
                ici(a, k, w[4], w).start()

    def forward():
        w = place()
        for a in range(n):
            for k, (px, py) in enumerate(w[3]):
                ici(a, k, 2 * px + py, w).wait_recv()
                if split[a]:
                    d2d(a, k, w[2], w).start()

    def finish():
        w = place()
        for a in range(n):
            for k in range(3):
                if split[a]:
                    d2d(a, k, 1 - w[2], w).wait_recv()
                    d2d(a, k, w[2], w).wait_send()
                ici(a, k, w[4], w).wait_send()

    return start, forward, finish


def _row_tile(rows, cap=256):
    return max(d for d in range(8, cap + 1, 8) if rows % d == 0)


def _swap_halves(name, gs):
    n = len(gs)

    def body(*refs):
        for step in _swap_steps(gs, refs[:n], refs[n:2 * n], refs[2 * n:2 * n + 2]):
            step()

    return pl.pallas_call(
        body, name=name, in_specs=[ANY_SPEC] * n, out_specs=[ANY_SPEC] * n, out_shape=_swap_out_shapes(gs),
        scratch_shapes=_swap_sems(n))(*gs)


def _swap_out_shapes(gs):
    return [jax.ShapeDtypeStruct((N_CHIPS, g.shape[1] // 2, g.shape[2]), g.dtype) for g in gs]


def _swap_sems(n):
    return [pltpu.SemaphoreType.DMA((N_CHIPS * n,))] * 2


def _swap_steps(gs, ins, outs, sems):
    send, recv = sems

    def copies():
        x, y, c, _ = _place()
        cps = []
        for a in range(len(gs)):
            half = gs[a].shape[1] // 2
            for q in range(N_CHIPS):
                cps.append(pltpu.make_async_remote_copy(
                    src_ref=ins[a].at[q, pl.ds((1 - c) * half, half)], dst_ref=outs[a].at[q], send_sem=send.at[N_CHIPS * a + q],
                    recv_sem=recv.at[N_CHIPS * a + q], device_id=(x, y, 1 - c), device_id_type=MESH_IDS))
        return cps

    def start():
        for cp in copies():
            cp.start()

    def finish():
        for cp in copies():
            cp.wait()

    return start, finish


def _add_halves(name, g, got, c_idx):
    rows, cols = got.shape[1:]
    tm = _row_tile(rows)
    per = rows // tm

    def kern(c_ref, g_ref, r_ref, o_ref):
        o_ref[...] = (g_ref[...] + r_ref[...]).astype(BF16)

    return pl.pallas_call(
        kern, name=name,
        grid_spec=pltpu.PrefetchScalarGridSpec(
            num_scalar_prefetch=1, grid=(N_CHIPS, per),
            in_specs=[pl.BlockSpec((None, tm, cols), lambda q, i, c_ref: (q, c_ref[0] * per + i, 0)),
                      pl.BlockSpec((None, tm, cols), lambda q, i, c_ref: (q, i, 0))],
            out_specs=pl.BlockSpec((None, tm, cols), lambda q, i, c_ref: (q, i, 0))),
        out_shape=jax.ShapeDtypeStruct((N_CHIPS, rows, cols), BF16),
        compiler_params=_cparams(("parallel", "parallel")))(c_idx, g, got)


def _scatter_sems(n):
    return [pltpu.SemaphoreType.DMA((3 * n,))] * 2


def _scatter_steps(n, ins, outs, sems):
    send, recv = sems

    def copy(a, k, slot, where):
        x, y, c, chips = where
        px, py = chips[k]
        return pltpu.make_async_remote_copy(src_ref=ins[a].at[2 * px + py], dst_ref=outs[a].at[slot], send_sem=send.at[3 * a + k],
                                            recv_sem=recv.at[3 * a + k], device_id=(px, py, c), device_id_type=MESH_IDS)

    def start():
        w = _place()
        for a in range(n):
            for k in range(3):
                copy(a, k, 2 * w[0] + w[1], w).start()

    def finish():
        w = _place()
        for a in range(n):
            for k, (px, py) in enumerate(w[3]):
                copy(a, k, 2 * px + py, w).wait()

    return start, finish


def _sum_chips(name, own, parts, idx):
    rows, cols = parts.shape[1:]
    tm = _row_tile(rows)
    per = rows // tm

    def kern(o_idx, a_ref, b_ref, c_ref, d_ref, o_ref):
        o_ref[...] = ((a_ref[...].astype(F32) + b_ref[...].astype(F32)) + c_ref[...].astype(F32)) + d_ref[...].astype(F32)

    def spec(k):
        return pl.BlockSpec((None, tm, cols), functools.partial(lambda i, o_idx, k: (o_idx[k], i, 0), k=k))

    return pl.pallas_call(
        kern, name=name,
        grid_spec=pltpu.PrefetchScalarGridSpec(
            num_scalar_prefetch=1, grid=(per,), in_specs=[spec(0), spec(1), spec(2), spec(3)],
            out_specs=pl.BlockSpec((None, tm, cols), lambda i, o_idx: (0, o_idx[4] * per + i, 0))),
        out_shape=jax.ShapeDtypeStruct((1, 2 * rows, cols), F32), compiler_params=_cparams(("parallel",)))(idx, own, parts, parts, parts)


def _share_with_sibling(gs):
    n = len(gs)

    def body(*refs):
        ins, send, recv = refs[:n], refs[2 * n], refs[2 * n + 1]
        x, y, c, _ = _place()
        cps = []
        for a in range(n):
            half = gs[a].shape[1] // 2
            mine = pl.ds(c * half, half)
            cps.append(pltpu.make_async_remote_copy(src_ref=ins[a].at[0, mine], dst_ref=refs[n + a].at[0, mine], send_sem=send.at[a],
                                                    recv_sem=recv.at[a], device_id=(x, y, 1 - c), device_id_type=MESH_IDS))
        for cp in cps:
            cp.start()
        for cp in cps:
            cp.wait()

    return pl.pallas_call(
        body, name="grad_share_sibling", in_specs=[ANY_SPEC] * n, out_specs=[ANY_SPEC] * n,
        out_shape=[jax.ShapeDtypeStruct(g.shape, g.dtype) for g in gs], input_output_aliases={a: a for a in range(n)},
        scratch_shapes=[pltpu.SemaphoreType.DMA((n,))] * 2)(*gs)


def _allreduce_small(v):
    def body(v_ref, o_ref, land, send, recv):
        x, y, c, _ = _place()
        me = 4 * x + 2 * y + c
        land[me] = v_ref[...]
        cps = []
        for rel in range(1, 8):
            bx, by, bc = (rel >> 2) & 1, (rel >> 1) & 1, rel & 1
            peer = (1 - x if bx else x, 1 - y if by else y, 1 - c if bc else c)
            cps.append(pltpu.make_async_remote_copy(src_ref=v_ref, dst_ref=land.at[me], send_sem=send.at[rel - 1],
                                                    recv_sem=recv.at[rel - 1], device_id=peer, device_id_type=MESH_IDS))
        for cp in cps:
            cp.start()
        for cp in cps:
            cp.wait()
        acc = land[0]
        for d in range(1, 8):
            acc = acc + land[d]
        o_ref[...] = acc

    vm = pl.BlockSpec(memory_space=pltpu.VMEM)
    return pl.pallas_call(
        body, name="allreduce_small", in_specs=[vm], out_specs=vm, out_shape=jax.ShapeDtypeStruct(v.shape, F32),
        scratch_shapes=[pltpu.VMEM((8,) + v.shape, F32), pltpu.SemaphoreType.DMA((7,)), pltpu.SemaphoreType.DMA((7,))])(v)


def _adamw(name, w, g, m, v):
    _, rows, cols = w.shape
    tm = rows if rows * cols <= 128 * 1024 else _row_tile(rows, max(256, 2048 * LANE // cols))
    c1 = 1.0 / (1.0 - ADAM_B1 ** ADAM_STEP)
    c2 = 1.0 / (1.0 - ADAM_B2 ** ADAM_STEP)

    def kern(w_ref, g_ref, m_ref, v_ref, d_ref, mo_ref, vo_ref):
        gv = g_ref[...]
        mn = ADAM_B1 * m_ref[...] + (1.0 - ADAM_B1) * gv
        vn = ADAM_B2 * v_ref[...] + (1.0 - ADAM_B2) * (gv * gv)
        d_ref[...] = -ADAM_LR * ((mn * c1) / (jnp.sqrt(vn * c2) + ADAM_EPS) + ADAM_WD * w_ref[...])
        mo_ref[...] = mn
        vo_ref[...] = vn

    spec = pl.BlockSpec((None, tm, cols), lambda i: (0, i, 0))
    return pl.pallas_call(
        kern, name=name, grid=(rows // tm,), in_specs=[spec] * 4, out_specs=[spec] * 3,
        out_shape=[jax.ShapeDtypeStruct(w.shape, F32)] * 3, compiler_params=_cparams(("parallel",)))(w, g, m, v)


SHARDED = (("w_in", 1), ("w_out", 0), ("w_up", 1), ("w_down", 0), ("w_ple_gate", 0), ("w_ple_proj", 1),
           ("ssm_conv_w", 1), ("ffn_conv_w", 1))
MATRICES = ("w_in", "w_out", "w_up", "w_down", "w_ple_gate", "w_ple_proj")
EARLY_REDUCED = MATRICES[1:]
REPLICATED = ("attn_norm_g", "q_norm_g", "k_norm_g", "ssm_conv_b", "dt_bias", "a_log", "d_skip", "ssm_norm_g",
              "ffn_norm_g", "ffn_conv_b", "ple_norm_g")
WEIGHT_ORDER = ("attn_norm_g", "w_in", "q_norm_g", "k_norm_g", "ssm_conv_w", "ssm_conv_b", "dt_bias", "a_log", "d_skip",
                "ssm_norm_g", "w_out", "ffn_norm_g", "w_up", "ffn_conv_w", "ffn_conv_b", "w_down", "ple_norm_g",
                "w_ple_gate", "w_ple_proj")


def _join_chips(g, axis):
    if axis == 0:
        return g.reshape(g.shape[0] * g.shape[1], g.shape[2])
    return jnp.transpose(g, (1, 0, 2)).reshape(g.shape[1], g.shape[0] * g.shape[2])


def _split_chips(g, axis):
    if axis == 0:
        return g.reshape(N_CHIPS, g.shape[0] // N_CHIPS, g.shape[1])
    r, c = g.shape
    return jnp.transpose(g.reshape(r, N_CHIPS, c // N_CHIPS), (1, 0, 2))


def _pack_small(vals, rows=SMALL_ROWS):
    flat = jnp.concatenate([v.reshape(-1) for v in vals])
    return jnp.pad(flat, (0, rows * LANE - flat.shape[0])).reshape(rows, LANE)


def _unpack_small(packed, like):
    flat, out, off = packed.reshape(-1), [], 0
    for v in like:
        out.append(flat[off:off + v.size].reshape(v.shape))
        off += v.size
    return out


def kernel(x, p, attn_norm_g, w_in, q_norm_g, k_norm_g, ssm_conv_w, ssm_conv_b, dt_bias, a_log, d_skip, ssm_norm_g, w_out, ffn_norm_g, w_up, ffn_conv_w, ffn_conv_b, w_down, ple_norm_g, w_ple_gate, w_ple_proj, loss_target, m_attn_norm_g, m_w_in, m_q_norm_g, m_k_norm_g, m_ssm_conv_w, m_ssm_conv_b, m_dt_bias, m_a_log, m_d_skip, m_ssm_norm_g, m_w_out, m_ffn_norm_g, m_w_up, m_ffn_conv_w, m_ffn_conv_b, m_w_down, m_ple_norm_g, m_w_ple_gate, m_w_ple_proj, v_attn_norm_g, v_w_in, v_q_norm_g, v_k_norm_g, v_ssm_conv_w, v_ssm_conv_b, v_dt_bias, v_a_log, v_d_skip, v_ssm_norm_g, v_w_out, v_ffn_norm_g, v_w_up, v_ffn_conv_w, v_ffn_conv_b, v_w_down, v_ple_norm_g, v_w_ple_gate, v_w_ple_proj):
    given = dict(locals())
    w2 = {n: given[n].reshape(given[n].shape[-2:]) if given[n].ndim == 3 else given[n] for n in WEIGHT_ORDER}

    cx, cy, cc = lax.axis_index("x"), lax.axis_index("y"), lax.axis_index("c")
    chip = 2 * cx + cy
    axis_of = dict(SHARDED)
    shard = lambda n: w2[n].astype(BF16) if n in MATRICES else w2[n]
    join = lambda n, g: _join_chips(lax.dynamic_update_index_in_dim(g, shard(n), chip, 0), axis_of[n])
    first = ("w_in", "ssm_conv_w", "ffn_conv_w")
    full = {n: join(n, g) for n, g in zip(first, _gather_over_chips([shard(n) for n in first]))}
    win = full["w_in"]
    w_in_p = jnp.concatenate([win[:, 2048:3584], win[:, 0:512], win[:, 1024:2048], win[:, 512:768], win[:, 768:1024],
                              win[:, 3584:3600], jnp.zeros((D_MODEL, PROJ_P - IN_PROJ), BF16)], axis=1)
    wts = {n: w2[n] for n in REPLICATED}
    wts.update(w_in_p=w_in_p, ssm_conv_w=full["ssm_conv_w"], ffn_conv_w=full["ffn_conv_w"])

    def join_late(gathered):
        late = {n: join(n, g) for n, g in zip(EARLY_REDUCED, gathered)}
        return dict(w_out_attn=late["w_out"][:ATTN_DIM], w_out_ssm=late["w_out"][ATTN_DIM:], w_up=late["w_up"],
                    w_down=late["w_down"], w_ple_gate=late["w_ple_gate"], w_ple_proj=late["w_ple_proj"])

    core = cc.astype(jnp.int32).reshape(1)
    idx = jnp.stack([chip, 2 * (1 - cx) + cy, 2 * cx + (1 - cy), 2 * (1 - cx) + (1 - cy), cc]).astype(jnp.int32)

    def major_of(names, gd):
        return [gd[n] if gd[n].ndim == 3 else _split_chips(gd[n], axis_of[n]) for n in names]

    def sums_of(names, major, got):
        return [_add_halves("grad_add_halves_" + n, g, r, core) for n, g, r in zip(names, major, got)]

    def w_in_sums(gd):
        gi = gd["w_in_p"]
        gd["w_in"] = jnp.concatenate([gi[:, OFF_Q:OFF_Q + ATTN_DIM], gi[:, OFF_K:OFF_K + KV_DIM], gi[:, OFF_V:OFF_V + KV_DIM],
                                      gi[:, OFF_Z:OFF_Z + SSM_INNER], gi[:, OFF_XBC:OFF_XBC + XBC_DIM], gi[:, OFF_DT:OFF_DT + SSM_HEADS]],
                                     axis=1)
        major = major_of(("w_in",), gd)
        return sums_of(("w_in",), major, _swap_halves("grad_swap_halves_late", major))

    sq, grad_x, grads, early, late = _local_step(
        x[0], p[0, 0], loss_target[0], wts, [shard(n) for n in EARLY_REDUCED], join_late,
        (functools.partial(major_of, EARLY_REDUCED), functools.partial(sums_of, EARLY_REDUCED)), w_in_sums)
    sums = dict(zip(EARLY_REDUCED + ("w_in",), list(zip(*early)) + list(zip(*late))))
    halves = [_sum_chips("grad_sum_chips_" + n, *sums[n], idx) for n in MATRICES]
    g_shard = dict(zip(MATRICES, _share_with_sibling(halves)))

    small_names = REPLICATED + ("ssm_conv_w", "ffn_conv_w")
    small_like = [grads[n] for n in small_names] + [jnp.zeros((1,), F32)]
    small = _allreduce_small(_pack_small([grads[n] for n in small_names] + [jnp.sum(sq).reshape(1)], ALL_SMALL_ROWS))
    small_vals = dict(zip(small_names + ("loss",), _unpack_small(small, small_like)))
    loss = (0.5 / D_MODEL) * small_vals["loss"][0]
    for n in ("ssm_conv_w", "ffn_conv_w"):
        cols = w2[n].shape[1]
        g_shard[n] = lax.dynamic_slice_in_dim(small_vals[n], chip * cols, cols, axis=1)[None]

    delta, new_m, new_v = {}, {}, {}
    for n, _ in SHARDED:
        if n == "w_in":
            r, c = w2[n].shape
            flat = lambda a: jnp.transpose(a.reshape(r, c)).reshape(1, r * c // LANE, LANE)
            back = lambda a: jnp.transpose(a.reshape(c, r)).reshape(1, r, c)
            outs = _adamw("adamw_" + n, flat(given[n]), flat(g_shard[n]), flat(given["m_" + n]), flat(given["v_" + n]))
            delta[n], new_m[n], new_v[n] = [back(o) for o in outs]
            continue
        delta[n], new_m[n], new_v[n] = _adamw("adamw_" + n, given[n], g_shard[n], given["m_" + n], given["v_" + n])
    packed = lambda prefix: _pack_small([given[prefix + n] for n in REPLICATED])[None]
    sm = _adamw("adamw_small", packed(""), _pack_small([small_vals[n] for n in REPLICATED])[None], packed("m_"), packed("v_"))
    for n in REPLICATED:
        g_shard[n] = small_vals[n]
    for dst, packed_out in zip((delta, new_m, new_v), sm):
        for n, val in zip(REPLICATED, _unpack_small(packed_out[0], [w2[n] for n in REPLICATED])):
            dst[n] = val

    def shaped(d):
        return [d[n].reshape(given[n].shape) for n in WEIGHT_ORDER]
    return (loss, grad_x[None], *shaped(g_shard), *shaped(delta), *shaped(new_m), *shaped(new_v))
```

```python
import functools

import jax
import jax.numpy as jnp
from jax import lax
from jax.experimental import pallas as pl
from jax.experimental.pallas import tpu as pltpu

F32 = jnp.float32
BF16 = jnp.bfloat16

D_MODEL = 1024
HEAD_DIM = 64
ATTN_DIM = 512
KV_DIM = 256
N_KV = 4
SSM_INNER = 1024
SSM_HEADS = 16
SSM_STATE = 128
BC_DIM = 256
XBC_DIM = SSM_INNER + 2 * BC_DIM
MIX_DIM = ATTN_DIM + SSM_INNER
IN_PROJ = 3600
D_FF = 2816
PLE_DIM = 256
CHUNK = 128
DILATIONS = (1, 4, 16)
EPS = 1e-6
ADAM_LR, ADAM_B1, ADAM_B2, ADAM_EPS, ADAM_WD, ADAM_STEP = 0.001, 0.9, 0.999, 1e-08, 0.01, 10

PROJ_P = 3712
OFF_XBC, OFF_Q, OFF_Z, OFF_K, OFF_V, OFF_DT = 0, 1536, 2048, 3072, 3328, 3584
LANE = 128
VMEM_LIMIT = 48 * 1024 * 1024
NEG = -1e30


def _cparams(sem):
    return pltpu.CompilerParams(dimension_semantics=sem, vmem_limit_bytes=VMEM_LIMIT)


def _sigmoid(x):
    return 1.0 / (1.0 + jnp.exp(-x))


def _dot(a, b):
    return jnp.dot(a, b, preferred_element_type=F32)


def _dot_nt(a, b):
    return lax.dot_general(a, b, (((1,), (1,)), ((), ())), preferred_element_type=F32)


def _dot_tn(a, b):
    return lax.dot_general(a, b, (((0,), (0,)), ((), ())), preferred_element_type=F32)


def _dot_split(x, m):
    hi = x.astype(BF16)
    lo = (x - hi.astype(F32)).astype(BF16)
    return _dot(hi, m) + _dot(lo, m)


def _rows(name, body, ins, outs, accs=(), tm=512):
    t_rows = next(s[1].shape[0] for s in ins if s[0] in ("t", "tc"))
    tm = min(tm, t_rows)
    in_specs, args = [], []
    for s in ins:
        if s[0] == "t":
            in_specs.append(pl.BlockSpec((tm, s[1].shape[1]), lambda i: (i, 0)))
        elif s[0] == "tc":
            in_specs.append(pl.BlockSpec((tm, s[2]), functools.partial(lambda i, c: (i, c), c=s[3])))
        else:
            in_specs.append(pl.BlockSpec(s[1].shape, lambda i: (0, 0)))
        args.append(s[1])
    out_shape = [jax.ShapeDtypeStruct((t_rows, w), dt) for w, dt in outs]
    out_specs = [pl.BlockSpec((tm, w), lambda i: (i, 0)) for w, _ in outs]
    out_shape += [jax.ShapeDtypeStruct(a, F32) for a in accs]
    out_specs += [pl.BlockSpec(a, lambda i: (0, 0)) for a in accs]
    n_acc = len(accs)

    def kern(*refs):
        if n_acc:
            @pl.when(pl.program_id(0) == 0)
            def _():
                for r in refs[len(refs) - n_acc:]:
                    r[...] = jnp.zeros(r.shape, F32)
        body(*refs)

    return pl.pallas_call(
        kern, name=name, grid=(t_rows // tm,), in_specs=in_specs, out_specs=out_specs, out_shape=out_shape,
        compiler_params=_cparams(("arbitrary",) if n_acc else ("parallel",)))(*args)


NCHUNK = 512


def _col_chunks(n):
    return [(c, min(NCHUNK, n - c)) for c in range(0, n, NCHUNK)]


def _mm_nt(name, pairs, out_dtype, tm=512, tn=None):
    m, n = pairs[0][0].shape[-2], pairs[0][1].shape[0]
    tn = n if tn is None else tn
    tm = min(tm, m)
    np_ = len(pairs)
    in_specs, args = [], []
    for a, w, kb, *lead in pairs:
        if lead:
            in_specs.append(pl.BlockSpec((None, tm, a.shape[2]), functools.partial(lambda j, i, ld: (ld, i, 0), ld=lead[0])))
        else:
            in_specs.append(pl.BlockSpec((tm, a.shape[1]), lambda j, i: (i, 0)))
        in_specs.append(pl.BlockSpec((tn, a.shape[-1]), functools.partial(lambda j, i, kb: (j, kb), kb=kb)))
        args += [a, w]

    def kern(*refs):
        o_ref = refs[-1]
        for c0, cw in _col_chunks(tn):
            acc = None
            for q in range(np_):
                d = _dot_nt(refs[2 * q][...], refs[2 * q + 1][c0:c0 + cw, :])
                acc = d if acc is None else acc + d
            o_ref[:, c0:c0 + cw] = acc.astype(o_ref.dtype)

    return pl.pallas_call(
        kern, name=name, grid=(n // tn, m // tm), in_specs=in_specs,
        out_specs=pl.BlockSpec((tm, tn), lambda j, i: (i, j)),
        out_shape=jax.ShapeDtypeStruct((m, n), out_dtype), compiler_params=_cparams(("parallel", "parallel")))(*args)


def _mm_tn(name, a, b, tm=None, tn=None, tk=1024, chip_cols=False):
    t, m = a.shape
    n = b.shape[-1] * (2 if b.ndim == 3 else 1)
    tm = m if tm is None else tm
    tn = n if tn is None else tn
    tk = min(tk, t)
    if b.ndim == 3:
        per = n // 2 // tn
        b_spec = pl.BlockSpec((None, tk, tn), lambda i, j, k: (j // per, k, j % per))
    else:
        b_spec = pl.BlockSpec((tk, tn), lambda i, j, k: (k, j))
    if chip_cols:
        out_spec = pl.BlockSpec((None, tm, tn), lambda i, j, k: (j, i, 0))
        out_shape = jax.ShapeDtypeStruct((n // tn, m, tn), F32)
    else:
        out_spec = pl.BlockSpec((tm, tn), lambda i, j, k: (i, j))
        out_shape = jax.ShapeDtypeStruct((m, n), F32)

    def kern(a_ref, b_ref, o_ref):
        @pl.when(pl.program_id(2) == 0)
        def _():
            o_ref[...] = jnp.zeros(o_ref.shape, F32)
        for c0, cw in _col_chunks(tn):
            o_ref[:, c0:c0 + cw] += _dot_tn(a_ref[...], b_ref[:, c0:c0 + cw])

    return pl.pallas_call(
        kern, name=name, grid=(m // tm, n // tn, t // tk),
        in_specs=[pl.BlockSpec((tk, tm), lambda i, j, k: (k, i)), b_spec], out_specs=out_spec, out_shape=out_shape,
        compiler_params=_cparams(("parallel", "parallel", "arbitrary")))(a, b)


def _rms_bwd(name, dh, x, g, dres):
    d = x.shape[1]

    def body(dh_ref, x_ref, g_ref, dres_ref, dx_ref, dxb_ref, dg_ref):
        xv, dhv = x_ref[...], dh_ref[...]
        r = lax.rsqrt(jnp.mean(xv * xv, axis=-1, keepdims=True) + EPS)
        gd = dhv * g_ref[...]
        dx = dres_ref[...] + r * gd - xv * (r * r * r * jnp.mean(xv * gd, axis=-1, keepdims=True))
        dx_ref[...] = dx
        dxb_ref[...] = dx.astype(BF16)
        dg_ref[...] += jnp.sum(dhv * xv * r, axis=0, keepdims=True)
    return _rows(name, body, [("t", dh), ("t", x), ("p", g), ("t", dres)], [(d, F32), (d, BF16)], accs=[(1, d)])


def _norm_mm(name, x, g, w, tm=512, tn=None, halves=False):
    m, k = x.shape
    n = w.shape[1]
    tn = n if tn is None else tn
    if halves:
        per = n // 2 // tn
        o_spec = pl.BlockSpec((None, tm, tn), lambda i, j: (j // per, i, j % per))
        o_shape = jax.ShapeDtypeStruct((2, m, n // 2), F32)
    else:
        o_spec = pl.BlockSpec((tm, tn), lambda i, j: (i, j))
        o_shape = jax.ShapeDtypeStruct((m, n), F32)

    def kern(x_ref, g_ref, w_ref, h_ref, o_ref):
        xv = x_ref[...]
        h = (xv * lax.rsqrt(jnp.mean(xv * xv, axis=-1, keepdims=True) + EPS) * g_ref[...]).astype(BF16)
        h_ref[...] = h
        for c0, cw in _col_chunks(tn):
            o_ref[:, c0:c0 + cw] = _dot(h, w_ref[:, c0:c0 + cw])

    return pl.pallas_call(
        kern, name=name, grid=(m // tm, n // tn),
        in_specs=[pl.BlockSpec((tm, k), lambda i, j: (i, 0)), pl.BlockSpec((1, k), lambda i, j: (0, 0)),
                  pl.BlockSpec((k, tn), lambda i, j: (0, j))],
        out_specs=[pl.BlockSpec((tm, k), lambda i, j: (i, 0)), o_spec],
        out_shape=[jax.ShapeDtypeStruct((m, k), BF16), o_shape],
        compiler_params=_cparams(("parallel", "arbitrary")))(x, g, w)


def _ple_head(x2, g, w_gate, pb, w_proj, tgt, tm=512):
    m, d = x2.shape

    def kern(x_ref, g_ref, wg_ref, p_ref, wp_ref, t_ref, h_ref, dy_ref, dgl_ref, dpp_ref, sq_ref):
        @pl.when(pl.program_id(0) == 0)
        def _():
            sq_ref[...] = jnp.zeros(sq_ref.shape, F32)
        xv = x_ref[...]
        h = (xv * lax.rsqrt(jnp.mean(xv * xv, axis=-1, keepdims=True) + EPS) * g_ref[...]).astype(BF16)
        h_ref[...] = h
        pv = p_ref[...]
        for c0, cw in _col_chunks(d):
            cs = slice(c0, c0 + cw)
            s = _sigmoid(_dot(h, wg_ref[:, cs]))
            ppv = _dot(pv, wp_ref[:, cs])
            diff = x_ref[:, cs] + s * ppv - t_ref[:, cs]
            dy = diff * (1.0 / d)
            dy_ref[:, cs] = dy
            dgl_ref[:, cs] = (dy * ppv * s * (1.0 - s)).astype(BF16)
            dpp_ref[:, cs] = (dy * s).astype(BF16)
            sq_ref[:, cs] += jnp.sum(diff * diff, axis=0, keepdims=True)

    row = lambda width: pl.BlockSpec((tm, width), lambda i: (i, 0))
    full = lambda a: pl.BlockSpec(a.shape, lambda i: (0, 0))
    return pl.pallas_call(
        kern, name="ple_head", grid=(m // tm,),
        in_specs=[row(d), full(g), full(w_gate), row(pb.shape[1]), full(w_proj), row(d)],
        out_specs=[row(d), row(d), row(d), row(d), pl.BlockSpec((1, d), lambda i: (0, 0))],
        out_shape=[jax.ShapeDtypeStruct((m, d), BF16), jax.ShapeDtypeStruct((m, d), F32), jax.ShapeDtypeStruct((m, d), BF16),
                   jax.ShapeDtypeStruct((m, d), BF16), jax.ShapeDtypeStruct((1, d), F32)],
        compiler_params=_cparams(("arbitrary",)))(x2, g, w_gate, pb, w_proj, tgt)


def _mix_out_proj(y, proj, g, w_ssm, os_, lses, w_attn, x, tm=256):
    m, d = y.shape

    def kern(y_ref, z_ref, g_ref, ws_ref, o1, o2, o3, l1, l2, l3, wa_ref, x_ref, s_ref, a_ref, lse_ref, o_ref):
        z = z_ref[...]
        yz = y_ref[...] * (z * _sigmoid(z))
        s = (yz * lax.rsqrt(jnp.mean(yz * yz, axis=-1, keepdims=True) + EPS) * g_ref[...]).astype(BF16)
        s_ref[...] = s
        pieces = []
        for kh in range(N_KV):
            a, b, c = l1[kh], l2[kh], l3[kh]
            mx = jnp.maximum(jnp.maximum(a, b), c)
            tot = mx + jnp.log(jnp.exp(a - mx) + jnp.exp(b - mx) + jnp.exp(c - mx))
            lse_ref[kh] = tot
            wa, wb, wc = jnp.exp(a - tot), jnp.exp(b - tot), jnp.exp(c - tot)
            for g_ in range(2):
                h = 2 * kh + g_
                acc = wa[:, g_:g_ + 1] * o1[h] + wb[:, g_:g_ + 1] * o2[h] + wc[:, g_:g_ + 1] * o3[h]
                pieces.append(acc[:, HEAD_DIM:])
        av = jnp.concatenate(pieces, axis=1).astype(BF16)
        a_ref[...] = av
        for c0, cw in _col_chunks(d):
            cs = slice(c0, c0 + cw)
            o_ref[:, cs] = x_ref[:, cs] + _dot(s, ws_ref[:, cs]) + _dot(av, wa_ref[:, cs])

    row = lambda width: pl.BlockSpec((tm, width), lambda i: (i, 0))
    full = lambda a: pl.BlockSpec(a.shape, lambda i: (0, 0))
    blk = lambda heads: pl.BlockSpec((heads, tm, LANE), lambda i: (0, i, 0))
    return pl.pallas_call(
        kern, name="out_proj", grid=(m // tm,),
        in_specs=[row(d), pl.BlockSpec((tm, d), lambda i: (i, OFF_Z // SSM_INNER)), full(g), full(w_ssm)] + [blk(N_QH)] * 3
        + [blk(N_KV)] * 3 + [full(w_attn), row(d)],
        out_specs=[row(d), row(ATTN_DIM), blk(N_KV), row(d)],
        out_shape=[jax.ShapeDtypeStruct((m, d), BF16), jax.ShapeDtypeStruct((m, ATTN_DIM), BF16),
                   jax.ShapeDtypeStruct((N_KV, m, LANE), F32), jax.ShapeDtypeStruct((m, d), F32)],
        compiler_params=_cparams(("parallel",)))(y, proj, g, w_ssm, *os_, *lses, w_attn, x)


def _mm_nt_rms_bwd(name, a, w, x, g, dres, parts=(), tm=512):
    m, k = a.shape
    n = w.shape[0]
    ns, steps = len(parts), m // tm

    def kern(a_ref, w_ref, x_ref, g_ref, dres_ref, *rest):
        dx_ref, dxb_ref, dg_ref = rest[ns:ns + 3]
        dh = rest[2 * ns + 3]
        if ns:
            start, finish = _scatter_steps(ns, rest[:ns], rest[ns + 3:2 * ns + 3], rest[2 * ns + 4:])
            pl.when(pl.program_id(0) == 0)(start)
            pl.when(pl.program_id(0) == steps - 1)(finish)

        @pl.when(pl.program_id(0) == 0)
        def _():
            dg_ref[...] = jnp.zeros(dg_ref.shape, F32)
        av = a_ref[...]
        for c0, cw in _col_chunks(n):
            dh[:, c0:c0 + cw] = _dot_nt(av, w_ref[c0:c0 + cw, :])
        xv, dhv = x_ref[...], dh[...]
        r = lax.rsqrt(jnp.mean(xv * xv, axis=-1, keepdims=True) + EPS)
        gd = dhv * g_ref[...]
        dx = dres_ref[...] + r * gd - xv * (r * r * r * jnp.mean(xv * gd, axis=-1, keepdims=True))
        dx_ref[...] = dx
        dxb_ref[...] = dx.astype(BF16)
        dg_ref[...] += jnp.sum(dhv * xv * r, axis=0, keepdims=True)

    row = lambda width: pl.BlockSpec((tm, width), lambda i: (i, 0))
    return pl.pallas_call(
        kern, name=name, grid=(steps,),
        in_specs=[row(k), pl.BlockSpec((n, k), lambda i: (0, 0)), row(n), pl.BlockSpec((1, n), lambda i: (0, 0)), row(n)]
        + [ANY_SPEC] * ns,
        out_specs=[row(n), row(n), pl.BlockSpec((1, n), lambda i: (0, 0))] + [ANY_SPEC] * ns,
        out_shape=[jax.ShapeDtypeStruct((m, n), F32), jax.ShapeDtypeStruct((m, n), BF16), jax.ShapeDtypeStruct((1, n), F32)]
        + [jax.ShapeDtypeStruct(s.shape, s.dtype) for s in parts],
        scratch_shapes=[pltpu.VMEM((tm, n), F32)] + (_scatter_sems(ns) if ns else []),
        compiler_params=_cparams(("arbitrary",)))(a, w, x, g, dres, *parts)


def _head_mean_matrix(width):
    i = jnp.arange(width) // HEAD_DIM
    return jnp.where(i[:, None] == i[None, :], 1.0 / HEAD_DIM, 0.0).astype(BF16)


ATT_SPAN = 2048
N_QH = 8


def _lane_lo(rows):
    return lax.broadcasted_iota(jnp.int32, (rows, LANE), 1) < HEAD_DIM


def _swap_halves_lanes(x):
    return pltpu.roll(x, HEAD_DIM, axis=1)


def _head_major_qkv(qn, kn, v, qo_ref, kvo_ref):
    lo = _lane_lo(qn.shape[0])
    for j in range(N_KV):
        blk = qn[:, j * LANE:(j + 1) * LANE]
        qo_ref[2 * j] = jnp.where(lo, blk, 0.0)
        qo_ref[2 * j + 1] = jnp.where(lo, _swap_halves_lanes(blk), 0.0)
    for j in range(2):
        kb, vb = kn[:, j * LANE:(j + 1) * LANE], v[:, j * LANE:(j + 1) * LANE]
        kvo_ref[2 * j] = jnp.where(lo, kb, _swap_halves_lanes(vb))
        kvo_ref[2 * j + 1] = jnp.where(lo, _swap_halves_lanes(kb), vb)


def _in_proj(x, g, w, gq_t, gk_t, tm=512):
    m, k = x.shape
    n = w.shape[1]
    bq, bk = _head_mean_matrix(ATTN_DIM), _head_mean_matrix(KV_DIM)
    scale = HEAD_DIM ** -0.5

    def kern(x_ref, g_ref, w_ref, gq_ref, gk_ref, bq_ref, bk_ref, h_ref, o_ref, qo_ref, kvo_ref):
        xv = x_ref[...]
        h = (xv * lax.rsqrt(jnp.mean(xv * xv, axis=-1, keepdims=True) + EPS) * g_ref[...]).astype(BF16)
        h_ref[...] = h
        for c0, cw in _col_chunks(n):
            o_ref[:, c0:c0 + cw] = _dot(h, w_ref[:, c0:c0 + cw])
        q, kk, v = o_ref[:, OFF_Q:OFF_Q + ATTN_DIM], o_ref[:, OFF_K:OFF_K + KV_DIM], o_ref[:, OFF_V:OFF_V + KV_DIM]
        qn = (q * lax.rsqrt(_dot_split(q * q, bq_ref[...]) + EPS) * gq_ref[...]) * scale
        kn = kk * lax.rsqrt(_dot_split(kk * kk, bk_ref[...]) + EPS) * gk_ref[...]
        _head_major_qkv(qn, kn, v, qo_ref, kvo_ref)

    row = lambda width: pl.BlockSpec((tm, width), lambda i: (i, 0))
    full = lambda a: pl.BlockSpec(a.shape, lambda i: (0, 0))
    blk = lambda heads: pl.BlockSpec((heads, tm, LANE), lambda i: (0, i, 0))
    return pl.pallas_call(
        kern, name="in_proj", grid=(m // tm,),
        in_specs=[row(k), full(g), full(w), full(gq_t), full(gk_t), full(bq), full(bk)],
        out_specs=[row(k), row(n), blk(N_QH), blk(N_KV)],
        out_shape=[jax.ShapeDtypeStruct((m, k), BF16), jax.ShapeDtypeStruct((m, n), F32),
                   jax.ShapeDtypeStruct((N_QH, m, LANE), F32), jax.ShapeDtypeStruct((N_KV, m, LANE), F32)],
        compiler_params=_cparams(("parallel",)))(x, g, w, gq_t, gk_t, bq, bk)


def _d_mix(dx1b, w_cat, y, proj, g, attn_out, tm=512):
    m, k = dx1b.shape
    n = w_cat.shape[0]

    def kern(a_ref, w_ref, y_ref, z_ref, g_ref, o_ref, dy_ref, dz_ref, dg_ref, dot_ref, d_ref, dmix):
        @pl.when(pl.program_id(0) == 0)
        def _():
            dg_ref[...] = jnp.zeros(dg_ref.shape, F32)
        av = a_ref[...]
        for c0, cw in _col_chunks(n):
            dmix[:, c0:c0 + cw] = _dot_nt(av, w_ref[c0:c0 + cw, :])
        z, yv, dout = z_ref[...], y_ref[...], dmix[:, :SSM_INNER]
        sg = _sigmoid(z)
        gz = z * sg
        yz = yv * gz
        r = lax.rsqrt(jnp.mean(yz * yz, axis=-1, keepdims=True) + EPS)
        gd = dout * g_ref[...]
        dyz = r * gd - yz * (r * r * r * jnp.mean(yz * gd, axis=-1, keepdims=True))
        dy_ref[...] = dyz * gz
        dz_ref[...] = (dyz * yv * (sg * (1.0 + z * (1.0 - sg)))).astype(BF16)
        dg_ref[...] += jnp.sum(dout * yz * r, axis=0, keepdims=True)
        do = dmix[:, SSM_INNER:]
        prod = do * o_ref[...].astype(F32)
        lo = _lane_lo(tm)
        lane = lax.broadcasted_iota(jnp.int32, (tm, LANE), 1)
        for kh in range(N_KV):
            blk, pb = do[:, kh * LANE:(kh + 1) * LANE], prod[:, kh * LANE:(kh + 1) * LANE]
            dot_ref[2 * kh] = jnp.where(lo, 0.0, _swap_halves_lanes(blk))
            dot_ref[2 * kh + 1] = jnp.where(lo, 0.0, blk)
            s_lo = jnp.sum(jnp.where(lo, pb, 0.0), axis=1, keepdims=True)
            s_hi = jnp.sum(pb, axis=1, keepdims=True) - s_lo
            d_ref[kh] = jnp.where(lane == 0, s_lo, jnp.where(lane == 1, s_hi, 0.0))

    row = lambda width: pl.BlockSpec((tm, width), lambda i: (i, 0))
    full = lambda a: pl.BlockSpec(a.shape, lambda i: (0, 0))
    blk = lambda heads: pl.BlockSpec((heads, tm, LANE), lambda i: (0, i, 0))
    return pl.pallas_call(
        kern, name="d_mix", grid=(m // tm,),
        in_specs=[row(k), full(w_cat), row(SSM_INNER), pl.BlockSpec((tm, SSM_INNER), lambda i: (i, OFF_Z // SSM_INNER)), full(g),
                  row(ATTN_DIM)],
        out_specs=[row(SSM_INNER), row(SSM_INNER), pl.BlockSpec((1, SSM_INNER), lambda i: (0, 0)), blk(N_QH), blk(N_KV)],
        out_shape=[jax.ShapeDtypeStruct((m, SSM_INNER), F32), jax.ShapeDtypeStruct((m, SSM_INNER), BF16),
                   jax.ShapeDtypeStruct((1, SSM_INNER), F32), jax.ShapeDtypeStruct((N_QH, m, LANE), F32),
                   jax.ShapeDtypeStruct((N_KV, m, LANE), F32)],
        scratch_shapes=[pltpu.VMEM((tm, n), F32)], compiler_params=_cparams(("arbitrary",)))(dx1b, w_cat, y, proj, g, attn_out)


def _qknorm_bwd2(proj, gq_t, gk_t, dqs, dkvs, tm=256):
    t = proj.shape[0]
    bq, bk = _head_mean_matrix(ATTN_DIM), _head_mean_matrix(KV_DIM)
    scale = HEAD_DIM ** -0.5

    def kern(q_ref, k_ref, gq_ref, gk_ref, bq_ref, bk_ref, a1, a2, a3, b1, b2, b3, dq_ref, dk_ref, dv_ref, dgq_ref, dgk_ref):
        @pl.when(pl.program_id(0) == 0)
        def _():
            dgq_ref[...] = jnp.zeros(dgq_ref.shape, F32)
            dgk_ref[...] = jnp.zeros(dgk_ref.shape, F32)
        lo = _lane_lo(tm)
        sq = [a1[h] + a2[h] + a3[h] for h in range(N_QH)]
        skv = [b1[h] + b2[h] + b3[h] for h in range(N_KV)]
        dqn = jnp.concatenate([jnp.where(lo, sq[2 * j], _swap_halves_lanes(sq[2 * j + 1])) for j in range(N_KV)], axis=1) * scale
        dkn = jnp.concatenate([jnp.where(lo, skv[2 * j], _swap_halves_lanes(skv[2 * j + 1])) for j in range(2)], axis=1)
        dv = jnp.concatenate([jnp.where(lo, _swap_halves_lanes(skv[2 * j]), skv[2 * j + 1]) for j in range(2)], axis=1)
        q, k = q_ref[...], k_ref[...]
        rq = lax.rsqrt(_dot_split(q * q, bq_ref[...]) + EPS)
        rk = lax.rsqrt(_dot_split(k * k, bk_ref[...]) + EPS)
        gdq, gdk = dqn * gq_ref[...], dkn * gk_ref[...]
        dq_ref[...] = (rq * gdq - q * (rq * rq * rq * _dot_split(q * gdq, bq_ref[...]))).astype(BF16)
        dk_ref[...] = (rk * gdk - k * (rk * rk * rk * _dot_split(k * gdk, bk_ref[...]))).astype(BF16)
        dv_ref[...] = dv.astype(BF16)
        dgq_ref[...] += jnp.sum(dqn * q * rq, axis=0, keepdims=True)
        dgk_ref[...] += jnp.sum(dkn * k * rk, axis=0, keepdims=True)

    col = lambda w, idx: pl.BlockSpec((tm, w), functools.partial(lambda i, idx: (i, idx), idx=idx))
    par = lambda a: pl.BlockSpec(a.shape, lambda i: (0, 0))
    blk = lambda n: pl.BlockSpec((n, tm, LANE), lambda i: (0, i, 0))
    row = lambda w: pl.BlockSpec((tm, w), lambda i: (i, 0))
    acc = lambda w: pl.BlockSpec((1, w), lambda i: (0, 0))
    return pl.pallas_call(
        kern, name="qknorm_bwd", grid=(t // tm,),
        in_specs=[col(ATTN_DIM, OFF_Q // ATTN_DIM), col(KV_DIM, OFF_K // KV_DIM), par(gq_t), par(gk_t), par(bq), par(bk)]
        + [blk(N_QH)] * 3 + [blk(N_KV)] * 3,
        out_specs=[row(ATTN_DIM), row(KV_DIM), row(KV_DIM), acc(ATTN_DIM), acc(KV_DIM)],
        out_shape=[jax.ShapeDtypeStruct((t, ATTN_DIM), BF16), jax.ShapeDtypeStruct((t, KV_DIM), BF16),
                   jax.ShapeDtypeStruct((t, KV_DIM), BF16), jax.ShapeDtypeStruct((1, ATTN_DIM), F32),
                   jax.ShapeDtypeStruct((1, KV_DIM), F32)],
        compiler_params=_cparams(("arbitrary",)))(proj, proj, gq_t, gk_t, bq, bk, *dqs, *dkvs)


def _att_rows(b, r, dil):
    if dil == 1:
        return pl.ds(b * CHUNK, CHUNK)
    return pl.ds(b * CHUNK * dil + r, CHUNK, stride=dil)


def _for_residues(dil, unit):
    for r in range(dil):
        unit(r, 0)


def _band_qk(first):
    ri = lax.broadcasted_iota(jnp.int32, (CHUNK, 2 * CHUNK), 0)
    cj = lax.broadcasted_iota(jnp.int32, (CHUNK, 2 * CHUNK), 1)
    band = (cj - ri >= 0) & (cj - ri <= CHUNK)
    return band if first is None else band & (jnp.logical_not(first) | (cj >= CHUNK))


def _band_kq(last):
    rj = lax.broadcasted_iota(jnp.int32, (CHUNK, 2 * CHUNK), 0)
    ci = lax.broadcasted_iota(jnp.int32, (CHUNK, 2 * CHUNK), 1)
    band = (ci - rj >= 0) & (ci - rj <= CHUNK)
    return band if last is None else band & (jnp.logical_not(last) | (ci < CHUNK))


def _att_specs(t, dil):
    sub = CHUNK * dil
    nb, last = ATT_SPAN // sub, t // sub - 1
    cur = lambda heads: pl.BlockSpec((heads, ATT_SPAN, LANE), lambda kh, n: (kh, n, 0))
    prev = lambda heads: pl.BlockSpec((heads, sub, LANE), lambda kh, n: (kh, jnp.maximum(n * nb - 1, 0), 0))
    nxt = lambda heads: pl.BlockSpec((heads, sub, LANE), lambda kh, n: (kh, jnp.minimum((n + 1) * nb, last), 0))
    return sub, nb, cur, prev, nxt


def _attn_fwd2(q, kv, dil):
    t = q.shape[1]
    sub, nb, cur, prev, _ = _att_specs(t, dil)

    def kern(q_ref, kvp_ref, kvc_ref, o_ref, lse_ref):
        n = pl.program_id(1)
        lane = lax.broadcasted_iota(jnp.int32, (CHUNK, LANE), 1)
        for b in range(nb):
            band = _band_qk((n == 0) if b == 0 else None)
            mask = jnp.concatenate([band, band], axis=0)

            def unit(r, carry, b=b, mask=mask):
                rows = _att_rows(b, r, dil)
                kvp = kvc_ref[_att_rows(b - 1, r, dil), :] if b > 0 else kvp_ref[_att_rows(0, r, dil), :]
                kvcat = jnp.concatenate([kvp, kvc_ref[rows, :]], axis=0).astype(BF16)
                qs = jnp.concatenate([q_ref.at[0][rows, :], q_ref.at[1][rows, :]], axis=0).astype(BF16)
                s = jnp.where(mask, _dot_nt(qs, kvcat), NEG)
                m = jnp.max(s, axis=1, keepdims=True)
                p = jnp.exp(s - m)
                l = jnp.sum(p, axis=1, keepdims=True)
                o = _dot(p.astype(BF16), kvcat) * (1.0 / l)
                o_ref.at[0][rows, :] = o[:CHUNK]
                o_ref.at[1][rows, :] = o[CHUNK:]
                lse = m + jnp.log(l)
                lse_ref[rows, :] = jnp.where(lane == 0, lse[:CHUNK], jnp.where(lane == 1, lse[CHUNK:], 0.0))
                return carry
            _for_residues(dil, unit)

    return pl.pallas_call(
        kern, name=f"attn_fwd_d{dil}", grid=(N_KV, t // ATT_SPAN), in_specs=[cur(2), prev(None), cur(None)],
        out_specs=[cur(2), cur(None)],
        out_shape=[jax.ShapeDtypeStruct((N_QH, t, LANE), F32), jax.ShapeDtypeStruct((N_KV, t, LANE), F32)],
        compiler_params=_cparams(("parallel", "parallel")))(q, kv, kv)


def _attn_dq2(q, kv, dot, lse, dsum, dil):
    t = q.shape[1]
    sub, nb, cur, prev, _ = _att_specs(t, dil)

    def kern(q_ref, kvp_ref, kvc_ref, do_ref, lse_ref, d_ref, dq_ref):
        n = pl.program_id(1)
        for b in range(nb):
            band = _band_qk((n == 0) if b == 0 else None)
            mask = jnp.concatenate([band, band], axis=0)

            def unit(r, carry, b=b, mask=mask):
                rows = _att_rows(b, r, dil)
                kvp = kvc_ref[_att_rows(b - 1, r, dil), :] if b > 0 else kvp_ref[_att_rows(0, r, dil), :]
                kvcat = jnp.concatenate([kvp, kvc_ref[rows, :]], axis=0).astype(BF16)
                lse_t, d_t = lse_ref[rows, :], d_ref[rows, :]
                qs = jnp.concatenate([q_ref.at[0][rows, :], q_ref.at[1][rows, :]], axis=0).astype(BF16)
                dos = jnp.concatenate([do_ref.at[0][rows, :], do_ref.at[1][rows, :]], axis=0).astype(BF16)
                lse2 = jnp.concatenate([lse_t[:, 0:1], lse_t[:, 1:2]], axis=0)
                d2 = jnp.concatenate([d_t[:, 0:1], d_t[:, 1:2]], axis=0)
                p = jnp.exp(jnp.where(mask, _dot_nt(qs, kvcat), NEG) - lse2)
                ds = p * (_dot_nt(dos, kvcat) - d2)
                dq = _dot(ds.astype(BF16), kvcat)
                dq_ref.at[0][rows, :] = dq[:CHUNK]
                dq_ref.at[1][rows, :] = dq[CHUNK:]
                return carry
            _for_residues(dil, unit)

    return pl.pallas_call(
        kern, name=f"attn_dq_d{dil}", grid=(N_KV, t // ATT_SPAN),
        in_specs=[cur(2), prev(None), cur(None), cur(2), cur(None), cur(None)], out_specs=cur(2),
        out_shape=jax.ShapeDtypeStruct((N_QH, t, LANE), F32),
        compiler_params=_cparams(("parallel", "parallel")))(q, kv, kv, dot, lse, dsum)


def _attn_dkv2(q, kv, dot, lse, dsum, dil):
    t = q.shape[1]
    sub, nb, cur, _, nxt = _att_specs(t, dil)
    nsteps = t // ATT_SPAN

    def kern(kv_ref, qc_ref, qn_ref, doc_ref, don_ref, lc_ref, ln_ref, dc_ref, dn_ref, dkv_ref):
        n = pl.program_id(1)
        for b in range(nb):
            inside = b < nb - 1
            mask = _band_kq(None if inside else (n == nsteps - 1))

            def unit(r, carry, b=b, inside=inside, mask=mask):
                rows = _att_rows(b, r, dil)
                nrows = _att_rows(b + 1, r, dil) if inside else _att_rows(0, r, dil)
                kvb = kv_ref[rows, :].astype(BF16)
                follow = lambda cref, nref: (cref if inside else nref)[nrows, :]
                lse_t = jnp.concatenate([lc_ref[rows, :].T, follow(lc_ref, ln_ref).T], axis=1)
                d_t = jnp.concatenate([dc_ref[rows, :].T, follow(dc_ref, dn_ref).T], axis=1)
                qdo = jnp.concatenate([qc_ref.at[0][rows, :], follow(qc_ref.at[0], qn_ref.at[0]),
                                       qc_ref.at[1][rows, :], follow(qc_ref.at[1], qn_ref.at[1]),
                                       doc_ref.at[0][rows, :], follow(doc_ref.at[0], don_ref.at[0]),
                                       doc_ref.at[1][rows, :], follow(doc_ref.at[1], don_ref.at[1])], axis=0).astype(BF16)
                both = _dot_nt(kvb, qdo)
                half = 4 * CHUNK
                mask2 = jnp.concatenate([mask, mask], axis=1)
                lse2 = jnp.concatenate([lse_t[0:1, :], lse_t[1:2, :]], axis=1)
                d2 = jnp.concatenate([d_t[0:1, :], d_t[1:2, :]], axis=1)
                pt = jnp.exp(jnp.where(mask2, both[:, :half], NEG) - lse2)
                dst = pt * (both[:, half:] - d2)
                dkv_ref[rows, :] = _dot(jnp.concatenate([dst, pt], axis=1).astype(BF16), qdo)
                return carry
            _for_residues(dil, unit)

    return pl.pallas_call(
        kern, name=f"attn_dkv_d{dil}", grid=(N_KV, nsteps),
        in_specs=[cur(None), cur(2), nxt(2), cur(2), nxt(2), cur(None), nxt(None), cur(None), nxt(None)], out_specs=cur(None),
        out_shape=jax.ShapeDtypeStruct((N_KV, t, LANE), F32),
        compiler_params=_cparams(("parallel", "parallel")))(kv, q, q, dot, dot, lse, lse, dsum, dsum)


HALO = 8
SSM_CONV_TM, SSM_CONV_W = 512, 512
FFN_CONV_TM, FFN_CONV_W = 256, 1408


def _halo_specs(tm, width, t_rows, col_off=0, lead=None):
    per, last = tm // HALO, t_rows // HALO - 1
    row_maps = (lambda i: i, lambda i: jnp.maximum(i * per - 1, 0), lambda i: jnp.minimum((i + 1) * per, last))
    specs = []
    for rows, rm in zip((tm, HALO, HALO), row_maps):
        if lead is None:
            specs.append(pl.BlockSpec((rows, width), functools.partial(lambda c, i, rm: (rm(i), c + col_off), rm=rm)))
        else:
            specs.append(pl.BlockSpec((None, rows, width), functools.partial(lambda c, i, rm: (lead, rm(i), c + col_off), rm=rm)))
    return specs


def _fill_ext(buf, tile_ref, before_ref, after_ref, i, nt):
    tm = tile_ref.shape[0]
    buf[0:HALO, :] = jnp.where(i > 0, before_ref[...].astype(F32), 0.0)
    buf[HALO:HALO + tm, :] = tile_ref[...].astype(F32)
    if after_ref is not None:
        buf[HALO + tm:, :] = jnp.where(i < nt - 1, after_ref[...].astype(F32), 0.0)


CONV_RB, CONV_CW = 16, 256


def _lane_chunks(width):
    return [slice(c0, min(c0 + CONV_CW, width)) for c0 in range(0, width, CONV_CW)]


def _shifted(buf, taps, r0, rows, cs):
    return [buf[pl.ds(HALO - (taps - 1) + k + r0, rows), cs] for k in range(taps)]


def _taps_fwd(xs, w, b):
    acc = b
    for k, xk in enumerate(xs):
        acc = acc + w[k:k + 1, :] * xk
    return acc


def _taps_bwd(bufd, w, taps, r0, rows, cs):
    acc = None
    for k in range(taps):
        term = w[k:k + 1, :] * bufd[pl.ds(r0 + (taps - 1) - k, rows), cs]
        acc = term if acc is None else acc + term
    return acc


def _fold8(z):
    return z[:HALO] + z[HALO:] if z.shape[0] == 2 * HALO else z


def _silu_grad(pre):
    sg = _sigmoid(pre)
    return sg * (1.0 + pre * (1.0 - sg))


def _ssm_conv_fwd(proj, w, b):
    t = proj.shape[0]
    tm, wd = min(SSM_CONV_TM, t), SSM_CONV_W
    nt, taps = t // tm, w.shape[0]

    def kern(x_ref, xb_ref, w_ref, b_ref, o_ref, buf):
        _fill_ext(buf, x_ref, xb_ref, None, pl.program_id(1), nt)
        for cs in _lane_chunks(wd):
            wv, bv = w_ref[:, cs], b_ref[:, cs]
            for r0 in range(0, tm, CONV_RB):
                pre = _taps_fwd(_shifted(buf, taps, r0, CONV_RB, cs), wv, bv)
                o_ref[r0:r0 + CONV_RB, cs] = pre * _sigmoid(pre)

    tile, before, _ = _halo_specs(tm, wd, t)
    par = lambda rows: pl.BlockSpec((rows, wd), lambda c, i: (0, c))
    return pl.pallas_call(
        kern, name="ssm_conv_fwd", grid=(XBC_DIM // wd, nt), in_specs=[tile, before, par(taps), par(1)],
        out_specs=pl.BlockSpec((tm, wd), lambda c, i: (i, c)), out_shape=jax.ShapeDtypeStruct((t, XBC_DIM), F32),
        scratch_shapes=[pltpu.VMEM((tm + HALO, wd), F32)],
        compiler_params=_cparams(("parallel", "parallel")))(proj, proj, w, b)


def _ssm_conv_bwd(proj, w, b, dact, parts):
    t = proj.shape[0]
    tm, wd = min(SSM_CONV_TM, t), SSM_CONV_W
    nt, taps, ncol, ns = t // tm, w.shape[0], XBC_DIM // SSM_CONV_W, len(parts)

    def kern(x_ref, xb_ref, xa_ref, d_ref, dn_ref, w_ref, b_ref, *rest):
        dx_ref, gw_ref, gb_ref = rest[ns:ns + 3]
        buf, bufd = rest[2 * ns + 3:2 * ns + 5]
        i = pl.program_id(1)
        if ns:
            start, finish = _scatter_steps(ns, rest[:ns], rest[ns + 3:2 * ns + 3], rest[2 * ns + 5:])
            pl.when((pl.program_id(0) == 0) & (i == 0))(start)
            pl.when((pl.program_id(0) == ncol - 1) & (i == nt - 1))(finish)
        _fill_ext(buf, x_ref, xb_ref, xa_ref, i, nt)

        @pl.when(i == 0)
        def _():
            gw_ref[...] = jnp.zeros(gw_ref.shape, F32)
            gb_ref[...] = jnp.zeros(gb_ref.shape, F32)
        for cs in _lane_chunks(wd):
            wv, bv = w_ref[:, cs], b_ref[:, cs]
            acc = [jnp.zeros((HALO, cs.stop - cs.start), F32) for _ in range(taps + 1)]
            for r0 in list(range(0, tm, CONV_RB)) + [tm]:
                inside = r0 < tm
                rows = CONV_RB if inside else HALO
                xs = _shifted(buf, taps, r0, rows, cs)
                d = d_ref[r0:r0 + rows, cs] if inside else jnp.where(i < nt - 1, dn_ref[:, cs], 0.0)
                dpre = d * _silu_grad(_taps_fwd(xs, wv, bv))
                bufd[r0:r0 + rows, cs] = dpre
                if inside:
                    acc[taps] = acc[taps] + _fold8(dpre)
                    for k in range(taps):
                        acc[k] = acc[k] + _fold8(dpre * xs[k])
            gb_ref[:, cs] += jnp.sum(acc[taps], axis=0, keepdims=True)
            for k in range(taps):
                gw_ref[k:k + 1, cs] += jnp.sum(acc[k], axis=0, keepdims=True)
            for r0 in range(0, tm, CONV_RB):
                dx_ref[r0:r0 + CONV_RB, cs] = _taps_bwd(bufd, wv, taps, r0, CONV_RB, cs).astype(BF16)

    xt, xb, xa = _halo_specs(tm, wd, t)
    dt_, _, dn = _halo_specs(tm, wd, t)
    par = lambda rows: pl.BlockSpec((rows, wd), lambda c, i: (0, c))
    return pl.pallas_call(
        kern, name="ssm_conv_bwd", grid=(ncol, nt), in_specs=[xt, xb, xa, dt_, dn, par(taps), par(1)] + [ANY_SPEC] * ns,
        out_specs=[pl.BlockSpec((tm, wd), lambda c, i: (i, c)), par(taps), par(1)] + [ANY_SPEC] * ns,
        out_shape=[jax.ShapeDtypeStruct((t, XBC_DIM), BF16), jax.ShapeDtypeStruct((taps, XBC_DIM), F32),
                   jax.ShapeDtypeStruct((1, XBC_DIM), F32)] + [jax.ShapeDtypeStruct(s.shape, s.dtype) for s in parts],
        scratch_shapes=[pltpu.VMEM((tm + 2 * HALO, wd), F32), pltpu.VMEM((tm + HALO, wd), F32)] + (_scatter_sems(ns) if ns else []),
        compiler_params=_cparams(("arbitrary", "arbitrary")))(proj, proj, proj, dact, dact, w, b, *parts)


def _ffn_act_down(u, w, b, w_down, x1):
    t = u.shape[1]
    tm, wd = min(FFN_CONV_TM, t), D_FF
    nt, taps = t // tm, w.shape[0]

    def kern(g_ref, gb_ref, v_ref, vb_ref, wg_ref, wv_ref, bg_ref, bv_ref, wd_ref, x1_ref, a_ref, x2_ref, bufg, bufv):
        i = pl.program_id(1)
        _fill_ext(bufg, g_ref, gb_ref, None, i, nt)
        _fill_ext(bufv, v_ref, vb_ref, None, i, nt)
        acc = x1_ref[...]
        for cs in _lane_chunks(wd):
            wg, wv, bg, bv = wg_ref[:, cs], wv_ref[:, cs], bg_ref[:, cs], bv_ref[:, cs]
            for r0 in range(0, tm, CONV_RB):
                g = _taps_fwd(_shifted(bufg, taps, r0, CONV_RB, cs), wg, bg)
                v = _taps_fwd(_shifted(bufv, taps, r0, CONV_RB, cs), wv, bv)
                a_ref[r0:r0 + CONV_RB, cs] = (g * _sigmoid(g) * v).astype(BF16)
            acc = acc + _dot(a_ref[:, cs], wd_ref[cs, :])
        x2_ref[...] = acc

    gt, gbf, _ = _halo_specs(tm, wd, t, lead=0)
    vt, vbf, _ = _halo_specs(tm, wd, t, lead=1)
    par = lambda rows, off: pl.BlockSpec((rows, wd), functools.partial(lambda c, i, off: (0, c + off), off=off))
    row = lambda width: pl.BlockSpec((tm, width), lambda c, i: (i, 0))
    return pl.pallas_call(
        kern, name="ffn_act_down", grid=(1, nt),
        in_specs=[gt, gbf, vt, vbf, par(taps, 0), par(taps, 1), par(1, 0), par(1, 1),
                  pl.BlockSpec(w_down.shape, lambda c, i: (0, 0)), row(D_MODEL)],
        out_specs=[row(wd), row(D_MODEL)],
        out_shape=[jax.ShapeDtypeStruct((t, D_FF), BF16), jax.ShapeDtypeStruct((t, D_MODEL), F32)],
        scratch_shapes=[pltpu.VMEM((tm + HALO, wd), F32)] * 2,
        compiler_params=_cparams(("parallel", "parallel")))(u, u, u, u, w, w, b, b, w_down, x1)


def _ffn_act_bwd(u, w, b, da):
    t = u.shape[1]
    tm, wd = min(FFN_CONV_TM, t), FFN_CONV_W
    nt, taps, nc = t // tm, w.shape[0], D_FF // FFN_CONV_W

    def kern(g_ref, gb_ref, ga_ref, v_ref, vb_ref, va_ref, d_ref, dn_ref, wg_ref, wv_ref, bg_ref, bv_ref,
             du_ref, gwg_ref, gwv_ref, gbg_ref, gbv_ref, bufg, bufv, bufdg, bufdv):
        i = pl.program_id(1)
        _fill_ext(bufg, g_ref, gb_ref, ga_ref, i, nt)
        _fill_ext(bufv, v_ref, vb_ref, va_ref, i, nt)

        @pl.when(i == 0)
        def _():
            for r in (gwg_ref, gwv_ref, gbg_ref, gbv_ref):
                r[...] = jnp.zeros(r.shape, F32)
        for cs in _lane_chunks(wd):
            wg, wv, bg, bv = wg_ref[:, cs], wv_ref[:, cs], bg_ref[:, cs], bv_ref[:, cs]
            zero = jnp.zeros((HALO, cs.stop - cs.start), F32)
            accg, accv = [zero] * (taps + 1), [zero] * (taps + 1)
            for r0 in list(range(0, tm, CONV_RB)) + [tm]:
                inside = r0 < tm
                rows = CONV_RB if inside else HALO
                xg, xv = _shifted(bufg, taps, r0, rows, cs), _shifted(bufv, taps, r0, rows, cs)
                g, v = _taps_fwd(xg, wg, bg), _taps_fwd(xv, wv, bv)
                dav = d_ref[r0:r0 + rows, cs] if inside else jnp.where(i < nt - 1, dn_ref[:, cs], 0.0)
                sg = _sigmoid(g)
                dg = dav * v * (sg * (1.0 + g * (1.0 - sg)))
                dv = dav * (g * sg)
                bufdg[r0:r0 + rows, cs] = dg
                bufdv[r0:r0 + rows, cs] = dv
                if inside:
                    accg[taps], accv[taps] = accg[taps] + _fold8(dg), accv[taps] + _fold8(dv)
                    for k in range(taps):
                        accg[k], accv[k] = accg[k] + _fold8(dg * xg[k]), accv[k] + _fold8(dv * xv[k])
            gbg_ref[:, cs] += jnp.sum(accg[taps], axis=0, keepdims=True)
            gbv_ref[:, cs] += jnp.sum(accv[taps], axis=0, keepdims=True)
            for k in range(taps):
                gwg_ref[k:k + 1, cs] += jnp.sum(accg[k], axis=0, keepdims=True)
                gwv_ref[k:k + 1, cs] += jnp.sum(accv[k], axis=0, keepdims=True)
            for r0 in range(0, tm, CONV_RB):
                du_ref[0, r0:r0 + CONV_RB, cs] = _taps_bwd(bufdg, wg, taps, r0, CONV_RB, cs).astype(BF16)
                du_ref[1, r0:r0 + CONV_RB, cs] = _taps_bwd(bufdv, wv, taps, r0, CONV_RB, cs).astype(BF16)

    gt, gbf, gaf = _halo_specs(tm, wd, t, lead=0)
    vt, vbf, vaf = _halo_specs(tm, wd, t, lead=1)
    dt_, _, dn = _halo_specs(tm, wd, t)
    par = lambda rows, off: pl.BlockSpec((rows, wd), functools.partial(lambda c, i, off: (0, c + off), off=off))
    return pl.pallas_call(
        kern, name="ffn_act_bwd", grid=(nc, nt),
        in_specs=[gt, gbf, gaf, vt, vbf, vaf, dt_, dn, par(taps, 0), par(taps, nc), par(1, 0), par(1, nc)],
        out_specs=[pl.BlockSpec((2, tm, wd), lambda c, i: (0, i, c)), par(taps, 0), par(taps, 0), par(1, 0), par(1, 0)],
        out_shape=[jax.ShapeDtypeStruct((2, t, D_FF), BF16)] + [jax.ShapeDtypeStruct((taps, D_FF), F32)] * 2
        + [jax.ShapeDtypeStruct((1, D_FF), F32)] * 2,
        scratch_shapes=[pltpu.VMEM((tm + 2 * HALO, wd), F32)] * 2 + [pltpu.VMEM((tm + HALO, wd), F32)] * 2,
        compiler_params=_cparams(("parallel", "arbitrary")))(u, u, u, u, u, u, da, da, w, w, b, b)


def _softplus(x):
    e = jnp.exp(-jnp.abs(x))
    return jnp.maximum(x, 0.0) + jnp.where(e < 1e-4, e - 0.5 * e * e, jnp.log(1.0 + e))


def _tri(lower):
    r = lax.broadcasted_iota(jnp.int32, (CHUNK, CHUNK), 0)
    c = lax.broadcasted_iota(jnp.int32, (CHUNK, CHUNK), 1)
    return (r >= c) if lower else (r <= c)


def _cum(mat_bool, x):
    return jnp.dot(mat_bool.astype(F32), x, precision=lax.Precision.HIGHEST, preferred_element_type=F32)


def _pair_sel(lane_lo, tile, h0):
    return jnp.where(lane_lo, tile[:, h0:h0 + 1], tile[:, h0 + 1:h0 + 2])


def _pair_sel_mxu(lane_lo, tile, h0):
    rows = lax.broadcasted_iota(jnp.int32, (LANE, LANE), 0)
    sel = (rows == jnp.where(lane_lo, h0, h0 + 1)).astype(BF16)
    return _dot_split(tile, sel)


def _ssd_fwd(xbc_act, proj, dt_bias_p, a_log_p, dskip_t, shards):
    t = xbc_act.shape[0]
    nch = t // CHUNK
    ns = len(shards)

    def kern(xa_ref, dtr_ref, bias_ref, alog_ref, dsk_ref, *rest):
        y_ref, dt_ref, hs_ref = rest[ns:ns + 3]
        hst = rest[2 * ns + 3]
        if ns:
            start, forward, finish = _gather_steps(shards, rest[:ns], rest[ns + 3:2 * ns + 3], rest[2 * ns + 4:])
            pl.when(pl.program_id(0) == 0)(start)
            pl.when(pl.program_id(0) == (3 * nch) // 4)(forward)
            pl.when(pl.program_id(0) == nch - 1)(finish)

        @pl.when(pl.program_id(0) == 0)
        def _():
            hst[...] = jnp.zeros(hst.shape, F32)
        dt = _softplus(dtr_ref[...] + bias_ref[...])
        dt_ref[...] = dt
        acum = _cum(_tri(True), dt * (-jnp.exp(alog_ref[...])))
        acum_t = acum.T
        ea = jnp.exp(acum)
        a_last = acum[CHUNK - 1:CHUNK, :]
        dend = jnp.exp(a_last - acum)
        ea_last = jnp.exp(a_last)
        causal = _tri(True)
        lane_lo = lax.broadcasted_iota(jnp.int32, (CHUNK, LANE), 1) < HEAD_DIM
        row_lo = lax.broadcasted_iota(jnp.int32, (CHUNK, LANE), 0) < HEAD_DIM
        for g in range(2):
            bg = xa_ref[:, SSM_INNER + g * SSM_STATE:SSM_INNER + (g + 1) * SSM_STATE].astype(BF16)
            cg = xa_ref[:, SSM_INNER + BC_DIM + g * SSM_STATE:SSM_INNER + BC_DIM + (g + 1) * SSM_STATE].astype(BF16)
            cb = _dot_nt(cg, bg)
            for j in range(4 * g, 4 * g + 4):
                h0 = 2 * j
                cols = slice(j * LANE, (j + 1) * LANE)
                xp = xa_ref[:, cols]
                xdt = xp * _pair_sel(lane_lo, dt, h0)
                ydiag = None
                for hh, sel in ((h0, lane_lo), (h0 + 1, ~lane_lo)):
                    seg = acum[:, hh:hh + 1] - acum_t[hh:hh + 1, :]
                    mm = (cb * jnp.where(causal, jnp.exp(jnp.minimum(seg, 0.0)), 0.0)).astype(BF16)
                    d = _dot(mm, jnp.where(sel, xdt, 0.0).astype(BF16))
                    ydiag = d if ydiag is None else ydiag + d
                hp = hst[cols, :]
                hs_ref[cols, :] = hp
                yoff = _dot_nt(cg, hp.astype(BF16)) * _pair_sel(lane_lo, ea, h0)
                y_ref[:, cols] = ydiag + yoff + dsk_ref[:, cols] * xp
                xw = (xdt * _pair_sel(lane_lo, dend, h0)).astype(BF16)
                rowf = jnp.where(row_lo, ea_last[:, h0:h0 + 1], ea_last[:, h0 + 1:h0 + 2])
                hst[cols, :] = hp * rowf + _dot_tn(xw, bg)

    return pl.pallas_call(
        kern, name="ssd_fwd", grid=(nch,),
        in_specs=[pl.BlockSpec((CHUNK, XBC_DIM), lambda c: (c, 0)), pl.BlockSpec((CHUNK, LANE), lambda c: (c, OFF_DT // LANE)),
                  pl.BlockSpec((1, LANE), lambda c: (0, 0)), pl.BlockSpec((1, LANE), lambda c: (0, 0)),
                  pl.BlockSpec((1, SSM_INNER), lambda c: (0, 0))] + [ANY_SPEC] * ns,
        out_specs=[pl.BlockSpec((CHUNK, SSM_INNER), lambda c: (c, 0)), pl.BlockSpec((CHUNK, LANE), lambda c: (c, 0)),
                   pl.BlockSpec((None, SSM_INNER, SSM_STATE), lambda c: (c, 0, 0))] + [ANY_SPEC] * ns,
        out_shape=[jax.ShapeDtypeStruct((t, SSM_INNER), F32), jax.ShapeDtypeStruct((t, LANE), F32),
                   jax.ShapeDtypeStruct((nch, SSM_INNER, SSM_STATE), F32)] + _gather_out_shapes(shards),
        scratch_shapes=[pltpu.VMEM((SSM_INNER, SSM_STATE), F32)] + (_gather_sems(ns) if ns else []),
        compiler_params=_cparams(("arbitrary",)))(xbc_act, proj, dt_bias_p, a_log_p, dskip_t, *shards)


def _ssd_bwd(xbc_act, proj, dt_sp, hstates, dy, dt_bias_p, a_log_p, dskip_t, swaps):
    t = xbc_act.shape[0]
    nch = t // CHUNK
    ns = len(swaps)

    pair = jnp.arange(SSM_HEADS // 2)[:, None, None]
    psel = (jnp.arange(LANE)[None, None, :] == 2 * pair + (jnp.arange(LANE) // HEAD_DIM)[None, :, None]).astype(BF16)

    def kern(xa_ref, dtr_ref, dt_ref, hs_ref, dy_ref, bias_ref, alog_ref, dsk_ref, psel_ref, *rest):
        dact_ref, ddtr_ref, da_ref, dbias_ref, ddsk_ref = rest[ns:ns + 5]
        dh = rest[2 * ns + 5]
        if ns:
            start, finish = _swap_steps(swaps, rest[:ns], rest[ns + 5:2 * ns + 5], rest[2 * ns + 6:])
            pl.when(pl.program_id(0) == 0)(start)
            pl.when(pl.program_id(0) == nch - 1)(finish)

        @pl.when(pl.program_id(0) == 0)
        def _():
            dh[...] = jnp.zeros(dh.shape, F32)
            for r in (da_ref, dbias_ref, ddsk_ref):
                r[...] = jnp.zeros(r.shape, F32)
        dt = dt_ref[...]
        a_neg = -jnp.exp(alog_ref[...])
        acum = _cum(_tri(True), dt * a_neg)
        acum_t = acum.T
        ea = jnp.exp(acum)
        a_last = acum[CHUNK - 1:CHUNK, :]
        dend = jnp.exp(a_last - acum)
        ea_last = jnp.exp(a_last)
        causal = _tri(True)
        lane = lax.broadcasted_iota(jnp.int32, (CHUNK, LANE), 1)
        rowi = lax.broadcasted_iota(jnp.int32, (CHUNK, LANE), 0)
        lane_lo, row_lo, last_row = lane < HEAD_DIM, rowi < HEAD_DIM, rowi == CHUNK - 1
        d_dt = jnp.zeros((CHUNK, LANE), F32)
        d_acum = jnp.zeros((CHUNK, LANE), F32)
        for g in range(2):
            bcols = slice(SSM_INNER + g * SSM_STATE, SSM_INNER + (g + 1) * SSM_STATE)
            ccols = slice(SSM_INNER + BC_DIM + g * SSM_STATE, SSM_INNER + BC_DIM + (g + 1) * SSM_STATE)
            bg, cg = xa_ref[:, bcols].astype(BF16), xa_ref[:, ccols].astype(BF16)
            cb = _dot_nt(cg, bg)
            dg_sum = jnp.zeros((CHUNK, CHUNK), F32)
            dcg = jnp.zeros((CHUNK, SSM_STATE), F32)
            dbg = jnp.zeros((CHUNK, SSM_STATE), F32)
            for j in range(4 * g, 4 * g + 4):
                h0 = 2 * j
                cols = slice(j * LANE, (j + 1) * LANE)
                xp, dyp = xa_ref[:, cols], dy_ref[:, cols]
                dtsel = _pair_sel_mxu(lane_lo, dt, h0)
                xdt = xp * dtsel
                xdt_b = xdt.astype(BF16)
                hp, dhp = hs_ref[cols, :], dh[cols, :]
                hp_b, dhp_b = hp.astype(BF16), dhp.astype(BF16)
                easel, dendsel = _pair_sel_mxu(lane_lo, ea, h0), _pair_sel_mxu(lane_lo, dend, h0)
                dx, ydiag = None, None
                for hh, sel in ((h0, lane_lo), (h0 + 1, ~lane_lo)):
                    dyh = jnp.where(sel, dyp, 0.0).astype(BF16)
                    seg = acum[:, hh:hh + 1] - acum_t[hh:hh + 1, :]
                    dec = jnp.where(causal, jnp.exp(jnp.minimum(seg, 0.0)), 0.0)
                    mm_b = (cb * dec).astype(BF16)
                    dg_sum = dg_sum + dec * _dot_nt(dyh, xdt_b)
                    d = _dot_tn(mm_b, dyh)
                    y = _dot(mm_b, jnp.where(sel, xdt, 0.0).astype(BF16))
                    dx = d if dx is None else dx + d
                    ydiag = y if ydiag is None else ydiag + y
                g2 = _dot_nt(bg, dhp_b)
                tprod = xdt * g2 * dendsel
                yoff = _dot_nt(cg, hp_b) * easel
                yc = dyp.astype(BF16).astype(F32) * ydiag + dyp * yoff - (xdt_b.astype(F32) * dx + tprod)
                dx = dx + g2 * dendsel
                psel = psel_ref[j]
                t_lo = jnp.sum(jnp.where(lane_lo, tprod, 0.0), keepdims=True).reshape(1, 1)
                t_hi = jnp.sum(tprod, keepdims=True).reshape(1, 1) - t_lo
                hh_prod = dhp * hp
                s_lo = jnp.sum(jnp.where(row_lo, hh_prod, 0.0), keepdims=True).reshape(1, 1)
                s_hi = jnp.sum(hh_prod, keepdims=True).reshape(1, 1) - s_lo
                end_lo = ea_last[:, h0:h0 + 1] * s_lo + t_lo
                end_hi = ea_last[:, h0 + 1:h0 + 2] * s_hi + t_hi
                ends = jnp.where(lane == h0, end_lo, jnp.where(lane == h0 + 1, end_hi, 0.0))
                d_acum = d_acum + _dot_split(yc, psel) + jnp.where(last_row, ends, 0.0)
                dye = (dyp * easel).astype(BF16)
                dcg = dcg + _dot(dye, hp_b)
                dbg = dbg + _dot((xdt * dendsel).astype(BF16), dhp_b)
                rowf = jnp.where(row_lo, ea_last[:, h0:h0 + 1], ea_last[:, h0 + 1:h0 + 2])
                dh[cols, :] = dhp * rowf + _dot_tn(dye, cg)
                dact_ref[:, cols] = dx * dtsel + dsk_ref[:, cols] * dyp
                d_dt = d_dt + _dot_split(dx * xp, psel)
                ddsk_ref[:, cols] += jnp.sum(dyp * xp, axis=0, keepdims=True)
            dg_b = dg_sum.astype(BF16)
            dact_ref[:, ccols] = dcg + _dot(dg_b, bg)
            dact_ref[:, bcols] = dbg + _dot_tn(dg_b, cg)
        d_adt = _cum(_tri(False), d_acum)
        d_dt = d_dt + d_adt * a_neg
        da_ref[...] += jnp.sum(d_adt * dt, axis=0, keepdims=True)
        d_raw = jnp.where(lane < SSM_HEADS, d_dt * _sigmoid(dtr_ref[...] + bias_ref[...]), 0.0)
        ddtr_ref[...] = d_raw.astype(BF16)
        dbias_ref[...] += jnp.sum(d_raw, axis=0, keepdims=True)

    rev = lambda c: (nch - 1 - c, 0)
    return pl.pallas_call(
        kern, name="ssd_bwd", grid=(nch,),
        in_specs=[pl.BlockSpec((CHUNK, XBC_DIM), rev), pl.BlockSpec((CHUNK, LANE), lambda c: (nch - 1 - c, OFF_DT // LANE)),
                  pl.BlockSpec((CHUNK, LANE), rev), pl.BlockSpec((None, SSM_INNER, SSM_STATE), lambda c: (nch - 1 - c, 0, 0)),
                  pl.BlockSpec((CHUNK, SSM_INNER), rev),
                  pl.BlockSpec((1, LANE), lambda c: (0, 0)), pl.BlockSpec((1, LANE), lambda c: (0, 0)),
                  pl.BlockSpec((1, SSM_INNER), lambda c: (0, 0)), pl.BlockSpec(psel.shape, lambda c: (0, 0, 0))] + [ANY_SPEC] * ns,
        out_specs=[pl.BlockSpec((CHUNK, XBC_DIM), rev), pl.BlockSpec((CHUNK, LANE), rev),
                   pl.BlockSpec((1, LANE), lambda c: (0, 0)), pl.BlockSpec((1, LANE), lambda c: (0, 0)),
                   pl.BlockSpec((1, SSM_INNER), lambda c: (0, 0))] + [ANY_SPEC] * ns,
        out_shape=[jax.ShapeDtypeStruct((t, XBC_DIM), F32), jax.ShapeDtypeStruct((t, LANE), BF16),
                   jax.ShapeDtypeStruct((1, LANE), F32), jax.ShapeDtypeStruct((1, LANE), F32),
                   jax.ShapeDtypeStruct((1, SSM_INNER), F32)] + _swap_out_shapes(swaps),
        scratch_shapes=[pltpu.VMEM((SSM_INNER, SSM_STATE), F32)] + (_swap_sems(ns) if ns else []),
        compiler_params=_cparams(("arbitrary",)))(xbc_act, proj, dt_sp, hstates, dy, dt_bias_p, a_log_p, dskip_t, psel, *swaps)


def _pad_lanes(v, width=LANE):
    return jnp.pad(v, ((0, 0), (0, width - v.shape[1])))


def _local_step(x, p, tgt, wts, late_shards=(), join_late=None, reduce_early=None, reduce_late=None):
    g_attn, g_ssm, g_ffn, g_ple = wts["attn_norm_g"], wts["ssm_norm_g"], wts["ffn_norm_g"], wts["ple_norm_g"]
    w_in_p = wts["w_in_p"]
    gq_t = jnp.tile(wts["q_norm_g"], (1, ATTN_DIM // HEAD_DIM))
    gk_t = jnp.tile(wts["k_norm_g"], (1, KV_DIM // HEAD_DIM))
    dt_bias_p, a_log_p = _pad_lanes(wts["dt_bias"]), _pad_lanes(wts["a_log"])
    dskip_t = jnp.repeat(wts["d_skip"], HEAD_DIM, axis=1)

    h1, proj, q_hm, kv_hm = _in_proj(x, g_attn, w_in_p, gq_t, gk_t)
    pats = [_attn_fwd2(q_hm, kv_hm, d) for d in DILATIONS]
    xbc_act =_ssm_conv_fwd(proj, wts["ssm_conv_w"], wts["ssm_conv_b"])
    y_ssd, dt_sp, hstates, *gathered = _ssd_fwd(xbc_act, proj, dt_bias_p, a_log_p, dskip_t, list(late_shards))
    if join_late is not None:
        wts = {**wts, **join_late(gathered)}
    w_out_s, w_out_a = wts["w_out_ssm"], wts["w_out_attn"]
    w_up, w_down, w_gate, w_proj = wts["w_up"], wts["w_down"], wts["w_ple_gate"], wts["w_ple_proj"]
    ssm_out, attn_out, lse, x1 = _mix_out_proj(y_ssd, proj, g_ssm, w_out_s, [o for o, _ in pats], [l for _, l in pats], w_out_a, x)
    h2, u = _norm_mm("ffn_up", x1, g_ffn, w_up, tm=1024, tn=1408, halves=True)
    a, x2 = _ffn_act_down(u, wts["ffn_conv_w"], wts["ffn_conv_b"], w_down, x1)
    pb = p.astype(BF16)
    h3, dy, dgl, dpp, sq = _ple_head(x2, g_ple, w_gate, pb, w_proj, tgt)

    grads = {}
    grads["w_ple_proj"] = _mm_tn("g_ple_proj", pb, dpp, tn=PLE_DIM, chip_cols=True)
    grads["w_ple_gate"] = _mm_tn("g_ple_gate", h3, dgl)
    dx2, dx2b, grads["ple_norm_g"] = _mm_nt_rms_bwd("d_h3", dgl, w_gate, x2, g_ple, dy)
    da = _mm_nt("d_ffn_act", [(dx2b, w_down, 0)], F32, tm=1024, tn=1408)
    grads["w_down"] = _mm_tn("g_ffn_down", a, dx2b, tm=1408)
    du, gwg, gwv, gbg, gbv = _ffn_act_bwd(u, wts["ffn_conv_w"], wts["ffn_conv_b"], da)
    grads["ffn_conv_w"] = jnp.concatenate([gwg, gwv], axis=1)
    grads["ffn_conv_b"] = jnp.concatenate([gbg, gbv], axis=1)
    grads["w_up"] = _mm_tn("g_ffn_up", h2, du, tn=1408, chip_cols=True)
    dh2 = _mm_nt("d_h2", [(du, w_up, 0, 0), (du, w_up, 1, 1)], F32, tm=1024, tn=512)
    dx1, dx1b, grads["ffn_norm_g"] = _rms_bwd("rms_ffn_bwd", dh2, x1, g_ffn, dx2)
    dy_ssd, dz, grads["ssm_norm_g"], do_hm, dsum = _d_mix(dx1b, jnp.concatenate([w_out_s, w_out_a], axis=0), y_ssd, proj, g_ssm,
                                                            attn_out)
    grads["w_out"] = jnp.concatenate([_mm_tn("g_out_attn", attn_out, dx1b), _mm_tn("g_out_ssm", ssm_out, dx1b)], axis=0)
    early_major = reduce_early[0](grads) if reduce_early is not None else []
    dact, ddtr, d_a, d_bias, d_dsk, *early_got = _ssd_bwd(xbc_act, proj, dt_sp, hstates, dy_ssd, dt_bias_p, a_log_p, dskip_t,
                                                           early_major)
    grads["dt_bias"] = d_bias[:, :SSM_HEADS]
    grads["a_log"] = d_a[:, :SSM_HEADS] * (-jnp.exp(wts["a_log"]))
    grads["d_skip"] = jnp.sum(d_dsk.reshape(SSM_HEADS, HEAD_DIM), axis=1)[None, :]
    chip_sums = reduce_early[1](early_major, early_got) if reduce_early is not None else []
    dxbc, grads["ssm_conv_w"], grads["ssm_conv_b"], *scattered = _ssm_conv_bwd(proj, wts["ssm_conv_w"], wts["ssm_conv_b"], dact,
                                                                                chip_sums)
    dqs = [_attn_dq2(q_hm, kv_hm, do_hm, lse, dsum, d) for d in DILATIONS]
    dkvs = [_attn_dkv2(q_hm, kv_hm, do_hm, lse, dsum, d) for d in DILATIONS]
    dq, dk, dv, dgq, dgk = _qknorm_bwd2(proj, gq_t, gk_t, dqs, dkvs)
    grads["q_norm_g"] = jnp.sum(dgq.reshape(ATTN_DIM // HEAD_DIM, HEAD_DIM), axis=0)[None, :]
    grads["k_norm_g"] = jnp.sum(dgk.reshape(KV_DIM // HEAD_DIM, HEAD_DIM), axis=0)[None, :]
    dproj = jnp.concatenate([dxbc, dq, dz, dk, dv, ddtr], axis=1)
    grads["w_in_p"] = _mm_tn("g_in_proj", h1, dproj, tm=512)
    late_sums = reduce_late(grads) if reduce_late is not None else []
    grad_x, _, grads["attn_norm_g"], *late_scattered = _mm_nt_rms_bwd("d_h1", dproj, w_in_p, x, g_attn, dx1, late_sums)
    return sq, grad_x, grads, (chip_sums, scattered), (late_sums, late_scattered)


MESH_IDS = pl.DeviceIdType.MESH
N_CHIPS = 4
ANY_SPEC = pl.BlockSpec(memory_space=pl.ANY)
SMALL_ROWS = 96
ALL_SMALL_ROWS = 272


def _place():
    x, y, c = lax.axis_index("x"), lax.axis_index("y"), lax.axis_index("c")
    return x, y, c, [(1 - x, y), (x, 1 - y), (1 - x, 1 - y)]


def _gather_over_chips(arrs):
    n = len(arrs)

    def body(*refs):
        steps = _gather_steps(arrs, refs[:n], refs[n:2 * n], refs[2 * n:2 * n + 4])
        for step in steps:
            step()

    return pl.pallas_call(
        body, name="gather_weights", in_specs=[ANY_SPEC] * n, out_specs=[ANY_SPEC] * n,
        out_shape=_gather_out_shapes(arrs), scratch_shapes=_gather_sems(n))(*arrs)


def _gather_out_shapes(arrs):
    return [jax.ShapeDtypeStruct((N_CHIPS,) + a.shape, a.dtype) for a in arrs]


def _gather_sems(n):
    return [pltpu.SemaphoreType.DMA((3 * n,))] * 4


def _gather_steps(arrs, ins, outs, sems):
    n = len(arrs)
    split = [a.shape[0] % 64 == 0 for a in arrs]
    ici_send, ici_recv, d2d_send, d2d_recv = sems

    def place():
        x, y, c, chips = _place()
        return x, y, c, chips, 2 * x + y

    def part(ref, a, core):
        if not split[a]:
            return ref
        half = arrs[a].shape[0] // 2
        return ref.at[pl.ds(core * half, half)]

    def ici(a, k, slot, where):
        x, y, c, chips, _ = where
        px, py = chips[k]
        return pltpu.make_async_remote_copy(
            src_ref=part(ins[a], a, c), dst_ref=part(outs[a].at[slot], a, c), send_sem=ici_send.at[3 * a + k],
            recv_sem=ici_recv.at[3 * a + k], device_id=(px, py, c), device_id_type=MESH_IDS)

    def d2d(a, k, core, where):
        x, y, c, chips, _ = where
        px, py = chips[k]
        piece = part(outs[a].at[2 * px + py], a, core)
        return pltpu.make_async_remote_copy(src_ref=piece, dst_ref=piece, send_sem=d2d_send.at[3 * a + k],
                                            recv_sem=d2d_recv.at[3 * a + k], device_id=(x, y, 1 - c), device_id_type=MESH_IDS)

    def start():
        w = place()
        for a in range(n):
            for k in range(3):
                ici(a, k, w[4], w).start()

    def forward():
        w = place()
        for a in range(n):
            for k, (px, py) in enumerate(w[3]):
                ici(a, k, 2 * px + py, w).wait_recv()
                if split[a]:
                    d2d(a, k, w[2], w).start()

    def finish():
        w = place()
        for a in range(n):
            for k in range(3):
                if split[a]:
                    d2d(a, k, 1 - w[2], w).wait_recv()
                    d2d(a, k, w[2], w).wait_send()
                ici(a, k, w[4], w).wait_send()

    return start, forward, finish


def _row_tile(rows, cap=256):
    return max(d for d in range(8, cap + 1, 8) if rows % d == 0)


def _swap_halves(name, gs):
    n = len(gs)

    def body(*refs):
        for step in _swap_steps(gs, refs[:n], refs[n:2 * n], refs[2 * n:2 * n + 2]):
            step()

    return pl.pallas_call(
        body, name=name, in_specs=[ANY_SPEC] * n, out_specs=[ANY_SPEC] * n, out_shape=_swap_out_shapes(gs),
        scratch_shapes=_swap_sems(n))(*gs)


def _swap_out_shapes(gs):
    return [jax.ShapeDtypeStruct((N_CHIPS, g.shape[1] // 2, g.shape[2]), g.dtype) for g in gs]


def _swap_sems(n):
    return [pltpu.SemaphoreType.DMA((N_CHIPS * n,))] * 2


def _swap_steps(gs, ins, outs, sems):
    send, recv = sems

    def copies():
        x, y, c, _ = _place()
        cps = []
        for a in range(len(gs)):
            half = gs[a].shape[1] // 2
            for q in range(N_CHIPS):
                cps.append(pltpu.make_async_remote_copy(
                    src_ref=ins[a].at[q, pl.ds((1 - c) * half, half)], dst_ref=outs[a].at[q], send_sem=send.at[N_CHIPS * a + q],
                    recv_sem=recv.at[N_CHIPS * a + q], device_id=(x, y, 1 - c), device_id_type=MESH_IDS))
        return cps

    def start():
        for cp in copies():
            cp.start()

    def finish():
        for cp in copies():
            cp.wait()

    return start, finish


def _add_halves(name, g, got, c_idx):
    rows, cols = got.shape[1:]
    tm = _row_tile(rows)
    per = rows // tm

    def kern(c_ref, g_ref, r_ref, o_ref):
        o_ref[...] = (g_ref[...] + r_ref[...]).astype(BF16)

    return pl.pallas_call(
        kern, name=name,
        grid_spec=pltpu.PrefetchScalarGridSpec(
            num_scalar_prefetch=1, grid=(N_CHIPS, per),
            in_specs=[pl.BlockSpec((None, tm, cols), lambda q, i, c_ref: (q, c_ref[0] * per + i, 0)),
                      pl.BlockSpec((None, tm, cols), lambda q, i, c_ref: (q, i, 0))],
            out_specs=pl.BlockSpec((None, tm, cols), lambda q, i, c_ref: (q, i, 0))),
        out_shape=jax.ShapeDtypeStruct((N_CHIPS, rows, cols), BF16),
        compiler_params=_cparams(("parallel", "parallel")))(c_idx, g, got)


def _scatter_sems(n):
    return [pltpu.SemaphoreType.DMA((3 * n,))] * 2


def _scatter_steps(n, ins, outs, sems):
    send, recv = sems

    def copy(a, k, slot, where):
        x, y, c, chips = where
        px, py = chips[k]
        return pltpu.make_async_remote_copy(src_ref=ins[a].at[2 * px + py], dst_ref=outs[a].at[slot], send_sem=send.at[3 * a + k],
                                            recv_sem=recv.at[3 * a + k], device_id=(px, py, c), device_id_type=MESH_IDS)

    def start():
        w = _place()
        for a in range(n):
            for k in range(3):
                copy(a, k, 2 * w[0] + w[1], w).start()

    def finish():
        w = _place()
        for a in range(n):
            for k, (px, py) in enumerate(w[3]):
                copy(a, k, 2 * px + py, w).wait()

    return start, finish


def _sum_chips(name, own, parts, idx):
    rows, cols = parts.shape[1:]
    tm = _row_tile(rows)
    per = rows // tm

    def kern(o_idx, a_ref, b_ref, c_ref, d_ref, o_ref):
        o_ref[...] = ((a_ref[...].astype(F32) + b_ref[...].astype(F32)) + c_ref[...].astype(F32)) + d_ref[...].astype(F32)

    def spec(k):
        return pl.BlockSpec((None, tm, cols), functools.partial(lambda i, o_idx, k: (o_idx[k], i, 0), k=k))

    return pl.pallas_call(
        kern, name=name,
        grid_spec=pltpu.PrefetchScalarGridSpec(
            num_scalar_prefetch=1, grid=(per,), in_specs=[spec(0), spec(1), spec(2), spec(3)],
            out_specs=pl.BlockSpec((None, tm, cols), lambda i, o_idx: (0, o_idx[4] * per + i, 0))),
        out_shape=jax.ShapeDtypeStruct((1, 2 * rows, cols), F32), compiler_params=_cparams(("parallel",)))(idx, own, parts, parts, parts)


def _share_with_sibling(gs):
    n = len(gs)

    def body(*refs):
        ins, send, recv = refs[:n], refs[2 * n], refs[2 * n + 1]
        x, y, c, _ = _place()
        cps = []
        for a in range(n):
            half = gs[a].shape[1] // 2
            mine = pl.ds(c * half, half)
            cps.append(pltpu.make_async_remote_copy(src_ref=ins[a].at[0, mine], dst_ref=refs[n + a].at[0, mine], send_sem=send.at[a],
                                                    recv_sem=recv.at[a], device_id=(x, y, 1 - c), device_id_type=MESH_IDS))
        for cp in cps:
            cp.start()
        for cp in cps:
            cp.wait()

    return pl.pallas_call(
        body, name="grad_share_sibling", in_specs=[ANY_SPEC] * n, out_specs=[ANY_SPEC] * n,
        out_shape=[jax.ShapeDtypeStruct(g.shape, g.dtype) for g in gs], input_output_aliases={a: a for a in range(n)},
        scratch_shapes=[pltpu.SemaphoreType.DMA((n,))] * 2)(*gs)


def _allreduce_small(v):
    def body(v_ref, o_ref, land, send, recv):
        x, y, c, _ = _place()
        me = 4 * x + 2 * y + c
        land[me] = v_ref[...]
        cps = []
        for rel in range(1, 8):
            bx, by, bc = (rel >> 2) & 1, (rel >> 1) & 1, rel & 1
            peer = (1 - x if bx else x, 1 - y if by else y, 1 - c if bc else c)
            cps.append(pltpu.make_async_remote_copy(src_ref=v_ref, dst_ref=land.at[me], send_sem=send.at[rel - 1],
                                                    recv_sem=recv.at[rel - 1], device_id=peer, device_id_type=MESH_IDS))
        for cp in cps:
            cp.start()
        for cp in cps:
            cp.wait()
        acc = land[0]
        for d in range(1, 8):
            acc = acc + land[d]
        o_ref[...] = acc

    vm = pl.BlockSpec(memory_space=pltpu.VMEM)
    return pl.pallas_call(
        body, name="allreduce_small", in_specs=[vm], out_specs=vm, out_shape=jax.ShapeDtypeStruct(v.shape, F32),
        scratch_shapes=[pltpu.VMEM((8,) + v.shape, F32), pltpu.SemaphoreType.DMA((7,)), pltpu.SemaphoreType.DMA((7,))])(v)


def _adamw(name, w, g, m, v):
    _, rows, cols = w.shape
    tm = rows if rows * cols <= 128 * 1024 else _row_tile(rows, max(256, 2048 * LANE // cols))
    c1 = 1.0 / (1.0 - ADAM_B1 ** ADAM_STEP)
    c2 = 1.0 / (1.0 - ADAM_B2 ** ADAM_STEP)

    def kern(w_ref, g_ref, m_ref, v_ref, d_ref, mo_ref, vo_ref):
        gv = g_ref[...]
        mn = ADAM_B1 * m_ref[...] + (1.0 - ADAM_B1) * gv
        vn = ADAM_B2 * v_ref[...] + (1.0 - ADAM_B2) * (gv * gv)
        d_ref[...] = -ADAM_LR * ((mn * c1) / (jnp.sqrt(vn * c2) + ADAM_EPS) + ADAM_WD * w_ref[...])
        mo_ref[...] = mn
        vo_ref[...] = vn

    spec = pl.BlockSpec((None, tm, cols), lambda i: (0, i, 0))
    return pl.pallas_call(
        kern, name=name, grid=(rows // tm,), in_specs=[spec] * 4, out_specs=[spec] * 3,
        out_shape=[jax.ShapeDtypeStruct(w.shape, F32)] * 3, compiler_params=_cparams(("parallel",)))(w, g, m, v)


SHARDED = (("w_in", 1), ("w_out", 0), ("w_up", 1), ("w_down", 0), ("w_ple_gate", 0), ("w_ple_proj", 1),
           ("ssm_conv_w", 1), ("ffn_conv_w", 1))
MATRICES = ("w_in", "w_out", "w_up", "w_down", "w_ple_gate", "w_ple_proj")
EARLY_REDUCED = MATRICES[1:]
REPLICATED = ("attn_norm_g", "q_norm_g", "k_norm_g", "ssm_conv_b", "dt_bias", "a_log", "d_skip", "ssm_norm_g",
              "ffn_norm_g", "ffn_conv_b", "ple_norm_g")
WEIGHT_ORDER = ("attn_norm_g", "w_in", "q_norm_g", "k_norm_g", "ssm_conv_w", "ssm_conv_b", "dt_bias", "a_log", "d_skip",
                "ssm_norm_g", "w_out", "ffn_norm_g", "w_up", "ffn_conv_w", "ffn_conv_b", "w_down", "ple_norm_g",
                "w_ple_gate", "w_ple_proj")


def _join_chips(g, axis):
    if axis == 0:
        return g.reshape(g.shape[0] * g.shape[1], g.shape[2])
    return jnp.transpose(g, (1, 0, 2)).reshape(g.shape[1], g.shape[0] * g.shape[2])


def _split_chips(g, axis):
    if axis == 0:
        return g.reshape(N_CHIPS, g.shape[0] // N_CHIPS, g.shape[1])
    r, c = g.shape
    return jnp.transpose(g.reshape(r, N_CHIPS, c // N_CHIPS), (1, 0, 2))


def _pack_small(vals, rows=SMALL_ROWS):
    flat = jnp.concatenate([v.reshape(-1) for v in vals])
    return jnp.pad(flat, (0, rows * LANE - flat.shape[0])).reshape(rows, LANE)


def _unpack_small(packed, like):
    flat, out, off = packed.reshape(-1), [], 0
    for v in like:
        out.append(flat[off:off + v.size].reshape(v.shape))
        off += v.size
    return out


def kernel(x, p, attn_norm_g, w_in, q_norm_g, k_norm_g, ssm_conv_w, ssm_conv_b, dt_bias, a_log, d_skip, ssm_norm_g, w_out, ffn_norm_g, w_up, ffn_conv_w, ffn_conv_b, w_down, ple_norm_g, w_ple_gate, w_ple_proj, loss_target, m_attn_norm_g, m_w_in, m_q_norm_g, m_k_norm_g, m_ssm_conv_w, m_ssm_conv_b, m_dt_bias, m_a_log, m_d_skip, m_ssm_norm_g, m_w_out, m_ffn_norm_g, m_w_up, m_ffn_conv_w, m_ffn_conv_b, m_w_down, m_ple_norm_g, m_w_ple_gate, m_w_ple_proj, v_attn_norm_g, v_w_in, v_q_norm_g, v_k_norm_g, v_ssm_conv_w, v_ssm_conv_b, v_dt_bias, v_a_log, v_d_skip, v_ssm_norm_g, v_w_out, v_ffn_norm_g, v_w_up, v_ffn_conv_w, v_ffn_conv_b, v_w_down, v_ple_norm_g, v_w_ple_gate, v_w_ple_proj):
    given = dict(locals())
    w2 = {n: given[n].reshape(given[n].shape[-2:]) if given[n].ndim == 3 else given[n] for n in WEIGHT_ORDER}

    cx, cy, cc = lax.axis_index("x"), lax.axis_index("y"), lax.axis_index("c")
    chip = 2 * cx + cy
    axis_of = dict(SHARDED)
    shard = lambda n: w2[n].astype(BF16) if n in MATRICES else w2[n]
    join = lambda n, g: _join_chips(lax.dynamic_update_index_in_dim(g, shard(n), chip, 0), axis_of[n])
    first = ("w_in", "ssm_conv_w", "ffn_conv_w")
    full = {n: join(n, g) for n, g in zip(first, _gather_over_chips([shard(n) for n in first]))}
    win = full["w_in"]
    w_in_p = jnp.concatenate([win[:, 2048:3584], win[:, 0:512], win[:, 1024:2048], win[:, 512:768], win[:, 768:1024],
                              win[:, 3584:3600], jnp.zeros((D_MODEL, PROJ_P - IN_PROJ), BF16)], axis=1)
    wts = {n: w2[n] for n in REPLICATED}
    wts.update(w_in_p=w_in_p, ssm_conv_w=full["ssm_conv_w"], ffn_conv_w=full["ffn_conv_w"])

    def join_late(gathered):
        late = {n: join(n, g) for n, g in zip(EARLY_REDUCED, gathered)}
        return dict(w_out_attn=late["w_out"][:ATTN_DIM], w_out_ssm=late["w_out"][ATTN_DIM:], w_up=late["w_up"],
                    w_down=late["w_down"], w_ple_gate=late["w_ple_gate"], w_ple_proj=late["w_ple_proj"])

    core = cc.astype(jnp.int32).reshape(1)
    idx = jnp.stack([chip, 2 * (1 - cx) + cy, 2 * cx + (1 - cy), 2 * (1 - cx) + (1 - cy), cc]).astype(jnp.int32)

    def major_of(names, gd):
        return [gd[n] if gd[n].ndim == 3 else _split_chips(gd[n], axis_of[n]) for n in names]

    def sums_of(names, major, got):
        return [_add_halves("grad_add_halves_" + n, g, r, core) for n, g, r in zip(names, major, got)]

    def w_in_sums(gd):
        gi = gd["w_in_p"]
        gd["w_in"] = jnp.concatenate([gi[:, OFF_Q:OFF_Q + ATTN_DIM], gi[:, OFF_K:OFF_K + KV_DIM], gi[:, OFF_V:OFF_V + KV_DIM],
                                      gi[:, OFF_Z:OFF_Z + SSM_INNER], gi[:, OFF_XBC:OFF_XBC + XBC_DIM], gi[:, OFF_DT:OFF_DT + SSM_HEADS]],
                                     axis=1)
        major = major_of(("w_in",), gd)
        return sums_of(("w_in",), major, _swap_halves("grad_swap_halves_late", major))

    sq, grad_x, grads, early, late = _local_step(
        x[0], p[0, 0], loss_target[0], wts, [shard(n) for n in EARLY_REDUCED], join_late,
        (functools.partial(major_of, EARLY_REDUCED), functools.partial(sums_of, EARLY_REDUCED)), w_in_sums)
    sums = dict(zip(EARLY_REDUCED + ("w_in",), list(zip(*early)) + list(zip(*late))))
    halves = [_sum_chips("grad_sum_chips_" + n, *sums[n], idx) for n in MATRICES]
    g_shard = dict(zip(MATRICES, _share_with_sibling(halves)))

    small_names = REPLICATED + ("ssm_conv_w", "ffn_conv_w")
    small_like = [grads[n] for n in small_names] + [jnp.zeros((1,), F32)]
    small = _allreduce_small(_pack_small([grads[n] for n in small_names] + [jnp.sum(sq).reshape(1)], ALL_SMALL_ROWS))
    small_vals = dict(zip(small_names + ("loss",), _unpack_small(small, small_like)))
    loss = (0.5 / D_MODEL) * small_vals["loss"][0]
    for n in ("ssm_conv_w", "ffn_conv_w"):
        cols = w2[n].shape[1]
        g_shard[n] = lax.dynamic_slice_in_dim(small_vals[n], chip * cols, cols, axis=1)[None]

    delta, new_m, new_v = {}, {}, {}
    for n, _ in SHARDED:
        if n == "w_in":
            r, c = w2[n].shape
            flat = lambda a: jnp.transpose(a.reshape(r, c)).reshape(1, r * c // LANE, LANE)
            back = lambda a: jnp.transpose(a.reshape(c, r)).reshape(1, r, c)
            outs = _adamw("adamw_" + n, flat(given[n]), flat(g_shard[n]), flat(given["m_" + n]), flat(given["v_" + n]))
            delta[n], new_m[n], new_v[n] = [back(o) for o in outs]
            continue
        delta[n], new_m[n], new_v[n] = _adamw("adamw_" + n, given[n], g_shard[n], given["m_" + n], given["v_" + n])
    packed = lambda prefix: _pack_small([given[prefix + n] for n in REPLICATED])[None]
    sm = _adamw("adamw_small", packed(""), _pack_small([small_vals[n] for n in REPLICATED])[None], packed("m_"), packed("v_"))
    for n in REPLICATED:
        g_shard[n] = small_vals[n]
    for dst, packed_out in zip((delta, new_m, new_v), sm):
        for n, val in zip(REPLICATED, _unpack_small(packed_out[0], [w2[n] for n in REPLICATED])):
            dst[n] = val

    def shaped(d):
        return [d[n].reshape(given[n].shape) for n in WEIGHT_ORDER]
    return (loss, grad_x[None], *shaped(g_shard), *shaped(delta), *shaped(new_m), *shaped(new_v))
```

```python
import functools

import jax
import jax.numpy as jnp
from jax import lax
from jax.experimental import pallas as pl
from jax.experimental.pallas import tpu as pltpu

F32 = jnp.float32
BF16 = jnp.bfloat16

D_MODEL = 1024
HEAD_DIM = 64
ATTN_DIM = 512
KV_DIM = 256
N_KV = 4
SSM_INNER = 1024
SSM_HEADS = 16
SSM_STATE = 128
BC_DIM = 256
XBC_DIM = SSM_INNER + 2 * BC_DIM
MIX_DIM = ATTN_DIM + SSM_INNER
IN_PROJ = 3600
D_FF = 2816
PLE_DIM = 256
CHUNK = 128
DILATIONS = (1, 4, 16)
EPS = 1e-6
ADAM_LR, ADAM_B1, ADAM_B2, ADAM_EPS, ADAM_WD, ADAM_STEP = 0.001, 0.9, 0.999, 1e-08, 0.01, 10

PROJ_P = 3712
OFF_XBC, OFF_Q, OFF_Z, OFF_K, OFF_V, OFF_DT = 0, 1536, 2048, 3072, 3328, 3584
LANE = 128
VMEM_LIMIT = 48 * 1024 * 1024
NEG = -1e30


def _cparams(sem):
    return pltpu.CompilerParams(dimension_semantics=sem, vmem_limit_bytes=VMEM_LIMIT)


def _sigmoid(x):
    return 1.0 / (1.0 + jnp.exp(-x))


def _dot(a, b):
    return jnp.dot(a, b, preferred_element_type=F32)


def _dot_nt(a, b):
    return lax.dot_general(a, b, (((1,), (1,)), ((), ())), preferred_element_type=F32)


def _dot_tn(a, b):
    return lax.dot_general(a, b, (((0,), (0,)), ((), ())), preferred_element_type=F32)


def _dot_split(x, m):
    hi = x.astype(BF16)
    lo = (x - hi.astype(F32)).astype(BF16)
    return _dot(hi, m) + _dot(lo, m)


def _rows(name, body, ins, outs, accs=(), tm=512):
    t_rows = next(s[1].shape[0] for s in ins if s[0] in ("t", "tc"))
    tm = min(tm, t_rows)
    in_specs, args = [], []
    for s in ins:
        if s[0] == "t":
            in_specs.append(pl.BlockSpec((tm, s[1].shape[1]), lambda i: (i, 0)))
        elif s[0] == "tc":
            in_specs.append(pl.BlockSpec((tm, s[2]), functools.partial(lambda i, c: (i, c), c=s[3])))
        else:
            in_specs.append(pl.BlockSpec(s[1].shape, lambda i: (0, 0)))
        args.append(s[1])
    out_shape = [jax.ShapeDtypeStruct((t_rows, w), dt) for w, dt in outs]
    out_specs = [pl.BlockSpec((tm, w), lambda i: (i, 0)) for w, _ in outs]
    out_shape += [jax.ShapeDtypeStruct(a, F32) for a in accs]
    out_specs += [pl.BlockSpec(a, lambda i: (0, 0)) for a in accs]
    n_acc = len(accs)

    def kern(*refs):
        if n_acc:
            @pl.when(pl.program_id(0) == 0)
            def _():
                for r in refs[len(refs) - n_acc:]:
                    r[...] = jnp.zeros(r.shape, F32)
        body(*refs)

    return pl.pallas_call(
        kern, name=name, grid=(t_rows // tm,), in_specs=in_specs, out_specs=out_specs, out_shape=out_shape,
        compiler_params=_cparams(("arbitrary",) if n_acc else ("parallel",)))(*args)


NCHUNK = 512


def _col_chunks(n):
    return [(c, min(NCHUNK, n - c)) for c in range(0, n, NCHUNK)]


def _mm_nt(name, pairs, out_dtype, tm=512, tn=None):
    m, n = pairs[0][0].shape[-2], pairs[0][1].shape[0]
    tn = n if tn is None else tn
    tm = min(tm, m)
    np_ = len(pairs)
    in_specs, args = [], []
    for a, w, kb, *lead in pairs:
        if lead:
            in_specs.append(pl.BlockSpec((None, tm, a.shape[2]), functools.partial(lambda j, i, ld: (ld, i, 0), ld=lead[0])))
        else:
            in_specs.append(pl.BlockSpec((tm, a.shape[1]), lambda j, i: (i, 0)))
        in_specs.append(pl.BlockSpec((tn, a.shape[-1]), functools.partial(lambda j, i, kb: (j, kb), kb=kb)))
        args += [a, w]

    def kern(*refs):
        o_ref = refs[-1]
        for c0, cw in _col_chunks(tn):
            acc = None
            for q in range(np_):
                d = _dot_nt(refs[2 * q][...], refs[2 * q + 1][c0:c0 + cw, :])
                acc = d if acc is None else acc + d
            o_ref[:, c0:c0 + cw] = acc.astype(o_ref.dtype)

    return pl.pallas_call(
        kern, name=name, grid=(n // tn, m // tm), in_specs=in_specs,
        out_specs=pl.BlockSpec((tm, tn), lambda j, i: (i, j)),
        out_shape=jax.ShapeDtypeStruct((m, n), out_dtype), compiler_params=_cparams(("parallel", "parallel")))(*args)


def _mm_tn(name, a, b, tm=None, tn=None, tk=1024, chip_cols=False):
    t, m = a.shape
    n = b.shape[-1] * (2 if b.ndim == 3 else 1)
    tm = m if tm is None else tm
    tn = n if tn is None else tn
    tk = min(tk, t)
    if b.ndim == 3:
        per = n // 2 // tn
        b_spec = pl.BlockSpec((None, tk, tn), lambda i, j, k: (j // per, k, j % per))
    else:
        b_spec = pl.BlockSpec((tk, tn), lambda i, j, k: (k, j))
    if chip_cols:
        out_spec = pl.BlockSpec((None, tm, tn), lambda i, j, k: (j, i, 0))
        out_shape = jax.ShapeDtypeStruct((n // tn, m, tn), F32)
    else:
        out_spec = pl.BlockSpec((tm, tn), lambda i, j, k: (i, j))
        out_shape = jax.ShapeDtypeStruct((m, n), F32)

    def kern(a_ref, b_ref, o_ref):
        @pl.when(pl.program_id(2) == 0)
        def _():
            o_ref[...] = jnp.zeros(o_ref.shape, F32)
        for c0, cw in _col_chunks(tn):
            o_ref[:, c0:c0 + cw] += _dot_tn(a_ref[...], b_ref[:, c0:c0 + cw])

    return pl.pallas_call(
        kern, name=name, grid=(m // tm, n // tn, t // tk),
        in_specs=[pl.BlockSpec((tk, tm), lambda i, j, k: (k, i)), b_spec], out_specs=out_spec, out_shape=out_shape,
        compiler_params=_cparams(("parallel", "parallel", "arbitrary")))(a, b)


def _rms_bwd(name, dh, x, g, dres):
    d = x.shape[1]

    def body(dh_ref, x_ref, g_ref, dres_ref, dx_ref, dxb_ref, dg_ref):
        xv, dhv = x_ref[...], dh_ref[...]
        r = lax.rsqrt(jnp.mean(xv * xv, axis=-1, keepdims=True) + EPS)
        gd = dhv * g_ref[...]
        dx = dres_ref[...] + r * gd - xv * (r * r * r * jnp.mean(xv * gd, axis=-1, keepdims=True))
        dx_ref[...] = dx
        dxb_ref[...] = dx.astype(BF16)
        dg_ref[...] += jnp.sum(dhv * xv * r, axis=0, keepdims=True)
    return _rows(name, body, [("t", dh), ("t", x), ("p", g), ("t", dres)], [(d, F32), (d, BF16)], accs=[(1, d)])


def _norm_mm(name, x, g, w, tm=512, tn=None, halves=False):
    m, k = x.shape
    n = w.shape[1]
    tn = n if tn is None else tn
    if halves:
        per = n // 2 // tn
        o_spec = pl.BlockSpec((None, tm, tn), lambda i, j: (j // per, i, j % per))
        o_shape = jax.ShapeDtypeStruct((2, m, n // 2), F32)
    else:
        o_spec = pl.BlockSpec((tm, tn), lambda i, j: (i, j))
        o_shape = jax.ShapeDtypeStruct((m, n), F32)

    def kern(x_ref, g_ref, w_ref, h_ref, o_ref):
        xv = x_ref[...]
        h = (xv * lax.rsqrt(jnp.mean(xv * xv, axis=-1, keepdims=True) + EPS) * g_ref[...]).astype(BF16)
        h_ref[...] = h
        for c0, cw in _col_chunks(tn):
            o_ref[:, c0:c0 + cw] = _dot(h, w_ref[:, c0:c0 + cw])

    return pl.pallas_call(
        kern, name=name, grid=(m // tm, n // tn),
        in_specs=[pl.BlockSpec((tm, k), lambda i, j: (i, 0)), pl.BlockSpec((1, k), lambda i, j: (0, 0)),
                  pl.BlockSpec((k, tn), lambda i, j: (0, j))],
        out_specs=[pl.BlockSpec((tm, k), lambda i, j: (i, 0)), o_spec],
        out_shape=[jax.ShapeDtypeStruct((m, k), BF16), o_shape],
        compiler_params=_cparams(("parallel", "arbitrary")))(x, g, w)


def _ple_head(x2, g, w_gate, pb, w_proj, tgt, tm=512):
    m, d = x2.shape

    def kern(x_ref, g_ref, wg_ref, p_ref, wp_ref, t_ref, h_ref, dy_ref, dgl_ref, dpp_ref, sq_ref):
        @pl.when(pl.program_id(0) == 0)
        def _():
            sq_ref[...] = jnp.zeros(sq_ref.shape, F32)
        xv = x_ref[...]
        h = (xv * lax.rsqrt(jnp.mean(xv * xv, axis=-1, keepdims=True) + EPS) * g_ref[...]).astype(BF16)
        h_ref[...] = h
        pv = p_ref[...]
        for c0, cw in _col_chunks(d):
            cs = slice(c0, c0 + cw)
            s = _sigmoid(_dot(h, wg_ref[:, cs]))
            ppv = _dot(pv, wp_ref[:, cs])
            diff = x_ref[:, cs] + s * ppv - t_ref[:, cs]
            dy = diff * (1.0 / d)
            dy_ref[:, cs] = dy
            dgl_ref[:, cs] = (dy * ppv * s * (1.0 - s)).astype(BF16)
            dpp_ref[:, cs] = (dy * s).astype(BF16)
            sq_ref[:, cs] += jnp.sum(diff * diff, axis=0, keepdims=True)

    row = lambda width: pl.BlockSpec((tm, width), lambda i: (i, 0))
    full = lambda a: pl.BlockSpec(a.shape, lambda i: (0, 0))
    return pl.pallas_call(
        kern, name="ple_head", grid=(m // tm,),
        in_specs=[row(d), full(g), full(w_gate), row(pb.shape[1]), full(w_proj), row(d)],
        out_specs=[row(d), row(d), row(d), row(d), pl.BlockSpec((1, d), lambda i: (0, 0))],
        out_shape=[jax.ShapeDtypeStruct((m, d), BF16), jax.ShapeDtypeStruct((m, d), F32), jax.ShapeDtypeStruct((m, d), BF16),
                   jax.ShapeDtypeStruct((m, d), BF16), jax.ShapeDtypeStruct((1, d), F32)],
        compiler_params=_cparams(("arbitrary",)))(x2, g, w_gate, pb, w_proj, tgt)


def _mix_out_proj(y, proj, g, w_ssm, os_, lses, w_attn, x, tm=256):
    m, d = y.shape

    def kern(y_ref, z_ref, g_ref, ws_ref, o1, o2, o3, l1, l2, l3, wa_ref, x_ref, s_ref, a_ref, lse_ref, o_ref):
        z = z_ref[...]
        yz = y_ref[...] * (z * _sigmoid(z))
        s = (yz * lax.rsqrt(jnp.mean(yz * yz, axis=-1, keepdims=True) + EPS) * g_ref[...]).astype(BF16)
        s_ref[...] = s
        pieces = []
        for kh in range(N_KV):
            a, b, c = l1[kh], l2[kh], l3[kh]
            mx = jnp.maximum(jnp.maximum(a, b), c)
            tot = mx + jnp.log(jnp.exp(a - mx) + jnp.exp(b - mx) + jnp.exp(c - mx))
            lse_ref[kh] = tot
            wa, wb, wc = jnp.exp(a - tot), jnp.exp(b - tot), jnp.exp(c - tot)
            for g_ in range(2):
                h = 2 * kh + g_
                acc = wa[:, g_:g_ + 1] * o1[h] + wb[:, g_:g_ + 1] * o2[h] + wc[:, g_:g_ + 1] * o3[h]
                pieces.append(acc[:, HEAD_DIM:])
        av = jnp.concatenate(pieces, axis=1).astype(BF16)
        a_ref[...] = av
        for c0, cw in _col_chunks(d):
            cs = slice(c0, c0 + cw)
            o_ref[:, cs] = x_ref[:, cs] + _dot(s, ws_ref[:, cs]) + _dot(av, wa_ref[:, cs])

    row = lambda width: pl.BlockSpec((tm, width), lambda i: (i, 0))
    full = lambda a: pl.BlockSpec(a.shape, lambda i: (0, 0))
    blk = lambda heads: pl.BlockSpec((heads, tm, LANE), lambda i: (0, i, 0))
    return pl.pallas_call(
        kern, name="out_proj", grid=(m // tm,),
        in_specs=[row(d), pl.BlockSpec((tm, d), lambda i: (i, OFF_Z // SSM_INNER)), full(g), full(w_ssm)] + [blk(N_QH)] * 3
        + [blk(N_KV)] * 3 + [full(w_attn), row(d)],
        out_specs=[row(d), row(ATTN_DIM), blk(N_KV), row(d)],
        out_shape=[jax.ShapeDtypeStruct((m, d), BF16), jax.ShapeDtypeStruct((m, ATTN_DIM), BF16),
                   jax.ShapeDtypeStruct((N_KV, m, LANE), F32), jax.ShapeDtypeStruct((m, d), F32)],
        compiler_params=_cparams(("parallel",)))(y, proj, g, w_ssm, *os_, *lses, w_attn, x)


def _mm_nt_rms_bwd(name, a, w, x, g, dres, parts=(), tm=512):
    m, k = a.shape
    n = w.shape[0]
    ns, steps = len(parts), m // tm

    def kern(a_ref, w_ref, x_ref, g_ref, dres_ref, *rest):
        dx_ref, dxb_ref, dg_ref = rest[ns:ns + 3]
        dh = rest[2 * ns + 3]
        if ns:
            start, finish = _scatter_steps(ns, rest[:ns], rest[ns + 3:2 * ns + 3], rest[2 * ns + 4:])
            pl.when(pl.program_id(0) == 0)(start)
            pl.when(pl.program_id(0) == steps - 1)(finish)

        @pl.when(pl.program_id(0) == 0)
        def _():
            dg_ref[...] = jnp.zeros(dg_ref.shape, F32)
        av = a_ref[...]
        for c0, cw in _col_chunks(n):
            dh[:, c0:c0 + cw] = _dot_nt(av, w_ref[c0:c0 + cw, :])
        xv, dhv = x_ref[...], dh[...]
        r = lax.rsqrt(jnp.mean(xv * xv, axis=-1, keepdims=True) + EPS)
        gd = dhv * g_ref[...]
        dx = dres_ref[...] + r * gd - xv * (r * r * r * jnp.mean(xv * gd, axis=-1, keepdims=True))
        dx_ref[...] = dx
        dxb_ref[...] = dx.astype(BF16)
        dg_ref[...] += jnp.sum(dhv * xv * r, axis=0, keepdims=True)

    row = lambda width: pl.BlockSpec((tm, width), lambda i: (i, 0))
    return pl.pallas_call(
        kern, name=name, grid=(steps,),
        in_specs=[row(k), pl.BlockSpec((n, k), lambda i: (0, 0)), row(n), pl.BlockSpec((1, n), lambda i: (0, 0)), row(n)]
        + [ANY_SPEC] * ns,
        out_specs=[row(n), row(n), pl.BlockSpec((1, n), lambda i: (0, 0))] + [ANY_SPEC] * ns,
        out_shape=[jax.ShapeDtypeStruct((m, n), F32), jax.ShapeDtypeStruct((m, n), BF16), jax.ShapeDtypeStruct((1, n), F32)]
        + [jax.ShapeDtypeStruct(s.shape, s.dtype) for s in parts],
        scratch_shapes=[pltpu.VMEM((tm, n), F32)] + (_scatter_sems(ns) if ns else []),
        compiler_params=_cparams(("arbitrary",)))(a, w, x, g, dres, *parts)


def _head_mean_matrix(width):
    i = jnp.arange(width) // HEAD_DIM
    return jnp.where(i[:, None] == i[None, :], 1.0 / HEAD_DIM, 0.0).astype(BF16)


ATT_SPAN = 2048
N_QH = 8


def _lane_lo(rows):
    return lax.broadcasted_iota(jnp.int32, (rows, LANE), 1) < HEAD_DIM


def _swap_halves_lanes(x):
    return pltpu.roll(x, HEAD_DIM, axis=1)


def _head_major_qkv(qn, kn, v, qo_ref, kvo_ref):
    lo = _lane_lo(qn.shape[0])
    for j in range(N_KV):
        blk = qn[:, j * LANE:(j + 1) * LANE]
        qo_ref[2 * j] = jnp.where(lo, blk, 0.0)
        qo_ref[2 * j + 1] = jnp.where(lo, _swap_halves_lanes(blk), 0.0)
    for j in range(2):
        kb, vb = kn[:, j * LANE:(j + 1) * LANE], v[:, j * LANE:(j + 1) * LANE]
        kvo_ref[2 * j] = jnp.where(lo, kb, _swap_halves_lanes(vb))
        kvo_ref[2 * j + 1] = jnp.where(lo, _swap_halves_lanes(kb), vb)


def _in_proj(x, g, w, gq_t, gk_t, tm=512):
    m, k = x.shape
    n = w.shape[1]
    bq, bk = _head_mean_matrix(ATTN_DIM), _head_mean_matrix(KV_DIM)
    scale = HEAD_DIM ** -0.5

    def kern(x_ref, g_ref, w_ref, gq_ref, gk_ref, bq_ref, bk_ref, h_ref, o_ref, qo_ref, kvo_ref):
        xv = x_ref[...]
        h = (xv * lax.rsqrt(jnp.mean(xv * xv, axis=-1, keepdims=True) + EPS) * g_ref[...]).astype(BF16)
        h_ref[...] = h
        for c0, cw in _col_chunks(n):
            o_ref[:, c0:c0 + cw] = _dot(h, w_ref[:, c0:c0 + cw])
        q, kk, v = o_ref[:, OFF_Q:OFF_Q + ATTN_DIM], o_ref[:, OFF_K:OFF_K + KV_DIM], o_ref[:, OFF_V:OFF_V + KV_DIM]
        qn = (q * lax.rsqrt(_dot_split(q * q, bq_ref[...]) + EPS) * gq_ref[...]) * scale
        kn = kk * lax.rsqrt(_dot_split(kk * kk, bk_ref[...]) + EPS) * gk_ref[...]
        _head_major_qkv(qn, kn, v, qo_ref, kvo_ref)

    row = lambda width: pl.BlockSpec((tm, width), lambda i: (i, 0))
    full = lambda a: pl.BlockSpec(a.shape, lambda i: (0, 0))
    blk = lambda heads: pl.BlockSpec((heads, tm, LANE), lambda i: (0, i, 0))
    return pl.pallas_call(
        kern, name="in_proj", grid=(m // tm,),
        in_specs=[row(k), full(g), full(w), full(gq_t), full(gk_t), full(bq), full(bk)],
        out_specs=[row(k), row(n), blk(N_QH), blk(N_KV)],
        out_shape=[jax.ShapeDtypeStruct((m, k), BF16), jax.ShapeDtypeStruct((m, n), F32),
                   jax.ShapeDtypeStruct((N_QH, m, LANE), F32), jax.ShapeDtypeStruct((N_KV, m, LANE), F32)],
        compiler_params=_cparams(("parallel",)))(x, g, w, gq_t, gk_t, bq, bk)


def _d_mix(dx1b, w_cat, y, proj, g, attn_out, tm=512):
    m, k = dx1b.shape
    n = w_cat.shape[0]

    def kern(a_ref, w_ref, y_ref, z_ref, g_ref, o_ref, dy_ref, dz_ref, dg_ref, dot_ref, d_ref, dmix):
        @pl.when(pl.program_id(0) == 0)
        def _():
            dg_ref[...] = jnp.zeros(dg_ref.shape, F32)
        av = a_ref[...]
        for c0, cw in _col_chunks(n):
            dmix[:, c0:c0 + cw] = _dot_nt(av, w_ref[c0:c0 + cw, :])
        z, yv, dout = z_ref[...], y_ref[...], dmix[:, :SSM_INNER]
        sg = _sigmoid(z)
        gz = z * sg
        yz = yv * gz
        r = lax.rsqrt(jnp.mean(yz * yz, axis=-1, keepdims=True) + EPS)
        gd = dout * g_ref[...]
        dyz = r * gd - yz * (r * r * r * jnp.mean(yz * gd, axis=-1, keepdims=True))
        dy_ref[...] = dyz * gz
        dz_ref[...] = (dyz * yv * (sg * (1.0 + z * (1.0 - sg)))).astype(BF16)
        dg_ref[...] += jnp.sum(dout * yz * r, axis=0, keepdims=True)
        do = dmix[:, SSM_INNER:]
        prod = do * o_ref[...].astype(F32)
        lo = _lane_lo(tm)
        lane = lax.broadcasted_iota(jnp.int32, (tm, LANE), 1)
        for kh in range(N_KV):
            blk, pb = do[:, kh * LANE:(kh + 1) * LANE], prod[:, kh * LANE:(kh + 1) * LANE]
            dot_ref[2 * kh] = jnp.where(lo, 0.0, _swap_halves_lanes(blk))
            dot_ref[2 * kh + 1] = jnp.where(lo, 0.0, blk)
            s_lo = jnp.sum(jnp.where(lo, pb, 0.0), axis=1, keepdims=True)
            s_hi = jnp.sum(pb, axis=1, keepdims=True) - s_lo
            d_ref[kh] = jnp.where(lane == 0, s_lo, jnp.where(lane == 1, s_hi, 0.0))

    row = lambda width: pl.BlockSpec((tm, width), lambda i: (i, 0))
    full = lambda a: pl.BlockSpec(a.shape, lambda i: (0, 0))
    blk = lambda heads: pl.BlockSpec((heads, tm, LANE), lambda i: (0, i, 0))
    return pl.pallas_call(
        kern, name="d_mix", grid=(m // tm,),
        in_specs=[row(k), full(w_cat), row(SSM_INNER), pl.BlockSpec((tm, SSM_INNER), lambda i: (i, OFF_Z // SSM_INNER)), full(g),
                  row(ATTN_DIM)],
        out_specs=[row(SSM_INNER), row(SSM_INNER), pl.BlockSpec((1, SSM_INNER), lambda i: (0, 0)), blk(N_QH), blk(N_KV)],
        out_shape=[jax.ShapeDtypeStruct((m, SSM_INNER), F32), jax.ShapeDtypeStruct((m, SSM_INNER), BF16),
                   jax.ShapeDtypeStruct((1, SSM_INNER), F32), jax.ShapeDtypeStruct((N_QH, m, LANE), F32),
                   jax.ShapeDtypeStruct((N_KV, m, LANE), F32)],
        scratch_shapes=[pltpu.VMEM((tm, n), F32)], compiler_params=_cparams(("arbitrary",)))(dx1b, w_cat, y, proj, g, attn_out)


def _qknorm_bwd2(proj, gq_t, gk_t, dqs, dkvs, tm=256):
    t = proj.shape[0]
    bq, bk = _head_mean_matrix(ATTN_DIM), _head_mean_matrix(KV_DIM)
    scale = HEAD_DIM ** -0.5

    def kern(q_ref, k_ref, gq_ref, gk_ref, bq_ref, bk_ref, a1, a2, a3, b1, b2, b3, dq_ref, dk_ref, dv_ref, dgq_ref, dgk_ref):
        @pl.when(pl.program_id(0) == 0)
        def _():
            dgq_ref[...] = jnp.zeros(dgq_ref.shape, F32)
            dgk_ref[...] = jnp.zeros(dgk_ref.shape, F32)
        lo = _lane_lo(tm)
        sq = [a1[h] + a2[h] + a3[h] for h in range(N_QH)]
        skv = [b1[h] + b2[h] + b3[h] for h in range(N_KV)]
        dqn = jnp.concatenate([jnp.where(lo, sq[2 * j], _swap_halves_lanes(sq[2 * j + 1])) for j in range(N_KV)], axis=1) * scale
        dkn = jnp.concatenate([jnp.where(lo, skv[2 * j], _swap_halves_lanes(skv[2 * j + 1])) for j in range(2)], axis=1)
        dv = jnp.concatenate([jnp.where(lo, _swap_halves_lanes(skv[2 * j]), skv[2 * j + 1]) for j in range(2)], axis=1)
        q, k = q_ref[...], k_ref[...]
        rq = lax.rsqrt(_dot_split(q * q, bq_ref[...]) + EPS)
        rk = lax.rsqrt(_dot_split(k * k, bk_ref[...]) + EPS)
        gdq, gdk = dqn * gq_ref[...], dkn * gk_ref[...]
        dq_ref[...] = (rq * gdq - q * (rq * rq * rq * _dot_split(q * gdq, bq_ref[...]))).astype(BF16)
        dk_ref[...] = (rk * gdk - k * (rk * rk * rk * _dot_split(k * gdk, bk_ref[...]))).astype(BF16)
        dv_ref[...] = dv.astype(BF16)
        dgq_ref[...] += jnp.sum(dqn * q * rq, axis=0, keepdims=True)
        dgk_ref[...] += jnp.sum(dkn * k * rk, axis=0, keepdims=True)

    col = lambda w, idx: pl.BlockSpec((tm, w), functools.partial(lambda i, idx: (i, idx), idx=idx))
    par = lambda a: pl.BlockSpec(a.shape, lambda i: (0, 0))
    blk = lambda n: pl.BlockSpec((n, tm, LANE), lambda i: (0, i, 0))
    row = lambda w: pl.BlockSpec((tm, w), lambda i: (i, 0))
    acc = lambda w: pl.BlockSpec((1, w), lambda i: (0, 0))
    return pl.pallas_call(
        kern, name="qknorm_bwd", grid=(t // tm,),
        in_specs=[col(ATTN_DIM, OFF_Q // ATTN_DIM), col(KV_DIM, OFF_K // KV_DIM), par(gq_t), par(gk_t), par(bq), par(bk)]
        + [blk(N_QH)] * 3 + [blk(N_KV)] * 3,
        out_specs=[row(ATTN_DIM), row(KV_DIM), row(KV_DIM), acc(ATTN_DIM), acc(KV_DIM)],
        out_shape=[jax.ShapeDtypeStruct((t, ATTN_DIM), BF16), jax.ShapeDtypeStruct((t, KV_DIM), BF16),
                   jax.ShapeDtypeStruct((t, KV_DIM), BF16), jax.ShapeDtypeStruct((1, ATTN_DIM), F32),
                   jax.ShapeDtypeStruct((1, KV_DIM), F32)],
        compiler_params=_cparams(("arbitrary",)))(proj, proj, gq_t, gk_t, bq, bk, *dqs, *dkvs)


def _att_rows(b, r, dil):
    if dil == 1:
        return pl.ds(b * CHUNK, CHUNK)
    return pl.ds(b * CHUNK * dil + r, CHUNK, stride=dil)


def _for_residues(dil, unit):
    for r in range(dil):
        unit(r, 0)


def _band_qk(first):
    ri = lax.broadcasted_iota(jnp.int32, (CHUNK, 2 * CHUNK), 0)
    cj = lax.broadcasted_iota(jnp.int32, (CHUNK, 2 * CHUNK), 1)
    band = (cj - ri >= 0) & (cj - ri <= CHUNK)
    return band if first is None else band & (jnp.logical_not(first) | (cj >= CHUNK))


def _band_kq(last):
    rj = lax.broadcasted_iota(jnp.int32, (CHUNK, 2 * CHUNK), 0)
    ci = lax.broadcasted_iota(jnp.int32, (CHUNK, 2 * CHUNK), 1)
    band = (ci - rj >= 0) & (ci - rj <= CHUNK)
    return band if last is None else band & (jnp.logical_not(last) | (ci < CHUNK))


def _att_specs(t, dil):
    sub = CHUNK * dil
    nb, last = ATT_SPAN // sub, t // sub - 1
    cur = lambda heads: pl.BlockSpec((heads, ATT_SPAN, LANE), lambda kh, n: (kh, n, 0))
    prev = lambda heads: pl.BlockSpec((heads, sub, LANE), lambda kh, n: (kh, jnp.maximum(n * nb - 1, 0), 0))
    nxt = lambda heads: pl.BlockSpec((heads, sub, LANE), lambda kh, n: (kh, jnp.minimum((n + 1) * nb, last), 0))
    return sub, nb, cur, prev, nxt


def _attn_fwd2(q, kv, dil):
    t = q.shape[1]
    sub, nb, cur, prev, _ = _att_specs(t, dil)

    def kern(q_ref, kvp_ref, kvc_ref, o_ref, lse_ref):
        n = pl.program_id(1)
        lane = lax.broadcasted_iota(jnp.int32, (CHUNK, LANE), 1)
        for b in range(nb):
            band = _band_qk((n == 0) if b == 0 else None)
            mask = jnp.concatenate([band, band], axis=0)

            def unit(r, carry, b=b, mask=mask):
                rows = _att_rows(b, r, dil)
                kvp = kvc_ref[_att_rows(b - 1, r, dil), :] if b > 0 else kvp_ref[_att_rows(0, r, dil), :]
                kvcat = jnp.concatenate([kvp, kvc_ref[rows, :]], axis=0).astype(BF16)
                qs = jnp.concatenate([q_ref.at[0][rows, :], q_ref.at[1][rows, :]], axis=0).astype(BF16)
                s = jnp.where(mask, _dot_nt(qs, kvcat), NEG)
                m = jnp.max(s, axis=1, keepdims=True)
                p = jnp.exp(s - m)
                l = jnp.sum(p, axis=1, keepdims=True)
                o = _dot(p.astype(BF16), kvcat) * (1.0 / l)
                o_ref.at[0][rows, :] = o[:CHUNK]
                o_ref.at[1][rows, :] = o[CHUNK:]
                lse = m + jnp.log(l)
                lse_ref[rows, :] = jnp.where(lane == 0, lse[:CHUNK], jnp.where(lane == 1, lse[CHUNK:], 0.0))
                return carry
            _for_residues(dil, unit)

    return pl.pallas_call(
        kern, name=f"attn_fwd_d{dil}", grid=(N_KV, t // ATT_SPAN), in_specs=[cur(2), prev(None), cur(None)],
        out_specs=[cur(2), cur(None)],
        out_shape=[jax.ShapeDtypeStruct((N_QH, t, LANE), F32), jax.ShapeDtypeStruct((N_KV, t, LANE), F32)],
        compiler_params=_cparams(("parallel", "parallel")))(q, kv, kv)


def _attn_dq2(q, kv, dot, lse, dsum, dil):
    t = q.shape[1]
    sub, nb, cur, prev, _ = _att_specs(t, dil)

    def kern(q_ref, kvp_ref, kvc_ref, do_ref, lse_ref, d_ref, dq_ref):
        n = pl.program_id(1)
        for b in range(nb):
            band = _band_qk((n == 0) if b == 0 else None)
            mask = jnp.concatenate([band, band], axis=0)

            def unit(r, carry, b=b, mask=mask):
                rows = _att_rows(b, r, dil)
                kvp = kvc_ref[_att_rows(b - 1, r, dil), :] if b > 0 else kvp_ref[_att_rows(0, r, dil), :]
                kvcat = jnp.concatenate([kvp, kvc_ref[rows, :]], axis=0).astype(BF16)
                lse_t, d_t = lse_ref[rows, :], d_ref[rows, :]
                qs = jnp.concatenate([q_ref.at[0][rows, :], q_ref.at[1][rows, :]], axis=0).astype(BF16)
                dos = jnp.concatenate([do_ref.at[0][rows, :], do_ref.at[1][rows, :]], axis=0).astype(BF16)
                lse2 = jnp.concatenate([lse_t[:, 0:1], lse_t[:, 1:2]], axis=0)
                d2 = jnp.concatenate([d_t[:, 0:1], d_t[:, 1:2]], axis=0)
                p = jnp.exp(jnp.where(mask, _dot_nt(qs, kvcat), NEG) - lse2)
                ds = p * (_dot_nt(dos, kvcat) - d2)
                dq = _dot(ds.astype(BF16), kvcat)
                dq_ref.at[0][rows, :] = dq[:CHUNK]
                dq_ref.at[1][rows, :] = dq[CHUNK:]
                return carry
            _for_residues(dil, unit)

    return pl.pallas_call(
        kern, name=f"attn_dq_d{dil}", grid=(N_KV, t // ATT_SPAN),
        in_specs=[cur(2), prev(None), cur(None), cur(2), cur(None), cur(None)], out_specs=cur(2),
        out_shape=jax.ShapeDtypeStruct((N_QH, t, LANE), F32),
        compiler_params=_cparams(("parallel", "parallel")))(q, kv, kv, dot, lse, dsum)


def _attn_dkv2(q, kv, dot, lse, dsum, dil):
    t = q.shape[1]
    sub, nb, cur, _, nxt = _att_specs(t, dil)
    nsteps = t // ATT_SPAN

    def kern(kv_ref, qc_ref, qn_ref, doc_ref, don_ref, lc_ref, ln_ref, dc_ref, dn_ref, dkv_ref):
        n = pl.program_id(1)
        for b in range(nb):
            inside = b < nb - 1
            mask = _band_kq(None if inside else (n == nsteps - 1))

            def unit(r, carry, b=b, inside=inside, mask=mask):
                rows = _att_rows(b, r, dil)
                nrows = _att_rows(b + 1, r, dil) if inside else _att_rows(0, r, dil)
                kvb = kv_ref[rows, :].astype(BF16)
                follow = lambda cref, nref: (cref if inside else nref)[nrows, :]
                lse_t = jnp.concatenate([lc_ref[rows, :].T, follow(lc_ref, ln_ref).T], axis=1)
                d_t = jnp.concatenate([dc_ref[rows, :].T, follow(dc_ref, dn_ref).T], axis=1)
                qdo = jnp.concatenate([qc_ref.at[0][rows, :], follow(qc_ref.at[0], qn_ref.at[0]),
                                       qc_ref.at[1][rows, :], follow(qc_ref.at[1], qn_ref.at[1]),
                                       doc_ref.at[0][rows, :], follow(doc_ref.at[0], don_ref.at[0]),
                                       doc_ref.at[1][rows, :], follow(doc_ref.at[1], don_ref.at[1])], axis=0).astype(BF16)
                both = _dot_nt(kvb, qdo)
                half = 4 * CHUNK
                mask2 = jnp.concatenate([mask, mask], axis=1)
                lse2 = jnp.concatenate([lse_t[0:1, :], lse_t[1:2, :]], axis=1)
                d2 = jnp.concatenate([d_t[0:1, :], d_t[1:2, :]], axis=1)
                pt = jnp.exp(jnp.where(mask2, both[:, :half], NEG) - lse2)
                dst = pt * (both[:, half:] - d2)
                dkv_ref[rows, :] = _dot(jnp.concatenate([dst, pt], axis=1).astype(BF16), qdo)
                return carry
            _for_residues(dil, unit)

    return pl.pallas_call(
        kern, name=f"attn_dkv_d{dil}", grid=(N_KV, nsteps),
        in_specs=[cur(None), cur(2), nxt(2), cur(2), nxt(2), cur(None), nxt(None), cur(None), nxt(None)], out_specs=cur(None),
        out_shape=jax.ShapeDtypeStruct((N_KV, t, LANE), F32),
        compiler_params=_cparams(("parallel", "parallel")))(kv, q, q, dot, dot, lse, lse, dsum, dsum)


def _attn_fwd_all(q, kv):
    t = q.shape[1]
    specs = [_att_specs(t, d) for d in DILATIONS]
    cur = specs[0][2]
    nd = len(DILATIONS)

    def kern(q_ref, kvc_ref, *rest):
        n = pl.program_id(1)
        lane = lax.broadcasted_iota(jnp.int32, (CHUNK, LANE), 1)
        for di, dil in enumerate(DILATIONS):
            kvp_ref, o_ref, lse_ref = rest[di], rest[nd + 2 * di], rest[nd + 2 * di + 1]
            for b in range(specs[di][1]):
                band = _band_qk((n == 0) if b == 0 else None)
                mask = jnp.concatenate([band, band], axis=0)
                for r in range(dil):
                    rows = _att_rows(b, r, dil)
                    kvp = kvc_ref[_att_rows(b - 1, r, dil), :] if b > 0 else kvp_ref[_att_rows(0, r, dil), :]
                    kvcat = jnp.concatenate([kvp, kvc_ref[rows, :]], axis=0).astype(BF16)
                    qs = jnp.concatenate([q_ref.at[0][rows, :], q_ref.at[1][rows, :]], axis=0).astype(BF16)
                    s = jnp.where(mask, _dot_nt(qs, kvcat), NEG)
                    m = jnp.max(s, axis=1, keepdims=True)
                    p = jnp.exp(s - m)
                    l = jnp.sum(p, axis=1, keepdims=True)
                    o = _dot(p.astype(BF16), kvcat) * (1.0 / l)
                    o_ref.at[0][rows, :] = o[:CHUNK]
                    o_ref.at[1][rows, :] = o[CHUNK:]
                    lse = m + jnp.log(l)
                    lse_ref[rows, :] = jnp.where(lane == 0, lse[:CHUNK], jnp.where(lane == 1, lse[CHUNK:], 0.0))

    outs = pl.pallas_call(
        kern, name="attn_fwd", grid=(N_KV, t // ATT_SPAN), in_specs=[cur(2), cur(None)] + [sp[3](None) for sp in specs],
        out_specs=[cur(2), cur(None)] * nd,
        out_shape=[jax.ShapeDtypeStruct((N_QH, t, LANE), F32), jax.ShapeDtypeStruct((N_KV, t, LANE), F32)] * nd,
        compiler_params=_cparams(("parallel", "parallel")))(q, kv, *([kv] * nd))
    return [(outs[2 * i], outs[2 * i + 1]) for i in range(nd)]


def _attn_dq_all(q, kv, dot, lse, dsum):
    t = q.shape[1]
    specs = [_att_specs(t, d) for d in DILATIONS]
    cur = specs[0][2]
    nd = len(DILATIONS)

    def kern(q_ref, kvc_ref, do_ref, lse_ref, d_ref, *rest):
        n = pl.program_id(1)
        for di, dil in enumerate(DILATIONS):
            kvp_ref, dq_ref = rest[di], rest[nd + di]
            for b in range(specs[di][1]):
                band = _band_qk((n == 0) if b == 0 else None)
                mask = jnp.concatenate([band, band], axis=0)
                for r in range(dil):
                    rows = _att_rows(b, r, dil)
                    kvp = kvc_ref[_att_rows(b - 1, r, dil), :] if b > 0 else kvp_ref[_att_rows(0, r, dil), :]
                    kvcat = jnp.concatenate([kvp, kvc_ref[rows, :]], axis=0).astype(BF16)
                    lse_t, d_t = lse_ref[rows, :], d_ref[rows, :]
                    qs = jnp.concatenate([q_ref.at[0][rows, :], q_ref.at[1][rows, :]], axis=0).astype(BF16)
                    dos = jnp.concatenate([do_ref.at[0][rows, :], do_ref.at[1][rows, :]], axis=0).astype(BF16)
                    lse2 = jnp.concatenate([lse_t[:, 0:1], lse_t[:, 1:2]], axis=0)
                    d2 = jnp.concatenate([d_t[:, 0:1], d_t[:, 1:2]], axis=0)
                    p = jnp.exp(jnp.where(mask, _dot_nt(qs, kvcat), NEG) - lse2)
                    ds = p * (_dot_nt(dos, kvcat) - d2)
                    dq = _dot(ds.astype(BF16), kvcat)
                    dq_ref.at[0][rows, :] = dq[:CHUNK]
                    dq_ref.at[1][rows, :] = dq[CHUNK:]

    return pl.pallas_call(
        kern, name="attn_dq", grid=(N_KV, t // ATT_SPAN),
        in_specs=[cur(2), cur(None), cur(2), cur(None), cur(None)] + [sp[3](None) for sp in specs], out_specs=[cur(2)] * nd,
        out_shape=[jax.ShapeDtypeStruct((N_QH, t, LANE), F32)] * nd,
        compiler_params=_cparams(("parallel", "parallel")))(q, kv, dot, lse, dsum, *([kv] * nd))


HALO = 8
SSM_CONV_TM, SSM_CONV_W = 512, 512
FFN_CONV_TM, FFN_CONV_W = 256, 1408


def _halo_specs(tm, width, t_rows, col_off=0, lead=None):
    per, last = tm // HALO, t_rows // HALO - 1
    row_maps = (lambda i: i, lambda i: jnp.maximum(i * per - 1, 0), lambda i: jnp.minimum((i + 1) * per, last))
    specs = []
    for rows, rm in zip((tm, HALO, HALO), row_maps):
        if lead is None:
            specs.append(pl.BlockSpec((rows, width), functools.partial(lambda c, i, rm: (rm(i), c + col_off), rm=rm)))
        else:
            specs.append(pl.BlockSpec((None, rows, width), functools.partial(lambda c, i, rm: (lead, rm(i), c + col_off), rm=rm)))
    return specs


def _fill_ext(buf, tile_ref, before_ref, after_ref, i, nt):
    tm = tile_ref.shape[0]
    buf[0:HALO, :] = jnp.where(i > 0, before_ref[...].astype(F32), 0.0)
    buf[HALO:HALO + tm, :] = tile_ref[...].astype(F32)
    if after_ref is not None:
        buf[HALO + tm:, :] = jnp.where(i < nt - 1, after_ref[...].astype(F32), 0.0)


CONV_RB, CONV_CW = 16, 256


def _lane_chunks(width):
    return [slice(c0, min(c0 + CONV_CW, width)) for c0 in range(0, width, CONV_CW)]


def _shifted(buf, taps, r0, rows, cs):
    return [buf[pl.ds(HALO - (taps - 1) + k + r0, rows), cs] for k in range(taps)]


def _taps_fwd(xs, w, b):
    acc = b
    for k, xk in enumerate(xs):
        acc = acc + w[k:k + 1, :] * xk
    return acc


def _taps_bwd(bufd, w, taps, r0, rows, cs):
    acc = None
    for k in range(taps):
        term = w[k:k + 1, :] * bufd[pl.ds(r0 + (taps - 1) - k, rows), cs]
        acc = term if acc is None else acc + term
    return acc


def _fold8(z):
    return z[:HALO] + z[HALO:] if z.shape[0] == 2 * HALO else z


def _silu_grad(pre):
    sg = _sigmoid(pre)
    return sg * (1.0 + pre * (1.0 - sg))


def _ssm_conv_fwd(proj, w, b):
    t = proj.shape[0]
    tm, wd = min(SSM_CONV_TM, t), SSM_CONV_W
    nt, taps = t // tm, w.shape[0]

    def kern(x_ref, xb_ref, w_ref, b_ref, o_ref, buf):
        _fill_ext(buf, x_ref, xb_ref, None, pl.program_id(1), nt)
        for cs in _lane_chunks(wd):
            wv, bv = w_ref[:, cs], b_ref[:, cs]
            for r0 in range(0, tm, CONV_RB):
                pre = _taps_fwd(_shifted(buf, taps, r0, CONV_RB, cs), wv, bv)
                o_ref[r0:r0 + CONV_RB, cs] = pre * _sigmoid(pre)

    tile, before, _ = _halo_specs(tm, wd, t)
    par = lambda rows: pl.BlockSpec((rows, wd), lambda c, i: (0, c))
    return pl.pallas_call(
        kern, name="ssm_conv_fwd", grid=(XBC_DIM // wd, nt), in_specs=[tile, before, par(taps), par(1)],
        out_specs=pl.BlockSpec((tm, wd), lambda c, i: (i, c)), out_shape=jax.ShapeDtypeStruct((t, XBC_DIM), F32),
        scratch_shapes=[pltpu.VMEM((tm + HALO, wd), F32)],
        compiler_params=_cparams(("parallel", "parallel")))(proj, proj, w, b)


def _ssm_conv_bwd(proj, w, b, dact, parts):
    t = proj.shape[0]
    tm, wd = min(SSM_CONV_TM, t), SSM_CONV_W
    nt, taps, ncol, ns = t // tm, w.shape[0], XBC_DIM // SSM_CONV_W, len(parts)

    def kern(x_ref, xb_ref, xa_ref, d_ref, dn_ref, w_ref, b_ref, *rest):
        dx_ref, gw_ref, gb_ref = rest[ns:ns + 3]
        buf, bufd = rest[2 * ns + 3:2 * ns + 5]
        i = pl.program_id(1)
        if ns:
            start, finish = _scatter_steps(ns, rest[:ns], rest[ns + 3:2 * ns + 3], rest[2 * ns + 5:])
            pl.when((pl.program_id(0) == 0) & (i == 0))(start)
            pl.when((pl.program_id(0) == ncol - 1) & (i == nt - 1))(finish)
        _fill_ext(buf, x_ref, xb_ref, xa_ref, i, nt)

        @pl.when(i == 0)
        def _():
            gw_ref[...] = jnp.zeros(gw_ref.shape, F32)
            gb_ref[...] = jnp.zeros(gb_ref.shape, F32)
        for cs in _lane_chunks(wd):
            wv, bv = w_ref[:, cs], b_ref[:, cs]
            acc = [jnp.zeros((HALO, cs.stop - cs.start), F32) for _ in range(taps + 1)]
            for r0 in list(range(0, tm, CONV_RB)) + [tm]:
                inside = r0 < tm
                rows = CONV_RB if inside else HALO
                xs = _shifted(buf, taps, r0, rows, cs)
                d = d_ref[r0:r0 + rows, cs] if inside else jnp.where(i < nt - 1, dn_ref[:, cs], 0.0)
                dpre = d * _silu_grad(_taps_fwd(xs, wv, bv))
                bufd[r0:r0 + rows, cs] = dpre
                if inside:
                    acc[taps] = acc[taps] + _fold8(dpre)
                    for k in range(taps):
                        acc[k] = acc[k] + _fold8(dpre * xs[k])
            gb_ref[:, cs] += jnp.sum(acc[taps], axis=0, keepdims=True)
            for k in range(taps):
                gw_ref[k:k + 1, cs] += jnp.sum(acc[k], axis=0, keepdims=True)
            for r0 in range(0, tm, CONV_RB):
                dx_ref[r0:r0 + CONV_RB, cs] = _taps_bwd(bufd, wv, taps, r0, CONV_RB, cs).astype(BF16)

    xt, xb, xa = _halo_specs(tm, wd, t)
    dt_, _, dn = _halo_specs(tm, wd, t)
    par = lambda rows: pl.BlockSpec((rows, wd), lambda c, i: (0, c))
    return pl.pallas_call(
        kern, name="ssm_conv_bwd", grid=(ncol, nt), in_specs=[xt, xb, xa, dt_, dn, par(taps), par(1)] + [ANY_SPEC] * ns,
        out_specs=[pl.BlockSpec((tm, wd), lambda c, i: (i, c)), par(taps), par(1)] + [ANY_SPEC] * ns,
        out_shape=[jax.ShapeDtypeStruct((t, XBC_DIM), BF16), jax.ShapeDtypeStruct((taps, XBC_DIM), F32),
                   jax.ShapeDtypeStruct((1, XBC_DIM), F32)] + [jax.ShapeDtypeStruct(s.shape, s.dtype) for s in parts],
        scratch_shapes=[pltpu.VMEM((tm + 2 * HALO, wd), F32), pltpu.VMEM((tm + HALO, wd), F32)] + (_scatter_sems(ns) if ns else []),
        compiler_params=_cparams(("arbitrary", "arbitrary")))(proj, proj, proj, dact, dact, w, b, *parts)


def _ffn_act_down(u, w, b, w_down, x1):
    t = u.shape[1]
    tm, wd = min(FFN_CONV_TM, t), D_FF
    nt, taps = t // tm, w.shape[0]

    def kern(g_ref, gb_ref, v_ref, vb_ref, wg_ref, wv_ref, bg_ref, bv_ref, wd_ref, x1_ref, a_ref, x2_ref, bufg, bufv):
        i = pl.program_id(1)
        _fill_ext(bufg, g_ref, gb_ref, None, i, nt)
        _fill_ext(bufv, v_ref, vb_ref, None, i, nt)
        acc = x1_ref[...]
        for cs in _lane_chunks(wd):
            wg, wv, bg, bv = wg_ref[:, cs], wv_ref[:, cs], bg_ref[:, cs], bv_ref[:, cs]
            for r0 in range(0, tm, CONV_RB):
                g = _taps_fwd(_shifted(bufg, taps, r0, CONV_RB, cs), wg, bg)
                v = _taps_fwd(_shifted(bufv, taps, r0, CONV_RB, cs), wv, bv)
                a_ref[r0:r0 + CONV_RB, cs] = (g * _sigmoid(g) * v).astype(BF16)
            acc = acc + _dot(a_ref[:, cs], wd_ref[cs, :])
        x2_ref[...] = acc

    gt, gbf, _ = _halo_specs(tm, wd, t, lead=0)
    vt, vbf, _ = _halo_specs(tm, wd, t, lead=1)
    par = lambda rows, off: pl.BlockSpec((rows, wd), functools.partial(lambda c, i, off: (0, c + off), off=off))
    row = lambda width: pl.BlockSpec((tm, width), lambda c, i: (i, 0))
    return pl.pallas_call(
        kern, name="ffn_act_down", grid=(1, nt),
        in_specs=[gt, gbf, vt, vbf, par(taps, 0), par(taps, 1), par(1, 0), par(1, 1),
                  pl.BlockSpec(w_down.shape, lambda c, i: (0, 0)), row(D_MODEL)],
        out_specs=[row(wd), row(D_MODEL)],
        out_shape=[jax.ShapeDtypeStruct((t, D_FF), BF16), jax.ShapeDtypeStruct((t, D_MODEL), F32)],
        scratch_shapes=[pltpu.VMEM((tm + HALO, wd), F32)] * 2,
        compiler_params=_cparams(("parallel", "parallel")))(u, u, u, u, w, w, b, b, w_down, x1)


def _ffn_act_bwd(u, w, b, da):
    t = u.shape[1]
    tm, wd = min(FFN_CONV_TM, t), FFN_CONV_W
    nt, taps, nc = t // tm, w.shape[0], D_FF // FFN_CONV_W

    def kern(g_ref, gb_ref, ga_ref, v_ref, vb_ref, va_ref, d_ref, dn_ref, wg_ref, wv_ref, bg_ref, bv_ref,
             du_ref, gwg_ref, gwv_ref, gbg_ref, gbv_ref, bufg, bufv, bufdg, bufdv):
        i = pl.program_id(1)
        _fill_ext(bufg, g_ref, gb_ref, ga_ref, i, nt)
        _fill_ext(bufv, v_ref, vb_ref, va_ref, i, nt)

        @pl.when(i == 0)
        def _():
            for r in (gwg_ref, gwv_ref, gbg_ref, gbv_ref):
                r[...] = jnp.zeros(r.shape, F32)
        for cs in _lane_chunks(wd):
            wg, wv, bg, bv = wg_ref[:, cs], wv_ref[:, cs], bg_ref[:, cs], bv_ref[:, cs]
            zero = jnp.zeros((HALO, cs.stop - cs.start), F32)
            accg, accv = [zero] * (taps + 1), [zero] * (taps + 1)
            for r0 in list(range(0, tm, CONV_RB)) + [tm]:
                inside = r0 < tm
                rows = CONV_RB if inside else HALO
                xg, xv = _shifted(bufg, taps, r0, rows, cs), _shifted(bufv, taps, r0, rows, cs)
                g, v = _taps_fwd(xg, wg, bg), _taps_fwd(xv, wv, bv)
                dav = d_ref[r0:r0 + rows, cs] if inside else jnp.where(i < nt - 1, dn_ref[:, cs], 0.0)
                sg = _sigmoid(g)
                dg = dav * v * (sg * (1.0 + g * (1.0 - sg)))
                dv = dav * (g * sg)
                bufdg[r0:r0 + rows, cs] = dg
                bufdv[r0:r0 + rows, cs] = dv
                if inside:
                    accg[taps], accv[taps] = accg[taps] + _fold8(dg), accv[taps] + _fold8(dv)
                    for k in range(taps):
                        accg[k], accv[k] = accg[k] + _fold8(dg * xg[k]), accv[k] + _fold8(dv * xv[k])
            gbg_ref[:, cs] += jnp.sum(accg[taps], axis=0, keepdims=True)
            gbv_ref[:, cs] += jnp.sum(accv[taps], axis=0, keepdims=True)
            for k in range(taps):
                gwg_ref[k:k + 1, cs] += jnp.sum(accg[k], axis=0, keepdims=True)
                gwv_ref[k:k + 1, cs] += jnp.sum(accv[k], axis=0, keepdims=True)
            for r0 in range(0, tm, CONV_RB):
                du_ref[0, r0:r0 + CONV_RB, cs] = _taps_bwd(bufdg, wg, taps, r0, CONV_RB, cs).astype(BF16)
                du_ref[1, r0:r0 + CONV_RB, cs] = _taps_bwd(bufdv, wv, taps, r0, CONV_RB, cs).astype(BF16)

    gt, gbf, gaf = _halo_specs(tm, wd, t, lead=0)
    vt, vbf, vaf = _halo_specs(tm, wd, t, lead=1)
    dt_, _, dn = _halo_specs(tm, wd, t)
    par = lambda rows, off: pl.BlockSpec((rows, wd), functools.partial(lambda c, i, off: (0, c + off), off=off))
    return pl.pallas_call(
        kern, name="ffn_act_bwd", grid=(nc, nt),
        in_specs=[gt, gbf, gaf, vt, vbf, vaf, dt_, dn, par(taps, 0), par(taps, nc), par(1, 0), par(1, nc)],
        out_specs=[pl.BlockSpec((2, tm, wd), lambda c, i: (0, i, c)), par(taps, 0), par(taps, 0), par(1, 0), par(1, 0)],
        out_shape=[jax.ShapeDtypeStruct((2, t, D_FF), BF16)] + [jax.ShapeDtypeStruct((taps, D_FF), F32)] * 2
        + [jax.ShapeDtypeStruct((1, D_FF), F32)] * 2,
        scratch_shapes=[pltpu.VMEM((tm + 2 * HALO, wd), F32)] * 2 + [pltpu.VMEM((tm + HALO, wd), F32)] * 2,
        compiler_params=_cparams(("parallel", "arbitrary")))(u, u, u, u, u, u, da, da, w, w, b, b)


def _softplus(x):
    e = jnp.exp(-jnp.abs(x))
    return jnp.maximum(x, 0.0) + jnp.where(e < 1e-4, e - 0.5 * e * e, jnp.log(1.0 + e))


def _tri(lower):
    r = lax.broadcasted_iota(jnp.int32, (CHUNK, CHUNK), 0)
    c = lax.broadcasted_iota(jnp.int32, (CHUNK, CHUNK), 1)
    return (r >= c) if lower else (r <= c)


def _cum(mat_bool, x):
    return jnp.dot(mat_bool.astype(F32), x, precision=lax.Precision.HIGHEST, preferred_element_type=F32)


def _pair_sel(lane_lo, tile, h0):
    return jnp.where(lane_lo, tile[:, h0:h0 + 1], tile[:, h0 + 1:h0 + 2])


def _pair_sel_mxu(lane_lo, tile, h0):
    rows = lax.broadcasted_iota(jnp.int32, (LANE, LANE), 0)
    sel = (rows == jnp.where(lane_lo, h0, h0 + 1)).astype(BF16)
    return _dot_split(tile, sel)


def _ssd_fwd(xbc_act, proj, dt_bias_p, a_log_p, dskip_t, shards):
    t = xbc_act.shape[0]
    nch = t // CHUNK
    ns = len(shards)

    def kern(xa_ref, dtr_ref, bias_ref, alog_ref, dsk_ref, *rest):
        y_ref, dt_ref, hs_ref = rest[ns:ns + 3]
        hst = rest[2 * ns + 3]
        if ns:
            start, forward, finish = _gather_steps(shards, rest[:ns], rest[ns + 3:2 * ns + 3], rest[2 * ns + 4:])
            pl.when(pl.program_id(0) == 0)(start)
            pl.when(pl.program_id(0) == (3 * nch) // 4)(forward)
            pl.when(pl.program_id(0) == nch - 1)(finish)

        @pl.when(pl.program_id(0) == 0)
        def _():
            hst[...] = jnp.zeros(hst.shape, F32)
        dt = _softplus(dtr_ref[...] + bias_ref[...])
        dt_ref[...] = dt
        acum = _cum(_tri(True), dt * (-jnp.exp(alog_ref[...])))
        acum_t = acum.T
        ea = jnp.exp(acum)
        a_last = acum[CHUNK - 1:CHUNK, :]
        dend = jnp.exp(a_last - acum)
        ea_last = jnp.exp(a_last)
        causal = _tri(True)
        lane_lo = lax.broadcasted_iota(jnp.int32, (CHUNK, LANE), 1) < HEAD_DIM
        row_lo = lax.broadcasted_iota(jnp.int32, (CHUNK, LANE), 0) < HEAD_DIM
        for g in range(2):
            bg = xa_ref[:, SSM_INNER + g * SSM_STATE:SSM_INNER + (g + 1) * SSM_STATE].astype(BF16)
            cg = xa_ref[:, SSM_INNER + BC_DIM + g * SSM_STATE:SSM_INNER + BC_DIM + (g + 1) * SSM_STATE].astype(BF16)
            cb = _dot_nt(cg, bg)
            for j in range(4 * g, 4 * g + 4):
                h0 = 2 * j
                cols = slice(j * LANE, (j + 1) * LANE)
                xp = xa_ref[:, cols]
                xdt = xp * _pair_sel(lane_lo, dt, h0)
                ydiag = None
                for hh, sel in ((h0, lane_lo), (h0 + 1, ~lane_lo)):
                    seg = acum[:, hh:hh + 1] - acum_t[hh:hh + 1, :]
                    mm = (cb * jnp.where(causal, jnp.exp(jnp.minimum(seg, 0.0)), 0.0)).astype(BF16)
                    d = _dot(mm, jnp.where(sel, xdt, 0.0).astype(BF16))
                    ydiag = d if ydiag is None else ydiag + d
                hp = hst[cols, :]
                hs_ref[cols, :] = hp
                yoff = _dot_nt(cg, hp.astype(BF16)) * _pair_sel(lane_lo, ea, h0)
                y_ref[:, cols] = ydiag + yoff + dsk_ref[:, cols] * xp
                xw = (xdt * _pair_sel(lane_lo, dend, h0)).astype(BF16)
                rowf = jnp.where(row_lo, ea_last[:, h0:h0 + 1], ea_last[:, h0 + 1:h0 + 2])
                hst[cols, :] = hp * rowf + _dot_tn(xw, bg)

    return pl.pallas_call(
        kern, name="ssd_fwd", grid=(nch,),
        in_specs=[pl.BlockSpec((CHUNK, XBC_DIM), lambda c: (c, 0)), pl.BlockSpec((CHUNK, LANE), lambda c: (c, OFF_DT // LANE)),
                  pl.BlockSpec((1, LANE), lambda c: (0, 0)), pl.BlockSpec((1, LANE), lambda c: (0, 0)),
                  pl.BlockSpec((1, SSM_INNER), lambda c: (0, 0))] + [ANY_SPEC] * ns,
        out_specs=[pl.BlockSpec((CHUNK, SSM_INNER), lambda c: (c, 0)), pl.BlockSpec((CHUNK, LANE), lambda c: (c, 0)),
                   pl.BlockSpec((None, SSM_INNER, SSM_STATE), lambda c: (c, 0, 0))] + [ANY_SPEC] * ns,
        out_shape=[jax.ShapeDtypeStruct((t, SSM_INNER), F32), jax.ShapeDtypeStruct((t, LANE), F32),
                   jax.ShapeDtypeStruct((nch, SSM_INNER, SSM_STATE), F32)] + _gather_out_shapes(shards),
        scratch_shapes=[pltpu.VMEM((SSM_INNER, SSM_STATE), F32)] + (_gather_sems(ns) if ns else []),
        compiler_params=_cparams(("arbitrary",)))(xbc_act, proj, dt_bias_p, a_log_p, dskip_t, *shards)


def _ssd_bwd(xbc_act, proj, dt_sp, hstates, dy, dt_bias_p, a_log_p, dskip_t, swaps):
    t = xbc_act.shape[0]
    nch = t // CHUNK
    ns = len(swaps)

    pair = jnp.arange(SSM_HEADS // 2)[:, None, None]
    psel = (jnp.arange(LANE)[None, None, :] == 2 * pair + (jnp.arange(LANE) // HEAD_DIM)[None, :, None]).astype(BF16)

    def kern(xa_ref, dtr_ref, dt_ref, hs_ref, dy_ref, bias_ref, alog_ref, dsk_ref, psel_ref, *rest):
        dact_ref, ddtr_ref, da_ref, dbias_ref, ddsk_ref = rest[ns:ns + 5]
        dh = rest[2 * ns + 5]
        if ns:
            start, finish = _swap_steps(swaps, rest[:ns], rest[ns + 5:2 * ns + 5], rest[2 * ns + 6:])
            pl.when(pl.program_id(0) == 0)(start)
            pl.when(pl.program_id(0) == nch - 1)(finish)

        @pl.when(pl.program_id(0) == 0)
        def _():
            dh[...] = jnp.zeros(dh.shape, F32)
            for r in (da_ref, dbias_ref, ddsk_ref):
                r[...] = jnp.zeros(r.shape, F32)
        dt = dt_ref[...]
        a_neg = -jnp.exp(alog_ref[...])
        acum = _cum(_tri(True), dt * a_neg)
        acum_t = acum.T
        ea = jnp.exp(acum)
        a_last = acum[CHUNK - 1:CHUNK, :]
        dend = jnp.exp(a_last - acum)
        ea_last = jnp.exp(a_last)
        causal = _tri(True)
        lane = lax.broadcasted_iota(jnp.int32, (CHUNK, LANE), 1)
        rowi = lax.broadcasted_iota(jnp.int32, (CHUNK, LANE), 0)
        lane_lo, row_lo, last_row = lane < HEAD_DIM, rowi < HEAD_DIM, rowi == CHUNK - 1
        d_dt = jnp.zeros((CHUNK, LANE), F32)
        d_acum = jnp.zeros((CHUNK, LANE), F32)
        for g in range(2):
            bcols = slice(SSM_INNER + g * SSM_STATE, SSM_INNER + (g + 1) * SSM_STATE)
            ccols = slice(SSM_INNER + BC_DIM + g * SSM_STATE, SSM_INNER + BC_DIM + (g + 1) * SSM_STATE)
            bg, cg = xa_ref[:, bcols].astype(BF16), xa_ref[:, ccols].astype(BF16)
            cb = _dot_nt(cg, bg)
            dg_sum = jnp.zeros((CHUNK, CHUNK), F32)
            dcg = jnp.zeros((CHUNK, SSM_STATE), F32)
            dbg = jnp.zeros((CHUNK, SSM_STATE), F32)
            for j in range(4 * g, 4 * g + 4):
                h0 = 2 * j
                cols = slice(j * LANE, (j + 1) * LANE)
                xp, dyp = xa_ref[:, cols], dy_ref[:, cols]
                dtsel = _pair_sel_mxu(lane_lo, dt, h0)
                xdt = xp * dtsel
                xdt_b = xdt.astype(BF16)
                hp, dhp = hs_ref[cols, :], dh[cols, :]
                hp_b, dhp_b = hp.astype(BF16), dhp.astype(BF16)
                easel, dendsel = _pair_sel_mxu(lane_lo, ea, h0), _pair_sel_mxu(lane_lo, dend, h0)
                dx, ydiag = None, None
                for hh, sel in ((h0, lane_lo), (h0 + 1, ~lane_lo)):
                    dyh = jnp.where(sel, dyp, 0.0).astype(BF16)
                    seg = acum[:, hh:hh + 1] - acum_t[hh:hh + 1, :]
                    dec = jnp.where(causal, jnp.exp(jnp.minimum(seg, 0.0)), 0.0)
                    mm_b = (cb * dec).astype(BF16)
                    dg_sum = dg_sum + dec * _dot_nt(dyh, xdt_b)
                    d = _dot_tn(mm_b, dyh)
                    y = _dot(mm_b, jnp.where(sel, xdt, 0.0).astype(BF16))
                    dx = d if dx is None else dx + d
                    ydiag = y if ydiag is None else ydiag + y
                g2 = _dot_nt(bg, dhp_b)
                tprod = xdt * g2 * dendsel
                yoff = _dot_nt(cg, hp_b) * easel
                yc = dyp.astype(BF16).astype(F32) * ydiag + dyp * yoff - (xdt_b.astype(F32) * dx + tprod)
                dx = dx + g2 * dendsel
                psel = psel_ref[j]
                t_lo = jnp.sum(jnp.where(lane_lo, tprod, 0.0), keepdims=True).reshape(1, 1)
                t_hi = jnp.sum(tprod, keepdims=True).reshape(1, 1) - t_lo
                hh_prod = dhp * hp
                s_lo = jnp.sum(jnp.where(row_lo, hh_prod, 0.0), keepdims=True).reshape(1, 1)
                s_hi = jnp.sum(hh_prod, keepdims=True).reshape(1, 1) - s_lo
                end_lo = ea_last[:, h0:h0 + 1] * s_lo + t_lo
                end_hi = ea_last[:, h0 + 1:h0 + 2] * s_hi + t_hi
                ends = jnp.where(lane == h0, end_lo, jnp.where(lane == h0 + 1, end_hi, 0.0))
                d_acum = d_acum + _dot_split(yc, psel) + jnp.where(last_row, ends, 0.0)
                dye = (dyp * easel).astype(BF16)
                dcg = dcg + _dot(dye, hp_b)
                dbg = dbg + _dot((xdt * dendsel).astype(BF16), dhp_b)
                rowf = jnp.where(row_lo, ea_last[:, h0:h0 + 1], ea_last[:, h0 + 1:h0 + 2])
                dh[cols, :] = dhp * rowf + _dot_tn(dye, cg)
                dact_ref[:, cols] = dx * dtsel + dsk_ref[:, cols] * dyp
                d_dt = d_dt + _dot_split(dx * xp, psel)
                ddsk_ref[:, cols] += jnp.sum(dyp * xp, axis=0, keepdims=True)
            dg_b = dg_sum.astype(BF16)
            dact_ref[:, ccols] = dcg + _dot(dg_b, bg)
            dact_ref[:, bcols] = dbg + _dot_tn(dg_b, cg)
        d_adt = _cum(_tri(False), d_acum)
        d_dt = d_dt + d_adt * a_neg
        da_ref[...] += jnp.sum(d_adt * dt, axis=0, keepdims=True)
        d_raw = jnp.where(lane < SSM_HEADS, d_dt * _sigmoid(dtr_ref[...] + bias_ref[...]), 0.0)
        ddtr_ref[...] = d_raw.astype(BF16)
        dbias_ref[...] += jnp.sum(d_raw, axis=0, keepdims=True)

    rev = lambda c: (nch - 1 - c, 0)
    return pl.pallas_call(
        kern, name="ssd_bwd", grid=(nch,),
        in_specs=[pl.BlockSpec((CHUNK, XBC_DIM), rev), pl.BlockSpec((CHUNK, LANE), lambda c: (nch - 1 - c, OFF_DT // LANE)),
                  pl.BlockSpec((CHUNK, LANE), rev), pl.BlockSpec((None, SSM_INNER, SSM_STATE), lambda c: (nch - 1 - c, 0, 0)),
                  pl.BlockSpec((CHUNK, SSM_INNER), rev),
                  pl.BlockSpec((1, LANE), lambda c: (0, 0)), pl.BlockSpec((1, LANE), lambda c: (0, 0)),
                  pl.BlockSpec((1, SSM_INNER), lambda c: (0, 0)), pl.BlockSpec(psel.shape, lambda c: (0, 0, 0))] + [ANY_SPEC] * ns,
        out_specs=[pl.BlockSpec((CHUNK, XBC_DIM), rev), pl.BlockSpec((CHUNK, LANE), rev),
                   pl.BlockSpec((1, LANE), lambda c: (0, 0)), pl.BlockSpec((1, LANE), lambda c: (0, 0)),
                   pl.BlockSpec((1, SSM_INNER), lambda c: (0, 0))] + [ANY_SPEC] * ns,
        out_shape=[jax.ShapeDtypeStruct((t, XBC_DIM), F32), jax.ShapeDtypeStruct((t, LANE), BF16),
                   jax.ShapeDtypeStruct((1, LANE), F32), jax.ShapeDtypeStruct((1, LANE), F32),
                   jax.ShapeDtypeStruct((1, SSM_INNER), F32)] + _swap_out_shapes(swaps),
        scratch_shapes=[pltpu.VMEM((SSM_INNER, SSM_STATE), F32)] + (_swap_sems(ns) if ns else []),
        compiler_params=_cparams(("arbitrary",)))(xbc_act, proj, dt_sp, hstates, dy, dt_bias_p, a_log_p, dskip_t, psel, *swaps)


def _pad_lanes(v, width=LANE):
    return jnp.pad(v, ((0, 0), (0, width - v.shape[1])))


def _local_step(x, p, tgt, wts, late_shards=(), join_late=None, reduce_early=None, reduce_late=None):
    g_attn, g_ssm, g_ffn, g_ple = wts["attn_norm_g"], wts["ssm_norm_g"], wts["ffn_norm_g"], wts["ple_norm_g"]
    w_in_p = wts["w_in_p"]
    gq_t = jnp.tile(wts["q_norm_g"], (1, ATTN_DIM // HEAD_DIM))
    gk_t = jnp.tile(wts["k_norm_g"], (1, KV_DIM // HEAD_DIM))
    dt_bias_p, a_log_p = _pad_lanes(wts["dt_bias"]), _pad_lanes(wts["a_log"])
    dskip_t = jnp.repeat(wts["d_skip"], HEAD_DIM, axis=1)

    h1, proj, q_hm, kv_hm = _in_proj(x, g_attn, w_in_p, gq_t, gk_t)
    pats = _attn_fwd_all(q_hm, kv_hm)
    xbc_act =_ssm_conv_fwd(proj, wts["ssm_conv_w"], wts["ssm_conv_b"])
    y_ssd, dt_sp, hstates, *gathered = _ssd_fwd(xbc_act, proj, dt_bias_p, a_log_p, dskip_t, list(late_shards))
    if join_late is not None:
        wts = {**wts, **join_late(gathered)}
    w_out_s, w_out_a = wts["w_out_ssm"], wts["w_out_attn"]
    w_up, w_down, w_gate, w_proj = wts["w_up"], wts["w_down"], wts["w_ple_gate"], wts["w_ple_proj"]
    ssm_out, attn_out, lse, x1 = _mix_out_proj(y_ssd, proj, g_ssm, w_out_s, [o for o, _ in pats], [l for _, l in pats], w_out_a, x)
    h2, u = _norm_mm("ffn_up", x1, g_ffn, w_up, tm=1024, tn=1408, halves=True)
    a, x2 = _ffn_act_down(u, wts["ffn_conv_w"], wts["ffn_conv_b"], w_down, x1)
    pb = p.astype(BF16)
    h3, dy, dgl, dpp, sq = _ple_head(x2, g_ple, w_gate, pb, w_proj, tgt)

    grads = {}
    grads["w_ple_proj"] = _mm_tn("g_ple_proj", pb, dpp, tn=PLE_DIM, chip_cols=True)
    grads["w_ple_gate"] = _mm_tn("g_ple_gate", h3, dgl)
    dx2, dx2b, grads["ple_norm_g"] = _mm_nt_rms_bwd("d_h3", dgl, w_gate, x2, g_ple, dy)
    da = _mm_nt("d_ffn_act", [(dx2b, w_down, 0)], F32, tm=1024, tn=1408)
    grads["w_down"] = _mm_tn("g_ffn_down", a, dx2b, tm=1408)
    du, gwg, gwv, gbg, gbv = _ffn_act_bwd(u, wts["ffn_conv_w"], wts["ffn_conv_b"], da)
    grads["ffn_conv_w"] = jnp.concatenate([gwg, gwv], axis=1)
    grads["ffn_conv_b"] = jnp.concatenate([gbg, gbv], axis=1)
    grads["w_up"] = _mm_tn("g_ffn_up", h2, du, tn=1408, chip_cols=True)
    dh2 = _mm_nt("d_h2", [(du, w_up, 0, 0), (du, w_up, 1, 1)], F32, tm=1024, tn=512)
    dx1, dx1b, grads["ffn_norm_g"] = _rms_bwd("rms_ffn_bwd", dh2, x1, g_ffn, dx2)
    dy_ssd, dz, grads["ssm_norm_g"], do_hm, dsum = _d_mix(dx1b, jnp.concatenate([w_out_s, w_out_a], axis=0), y_ssd, proj, g_ssm,
                                                            attn_out)
    grads["w_out"] = jnp.concatenate([_mm_tn("g_out_attn", attn_out, dx1b), _mm_tn("g_out_ssm", ssm_out, dx1b)], axis=0)
    early_major = reduce_early[0](grads) if reduce_early is not None else []
    dact, ddtr, d_a, d_bias, d_dsk, *early_got = _ssd_bwd(xbc_act, proj, dt_sp, hstates, dy_ssd, dt_bias_p, a_log_p, dskip_t,
                                                           early_major)
    grads["dt_bias"] = d_bias[:, :SSM_HEADS]
    grads["a_log"] = d_a[:, :SSM_HEADS] * (-jnp.exp(wts["a_log"]))
    grads["d_skip"] = jnp.sum(d_dsk.reshape(SSM_HEADS, HEAD_DIM), axis=1)[None, :]
    chip_sums = reduce_early[1](early_major, early_got) if reduce_early is not None else []
    dxbc, grads["ssm_conv_w"], grads["ssm_conv_b"], *scattered = _ssm_conv_bwd(proj, wts["ssm_conv_w"], wts["ssm_conv_b"], dact,
                                                                                chip_sums)
    dqs = _attn_dq_all(q_hm, kv_hm, do_hm, lse, dsum)
    dkvs = [_attn_dkv2(q_hm, kv_hm, do_hm, lse, dsum, d) for d in DILATIONS]
    dq, dk, dv, dgq, dgk = _qknorm_bwd2(proj, gq_t, gk_t, dqs, dkvs)
    grads["q_norm_g"] = jnp.sum(dgq.reshape(ATTN_DIM // HEAD_DIM, HEAD_DIM), axis=0)[None, :]
    grads["k_norm_g"] = jnp.sum(dgk.reshape(KV_DIM // HEAD_DIM, HEAD_DIM), axis=0)[None, :]
    dproj = jnp.concatenate([dxbc, dq, dz, dk, dv, ddtr], axis=1)
    grads["w_in_p"] = _mm_tn("g_in_proj", h1, dproj, tm=512)
    late_sums = reduce_late(grads) if reduce_late is not None else []
    grad_x, _, grads["attn_norm_g"], *late_scattered = _mm_nt_rms_bwd("d_h1", dproj, w_in_p, x, g_attn, dx1, late_sums)
    return sq, grad_x, grads, (chip_sums, scattered), (late_sums, late_scattered)


MESH_IDS = pl.DeviceIdType.MESH
N_CHIPS = 4
ANY_SPEC = pl.BlockSpec(memory_space=pl.ANY)
SMALL_ROWS = 96
ALL_SMALL_ROWS = 272


def _place():
    x, y, c = lax.axis_index("x"), lax.axis_index("y"), lax.axis_index("c")
    return x, y, c, [(1 - x, y), (x, 1 - y), (1 - x, 1 - y)]


def _gather_over_chips(arrs):
    n = len(arrs)

    def body(*refs):
        steps = _gather_steps(arrs, refs[:n], refs[n:2 * n], refs[2 * n:2 * n + 4])
        for step in steps:
            step()

    return pl.pallas_call(
        body, name="gather_weights", in_specs=[ANY_SPEC] * n, out_specs=[ANY_SPEC] * n,
        out_shape=_gather_out_shapes(arrs), scratch_shapes=_gather_sems(n))(*arrs)


def _gather_out_shapes(arrs):
    return [jax.ShapeDtypeStruct((N_CHIPS,) + a.shape, a.dtype) for a in arrs]


def _gather_sems(n):
    return [pltpu.SemaphoreType.DMA((3 * n,))] * 4


def _gather_steps(arrs, ins, outs, sems):
    n = len(arrs)
    split = [a.shape[0] % 64 == 0 for a in arrs]
    ici_send, ici_recv, d2d_send, d2d_recv = sems

    def place():
        x, y, c, chips = _place()
        return x, y, c, chips, 2 * x + y

    def part(ref, a, core):
        if not split[a]:
            return ref
        half = arrs[a].shape[0] // 2
        return ref.at[pl.ds(core * half, half)]

    def ici(a, k, slot, where):
        x, y, c, chips, _ = where
        px, py = chips[k]
        return pltpu.make_async_remote_copy(
            src_ref=part(ins[a], a, c), dst_ref=part(outs[a].at[slot], a, c), send_sem=ici_send.at[3 * a + k],
            recv_sem=ici_recv.at[3 * a + k], device_id=(px, py, c), device_id_type=MESH_IDS)

    def d2d(a, k, core, where):
        x, y, c, chips, _ = where
        px, py = chips[k]
        piece = part(outs[a].at[2 * px + py], a, core)
        return pltpu.make_async_remote_copy(src_ref=piece, dst_ref=piece, send_sem=d2d_send.at[3 * a + k],
                                            recv_sem=d2d_recv.at[3 * a + k], device_id=(x, y, 1 - c), device_id_type=MESH_IDS)

    def start():
        w = place()
        for a in range(n):
            for k in range(3):
                ici(a, k, w[4], w).start()

    def forward():
        w = place()
        for a in range(n):
            for k, (px, py) in enumerate(w[3]):
                ici(a, k, 2 * px + py, w).wait_recv()
                if split[a]:
                    d2d(a, k, w[2], w).start()

    def finish():
        w = place()
        for a in range(n):
            for k in range(3):
                if split[a]:
                    d2d(a, k, 1 - w[2], w).wait_recv()
                    d2d(a, k, w[2], w).wait_send()
                ici(a, k, w[4], w).wait_send()

    return start, forward, finish


def _row_tile(rows, cap=256):
    return max(d for d in range(8, cap + 1, 8) if rows % d == 0)


def _swap_halves(name, gs):
    n = len(gs)

    def body(*refs):
        for step in _swap_steps(gs, refs[:n], refs[n:2 * n], refs[2 * n:2 * n + 2]):
            step()

    return pl.pallas_call(
        body, name=name, in_specs=[ANY_SPEC] * n, out_specs=[ANY_SPEC] * n, out_shape=_swap_out_shapes(gs),
        scratch_shapes=_swap_sems(n))(*gs)


def _swap_out_shapes(gs):
    return [jax.ShapeDtypeStruct((N_CHIPS, g.shape[1] // 2, g.shape[2]), g.dtype) for g in gs]


def _swap_sems(n):
    return [pltpu.SemaphoreType.DMA((N_CHIPS * n,))] * 2


def _swap_steps(gs, ins, outs, sems):
    send, recv = sems

    def copies():
        x, y, c, _ = _place()
        cps = []
        for a in range(len(gs)):
            half = gs[a].shape[1] // 2
            for q in range(N_CHIPS):
                cps.append(pltpu.make_async_remote_copy(
                    src_ref=ins[a].at[q, pl.ds((1 - c) * half, half)], dst_ref=outs[a].at[q], send_sem=send.at[N_CHIPS * a + q],
                    recv_sem=recv.at[N_CHIPS * a + q], device_id=(x, y, 1 - c), device_id_type=MESH_IDS))
        return cps

    def start():
        for cp in copies():
            cp.start()

    def finish():
        for cp in copies():
            cp.wait()

    return start, finish


def _add_halves(name, g, got, c_idx):
    rows, cols = got.shape[1:]
    tm = _row_tile(rows)
    per = rows // tm

    def kern(c_ref, g_ref, r_ref, o_ref):
        o_ref[...] = (g_ref[...] + r_ref[...]).astype(BF16)

    return pl.pallas_call(
        kern, name=name,
        grid_spec=pltpu.PrefetchScalarGridSpec(
            num_scalar_prefetch=1, grid=(N_CHIPS, per),
            in_specs=[pl.BlockSpec((None, tm, cols), lambda q, i, c_ref: (q, c_ref[0] * per + i, 0)),
                      pl.BlockSpec((None, tm, cols), lambda q, i, c_ref: (q, i, 0))],
            out_specs=pl.BlockSpec((None, tm, cols), lambda q, i, c_ref: (q, i, 0))),
        out_shape=jax.ShapeDtypeStruct((N_CHIPS, rows, cols), BF16),
        compiler_params=_cparams(("parallel", "parallel")))(c_idx, g, got)


def _scatter_sems(n):
    return [pltpu.SemaphoreType.DMA((3 * n,))] * 2


def _scatter_steps(n, ins, outs, sems):
    send, recv = sems

    def copy(a, k, slot, where):
        x, y, c, chips = where
        px, py = chips[k]
        return pltpu.make_async_remote_copy(src_ref=ins[a].at[2 * px + py], dst_ref=outs[a].at[slot], send_sem=send.at[3 * a + k],
                                            recv_sem=recv.at[3 * a + k], device_id=(px, py, c), device_id_type=MESH_IDS)

    def start():
        w = _place()
        for a in range(n):
            for k in range(3):
                copy(a, k, 2 * w[0] + w[1], w).start()

    def finish():
        w = _place()
        for a in range(n):
            for k, (px, py) in enumerate(w[3]):
                copy(a, k, 2 * px + py, w).wait()

    return start, finish


def _sum_chips(name, own, parts, idx):
    rows, cols = parts.shape[1:]
    tm = _row_tile(rows)
    per = rows // tm

    def kern(o_idx, a_ref, b_ref, c_ref, d_ref, o_ref):
        o_ref[...] = ((a_ref[...].astype(F32) + b_ref[...].astype(F32)) + c_ref[...].astype(F32)) + d_ref[...].astype(F32)

    def spec(k):
        return pl.BlockSpec((None, tm, cols), functools.partial(lambda i, o_idx, k: (o_idx[k], i, 0), k=k))

    return pl.pallas_call(
        kern, name=name,
        grid_spec=pltpu.PrefetchScalarGridSpec(
            num_scalar_prefetch=1, grid=(per,), in_specs=[spec(0), spec(1), spec(2), spec(3)],
            out_specs=pl.BlockSpec((None, tm, cols), lambda i, o_idx: (0, o_idx[4] * per + i, 0))),
        out_shape=jax.ShapeDtypeStruct((1, 2 * rows, cols), F32), compiler_params=_cparams(("parallel",)))(idx, own, parts, parts, parts)


def _share_with_sibling(gs):
    n = len(gs)

    def body(*refs):
        ins, send, recv = refs[:n], refs[2 * n], refs[2 * n + 1]
        x, y, c, _ = _place()
        cps = []
        for a in range(n):
            half = gs[a].shape[1] // 2
            mine = pl.ds(c * half, half)
            cps.append(pltpu.make_async_remote_copy(src_ref=ins[a].at[0, mine], dst_ref=refs[n + a].at[0, mine], send_sem=send.at[a],
                                                    recv_sem=recv.at[a], device_id=(x, y, 1 - c), device_id_type=MESH_IDS))
        for cp in cps:
            cp.start()
        for cp in cps:
            cp.wait()

    return pl.pallas_call(
        body, name="grad_share_sibling", in_specs=[ANY_SPEC] * n, out_specs=[ANY_SPEC] * n,
        out_shape=[jax.ShapeDtypeStruct(g.shape, g.dtype) for g in gs], input_output_aliases={a: a for a in range(n)},
        scratch_shapes=[pltpu.SemaphoreType.DMA((n,))] * 2)(*gs)


def _allreduce_small(v):
    def body(v_ref, o_ref, land, send, recv):
        x, y, c, _ = _place()
        me = 4 * x + 2 * y + c
        land[me] = v_ref[...]
        cps = []
        for rel in range(1, 8):
            bx, by, bc = (rel >> 2) & 1, (rel >> 1) & 1, rel & 1
            peer = (1 - x if bx else x, 1 - y if by else y, 1 - c if bc else c)
            cps.append(pltpu.make_async_remote_copy(src_ref=v_ref, dst_ref=land.at[me], send_sem=send.at[rel - 1],
                                                    recv_sem=recv.at[rel - 1], device_id=peer, device_id_type=MESH_IDS))
        for cp in cps:
            cp.start()
        for cp in cps:
            cp.wait()
        acc = land[0]
        for d in range(1, 8):
            acc = acc + land[d]
        o_ref[...] = acc

    vm = pl.BlockSpec(memory_space=pltpu.VMEM)
    return pl.pallas_call(
        body, name="allreduce_small", in_specs=[vm], out_specs=vm, out_shape=jax.ShapeDtypeStruct(v.shape, F32),
        scratch_shapes=[pltpu.VMEM((8,) + v.shape, F32), pltpu.SemaphoreType.DMA((7,)), pltpu.SemaphoreType.DMA((7,))])(v)


def _adamw(name, w, g, m, v):
    _, rows, cols = w.shape
    tm = rows if rows * cols <= 128 * 1024 else _row_tile(rows, max(256, 2048 * LANE // cols))
    c1 = 1.0 / (1.0 - ADAM_B1 ** ADAM_STEP)
    c2 = 1.0 / (1.0 - ADAM_B2 ** ADAM_STEP)

    def kern(w_ref, g_ref, m_ref, v_ref, d_ref, mo_ref, vo_ref):
        gv = g_ref[...]
        mn = ADAM_B1 * m_ref[...] + (1.0 - ADAM_B1) * gv
        vn = ADAM_B2 * v_ref[...] + (1.0 - ADAM_B2) * (gv * gv)
        d_ref[...] = -ADAM_LR * ((mn * c1) / (jnp.sqrt(vn * c2) + ADAM_EPS) + ADAM_WD * w_ref[...])
        mo_ref[...] = mn
        vo_ref[...] = vn

    spec = pl.BlockSpec((None, tm, cols), lambda i: (0, i, 0))
    return pl.pallas_call(
        kern, name=name, grid=(rows // tm,), in_specs=[spec] * 4, out_specs=[spec] * 3,
        out_shape=[jax.ShapeDtypeStruct(w.shape, F32)] * 3, compiler_params=_cparams(("parallel",)))(w, g, m, v)


SHARDED = (("w_in", 1), ("w_out", 0), ("w_up", 1), ("w_down", 0), ("w_ple_gate", 0), ("w_ple_proj", 1),
           ("ssm_conv_w", 1), ("ffn_conv_w", 1))
MATRICES = ("w_in", "w_out", "w_up", "w_down", "w_ple_gate", "w_ple_proj")
EARLY_REDUCED = MATRICES[1:]
REPLICATED = ("attn_norm_g", "q_norm_g", "k_norm_g", "ssm_conv_b", "dt_bias", "a_log", "d_skip", "ssm_norm_g",
              "ffn_norm_g", "ffn_conv_b", "ple_norm_g")
WEIGHT_ORDER = ("attn_norm_g", "w_in", "q_norm_g", "k_norm_g", "ssm_conv_w", "ssm_conv_b", "dt_bias", "a_log", "d_skip",
                "ssm_norm_g", "w_out", "ffn_norm_g", "w_up", "ffn_conv_w", "ffn_conv_b", "w_down", "ple_norm_g",
                "w_ple_gate", "w_ple_proj")


def _join_chips(g, axis):
    if axis == 0:
        return g.reshape(g.shape[0] * g.shape[1], g.shape[2])
    return jnp.transpose(g, (1, 0, 2)).reshape(g.shape[1], g.shape[0] * g.shape[2])


def _split_chips(g, axis):
    if axis == 0:
        return g.reshape(N_CHIPS, g.shape[0] // N_CHIPS, g.shape[1])
    r, c = g.shape
    return jnp.transpose(g.reshape(r, N_CHIPS, c // N_CHIPS), (1, 0, 2))


def _pack_small(vals, rows=SMALL_ROWS):
    flat = jnp.concatenate([v.reshape(-1) for v in vals])
    return jnp.pad(flat, (0, rows * LANE - flat.shape[0])).reshape(rows, LANE)


def _unpack_small(packed, like):
    flat, out, off = packed.reshape(-1), [], 0
    for v in like:
        out.append(flat[off:off + v.size].reshape(v.shape))
        off += v.size
    return out


def kernel(x, p, attn_norm_g, w_in, q_norm_g, k_norm_g, ssm_conv_w, ssm_conv_b, dt_bias, a_log, d_skip, ssm_norm_g, w_out, ffn_norm_g, w_up, ffn_conv_w, ffn_conv_b, w_down, ple_norm_g, w_ple_gate, w_ple_proj, loss_target, m_attn_norm_g, m_w_in, m_q_norm_g, m_k_norm_g, m_ssm_conv_w, m_ssm_conv_b, m_dt_bias, m_a_log, m_d_skip, m_ssm_norm_g, m_w_out, m_ffn_norm_g, m_w_up, m_ffn_conv_w, m_ffn_conv_b, m_w_down, m_ple_norm_g, m_w_ple_gate, m_w_ple_proj, v_attn_norm_g, v_w_in, v_q_norm_g, v_k_norm_g, v_ssm_conv_w, v_ssm_conv_b, v_dt_bias, v_a_log, v_d_skip, v_ssm_norm_g, v_w_out, v_ffn_norm_g, v_w_up, v_ffn_conv_w, v_ffn_conv_b, v_w_down, v_ple_norm_g, v_w_ple_gate, v_w_ple_proj):
    given = dict(locals())
    w2 = {n: given[n].reshape(given[n].shape[-2:]) if given[n].ndim == 3 else given[n] for n in WEIGHT_ORDER}

    cx, cy, cc = lax.axis_index("x"), lax.axis_index("y"), lax.axis_index("c")
    chip = 2 * cx + cy
    axis_of = dict(SHARDED)
    shard = lambda n: w2[n].astype(BF16) if n in MATRICES else w2[n]
    join = lambda n, g: _join_chips(lax.dynamic_update_index_in_dim(g, shard(n), chip, 0), axis_of[n])
    first = ("w_in", "ssm_conv_w", "ffn_conv_w")
    full = {n: join(n, g) for n, g in zip(first, _gather_over_chips([shard(n) for n in first]))}
    win = full["w_in"]
    w_in_p = jnp.concatenate([win[:, 2048:3584], win[:, 0:512], win[:, 1024:2048], win[:, 512:768], win[:, 768:1024],
                              win[:, 3584:3600], jnp.zeros((D_MODEL, PROJ_P - IN_PROJ), BF16)], axis=1)
    wts = {n: w2[n] for n in REPLICATED}
    wts.update(w_in_p=w_in_p, ssm_conv_w=full["ssm_conv_w"], ffn_conv_w=full["ffn_conv_w"])

    def join_late(gathered):
        late = {n: join(n, g) for n, g in zip(EARLY_REDUCED, gathered)}
        return dict(w_out_attn=late["w_out"][:ATTN_DIM], w_out_ssm=late["w_out"][ATTN_DIM:], w_up=late["w_up"],
                    w_down=late["w_down"], w_ple_gate=late["w_ple_gate"], w_ple_proj=late["w_ple_proj"])

    core = cc.astype(jnp.int32).reshape(1)
    idx = jnp.stack([chip, 2 * (1 - cx) + cy, 2 * cx + (1 - cy), 2 * (1 - cx) + (1 - cy), cc]).astype(jnp.int32)

    def major_of(names, gd):
        return [gd[n] if gd[n].ndim == 3 else _split_chips(gd[n], axis_of[n]) for n in names]

    def sums_of(names, major, got):
        return [_add_halves("grad_add_halves_" + n, g, r, core) for n, g, r in zip(names, major, got)]

    def w_in_sums(gd):
        gi = gd["w_in_p"]
        gd["w_in"] = jnp.concatenate([gi[:, OFF_Q:OFF_Q + ATTN_DIM], gi[:, OFF_K:OFF_K + KV_DIM], gi[:, OFF_V:OFF_V + KV_DIM],
                                      gi[:, OFF_Z:OFF_Z + SSM_INNER], gi[:, OFF_XBC:OFF_XBC + XBC_DIM], gi[:, OFF_DT:OFF_DT + SSM_HEADS]],
                                     axis=1)
        major = major_of(("w_in",), gd)
        return sums_of(("w_in",), major, _swap_halves("grad_swap_halves_late", major))

    sq, grad_x, grads, early, late = _local_step(
        x[0], p[0, 0], loss_target[0], wts, [shard(n) for n in EARLY_REDUCED], join_late,
        (functools.partial(major_of, EARLY_REDUCED), functools.partial(sums_of, EARLY_REDUCED)), w_in_sums)
    sums = dict(zip(EARLY_REDUCED + ("w_in",), list(zip(*early)) + list(zip(*late))))
    halves = [_sum_chips("grad_sum_chips_" + n, *sums[n], idx) for n in MATRICES]
    g_shard = dict(zip(MATRICES, _share_with_sibling(halves)))

    small_names = REPLICATED + ("ssm_conv_w", "ffn_conv_w")
    small_like = [grads[n] for n in small_names] + [jnp.zeros((1,), F32)]
    small = _allreduce_small(_pack_small([grads[n] for n in small_names] + [jnp.sum(sq).reshape(1)], ALL_SMALL_ROWS))
    small_vals = dict(zip(small_names + ("loss",), _unpack_small(small, small_like)))
    loss = (0.5 / D_MODEL) * small_vals["loss"][0]
    for n in ("ssm_conv_w", "ffn_conv_w"):
        cols = w2[n].shape[1]
        g_shard[n] = lax.dynamic_slice_in_dim(small_vals[n], chip * cols, cols, axis=1)[None]

    delta, new_m, new_v = {}, {}, {}
    for n, _ in SHARDED:
        if n == "w_in":
            r, c = w2[n].shape
            flat = lambda a: jnp.transpose(a.reshape(r, c)).reshape(1, r * c // LANE, LANE)
            back = lambda a: jnp.transpose(a.reshape(c, r)).reshape(1, r, c)
            outs = _adamw("adamw_" + n, flat(given[n]), flat(g_shard[n]), flat(given["m_" + n]), flat(given["v_" + n]))
            delta[n], new_m[n], new_v[n] = [back(o) for o in outs]
            continue
        delta[n], new_m[n], new_v[n] = _adamw("adamw_" + n, given[n], g_shard[n], given["m_" + n], given["v_" + n])
    packed = lambda prefix: _pack_small([given[prefix + n] for n in REPLICATED])[None]
    sm = _adamw("adamw_small", packed(""), _pack_small([small_vals[n] for n in REPLICATED])[None], packed("m_"), packed("v_"))
    for n in REPLICATED:
        g_shard[n] = small_vals[n]
    for dst, packed_out in zip((delta, new_m, new_v), sm):
        for n, val in zip(REPLICATED, _unpack_small(packed_out[0], [w2[n] for n in REPLICATED])):
            dst[n] = val

    def shaped(d):
        return [d[n].reshape(given[n].shape) for n in WEIGHT_ORDER]
    return (loss, grad_x[None], *shaped(g_shard), *shaped(delta), *shaped(new_m), *shaped(new_v))
```

```python
import functools

import jax
import jax.numpy as jnp
from jax import lax
from jax.experimental import pallas as pl
from jax.experimental.pallas import tpu as pltpu

F32 = jnp.float32
BF16 = jnp.bfloat16

D_MODEL = 1024
HEAD_DIM = 64
ATTN_DIM = 512
KV_DIM = 256
N_KV = 4
SSM_INNER = 1024
SSM_HEADS = 16
SSM_STATE = 128
BC_DIM = 256
XBC_DIM = SSM_INNER + 2 * BC_DIM
MIX_DIM = ATTN_DIM + SSM_INNER
IN_PROJ = 3600
D_FF = 2816
PLE_DIM = 256
CHUNK = 128
DILATIONS = (1, 4, 16)
EPS = 1e-6
ADAM_LR, ADAM_B1, ADAM_B2, ADAM_EPS, ADAM_WD, ADAM_STEP = 0.001, 0.9, 0.999, 1e-08, 0.01, 10

PROJ_P = 3712
OFF_XBC, OFF_Q, OFF_Z, OFF_K, OFF_V, OFF_DT = 0, 1536, 2048, 3072, 3328, 3584
LANE = 128
VMEM_LIMIT = 48 * 1024 * 1024
NEG = -1e30


def _cparams(sem):
    return pltpu.CompilerParams(dimension_semantics=sem, vmem_limit_bytes=VMEM_LIMIT)


def _sigmoid(x):
    return 1.0 / (1.0 + jnp.exp(-x))


def _dot(a, b):
    return jnp.dot(a, b, preferred_element_type=F32)


def _dot_nt(a, b):
    return lax.dot_general(a, b, (((1,), (1,)), ((), ())), preferred_element_type=F32)


def _dot_tn(a, b):
    return lax.dot_general(a, b, (((0,), (0,)), ((), ())), preferred_element_type=F32)


def _dot_split(x, m):
    hi = x.astype(BF16)
    lo = (x - hi.astype(F32)).astype(BF16)
    return _dot(hi, m) + _dot(lo, m)


def _rows(name, body, ins, outs, accs=(), tm=512):
    t_rows = next(s[1].shape[0] for s in ins if s[0] in ("t", "tc"))
    tm = min(tm, t_rows)
    in_specs, args = [], []
    for s in ins:
        if s[0] == "t":
            in_specs.append(pl.BlockSpec((tm, s[1].shape[1]), lambda i: (i, 0)))
        elif s[0] == "tc":
            in_specs.append(pl.BlockSpec((tm, s[2]), functools.partial(lambda i, c: (i, c), c=s[3])))
        else:
            in_specs.append(pl.BlockSpec(s[1].shape, lambda i: (0, 0)))
        args.append(s[1])
    out_shape = [jax.ShapeDtypeStruct((t_rows, w), dt) for w, dt in outs]
    out_specs = [pl.BlockSpec((tm, w), lambda i: (i, 0)) for w, _ in outs]
    out_shape += [jax.ShapeDtypeStruct(a, F32) for a in accs]
    out_specs += [pl.BlockSpec(a, lambda i: (0, 0)) for a in accs]
    n_acc = len(accs)

    def kern(*refs):
        if n_acc:
            @pl.when(pl.program_id(0) == 0)
            def _():
                for r in refs[len(refs) - n_acc:]:
                    r[...] = jnp.zeros(r.shape, F32)
        body(*refs)

    return pl.pallas_call(
        kern, name=name, grid=(t_rows // tm,), in_specs=in_specs, out_specs=out_specs, out_shape=out_shape,
        compiler_params=_cparams(("arbitrary",) if n_acc else ("parallel",)))(*args)


NCHUNK = 512


def _col_chunks(n):
    return [(c, min(NCHUNK, n - c)) for c in range(0, n, NCHUNK)]


def _mm_nt(name, pairs, out_dtype, tm=512, tn=None):
    m, n = pairs[0][0].shape[-2], pairs[0][1].shape[0]
    tn = n if tn is None else tn
    tm = min(tm, m)
    np_ = len(pairs)
    in_specs, args = [], []
    for a, w, kb, *lead in pairs:
        if lead:
            in_specs.append(pl.BlockSpec((None, tm, a.shape[2]), functools.partial(lambda j, i, ld: (ld, i, 0), ld=lead[0])))
        else:
            in_specs.append(pl.BlockSpec((tm, a.shape[1]), lambda j, i: (i, 0)))
        in_specs.append(pl.BlockSpec((tn, a.shape[-1]), functools.partial(lambda j, i, kb: (j, kb), kb=kb)))
        args += [a, w]

    def kern(*refs):
        o_ref = refs[-1]
        for c0, cw in _col_chunks(tn):
            acc = None
            for q in range(np_):
                d = _dot_nt(refs[2 * q][...], refs[2 * q + 1][c0:c0 + cw, :])
                acc = d if acc is None else acc + d
            o_ref[:, c0:c0 + cw] = acc.astype(o_ref.dtype)

    return pl.pallas_call(
        kern, name=name, grid=(n // tn, m // tm), in_specs=in_specs,
        out_specs=pl.BlockSpec((tm, tn), lambda j, i: (i, j)),
        out_shape=jax.ShapeDtypeStruct((m, n), out_dtype), compiler_params=_cparams(("parallel", "parallel")))(*args)


def _mm_tn(name, a, b, tm=None, tn=None, tk=1024, chip_cols=False):
    t, m = a.shape
    n = b.shape[-1] * (2 if b.ndim == 3 else 1)
    tm = m if tm is None else tm
    tn = n if tn is None else tn
    tk = min(tk, t)
    if b.ndim == 3:
        per = n // 2 // tn
        b_spec = pl.BlockSpec((None, tk, tn), lambda i, j, k: (j // per, k, j % per))
    else:
        b_spec = pl.BlockSpec((tk, tn), lambda i, j, k: (k, j))
    if chip_cols:
        out_spec = pl.BlockSpec((None, tm, tn), lambda i, j, k: (j, i, 0))
        out_shape = jax.ShapeDtypeStruct((n // tn, m, tn), F32)
    else:
        out_spec = pl.BlockSpec((tm, tn), lambda i, j, k: (i, j))
        out_shape = jax.ShapeDtypeStruct((m, n), F32)

    def kern(a_ref, b_ref, o_ref):
        @pl.when(pl.program_id(2) == 0)
        def _():
            o_ref[...] = jnp.zeros(o_ref.shape, F32)
        for c0, cw in _col_chunks(tn):
            o_ref[:, c0:c0 + cw] += _dot_tn(a_ref[...], b_ref[:, c0:c0 + cw])

    return pl.pallas_call(
        kern, name=name, grid=(m // tm, n // tn, t // tk),
        in_specs=[pl.BlockSpec((tk, tm), lambda i, j, k: (k, i)), b_spec], out_specs=out_spec, out_shape=out_shape,
        compiler_params=_cparams(("parallel", "parallel", "arbitrary")))(a, b)


def _rms_bwd(name, dh, x, g, dres):
    d = x.shape[1]

    def body(dh_ref, x_ref, g_ref, dres_ref, dx_ref, dxb_ref, dg_ref):
        xv, dhv = x_ref[...], dh_ref[...]
        r = lax.rsqrt(jnp.mean(xv * xv, axis=-1, keepdims=True) + EPS)
        gd = dhv * g_ref[...]
        dx = dres_ref[...] + r * gd - xv * (r * r * r * jnp.mean(xv * gd, axis=-1, keepdims=True))
        dx_ref[...] = dx
        dxb_ref[...] = dx.astype(BF16)
        dg_ref[...] += jnp.sum(dhv * xv * r, axis=0, keepdims=True)
    return _rows(name, body, [("t", dh), ("t", x), ("p", g), ("t", dres)], [(d, F32), (d, BF16)], accs=[(1, d)])


def _norm_mm(name, x, g, w, tm=512, tn=None, halves=False):
    m, k = x.shape
    n = w.shape[1]
    tn = n if tn is None else tn
    if halves:
        per = n // 2 // tn
        o_spec = pl.BlockSpec((None, tm, tn), lambda i, j: (j // per, i, j % per))
        o_shape = jax.ShapeDtypeStruct((2, m, n // 2), F32)
    else:
        o_spec = pl.BlockSpec((tm, tn), lambda i, j: (i, j))
        o_shape = jax.ShapeDtypeStruct((m, n), F32)

    def kern(x_ref, g_ref, w_ref, h_ref, o_ref):
        xv = x_ref[...]
        h = (xv * lax.rsqrt(jnp.mean(xv * xv, axis=-1, keepdims=True) + EPS) * g_ref[...]).astype(BF16)
        h_ref[...] = h
        for c0, cw in _col_chunks(tn):
            o_ref[:, c0:c0 + cw] = _dot(h, w_ref[:, c0:c0 + cw])

    return pl.pallas_call(
        kern, name=name, grid=(m // tm, n // tn),
        in_specs=[pl.BlockSpec((tm, k), lambda i, j: (i, 0)), pl.BlockSpec((1, k), lambda i, j: (0, 0)),
                  pl.BlockSpec((k, tn), lambda i, j: (0, j))],
        out_specs=[pl.BlockSpec((tm, k), lambda i, j: (i, 0)), o_spec],
        out_shape=[jax.ShapeDtypeStruct((m, k), BF16), o_shape],
        compiler_params=_cparams(("parallel", "arbitrary")))(x, g, w)


def _ple_head(x2, g, w_gate, pb, w_proj, tgt, tm=512):
    m, d = x2.shape

    def kern(x_ref, g_ref, wg_ref, p_ref, wp_ref, t_ref, h_ref, dy_ref, dgl_ref, dpp_ref, sq_ref):
        @pl.when(pl.program_id(0) == 0)
        def _():
            sq_ref[...] = jnp.zeros(sq_ref.shape, F32)
        xv = x_ref[...]
        h = (xv * lax.rsqrt(jnp.mean(xv * xv, axis=-1, keepdims=True) + EPS) * g_ref[...]).astype(BF16)
        h_ref[...] = h
        pv = p_ref[...]
        for c0, cw in _col_chunks(d):
            cs = slice(c0, c0 + cw)
            s = _sigmoid(_dot(h, wg_ref[:, cs]))
            ppv = _dot(pv, wp_ref[:, cs])
            diff = x_ref[:, cs] + s * ppv - t_ref[:, cs]
            dy = diff * (1.0 / d)
            dy_ref[:, cs] = dy
            dgl_ref[:, cs] = (dy * ppv * s * (1.0 - s)).astype(BF16)
            dpp_ref[:, cs] = (dy * s).astype(BF16)
            sq_ref[:, cs] += jnp.sum(diff * diff, axis=0, keepdims=True)

    row = lambda width: pl.BlockSpec((tm, width), lambda i: (i, 0))
    full = lambda a: pl.BlockSpec(a.shape, lambda i: (0, 0))
    return pl.pallas_call(
        kern, name="ple_head", grid=(m // tm,),
        in_specs=[row(d), full(g), full(w_gate), row(pb.shape[1]), full(w_proj), row(d)],
        out_specs=[row(d), row(d), row(d), row(d), pl.BlockSpec((1, d), lambda i: (0, 0))],
        out_shape=[jax.ShapeDtypeStruct((m, d), BF16), jax.ShapeDtypeStruct((m, d), F32), jax.ShapeDtypeStruct((m, d), BF16),
                   jax.ShapeDtypeStruct((m, d), BF16), jax.ShapeDtypeStruct((1, d), F32)],
        compiler_params=_cparams(("arbitrary",)))(x2, g, w_gate, pb, w_proj, tgt)


def _mix_out_proj(y, proj, g, w_ssm, os_, lses, w_attn, x, tm=256):
    m, d = y.shape

    def kern(y_ref, z_ref, g_ref, ws_ref, o1, o2, o3, l1, l2, l3, wa_ref, x_ref, s_ref, a_ref, lse_ref, o_ref):
        z = z_ref[...]
        yz = y_ref[...] * (z * _sigmoid(z))
        s = (yz * lax.rsqrt(jnp.mean(yz * yz, axis=-1, keepdims=True) + EPS) * g_ref[...]).astype(BF16)
        s_ref[...] = s
        pieces = []
        for kh in range(N_KV):
            a, b, c = l1[kh], l2[kh], l3[kh]
            mx = jnp.maximum(jnp.maximum(a, b), c)
            tot = mx + jnp.log(jnp.exp(a - mx) + jnp.exp(b - mx) + jnp.exp(c - mx))
            lse_ref[kh] = tot
            wa, wb, wc = jnp.exp(a - tot), jnp.exp(b - tot), jnp.exp(c - tot)
            for g_ in range(2):
                h = 2 * kh + g_
                acc = wa[:, g_:g_ + 1] * o1[h] + wb[:, g_:g_ + 1] * o2[h] + wc[:, g_:g_ + 1] * o3[h]
                pieces.append(acc[:, HEAD_DIM:])
        av = jnp.concatenate(pieces, axis=1).astype(BF16)
        a_ref[...] = av
        for c0, cw in _col_chunks(d):
            cs = slice(c0, c0 + cw)
            o_ref[:, cs] = x_ref[:, cs] + _dot(s, ws_ref[:, cs]) + _dot(av, wa_ref[:, cs])

    row = lambda width: pl.BlockSpec((tm, width), lambda i: (i, 0))
    full = lambda a: pl.BlockSpec(a.shape, lambda i: (0, 0))
    blk = lambda heads: pl.BlockSpec((heads, tm, LANE), lambda i: (0, i, 0))
    return pl.pallas_call(
        kern, name="out_proj", grid=(m // tm,),
        in_specs=[row(d), pl.BlockSpec((tm, d), lambda i: (i, OFF_Z // SSM_INNER)), full(g), full(w_ssm)] + [blk(N_QH)] * 3
        + [blk(N_KV)] * 3 + [full(w_attn), row(d)],
        out_specs=[row(d), row(ATTN_DIM), blk(N_KV), row(d)],
        out_shape=[jax.ShapeDtypeStruct((m, d), BF16), jax.ShapeDtypeStruct((m, ATTN_DIM), BF16),
                   jax.ShapeDtypeStruct((N_KV, m, LANE), F32), jax.ShapeDtypeStruct((m, d), F32)],
        compiler_params=_cparams(("parallel",)))(y, proj, g, w_ssm, *os_, *lses, w_attn, x)


def _mm_nt_rms_bwd(name, a, w, x, g, dres, parts=(), tm=512):
    m, k = a.shape
    n = w.shape[0]
    ns, steps = len(parts), m // tm

    def kern(a_ref, w_ref, x_ref, g_ref, dres_ref, *rest):
        dx_ref, dxb_ref, dg_ref = rest[ns:ns + 3]
        dh = rest[2 * ns + 3]
        if ns:
            start, finish = _scatter_steps(ns, rest[:ns], rest[ns + 3:2 * ns + 3], rest[2 * ns + 4:])
            pl.when(pl.program_id(0) == 0)(start)
            pl.when(pl.program_id(0) == steps - 1)(finish)

        @pl.when(pl.program_id(0) == 0)
        def _():
            dg_ref[...] = jnp.zeros(dg_ref.shape, F32)
        av = a_ref[...]
        for c0, cw in _col_chunks(n):
            dh[:, c0:c0 + cw] = _dot_nt(av, w_ref[c0:c0 + cw, :])
        xv, dhv = x_ref[...], dh[...]
        r = lax.rsqrt(jnp.mean(xv * xv, axis=-1, keepdims=True) + EPS)
        gd = dhv * g_ref[...]
        dx = dres_ref[...] + r * gd - xv * (r * r * r * jnp.mean(xv * gd, axis=-1, keepdims=True))
        dx_ref[...] = dx
        dxb_ref[...] = dx.astype(BF16)
        dg_ref[...] += jnp.sum(dhv * xv * r, axis=0, keepdims=True)

    row = lambda width: pl.BlockSpec((tm, width), lambda i: (i, 0))
    return pl.pallas_call(
        kern, name=name, grid=(steps,),
        in_specs=[row(k), pl.BlockSpec((n, k), lambda i: (0, 0)), row(n), pl.BlockSpec((1, n), lambda i: (0, 0)), row(n)]
        + [ANY_SPEC] * ns,
        out_specs=[row(n), row(n), pl.BlockSpec((1, n), lambda i: (0, 0))] + [ANY_SPEC] * ns,
        out_shape=[jax.ShapeDtypeStruct((m, n), F32), jax.ShapeDtypeStruct((m, n), BF16), jax.ShapeDtypeStruct((1, n), F32)]
        + [jax.ShapeDtypeStruct(s.shape, s.dtype) for s in parts],
        scratch_shapes=[pltpu.VMEM((tm, n), F32)] + (_scatter_sems(ns) if ns else []),
        compiler_params=_cparams(("arbitrary",)))(a, w, x, g, dres, *parts)


def _head_mean_matrix(width):
    i = jnp.arange(width) // HEAD_DIM
    return jnp.where(i[:, None] == i[None, :], 1.0 / HEAD_DIM, 0.0).astype(BF16)


ATT_SPAN = 2048
N_QH = 8


def _lane_lo(rows):
    return lax.broadcasted_iota(jnp.int32, (rows, LANE), 1) < HEAD_DIM


def _swap_halves_lanes(x):
    return pltpu.roll(x, HEAD_DIM, axis=1)


def _head_major_qkv(qn, kn, v, qo_ref, kvo_ref):
    lo = _lane_lo(qn.shape[0])
    for j in range(N_KV):
        blk = qn[:, j * LANE:(j + 1) * LANE]
        qo_ref[2 * j] = jnp.where(lo, blk, 0.0)
        qo_ref[2 * j + 1] = jnp.where(lo, _swap_halves_lanes(blk), 0.0)
    for j in range(2):
        kb, vb = kn[:, j * LANE:(j + 1) * LANE], v[:, j * LANE:(j + 1) * LANE]
        kvo_ref[2 * j] = jnp.where(lo, kb, _swap_halves_lanes(vb))
        kvo_ref[2 * j + 1] = jnp.where(lo, _swap_halves_lanes(kb), vb)


def _in_proj(x, g, w, gq_t, gk_t, tm=512):
    m, k = x.shape
    n = w.shape[1]
    bq, bk = _head_mean_matrix(ATTN_DIM), _head_mean_matrix(KV_DIM)
    scale = HEAD_DIM ** -0.5

    def kern(x_ref, g_ref, w_ref, gq_ref, gk_ref, bq_ref, bk_ref, h_ref, o_ref, qo_ref, kvo_ref):
        xv = x_ref[...]
        h = (xv * lax.rsqrt(jnp.mean(xv * xv, axis=-1, keepdims=True) + EPS) * g_ref[...]).astype(BF16)
        h_ref[...] = h
        for c0, cw in _col_chunks(n):
            o_ref[:, c0:c0 + cw] = _dot(h, w_ref[:, c0:c0 + cw])
        q, kk, v = o_ref[:, OFF_Q:OFF_Q + ATTN_DIM], o_ref[:, OFF_K:OFF_K + KV_DIM], o_ref[:, OFF_V:OFF_V + KV_DIM]
        qn = (q * lax.rsqrt(_dot_split(q * q, bq_ref[...]) + EPS) * gq_ref[...]) * scale
        kn = kk * lax.rsqrt(_dot_split(kk * kk, bk_ref[...]) + EPS) * gk_ref[...]
        _head_major_qkv(qn, kn, v, qo_ref, kvo_ref)

    row = lambda width: pl.BlockSpec((tm, width), lambda i: (i, 0))
    full = lambda a: pl.BlockSpec(a.shape, lambda i: (0, 0))
    blk = lambda heads: pl.BlockSpec((heads, tm, LANE), lambda i: (0, i, 0))
    return pl.pallas_call(
        kern, name="in_proj", grid=(m // tm,),
        in_specs=[row(k), full(g), full(w), full(gq_t), full(gk_t), full(bq), full(bk)],
        out_specs=[row(k), row(n), blk(N_QH), blk(N_KV)],
        out_shape=[jax.ShapeDtypeStruct((m, k), BF16), jax.ShapeDtypeStruct((m, n), F32),
                   jax.ShapeDtypeStruct((N_QH, m, LANE), F32), jax.ShapeDtypeStruct((N_KV, m, LANE), F32)],
        compiler_params=_cparams(("parallel",)))(x, g, w, gq_t, gk_t, bq, bk)


def _d_mix(dx1b, w_cat, y, proj, g, attn_out, tm=512):
    m, k = dx1b.shape
    n = w_cat.shape[0]

    def kern(a_ref, w_ref, y_ref, z_ref, g_ref, o_ref, dy_ref, dz_ref, dg_ref, dot_ref, d_ref, dmix):
        @pl.when(pl.program_id(0) == 0)
        def _():
            dg_ref[...] = jnp.zeros(dg_ref.shape, F32)
        av = a_ref[...]
        for c0, cw in _col_chunks(n):
            dmix[:, c0:c0 + cw] = _dot_nt(av, w_ref[c0:c0 + cw, :])
        z, yv, dout = z_ref[...], y_ref[...], dmix[:, :SSM_INNER]
        sg = _sigmoid(z)
        gz = z * sg
        yz = yv * gz
        r = lax.rsqrt(jnp.mean(yz * yz, axis=-1, keepdims=True) + EPS)
        gd = dout * g_ref[...]
        dyz = r * gd - yz * (r * r * r * jnp.mean(yz * gd, axis=-1, keepdims=True))
        dy_ref[...] = dyz * gz
        dz_ref[...] = (dyz * yv * (sg * (1.0 + z * (1.0 - sg)))).astype(BF16)
        dg_ref[...] += jnp.sum(dout * yz * r, axis=0, keepdims=True)
        do = dmix[:, SSM_INNER:]
        prod = do * o_ref[...].astype(F32)
        lo = _lane_lo(tm)
        lane = lax.broadcasted_iota(jnp.int32, (tm, LANE), 1)
        for kh in range(N_KV):
            blk, pb = do[:, kh * LANE:(kh + 1) * LANE], prod[:, kh * LANE:(kh + 1) * LANE]
            dot_ref[2 * kh] = jnp.where(lo, 0.0, _swap_halves_lanes(blk))
            dot_ref[2 * kh + 1] = jnp.where(lo, 0.0, blk)
            s_lo = jnp.sum(jnp.where(lo, pb, 0.0), axis=1, keepdims=True)
            s_hi = jnp.sum(pb, axis=1, keepdims=True) - s_lo
            d_ref[kh] = jnp.where(lane == 0, s_lo, jnp.where(lane == 1, s_hi, 0.0))

    row = lambda width: pl.BlockSpec((tm, width), lambda i: (i, 0))
    full = lambda a: pl.BlockSpec(a.shape, lambda i: (0, 0))
    blk = lambda heads: pl.BlockSpec((heads, tm, LANE), lambda i: (0, i, 0))
    return pl.pallas_call(
        kern, name="d_mix", grid=(m // tm,),
        in_specs=[row(k), full(w_cat), row(SSM_INNER), pl.BlockSpec((tm, SSM_INNER), lambda i: (i, OFF_Z // SSM_INNER)), full(g),
                  row(ATTN_DIM)],
        out_specs=[row(SSM_INNER), row(SSM_INNER), pl.BlockSpec((1, SSM_INNER), lambda i: (0, 0)), blk(N_QH), blk(N_KV)],
        out_shape=[jax.ShapeDtypeStruct((m, SSM_INNER), F32), jax.ShapeDtypeStruct((m, SSM_INNER), BF16),
                   jax.ShapeDtypeStruct((1, SSM_INNER), F32), jax.ShapeDtypeStruct((N_QH, m, LANE), F32),
                   jax.ShapeDtypeStruct((N_KV, m, LANE), F32)],
        scratch_shapes=[pltpu.VMEM((tm, n), F32)], compiler_params=_cparams(("arbitrary",)))(dx1b, w_cat, y, proj, g, attn_out)


def _qknorm_bwd2(proj, gq_t, gk_t, dqs, dkvs, tm=256):
    t = proj.shape[0]
    bq, bk = _head_mean_matrix(ATTN_DIM), _head_mean_matrix(KV_DIM)
    scale = HEAD_DIM ** -0.5

    def kern(q_ref, k_ref, gq_ref, gk_ref, bq_ref, bk_ref, a1, a2, a3, b1, b2, b3, dq_ref, dk_ref, dv_ref, dgq_ref, dgk_ref):
        @pl.when(pl.program_id(0) == 0)
        def _():
            dgq_ref[...] = jnp.zeros(dgq_ref.shape, F32)
            dgk_ref[...] = jnp.zeros(dgk_ref.shape, F32)
        lo = _lane_lo(tm)
        sq = [a1[h] + a2[h] + a3[h] for h in range(N_QH)]
        skv = [b1[h] + b2[h] + b3[h] for h in range(N_KV)]
        dqn = jnp.concatenate([jnp.where(lo, sq[2 * j], _swap_halves_lanes(sq[2 * j + 1])) for j in range(N_KV)], axis=1) * scale
        dkn = jnp.concatenate([jnp.where(lo, skv[2 * j], _swap_halves_lanes(skv[2 * j + 1])) for j in range(2)], axis=1)
        dv = jnp.concatenate([jnp.where(lo, _swap_halves_lanes(skv[2 * j]), skv[2 * j + 1]) for j in range(2)], axis=1)
        q, k = q_ref[...], k_ref[...]
        rq = lax.rsqrt(_dot_split(q * q, bq_ref[...]) + EPS)
        rk = lax.rsqrt(_dot_split(k * k, bk_ref[...]) + EPS)
        gdq, gdk = dqn * gq_ref[...], dkn * gk_ref[...]
        dq_ref[...] = (rq * gdq - q * (rq * rq * rq * _dot_split(q * gdq, bq_ref[...]))).astype(BF16)
        dk_ref[...] = (rk * gdk - k * (rk * rk * rk * _dot_split(k * gdk, bk_ref[...]))).astype(BF16)
        dv_ref[...] = dv.astype(BF16)
        dgq_ref[...] += jnp.sum(dqn * q * rq, axis=0, keepdims=True)
        dgk_ref[...] += jnp.sum(dkn * k * rk, axis=0, keepdims=True)

    col = lambda w, idx: pl.BlockSpec((tm, w), functools.partial(lambda i, idx: (i, idx), idx=idx))
    par = lambda a: pl.BlockSpec(a.shape, lambda i: (0, 0))
    blk = lambda n: pl.BlockSpec((n, tm, LANE), lambda i: (0, i, 0))
    row = lambda w: pl.BlockSpec((tm, w), lambda i: (i, 0))
    acc = lambda w: pl.BlockSpec((1, w), lambda i: (0, 0))
    return pl.pallas_call(
        kern, name="qknorm_bwd", grid=(t // tm,),
        in_specs=[col(ATTN_DIM, OFF_Q // ATTN_DIM), col(KV_DIM, OFF_K // KV_DIM), par(gq_t), par(gk_t), par(bq), par(bk)]
        + [blk(N_QH)] * 3 + [blk(N_KV)] * 3,
        out_specs=[row(ATTN_DIM), row(KV_DIM), row(KV_DIM), acc(ATTN_DIM), acc(KV_DIM)],
        out_shape=[jax.ShapeDtypeStruct((t, ATTN_DIM), BF16), jax.ShapeDtypeStruct((t, KV_DIM), BF16),
                   jax.ShapeDtypeStruct((t, KV_DIM), BF16), jax.ShapeDtypeStruct((1, ATTN_DIM), F32),
                   jax.ShapeDtypeStruct((1, KV_DIM), F32)],
        compiler_params=_cparams(("arbitrary",)))(proj, proj, gq_t, gk_t, bq, bk, *dqs, *dkvs)


def _att_rows(b, r, dil):
    if dil == 1:
        return pl.ds(b * CHUNK, CHUNK)
    return pl.ds(b * CHUNK * dil + r, CHUNK, stride=dil)


def _for_residues(dil, unit):
    for r in range(dil):
        unit(r, 0)


def _band_qk(first):
    ri = lax.broadcasted_iota(jnp.int32, (CHUNK, 2 * CHUNK), 0)
    cj = lax.broadcasted_iota(jnp.int32, (CHUNK, 2 * CHUNK), 1)
    band = (cj - ri >= 0) & (cj - ri <= CHUNK)
    return band if first is None else band & (jnp.logical_not(first) | (cj >= CHUNK))


def _band_kq(last):
    rj = lax.broadcasted_iota(jnp.int32, (CHUNK, 2 * CHUNK), 0)
    ci = lax.broadcasted_iota(jnp.int32, (CHUNK, 2 * CHUNK), 1)
    band = (ci - rj >= 0) & (ci - rj <= CHUNK)
    return band if last is None else band & (jnp.logical_not(last) | (ci < CHUNK))


def _att_specs(t, dil):
    sub = CHUNK * dil
    nb, last = ATT_SPAN // sub, t // sub - 1
    cur = lambda heads: pl.BlockSpec((heads, ATT_SPAN, LANE), lambda kh, n: (kh, n, 0))
    prev = lambda heads: pl.BlockSpec((heads, sub, LANE), lambda kh, n: (kh, jnp.maximum(n * nb - 1, 0), 0))
    nxt = lambda heads: pl.BlockSpec((heads, sub, LANE), lambda kh, n: (kh, jnp.minimum((n + 1) * nb, last), 0))
    return sub, nb, cur, prev, nxt


def _attn_fwd2(q, kv, dil):
    t = q.shape[1]
    sub, nb, cur, prev, _ = _att_specs(t, dil)

    def kern(q_ref, kvp_ref, kvc_ref, o_ref, lse_ref):
        n = pl.program_id(1)
        lane = lax.broadcasted_iota(jnp.int32, (CHUNK, LANE), 1)
        for b in range(nb):
            band = _band_qk((n == 0) if b == 0 else None)
            mask = jnp.concatenate([band, band], axis=0)

            def unit(r, carry, b=b, mask=mask):
                rows = _att_rows(b, r, dil)
                kvp = kvc_ref[_att_rows(b - 1, r, dil), :] if b > 0 else kvp_ref[_att_rows(0, r, dil), :]
                kvcat = jnp.concatenate([kvp, kvc_ref[rows, :]], axis=0).astype(BF16)
                qs = jnp.concatenate([q_ref.at[0][rows, :], q_ref.at[1][rows, :]], axis=0).astype(BF16)
                s = jnp.where(mask, _dot_nt(qs, kvcat), NEG)
                m = jnp.max(s, axis=1, keepdims=True)
                p = jnp.exp(s - m)
                l = jnp.sum(p, axis=1, keepdims=True)
                o = _dot(p.astype(BF16), kvcat) * (1.0 / l)
                o_ref.at[0][rows, :] = o[:CHUNK]
                o_ref.at[1][rows, :] = o[CHUNK:]
                lse = m + jnp.log(l)
                lse_ref[rows, :] = jnp.where(lane == 0, lse[:CHUNK], jnp.where(lane == 1, lse[CHUNK:], 0.0))
                return carry
            _for_residues(dil, unit)

    return pl.pallas_call(
        kern, name=f"attn_fwd_d{dil}", grid=(N_KV, t // ATT_SPAN), in_specs=[cur(2), prev(None), cur(None)],
        out_specs=[cur(2), cur(None)],
        out_shape=[jax.ShapeDtypeStruct((N_QH, t, LANE), F32), jax.ShapeDtypeStruct((N_KV, t, LANE), F32)],
        compiler_params=_cparams(("parallel", "parallel")))(q, kv, kv)


def _attn_dq2(q, kv, dot, lse, dsum, dil):
    t = q.shape[1]
    sub, nb, cur, prev, _ = _att_specs(t, dil)

    def kern(q_ref, kvp_ref, kvc_ref, do_ref, lse_ref, d_ref, dq_ref):
        n = pl.program_id(1)
        for b in range(nb):
            band = _band_qk((n == 0) if b == 0 else None)
            mask = jnp.concatenate([band, band], axis=0)

            def unit(r, carry, b=b, mask=mask):
                rows = _att_rows(b, r, dil)
                kvp = kvc_ref[_att_rows(b - 1, r, dil), :] if b > 0 else kvp_ref[_att_rows(0, r, dil), :]
                kvcat = jnp.concatenate([kvp, kvc_ref[rows, :]], axis=0).astype(BF16)
                lse_t, d_t = lse_ref[rows, :], d_ref[rows, :]
                qs = jnp.concatenate([q_ref.at[0][rows, :], q_ref.at[1][rows, :]], axis=0).astype(BF16)
                dos = jnp.concatenate([do_ref.at[0][rows, :], do_ref.at[1][rows, :]], axis=0).astype(BF16)
                lse2 = jnp.concatenate([lse_t[:, 0:1], lse_t[:, 1:2]], axis=0)
                d2 = jnp.concatenate([d_t[:, 0:1], d_t[:, 1:2]], axis=0)
                p = jnp.exp(jnp.where(mask, _dot_nt(qs, kvcat), NEG) - lse2)
                ds = p * (_dot_nt(dos, kvcat) - d2)
                dq = _dot(ds.astype(BF16), kvcat)
                dq_ref.at[0][rows, :] = dq[:CHUNK]
                dq_ref.at[1][rows, :] = dq[CHUNK:]
                return carry
            _for_residues(dil, unit)

    return pl.pallas_call(
        kern, name=f"attn_dq_d{dil}", grid=(N_KV, t // ATT_SPAN),
        in_specs=[cur(2), prev(None), cur(None), cur(2), cur(None), cur(None)], out_specs=cur(2),
        out_shape=jax.ShapeDtypeStruct((N_QH, t, LANE), F32),
        compiler_params=_cparams(("parallel", "parallel")))(q, kv, kv, dot, lse, dsum)


def _attn_dkv2(q, kv, dot, lse, dsum, dil):
    t = q.shape[1]
    sub, nb, cur, _, nxt = _att_specs(t, dil)
    nsteps = t // ATT_SPAN

    def kern(kv_ref, qc_ref, qn_ref, doc_ref, don_ref, lc_ref, ln_ref, dc_ref, dn_ref, dkv_ref):
        n = pl.program_id(1)
        for b in range(nb):
            inside = b < nb - 1
            mask = _band_kq(None if inside else (n == nsteps - 1))

            def unit(r, carry, b=b, inside=inside, mask=mask):
                rows = _att_rows(b, r, dil)
                nrows = _att_rows(b + 1, r, dil) if inside else _att_rows(0, r, dil)
                kvb = kv_ref[rows, :].astype(BF16)
                follow = lambda cref, nref: (cref if inside else nref)[nrows, :]
                lse_t = jnp.concatenate([lc_ref[rows, :].T, follow(lc_ref, ln_ref).T], axis=1)
                d_t = jnp.concatenate([dc_ref[rows, :].T, follow(dc_ref, dn_ref).T], axis=1)
                qdo = jnp.concatenate([qc_ref.at[0][rows, :], follow(qc_ref.at[0], qn_ref.at[0]),
                                       qc_ref.at[1][rows, :], follow(qc_ref.at[1], qn_ref.at[1]),
                                       doc_ref.at[0][rows, :], follow(doc_ref.at[0], don_ref.at[0]),
                                       doc_ref.at[1][rows, :], follow(doc_ref.at[1], don_ref.at[1])], axis=0).astype(BF16)
                both = _dot_nt(kvb, qdo)
                half = 4 * CHUNK
                mask2 = jnp.concatenate([mask, mask], axis=1)
                lse2 = jnp.concatenate([lse_t[0:1, :], lse_t[1:2, :]], axis=1)
                d2 = jnp.concatenate([d_t[0:1, :], d_t[1:2, :]], axis=1)
                pt = jnp.exp(jnp.where(mask2, both[:, :half], NEG) - lse2)
                dst = pt * (both[:, half:] - d2)
                dkv_ref[rows, :] = _dot(jnp.concatenate([dst, pt], axis=1).astype(BF16), qdo)
                return carry
            _for_residues(dil, unit)

    return pl.pallas_call(
        kern, name=f"attn_dkv_d{dil}", grid=(N_KV, nsteps),
        in_specs=[cur(None), cur(2), nxt(2), cur(2), nxt(2), cur(None), nxt(None), cur(None), nxt(None)], out_specs=cur(None),
        out_shape=jax.ShapeDtypeStruct((N_KV, t, LANE), F32),
        compiler_params=_cparams(("parallel", "parallel")))(kv, q, q, dot, dot, lse, lse, dsum, dsum)


def _attn_fwd_all(q, kv):
    t = q.shape[1]
    specs = [_att_specs(t, d) for d in DILATIONS]
    cur = specs[0][2]
    nd = len(DILATIONS)

    def kern(q_ref, kvc_ref, *rest):
        n = pl.program_id(1)
        lane = lax.broadcasted_iota(jnp.int32, (CHUNK, LANE), 1)
        for di, dil in enumerate(DILATIONS):
            kvp_ref, o_ref, lse_ref = rest[di], rest[nd + 2 * di], rest[nd + 2 * di + 1]
            for b in range(specs[di][1]):
                band = _band_qk((n == 0) if b == 0 else None)
                mask = jnp.concatenate([band, band], axis=0)
                for r in range(dil):
                    rows = _att_rows(b, r, dil)
                    kvp = kvc_ref[_att_rows(b - 1, r, dil), :] if b > 0 else kvp_ref[_att_rows(0, r, dil), :]
                    kvcat = jnp.concatenate([kvp, kvc_ref[rows, :]], axis=0).astype(BF16)
                    qs = jnp.concatenate([q_ref.at[0][rows, :], q_ref.at[1][rows, :]], axis=0).astype(BF16)
                    s = jnp.where(mask, _dot_nt(qs, kvcat), NEG)
                    m = jnp.max(s, axis=1, keepdims=True)
                    p = jnp.exp(s - m)
                    l = jnp.sum(p, axis=1, keepdims=True)
                    o = _dot(p.astype(BF16), kvcat) * (1.0 / l)
                    o_ref.at[0][rows, :] = o[:CHUNK]
                    o_ref.at[1][rows, :] = o[CHUNK:]
                    lse = m + jnp.log(l)
                    lse_ref[rows, :] = jnp.where(lane == 0, lse[:CHUNK], jnp.where(lane == 1, lse[CHUNK:], 0.0))

    outs = pl.pallas_call(
        kern, name="attn_fwd", grid=(N_KV, t // ATT_SPAN), in_specs=[cur(2), cur(None)] + [sp[3](None) for sp in specs],
        out_specs=[cur(2), cur(None)] * nd,
        out_shape=[jax.ShapeDtypeStruct((N_QH, t, LANE), F32), jax.ShapeDtypeStruct((N_KV, t, LANE), F32)] * nd,
        compiler_params=_cparams(("parallel", "parallel")))(q, kv, *([kv] * nd))
    return [(outs[2 * i], outs[2 * i + 1]) for i in range(nd)]


def _attn_dq_all(q, kv, dot, lse, dsum):
    t = q.shape[1]
    specs = [_att_specs(t, d) for d in DILATIONS]
    cur = specs[0][2]
    nd = len(DILATIONS)

    def kern(q_ref, kvc_ref, do_ref, lse_ref, d_ref, *rest):
        n = pl.program_id(1)
        for di, dil in enumerate(DILATIONS):
            kvp_ref, dq_ref = rest[di], rest[nd + di]
            for b in range(specs[di][1]):
                band = _band_qk((n == 0) if b == 0 else None)
                mask = jnp.concatenate([band, band], axis=0)
                for r in range(dil):
                    rows = _att_rows(b, r, dil)
                    kvp = kvc_ref[_att_rows(b - 1, r, dil), :] if b > 0 else kvp_ref[_att_rows(0, r, dil), :]
                    kvcat = jnp.concatenate([kvp, kvc_ref[rows, :]], axis=0).astype(BF16)
                    lse_t, d_t = lse_ref[rows, :], d_ref[rows, :]
                    qs = jnp.concatenate([q_ref.at[0][rows, :], q_ref.at[1][rows, :]], axis=0).astype(BF16)
                    dos = jnp.concatenate([do_ref.at[0][rows, :], do_ref.at[1][rows, :]], axis=0).astype(BF16)
                    lse2 = jnp.concatenate([lse_t[:, 0:1], lse_t[:, 1:2]], axis=0)
                    d2 = jnp.concatenate([d_t[:, 0:1], d_t[:, 1:2]], axis=0)
                    p = jnp.exp(jnp.where(mask, _dot_nt(qs, kvcat), NEG) - lse2)
                    ds = p * (_dot_nt(dos, kvcat) - d2)
                    dq = _dot(ds.astype(BF16), kvcat)
                    dq_ref.at[0][rows, :] = dq[:CHUNK]
                    dq_ref.at[1][rows, :] = dq[CHUNK:]

    return pl.pallas_call(
        kern, name="attn_dq", grid=(N_KV, t // ATT_SPAN),
        in_specs=[cur(2), cur(None), cur(2), cur(None), cur(None)] + [sp[3](None) for sp in specs], out_specs=[cur(2)] * nd,
        out_shape=[jax.ShapeDtypeStruct((N_QH, t, LANE), F32)] * nd,
        compiler_params=_cparams(("parallel", "parallel")))(q, kv, dot, lse, dsum, *([kv] * nd))


def _attn_dkv_all(q, kv, dot, lse, dsum):
    t = q.shape[1]
    specs = [_att_specs(t, d) for d in DILATIONS]
    cur = specs[0][2]
    nd = len(DILATIONS)
    nsteps = t // ATT_SPAN

    def kern(kv_ref, qc_ref, doc_ref, lc_ref, dc_ref, *rest):
        n = pl.program_id(1)
        for di, dil in enumerate(DILATIONS):
            qn_ref, don_ref, ln_ref, dn_ref = rest[4 * di:4 * di + 4]
            dkv_ref = rest[4 * nd + di]
            nb = specs[di][1]
            for b in range(nb):
                inside = b < nb - 1
                mask = _band_kq(None if inside else (n == nsteps - 1))
                mask2 = jnp.concatenate([mask, mask], axis=1)
                for r in range(dil):
                    rows = _att_rows(b, r, dil)
                    nrows = _att_rows(b + 1, r, dil) if inside else _att_rows(0, r, dil)
                    kvb = kv_ref[rows, :].astype(BF16)
                    follow = lambda cref, nref: (cref if inside else nref)[nrows, :]
                    lse_t = jnp.concatenate([lc_ref[rows, :].T, follow(lc_ref, ln_ref).T], axis=1)
                    d_t = jnp.concatenate([dc_ref[rows, :].T, follow(dc_ref, dn_ref).T], axis=1)
                    qdo = jnp.concatenate([qc_ref.at[0][rows, :], follow(qc_ref.at[0], qn_ref.at[0]),
                                           qc_ref.at[1][rows, :], follow(qc_ref.at[1], qn_ref.at[1]),
                                           doc_ref.at[0][rows, :], follow(doc_ref.at[0], don_ref.at[0]),
                                           doc_ref.at[1][rows, :], follow(doc_ref.at[1], don_ref.at[1])], axis=0).astype(BF16)
                    both = _dot_nt(kvb, qdo)
                    half = 4 * CHUNK
                    lse2 = jnp.concatenate([lse_t[0:1, :], lse_t[1:2, :]], axis=1)
                    d2 = jnp.concatenate([d_t[0:1, :], d_t[1:2, :]], axis=1)
                    pt = jnp.exp(jnp.where(mask2, both[:, :half], NEG) - lse2)
                    dst = pt * (both[:, half:] - d2)
                    dkv_ref[rows, :] = _dot(jnp.concatenate([dst, pt], axis=1).astype(BF16), qdo)

    in_specs = [cur(None), cur(2), cur(2), cur(None), cur(None)]
    args = [kv, q, dot, lse, dsum]
    for sp in specs:
        in_specs += [sp[4](2), sp[4](2), sp[4](None), sp[4](None)]
        args += [q, dot, lse, dsum]
    return pl.pallas_call(
        kern, name="attn_dkv", grid=(N_KV, nsteps), in_specs=in_specs, out_specs=[cur(None)] * nd,
        out_shape=[jax.ShapeDtypeStruct((N_KV, t, LANE), F32)] * nd,
        compiler_params=_cparams(("parallel", "parallel")))(*args)


HALO = 8
SSM_CONV_TM, SSM_CONV_W = 512, 512
FFN_CONV_TM, FFN_CONV_W = 256, 1408


def _halo_specs(tm, width, t_rows, col_off=0, lead=None):
    per, last = tm // HALO, t_rows // HALO - 1
    row_maps = (lambda i: i, lambda i: jnp.maximum(i * per - 1, 0), lambda i: jnp.minimum((i + 1) * per, last))
    specs = []
    for rows, rm in zip((tm, HALO, HALO), row_maps):
        if lead is None:
            specs.append(pl.BlockSpec((rows, width), functools.partial(lambda c, i, rm: (rm(i), c + col_off), rm=rm)))
        else:
            specs.append(pl.BlockSpec((None, rows, width), functools.partial(lambda c, i, rm: (lead, rm(i), c + col_off), rm=rm)))
    return specs


def _fill_ext(buf, tile_ref, before_ref, after_ref, i, nt):
    tm = tile_ref.shape[0]
    buf[0:HALO, :] = jnp.where(i > 0, before_ref[...].astype(F32), 0.0)
    buf[HALO:HALO + tm, :] = tile_ref[...].astype(F32)
    if after_ref is not None:
        buf[HALO + tm:, :] = jnp.where(i < nt - 1, after_ref[...].astype(F32), 0.0)


CONV_RB, CONV_CW = 16, 256


def _lane_chunks(width):
    return [slice(c0, min(c0 + CONV_CW, width)) for c0 in range(0, width, CONV_CW)]


def _shifted(buf, taps, r0, rows, cs):
    return [buf[pl.ds(HALO - (taps - 1) + k + r0, rows), cs] for k in range(taps)]


def _taps_fwd(xs, w, b):
    acc = b
    for k, xk in enumerate(xs):
        acc = acc + w[k:k + 1, :] * xk
    return acc


def _taps_bwd(bufd, w, taps, r0, rows, cs):
    acc = None
    for k in range(taps):
        term = w[k:k + 1, :] * bufd[pl.ds(r0 + (taps - 1) - k, rows), cs]
        acc = term if acc is None else acc + term
    return acc


def _fold8(z):
    return z[:HALO] + z[HALO:] if z.shape[0] == 2 * HALO else z


def _silu_grad(pre):
    sg = _sigmoid(pre)
    return sg * (1.0 + pre * (1.0 - sg))


def _ssm_conv_fwd(proj, w, b):
    t = proj.shape[0]
    tm, wd = min(SSM_CONV_TM, t), SSM_CONV_W
    nt, taps = t // tm, w.shape[0]

    def kern(x_ref, xb_ref, w_ref, b_ref, o_ref, buf):
        _fill_ext(buf, x_ref, xb_ref, None, pl.program_id(1), nt)
        for cs in _lane_chunks(wd):
            wv, bv = w_ref[:, cs], b_ref[:, cs]
            for r0 in range(0, tm, CONV_RB):
                pre = _taps_fwd(_shifted(buf, taps, r0, CONV_RB, cs), wv, bv)
                o_ref[r0:r0 + CONV_RB, cs] = pre * _sigmoid(pre)

    tile, before, _ = _halo_specs(tm, wd, t)
    par = lambda rows: pl.BlockSpec((rows, wd), lambda c, i: (0, c))
    return pl.pallas_call(
        kern, name="ssm_conv_fwd", grid=(XBC_DIM // wd, nt), in_specs=[tile, before, par(taps), par(1)],
        out_specs=pl.BlockSpec((tm, wd), lambda c, i: (i, c)), out_shape=jax.ShapeDtypeStruct((t, XBC_DIM), F32),
        scratch_shapes=[pltpu.VMEM((tm + HALO, wd), F32)],
        compiler_params=_cparams(("parallel", "parallel")))(proj, proj, w, b)


def _ssm_conv_bwd(proj, w, b, dact, parts):
    t = proj.shape[0]
    tm, wd = min(SSM_CONV_TM, t), SSM_CONV_W
    nt, taps, ncol, ns = t // tm, w.shape[0], XBC_DIM // SSM_CONV_W, len(parts)

    def kern(x_ref, xb_ref, xa_ref, d_ref, dn_ref, w_ref, b_ref, *rest):
        dx_ref, gw_ref, gb_ref = rest[ns:ns + 3]
        buf, bufd = rest[2 * ns + 3:2 * ns + 5]
        i = pl.program_id(1)
        if ns:
            start, finish = _scatter_steps(ns, rest[:ns], rest[ns + 3:2 * ns + 3], rest[2 * ns + 5:])
            pl.when((pl.program_id(0) == 0) & (i == 0))(start)
            pl.when((pl.program_id(0) == ncol - 1) & (i == nt - 1))(finish)
        _fill_ext(buf, x_ref, xb_ref, xa_ref, i, nt)

        @pl.when(i == 0)
        def _():
            gw_ref[...] = jnp.zeros(gw_ref.shape, F32)
            gb_ref[...] = jnp.zeros(gb_ref.shape, F32)
        for cs in _lane_chunks(wd):
            wv, bv = w_ref[:, cs], b_ref[:, cs]
            acc = [jnp.zeros((HALO, cs.stop - cs.start), F32) for _ in range(taps + 1)]
            for r0 in list(range(0, tm, CONV_RB)) + [tm]:
                inside = r0 < tm
                rows = CONV_RB if inside else HALO
                xs = _shifted(buf, taps, r0, rows, cs)
                d = d_ref[r0:r0 + rows, cs] if inside else jnp.where(i < nt - 1, dn_ref[:, cs], 0.0)
                dpre = d * _silu_grad(_taps_fwd(xs, wv, bv))
                bufd[r0:r0 + rows, cs] = dpre
                if inside:
                    acc[taps] = acc[taps] + _fold8(dpre)
                    for k in range(taps):
                        acc[k] = acc[k] + _fold8(dpre * xs[k])
            gb_ref[:, cs] += jnp.sum(acc[taps], axis=0, keepdims=True)
            for k in range(taps):
                gw_ref[k:k + 1, cs] += jnp.sum(acc[k], axis=0, keepdims=True)
            for r0 in range(0, tm, CONV_RB):
                dx_ref[r0:r0 + CONV_RB, cs] = _taps_bwd(bufd, wv, taps, r0, CONV_RB, cs).astype(BF16)

    xt, xb, xa = _halo_specs(tm, wd, t)
    dt_, _, dn = _halo_specs(tm, wd, t)
    par = lambda rows: pl.BlockSpec((rows, wd), lambda c, i: (0, c))
    return pl.pallas_call(
        kern, name="ssm_conv_bwd", grid=(ncol, nt), in_specs=[xt, xb, xa, dt_, dn, par(taps), par(1)] + [ANY_SPEC] * ns,
        out_specs=[pl.BlockSpec((tm, wd), lambda c, i: (i, c)), par(taps), par(1)] + [ANY_SPEC] * ns,
        out_shape=[jax.ShapeDtypeStruct((t, XBC_DIM), BF16), jax.ShapeDtypeStruct((taps, XBC_DIM), F32),
                   jax.ShapeDtypeStruct((1, XBC_DIM), F32)] + [jax.ShapeDtypeStruct(s.shape, s.dtype) for s in parts],
        scratch_shapes=[pltpu.VMEM((tm + 2 * HALO, wd), F32), pltpu.VMEM((tm + HALO, wd), F32)] + (_scatter_sems(ns) if ns else []),
        compiler_params=_cparams(("arbitrary", "arbitrary")))(proj, proj, proj, dact, dact, w, b, *parts)


def _ffn_act_down(u, w, b, w_down, x1):
    t = u.shape[1]
    tm, wd = min(FFN_CONV_TM, t), D_FF
    nt, taps = t // tm, w.shape[0]

    def kern(g_ref, gb_ref, v_ref, vb_ref, wg_ref, wv_ref, bg_ref, bv_ref, wd_ref, x1_ref, a_ref, x2_ref, bufg, bufv):
        i = pl.program_id(1)
        _fill_ext(bufg, g_ref, gb_ref, None, i, nt)
        _fill_ext(bufv, v_ref, vb_ref, None, i, nt)
        acc = x1_ref[...]
        for cs in _lane_chunks(wd):
            wg, wv, bg, bv = wg_ref[:, cs], wv_ref[:, cs], bg_ref[:, cs], bv_ref[:, cs]
            for r0 in range(0, tm, CONV_RB):
                g = _taps_fwd(_shifted(bufg, taps, r0, CONV_RB, cs), wg, bg)
                v = _taps_fwd(_shifted(bufv, taps, r0, CONV_RB, cs), wv, bv)
                a_ref[r0:r0 + CONV_RB, cs] = (g * _sigmoid(g) * v).astype(BF16)
            acc = acc + _dot(a_ref[:, cs], wd_ref[cs, :])
        x2_ref[...] = acc

    gt, gbf, _ = _halo_specs(tm, wd, t, lead=0)
    vt, vbf, _ = _halo_specs(tm, wd, t, lead=1)
    par = lambda rows, off: pl.BlockSpec((rows, wd), functools.partial(lambda c, i, off: (0, c + off), off=off))
    row = lambda width: pl.BlockSpec((tm, width), lambda c, i: (i, 0))
    return pl.pallas_call(
        kern, name="ffn_act_down", grid=(1, nt),
        in_specs=[gt, gbf, vt, vbf, par(taps, 0), par(taps, 1), par(1, 0), par(1, 1),
                  pl.BlockSpec(w_down.shape, lambda c, i: (0, 0)), row(D_MODEL)],
        out_specs=[row(wd), row(D_MODEL)],
        out_shape=[jax.ShapeDtypeStruct((t, D_FF), BF16), jax.ShapeDtypeStruct((t, D_MODEL), F32)],
        scratch_shapes=[pltpu.VMEM((tm + HALO, wd), F32)] * 2,
        compiler_params=_cparams(("parallel", "parallel")))(u, u, u, u, w, w, b, b, w_down, x1)


def _ffn_act_bwd(u, w, b, da):
    t = u.shape[1]
    tm, wd = min(FFN_CONV_TM, t), FFN_CONV_W
    nt, taps, nc = t // tm, w.shape[0], D_FF // FFN_CONV_W

    def kern(g_ref, gb_ref, ga_ref, v_ref, vb_ref, va_ref, d_ref, dn_ref, wg_ref, wv_ref, bg_ref, bv_ref,
             du_ref, gwg_ref, gwv_ref, gbg_ref, gbv_ref, bufg, bufv, bufdg, bufdv):
        i = pl.program_id(1)
        _fill_ext(bufg, g_ref, gb_ref, ga_ref, i, nt)
        _fill_ext(bufv, v_ref, vb_ref, va_ref, i, nt)

        @pl.when(i == 0)
        def _():
            for r in (gwg_ref, gwv_ref, gbg_ref, gbv_ref):
                r[...] = jnp.zeros(r.shape, F32)
        for cs in _lane_chunks(wd):
            wg, wv, bg, bv = wg_ref[:, cs], wv_ref[:, cs], bg_ref[:, cs], bv_ref[:, cs]
            zero = jnp.zeros((HALO, cs.stop - cs.start), F32)
            accg, accv = [zero] * (taps + 1), [zero] * (taps + 1)
            for r0 in list(range(0, tm, CONV_RB)) + [tm]:
                inside = r0 < tm
                rows = CONV_RB if inside else HALO
                xg, xv = _shifted(bufg, taps, r0, rows, cs), _shifted(bufv, taps, r0, rows, cs)
                g, v = _taps_fwd(xg, wg, bg), _taps_fwd(xv, wv, bv)
                dav = d_ref[r0:r0 + rows, cs] if inside else jnp.where(i < nt - 1, dn_ref[:, cs], 0.0)
                sg = _sigmoid(g)
                dg = dav * v * (sg * (1.0 + g * (1.0 - sg)))
                dv = dav * (g * sg)
                bufdg[r0:r0 + rows, cs] = dg
                bufdv[r0:r0 + rows, cs] = dv
                if inside:
                    accg[taps], accv[taps] = accg[taps] + _fold8(dg), accv[taps] + _fold8(dv)
                    for k in range(taps):
                        accg[k], accv[k] = accg[k] + _fold8(dg * xg[k]), accv[k] + _fold8(dv * xv[k])
            gbg_ref[:, cs] += jnp.sum(accg[taps], axis=0, keepdims=True)
            gbv_ref[:, cs] += jnp.sum(accv[taps], axis=0, keepdims=True)
            for k in range(taps):
                gwg_ref[k:k + 1, cs] += jnp.sum(accg[k], axis=0, keepdims=True)
                gwv_ref[k:k + 1, cs] += jnp.sum(accv[k], axis=0, keepdims=True)
            for r0 in range(0, tm, CONV_RB):
                du_ref[0, r0:r0 + CONV_RB, cs] = _taps_bwd(bufdg, wg, taps, r0, CONV_RB, cs).astype(BF16)
                du_ref[1, r0:r0 + CONV_RB, cs] = _taps_bwd(bufdv, wv, taps, r0, CONV_RB, cs).astype(BF16)

    gt, gbf, gaf = _halo_specs(tm, wd, t, lead=0)
    vt, vbf, vaf = _halo_specs(tm, wd, t, lead=1)
    dt_, _, dn = _halo_specs(tm, wd, t)
    par = lambda rows, off: pl.BlockSpec((rows, wd), functools.partial(lambda c, i, off: (0, c + off), off=off))
    return pl.pallas_call(
        kern, name="ffn_act_bwd", grid=(nc, nt),
        in_specs=[gt, gbf, gaf, vt, vbf, vaf, dt_, dn, par(taps, 0), par(taps, nc), par(1, 0), par(1, nc)],
        out_specs=[pl.BlockSpec((2, tm, wd), lambda c, i: (0, i, c)), par(taps, 0), par(taps, 0), par(1, 0), par(1, 0)],
        out_shape=[jax.ShapeDtypeStruct((2, t, D_FF), BF16)] + [jax.ShapeDtypeStruct((taps, D_FF), F32)] * 2
        + [jax.ShapeDtypeStruct((1, D_FF), F32)] * 2,
        scratch_shapes=[pltpu.VMEM((tm + 2 * HALO, wd), F32)] * 2 + [pltpu.VMEM((tm + HALO, wd), F32)] * 2,
        compiler_params=_cparams(("parallel", "arbitrary")))(u, u, u, u, u, u, da, da, w, w, b, b)


def _softplus(x):
    e = jnp.exp(-jnp.abs(x))
    return jnp.maximum(x, 0.0) + jnp.where(e < 1e-4, e - 0.5 * e * e, jnp.log(1.0 + e))


def _tri(lower):
    r = lax.broadcasted_iota(jnp.int32, (CHUNK, CHUNK), 0)
    c = lax.broadcasted_iota(jnp.int32, (CHUNK, CHUNK), 1)
    return (r >= c) if lower else (r <= c)


def _cum(mat_bool, x):
    return jnp.dot(mat_bool.astype(F32), x, precision=lax.Precision.HIGHEST, preferred_element_type=F32)


def _pair_sel(lane_lo, tile, h0):
    return jnp.where(lane_lo, tile[:, h0:h0 + 1], tile[:, h0 + 1:h0 + 2])


def _pair_sel_mxu(lane_lo, tile, h0):
    rows = lax.broadcasted_iota(jnp.int32, (LANE, LANE), 0)
    sel = (rows == jnp.where(lane_lo, h0, h0 + 1)).astype(BF16)
    return _dot_split(tile, sel)


def _ssd_fwd(xbc_act, proj, dt_bias_p, a_log_p, dskip_t, shards):
    t = xbc_act.shape[0]
    nch = t // CHUNK
    ns = len(shards)

    def kern(xa_ref, dtr_ref, bias_ref, alog_ref, dsk_ref, *rest):
        y_ref, dt_ref, hs_ref = rest[ns:ns + 3]
        hst = rest[2 * ns + 3]
        if ns:
            start, forward, finish = _gather_steps(shards, rest[:ns], rest[ns + 3:2 * ns + 3], rest[2 * ns + 4:])
            pl.when(pl.program_id(0) == 0)(start)
            pl.when(pl.program_id(0) == (3 * nch) // 4)(forward)
            pl.when(pl.program_id(0) == nch - 1)(finish)

        @pl.when(pl.program_id(0) == 0)
        def _():
            hst[...] = jnp.zeros(hst.shape, F32)
        dt = _softplus(dtr_ref[...] + bias_ref[...])
        dt_ref[...] = dt
        acum = _cum(_tri(True), dt * (-jnp.exp(alog_ref[...])))
        acum_t = acum.T
        ea = jnp.exp(acum)
        a_last = acum[CHUNK - 1:CHUNK, :]
        dend = jnp.exp(a_last - acum)
        ea_last = jnp.exp(a_last)
        causal = _tri(True)
        lane_lo = lax.broadcasted_iota(jnp.int32, (CHUNK, LANE), 1) < HEAD_DIM
        row_lo = lax.broadcasted_iota(jnp.int32, (CHUNK, LANE), 0) < HEAD_DIM
        for g in range(2):
            bg = xa_ref[:, SSM_INNER + g * SSM_STATE:SSM_INNER + (g + 1) * SSM_STATE].astype(BF16)
            cg = xa_ref[:, SSM_INNER + BC_DIM + g * SSM_STATE:SSM_INNER + BC_DIM + (g + 1) * SSM_STATE].astype(BF16)
            cb = _dot_nt(cg, bg)
            for j in range(4 * g, 4 * g + 4):
                h0 = 2 * j
                cols = slice(j * LANE, (j + 1) * LANE)
                xp = xa_ref[:, cols]
                xdt = xp * _pair_sel(lane_lo, dt, h0)
                ydiag = None
                for hh, sel in ((h0, lane_lo), (h0 + 1, ~lane_lo)):
                    seg = acum[:, hh:hh + 1] - acum_t[hh:hh + 1, :]
                    mm = (cb * jnp.where(causal, jnp.exp(jnp.minimum(seg, 0.0)), 0.0)).astype(BF16)
                    d = _dot(mm, jnp.where(sel, xdt, 0.0).astype(BF16))
                    ydiag = d if ydiag is None else ydiag + d
                hp = hst[cols, :]
                hs_ref[cols, :] = hp
                yoff = _dot_nt(cg, hp.astype(BF16)) * _pair_sel(lane_lo, ea, h0)
                y_ref[:, cols] = ydiag + yoff + dsk_ref[:, cols] * xp
                xw = (xdt * _pair_sel(lane_lo, dend, h0)).astype(BF16)
                rowf = jnp.where(row_lo, ea_last[:, h0:h0 + 1], ea_last[:, h0 + 1:h0 + 2])
                hst[cols, :] = hp * rowf + _dot_tn(xw, bg)

    return pl.pallas_call(
        kern, name="ssd_fwd", grid=(nch,),
        in_specs=[pl.BlockSpec((CHUNK, XBC_DIM), lambda c: (c, 0)), pl.BlockSpec((CHUNK, LANE), lambda c: (c, OFF_DT // LANE)),
                  pl.BlockSpec((1, LANE), lambda c: (0, 0)), pl.BlockSpec((1, LANE), lambda c: (0, 0)),
                  pl.BlockSpec((1, SSM_INNER), lambda c: (0, 0))] + [ANY_SPEC] * ns,
        out_specs=[pl.BlockSpec((CHUNK, SSM_INNER), lambda c: (c, 0)), pl.BlockSpec((CHUNK, LANE), lambda c: (c, 0)),
                   pl.BlockSpec((None, SSM_INNER, SSM_STATE), lambda c: (c, 0, 0))] + [ANY_SPEC] * ns,
        out_shape=[jax.ShapeDtypeStruct((t, SSM_INNER), F32), jax.ShapeDtypeStruct((t, LANE), F32),
                   jax.ShapeDtypeStruct((nch, SSM_INNER, SSM_STATE), F32)] + _gather_out_shapes(shards),
        scratch_shapes=[pltpu.VMEM((SSM_INNER, SSM_STATE), F32)] + (_gather_sems(ns) if ns else []),
        compiler_params=_cparams(("arbitrary",)))(xbc_act, proj, dt_bias_p, a_log_p, dskip_t, *shards)


def _ssd_bwd(xbc_act, proj, dt_sp, hstates, dy, dt_bias_p, a_log_p, dskip_t, swaps):
    t = xbc_act.shape[0]
    nch = t // CHUNK
    ns = len(swaps)

    pair = jnp.arange(SSM_HEADS // 2)[:, None, None]
    psel = (jnp.arange(LANE)[None, None, :] == 2 * pair + (jnp.arange(LANE) // HEAD_DIM)[None, :, None]).astype(BF16)

    def kern(xa_ref, dtr_ref, dt_ref, hs_ref, dy_ref, bias_ref, alog_ref, dsk_ref, psel_ref, *rest):
        dact_ref, ddtr_ref, da_ref, dbias_ref, ddsk_ref = rest[ns:ns + 5]
        dh = rest[2 * ns + 5]
        if ns:
            start, finish = _swap_steps(swaps, rest[:ns], rest[ns + 5:2 * ns + 5], rest[2 * ns + 6:])
            pl.when(pl.program_id(0) == 0)(start)
            pl.when(pl.program_id(0) == nch - 1)(finish)

        @pl.when(pl.program_id(0) == 0)
        def _():
            dh[...] = jnp.zeros(dh.shape, F32)
            for r in (da_ref, dbias_ref, ddsk_ref):
                r[...] = jnp.zeros(r.shape, F32)
        dt = dt_ref[...]
        a_neg = -jnp.exp(alog_ref[...])
        acum = _cum(_tri(True), dt * a_neg)
        acum_t = acum.T
        ea = jnp.exp(acum)
        a_last = acum[CHUNK - 1:CHUNK, :]
        dend = jnp.exp(a_last - acum)
        ea_last = jnp.exp(a_last)
        causal = _tri(True)
        lane = lax.broadcasted_iota(jnp.int32, (CHUNK, LANE), 1)
        rowi = lax.broadcasted_iota(jnp.int32, (CHUNK, LANE), 0)
        lane_lo, row_lo, last_row = lane < HEAD_DIM, rowi < HEAD_DIM, rowi == CHUNK - 1
        d_dt = jnp.zeros((CHUNK, LANE), F32)
        d_acum = jnp.zeros((CHUNK, LANE), F32)
        for g in range(2):
            bcols = slice(SSM_INNER + g * SSM_STATE, SSM_INNER + (g + 1) * SSM_STATE)
            ccols = slice(SSM_INNER + BC_DIM + g * SSM_STATE, SSM_INNER + BC_DIM + (g + 1) * SSM_STATE)
            bg, cg = xa_ref[:, bcols].astype(BF16), xa_ref[:, ccols].astype(BF16)
            cb = _dot_nt(cg, bg)
            dg_sum = jnp.zeros((CHUNK, CHUNK), F32)
            dcg = jnp.zeros((CHUNK, SSM_STATE), F32)
            dbg = jnp.zeros((CHUNK, SSM_STATE), F32)
            for j in range(4 * g, 4 * g + 4):
                h0 = 2 * j
                cols = slice(j * LANE, (j + 1) * LANE)
                xp, dyp = xa_ref[:, cols], dy_ref[:, cols]
                dtsel = _pair_sel_mxu(lane_lo, dt, h0)
                xdt = xp * dtsel
                xdt_b = xdt.astype(BF16)
                hp, dhp = hs_ref[cols, :], dh[cols, :]
                hp_b, dhp_b = hp.astype(BF16), dhp.astype(BF16)
                easel, dendsel = _pair_sel_mxu(lane_lo, ea, h0), _pair_sel_mxu(lane_lo, dend, h0)
                dx, ydiag = None, None
                for hh, sel in ((h0, lane_lo), (h0 + 1, ~lane_lo)):
                    dyh = jnp.where(sel, dyp, 0.0).astype(BF16)
                    seg = acum[:, hh:hh + 1] - acum_t[hh:hh + 1, :]
                    dec = jnp.where(causal, jnp.exp(jnp.minimum(seg, 0.0)), 0.0)
                    mm_b = (cb * dec).astype(BF16)
                    dg_sum = dg_sum + dec * _dot_nt(dyh, xdt_b)
                    d = _dot_tn(mm_b, dyh)
                    y = _dot(mm_b, jnp.where(sel, xdt, 0.0).astype(BF16))
                    dx = d if dx is None else dx + d
                    ydiag = y if ydiag is None else ydiag + y
                g2 = _dot_nt(bg, dhp_b)
                tprod = xdt * g2 * dendsel
                yoff = _dot_nt(cg, hp_b) * easel
                yc = dyp.astype(BF16).astype(F32) * ydiag + dyp * yoff - (xdt_b.astype(F32) * dx + tprod)
                dx = dx + g2 * dendsel
                psel = psel_ref[j]
                t_lo = jnp.sum(jnp.where(lane_lo, tprod, 0.0), keepdims=True).reshape(1, 1)
                t_hi = jnp.sum(tprod, keepdims=True).reshape(1, 1) - t_lo
                hh_prod = dhp * hp
                s_lo = jnp.sum(jnp.where(row_lo, hh_prod, 0.0), keepdims=True).reshape(1, 1)
                s_hi = jnp.sum(hh_prod, keepdims=True).reshape(1, 1) - s_lo
                end_lo = ea_last[:, h0:h0 + 1] * s_lo + t_lo
                end_hi = ea_last[:, h0 + 1:h0 + 2] * s_hi + t_hi
                ends = jnp.where(lane == h0, end_lo, jnp.where(lane == h0 + 1, end_hi, 0.0))
                d_acum = d_acum + _dot_split(yc, psel) + jnp.where(last_row, ends, 0.0)
                dye = (dyp * easel).astype(BF16)
                dcg = dcg + _dot(dye, hp_b)
                dbg = dbg + _dot((xdt * dendsel).astype(BF16), dhp_b)
                rowf = jnp.where(row_lo, ea_last[:, h0:h0 + 1], ea_last[:, h0 + 1:h0 + 2])
                dh[cols, :] = dhp * rowf + _dot_tn(dye, cg)
                dact_ref[:, cols] = dx * dtsel + dsk_ref[:, cols] * dyp
                d_dt = d_dt + _dot_split(dx * xp, psel)
                ddsk_ref[:, cols] += jnp.sum(dyp * xp, axis=0, keepdims=True)
            dg_b = dg_sum.astype(BF16)
            dact_ref[:, ccols] = dcg + _dot(dg_b, bg)
            dact_ref[:, bcols] = dbg + _dot_tn(dg_b, cg)
        d_adt = _cum(_tri(False), d_acum)
        d_dt = d_dt + d_adt * a_neg
        da_ref[...] += jnp.sum(d_adt * dt, axis=0, keepdims=True)
        d_raw = jnp.where(lane < SSM_HEADS, d_dt * _sigmoid(dtr_ref[...] + bias_ref[...]), 0.0)
        ddtr_ref[...] = d_raw.astype(BF16)
        dbias_ref[...] += jnp.sum(d_raw, axis=0, keepdims=True)

    rev = lambda c: (nch - 1 - c, 0)
    return pl.pallas_call(
        kern, name="ssd_bwd", grid=(nch,),
        in_specs=[pl.BlockSpec((CHUNK, XBC_DIM), rev), pl.BlockSpec((CHUNK, LANE), lambda c: (nch - 1 - c, OFF_DT // LANE)),
                  pl.BlockSpec((CHUNK, LANE), rev), pl.BlockSpec((None, SSM_INNER, SSM_STATE), lambda c: (nch - 1 - c, 0, 0)),
                  pl.BlockSpec((CHUNK, SSM_INNER), rev),
                  pl.BlockSpec((1, LANE), lambda c: (0, 0)), pl.BlockSpec((1, LANE), lambda c: (0, 0)),
                  pl.BlockSpec((1, SSM_INNER), lambda c: (0, 0)), pl.BlockSpec(psel.shape, lambda c: (0, 0, 0))] + [ANY_SPEC] * ns,
        out_specs=[pl.BlockSpec((CHUNK, XBC_DIM), rev), pl.BlockSpec((CHUNK, LANE), rev),
                   pl.BlockSpec((1, LANE), lambda c: (0, 0)), pl.BlockSpec((1, LANE), lambda c: (0, 0)),
                   pl.BlockSpec((1, SSM_INNER), lambda c: (0, 0))] + [ANY_SPEC] * ns,
        out_shape=[jax.ShapeDtypeStruct((t, XBC_DIM), F32), jax.ShapeDtypeStruct((t, LANE), BF16),
                   jax.ShapeDtypeStruct((1, LANE), F32), jax.ShapeDtypeStruct((1, LANE), F32),
                   jax.ShapeDtypeStruct((1, SSM_INNER), F32)] + _swap_out_shapes(swaps),
        scratch_shapes=[pltpu.VMEM((SSM_INNER, SSM_STATE), F32)] + (_swap_sems(ns) if ns else []),
        compiler_params=_cparams(("arbitrary",)))(xbc_act, proj, dt_sp, hstates, dy, dt_bias_p, a_log_p, dskip_t, psel, *swaps)


def _pad_lanes(v, width=LANE):
    return jnp.pad(v, ((0, 0), (0, width - v.shape[1])))


def _local_step(x, p, tgt, wts, late_shards=(), join_late=None, reduce_early=None, reduce_late=None):
    g_attn, g_ssm, g_ffn, g_ple = wts["attn_norm_g"], wts["ssm_norm_g"], wts["ffn_norm_g"], wts["ple_norm_g"]
    w_in_p = wts["w_in_p"]
    gq_t = jnp.tile(wts["q_norm_g"], (1, ATTN_DIM // HEAD_DIM))
    gk_t = jnp.tile(wts["k_norm_g"], (1, KV_DIM // HEAD_DIM))
    dt_bias_p, a_log_p = _pad_lanes(wts["dt_bias"]), _pad_lanes(wts["a_log"])
    dskip_t = jnp.repeat(wts["d_skip"], HEAD_DIM, axis=1)

    h1, proj, q_hm, kv_hm = _in_proj(x, g_attn, w_in_p, gq_t, gk_t)
    pats = _attn_fwd_all(q_hm, kv_hm)
    xbc_act =_ssm_conv_fwd(proj, wts["ssm_conv_w"], wts["ssm_conv_b"])
    y_ssd, dt_sp, hstates, *gathered = _ssd_fwd(xbc_act, proj, dt_bias_p, a_log_p, dskip_t, list(late_shards))
    if join_late is not None:
        wts = {**wts, **join_late(gathered)}
    w_out_s, w_out_a = wts["w_out_ssm"], wts["w_out_attn"]
    w_up, w_down, w_gate, w_proj = wts["w_up"], wts["w_down"], wts["w_ple_gate"], wts["w_ple_proj"]
    ssm_out, attn_out, lse, x1 = _mix_out_proj(y_ssd, proj, g_ssm, w_out_s, [o for o, _ in pats], [l for _, l in pats], w_out_a, x)
    h2, u = _norm_mm("ffn_up", x1, g_ffn, w_up, tm=1024, tn=1408, halves=True)
    a, x2 = _ffn_act_down(u, wts["ffn_conv_w"], wts["ffn_conv_b"], w_down, x1)
    pb = p.astype(BF16)
    h3, dy, dgl, dpp, sq = _ple_head(x2, g_ple, w_gate, pb, w_proj, tgt)

    grads = {}
    grads["w_ple_proj"] = _mm_tn("g_ple_proj", pb, dpp, tn=PLE_DIM, chip_cols=True)
    grads["w_ple_gate"] = _mm_tn("g_ple_gate", h3, dgl)
    dx2, dx2b, grads["ple_norm_g"] = _mm_nt_rms_bwd("d_h3", dgl, w_gate, x2, g_ple, dy)
    da = _mm_nt("d_ffn_act", [(dx2b, w_down, 0)], F32, tm=1024, tn=1408)
    grads["w_down"] = _mm_tn("g_ffn_down", a, dx2b, tm=1408)
    du, gwg, gwv, gbg, gbv = _ffn_act_bwd(u, wts["ffn_conv_w"], wts["ffn_conv_b"], da)
    grads["ffn_conv_w"] = jnp.concatenate([gwg, gwv], axis=1)
    grads["ffn_conv_b"] = jnp.concatenate([gbg, gbv], axis=1)
    grads["w_up"] = _mm_tn("g_ffn_up", h2, du, tn=1408, chip_cols=True)
    dh2 = _mm_nt("d_h2", [(du, w_up, 0, 0), (du, w_up, 1, 1)], F32, tm=1024, tn=512)
    dx1, dx1b, grads["ffn_norm_g"] = _rms_bwd("rms_ffn_bwd", dh2, x1, g_ffn, dx2)
    dy_ssd, dz, grads["ssm_norm_g"], do_hm, dsum = _d_mix(dx1b, jnp.concatenate([w_out_s, w_out_a], axis=0), y_ssd, proj, g_ssm,
                                                            attn_out)
    grads["w_out"] = jnp.concatenate([_mm_tn("g_out_attn", attn_out, dx1b), _mm_tn("g_out_ssm", ssm_out, dx1b)], axis=0)
    early_major = reduce_early[0](grads) if reduce_early is not None else []
    dact, ddtr, d_a, d_bias, d_dsk, *early_got = _ssd_bwd(xbc_act, proj, dt_sp, hstates, dy_ssd, dt_bias_p, a_log_p, dskip_t,
                                                           early_major)
    grads["dt_bias"] = d_bias[:, :SSM_HEADS]
    grads["a_log"] = d_a[:, :SSM_HEADS] * (-jnp.exp(wts["a_log"]))
    grads["d_skip"] = jnp.sum(d_dsk.reshape(SSM_HEADS, HEAD_DIM), axis=1)[None, :]
    chip_sums = reduce_early[1](early_major, early_got) if reduce_early is not None else []
    dxbc, grads["ssm_conv_w"], grads["ssm_conv_b"], *scattered = _ssm_conv_bwd(proj, wts["ssm_conv_w"], wts["ssm_conv_b"], dact,
                                                                                chip_sums)
    dqs = _attn_dq_all(q_hm, kv_hm, do_hm, lse, dsum)
    dkvs = _attn_dkv_all(q_hm, kv_hm, do_hm, lse, dsum)
    dq, dk, dv, dgq, dgk = _qknorm_bwd2(proj, gq_t, gk_t, dqs, dkvs)
    grads["q_norm_g"] = jnp.sum(dgq.reshape(ATTN_DIM // HEAD_DIM, HEAD_DIM), axis=0)[None, :]
    grads["k_norm_g"] = jnp.sum(dgk.reshape(KV_DIM // HEAD_DIM, HEAD_DIM), axis=0)[None, :]
    dproj = jnp.concatenate([dxbc, dq, dz, dk, dv, ddtr], axis=1)
    grads["w_in_p"] = _mm_tn("g_in_proj", h1, dproj, tm=512)
    late_sums = reduce_late(grads) if reduce_late is not None else []
    grad_x, _, grads["attn_norm_g"], *late_scattered = _mm_nt_rms_bwd("d_h1", dproj, w_in_p, x, g_attn, dx1, late_sums)
    return sq, grad_x, grads, (chip_sums, scattered), (late_sums, late_scattered)


MESH_IDS = pl.DeviceIdType.MESH
N_CHIPS = 4
ANY_SPEC = pl.BlockSpec(memory_space=pl.ANY)
SMALL_ROWS = 96
ALL_SMALL_ROWS = 272


def _place():
    x, y, c = lax.axis_index("x"), lax.axis_index("y"), lax.axis_index("c")
    return x, y, c, [(1 - x, y), (x, 1 - y), (1 - x, 1 - y)]


def _gather_over_chips(arrs):
    n = len(arrs)

    def body(*refs):
        steps = _gather_steps(arrs, refs[:n], refs[n:2 * n], refs[2 * n:2 * n + 4])
        for step in steps:
            step()

    return pl.pallas_call(
        body, name="gather_weights", in_specs=[ANY_SPEC] * n, out_specs=[ANY_SPEC] * n,
        out_shape=_gather_out_shapes(arrs), scratch_shapes=_gather_sems(n))(*arrs)


def _gather_out_shapes(arrs):
    return [jax.ShapeDtypeStruct((N_CHIPS,) + a.shape, a.dtype) for a in arrs]


def _gather_sems(n):
    return [pltpu.SemaphoreType.DMA((3 * n,))] * 4


def _gather_steps(arrs, ins, outs, sems):
    n = len(arrs)
    split = [a.shape[0] % 64 == 0 for a in arrs]
    ici_send, ici_recv, d2d_send, d2d_recv = sems

    def place():
        x, y, c, chips = _place()
        return x, y, c, chips, 2 * x + y

    def part(ref, a, core):
        if not split[a]:
            return ref
        half = arrs[a].shape[0] // 2
        return ref.at[pl.ds(core * half, half)]

    def ici(a, k, slot, where):
        x, y, c, chips, _ = where
        px, py = chips[k]
        return pltpu.make_async_remote_copy(
            src_ref=part(ins[a], a, c), dst_ref=part(outs[a].at[slot], a, c), send_sem=ici_send.at[3 * a + k],
            recv_sem=ici_recv.at[3 * a + k], device_id=(px, py, c), device_id_type=MESH_IDS)

    def d2d(a, k, core, where):
        x, y, c, chips, _ = where
        px, py = chips[k]
        piece = part(outs[a].at[2 * px + py], a, core)
        return pltpu.make_async_remote_copy(src_ref=piece, dst_ref=piece, send_sem=d2d_send.at[3 * a + k],
                                            recv_sem=d2d_recv.at[3 * a + k], device_id=(x, y, 1 - c), device_id_type=MESH_IDS)

    def start():
        w = place()
        for a in range(n):
            for k in range(3):
                ici(a, k, w[4], w).start()

    def forward():
        w = place()
        for a in range(n):
            for k, (px, py) in enumerate(w[3]):
                ici(a, k, 2 * px + py, w).wait_recv()
                if split[a]:
                    d2d(a, k, w[2], w).start()

    def finish():
        w = place()
        for a in range(n):
            for k in range(3):
                if split[a]:
                    d2d(a, k, 1 - w[2], w).wait_recv()
                    d2d(a, k, w[2], w).wait_send()
                ici(a, k, w[4], w).wait_send()

    return start, forward, finish


def _row_tile(rows, cap=256):
    return max(d for d in range(8, cap + 1, 8) if rows % d == 0)


def _swap_halves(name, gs):
    n = len(gs)

    def body(*refs):
        for step in _swap_steps(gs, refs[:n], refs[n:2 * n], refs[2 * n:2 * n + 2]):
            step()

    return pl.pallas_call(
        body, name=name, in_specs=[ANY_SPEC] * n, out_specs=[ANY_SPEC] * n, out_shape=_swap_out_shapes(gs),
        scratch_shapes=_swap_sems(n))(*gs)


def _swap_out_shapes(gs):
    return [jax.ShapeDtypeStruct((N_CHIPS, g.shape[1] // 2, g.shape[2]), g.dtype) for g in gs]


def _swap_sems(n):
    return [pltpu.SemaphoreType.DMA((N_CHIPS * n,))] * 2


def _swap_steps(gs, ins, outs, sems):
    send, recv = sems

    def copies():
        x, y, c, _ = _place()
        cps = []
        for a in range(len(gs)):
            half = gs[a].shape[1] // 2
            for q in range(N_CHIPS):
                cps.append(pltpu.make_async_remote_copy(
                    src_ref=ins[a].at[q, pl.ds((1 - c) * half, half)], dst_ref=outs[a].at[q], send_sem=send.at[N_CHIPS * a + q],
                    recv_sem=recv.at[N_CHIPS * a + q], device_id=(x, y, 1 - c), device_id_type=MESH_IDS))
        return cps

    def start():
        for cp in copies():
            cp.start()

    def finish():
        for cp in copies():
            cp.wait()

    return start, finish


def _add_halves(name, g, got, c_idx):
    rows, cols = got.shape[1:]
    tm = _row_tile(rows)
    per = rows // tm

    def kern(c_ref, g_ref, r_ref, o_ref):
        o_ref[...] = (g_ref[...] + r_ref[...]).astype(BF16)

    return pl.pallas_call(
        kern, name=name,
        grid_spec=pltpu.PrefetchScalarGridSpec(
            num_scalar_prefetch=1, grid=(N_CHIPS, per),
            in_specs=[pl.BlockSpec((None, tm, cols), lambda q, i, c_ref: (q, c_ref[0] * per + i, 0)),
                      pl.BlockSpec((None, tm, cols), lambda q, i, c_ref: (q, i, 0))],
            out_specs=pl.BlockSpec((None, tm, cols), lambda q, i, c_ref: (q, i, 0))),
        out_shape=jax.ShapeDtypeStruct((N_CHIPS, rows, cols), BF16),
        compiler_params=_cparams(("parallel", "parallel")))(c_idx, g, got)


def _scatter_sems(n):
    return [pltpu.SemaphoreType.DMA((3 * n,))] * 2


def _scatter_steps(n, ins, outs, sems):
    send, recv = sems

    def copy(a, k, slot, where):
        x, y, c, chips = where
        px, py = chips[k]
        return pltpu.make_async_remote_copy(src_ref=ins[a].at[2 * px + py], dst_ref=outs[a].at[slot], send_sem=send.at[3 * a + k],
                                            recv_sem=recv.at[3 * a + k], device_id=(px, py, c), device_id_type=MESH_IDS)

    def start():
        w = _place()
        for a in range(n):
            for k in range(3):
                copy(a, k, 2 * w[0] + w[1], w).start()

    def finish():
        w = _place()
        for a in range(n):
            for k, (px, py) in enumerate(w[3]):
                copy(a, k, 2 * px + py, w).wait()

    return start, finish


def _sum_chips(name, own, parts, idx):
    rows, cols = parts.shape[1:]
    tm = _row_tile(rows)
    per = rows // tm

    def kern(o_idx, a_ref, b_ref, c_ref, d_ref, o_ref):
        o_ref[...] = ((a_ref[...].astype(F32) + b_ref[...].astype(F32)) + c_ref[...].astype(F32)) + d_ref[...].astype(F32)

    def spec(k):
        return pl.BlockSpec((None, tm, cols), functools.partial(lambda i, o_idx, k: (o_idx[k], i, 0), k=k))

    return pl.pallas_call(
        kern, name=name,
        grid_spec=pltpu.PrefetchScalarGridSpec(
            num_scalar_prefetch=1, grid=(per,), in_specs=[spec(0), spec(1), spec(2), spec(3)],
            out_specs=pl.BlockSpec((None, tm, cols), lambda i, o_idx: (0, o_idx[4] * per + i, 0))),
        out_shape=jax.ShapeDtypeStruct((1, 2 * rows, cols), F32), compiler_params=_cparams(("parallel",)))(idx, own, parts, parts, parts)


def _share_with_sibling(gs):
    n = len(gs)

    def body(*refs):
        ins, send, recv = refs[:n], refs[2 * n], refs[2 * n + 1]
        x, y, c, _ = _place()
        cps = []
        for a in range(n):
            half = gs[a].shape[1] // 2
            mine = pl.ds(c * half, half)
            cps.append(pltpu.make_async_remote_copy(src_ref=ins[a].at[0, mine], dst_ref=refs[n + a].at[0, mine], send_sem=send.at[a],
                                                    recv_sem=recv.at[a], device_id=(x, y, 1 - c), device_id_type=MESH_IDS))
        for cp in cps:
            cp.start()
        for cp in cps:
            cp.wait()

    return pl.pallas_call(
        body, name="grad_share_sibling", in_specs=[ANY_SPEC] * n, out_specs=[ANY_SPEC] * n,
        out_shape=[jax.ShapeDtypeStruct(g.shape, g.dtype) for g in gs], input_output_aliases={a: a for a in range(n)},
        scratch_shapes=[pltpu.SemaphoreType.DMA((n,))] * 2)(*gs)


def _allreduce_small(v):
    def body(v_ref, o_ref, land, send, recv):
        x, y, c, _ = _place()
        me = 4 * x + 2 * y + c
        land[me] = v_ref[...]
        cps = []
        for rel in range(1, 8):
            bx, by, bc = (rel >> 2) & 1, (rel >> 1) & 1, rel & 1
            peer = (1 - x if bx else x, 1 - y if by else y, 1 - c if bc else c)
            cps.append(pltpu.make_async_remote_copy(src_ref=v_ref, dst_ref=land.at[me], send_sem=send.at[rel - 1],
                                                    recv_sem=recv.at[rel - 1], device_id=peer, device_id_type=MESH_IDS))
        for cp in cps:
            cp.start()
        for cp in cps:
            cp.wait()
        acc = land[0]
        for d in range(1, 8):
            acc = acc + land[d]
        o_ref[...] = acc

    vm = pl.BlockSpec(memory_space=pltpu.VMEM)
    return pl.pallas_call(
        body, name="allreduce_small", in_specs=[vm], out_specs=vm, out_shape=jax.ShapeDtypeStruct(v.shape, F32),
        scratch_shapes=[pltpu.VMEM((8,) + v.shape, F32), pltpu.SemaphoreType.DMA((7,)), pltpu.SemaphoreType.DMA((7,))])(v)


def _adamw(name, w, g, m, v):
    _, rows, cols = w.shape
    tm = rows if rows * cols <= 128 * 1024 else _row_tile(rows, max(256, 2048 * LANE // cols))
    c1 = 1.0 / (1.0 - ADAM_B1 ** ADAM_STEP)
    c2 = 1.0 / (1.0 - ADAM_B2 ** ADAM_STEP)

    def kern(w_ref, g_ref, m_ref, v_ref, d_ref, mo_ref, vo_ref):
        gv = g_ref[...]
        mn = ADAM_B1 * m_ref[...] + (1.0 - ADAM_B1) * gv
        vn = ADAM_B2 * v_ref[...] + (1.0 - ADAM_B2) * (gv * gv)
        d_ref[...] = -ADAM_LR * ((mn * c1) / (jnp.sqrt(vn * c2) + ADAM_EPS) + ADAM_WD * w_ref[...])
        mo_ref[...] = mn
        vo_ref[...] = vn

    spec = pl.BlockSpec((None, tm, cols), lambda i: (0, i, 0))
    return pl.pallas_call(
        kern, name=name, grid=(rows // tm,), in_specs=[spec] * 4, out_specs=[spec] * 3,
        out_shape=[jax.ShapeDtypeStruct(w.shape, F32)] * 3, compiler_params=_cparams(("parallel",)))(w, g, m, v)


SHARDED = (("w_in", 1), ("w_out", 0), ("w_up", 1), ("w_down", 0), ("w_ple_gate", 0), ("w_ple_proj", 1),
           ("ssm_conv_w", 1), ("ffn_conv_w", 1))
MATRICES = ("w_in", "w_out", "w_up", "w_down", "w_ple_gate", "w_ple_proj")
EARLY_REDUCED = MATRICES[1:]
REPLICATED = ("attn_norm_g", "q_norm_g", "k_norm_g", "ssm_conv_b", "dt_bias", "a_log", "d_skip", "ssm_norm_g",
              "ffn_norm_g", "ffn_conv_b", "ple_norm_g")
WEIGHT_ORDER = ("attn_norm_g", "w_in", "q_norm_g", "k_norm_g", "ssm_conv_w", "ssm_conv_b", "dt_bias", "a_log", "d_skip",
                "ssm_norm_g", "w_out", "ffn_norm_g", "w_up", "ffn_conv_w", "ffn_conv_b", "w_down", "ple_norm_g",
                "w_ple_gate", "w_ple_proj")


def _join_chips(g, axis):
    if axis == 0:
        return g.reshape(g.shape[0] * g.shape[1], g.shape[2])
    return jnp.transpose(g, (1, 0, 2)).reshape(g.shape[1], g.shape[0] * g.shape[2])


def _split_chips(g, axis):
    if axis == 0:
        return g.reshape(N_CHIPS, g.shape[0] // N_CHIPS, g.shape[1])
    r, c = g.shape
    return jnp.transpose(g.reshape(r, N_CHIPS, c // N_CHIPS), (1, 0, 2))


def _pack_small(vals, rows=SMALL_ROWS):
    flat = jnp.concatenate([v.reshape(-1) for v in vals])
    return jnp.pad(flat, (0, rows * LANE - flat.shape[0])).reshape(rows, LANE)


def _unpack_small(packed, like):
    flat, out, off = packed.reshape(-1), [], 0
    for v in like:
        out.append(flat[off:off + v.size].reshape(v.shape))
        off += v.size
    return out


def kernel(x, p, attn_norm_g, w_in, q_norm_g, k_norm_g, ssm_conv_w, ssm_conv_b, dt_bias, a_log, d_skip, ssm_norm_g, w_out, ffn_norm_g, w_up, ffn_conv_w, ffn_conv_b, w_down, ple_norm_g, w_ple_gate, w_ple_proj, loss_target, m_attn_norm_g, m_w_in, m_q_norm_g, m_k_norm_g, m_ssm_conv_w, m_ssm_conv_b, m_dt_bias, m_a_log, m_d_skip, m_ssm_norm_g, m_w_out, m_ffn_norm_g, m_w_up, m_ffn_conv_w, m_ffn_conv_b, m_w_down, m_ple_norm_g, m_w_ple_gate, m_w_ple_proj, v_attn_norm_g, v_w_in, v_q_norm_g, v_k_norm_g, v_ssm_conv_w, v_ssm_conv_b, v_dt_bias, v_a_log, v_d_skip, v_ssm_norm_g, v_w_out, v_ffn_norm_g, v_w_up, v_ffn_conv_w, v_ffn_conv_b, v_w_down, v_ple_norm_g, v_w_ple_gate, v_w_ple_proj):
    given = dict(locals())
    w2 = {n: given[n].reshape(given[n].shape[-2:]) if given[n].ndim == 3 else given[n] for n in WEIGHT_ORDER}

    cx, cy, cc = lax.axis_index("x"), lax.axis_index("y"), lax.axis_index("c")
    chip = 2 * cx + cy
    axis_of = dict(SHARDED)
    shard = lambda n: w2[n].astype(BF16) if n in MATRICES else w2[n]
    join = lambda n, g: _join_chips(lax.dynamic_update_index_in_dim(g, shard(n), chip, 0), axis_of[n])
    first = ("w_in", "ssm_conv_w", "ffn_conv_w")
    full = {n: join(n, g) for n, g in zip(first, _gather_over_chips([shard(n) for n in first]))}
    win = full["w_in"]
    w_in_p = jnp.concatenate([win[:, 2048:3584], win[:, 0:512], win[:, 1024:2048], win[:, 512:768], win[:, 768:1024],
                              win[:, 3584:3600], jnp.zeros((D_MODEL, PROJ_P - IN_PROJ), BF16)], axis=1)
    wts = {n: w2[n] for n in REPLICATED}
    wts.update(w_in_p=w_in_p, ssm_conv_w=full["ssm_conv_w"], ffn_conv_w=full["ffn_conv_w"])

    def join_late(gathered):
        late = {n: join(n, g) for n, g in zip(EARLY_REDUCED, gathered)}
        return dict(w_out_attn=late["w_out"][:ATTN_DIM], w_out_ssm=late["w_out"][ATTN_DIM:], w_up=late["w_up"],
                    w_down=late["w_down"], w_ple_gate=late["w_ple_gate"], w_ple_proj=late["w_ple_proj"])

    core = cc.astype(jnp.int32).reshape(1)
    idx = jnp.stack([chip, 2 * (1 - cx) + cy, 2 * cx + (1 - cy), 2 * (1 - cx) + (1 - cy), cc]).astype(jnp.int32)

    def major_of(names, gd):
        return [gd[n] if gd[n].ndim == 3 else _split_chips(gd[n], axis_of[n]) for n in names]

    def sums_of(names, major, got):
        return [_add_halves("grad_add_halves_" + n, g, r, core) for n, g, r in zip(names, major, got)]

    def w_in_sums(gd):
        gi = gd["w_in_p"]
        gd["w_in"] = jnp.concatenate([gi[:, OFF_Q:OFF_Q + ATTN_DIM], gi[:, OFF_K:OFF_K + KV_DIM], gi[:, OFF_V:OFF_V + KV_DIM],
                                      gi[:, OFF_Z:OFF_Z + SSM_INNER], gi[:, OFF_XBC:OFF_XBC + XBC_DIM], gi[:, OFF_DT:OFF_DT + SSM_HEADS]],
                                     axis=1)
        major = major_of(("w_in",), gd)
        return sums_of(("w_in",), major, _swap_halves("grad_swap_halves_late", major))

    sq, grad_x, grads, early, late = _local_step(
        x[0], p[0, 0], loss_target[0], wts, [shard(n) for n in EARLY_REDUCED], join_late,
        (functools.partial(major_of, EARLY_REDUCED), functools.partial(sums_of, EARLY_REDUCED)), w_in_sums)
    sums = dict(zip(EARLY_REDUCED + ("w_in",), list(zip(*early)) + list(zip(*late))))
    halves = [_sum_chips("grad_sum_chips_" + n, *sums[n], idx) for n in MATRICES]
    g_shard = dict(zip(MATRICES, _share_with_sibling(halves)))

    small_names = REPLICATED + ("ssm_conv_w", "ffn_conv_w")
    small_like = [grads[n] for n in small_names] + [jnp.zeros((1,), F32)]
    small = _allreduce_small(_pack_small([grads[n] for n in small_names] + [jnp.sum(sq).reshape(1)], ALL_SMALL_ROWS))
    small_vals = dict(zip(small_names + ("loss",), _unpack_small(small, small_like)))
    loss = (0.5 / D_MODEL) * small_vals["loss"][0]
    for n in ("ssm_conv_w", "ffn_conv_w"):
        cols = w2[n].shape[1]
        g_shard[n] = lax.dynamic_slice_in_dim(small_vals[n], chip * cols, cols, axis=1)[None]

    delta, new_m, new_v = {}, {}, {}
    for n, _ in SHARDED:
        if n == "w_in":
            r, c = w2[n].shape
            flat = lambda a: jnp.transpose(a.reshape(r, c)).reshape(1, r * c // LANE, LANE)
            back = lambda a: jnp.transpose(a.reshape(c, r)).reshape(1, r, c)
            outs = _adamw("adamw_" + n, flat(given[n]), flat(g_shard[n]), flat(given["m_" + n]), flat(given["v_" + n]))
            delta[n], new_m[n], new_v[n] = [back(o) for o in outs]
            continue
        delta[n], new_m[n], new_v[n] = _adamw("adamw_" + n, given[n], g_shard[n], given["m_" + n], given["v_" + n])
    packed = lambda prefix: _pack_small([given[prefix + n] for n in REPLICATED])[None]
    sm = _adamw("adamw_small", packed(""), _pack_small([small_vals[n] for n in REPLICATED])[None], packed("m_"), packed("v_"))
    for n in REPLICATED:
        g_shard[n] = small_vals[n]
    for dst, packed_out in zip((delta, new_m, new_v), sm):
        for n, val in zip(REPLICATED, _unpack_small(packed_out[0], [w2[n] for n in REPLICATED])):
            dst[n] = val

    def shaped(d):
        return [d[n].reshape(given[n].shape) for n in WEIGHT_ORDER]
    return (loss, grad_x[None], *shaped(g_shard), *shaped(delta), *shaped(new_m), *shaped(new_v))
```
